```python
import jax, jax.numpy as jnp
from jax import lax
import numpy as np

D_MODEL = 1024
BATCH = 8
SEQ = 8192
DEPTH = 4

N_MIXERS = 2
N_LAYERS_A = (DEPTH + N_MIXERS - 1) // N_MIXERS
N_LAYERS_B = DEPTH // N_MIXERS
LRU_WIDTH = D_MODEL
LRU_HEADS = 4
LRU_HEAD_DIM = LRU_WIDTH // LRU_HEADS
CONV_WIDTH = 4
LRU_C = 8.0
MIN_RAD = 0.9
MAX_RAD = 0.999
POOL_WINDOWS = (2, 4, 8, 16)
POOL_GROUPS = len(POOL_WINDOWS)
POOL_GROUP_DIM = D_MODEL // POOL_GROUPS
D_FF = 4 * D_MODEL
N_MOD = 6
EPS = 1e-6

kernel_name = "hybrid_rglru_multiscale_pool_adaln"


def rms_norm(x, g):
    xf = x.astype(jnp.float32)
    y = xf * lax.rsqrt(jnp.mean(xf * xf, axis=-1, keepdims=True) + EPS)
    return (y * g.astype(jnp.float32)).astype(x.dtype)


def modulate(h, shift, scale):
    return h * (1.0 + scale[:, None, :]) + shift[:, None, :]


def causal_depthwise_conv(x, w, b):
    s = x.shape[1]
    xp = jnp.pad(x, ((0, 0), (CONV_WIDTH - 1, 0), (0, 0)))
    y = xp[:, 0:s] * w[0]
    for k in range(1, CONV_WIDTH):
        y = y + xp[:, k:k + s] * w[k]
    return y + b


def _lru_combine(left, right):
    a1, b1 = left
    a2, b2 = right
    return a1 * a2, a2 * b1 + b2


def block_diag_linear(x, w, b):
    bsz, s, _ = x.shape
    xh = x.reshape(bsz, s, LRU_HEADS, LRU_HEAD_DIM)
    y = jnp.einsum("bshi,hij->bshj", xh, w) + b
    return y.reshape(bsz, s, LRU_WIDTH)


def rg_lru(x, w_a, b_a, w_x, b_x, lam):
    gate_r = jax.nn.sigmoid(block_diag_linear(x, w_a, b_a)).astype(jnp.float32)
    gate_i = jax.nn.sigmoid(block_diag_linear(x, w_x, b_x)).astype(jnp.float32)
    log_a = LRU_C * gate_r * jax.nn.log_sigmoid(lam.astype(jnp.float32))
    a = jnp.exp(log_a)
    mult = jnp.sqrt(-jnp.expm1(2.0 * log_a))
    u = mult * (gate_i * x.astype(jnp.float32))
    _, h = lax.associative_scan(_lru_combine, (a, u), axis=1)
    return h.astype(x.dtype)


def recurrent_mixer(h, w_y, b_y, w_in, b_in, conv_w, conv_b, w_a, b_a, w_x, b_x, lam, w_out, b_out):
    gate_branch = jax.nn.gelu(jnp.einsum("bsd,dw->bsw", h, w_y) + b_y)
    xr = jnp.einsum("bsd,dw->bsw", h, w_in) + b_in
    xr = causal_depthwise_conv(xr, conv_w, conv_b)
    xr = rg_lru(xr, w_a, b_a, w_x, b_x, lam)
    return jnp.einsum("bsw,wd->bsd", xr * gate_branch, w_out) + b_out


def pool_mixer(h, w_pool, pool_scale):
    bsz, s, _ = h.shape
    hf = h.astype(jnp.float32)
    counts = jnp.arange(1, s + 1, dtype=jnp.float32)
    outs = []
    for g, win in enumerate(POOL_WINDOWS):
        xg = hf[..., g * POOL_GROUP_DIM:(g + 1) * POOL_GROUP_DIM]
        cs = jnp.cumsum(xg, axis=1)
        cs_lag = jnp.pad(cs, ((0, 0), (win, 0), (0, 0)))[:, :s]
        mean = (cs - cs_lag) / jnp.minimum(counts, float(win))[None, :, None]
        outs.append(mean - xg)
    pooled = jnp.stack(outs, axis=2).astype(h.dtype)
    mixed = jnp.einsum("bsgi,gij->bsgj", pooled, w_pool).reshape(bsz, s, D_MODEL)
    return mixed * pool_scale


def sq_relu_mlp(h, w1, w2):
    u = jax.nn.relu(jnp.einsum("bsd,df->bsf", h, w1))
    return jnp.einsum("bsf,fd->bsd", u * u, w2)


def _fwd_setup_inputs(seed: int = 0) -> dict:
    key = jax.random.key(seed)
    ks = jax.random.split(key, 26)
    f32 = jnp.float32
    nrm = lambda k, shape, s: (jax.random.normal(k, shape, f32) * s)
    d, w, hd, na, nb = D_MODEL, LRU_WIDTH, LRU_HEAD_DIM, N_LAYERS_A, N_LAYERS_B
    rad = jnp.sqrt(jax.random.uniform(ks[15], (na, w), f32, MIN_RAD ** 2, MAX_RAD ** 2))
    return {
        "x": nrm(ks[0], (BATCH, SEQ, d), 1.0),
        "c": nrm(ks[1], (BATCH, d), 1.0),
        "w_mod": nrm(ks[2], (DEPTH, d, N_MOD * d), 0.5 * d ** -0.5),
        "b_mod": nrm(ks[3], (DEPTH, N_MOD * d), 0.02),
        "norm_mix_g": 1.0 + nrm(ks[4], (DEPTH, d), 0.05),
        "norm_ffn_g": 1.0 + nrm(ks[5], (DEPTH, d), 0.05),
        "lru_w_y": nrm(ks[6], (na, d, w), d ** -0.5),
        "lru_b_y": nrm(ks[7], (na, w), 0.02),
        "lru_w_in": nrm(ks[8], (na, d, w), d ** -0.5),
        "lru_b_in": nrm(ks[9], (na, w), 0.02),
        "lru_conv_w": nrm(ks[10], (na, CONV_WIDTH, w), CONV_WIDTH ** -0.5),
        "lru_conv_b": nrm(ks[11], (na, w), 0.02),
        "lru_w_a": nrm(ks[12], (na, LRU_HEADS, hd, hd), hd ** -0.5),
        "lru_b_a": nrm(ks[13], (na, LRU_HEADS, hd), 0.02),
        "lru_w_x": nrm(ks[14], (na, LRU_HEADS, hd, hd), hd ** -0.5),
        "lru_b_x": nrm(ks[16], (na, LRU_HEADS, hd), 0.02),
        "lru_lambda": jnp.log(rad) - jnp.log1p(-rad),
        "lru_w_out": nrm(ks[17], (na, w, d), w ** -0.5),
        "lru_b_out": nrm(ks[18], (na, d), 0.02),
        "pool_w": nrm(ks[19], (nb, POOL_GROUPS, POOL_GROUP_DIM, POOL_GROUP_DIM), POOL_GROUP_DIM ** -0.5),
        "pool_scale": 1.0 + nrm(ks[20], (nb, d), 0.1),
        "ffn_w1": nrm(ks[21], (DEPTH, d, D_FF), d ** -0.5),
        "ffn_w2": nrm(ks[22], (DEPTH, D_FF, d), D_FF ** -0.5),
        "final_norm_g": 1.0 + nrm(ks[23], (d,), 0.05),
    }


def _fwd_reference(x, c, w_mod, b_mod, norm_mix_g, norm_ffn_g, lru_w_y, lru_b_y, lru_w_in, lru_b_in,
              lru_conv_w, lru_conv_b, lru_w_a, lru_b_a, lru_w_x, lru_b_x, lru_lambda, lru_w_out,
              lru_b_out, pool_w, pool_scale, ffn_w1, ffn_w2, final_norm_g):
    cond = jax.nn.silu(c)
    for i in range(DEPTH):
        mod = jnp.einsum("bd,de->be", cond, w_mod[i]) + b_mod[i]
        sh_m, sc_m, gt_m, sh_f, sc_f, gt_f = jnp.split(mod, N_MOD, axis=-1)
        h = modulate(rms_norm(x, norm_mix_g[i]), sh_m, sc_m)
        j = i // N_MIXERS
        if i % N_MIXERS == 0:
            y = recurrent_mixer(h, lru_w_y[j], lru_b_y[j], lru_w_in[j], lru_b_in[j],
                                lru_conv_w[j], lru_conv_b[j], lru_w_a[j], lru_b_a[j],
                                lru_w_x[j], lru_b_x[j], lru_lambda[j], lru_w_out[j], lru_b_out[j])
        else:
            y = pool_mixer(h, pool_w[j], pool_scale[j])
        x = x + gt_m[:, None, :] * y
        h = modulate(rms_norm(x, norm_ffn_g[i]), sh_f, sc_f)
        x = x + gt_f[:, None, :] * sq_relu_mlp(h, ffn_w1[i], ffn_w2[i])
    return rms_norm(x, final_norm_g)


import jax as _jax
import jax.numpy as _jnp

TWIN_FORMAT = 'train_step'
FWD_PARAMS = ['x', 'c', 'w_mod', 'b_mod', 'norm_mix_g', 'norm_ffn_g', 'lru_w_y', 'lru_b_y', 'lru_w_in', 'lru_b_in', 'lru_conv_w', 'lru_conv_b', 'lru_w_a', 'lru_b_a', 'lru_w_x', 'lru_b_x', 'lru_lambda', 'lru_w_out', 'lru_b_out', 'pool_w', 'pool_scale', 'ffn_w1', 'ffn_w2', 'final_norm_g']
TWIN_WEIGHTS = ['w_mod', 'b_mod', 'norm_mix_g', 'norm_ffn_g', 'lru_w_y', 'lru_b_y', 'lru_w_in', 'lru_b_in', 'lru_conv_w', 'lru_conv_b', 'lru_w_a', 'lru_b_a', 'lru_w_x', 'lru_b_x', 'lru_lambda', 'lru_w_out', 'lru_b_out', 'pool_w', 'pool_scale', 'ffn_w1', 'ffn_w2', 'final_norm_g']
TWIN_DIFF_INPUT = 'x'
TWIN_INPUTS = ['x', 'c', 'w_mod', 'b_mod', 'norm_mix_g', 'norm_ffn_g', 'lru_w_y', 'lru_b_y', 'lru_w_in', 'lru_b_in', 'lru_conv_w', 'lru_conv_b', 'lru_w_a', 'lru_b_a', 'lru_w_x', 'lru_b_x', 'lru_lambda', 'lru_w_out', 'lru_b_out', 'pool_w', 'pool_scale', 'ffn_w1', 'ffn_w2', 'final_norm_g', 'loss_target', 'm_w_mod', 'm_b_mod', 'm_norm_mix_g', 'm_norm_ffn_g', 'm_lru_w_y', 'm_lru_b_y', 'm_lru_w_in', 'm_lru_b_in', 'm_lru_conv_w', 'm_lru_conv_b', 'm_lru_w_a', 'm_lru_b_a', 'm_lru_w_x', 'm_lru_b_x', 'm_lru_lambda', 'm_lru_w_out', 'm_lru_b_out', 'm_pool_w', 'm_pool_scale', 'm_ffn_w1', 'm_ffn_w2', 'm_final_norm_g', 'v_w_mod', 'v_b_mod', 'v_norm_mix_g', 'v_norm_ffn_g', 'v_lru_w_y', 'v_lru_b_y', 'v_lru_w_in', 'v_lru_b_in', 'v_lru_conv_w', 'v_lru_conv_b', 'v_lru_w_a', 'v_lru_b_a', 'v_lru_w_x', 'v_lru_b_x', 'v_lru_lambda', 'v_lru_w_out', 'v_lru_b_out', 'v_pool_w', 'v_pool_scale', 'v_ffn_w1', 'v_ffn_w2', 'v_final_norm_g']
TWIN_OUTPUTS = ['loss', 'grad_x', 'grad_w_mod', 'grad_b_mod', 'grad_norm_mix_g', 'grad_norm_ffn_g', 'grad_lru_w_y', 'grad_lru_b_y', 'grad_lru_w_in', 'grad_lru_b_in', 'grad_lru_conv_w', 'grad_lru_conv_b', 'grad_lru_w_a', 'grad_lru_b_a', 'grad_lru_w_x', 'grad_lru_b_x', 'grad_lru_lambda', 'grad_lru_w_out', 'grad_lru_b_out', 'grad_pool_w', 'grad_pool_scale', 'grad_ffn_w1', 'grad_ffn_w2', 'grad_final_norm_g', 'delta_w_mod', 'delta_b_mod', 'delta_norm_mix_g', 'delta_norm_ffn_g', 'delta_lru_w_y', 'delta_lru_b_y', 'delta_lru_w_in', 'delta_lru_b_in', 'delta_lru_conv_w', 'delta_lru_conv_b', 'delta_lru_w_a', 'delta_lru_b_a', 'delta_lru_w_x', 'delta_lru_b_x', 'delta_lru_lambda', 'delta_lru_w_out', 'delta_lru_b_out', 'delta_pool_w', 'delta_pool_scale', 'delta_ffn_w1', 'delta_ffn_w2', 'delta_final_norm_g', 'new_m_w_mod', 'new_m_b_mod', 'new_m_norm_mix_g', 'new_m_norm_ffn_g', 'new_m_lru_w_y', 'new_m_lru_b_y', 'new_m_lru_w_in', 'new_m_lru_b_in', 'new_m_lru_conv_w', 'new_m_lru_conv_b', 'new_m_lru_w_a', 'new_m_lru_b_a', 'new_m_lru_w_x', 'new_m_lru_b_x', 'new_m_lru_lambda', 'new_m_lru_w_out', 'new_m_lru_b_out', 'new_m_pool_w', 'new_m_pool_scale', 'new_m_ffn_w1', 'new_m_ffn_w2', 'new_m_final_norm_g', 'new_v_w_mod', 'new_v_b_mod', 'new_v_norm_mix_g', 'new_v_norm_ffn_g', 'new_v_lru_w_y', 'new_v_lru_b_y', 'new_v_lru_w_in', 'new_v_lru_b_in', 'new_v_lru_conv_w', 'new_v_lru_conv_b', 'new_v_lru_w_a', 'new_v_lru_b_a', 'new_v_lru_w_x', 'new_v_lru_b_x', 'new_v_lru_lambda', 'new_v_lru_w_out', 'new_v_lru_b_out', 'new_v_pool_w', 'new_v_pool_scale', 'new_v_ffn_w1', 'new_v_ffn_w2', 'new_v_final_norm_g']
TWIN_LEAF_KINDS = {'loss': 'loss', 'grad_x': 'grad_x', 'grad_w_mod': 'grad_w', 'grad_b_mod': 'grad_w', 'grad_norm_mix_g': 'grad_w', 'grad_norm_ffn_g': 'grad_w', 'grad_lru_w_y': 'grad_w', 'grad_lru_b_y': 'grad_w', 'grad_lru_w_in': 'grad_w', 'grad_lru_b_in': 'grad_w', 'grad_lru_conv_w': 'grad_w', 'grad_lru_conv_b': 'grad_w', 'grad_lru_w_a': 'grad_w', 'grad_lru_b_a': 'grad_w', 'grad_lru_w_x': 'grad_w', 'grad_lru_b_x': 'grad_w', 'grad_lru_lambda': 'grad_w', 'grad_lru_w_out': 'grad_w', 'grad_lru_b_out': 'grad_w', 'grad_pool_w': 'grad_w', 'grad_pool_scale': 'grad_w', 'grad_ffn_w1': 'grad_w', 'grad_ffn_w2': 'grad_w', 'grad_final_norm_g': 'grad_w', 'delta_w_mod': 'delta_w', 'delta_b_mod': 'delta_w', 'delta_norm_mix_g': 'delta_w', 'delta_norm_ffn_g': 'delta_w', 'delta_lru_w_y': 'delta_w', 'delta_lru_b_y': 'delta_w', 'delta_lru_w_in': 'delta_w', 'delta_lru_b_in': 'delta_w', 'delta_lru_conv_w': 'delta_w', 'delta_lru_conv_b': 'delta_w', 'delta_lru_w_a': 'delta_w', 'delta_lru_b_a': 'delta_w', 'delta_lru_w_x': 'delta_w', 'delta_lru_b_x': 'delta_w', 'delta_lru_lambda': 'delta_w', 'delta_lru_w_out': 'delta_w', 'delta_lru_b_out': 'delta_w', 'delta_pool_w': 'delta_w', 'delta_pool_scale': 'delta_w', 'delta_ffn_w1': 'delta_w', 'delta_ffn_w2': 'delta_w', 'delta_final_norm_g': 'delta_w', 'new_m_w_mod': 'new_m', 'new_m_b_mod': 'new_m', 'new_m_norm_mix_g': 'new_m', 'new_m_norm_ffn_g': 'new_m', 'new_m_lru_w_y': 'new_m', 'new_m_lru_b_y': 'new_m', 'new_m_lru_w_in': 'new_m', 'new_m_lru_b_in': 'new_m', 'new_m_lru_conv_w': 'new_m', 'new_m_lru_conv_b': 'new_m', 'new_m_lru_w_a': 'new_m', 'new_m_lru_b_a': 'new_m', 'new_m_lru_w_x': 'new_m', 'new_m_lru_b_x': 'new_m', 'new_m_lru_lambda': 'new_m', 'new_m_lru_w_out': 'new_m', 'new_m_lru_b_out': 'new_m', 'new_m_pool_w': 'new_m', 'new_m_pool_scale': 'new_m', 'new_m_ffn_w1': 'new_m', 'new_m_ffn_w2': 'new_m', 'new_m_final_norm_g': 'new_m', 'new_v_w_mod': 'new_v', 'new_v_b_mod': 'new_v', 'new_v_norm_mix_g': 'new_v', 'new_v_norm_ffn_g': 'new_v', 'new_v_lru_w_y': 'new_v', 'new_v_lru_b_y': 'new_v', 'new_v_lru_w_in': 'new_v', 'new_v_lru_b_in': 'new_v', 'new_v_lru_conv_w': 'new_v', 'new_v_lru_conv_b': 'new_v', 'new_v_lru_w_a': 'new_v', 'new_v_lru_b_a': 'new_v', 'new_v_lru_w_x': 'new_v', 'new_v_lru_b_x': 'new_v', 'new_v_lru_lambda': 'new_v', 'new_v_lru_w_out': 'new_v', 'new_v_lru_b_out': 'new_v', 'new_v_pool_w': 'new_v', 'new_v_pool_scale': 'new_v', 'new_v_ffn_w1': 'new_v', 'new_v_ffn_w2': 'new_v', 'new_v_final_norm_g': 'new_v'}


def _forward(args):
    return _fwd_reference(*[args[k] for k in FWD_PARAMS])


def _output_shape():
    def fwd():
        inp = _fwd_setup_inputs(0)
        return _fwd_reference(*[inp[k] for k in FWD_PARAMS])
    out = _jax.eval_shape(fwd)
    return out.shape, out.dtype

N_MICROBATCH = 1
ADAM_LR = 0.001
ADAM_B1 = 0.9
ADAM_B2 = 0.999
ADAM_EPS = 1e-08
ADAM_WD = 0.01
ADAM_STEP = 10
PER_EXAMPLE_BATCH_AXIS = {'x': 0, 'c': 0, 'loss_target': 0}
SHARED_INPUTS = []
_WEIGHT_DTYPES = {'w_mod': _jnp.float32, 'b_mod': _jnp.float32, 'norm_mix_g': _jnp.float32, 'norm_ffn_g': _jnp.float32, 'lru_w_y': _jnp.float32, 'lru_b_y': _jnp.float32, 'lru_w_in': _jnp.float32, 'lru_b_in': _jnp.float32, 'lru_conv_w': _jnp.float32, 'lru_conv_b': _jnp.float32, 'lru_w_a': _jnp.float32, 'lru_b_a': _jnp.float32, 'lru_w_x': _jnp.float32, 'lru_b_x': _jnp.float32, 'lru_lambda': _jnp.float32, 'lru_w_out': _jnp.float32, 'lru_b_out': _jnp.float32, 'pool_w': _jnp.float32, 'pool_scale': _jnp.float32, 'ffn_w1': _jnp.float32, 'ffn_w2': _jnp.float32, 'final_norm_g': _jnp.float32}
MOMENT_SCALE = {'w_mod': 1.718915e-01, 'b_mod': 3.721342e-01, 'norm_mix_g': 6.089527e-02, 'norm_ffn_g': 1.081071e-01, 'lru_w_y': 6.018975e-02, 'lru_b_y': 7.282637e-02, 'lru_w_in': 6.688562e-02, 'lru_b_in': 1.769474e-01, 'lru_conv_w': 6.795026e-02, 'lru_conv_b': 1.694678e-01, 'lru_w_a': 9.076946e-03, 'lru_b_a': 1.287233e-02, 'lru_w_x': 1.586878e-02, 'lru_b_x': 2.439061e-02, 'lru_lambda': 3.145245e-02, 'lru_w_out': 6.537107e-02, 'lru_b_out': 1.264731e-01, 'pool_w': 5.948372e-02, 'pool_scale': 1.919939e-01, 'ffn_w1': 5.575226e-02, 'ffn_w2': 1.311146e-01, 'final_norm_g': 6.457466e+01}


def _to_microbatches(a, axis):
    t = _jnp.moveaxis(a, axis, 0)
    t = t.reshape((N_MICROBATCH, t.shape[0] // N_MICROBATCH) + t.shape[1:])
    return _jnp.moveaxis(t, 1, axis + 1)


def setup_inputs(seed: int = 0) -> dict:
    inp = _fwd_setup_inputs(seed)
    key = _jax.random.fold_in(_jax.random.key(seed), 7919)
    shape, _ = _output_shape()
    out = dict(inp)
    out["loss_target"] = _jax.random.normal(_jax.random.fold_in(key, 0), shape, _jnp.float32)
    for i, name in enumerate(TWIN_WEIGHTS):
        w = inp[name].astype(_jnp.float32)
        if MOMENT_SCALE is None:
            s = _jnp.sqrt(_jnp.mean(_jnp.square(w)) + 1e-30)
        else:
            s = MOMENT_SCALE[name]
        km, kv = _jax.random.split(_jax.random.fold_in(key, i + 1))
        out[name] = w
        out["m_" + name] = s * _jax.random.normal(km, w.shape, _jnp.float32)
        out["v_" + name] = (s * s) * _jax.random.uniform(kv, w.shape, _jnp.float32, 0.5, 1.5)
    if N_MICROBATCH > 1:
        for name, axis in PER_EXAMPLE_BATCH_AXIS.items():
            out[name] = _to_microbatches(out[name], axis)
    return {'x': out['x'], 'c': out['c'], 'w_mod': out['w_mod'], 'b_mod': out['b_mod'], 'norm_mix_g': out['norm_mix_g'], 'norm_ffn_g': out['norm_ffn_g'], 'lru_w_y': out['lru_w_y'], 'lru_b_y': out['lru_b_y'], 'lru_w_in': out['lru_w_in'], 'lru_b_in': out['lru_b_in'], 'lru_conv_w': out['lru_conv_w'], 'lru_conv_b': out['lru_conv_b'], 'lru_w_a': out['lru_w_a'], 'lru_b_a': out['lru_b_a'], 'lru_w_x': out['lru_w_x'], 'lru_b_x': out['lru_b_x'], 'lru_lambda': out['lru_lambda'], 'lru_w_out': out['lru_w_out'], 'lru_b_out': out['lru_b_out'], 'pool_w': out['pool_w'], 'pool_scale': out['pool_scale'], 'ffn_w1': out['ffn_w1'], 'ffn_w2': out['ffn_w2'], 'final_norm_g': out['final_norm_g'], 'loss_target': out['loss_target'], 'm_w_mod': out['m_w_mod'], 'm_b_mod': out['m_b_mod'], 'm_norm_mix_g': out['m_norm_mix_g'], 'm_norm_ffn_g': out['m_norm_ffn_g'], 'm_lru_w_y': out['m_lru_w_y'], 'm_lru_b_y': out['m_lru_b_y'], 'm_lru_w_in': out['m_lru_w_in'], 'm_lru_b_in': out['m_lru_b_in'], 'm_lru_conv_w': out['m_lru_conv_w'], 'm_lru_conv_b': out['m_lru_conv_b'], 'm_lru_w_a': out['m_lru_w_a'], 'm_lru_b_a': out['m_lru_b_a'], 'm_lru_w_x': out['m_lru_w_x'], 'm_lru_b_x': out['m_lru_b_x'], 'm_lru_lambda': out['m_lru_lambda'], 'm_lru_w_out': out['m_lru_w_out'], 'm_lru_b_out': out['m_lru_b_out'], 'm_pool_w': out['m_pool_w'], 'm_pool_scale': out['m_pool_scale'], 'm_ffn_w1': out['m_ffn_w1'], 'm_ffn_w2': out['m_ffn_w2'], 'm_final_norm_g': out['m_final_norm_g'], 'v_w_mod': out['v_w_mod'], 'v_b_mod': out['v_b_mod'], 'v_norm_mix_g': out['v_norm_mix_g'], 'v_norm_ffn_g': out['v_norm_ffn_g'], 'v_lru_w_y': out['v_lru_w_y'], 'v_lru_b_y': out['v_lru_b_y'], 'v_lru_w_in': out['v_lru_w_in'], 'v_lru_b_in': out['v_lru_b_in'], 'v_lru_conv_w': out['v_lru_conv_w'], 'v_lru_conv_b': out['v_lru_conv_b'], 'v_lru_w_a': out['v_lru_w_a'], 'v_lru_b_a': out['v_lru_b_a'], 'v_lru_w_x': out['v_lru_w_x'], 'v_lru_b_x': out['v_lru_b_x'], 'v_lru_lambda': out['v_lru_lambda'], 'v_lru_w_out': out['v_lru_w_out'], 'v_lru_b_out': out['v_lru_b_out'], 'v_pool_w': out['v_pool_w'], 'v_pool_scale': out['v_pool_scale'], 'v_ffn_w1': out['v_ffn_w1'], 'v_ffn_w2': out['v_ffn_w2'], 'v_final_norm_g': out['v_final_norm_g']}


def _loss(weights, diff, rest, loss_target):
    with _jax.named_scope("forward"):
        args = {**rest, TWIN_DIFF_INPUT: diff, **{k: w.astype(_WEIGHT_DTYPES[k]) for k, w in weights.items()}}
        y = _forward(args)
    with _jax.named_scope("loss_head"):
        err = _jnp.square(y.astype(_jnp.float32) - loss_target)
        return 0.5 * _jnp.sum(_jnp.mean(err, axis=-1)) if err.ndim else 0.5 * err


def _adamw(w, g, m, v):
    m = ADAM_B1 * m + (1.0 - ADAM_B1) * g
    v = ADAM_B2 * v + (1.0 - ADAM_B2) * _jnp.square(g)
    m_hat = m / (1.0 - ADAM_B1 ** ADAM_STEP)
    v_hat = v / (1.0 - ADAM_B2 ** ADAM_STEP)
    delta = -ADAM_LR * (m_hat / (_jnp.sqrt(v_hat) + ADAM_EPS) + ADAM_WD * w)
    return delta, m, v


def reference(x, c, w_mod, b_mod, norm_mix_g, norm_ffn_g, lru_w_y, lru_b_y, lru_w_in, lru_b_in, lru_conv_w, lru_conv_b, lru_w_a, lru_b_a, lru_w_x, lru_b_x, lru_lambda, lru_w_out, lru_b_out, pool_w, pool_scale, ffn_w1, ffn_w2, final_norm_g, loss_target, m_w_mod, m_b_mod, m_norm_mix_g, m_norm_ffn_g, m_lru_w_y, m_lru_b_y, m_lru_w_in, m_lru_b_in, m_lru_conv_w, m_lru_conv_b, m_lru_w_a, m_lru_b_a, m_lru_w_x, m_lru_b_x, m_lru_lambda, m_lru_w_out, m_lru_b_out, m_pool_w, m_pool_scale, m_ffn_w1, m_ffn_w2, m_final_norm_g, v_w_mod, v_b_mod, v_norm_mix_g, v_norm_ffn_g, v_lru_w_y, v_lru_b_y, v_lru_w_in, v_lru_b_in, v_lru_conv_w, v_lru_conv_b, v_lru_w_a, v_lru_b_a, v_lru_w_x, v_lru_b_x, v_lru_lambda, v_lru_w_out, v_lru_b_out, v_pool_w, v_pool_scale, v_ffn_w1, v_ffn_w2, v_final_norm_g):
    given = dict(x=x, c=c, w_mod=w_mod, b_mod=b_mod, norm_mix_g=norm_mix_g, norm_ffn_g=norm_ffn_g, lru_w_y=lru_w_y, lru_b_y=lru_b_y, lru_w_in=lru_w_in, lru_b_in=lru_b_in, lru_conv_w=lru_conv_w, lru_conv_b=lru_conv_b, lru_w_a=lru_w_a, lru_b_a=lru_b_a, lru_w_x=lru_w_x, lru_b_x=lru_b_x, lru_lambda=lru_lambda, lru_w_out=lru_w_out, lru_b_out=lru_b_out, pool_w=pool_w, pool_scale=pool_scale, ffn_w1=ffn_w1, ffn_w2=ffn_w2, final_norm_g=final_norm_g, loss_target=loss_target, m_w_mod=m_w_mod, m_b_mod=m_b_mod, m_norm_mix_g=m_norm_mix_g, m_norm_ffn_g=m_norm_ffn_g, m_lru_w_y=m_lru_w_y, m_lru_b_y=m_lru_b_y, m_lru_w_in=m_lru_w_in, m_lru_b_in=m_lru_b_in, m_lru_conv_w=m_lru_conv_w, m_lru_conv_b=m_lru_conv_b, m_lru_w_a=m_lru_w_a, m_lru_b_a=m_lru_b_a, m_lru_w_x=m_lru_w_x, m_lru_b_x=m_lru_b_x, m_lru_lambda=m_lru_lambda, m_lru_w_out=m_lru_w_out, m_lru_b_out=m_lru_b_out, m_pool_w=m_pool_w, m_pool_scale=m_pool_scale, m_ffn_w1=m_ffn_w1, m_ffn_w2=m_ffn_w2, m_final_norm_g=m_final_norm_g, v_w_mod=v_w_mod, v_b_mod=v_b_mod, v_norm_mix_g=v_norm_mix_g, v_norm_ffn_g=v_norm_ffn_g, v_lru_w_y=v_lru_w_y, v_lru_b_y=v_lru_b_y, v_lru_w_in=v_lru_w_in, v_lru_b_in=v_lru_b_in, v_lru_conv_w=v_lru_conv_w, v_lru_conv_b=v_lru_conv_b, v_lru_w_a=v_lru_w_a, v_lru_b_a=v_lru_b_a, v_lru_w_x=v_lru_w_x, v_lru_b_x=v_lru_b_x, v_lru_lambda=v_lru_lambda, v_lru_w_out=v_lru_w_out, v_lru_b_out=v_lru_b_out, v_pool_w=v_pool_w, v_pool_scale=v_pool_scale, v_ffn_w1=v_ffn_w1, v_ffn_w2=v_ffn_w2, v_final_norm_g=v_final_norm_g)
    weights = {n: given[n] for n in TWIN_WEIGHTS}
    shared = {n: given[n] for n in SHARED_INPUTS}
    per_example = {n: given[n] for n in ['x', 'c']}
    grad_fn = _jax.value_and_grad(_loss, argnums=(0, 1))

    def one_microbatch(ex, loss_target):
        ex = dict(ex)
        diff = ex.pop(TWIN_DIFF_INPUT)
        return grad_fn(weights, diff, {**shared, **ex}, loss_target)

    if N_MICROBATCH == 1:
        loss, (grad_w, grad_x) = one_microbatch(per_example, given["loss_target"])
    else:
        def body(carry, xs):
            loss_sum, grad_sum = carry
            l_k, (gw_k, gx_k) = one_microbatch(xs[0], xs[1])
            with _jax.named_scope("update"):
                return (loss_sum + l_k, _jax.tree.map(_jnp.add, grad_sum, gw_k)), gx_k

        init = (_jnp.zeros((), _jnp.float32), _jax.tree.map(_jnp.zeros_like, weights))
        (loss, grad_w), grad_x = _jax.lax.scan(body, init, (per_example, given["loss_target"]))
    with _jax.named_scope("update"):
        delta_w, new_m, new_v = {}, {}, {}
        for n in TWIN_WEIGHTS:
            delta_w[n], new_m[n], new_v[n] = _adamw(weights[n], grad_w[n], given["m_" + n], given["v_" + n])
    return (loss, grad_x, *[grad_w[n] for n in TWIN_WEIGHTS], *[delta_w[n] for n in TWIN_WEIGHTS],
            *[new_m[n] for n in TWIN_WEIGHTS], *[new_v[n] for n in TWIN_WEIGHTS])
```

```python
import functools
import math

import jax
import jax.numpy as jnp
from jax import lax
from jax.experimental import pallas as pl
from jax.experimental.pallas import tpu as pltpu

f32, bf16 = jnp.float32, jnp.bfloat16

D_MODEL = 1024
LRU_WIDTH = 1024
HEADS = 4
HEAD_DIM = 256
D_FF = 4096
DEPTH = 4
N_MOD = 6
N_DEV = 8
FF_CHUNK = D_FF // N_DEV
POOL_WINDOWS = (2, 4, 8, 16)
POOL_HALO = 16
EPS = 1e-6
LRU_C = 8.0

ADAM_LR = 0.001
ADAM_B1 = 0.9
ADAM_B2 = 0.999
ADAM_EPS = 1e-08
ADAM_WD = 0.01
ADAM_STEP = 10

V7X_VMEM_BYTES = 64 * 1024 * 1024
SUBLANES = 8
BF16_ROWS = 16

R_SH_M, R_SC_M, R_GT_M, R_SH_F, R_SC_F, R_GT_F, R_GS_M, R_GS_F = range(8)
P_BY, P_BIN, P_CONVB, P_BA, P_BX, P_LAM, P_BOUT, P_CW0 = 0, 1, 2, 3, 4, 5, 6, 8
G_SH, G_GS, G_GT, G_BY, G_BIN, G_CONVB, G_BA, G_BX, G_LS, G_BOUT, G_CW0 = 0, 1, 2, 3, 4, 5, 6, 7, 8, 9, 10
K_MOD, K_NMIX, K_NFFN, K_LRUB, K_CONVW, K_BA, K_BX, K_PS, K_FIN, K_ROWS = 0, 24, 28, 32, 42, 50, 52, 54, 56, 64


def _params(semantics=None, vmem_mb=48):
    return pltpu.CompilerParams(dimension_semantics=semantics, vmem_limit_bytes=vmem_mb * 1024 * 1024)


def _mm(a, b):
    return jnp.dot(a, b, preferred_element_type=f32)


def _mm_nt(a, b):
    return lax.dot_general(a, b, (((1,), (1,)), ((), ())), preferred_element_type=f32)


def _mm_tn(a, b):
    return lax.dot_general(a, b, (((0,), (0,)), ((), ())), preferred_element_type=f32)


def _rms(x):
    r = lax.rsqrt(jnp.mean(x * x, axis=-1, keepdims=True) + EPS)
    return x * r, r


def _norm_bwd(dh, n, r, gs):
    dn = dh * gs
    return r * (dn - n * jnp.mean(dn * n, axis=-1, keepdims=True))


def _colsum(v):
    return jnp.sum(v, axis=0, keepdims=True)


def _sigmoid(v):
    return 1.0 / (1.0 + jnp.exp(-v))


def _log_sigmoid(v):
    return jnp.minimum(v, 0.0) - jnp.log1p(jnp.exp(-jnp.abs(v)))


_GELU_C = 0.7978845608028654
_GELU_A = 0.044715


def _gelu(v):
    return 0.5 * v * (1.0 + jnp.tanh(_GELU_C * (v + _GELU_A * v * v * v)))


def _gelu_grad(v):
    t = jnp.tanh(_GELU_C * (v + _GELU_A * v * v * v))
    return 0.5 * (1.0 + t) + 0.5 * v * (1.0 - t * t) * _GELU_C * (1.0 + 3.0 * _GELU_A * v * v)


def _neg_expm1(v):
    series = -v * (1.0 + v * (0.5 + v * (1.0 / 6 + v * (1.0 / 24 + v * (1.0 / 120 + v * (1.0 / 720))))))
    return jnp.where(v > -0.1, series, 1.0 - jnp.exp(v))


def _rows_before(halo, v, shifts):
    hr = halo.shape[0]
    ext = jnp.concatenate([halo, v], axis=0)
    return [pltpu.roll(ext, k, 0)[hr:] for k in shifts]


def _rows_after(v, halo, shifts):
    n = v.shape[0]
    ext = jnp.concatenate([v, halo], axis=0)
    return [pltpu.roll(ext, ext.shape[0] - k, 0)[:n] for k in shifts]


def _block_diag(v, w_ref, kind):
    return jnp.concatenate(
        [_mm(v[:, h * HEAD_DIM:(h + 1) * HEAD_DIM], w_ref[kind, h]) for h in range(HEADS)], axis=1)


def _block_diag_t(v, w_ref, kind):
    return jnp.concatenate(
        [_mm_nt(v[:, h * HEAD_DIM:(h + 1) * HEAD_DIM], w_ref[kind, h]) for h in range(HEADS)], axis=1)


def _scan(a_ref, u_ref, out_ref, carry, reverse):
    groups = a_ref.shape[0] // SUBLANES
    width = a_ref.shape[1]
    row = lax.broadcasted_iota(jnp.int32, (SUBLANES, width), 0)

    def step(j, c):
        g = groups - 1 - j if reverse else j
        off = pl.multiple_of(g * SUBLANES, SUBLANES)
        a = a_ref[pl.ds(off, SUBLANES), :]
        u = u_ref[pl.ds(off, SUBLANES), :]
        for k in (1, 2, 4):
            if reverse:
                valid, shift = row < SUBLANES - k, SUBLANES - k
            else:
                valid, shift = row >= k, k
            a_s = jnp.where(valid, pltpu.roll(a, shift, 0), 1.0)
            u_s = jnp.where(valid, pltpu.roll(u, shift, 0), 0.0)
            u = u + a * u_s
            a = a * a_s
        h = u + a * c
        out_ref[pl.ds(off, SUBLANES), :] = h
        last = h[0:1, :] if reverse else h[SUBLANES - 1:SUBLANES, :]
        return jnp.broadcast_to(last, (SUBLANES, width))

    return lax.fori_loop(0, groups, step, carry)


def _exchange(arrays, gather, name):
    n = len(arrays)
    peers = N_DEV - 1

    def body(*refs):
        ins, outs = refs[:n], refs[n:2 * n]
        send_sems, recv_sems, local_sems = refs[2 * n:]
        x, y, c = lax.axis_index("x"), lax.axis_index("y"), lax.axis_index("c")
        me = 4 * x + 2 * y + c
        local = []
        for k in range(n):
            cp = pltpu.make_async_copy(ins[k] if gather else ins[k].at[me], outs[k].at[me], local_sems.at[k])
            cp.start()
            local.append(cp)
        remote = []
        for p in range(1, N_DEV):
            px = 1 - x if p & 4 else x
            py = 1 - y if p & 2 else y
            pc = 1 - c if p & 1 else c
            for k in range(n):
                cp = pltpu.make_async_remote_copy(
                    src_ref=ins[k] if gather else ins[k].at[4 * px + 2 * py + pc],
                    dst_ref=outs[k].at[me],
                    send_sem=send_sems.at[k * peers + p - 1],
                    recv_sem=recv_sems.at[k * peers + p - 1],
                    device_id=(px, py, pc), device_id_type=pl.DeviceIdType.MESH)
                cp.start()
                remote.append(cp)
        for cp in remote:
            cp.wait()
        for cp in local:
            cp.wait()

    out_shape = tuple(
        jax.ShapeDtypeStruct(((N_DEV,) + a.shape) if gather else a.shape, a.dtype) for a in arrays)
    outs = pl.pallas_call(
        body, name=name, out_shape=out_shape,
        in_specs=[pl.BlockSpec(memory_space=pl.ANY)] * n,
        out_specs=tuple(pl.BlockSpec(memory_space=pl.ANY) for _ in range(n)),
        scratch_shapes=[pltpu.SemaphoreType.DMA((n * peers,)), pltpu.SemaphoreType.DMA((n * peers,)),
                        pltpu.SemaphoreType.DMA((n,))],
        compiler_params=pltpu.CompilerParams(has_side_effects=True),
    )(*arrays)
    return list(outs)


def _mod_part(c_all, w_mod):
    depth, d, cols = w_mod.shape

    def body(c_ref, w_ref, o_ref):
        cv = c_ref[...]
        cond = cv * _sigmoid(cv)
        o_ref[...] = jnp.dot(cond, w_ref[...], preferred_element_type=f32, precision=lax.Precision.HIGHEST)

    return pl.pallas_call(
        body, name="mod_part", grid=(depth,),
        out_shape=jax.ShapeDtypeStruct((depth, N_DEV, cols), f32),
        in_specs=[pl.BlockSpec((N_DEV, d), lambda i: (0, 0)), pl.BlockSpec((None, d, cols), lambda i: (i, 0, 0))],
        out_specs=pl.BlockSpec((None, N_DEV, cols), lambda i: (i, 0, 0)),
        compiler_params=_params(("arbitrary",), 32),
    )(c_all, w_mod)


def _mod_table(mod_row, b_mod, g_mix, g_ffn):
    def body(m_ref, b_ref, gm_ref, gf_ref, o_ref):
        for i in range(DEPTH):
            for k in range(N_MOD):
                o_ref[i, k:k + 1, :] = m_ref[i:i + 1, k * D_MODEL:(k + 1) * D_MODEL] + b_ref[i:i + 1, k * D_MODEL:(k + 1) * D_MODEL]
            o_ref[i, R_GS_M:R_GS_M + 1, :] = gm_ref[i:i + 1, :] * (1.0 + o_ref[i, R_SC_M:R_SC_M + 1, :])
            o_ref[i, R_GS_F:R_GS_F + 1, :] = gf_ref[i:i + 1, :] * (1.0 + o_ref[i, R_SC_F:R_SC_F + 1, :])

    return pl.pallas_call(body, name="mod_table", out_shape=jax.ShapeDtypeStruct((DEPTH, 8, D_MODEL), f32))(
        mod_row, b_mod, g_mix, g_ffn)


def _ffn_tile(s):
    return min(512, s)


def _ffn_fwd(x, vec, w1g, w2g, layer):
    s = x.shape[0]
    ts = _ffn_tile(s)

    def body(x_ref, vec_ref, w1_ref, w2_ref, xo_ref, u_ref, y_ref, hb_ref, acc_ref):
        f = pl.program_id(1)

        @pl.when(f == 0)
        def _():
            n, _ = _rms(x_ref[...])
            hb_ref[...] = (n * vec_ref[R_GS_F:R_GS_F + 1, :] + vec_ref[R_SH_F:R_SH_F + 1, :]).astype(bf16)
            acc_ref[...] = jnp.zeros_like(acc_ref)

        u = jnp.maximum(_mm(hb_ref[...], w1_ref[...]), 0.0)
        u_ref[...] = u.astype(bf16)
        acc_ref[...] += _mm((u * u).astype(bf16), w2_ref[...])

        @pl.when(f == N_DEV - 1)
        def _():
            yv = acc_ref[...]
            y_ref[...] = yv.astype(bf16)
            xo_ref[...] = x_ref[...] + vec_ref[R_GT_F:R_GT_F + 1, :] * yv

    row = pl.BlockSpec((ts, D_MODEL), lambda i, f: (i, 0))
    return pl.pallas_call(
        body, name=f"ffn_fwd_{layer}", grid=(s // ts, N_DEV),
        out_shape=(jax.ShapeDtypeStruct((s, D_MODEL), f32), jax.ShapeDtypeStruct((s, D_FF), bf16),
                   jax.ShapeDtypeStruct((s, D_MODEL), bf16), jax.ShapeDtypeStruct((s, D_MODEL), bf16)),
        in_specs=[row, pl.BlockSpec((8, D_MODEL), lambda i, f: (0, 0)),
                  pl.BlockSpec((None, None, D_MODEL, FF_CHUNK), lambda i, f: (f, layer, 0, 0)),
                  pl.BlockSpec((None, None, FF_CHUNK, D_MODEL), lambda i, f: (f, layer, 0, 0))],
        out_specs=(row, pl.BlockSpec((ts, FF_CHUNK), lambda i, f: (i, f)), row, row),
        scratch_shapes=[pltpu.VMEM((ts, D_MODEL), f32)],
        compiler_params=_params(("arbitrary", "arbitrary")),
    )(x, vec, w1g, w2g)


def _ffn_bwd_act(x, dx, u, y, vec, w1g, w2g, layer):
    s = x.shape[0]
    ts = _ffn_tile(s)

    def body(x_ref, dx_ref, u_ref, y_ref, vec_ref, w1_ref, w2_ref, dxo_ref, da_ref, dyb_ref, sm_ref, acc_ref):
        i, f = pl.program_id(0), pl.program_id(1)

        @pl.when((i == 0) & (f == 0))
        def _():
            sm_ref[...] = jnp.zeros_like(sm_ref)

        @pl.when(f == 0)
        def _():
            dxv = dx_ref[...]
            dyb_ref[...] = (dxv * vec_ref[R_GT_F:R_GT_F + 1, :]).astype(bf16)
            acc_ref[...] = jnp.zeros_like(acc_ref)
            sm_ref[G_GT:G_GT + 1, :] += _colsum(dxv * y_ref[...].astype(f32))

        dz = _mm_nt(dyb_ref[...], w2_ref[...])
        dab = (dz * (2.0 * u_ref[...].astype(f32))).astype(bf16)
        da_ref[...] = dab
        acc_ref[...] += _mm_nt(dab, w1_ref[...])

        @pl.when(f == N_DEV - 1)
        def _():
            dh = acc_ref[...]
            n, r = _rms(x_ref[...])
            sm_ref[G_SH:G_SH + 1, :] += _colsum(dh)
            sm_ref[G_GS:G_GS + 1, :] += _colsum(dh * n)
            dxo_ref[...] = dx_ref[...] + _norm_bwd(dh, n, r, vec_ref[R_GS_F:R_GS_F + 1, :])

    row = pl.BlockSpec((ts, D_MODEL), lambda i, f: (i, 0))
    chunk = pl.BlockSpec((ts, FF_CHUNK), lambda i, f: (i, f))
    return pl.pallas_call(
        body, name=f"ffn_bwd_act_{layer}", grid=(s // ts, N_DEV),
        out_shape=(jax.ShapeDtypeStruct((s, D_MODEL), f32), jax.ShapeDtypeStruct((s, D_FF), bf16),
                   jax.ShapeDtypeStruct((s, D_MODEL), bf16), jax.ShapeDtypeStruct((8, D_MODEL), f32)),
        in_specs=[row, row, chunk, row, pl.BlockSpec((8, D_MODEL), lambda i, f: (0, 0)),
                  pl.BlockSpec((None, None, D_MODEL, FF_CHUNK), lambda i, f: (f, layer, 0, 0)),
                  pl.BlockSpec((None, None, FF_CHUNK, D_MODEL), lambda i, f: (f, layer, 0, 0))],
        out_specs=(row, chunk, row, pl.BlockSpec((8, D_MODEL), lambda i, f: (0, 0))),
        scratch_shapes=[pltpu.VMEM((ts, D_MODEL), f32)],
        compiler_params=_params(("arbitrary", "arbitrary")),
    )(x, dx, u, y, vec, w1g, w2g)


def _ffn_bwd_weights(hb, u, da, dyb, layer):
    s = hb.shape[0]
    ts = _ffn_tile(s)
    nt = s // ts

    def body(hb_ref, u_ref, da_ref, dyb_ref, dw1_ref, dw2_ref, a1_ref, a2_ref):
        i = pl.program_id(1)

        @pl.when(i == 0)
        def _():
            a1_ref[...] = jnp.zeros_like(a1_ref)
            a2_ref[...] = jnp.zeros_like(a2_ref)

        a1_ref[...] += _mm_tn(hb_ref[...], da_ref[...])
        uv = u_ref[...].astype(f32)
        a2_ref[...] += _mm_tn((uv * uv).astype(bf16), dyb_ref[...])

        @pl.when(i == nt - 1)
        def _():
            dw1_ref[...] = a1_ref[...].astype(bf16)
            dw2_ref[...] = a2_ref[...].astype(bf16)

    row = pl.BlockSpec((ts, D_MODEL), lambda f, i: (i, 0))
    chunk = pl.BlockSpec((ts, FF_CHUNK), lambda f, i: (i, f))
    return pl.pallas_call(
        body, name=f"ffn_bwd_w_{layer}", grid=(N_DEV, nt),
        out_shape=(jax.ShapeDtypeStruct((N_DEV, D_MODEL, FF_CHUNK), bf16),
                   jax.ShapeDtypeStruct((N_DEV, FF_CHUNK, D_MODEL), bf16)),
        in_specs=[row, chunk, chunk, row],
        out_specs=(pl.BlockSpec((None, D_MODEL, FF_CHUNK), lambda f, i: (f, 0, 0)),
                   pl.BlockSpec((None, FF_CHUNK, D_MODEL), lambda f, i: (f, 0, 0))),
        scratch_shapes=[pltpu.VMEM((D_MODEL, FF_CHUNK), f32), pltpu.VMEM((FF_CHUNK, D_MODEL), f32)],
        compiler_params=_params(("arbitrary", "arbitrary")),
    )(hb, u, da, dyb)


def _lru_tile(s):
    return min(256, s)


def _lru_gates(xc, wsm_ref, pv_ref):
    xcb = xc.astype(bf16)
    gr = _sigmoid(_block_diag(xcb, wsm_ref, 0) + pv_ref[P_BA:P_BA + 1, :])
    gi = _sigmoid(_block_diag(xcb, wsm_ref, 1) + pv_ref[P_BX:P_BX + 1, :])
    ls = _log_sigmoid(pv_ref[P_LAM:P_LAM + 1, :])
    log_a = LRU_C * gr * ls
    return xcb, gr, gi, jnp.exp(log_a), jnp.sqrt(_neg_expm1(2.0 * log_a)), ls


def _conv(xr, taps_before, pv_ref):
    xc = xr * pv_ref[P_CW0 + 3:P_CW0 + 4, :] + pv_ref[P_CONVB:P_CONVB + 1, :]
    for k, v in zip((2, 1, 0), taps_before):
        xc = xc + v * pv_ref[P_CW0 + k:P_CW0 + k + 1, :]
    return xc


def _lru_fwd(x, vec, wbig, wsm, pvec, layer):
    s = x.shape[0]
    ts = _lru_tile(s)
    w = LRU_WIDTH

    def body(x_ref, vec_ref, wb_ref, wsm_ref, pv_ref, xo_ref, gpre_ref, xr_ref, hs_ref, y_ref,
             tail_ref, carry_ref, a_scr, u_scr):
        @pl.when(pl.program_id(0) == 0)
        def _():
            tail_ref[...] = jnp.zeros_like(tail_ref)
            carry_ref[...] = jnp.zeros_like(carry_ref)

        xv = x_ref[...]
        n, _ = _rms(xv)
        hb = (n * vec_ref[R_GS_M:R_GS_M + 1, :] + vec_ref[R_SH_M:R_SH_M + 1, :]).astype(bf16)
        gpre = _mm(hb, wb_ref[0]) + pv_ref[P_BY:P_BY + 1, :]
        xr = _mm(hb, wb_ref[1]) + pv_ref[P_BIN:P_BIN + 1, :]
        gpre_ref[...] = gpre.astype(bf16)
        xr_ref[...] = xr.astype(bf16)
        xc = _conv(xr, _rows_before(tail_ref[...], xr, (1, 2, 3)), pv_ref)
        tail_ref[...] = xr[ts - SUBLANES:, :]
        _, _, gi, a, mult, _ = _lru_gates(xc, wsm_ref, pv_ref)
        a_scr[...] = a
        u_scr[...] = mult * (gi * xc)
        carry_ref[...] = _scan(a_scr, u_scr, hs_ref, carry_ref[...], reverse=False)
        m = hs_ref[...] * _gelu(gpre)
        yv = _mm(m.astype(bf16), wb_ref[2]) + pv_ref[P_BOUT:P_BOUT + 1, :]
        y_ref[...] = yv.astype(bf16)
        xo_ref[...] = xv + vec_ref[R_GT_M:R_GT_M + 1, :] * yv

    row = pl.BlockSpec((ts, D_MODEL), lambda i: (i, 0))
    roww = pl.BlockSpec((ts, w), lambda i: (i, 0))
    return pl.pallas_call(
        body, name=f"lru_fwd_{layer}", grid=(s // ts,),
        out_shape=(jax.ShapeDtypeStruct((s, D_MODEL), f32), jax.ShapeDtypeStruct((s, w), bf16),
                   jax.ShapeDtypeStruct((s, w), bf16), jax.ShapeDtypeStruct((s, w), f32),
                   jax.ShapeDtypeStruct((s, D_MODEL), bf16)),
        in_specs=[row, pl.BlockSpec((8, D_MODEL), lambda i: (0, 0)),
                  pl.BlockSpec((3, w, w), lambda i: (0, 0, 0)),
                  pl.BlockSpec((2, HEADS, HEAD_DIM, HEAD_DIM), lambda i: (0, 0, 0, 0)),
                  pl.BlockSpec((16, w), lambda i: (0, 0))],
        out_specs=(row, roww, roww, roww, row),
        scratch_shapes=[pltpu.VMEM((SUBLANES, w), f32), pltpu.VMEM((SUBLANES, w), f32),
                        pltpu.VMEM((ts, w), f32), pltpu.VMEM((ts, w), f32)],
        compiler_params=_params(("arbitrary",)),
    )(x, vec, wbig, wsm, pvec)


def _lru_bwd(x, dx, gpre, xr, hs, y, vec, wbig, wsm, pvec, layer):
    s = x.shape[0]
    ts = min(128, s)
    nt = s // ts
    w = LRU_WIDTH
    shard = w // N_DEV
    hshard = HEAD_DIM // N_DEV

    def body(x_ref, dx_ref, gpre_ref, xr_ref, xrh_ref, hs_ref, hsh_ref, y_ref, vec_ref, wb_ref, wsm_ref, pv_ref,
             dxo_ref, dwb_ref, dwsm_ref, sm_ref,
             accb_ref, accs_ref, eps8_ref, dxc8_ref, a_scr, u_scr, e_scr):
        i = pl.program_id(0)
        first_tile = i == nt - 1

        @pl.when(i == 0)
        def _():
            accb_ref[...] = jnp.zeros_like(accb_ref)
            accs_ref[...] = jnp.zeros_like(accs_ref)
            sm_ref[...] = jnp.zeros_like(sm_ref)
            eps8_ref[...] = jnp.zeros_like(eps8_ref)
            dxc8_ref[...] = jnp.zeros_like(dxc8_ref)

        gs = vec_ref[R_GS_M:R_GS_M + 1, :]
        xv = x_ref[...]
        dxv = dx_ref[...]
        n, r = _rms(xv)
        hb = (n * gs + vec_ref[R_SH_M:R_SH_M + 1, :]).astype(bf16)
        gpre_v = gpre_ref[...].astype(f32)
        xrv = xr_ref[...].astype(f32)
        hsv = hs_ref[...]
        xr_halo = jnp.where(first_tile, 0.0, xrh_ref[...].astype(f32))
        hs_halo = jnp.where(first_tile, 0.0, hsh_ref[...])
        xs1, xs2, xs3 = _rows_before(xr_halo, xrv, (1, 2, 3))
        xc = _conv(xrv, (xs1, xs2, xs3), pv_ref)
        xcb, gr, gi, a, mult, ls = _lru_gates(xc, wsm_ref, pv_ref)
        gelu_v = _gelu(gpre_v)

        dy = dxv * vec_ref[R_GT_M:R_GT_M + 1, :]
        dyb = dy.astype(bf16)
        sm_ref[G_GT:G_GT + 1, :] += _colsum(dxv * y_ref[...].astype(f32))
        sm_ref[G_BOUT:G_BOUT + 1, :] += _colsum(dy)
        accb_ref[2] += _mm_tn((hsv * gelu_v).astype(bf16), dyb)
        dm = _mm_nt(dyb, wb_ref[2])
        dhs = dm * gelu_v
        dgpre = dm * hsv * _gelu_grad(gpre_v)

        a_scr[...] = a
        u_scr[...] = a * dhs
        _scan(a_scr, u_scr, e_scr, jnp.broadcast_to(eps8_ref[0:1, :], (SUBLANES, w)), reverse=True)
        (eps_next,) = _rows_after(e_scr[...], eps8_ref[...], (1,))
        eps8_ref[...] = e_scr[0:SUBLANES, :]
        delta = dhs + eps_next
        (h_prev,) = _rows_before(hs_halo, hsv, (1,))
        da = delta * h_prev
        dgi = delta * mult * xc
        dxc = delta * mult * gi
        dla = da * a - (delta * gi * xc) * (a * a) / mult
        sm_ref[G_LS:G_LS + 1, :] += _colsum(dla * (LRU_C * gr))
        dra = dla * (LRU_C * ls) * gr * (1.0 - gr)
        drx = dgi * gi * (1.0 - gi)
        drab, drxb = dra.astype(bf16), drx.astype(bf16)
        sm_ref[G_BA:G_BA + 1, :] += _colsum(dra)
        sm_ref[G_BX:G_BX + 1, :] += _colsum(drx)
        for h in range(HEADS):
            cols = slice(h * HEAD_DIM, (h + 1) * HEAD_DIM)
            accs_ref[0, h] += _mm_tn(xcb[:, cols], drab[:, cols])
            accs_ref[1, h] += _mm_tn(xcb[:, cols], drxb[:, cols])
        dxc = dxc + _block_diag_t(drab, wsm_ref, 0) + _block_diag_t(drxb, wsm_ref, 1)

        sm_ref[G_CONVB:G_CONVB + 1, :] += _colsum(dxc)
        for k, v in zip((3, 2, 1, 0), (xrv, xs1, xs2, xs3)):
            sm_ref[G_CW0 + k:G_CW0 + k + 1, :] += _colsum(dxc * v)
        ups = _rows_after(dxc, dxc8_ref[...], (1, 2, 3))
        dxc8_ref[...] = dxc[0:SUBLANES, :]
        dxr = dxc * pv_ref[P_CW0 + 3:P_CW0 + 4, :]
        for k, v in zip((2, 1, 0), ups):
            dxr = dxr + v * pv_ref[P_CW0 + k:P_CW0 + k + 1, :]

        dgb, dxrb = dgpre.astype(bf16), dxr.astype(bf16)
        sm_ref[G_BY:G_BY + 1, :] += _colsum(dgpre)
        sm_ref[G_BIN:G_BIN + 1, :] += _colsum(dxr)
        accb_ref[0] += _mm_tn(hb, dgb)
        accb_ref[1] += _mm_tn(hb, dxrb)
        dh = _mm_nt(dgb, wb_ref[0]) + _mm_nt(dxrb, wb_ref[1])
        sm_ref[G_SH:G_SH + 1, :] += _colsum(dh)
        sm_ref[G_GS:G_GS + 1, :] += _colsum(dh * n)
        dxo_ref[...] = dxv + _norm_bwd(dh, n, r, gs)

        @pl.when(i == nt - 1)
        def _():
            for k in range(3):
                dwb_ref[:, k] = accb_ref[k].astype(bf16).reshape(N_DEV, shard, w)
            for k in range(2):
                for h in range(HEADS):
                    dwsm_ref[:, k, h] = accs_ref[k, h].astype(bf16).reshape(N_DEV, hshard, HEAD_DIM)

    rev = lambda i: (nt - 1 - i, 0)
    row = pl.BlockSpec((ts, D_MODEL), rev)
    roww = pl.BlockSpec((ts, w), rev)
    halo16 = pl.BlockSpec((BF16_ROWS, w), lambda i: (jnp.maximum((nt - 1 - i) * (ts // BF16_ROWS) - 1, 0), 0))
    halo8 = pl.BlockSpec((SUBLANES, w), lambda i: (jnp.maximum((nt - 1 - i) * (ts // SUBLANES) - 1, 0), 0))
    const = lambda *shape: pl.BlockSpec(shape, lambda i: (0,) * len(shape), pipeline_mode=pl.Buffered(1))
    return pl.pallas_call(
        body, name=f"lru_bwd_{layer}", grid=(nt,),
        out_shape=(jax.ShapeDtypeStruct((s, D_MODEL), f32),
                   jax.ShapeDtypeStruct((N_DEV, 3, shard, w), bf16),
                   jax.ShapeDtypeStruct((N_DEV, 2, HEADS, hshard, HEAD_DIM), bf16),
                   jax.ShapeDtypeStruct((16, w), f32)),
        in_specs=[row, row, roww, roww, halo16, roww, halo8, row, const(8, D_MODEL), const(3, w, w),
                  const(2, HEADS, HEAD_DIM, HEAD_DIM), const(16, w)],
        out_specs=(row, const(N_DEV, 3, shard, w), const(N_DEV, 2, HEADS, hshard, HEAD_DIM), const(16, w)),
        scratch_shapes=[pltpu.VMEM((3, w, w), f32), pltpu.VMEM((2, HEADS, HEAD_DIM, HEAD_DIM), f32),
                        pltpu.VMEM((SUBLANES, w), f32), pltpu.VMEM((SUBLANES, w), f32),
                        pltpu.VMEM((ts, w), f32), pltpu.VMEM((ts, w), f32), pltpu.VMEM((ts, w), f32)],
        compiler_params=_params(("arbitrary",), 56),
    )(x, dx, gpre, xr, xr, hs, hs, y, vec, wbig, wsm, pvec)


def _pool_tile(s):
    return min(256, s)


def _pool_counts(tile_index, ts):
    t = (tile_index * ts + lax.broadcasted_iota(jnp.int32, (ts, 1), 0) + 1).astype(f32)
    return [1.0 / jnp.minimum(t, float(win)) for win in POOL_WINDOWS]


def _pooled(h, halo, inv):
    ext = jnp.concatenate([halo, h], axis=0)
    out = []
    for g in range(len(POOL_WINDOWS)):
        acc = ext[:, g * HEAD_DIM:(g + 1) * HEAD_DIM]
        for step in range(g + 1):
            acc = acc + pltpu.roll(acc, 1 << step, 0)
        out.append(acc[POOL_HALO:] * inv[g] - h[:, g * HEAD_DIM:(g + 1) * HEAD_DIM])
    return out


def _pool_fwd(x, vec, pw, ps, layer):
    s = x.shape[0]
    ts = _pool_tile(s)

    def body(x_ref, vec_ref, pw_ref, ps_ref, xo_ref, y_ref, halo_ref):
        i = pl.program_id(0)

        @pl.when(i == 0)
        def _():
            halo_ref[...] = jnp.zeros_like(halo_ref)

        xv = x_ref[...]
        n, _ = _rms(xv)
        h = n * vec_ref[R_GS_M:R_GS_M + 1, :] + vec_ref[R_SH_M:R_SH_M + 1, :]
        pooled = _pooled(h, halo_ref[...], _pool_counts(i, ts))
        halo_ref[...] = h[ts - POOL_HALO:, :]
        mixed = jnp.concatenate([_mm(pooled[g].astype(bf16), pw_ref[g]) for g in range(HEADS)], axis=1)
        yv = mixed * ps_ref[0:1, :]
        y_ref[...] = yv.astype(bf16)
        xo_ref[...] = xv + vec_ref[R_GT_M:R_GT_M + 1, :] * yv

    row = pl.BlockSpec((ts, D_MODEL), lambda i: (i, 0))
    return pl.pallas_call(
        body, name=f"pool_fwd_{layer}", grid=(s // ts,),
        out_shape=(jax.ShapeDtypeStruct((s, D_MODEL), f32), jax.ShapeDtypeStruct((s, D_MODEL), bf16)),
        in_specs=[row, pl.BlockSpec((8, D_MODEL), lambda i: (0, 0)),
                  pl.BlockSpec((HEADS, HEAD_DIM, HEAD_DIM), lambda i: (0, 0, 0)),
                  pl.BlockSpec((8, D_MODEL), lambda i: (0, 0))],
        out_specs=(row, row),
        scratch_shapes=[pltpu.VMEM((POOL_HALO, D_MODEL), f32)],
        compiler_params=_params(("arbitrary",)),
    )(x, vec, pw, ps)


def _pool_bwd(x, dx, y, vec, pw, ps, layer):
    s = x.shape[0]
    ts = _pool_tile(s)
    nt = s // ts
    hshard = HEAD_DIM // N_DEV

    def body(x_ref, xh_ref, dx_ref, y_ref, vec_ref, pw_ref, ps_ref, dxo_ref, dpw_ref, sm_ref, acc_ref, q16_ref):
        i = pl.program_id(0)
        tile = nt - 1 - i

        @pl.when(i == 0)
        def _():
            acc_ref[...] = jnp.zeros_like(acc_ref)
            sm_ref[...] = jnp.zeros_like(sm_ref)
            q16_ref[...] = jnp.zeros_like(q16_ref)

        gs, sh = vec_ref[R_GS_M:R_GS_M + 1, :], vec_ref[R_SH_M:R_SH_M + 1, :]
        xv = x_ref[...]
        dxv = dx_ref[...]
        n, r = _rms(xv)
        h = n * gs + sh
        nh, _ = _rms(xh_ref[...])
        halo = jnp.where(tile == 0, 0.0, nh * gs + sh)
        inv = _pool_counts(tile, ts)
        pooled = _pooled(h, halo, inv)
        mixed = jnp.concatenate([_mm(pooled[g].astype(bf16), pw_ref[g]) for g in range(HEADS)], axis=1)

        dy = dxv * vec_ref[R_GT_M:R_GT_M + 1, :]
        sm_ref[G_GT:G_GT + 1, :] += _colsum(dxv * y_ref[...].astype(f32))
        sm_ref[3:4, :] += _colsum(dy * mixed)
        dmixed = (dy * ps_ref[0:1, :]).astype(bf16)
        dh_parts = []
        for g in range(HEADS):
            cols = slice(g * HEAD_DIM, (g + 1) * HEAD_DIM)
            acc_ref[g] += _mm_tn(pooled[g].astype(bf16), dmixed[:, cols])
            dpooled = _mm_nt(dmixed[:, cols], pw_ref[g])
            q = dpooled * inv[g]
            ext = jnp.concatenate([q, q16_ref[:, cols]], axis=0)
            q16_ref[:, cols] = q[0:POOL_HALO, :]
            for step in range(g + 1):
                ext = ext + pltpu.roll(ext, ext.shape[0] - (1 << step), 0)
            dh_parts.append(ext[:ts] - dpooled)
        dh = jnp.concatenate(dh_parts, axis=1)
        sm_ref[G_SH:G_SH + 1, :] += _colsum(dh)
        sm_ref[G_GS:G_GS + 1, :] += _colsum(dh * n)
        dxo_ref[...] = dxv + _norm_bwd(dh, n, r, gs)

        @pl.when(i == nt - 1)
        def _():
            for g in range(HEADS):
                dpw_ref[:, g] = acc_ref[g].astype(bf16).reshape(N_DEV, hshard, HEAD_DIM)

    rev = lambda i: (nt - 1 - i, 0)
    row = pl.BlockSpec((ts, D_MODEL), rev)
    halo16 = pl.BlockSpec((POOL_HALO, D_MODEL), lambda i: (jnp.maximum((nt - 1 - i) * (ts // POOL_HALO) - 1, 0), 0))
    const = lambda *shape: pl.BlockSpec(shape, lambda i: (0,) * len(shape))
    return pl.pallas_call(
        body, name=f"pool_bwd_{layer}", grid=(nt,),
        out_shape=(jax.ShapeDtypeStruct((s, D_MODEL), f32),
                   jax.ShapeDtypeStruct((N_DEV, HEADS, hshard, HEAD_DIM), bf16),
                   jax.ShapeDtypeStruct((8, D_MODEL), f32)),
        in_specs=[row, halo16, row, row, const(8, D_MODEL), const(HEADS, HEAD_DIM, HEAD_DIM), const(8, D_MODEL)],
        out_specs=(row, const(N_DEV, HEADS, hshard, HEAD_DIM), const(8, D_MODEL)),
        scratch_shapes=[pltpu.VMEM((HEADS, HEAD_DIM, HEAD_DIM), f32), pltpu.VMEM((POOL_HALO, D_MODEL), f32)],
        compiler_params=_params(("arbitrary",)),
    )(x, x, dx, y, vec, pw, ps)


def _final(x, target, g_fin):
    s = x.shape[0]
    ts = min(512, s)

    def body(x_ref, t_ref, g_ref, dx_ref, loss_ref, sm_ref):
        @pl.when(pl.program_id(0) == 0)
        def _():
            loss_ref[...] = jnp.zeros_like(loss_ref)
            sm_ref[...] = jnp.zeros_like(sm_ref)

        g = g_ref[0:1, :]
        n, r = _rms(x_ref[...])
        err = n * g - t_ref[...]
        loss_ref[...] += 0.5 * jnp.sum(jnp.mean(err * err, axis=-1, keepdims=True), axis=0, keepdims=True)
        dyv = err * (1.0 / D_MODEL)
        sm_ref[0:1, :] += _colsum(dyv * n)
        dx_ref[...] = _norm_bwd(dyv, n, r, g)

    row = pl.BlockSpec((ts, D_MODEL), lambda i: (i, 0))
    return pl.pallas_call(
        body, name="final_loss", grid=(s // ts,),
        out_shape=(jax.ShapeDtypeStruct((s, D_MODEL), f32), jax.ShapeDtypeStruct((8, 128), f32),
                   jax.ShapeDtypeStruct((8, D_MODEL), f32)),
        in_specs=[row, row, pl.BlockSpec((8, D_MODEL), lambda i: (0, 0))],
        out_specs=(row, pl.BlockSpec((8, 128), lambda i: (0, 0)), pl.BlockSpec((8, D_MODEL), lambda i: (0, 0))),
        compiler_params=_params(("arbitrary",)),
    )(x, target, g_fin)


def _small_pack(sm_ffn, sm_mix, sm_fin, table, g_mix, g_ffn, lam):
    def body(*refs):
        ffn, mix = refs[0:DEPTH], refs[DEPTH:2 * DEPTH]
        fin_ref, tab_ref, gm_ref, gf_ref, lam_ref, o_ref = refs[2 * DEPTH:]
        o_ref[...] = jnp.zeros_like(o_ref)
        for i in range(DEPTH):
            base = K_MOD + i * N_MOD
            o_ref[base + 0:base + 1, :] = mix[i][G_SH:G_SH + 1, :]
            o_ref[base + 1:base + 2, :] = mix[i][G_GS:G_GS + 1, :] * gm_ref[i:i + 1, :]
            o_ref[base + 2:base + 3, :] = mix[i][G_GT:G_GT + 1, :]
            o_ref[base + 3:base + 4, :] = ffn[i][G_SH:G_SH + 1, :]
            o_ref[base + 4:base + 5, :] = ffn[i][G_GS:G_GS + 1, :] * gf_ref[i:i + 1, :]
            o_ref[base + 5:base + 6, :] = ffn[i][G_GT:G_GT + 1, :]
            o_ref[K_NMIX + i:K_NMIX + i + 1, :] = mix[i][G_GS:G_GS + 1, :] * (1.0 + tab_ref[i, R_SC_M:R_SC_M + 1, :])
            o_ref[K_NFFN + i:K_NFFN + i + 1, :] = ffn[i][G_GS:G_GS + 1, :] * (1.0 + tab_ref[i, R_SC_F:R_SC_F + 1, :])
            j = i // 2
            if i % 2 == 0:
                for k, src in enumerate((G_BY, G_BIN, G_CONVB, None, G_BOUT)):
                    dst = K_LRUB + j * 5 + k
                    if src is None:
                        o_ref[dst:dst + 1, :] = mix[i][G_LS:G_LS + 1, :] * _sigmoid(-lam_ref[j:j + 1, :])
                    else:
                        o_ref[dst:dst + 1, :] = mix[i][src:src + 1, :]
                o_ref[K_CONVW + j * 4:K_CONVW + j * 4 + 4, :] = mix[i][G_CW0:G_CW0 + 4, :]
                o_ref[K_BA + j:K_BA + j + 1, :] = mix[i][G_BA:G_BA + 1, :]
                o_ref[K_BX + j:K_BX + j + 1, :] = mix[i][G_BX:G_BX + 1, :]
            else:
                o_ref[K_PS + j:K_PS + j + 1, :] = mix[i][3:4, :]
        o_ref[K_FIN:K_FIN + 1, :] = fin_ref[0:1, :]

    return pl.pallas_call(body, name="small_pack", out_shape=jax.ShapeDtypeStruct((K_ROWS, D_MODEL), f32))(
        *sm_ffn, *sm_mix, sm_fin, table, g_mix, g_ffn, lam)


def _small_sum(gathered):
    def body(g_ref, o_ref):
        tot = g_ref[0]
        for src in range(1, N_DEV):
            tot = tot + g_ref[src]
        o_ref[...] = tot

    return pl.pallas_call(body, name="small_sum", out_shape=jax.ShapeDtypeStruct(gathered.shape[1:], f32))(gathered)


def _adamw_math(g, w, m, v):
    m = ADAM_B1 * m + (1.0 - ADAM_B1) * g
    v = ADAM_B2 * v + (1.0 - ADAM_B2) * (g * g)
    m_hat = m / (1.0 - ADAM_B1 ** ADAM_STEP)
    v_hat = v / (1.0 - ADAM_B2 ** ADAM_STEP)
    delta = -ADAM_LR * (m_hat / (jnp.sqrt(v_hat) + ADAM_EPS) + ADAM_WD * w)
    return delta, m, v


def _adamw_small(name, g, w, m, v):
    shape = w.shape
    two_d = (1, shape[0]) if len(shape) == 1 else (math.prod(shape[:-1]), shape[-1])

    def body(g_ref, w_ref, m_ref, v_ref, d_ref, mo_ref, vo_ref):
        d_ref[...], mo_ref[...], vo_ref[...] = _adamw_math(g_ref[...], w_ref[...], m_ref[...], v_ref[...])

    outs = pl.pallas_call(body, name=f"adamw_{name}", out_shape=tuple(jax.ShapeDtypeStruct(two_d, f32) for _ in range(3)))(
        *(t.reshape(two_d) for t in (g, w, m, v)))
    return tuple(t.reshape(shape) for t in outs)


def _block_rows(rows, cols):
    tr = max(SUBLANES, min(rows, (512 * 1024) // (4 * cols)))
    while rows % tr:
        tr //= 2
    return tr


def _adamw_reduce(name, landing, kind, w, m, v):
    rows, cols = w.shape
    tr = _block_rows(rows, cols)

    def body(l_ref, w_ref, m_ref, v_ref, g_ref, d_ref, mo_ref, vo_ref):
        g = l_ref[0].astype(f32)
        for src in range(1, N_DEV):
            g = g + l_ref[src].astype(f32)
        g_ref[...] = g
        d_ref[...], mo_ref[...], vo_ref[...] = _adamw_math(g, w_ref[...], m_ref[...], v_ref[...])

    blk = pl.BlockSpec((tr, cols), lambda i: (i, 0))
    return pl.pallas_call(
        body, name=f"adamw_{name}", grid=(rows // tr,),
        out_shape=tuple(jax.ShapeDtypeStruct((rows, cols), f32) for _ in range(4)),
        in_specs=[pl.BlockSpec((N_DEV, None, tr, cols), lambda i: (0, kind, i, 0)), blk, blk, blk],
        out_specs=(blk, blk, blk, blk),
        compiler_params=_params(("arbitrary",), 32),
    )(landing, w, m, v)


def _adamw_w_mod(c_all, dmod_all, w, m, v):
    depth, d, cols = w.shape
    tr = 256

    def body(c_ref, dm_ref, w_ref, m_ref, v_ref, g_ref, d_ref, mo_ref, vo_ref):
        cv = c_ref[...]
        cond = cv * _sigmoid(cv)
        g = lax.dot_general(cond, dm_ref[...], (((0,), (0,)), ((), ())), preferred_element_type=f32,
                            precision=lax.Precision.HIGHEST)
        g_ref[...] = g
        d_ref[...], mo_ref[...], vo_ref[...] = _adamw_math(g, w_ref[...], m_ref[...], v_ref[...])

    blk = pl.BlockSpec((None, tr, cols), lambda i, r: (i, r, 0))
    return pl.pallas_call(
        body, name="adamw_w_mod", grid=(depth, d // tr),
        out_shape=tuple(jax.ShapeDtypeStruct(w.shape, f32) for _ in range(4)),
        in_specs=[pl.BlockSpec((N_DEV, tr), lambda i, r: (0, r)),
                  pl.BlockSpec((None, N_DEV, cols), lambda i, r: (i, 0, 0)), blk, blk, blk],
        out_specs=(blk, blk, blk, blk),
        compiler_params=_params(("arbitrary", "arbitrary"), 32),
    )(c_all, dmod_all, w, m, v)


def kernel(x, c, w_mod, b_mod, norm_mix_g, norm_ffn_g, lru_w_y, lru_b_y, lru_w_in, lru_b_in, lru_conv_w, lru_conv_b, lru_w_a, lru_b_a, lru_w_x, lru_b_x, lru_lambda, lru_w_out, lru_b_out, pool_w, pool_scale, ffn_w1, ffn_w2, final_norm_g, loss_target, m_w_mod, m_b_mod, m_norm_mix_g, m_norm_ffn_g, m_lru_w_y, m_lru_b_y, m_lru_w_in, m_lru_b_in, m_lru_conv_w, m_lru_conv_b, m_lru_w_a, m_lru_b_a, m_lru_w_x, m_lru_b_x, m_lru_lambda, m_lru_w_out, m_lru_b_out, m_pool_w, m_pool_scale, m_ffn_w1, m_ffn_w2, m_final_norm_g, v_w_mod, v_b_mod, v_norm_mix_g, v_norm_ffn_g, v_lru_w_y, v_lru_b_y, v_lru_w_in, v_lru_b_in, v_lru_conv_w, v_lru_conv_b, v_lru_w_a, v_lru_b_a, v_lru_w_x, v_lru_b_x, v_lru_lambda, v_lru_w_out, v_lru_b_out, v_pool_w, v_pool_scale, v_ffn_w1, v_ffn_w2, v_final_norm_g):
    me = 4 * lax.axis_index("x") + 2 * lax.axis_index("y") + lax.axis_index("c")
    n_lru = lru_w_y.shape[0]
    shard = LRU_WIDTH // N_DEV
    hshard = HEAD_DIM // N_DEV
    xs = x[0]
    target = loss_target[0]

    lru_big = jnp.stack([lru_w_y, lru_w_in, lru_w_out]).astype(bf16)
    small_mats = jnp.stack([lru_w_a, lru_w_x, pool_w]).astype(bf16)
    small_vecs = jnp.concatenate([
        lru_conv_w.reshape(n_lru * 4, shard), lru_b_a.reshape(n_lru, HEADS * hshard),
        lru_b_x.reshape(n_lru, HEADS * hshard), pool_scale, jnp.zeros((2, shard), f32)], axis=0)
    w1g, w2g, lru_g, sm_g, sv_g, c_g = _exchange(
        [ffn_w1.astype(bf16), ffn_w2.astype(bf16), lru_big, small_mats, small_vecs, c], True, "gather_weights")
    wbig = lru_g.transpose(1, 2, 0, 3, 4).reshape(3, n_lru, LRU_WIDTH, LRU_WIDTH)
    wsm = sm_g.transpose(1, 2, 3, 0, 4, 5).reshape(3, n_lru, HEADS, HEAD_DIM, HEAD_DIM)
    conv_w_full = sv_g[:, 0:8].reshape(N_DEV, n_lru, 4, shard).transpose(1, 2, 0, 3).reshape(n_lru, 4, LRU_WIDTH)
    b_a_full = sv_g[:, 8:10].reshape(N_DEV, n_lru, HEADS, hshard).transpose(1, 2, 0, 3).reshape(n_lru, LRU_WIDTH)
    b_x_full = sv_g[:, 10:12].reshape(N_DEV, n_lru, HEADS, hshard).transpose(1, 2, 0, 3).reshape(n_lru, LRU_WIDTH)
    ps_full = sv_g[:, 12:14].transpose(1, 0, 2).reshape(n_lru, D_MODEL)
    c_all = c_g.reshape(N_DEV, D_MODEL)

    (mod_g,) = _exchange([_mod_part(c_all, w_mod)], True, "gather_mod")
    mod_row = lax.dynamic_index_in_dim(mod_g, me, axis=2, keepdims=False)
    mod_row = mod_row.transpose(1, 0, 2).reshape(DEPTH, N_MOD * D_MODEL)
    table = _mod_table(mod_row, b_mod, norm_mix_g, norm_ffn_g)

    zero_row = jnp.zeros((1, LRU_WIDTH), f32)
    pvecs = [jnp.concatenate([lru_b_y[j:j + 1], lru_b_in[j:j + 1], lru_conv_b[j:j + 1], b_a_full[j:j + 1],
                              b_x_full[j:j + 1], lru_lambda[j:j + 1], lru_b_out[j:j + 1], zero_row,
                              conv_w_full[j], zero_row, zero_row, zero_row, zero_row], axis=0) for j in range(n_lru)]
    ps_rows = [jnp.concatenate([ps_full[j:j + 1], jnp.zeros((7, D_MODEL), f32)], axis=0) for j in range(n_lru)]

    saved = []
    h = xs
    for i in range(DEPTH):
        j = i // 2
        if i % 2 == 0:
            h_mid, gpre, xr, hs, y_mix = _lru_fwd(h, table[i], wbig[:, j], wsm[0:2, j], pvecs[j], i)
            mix_saved = (h, gpre, xr, hs, y_mix)
        else:
            h_mid, y_mix = _pool_fwd(h, table[i], wsm[2, j], ps_rows[j], i)
            mix_saved = (h, y_mix)
        h_out, u, y_ffn, hb = _ffn_fwd(h_mid, table[i], w1g, w2g, i)
        saved.append((mix_saved, (h_mid, u, y_ffn, hb)))
        h = h_out
    fin_rows = jnp.concatenate([final_norm_g[None, :], jnp.zeros((7, D_MODEL), f32)], axis=0)
    dx, loss_part, sm_fin = _final(h, target, fin_rows)
    loss = lax.psum(loss_part[0, 0], ("x", "y", "c"))

    sm_ffn, sm_mix = [None] * DEPTH, [None] * DEPTH
    dw1, dw2 = [None] * DEPTH, [None] * DEPTH
    dlru_big, dlru_small, dpool = [None] * n_lru, [None] * n_lru, [None] * n_lru
    for i in reversed(range(DEPTH)):
        j = i // 2
        mix_saved, (h_mid, u, y_ffn, hb) = saved[i]
        dx, da, dyb, sm_ffn[i] = _ffn_bwd_act(h_mid, dx, u, y_ffn, table[i], w1g, w2g, i)
        dw1[i], dw2[i] = _ffn_bwd_weights(hb, u, da, dyb, i)
        if i % 2 == 0:
            h_in, gpre, xr, hs, y_mix = mix_saved
            dx, dlru_big[j], dlru_small[j], sm_mix[i] = _lru_bwd(
                h_in, dx, gpre, xr, hs, y_mix, table[i], wbig[:, j], wsm[0:2, j], pvecs[j], i)
        else:
            h_in, y_mix = mix_saved
            dx, dpool[j], sm = _pool_bwd(h_in, dx, y_mix, table[i], wsm[2, j], ps_rows[j], i)
            sm_mix[i] = jnp.concatenate([sm, jnp.zeros((8, D_MODEL), f32)], axis=0)
    grad_x = dx[None]

    p_w1 = jnp.stack(dw1, axis=1)
    p_w2 = jnp.stack(dw2, axis=1)
    p_lru = jnp.stack(dlru_big, axis=2)
    p_small = jnp.concatenate([jnp.stack(dlru_small, axis=2), jnp.stack(dpool, axis=1)[:, None]], axis=1)
    l_w1, l_w2, l_lru, l_small = _exchange([p_w1, p_w2, p_lru, p_small], False, "exchange_grads")

    def reduce_update(name, landing, kind, w, m, v):
        rows = w.size // w.shape[-1]
        two_d = (rows, w.shape[-1])
        land = landing.reshape(N_DEV, -1, rows, w.shape[-1])
        outs = _adamw_reduce(name, land, kind, w.reshape(two_d), m.reshape(two_d), v.reshape(two_d))
        return tuple(t.reshape(w.shape) for t in outs)

    big = {
        "ffn_w1": reduce_update("ffn_w1", l_w1, 0, ffn_w1, m_ffn_w1, v_ffn_w1),
        "ffn_w2": reduce_update("ffn_w2", l_w2, 0, ffn_w2, m_ffn_w2, v_ffn_w2),
        "lru_w_y": reduce_update("lru_w_y", l_lru, 0, lru_w_y, m_lru_w_y, v_lru_w_y),
        "lru_w_in": reduce_update("lru_w_in", l_lru, 1, lru_w_in, m_lru_w_in, v_lru_w_in),
        "lru_w_out": reduce_update("lru_w_out", l_lru, 2, lru_w_out, m_lru_w_out, v_lru_w_out),
        "lru_w_a": reduce_update("lru_w_a", l_small, 0, lru_w_a, m_lru_w_a, v_lru_w_a),
        "lru_w_x": reduce_update("lru_w_x", l_small, 1, lru_w_x, m_lru_w_x, v_lru_w_x),
        "pool_w": reduce_update("pool_w", l_small, 2, pool_w, m_pool_w, v_pool_w),
    }

    pack = _small_pack(sm_ffn, sm_mix, sm_fin, table, norm_mix_g, norm_ffn_g, lru_lambda)
    (pack_g,) = _exchange([pack], True, "gather_small_grads")
    tot = _small_sum(pack_g)
    cols = w_mod.shape[2]
    dmod_all = lax.dynamic_slice_in_dim(pack_g[:, K_MOD:K_MOD + DEPTH * N_MOD].reshape(N_DEV, DEPTH, N_MOD * D_MODEL),
                                        me * cols, cols, axis=2).transpose(1, 0, 2)
    big["w_mod"] = _adamw_w_mod(c_all, dmod_all, w_mod, m_w_mod, v_w_mod)

    def my_cols(full, width):
        return lax.dynamic_slice_in_dim(full, me * width, width, axis=full.ndim - 1)

    small_grads = {
        "b_mod": tot[K_MOD:K_MOD + DEPTH * N_MOD].reshape(DEPTH, N_MOD * D_MODEL),
        "norm_mix_g": tot[K_NMIX:K_NMIX + DEPTH],
        "norm_ffn_g": tot[K_NFFN:K_NFFN + DEPTH],
        "lru_b_y": tot[K_LRUB:K_LRUB + 5 * n_lru].reshape(n_lru, 5, LRU_WIDTH)[:, 0],
        "lru_b_in": tot[K_LRUB:K_LRUB + 5 * n_lru].reshape(n_lru, 5, LRU_WIDTH)[:, 1],
        "lru_conv_b": tot[K_LRUB:K_LRUB + 5 * n_lru].reshape(n_lru, 5, LRU_WIDTH)[:, 2],
        "lru_lambda": tot[K_LRUB:K_LRUB + 5 * n_lru].reshape(n_lru, 5, LRU_WIDTH)[:, 3],
        "lru_b_out": tot[K_LRUB:K_LRUB + 5 * n_lru].reshape(n_lru, 5, LRU_WIDTH)[:, 4],
        "lru_conv_w": my_cols(tot[K_CONVW:K_CONVW + 4 * n_lru].reshape(n_lru, 4, LRU_WIDTH), shard),
        "lru_b_a": my_cols(tot[K_BA:K_BA + n_lru].reshape(n_lru, HEADS, HEAD_DIM), hshard),
        "lru_b_x": my_cols(tot[K_BX:K_BX + n_lru].reshape(n_lru, HEADS, HEAD_DIM), hshard),
        "pool_scale": my_cols(tot[K_PS:K_PS + n_lru], shard),
        "final_norm_g": tot[K_FIN],
    }
    given = dict(b_mod=(b_mod, m_b_mod, v_b_mod), norm_mix_g=(norm_mix_g, m_norm_mix_g, v_norm_mix_g),
                 norm_ffn_g=(norm_ffn_g, m_norm_ffn_g, v_norm_ffn_g), lru_b_y=(lru_b_y, m_lru_b_y, v_lru_b_y),
                 lru_b_in=(lru_b_in, m_lru_b_in, v_lru_b_in), lru_conv_w=(lru_conv_w, m_lru_conv_w, v_lru_conv_w),
                 lru_conv_b=(lru_conv_b, m_lru_conv_b, v_lru_conv_b), lru_b_a=(lru_b_a, m_lru_b_a, v_lru_b_a),
                 lru_b_x=(lru_b_x, m_lru_b_x, v_lru_b_x), lru_lambda=(lru_lambda, m_lru_lambda, v_lru_lambda),
                 lru_b_out=(lru_b_out, m_lru_b_out, v_lru_b_out), pool_scale=(pool_scale, m_pool_scale, v_pool_scale),
                 final_norm_g=(final_norm_g, m_final_norm_g, v_final_norm_g))
    results = dict(big)
    for name, g in small_grads.items():
        results[name] = (g,) + _adamw_small(name, g, *given[name])

    order = ["w_mod", "b_mod", "norm_mix_g", "norm_ffn_g", "lru_w_y", "lru_b_y", "lru_w_in", "lru_b_in", "lru_conv_w",
             "lru_conv_b", "lru_w_a", "lru_b_a", "lru_w_x", "lru_b_x", "lru_lambda", "lru_w_out", "lru_b_out", "pool_w",
             "pool_scale", "ffn_w1", "ffn_w2", "final_norm_g"]
    return (loss, grad_x, *[results[n][0] for n in order], *[results[n][1] for n in order],
            *[results[n][2] for n in order], *[results[n][3] for n in order])
```

```python
import functools
import math

import jax
import jax.numpy as jnp
from jax import lax
from jax.experimental import pallas as pl
from jax.experimental.pallas import tpu as pltpu

f32, bf16 = jnp.float32, jnp.bfloat16

D_MODEL = 1024
LRU_WIDTH = 1024
HEADS = 4
HEAD_DIM = 256
D_FF = 4096
DEPTH = 4
N_MOD = 6
N_DEV = 8
FF_CHUNK = D_FF // N_DEV
POOL_WINDOWS = (2, 4, 8, 16)
POOL_HALO = 16
EPS = 1e-6
LRU_C = 8.0

ADAM_LR = 0.001
ADAM_B1 = 0.9
ADAM_B2 = 0.999
ADAM_EPS = 1e-08
ADAM_WD = 0.01
ADAM_STEP = 10

V7X_VMEM_BYTES = 64 * 1024 * 1024
SUBLANES = 8
BF16_ROWS = 16

R_SH_M, R_SC_M, R_GT_M, R_SH_F, R_SC_F, R_GT_F, R_GS_M, R_GS_F = range(8)
P_BY, P_BIN, P_CONVB, P_BA, P_BX, P_LAM, P_BOUT, P_CW0 = 0, 1, 2, 3, 4, 5, 6, 8
G_SH, G_GS, G_GT, G_BY, G_BIN, G_CONVB, G_BA, G_BX, G_LS, G_BOUT, G_CW0 = 0, 1, 2, 3, 4, 5, 6, 7, 8, 9, 10
K_MOD, K_NMIX, K_NFFN, K_LRUB, K_CONVW, K_BA, K_BX, K_PS, K_FIN, K_ROWS = 0, 24, 28, 32, 42, 50, 52, 54, 56, 64


def _params(semantics=None, vmem_mb=48):
    return pltpu.CompilerParams(dimension_semantics=semantics, vmem_limit_bytes=vmem_mb * 1024 * 1024)


def _mm(a, b):
    return jnp.dot(a, b, preferred_element_type=f32)


def _mm_nt(a, b):
    return lax.dot_general(a, b, (((1,), (1,)), ((), ())), preferred_element_type=f32)


def _mm_tn(a, b):
    return lax.dot_general(a, b, (((0,), (0,)), ((), ())), preferred_element_type=f32)


def _rms(x):
    r = lax.rsqrt(jnp.mean(x * x, axis=-1, keepdims=True) + EPS)
    return x * r, r


def _norm_bwd(dh, n, r, gs):
    dn = dh * gs
    return r * (dn - n * jnp.mean(dn * n, axis=-1, keepdims=True))


def _colsum(v):
    return jnp.sum(v, axis=0, keepdims=True)


def _sigmoid(v):
    return 1.0 / (1.0 + jnp.exp(-v))


def _log_sigmoid(v):
    return jnp.minimum(v, 0.0) - jnp.log1p(jnp.exp(-jnp.abs(v)))


_GELU_C = 0.7978845608028654
_GELU_A = 0.044715


def _gelu(v):
    return 0.5 * v * (1.0 + jnp.tanh(_GELU_C * (v + _GELU_A * v * v * v)))


def _gelu_grad(v):
    t = jnp.tanh(_GELU_C * (v + _GELU_A * v * v * v))
    return 0.5 * (1.0 + t) + 0.5 * v * (1.0 - t * t) * _GELU_C * (1.0 + 3.0 * _GELU_A * v * v)


def _neg_expm1(v):
    series = -v * (1.0 + v * (0.5 + v * (1.0 / 6 + v * (1.0 / 24 + v * (1.0 / 120 + v * (1.0 / 720))))))
    return jnp.where(v > -0.1, series, 1.0 - jnp.exp(v))


def _rows_before(halo, v, shifts):
    hr = halo.shape[0]
    ext = jnp.concatenate([halo, v], axis=0)
    return [pltpu.roll(ext, k, 0)[hr:] for k in shifts]


def _rows_after(v, halo, shifts):
    n = v.shape[0]
    ext = jnp.concatenate([v, halo], axis=0)
    return [pltpu.roll(ext, ext.shape[0] - k, 0)[:n] for k in shifts]


def _block_diag(v, w_ref, kind):
    return jnp.concatenate(
        [_mm(v[:, h * HEAD_DIM:(h + 1) * HEAD_DIM], w_ref[kind, h]) for h in range(HEADS)], axis=1)


def _block_diag_t(v, w_ref, kind):
    return jnp.concatenate(
        [_mm_nt(v[:, h * HEAD_DIM:(h + 1) * HEAD_DIM], w_ref[kind, h]) for h in range(HEADS)], axis=1)


def _scan(a_ref, u_ref, out_ref, carry, reverse):
    groups = a_ref.shape[0] // SUBLANES
    width = a_ref.shape[1]
    row = lax.broadcasted_iota(jnp.int32, (SUBLANES, width), 0)

    def step(j, c):
        g = groups - 1 - j if reverse else j
        off = pl.multiple_of(g * SUBLANES, SUBLANES)
        a = a_ref[pl.ds(off, SUBLANES), :]
        u = u_ref[pl.ds(off, SUBLANES), :]
        for k in (1, 2, 4):
            if reverse:
                valid, shift = row < SUBLANES - k, SUBLANES - k
            else:
                valid, shift = row >= k, k
            a_s = jnp.where(valid, pltpu.roll(a, shift, 0), 1.0)
            u_s = jnp.where(valid, pltpu.roll(u, shift, 0), 0.0)
            u = u + a * u_s
            a = a * a_s
        h = u + a * c
        out_ref[pl.ds(off, SUBLANES), :] = h
        last = h[0:1, :] if reverse else h[SUBLANES - 1:SUBLANES, :]
        return jnp.broadcast_to(last, (SUBLANES, width))

    return lax.fori_loop(0, groups, step, carry)


def _exchange(arrays, gather, name):
    n = len(arrays)
    peers = N_DEV - 1

    def body(*refs):
        ins, outs = refs[:n], refs[n:2 * n]
        send_sems, recv_sems, local_sems = refs[2 * n:]
        x, y, c = lax.axis_index("x"), lax.axis_index("y"), lax.axis_index("c")
        me = 4 * x + 2 * y + c
        local = []
        for k in range(n):
            cp = pltpu.make_async_copy(ins[k] if gather else ins[k].at[me], outs[k].at[me], local_sems.at[k])
            cp.start()
            local.append(cp)
        remote = []
        for p in range(1, N_DEV):
            px = 1 - x if p & 4 else x
            py = 1 - y if p & 2 else y
            pc = 1 - c if p & 1 else c
            for k in range(n):
                cp = pltpu.make_async_remote_copy(
                    src_ref=ins[k] if gather else ins[k].at[4 * px + 2 * py + pc],
                    dst_ref=outs[k].at[me],
                    send_sem=send_sems.at[k * peers + p - 1],
                    recv_sem=recv_sems.at[k * peers + p - 1],
                    device_id=(px, py, pc), device_id_type=pl.DeviceIdType.MESH)
                cp.start()
                remote.append(cp)
        for cp in remote:
            cp.wait()
        for cp in local:
            cp.wait()

    out_shape = tuple(
        jax.ShapeDtypeStruct(((N_DEV,) + a.shape) if gather else a.shape, a.dtype) for a in arrays)
    outs = pl.pallas_call(
        body, name=name, out_shape=out_shape,
        in_specs=[pl.BlockSpec(memory_space=pl.ANY)] * n,
        out_specs=tuple(pl.BlockSpec(memory_space=pl.ANY) for _ in range(n)),
        scratch_shapes=[pltpu.SemaphoreType.DMA((n * peers,)), pltpu.SemaphoreType.DMA((n * peers,)),
                        pltpu.SemaphoreType.DMA((n,))],
        compiler_params=pltpu.CompilerParams(has_side_effects=True),
    )(*arrays)
    return list(outs)


_HBM = pl.BlockSpec(memory_space=pltpu.HBM)
_SEM = pl.BlockSpec(memory_space=pltpu.SEMAPHORE)
_DATAFLOW = pltpu.SideEffectType.DATAFLOW_SIDE_EFFECTING


def _peer_copies(src_refs, land_refs, send_sems, recv_sems, gather):
    x, y, c = lax.axis_index("x"), lax.axis_index("y"), lax.axis_index("c")
    me = 4 * x + 2 * y + c
    peers = N_DEV - 1
    copies = []
    for p in range(1, N_DEV):
        px = 1 - x if p & 4 else x
        py = 1 - y if p & 2 else y
        pc = 1 - c if p & 1 else c
        for k in range(len(src_refs)):
            copies.append(pltpu.make_async_remote_copy(
                src_ref=src_refs[k] if gather else src_refs[k].at[4 * px + 2 * py + pc],
                dst_ref=land_refs[k].at[me],
                send_sem=send_sems.at[k * peers + p - 1], recv_sem=recv_sems.at[k * peers + p - 1],
                device_id=(px, py, pc), device_id_type=pl.DeviceIdType.MESH))
    return copies


def _landing(srcs, gather, me):
    out = []
    for a in srcs:
        own = a if gather else lax.dynamic_index_in_dim(a, me, 0, keepdims=False)
        out.append(lax.dynamic_update_index_in_dim(lax.empty((N_DEV,) + own.shape, own.dtype), own, me, 0))
    return out


def _send_start(name, srcs, gather, me):
    n = len(srcs)
    lands = _landing(srcs, gather, me)

    def body(*refs):
        src_refs, land_refs = refs[:n], refs[n:2 * n]
        send_sems, recv_sems, token = refs[2 * n], refs[2 * n + 1], refs[-1]
        for cp in _peer_copies(src_refs, land_refs, send_sems, recv_sems, gather):
            cp.start()
        token[...] = jnp.zeros_like(token)

    sems = pltpu.SemaphoreType.DMA((n * (N_DEV - 1),))
    outs = pl.pallas_call(
        body, name=name,
        out_shape=(sems, sems, *[pltpu.HBM(a.shape, a.dtype) for a in (*srcs, *lands)], jax.ShapeDtypeStruct((8, 128), f32)),
        in_specs=[_HBM] * (2 * n),
        out_specs=(_SEM, _SEM, *[_HBM] * (2 * n), pl.BlockSpec(memory_space=pltpu.VMEM)),
        input_output_aliases={k: 2 + k for k in range(2 * n)},
        compiler_params=pltpu.CompilerParams(has_side_effects=_DATAFLOW),
    )(*[pltpu.with_memory_space_constraint(a, pltpu.HBM) for a in (*srcs, *lands)])
    return (outs[0], outs[1], list(outs[2:2 + n]), list(outs[2 + n:2 + 2 * n]), gather), outs[-1]


def _send_wait(name, handle, after):
    send_sems, recv_sems, srcs, lands, gather = handle
    n = len(srcs)

    def body(*refs):
        src_refs, land_refs = refs[:n], refs[n:2 * n]
        for cp in _peer_copies(src_refs, land_refs, refs[2 * n], refs[2 * n + 1], gather):
            cp.wait_send()
            cp.wait_recv()

    outs = pl.pallas_call(
        body, name=name, out_shape=tuple(pltpu.HBM(a.shape, a.dtype) for a in (*srcs, *lands)),
        in_specs=[_HBM] * (2 * n) + [_SEM, _SEM, pl.BlockSpec(memory_space=pl.ANY)],
        out_specs=tuple([_HBM] * (2 * n)), input_output_aliases={k: k for k in range(2 * n)},
        compiler_params=pltpu.CompilerParams(has_side_effects=_DATAFLOW),
    )(*srcs, *lands, send_sems, recv_sems, after)
    return list(outs[n:])


def _mod_part(c_all, w_mod):
    depth, d, cols = w_mod.shape

    def body(c_ref, w_ref, o_ref):
        cv = c_ref[...]
        cond = cv * _sigmoid(cv)
        o_ref[...] = jnp.dot(cond, w_ref[...], preferred_element_type=f32, precision=lax.Precision.HIGHEST)

    return pl.pallas_call(
        body, name="mod_part", grid=(depth,),
        out_shape=jax.ShapeDtypeStruct((depth, N_DEV, cols), f32),
        in_specs=[pl.BlockSpec((N_DEV, d), lambda i: (0, 0)), pl.BlockSpec((None, d, cols), lambda i: (i, 0, 0))],
        out_specs=pl.BlockSpec((None, N_DEV, cols), lambda i: (i, 0, 0)),
        compiler_params=_params(("arbitrary",), 32),
    )(c_all, w_mod)


def _mod_table(mod_row, b_mod, g_mix, g_ffn):
    def body(m_ref, b_ref, gm_ref, gf_ref, o_ref):
        for i in range(DEPTH):
            for k in range(N_MOD):
                o_ref[i, k:k + 1, :] = m_ref[i:i + 1, k * D_MODEL:(k + 1) * D_MODEL] + b_ref[i:i + 1, k * D_MODEL:(k + 1) * D_MODEL]
            o_ref[i, R_GS_M:R_GS_M + 1, :] = gm_ref[i:i + 1, :] * (1.0 + o_ref[i, R_SC_M:R_SC_M + 1, :])
            o_ref[i, R_GS_F:R_GS_F + 1, :] = gf_ref[i:i + 1, :] * (1.0 + o_ref[i, R_SC_F:R_SC_F + 1, :])

    return pl.pallas_call(body, name="mod_table", out_shape=jax.ShapeDtypeStruct((DEPTH, 8, D_MODEL), f32))(
        mod_row, b_mod, g_mix, g_ffn)


def _ffn_tile(s):
    return min(512, s)


def _layer_weights(shape):
    return pl.BlockSpec((N_DEV,) + shape, lambda i: (0, 0, 0))


def _ffn_fwd(x, vec, w1g, w2g, layer):
    s = x.shape[0]
    ts = _ffn_tile(s)

    def body(x_ref, vec_ref, w1_ref, w2_ref, xo_ref, u_ref, y_ref, hb_ref):
        xv = x_ref[...]
        n, _ = _rms(xv)
        hb = (n * vec_ref[R_GS_F:R_GS_F + 1, :] + vec_ref[R_SH_F:R_SH_F + 1, :]).astype(bf16)
        hb_ref[...] = hb
        yv = jnp.zeros((ts, D_MODEL), f32)
        for f in range(N_DEV):
            u = jnp.maximum(_mm(hb, w1_ref[f]), 0.0)
            u_ref[:, f * FF_CHUNK:(f + 1) * FF_CHUNK] = u.astype(bf16)
            yv = yv + _mm((u * u).astype(bf16), w2_ref[f])
        y_ref[...] = yv.astype(bf16)
        xo_ref[...] = xv + vec_ref[R_GT_F:R_GT_F + 1, :] * yv

    row = pl.BlockSpec((ts, D_MODEL), lambda i: (i, 0))
    return pl.pallas_call(
        body, name=f"ffn_fwd_{layer}", grid=(s // ts,),
        out_shape=(jax.ShapeDtypeStruct((s, D_MODEL), f32), jax.ShapeDtypeStruct((s, D_FF), bf16),
                   jax.ShapeDtypeStruct((s, D_MODEL), bf16), jax.ShapeDtypeStruct((s, D_MODEL), bf16)),
        in_specs=[row, pl.BlockSpec((8, D_MODEL), lambda i: (0, 0)),
                  _layer_weights((D_MODEL, FF_CHUNK)), _layer_weights((FF_CHUNK, D_MODEL))],
        out_specs=(row, pl.BlockSpec((ts, D_FF), lambda i: (i, 0)), row, row),
        compiler_params=_params(("arbitrary",), 56),
    )(x, vec, w1g, w2g)


def _ffn_bwd_act(x, dx, u, y, vec, w1g, w2g, layer):
    s = x.shape[0]
    ts = _ffn_tile(s)

    def body(x_ref, dx_ref, u_ref, y_ref, vec_ref, w1_ref, w2_ref, dxo_ref, da_ref, dyb_ref, sm_ref):
        @pl.when(pl.program_id(0) == 0)
        def _():
            sm_ref[...] = jnp.zeros_like(sm_ref)

        dxv = dx_ref[...]
        dyb = (dxv * vec_ref[R_GT_F:R_GT_F + 1, :]).astype(bf16)
        dyb_ref[...] = dyb
        sm_ref[G_GT:G_GT + 1, :] += _colsum(dxv * y_ref[...].astype(f32))
        dh = jnp.zeros((ts, D_MODEL), f32)
        for f in range(N_DEV):
            cols = slice(f * FF_CHUNK, (f + 1) * FF_CHUNK)
            dz = _mm_nt(dyb, w2_ref[f])
            dab = (dz * (2.0 * u_ref[:, cols].astype(f32))).astype(bf16)
            da_ref[:, cols] = dab
            dh = dh + _mm_nt(dab, w1_ref[f])
        n, r = _rms(x_ref[...])
        sm_ref[G_SH:G_SH + 1, :] += _colsum(dh)
        sm_ref[G_GS:G_GS + 1, :] += _colsum(dh * n)
        dxo_ref[...] = dxv + _norm_bwd(dh, n, r, vec_ref[R_GS_F:R_GS_F + 1, :])

    row = pl.BlockSpec((ts, D_MODEL), lambda i: (i, 0))
    wide = pl.BlockSpec((ts, D_FF), lambda i: (i, 0))
    return pl.pallas_call(
        body, name=f"ffn_bwd_act_{layer}", grid=(s // ts,),
        out_shape=(jax.ShapeDtypeStruct((s, D_MODEL), f32), jax.ShapeDtypeStruct((s, D_FF), bf16),
                   jax.ShapeDtypeStruct((s, D_MODEL), bf16), jax.ShapeDtypeStruct((8, D_MODEL), f32)),
        in_specs=[row, row, wide, row, pl.BlockSpec((8, D_MODEL), lambda i: (0, 0)),
                  _layer_weights((D_MODEL, FF_CHUNK)), _layer_weights((FF_CHUNK, D_MODEL))],
        out_specs=(row, wide, row, pl.BlockSpec((8, D_MODEL), lambda i: (0, 0))),
        compiler_params=_params(("arbitrary",), 58),
    )(x, dx, u, y, vec, w1g, w2g)


def _ffn_bwd_w1(hb, da, layer):
    s = hb.shape[0]
    ts = _ffn_tile(s)
    nt = s // ts

    def body(hb_ref, da_ref, dw_ref, acc_ref):
        i = pl.program_id(0)

        @pl.when(i == 0)
        def _():
            acc_ref[...] = jnp.zeros_like(acc_ref)

        hb = hb_ref[...]
        for f in range(N_DEV):
            acc_ref[f] += _mm_tn(hb, da_ref[:, f * FF_CHUNK:(f + 1) * FF_CHUNK])

        @pl.when(i == nt - 1)
        def _():
            dw_ref[...] = acc_ref[...].astype(bf16)

    return pl.pallas_call(
        body, name=f"ffn_bwd_w1_{layer}", grid=(nt,),
        out_shape=jax.ShapeDtypeStruct((N_DEV, D_MODEL, FF_CHUNK), bf16),
        in_specs=[pl.BlockSpec((ts, D_MODEL), lambda i: (i, 0)), pl.BlockSpec((ts, D_FF), lambda i: (i, 0))],
        out_specs=pl.BlockSpec((N_DEV, D_MODEL, FF_CHUNK), lambda i: (0, 0, 0)),
        scratch_shapes=[pltpu.VMEM((N_DEV, D_MODEL, FF_CHUNK), f32)],
        compiler_params=_params(("arbitrary",), 56),
    )(hb, da)


def _ffn_bwd_w2(u, dyb, layer):
    s = u.shape[0]
    ts = _ffn_tile(s)
    nt = s // ts

    def body(u_ref, dyb_ref, dw_ref, acc_ref):
        i = pl.program_id(0)

        @pl.when(i == 0)
        def _():
            acc_ref[...] = jnp.zeros_like(acc_ref)

        dyb = dyb_ref[...]
        for f in range(N_DEV):
            uv = u_ref[:, f * FF_CHUNK:(f + 1) * FF_CHUNK].astype(f32)
            acc_ref[f] += _mm_tn((uv * uv).astype(bf16), dyb)

        @pl.when(i == nt - 1)
        def _():
            dw_ref[...] = acc_ref[...].astype(bf16)

    return pl.pallas_call(
        body, name=f"ffn_bwd_w2_{layer}", grid=(nt,),
        out_shape=jax.ShapeDtypeStruct((N_DEV, FF_CHUNK, D_MODEL), bf16),
        in_specs=[pl.BlockSpec((ts, D_FF), lambda i: (i, 0)), pl.BlockSpec((ts, D_MODEL), lambda i: (i, 0))],
        out_specs=pl.BlockSpec((N_DEV, FF_CHUNK, D_MODEL), lambda i: (0, 0, 0)),
        scratch_shapes=[pltpu.VMEM((N_DEV, FF_CHUNK, D_MODEL), f32)],
        compiler_params=_params(("arbitrary",), 56),
    )(u, dyb)


def _lru_tile(s):
    return min(256, s)


def _lru_gates(xc, wsm_ref, pv_ref):
    xcb = xc.astype(bf16)
    gr = _sigmoid(_block_diag(xcb, wsm_ref, 0) + pv_ref[P_BA:P_BA + 1, :])
    gi = _sigmoid(_block_diag(xcb, wsm_ref, 1) + pv_ref[P_BX:P_BX + 1, :])
    ls = _log_sigmoid(pv_ref[P_LAM:P_LAM + 1, :])
    log_a = LRU_C * gr * ls
    return xcb, gr, gi, jnp.exp(log_a), jnp.sqrt(_neg_expm1(2.0 * log_a)), ls


def _conv(xr, taps_before, pv_ref):
    xc = xr * pv_ref[P_CW0 + 3:P_CW0 + 4, :] + pv_ref[P_CONVB:P_CONVB + 1, :]
    for k, v in zip((2, 1, 0), taps_before):
        xc = xc + v * pv_ref[P_CW0 + k:P_CW0 + k + 1, :]
    return xc


def _lru_fwd(x, vec, wbig, wsm, pvec, layer):
    s = x.shape[0]
    ts = _lru_tile(s)
    w = LRU_WIDTH

    def body(x_ref, vec_ref, wb_ref, wsm_ref, pv_ref, xo_ref, gpre_ref, xr_ref, hs_ref, y_ref,
             tail_ref, carry_ref, a_scr, u_scr):
        @pl.when(pl.program_id(0) == 0)
        def _():
            tail_ref[...] = jnp.zeros_like(tail_ref)
            carry_ref[...] = jnp.zeros_like(carry_ref)

        xv = x_ref[...]
        n, _ = _rms(xv)
        hb = (n * vec_ref[R_GS_M:R_GS_M + 1, :] + vec_ref[R_SH_M:R_SH_M + 1, :]).astype(bf16)
        gpre = _mm(hb, wb_ref[0]) + pv_ref[P_BY:P_BY + 1, :]
        xr = _mm(hb, wb_ref[1]) + pv_ref[P_BIN:P_BIN + 1, :]
        gpre_ref[...] = gpre.astype(bf16)
        xr_ref[...] = xr.astype(bf16)
        xc = _conv(xr, _rows_before(tail_ref[...], xr, (1, 2, 3)), pv_ref)
        tail_ref[...] = xr[ts - SUBLANES:, :]
        _, _, gi, a, mult, _ = _lru_gates(xc, wsm_ref, pv_ref)
        a_scr[...] = a
        u_scr[...] = mult * (gi * xc)
        carry_ref[...] = _scan(a_scr, u_scr, hs_ref, carry_ref[...], reverse=False)
        m = hs_ref[...] * _gelu(gpre)
        yv = _mm(m.astype(bf16), wb_ref[2]) + pv_ref[P_BOUT:P_BOUT + 1, :]
        y_ref[...] = yv.astype(bf16)
        xo_ref[...] = xv + vec_ref[R_GT_M:R_GT_M + 1, :] * yv

    row = pl.BlockSpec((ts, D_MODEL), lambda i: (i, 0))
    roww = pl.BlockSpec((ts, w), lambda i: (i, 0))
    return pl.pallas_call(
        body, name=f"lru_fwd_{layer}", grid=(s // ts,),
        out_shape=(jax.ShapeDtypeStruct((s, D_MODEL), f32), jax.ShapeDtypeStruct((s, w), bf16),
                   jax.ShapeDtypeStruct((s, w), bf16), jax.ShapeDtypeStruct((s, w), f32),
                   jax.ShapeDtypeStruct((s, D_MODEL), bf16)),
        in_specs=[row, pl.BlockSpec((8, D_MODEL), lambda i: (0, 0)),
                  pl.BlockSpec((3, w, w), lambda i: (0, 0, 0)),
                  pl.BlockSpec((2, HEADS, HEAD_DIM, HEAD_DIM), lambda i: (0, 0, 0, 0)),
                  pl.BlockSpec((16, w), lambda i: (0, 0))],
        out_specs=(row, roww, roww, roww, row),
        scratch_shapes=[pltpu.VMEM((SUBLANES, w), f32), pltpu.VMEM((SUBLANES, w), f32),
                        pltpu.VMEM((ts, w), f32), pltpu.VMEM((ts, w), f32)],
        compiler_params=_params(("arbitrary",)),
    )(x, vec, wbig, wsm, pvec)


def _lru_bwd(x, dx, gpre, xr, hs, y, vec, wbig, wsm, pvec, layer):
    s = x.shape[0]
    ts = min(128, s)
    nt = s // ts
    w = LRU_WIDTH
    shard = w // N_DEV
    hshard = HEAD_DIM // N_DEV

    def body(x_ref, dx_ref, gpre_ref, xr_ref, xrh_ref, hs_ref, hsh_ref, y_ref, vec_ref, wb_ref, wsm_ref, pv_ref,
             dxo_ref, dwb_ref, dwsm_ref, sm_ref,
             accb_ref, accs_ref, eps8_ref, dxc8_ref, a_scr, u_scr, e_scr):
        i = pl.program_id(0)
        first_tile = i == nt - 1

        @pl.when(i == 0)
        def _():
            accb_ref[...] = jnp.zeros_like(accb_ref)
            accs_ref[...] = jnp.zeros_like(accs_ref)
            sm_ref[...] = jnp.zeros_like(sm_ref)
            eps8_ref[...] = jnp.zeros_like(eps8_ref)
            dxc8_ref[...] = jnp.zeros_like(dxc8_ref)

        gs = vec_ref[R_GS_M:R_GS_M + 1, :]
        xv = x_ref[...]
        dxv = dx_ref[...]
        n, r = _rms(xv)
        hb = (n * gs + vec_ref[R_SH_M:R_SH_M + 1, :]).astype(bf16)
        gpre_v = gpre_ref[...].astype(f32)
        xrv = xr_ref[...].astype(f32)
        hsv = hs_ref[...]
        xr_halo = jnp.where(first_tile, 0.0, xrh_ref[...].astype(f32))
        hs_halo = jnp.where(first_tile, 0.0, hsh_ref[...])
        xs1, xs2, xs3 = _rows_before(xr_halo, xrv, (1, 2, 3))
        xc = _conv(xrv, (xs1, xs2, xs3), pv_ref)
        xcb, gr, gi, a, mult, ls = _lru_gates(xc, wsm_ref, pv_ref)
        gelu_v = _gelu(gpre_v)

        dy = dxv * vec_ref[R_GT_M:R_GT_M + 1, :]
        dyb = dy.astype(bf16)
        sm_ref[G_GT:G_GT + 1, :] += _colsum(dxv * y_ref[...].astype(f32))
        sm_ref[G_BOUT:G_BOUT + 1, :] += _colsum(dy)
        accb_ref[2] += _mm_tn((hsv * gelu_v).astype(bf16), dyb)
        dm = _mm_nt(dyb, wb_ref[2])
        dhs = dm * gelu_v
        dgpre = dm * hsv * _gelu_grad(gpre_v)

        a_scr[...] = a
        u_scr[...] = a * dhs
        _scan(a_scr, u_scr, e_scr, jnp.broadcast_to(eps8_ref[0:1, :], (SUBLANES, w)), reverse=True)
        (eps_next,) = _rows_after(e_scr[...], eps8_ref[...], (1,))
        eps8_ref[...] = e_scr[0:SUBLANES, :]
        delta = dhs + eps_next
        (h_prev,) = _rows_before(hs_halo, hsv, (1,))
        da = delta * h_prev
        dgi = delta * mult * xc
        dxc = delta * mult * gi
        dla = da * a - (delta * gi * xc) * (a * a) / mult
        sm_ref[G_LS:G_LS + 1, :] += _colsum(dla * (LRU_C * gr))
        dra = dla * (LRU_C * ls) * gr * (1.0 - gr)
        drx = dgi * gi * (1.0 - gi)
        drab, drxb = dra.astype(bf16), drx.astype(bf16)
        sm_ref[G_BA:G_BA + 1, :] += _colsum(dra)
        sm_ref[G_BX:G_BX + 1, :] += _colsum(drx)
        for h in range(HEADS):
            cols = slice(h * HEAD_DIM, (h + 1) * HEAD_DIM)
            accs_ref[0, h] += _mm_tn(xcb[:, cols], drab[:, cols])
            accs_ref[1, h] += _mm_tn(xcb[:, cols], drxb[:, cols])
        dxc = dxc + _block_diag_t(drab, wsm_ref, 0) + _block_diag_t(drxb, wsm_ref, 1)

        sm_ref[G_CONVB:G_CONVB + 1, :] += _colsum(dxc)
        for k, v in zip((3, 2, 1, 0), (xrv, xs1, xs2, xs3)):
            sm_ref[G_CW0 + k:G_CW0 + k + 1, :] += _colsum(dxc * v)
        ups = _rows_after(dxc, dxc8_ref[...], (1, 2, 3))
        dxc8_ref[...] = dxc[0:SUBLANES, :]
        dxr = dxc * pv_ref[P_CW0 + 3:P_CW0 + 4, :]
        for k, v in zip((2, 1, 0), ups):
            dxr = dxr + v * pv_ref[P_CW0 + k:P_CW0 + k + 1, :]

        dgb, dxrb = dgpre.astype(bf16), dxr.astype(bf16)
        sm_ref[G_BY:G_BY + 1, :] += _colsum(dgpre)
        sm_ref[G_BIN:G_BIN + 1, :] += _colsum(dxr)
        accb_ref[0] += _mm_tn(hb, dgb)
        accb_ref[1] += _mm_tn(hb, dxrb)
        dh = _mm_nt(dgb, wb_ref[0]) + _mm_nt(dxrb, wb_ref[1])
        sm_ref[G_SH:G_SH + 1, :] += _colsum(dh)
        sm_ref[G_GS:G_GS + 1, :] += _colsum(dh * n)
        dxo_ref[...] = dxv + _norm_bwd(dh, n, r, gs)

        @pl.when(i == nt - 1)
        def _():
            for k in range(3):
                dwb_ref[:, k] = accb_ref[k].astype(bf16).reshape(N_DEV, shard, w)
            for k in range(2):
                for h in range(HEADS):
                    dwsm_ref[:, k, h] = accs_ref[k, h].astype(bf16).reshape(N_DEV, hshard, HEAD_DIM)

    rev = lambda i: (nt - 1 - i, 0)
    row = pl.BlockSpec((ts, D_MODEL), rev)
    roww = pl.BlockSpec((ts, w), rev)
    halo16 = pl.BlockSpec((BF16_ROWS, w), lambda i: (jnp.maximum((nt - 1 - i) * (ts // BF16_ROWS) - 1, 0), 0))
    halo8 = pl.BlockSpec((SUBLANES, w), lambda i: (jnp.maximum((nt - 1 - i) * (ts // SUBLANES) - 1, 0), 0))
    const = lambda *shape: pl.BlockSpec(shape, lambda i: (0,) * len(shape), pipeline_mode=pl.Buffered(1))
    return pl.pallas_call(
        body, name=f"lru_bwd_{layer}", grid=(nt,),
        out_shape=(jax.ShapeDtypeStruct((s, D_MODEL), f32),
                   jax.ShapeDtypeStruct((N_DEV, 3, shard, w), bf16),
                   jax.ShapeDtypeStruct((N_DEV, 2, HEADS, hshard, HEAD_DIM), bf16),
                   jax.ShapeDtypeStruct((16, w), f32)),
        in_specs=[row, row, roww, roww, halo16, roww, halo8, row, const(8, D_MODEL), const(3, w, w),
                  const(2, HEADS, HEAD_DIM, HEAD_DIM), const(16, w)],
        out_specs=(row, const(N_DEV, 3, shard, w), const(N_DEV, 2, HEADS, hshard, HEAD_DIM), const(16, w)),
        scratch_shapes=[pltpu.VMEM((3, w, w), f32), pltpu.VMEM((2, HEADS, HEAD_DIM, HEAD_DIM), f32),
                        pltpu.VMEM((SUBLANES, w), f32), pltpu.VMEM((SUBLANES, w), f32),
                        pltpu.VMEM((ts, w), f32), pltpu.VMEM((ts, w), f32), pltpu.VMEM((ts, w), f32)],
        compiler_params=_params(("arbitrary",), 56),
    )(x, dx, gpre, xr, xr, hs, hs, y, vec, wbig, wsm, pvec)


def _pool_tile(s):
    return min(256, s)


def _pool_counts(tile_index, ts):
    t = (tile_index * ts + lax.broadcasted_iota(jnp.int32, (ts, 1), 0) + 1).astype(f32)
    return [1.0 / jnp.minimum(t, float(win)) for win in POOL_WINDOWS]


def _pooled(h, halo, inv):
    ext = jnp.concatenate([halo, h], axis=0)
    out = []
    for g in range(len(POOL_WINDOWS)):
        acc = ext[:, g * HEAD_DIM:(g + 1) * HEAD_DIM]
        for step in range(g + 1):
            acc = acc + pltpu.roll(acc, 1 << step, 0)
        out.append(acc[POOL_HALO:] * inv[g] - h[:, g * HEAD_DIM:(g + 1) * HEAD_DIM])
    return out


def _pool_fwd(x, vec, pw, ps, layer):
    s = x.shape[0]
    ts = _pool_tile(s)

    def body(x_ref, vec_ref, pw_ref, ps_ref, xo_ref, y_ref, halo_ref):
        i = pl.program_id(0)

        @pl.when(i == 0)
        def _():
            halo_ref[...] = jnp.zeros_like(halo_ref)

        xv = x_ref[...]
        n, _ = _rms(xv)
        h = n * vec_ref[R_GS_M:R_GS_M + 1, :] + vec_ref[R_SH_M:R_SH_M + 1, :]
        pooled = _pooled(h, halo_ref[...], _pool_counts(i, ts))
        halo_ref[...] = h[ts - POOL_HALO:, :]
        mixed = jnp.concatenate([_mm(pooled[g].astype(bf16), pw_ref[g]) for g in range(HEADS)], axis=1)
        yv = mixed * ps_ref[0:1, :]
        y_ref[...] = yv.astype(bf16)
        xo_ref[...] = xv + vec_ref[R_GT_M:R_GT_M + 1, :] * yv

    row = pl.BlockSpec((ts, D_MODEL), lambda i: (i, 0))
    return pl.pallas_call(
        body, name=f"pool_fwd_{layer}", grid=(s // ts,),
        out_shape=(jax.ShapeDtypeStruct((s, D_MODEL), f32), jax.ShapeDtypeStruct((s, D_MODEL), bf16)),
        in_specs=[row, pl.BlockSpec((8, D_MODEL), lambda i: (0, 0)),
                  pl.BlockSpec((HEADS, HEAD_DIM, HEAD_DIM), lambda i: (0, 0, 0)),
                  pl.BlockSpec((8, D_MODEL), lambda i: (0, 0))],
        out_specs=(row, row),
        scratch_shapes=[pltpu.VMEM((POOL_HALO, D_MODEL), f32)],
        compiler_params=_params(("arbitrary",)),
    )(x, vec, pw, ps)


def _pool_bwd(x, dx, y, vec, pw, ps, layer):
    s = x.shape[0]
    ts = _pool_tile(s)
    nt = s // ts
    hshard = HEAD_DIM // N_DEV

    def body(x_ref, xh_ref, dx_ref, y_ref, vec_ref, pw_ref, ps_ref, dxo_ref, dpw_ref, sm_ref, acc_ref, q16_ref):
        i = pl.program_id(0)
        tile = nt - 1 - i

        @pl.when(i == 0)
        def _():
            acc_ref[...] = jnp.zeros_like(acc_ref)
            sm_ref[...] = jnp.zeros_like(sm_ref)
            q16_ref[...] = jnp.zeros_like(q16_ref)

        gs, sh = vec_ref[R_GS_M:R_GS_M + 1, :], vec_ref[R_SH_M:R_SH_M + 1, :]
        xv = x_ref[...]
        dxv = dx_ref[...]
        n, r = _rms(xv)
        h = n * gs + sh
        nh, _ = _rms(xh_ref[...])
        halo = jnp.where(tile == 0, 0.0, nh * gs + sh)
        inv = _pool_counts(tile, ts)
        pooled = _pooled(h, halo, inv)
        mixed = jnp.concatenate([_mm(pooled[g].astype(bf16), pw_ref[g]) for g in range(HEADS)], axis=1)

        dy = dxv * vec_ref[R_GT_M:R_GT_M + 1, :]
        sm_ref[G_GT:G_GT + 1, :] += _colsum(dxv * y_ref[...].astype(f32))
        sm_ref[3:4, :] += _colsum(dy * mixed)
        dmixed = (dy * ps_ref[0:1, :]).astype(bf16)
        dh_parts = []
        for g in range(HEADS):
            cols = slice(g * HEAD_DIM, (g + 1) * HEAD_DIM)
            acc_ref[g] += _mm_tn(pooled[g].astype(bf16), dmixed[:, cols])
            dpooled = _mm_nt(dmixed[:, cols], pw_ref[g])
            q = dpooled * inv[g]
            ext = jnp.concatenate([q, q16_ref[:, cols]], axis=0)
            q16_ref[:, cols] = q[0:POOL_HALO, :]
            for step in range(g + 1):
                ext = ext + pltpu.roll(ext, ext.shape[0] - (1 << step), 0)
            dh_parts.append(ext[:ts] - dpooled)
        dh = jnp.concatenate(dh_parts, axis=1)
        sm_ref[G_SH:G_SH + 1, :] += _colsum(dh)
        sm_ref[G_GS:G_GS + 1, :] += _colsum(dh * n)
        dxo_ref[...] = dxv + _norm_bwd(dh, n, r, gs)

        @pl.when(i == nt - 1)
        def _():
            for g in range(HEADS):
                dpw_ref[:, g] = acc_ref[g].astype(bf16).reshape(N_DEV, hshard, HEAD_DIM)

    rev = lambda i: (nt - 1 - i, 0)
    row = pl.BlockSpec((ts, D_MODEL), rev)
    halo16 = pl.BlockSpec((POOL_HALO, D_MODEL), lambda i: (jnp.maximum((nt - 1 - i) * (ts // POOL_HALO) - 1, 0), 0))
    const = lambda *shape: pl.BlockSpec(shape, lambda i: (0,) * len(shape))
    return pl.pallas_call(
        body, name=f"pool_bwd_{layer}", grid=(nt,),
        out_shape=(jax.ShapeDtypeStruct((s, D_MODEL), f32),
                   jax.ShapeDtypeStruct((N_DEV, HEADS, hshard, HEAD_DIM), bf16),
                   jax.ShapeDtypeStruct((8, D_MODEL), f32)),
        in_specs=[row, halo16, row, row, const(8, D_MODEL), const(HEADS, HEAD_DIM, HEAD_DIM), const(8, D_MODEL)],
        out_specs=(row, const(N_DEV, HEADS, hshard, HEAD_DIM), const(8, D_MODEL)),
        scratch_shapes=[pltpu.VMEM((HEADS, HEAD_DIM, HEAD_DIM), f32), pltpu.VMEM((POOL_HALO, D_MODEL), f32)],
        compiler_params=_params(("arbitrary",)),
    )(x, x, dx, y, vec, pw, ps)


def _final(x, target, g_fin):
    s = x.shape[0]
    ts = min(512, s)

    def body(x_ref, t_ref, g_ref, dx_ref, loss_ref, sm_ref):
        @pl.when(pl.program_id(0) == 0)
        def _():
            loss_ref[...] = jnp.zeros_like(loss_ref)
            sm_ref[...] = jnp.zeros_like(sm_ref)

        g = g_ref[0:1, :]
        n, r = _rms(x_ref[...])
        err = n * g - t_ref[...]
        loss_ref[...] += 0.5 * jnp.sum(jnp.mean(err * err, axis=-1, keepdims=True), axis=0, keepdims=True)
        dyv = err * (1.0 / D_MODEL)
        sm_ref[0:1, :] += _colsum(dyv * n)
        dx_ref[...] = _norm_bwd(dyv, n, r, g)

    row = pl.BlockSpec((ts, D_MODEL), lambda i: (i, 0))
    return pl.pallas_call(
        body, name="final_loss", grid=(s // ts,),
        out_shape=(jax.ShapeDtypeStruct((s, D_MODEL), f32), jax.ShapeDtypeStruct((8, 128), f32),
                   jax.ShapeDtypeStruct((8, D_MODEL), f32)),
        in_specs=[row, row, pl.BlockSpec((8, D_MODEL), lambda i: (0, 0))],
        out_specs=(row, pl.BlockSpec((8, 128), lambda i: (0, 0)), pl.BlockSpec((8, D_MODEL), lambda i: (0, 0))),
        compiler_params=_params(("arbitrary",)),
    )(x, target, g_fin)


def _small_pack(sm_ffn, sm_mix, sm_fin, table, g_mix, g_ffn, lam):
    def body(*refs):
        ffn, mix = refs[0:DEPTH], refs[DEPTH:2 * DEPTH]
        fin_ref, tab_ref, gm_ref, gf_ref, lam_ref, o_ref = refs[2 * DEPTH:]
        o_ref[...] = jnp.zeros_like(o_ref)
        for i in range(DEPTH):
            base = K_MOD + i * N_MOD
            o_ref[base + 0:base + 1, :] = mix[i][G_SH:G_SH + 1, :]
            o_ref[base + 1:base + 2, :] = mix[i][G_GS:G_GS + 1, :] * gm_ref[i:i + 1, :]
            o_ref[base + 2:base + 3, :] = mix[i][G_GT:G_GT + 1, :]
            o_ref[base + 3:base + 4, :] = ffn[i][G_SH:G_SH + 1, :]
            o_ref[base + 4:base + 5, :] = ffn[i][G_GS:G_GS + 1, :] * gf_ref[i:i + 1, :]
            o_ref[base + 5:base + 6, :] = ffn[i][G_GT:G_GT + 1, :]
            o_ref[K_NMIX + i:K_NMIX + i + 1, :] = mix[i][G_GS:G_GS + 1, :] * (1.0 + tab_ref[i, R_SC_M:R_SC_M + 1, :])
            o_ref[K_NFFN + i:K_NFFN + i + 1, :] = ffn[i][G_GS:G_GS + 1, :] * (1.0 + tab_ref[i, R_SC_F:R_SC_F + 1, :])
            j = i // 2
            if i % 2 == 0:
                for k, src in enumerate((G_BY, G_BIN, G_CONVB, None, G_BOUT)):
                    dst = K_LRUB + j * 5 + k
                    if src is None:
                        o_ref[dst:dst + 1, :] = mix[i][G_LS:G_LS + 1, :] * _sigmoid(-lam_ref[j:j + 1, :])
                    else:
                        o_ref[dst:dst + 1, :] = mix[i][src:src + 1, :]
                o_ref[K_CONVW + j * 4:K_CONVW + j * 4 + 4, :] = mix[i][G_CW0:G_CW0 + 4, :]
                o_ref[K_BA + j:K_BA + j + 1, :] = mix[i][G_BA:G_BA + 1, :]
                o_ref[K_BX + j:K_BX + j + 1, :] = mix[i][G_BX:G_BX + 1, :]
            else:
                o_ref[K_PS + j:K_PS + j + 1, :] = mix[i][3:4, :]
        o_ref[K_FIN:K_FIN + 1, :] = fin_ref[0:1, :]

    return pl.pallas_call(body, name="small_pack", out_shape=jax.ShapeDtypeStruct((K_ROWS, D_MODEL), f32))(
        *sm_ffn, *sm_mix, sm_fin, table, g_mix, g_ffn, lam)


def _small_sum(gathered):
    def body(g_ref, o_ref):
        tot = g_ref[0]
        for src in range(1, N_DEV):
            tot = tot + g_ref[src]
        o_ref[...] = tot

    return pl.pallas_call(body, name="small_sum", out_shape=jax.ShapeDtypeStruct(gathered.shape[1:], f32))(gathered)


def _adamw_math(g, w, m, v):
    m = ADAM_B1 * m + (1.0 - ADAM_B1) * g
    v = ADAM_B2 * v + (1.0 - ADAM_B2) * (g * g)
    m_hat = m / (1.0 - ADAM_B1 ** ADAM_STEP)
    v_hat = v / (1.0 - ADAM_B2 ** ADAM_STEP)
    delta = -ADAM_LR * (m_hat / (jnp.sqrt(v_hat) + ADAM_EPS) + ADAM_WD * w)
    return delta, m, v


def _adamw_small(name, g, w, m, v):
    shape = w.shape
    two_d = (1, shape[0]) if len(shape) == 1 else (math.prod(shape[:-1]), shape[-1])

    def body(g_ref, w_ref, m_ref, v_ref, d_ref, mo_ref, vo_ref):
        d_ref[...], mo_ref[...], vo_ref[...] = _adamw_math(g_ref[...], w_ref[...], m_ref[...], v_ref[...])

    outs = pl.pallas_call(body, name=f"adamw_{name}", out_shape=tuple(jax.ShapeDtypeStruct(two_d, f32) for _ in range(3)))(
        *(t.reshape(two_d) for t in (g, w, m, v)))
    return tuple(t.reshape(shape) for t in outs)


def _block_rows(rows, cols):
    tr = max(SUBLANES, min(rows, (512 * 1024) // (4 * cols)))
    while rows % tr:
        tr //= 2
    return tr


def _adamw_reduce(name, landings, kind, w, m, v):
    nl = len(landings)
    rows, cols = landings[0].shape[2:]
    tr = _block_rows(rows, cols)
    per_layer = rows // tr

    def body(*refs):
        l_refs = refs[:nl]
        w_ref, m_ref, v_ref, g_ref, d_ref, mo_ref, vo_ref = refs[nl:]
        layer = pl.program_id(0)
        for k in range(nl):
            @pl.when(layer == k)
            def _(k=k):
                g = l_refs[k][0].astype(f32)
                for src in range(1, N_DEV):
                    g = g + l_refs[k][src].astype(f32)
                g_ref[...] = g
        d_ref[...], mo_ref[...], vo_ref[...] = _adamw_math(g_ref[...], w_ref[...], m_ref[...], v_ref[...])

    blk = pl.BlockSpec((tr, cols), lambda l, r: (l * per_layer + r, 0))
    land = [pl.BlockSpec((N_DEV, None, tr, cols), lambda l, r, k=k: (0, kind, jnp.where(l == k, r, 0), 0)) for k in range(nl)]
    return pl.pallas_call(
        body, name=f"adamw_{name}", grid=(nl, per_layer),
        out_shape=tuple(jax.ShapeDtypeStruct((nl * rows, cols), f32) for _ in range(4)),
        in_specs=land + [blk, blk, blk],
        out_specs=(blk, blk, blk, blk),
        compiler_params=_params(("arbitrary", "arbitrary"), 32),
    )(*landings, w, m, v)


def _adamw_w_mod(c_all, dmod_all, w, m, v):
    depth, d, cols = w.shape
    tr = 256

    def body(c_ref, dm_ref, w_ref, m_ref, v_ref, g_ref, d_ref, mo_ref, vo_ref):
        cv = c_ref[...]
        cond = cv * _sigmoid(cv)
        g = lax.dot_general(cond, dm_ref[...], (((0,), (0,)), ((), ())), preferred_element_type=f32,
                            precision=lax.Precision.HIGHEST)
        g_ref[...] = g
        d_ref[...], mo_ref[...], vo_ref[...] = _adamw_math(g, w_ref[...], m_ref[...], v_ref[...])

    blk = pl.BlockSpec((None, tr, cols), lambda i, r: (i, r, 0))
    return pl.pallas_call(
        body, name="adamw_w_mod", grid=(depth, d // tr),
        out_shape=tuple(jax.ShapeDtypeStruct(w.shape, f32) for _ in range(4)),
        in_specs=[pl.BlockSpec((N_DEV, tr), lambda i, r: (0, r)),
                  pl.BlockSpec((None, N_DEV, cols), lambda i, r: (i, 0, 0)), blk, blk, blk],
        out_specs=(blk, blk, blk, blk),
        compiler_params=_params(("arbitrary", "arbitrary"), 32),
    )(c_all, dmod_all, w, m, v)


def kernel(x, c, w_mod, b_mod, norm_mix_g, norm_ffn_g, lru_w_y, lru_b_y, lru_w_in, lru_b_in, lru_conv_w, lru_conv_b, lru_w_a, lru_b_a, lru_w_x, lru_b_x, lru_lambda, lru_w_out, lru_b_out, pool_w, pool_scale, ffn_w1, ffn_w2, final_norm_g, loss_target, m_w_mod, m_b_mod, m_norm_mix_g, m_norm_ffn_g, m_lru_w_y, m_lru_b_y, m_lru_w_in, m_lru_b_in, m_lru_conv_w, m_lru_conv_b, m_lru_w_a, m_lru_b_a, m_lru_w_x, m_lru_b_x, m_lru_lambda, m_lru_w_out, m_lru_b_out, m_pool_w, m_pool_scale, m_ffn_w1, m_ffn_w2, m_final_norm_g, v_w_mod, v_b_mod, v_norm_mix_g, v_norm_ffn_g, v_lru_w_y, v_lru_b_y, v_lru_w_in, v_lru_b_in, v_lru_conv_w, v_lru_conv_b, v_lru_w_a, v_lru_b_a, v_lru_w_x, v_lru_b_x, v_lru_lambda, v_lru_w_out, v_lru_b_out, v_pool_w, v_pool_scale, v_ffn_w1, v_ffn_w2, v_final_norm_g):
    me = 4 * lax.axis_index("x") + 2 * lax.axis_index("y") + lax.axis_index("c")
    n_lru = lru_w_y.shape[0]
    shard = LRU_WIDTH // N_DEV
    hshard = HEAD_DIM // N_DEV
    xs = x[0]
    target = loss_target[0]

    lru_big = jnp.stack([lru_w_y, lru_w_in, lru_w_out]).astype(bf16)
    small_mats = jnp.stack([lru_w_a, lru_w_x, pool_w]).astype(bf16)
    small_vecs = jnp.concatenate([
        lru_conv_w.reshape(n_lru * 4, shard), lru_b_a.reshape(n_lru, HEADS * hshard),
        lru_b_x.reshape(n_lru, HEADS * hshard), pool_scale, jnp.zeros((2, shard), f32)], axis=0)
    h_first, token = _send_start("gather_first_start", [small_vecs, c, lru_big, small_mats], True, me)
    h_ffn = []
    for i in range(DEPTH):
        handle, token = _send_start(f"gather_ffn_start_{i}", [(ffn_w1[i] + token[0, 0]).astype(bf16),
                                                               (ffn_w2[i] + token[0, 0]).astype(bf16)], True, me)
        h_ffn.append(handle)
    sv_g, c_g, lru_g, sm_g = _send_wait("gather_first_wait", h_first, small_vecs)
    wbig = lru_g.transpose(1, 2, 0, 3, 4).reshape(3, n_lru, LRU_WIDTH, LRU_WIDTH)
    wsm = sm_g.transpose(1, 2, 3, 0, 4, 5).reshape(3, n_lru, HEADS, HEAD_DIM, HEAD_DIM)
    conv_w_full = sv_g[:, 0:8].reshape(N_DEV, n_lru, 4, shard).transpose(1, 2, 0, 3).reshape(n_lru, 4, LRU_WIDTH)
    b_a_full = sv_g[:, 8:10].reshape(N_DEV, n_lru, HEADS, hshard).transpose(1, 2, 0, 3).reshape(n_lru, LRU_WIDTH)
    b_x_full = sv_g[:, 10:12].reshape(N_DEV, n_lru, HEADS, hshard).transpose(1, 2, 0, 3).reshape(n_lru, LRU_WIDTH)
    ps_full = sv_g[:, 12:14].transpose(1, 0, 2).reshape(n_lru, D_MODEL)
    c_all = c_g.reshape(N_DEV, D_MODEL)

    (mod_g,) = _exchange([_mod_part(c_all, w_mod)], True, "gather_mod")
    mod_row = lax.dynamic_index_in_dim(mod_g, me, axis=2, keepdims=False)
    mod_row = mod_row.transpose(1, 0, 2).reshape(DEPTH, N_MOD * D_MODEL)
    table = _mod_table(mod_row, b_mod, norm_mix_g, norm_ffn_g)

    zero_row = jnp.zeros((1, LRU_WIDTH), f32)
    pvecs = [jnp.concatenate([lru_b_y[j:j + 1], lru_b_in[j:j + 1], lru_conv_b[j:j + 1], b_a_full[j:j + 1],
                              b_x_full[j:j + 1], lru_lambda[j:j + 1], lru_b_out[j:j + 1], zero_row,
                              conv_w_full[j], zero_row, zero_row, zero_row, zero_row], axis=0) for j in range(n_lru)]
    ps_rows = [jnp.concatenate([ps_full[j:j + 1], jnp.zeros((7, D_MODEL), f32)], axis=0) for j in range(n_lru)]

    saved = []
    ffn_w = []
    h = xs
    for i in range(DEPTH):
        j = i // 2
        if i % 2 == 0:
            h_mid, gpre, xr, hs, y_mix = _lru_fwd(h, table[i], wbig[:, j], wsm[0:2, j], pvecs[j], i)
            mix_saved = (h, gpre, xr, hs, y_mix)
        else:
            h_mid, y_mix = _pool_fwd(h, table[i], wsm[2, j], ps_rows[j], i)
            mix_saved = (h, y_mix)
        ffn_w.append(_send_wait(f"gather_ffn_wait_{i}", h_ffn[i], h_mid))
        h_out, u, y_ffn, hb = _ffn_fwd(h_mid, table[i], ffn_w[i][0], ffn_w[i][1], i)
        saved.append((mix_saved, (h_mid, u, y_ffn, hb)))
        h = h_out
    fin_rows = jnp.concatenate([final_norm_g[None, :], jnp.zeros((7, D_MODEL), f32)], axis=0)
    dx, loss_part, sm_fin = _final(h, target, fin_rows)
    loss = lax.psum(loss_part[0, 0], ("x", "y", "c"))

    sm_ffn, sm_mix = [None] * DEPTH, [None] * DEPTH
    x_ffn, x_mix = [None] * DEPTH, [None] * DEPTH
    token = jnp.zeros((8, 128), f32)
    for i in reversed(range(DEPTH)):
        j = i // 2
        mix_saved, (h_mid, u, y_ffn, hb) = saved[i]
        dx, da, dyb, sm_ffn[i] = _ffn_bwd_act(h_mid, dx, u, y_ffn, table[i] + token[0, 0], ffn_w[i][0], ffn_w[i][1], i)
        x_ffn[i], token = _send_start(f"grads_ffn_start_{i}", [_ffn_bwd_w1(hb, da, i), _ffn_bwd_w2(u, dyb, i)], False, me)
        if i % 2 == 0:
            h_in, gpre, xr, hs, y_mix = mix_saved
            dx, dbig, dsmall, sm_mix[i] = _lru_bwd(
                h_in, dx, gpre, xr, hs, y_mix, table[i] + token[0, 0], wbig[:, j], wsm[0:2, j], pvecs[j], i)
            x_mix[i], token = _send_start(f"grads_mix_start_{i}", [dbig, dsmall], False, me)
        else:
            h_in, y_mix = mix_saved
            dx, dpool, sm = _pool_bwd(h_in, dx, y_mix, table[i] + token[0, 0], wsm[2, j], ps_rows[j], i)
            sm_mix[i] = jnp.concatenate([sm, jnp.zeros((8, D_MODEL), f32)], axis=0)
            x_mix[i], token = _send_start(f"grads_mix_start_{i}", [dpool], False, me)
    grad_x = dx[None]

    pack = _small_pack(sm_ffn, sm_mix, sm_fin, table + token[0, 0], norm_mix_g, norm_ffn_g, lru_lambda)
    (pack_g,) = _exchange([pack], True, "gather_small_grads")
    tot = _small_sum(pack_g)
    cols = w_mod.shape[2]
    dmod_all = lax.dynamic_slice_in_dim(pack_g[:, K_MOD:K_MOD + DEPTH * N_MOD].reshape(N_DEV, DEPTH, N_MOD * D_MODEL),
                                        me * cols, cols, axis=2).transpose(1, 0, 2)
    results = {"w_mod": _adamw_w_mod(c_all, dmod_all, w_mod, m_w_mod, v_w_mod)}

    after = results["w_mod"][1]
    l_ffn = [_send_wait(f"grads_ffn_wait_{i}", x_ffn[i], after) for i in reversed(range(DEPTH))][::-1]
    l_mix = [_send_wait(f"grads_mix_wait_{i}", x_mix[i], after) for i in reversed(range(DEPTH))][::-1]

    def reduce_update(name, landings, kind, w, m, v):
        rows = w.size // w.shape[-1]
        two_d = (rows, w.shape[-1])
        lands = [t.reshape(N_DEV, -1, rows // len(landings), w.shape[-1]) for t in landings]
        outs = _adamw_reduce(name, lands, kind, w.reshape(two_d), m.reshape(two_d), v.reshape(two_d))
        return tuple(t.reshape(w.shape) for t in outs)

    l_lru_big = [l_mix[i][0] for i in range(0, DEPTH, 2)]
    l_lru_small = [l_mix[i][1] for i in range(0, DEPTH, 2)]
    l_pool = [l_mix[i][0] for i in range(1, DEPTH, 2)]
    results["ffn_w1"] = reduce_update("ffn_w1", [t[0] for t in l_ffn], 0, ffn_w1, m_ffn_w1, v_ffn_w1)
    results["ffn_w2"] = reduce_update("ffn_w2", [t[1] for t in l_ffn], 0, ffn_w2, m_ffn_w2, v_ffn_w2)
    results["lru_w_y"] = reduce_update("lru_w_y", l_lru_big, 0, lru_w_y, m_lru_w_y, v_lru_w_y)
    results["lru_w_in"] = reduce_update("lru_w_in", l_lru_big, 1, lru_w_in, m_lru_w_in, v_lru_w_in)
    results["lru_w_out"] = reduce_update("lru_w_out", l_lru_big, 2, lru_w_out, m_lru_w_out, v_lru_w_out)
    results["lru_w_a"] = reduce_update("lru_w_a", l_lru_small, 0, lru_w_a, m_lru_w_a, v_lru_w_a)
    results["lru_w_x"] = reduce_update("lru_w_x", l_lru_small, 1, lru_w_x, m_lru_w_x, v_lru_w_x)
    results["pool_w"] = reduce_update("pool_w", l_pool, 0, pool_w, m_pool_w, v_pool_w)

    def my_cols(full, width):
        return lax.dynamic_slice_in_dim(full, me * width, width, axis=full.ndim - 1)

    lru_rows = tot[K_LRUB:K_LRUB + 5 * n_lru].reshape(n_lru, 5, LRU_WIDTH)
    small_grads = {
        "b_mod": tot[K_MOD:K_MOD + DEPTH * N_MOD].reshape(DEPTH, N_MOD * D_MODEL),
        "norm_mix_g": tot[K_NMIX:K_NMIX + DEPTH],
        "norm_ffn_g": tot[K_NFFN:K_NFFN + DEPTH],
        "lru_b_y": lru_rows[:, 0], "lru_b_in": lru_rows[:, 1], "lru_conv_b": lru_rows[:, 2],
        "lru_lambda": lru_rows[:, 3], "lru_b_out": lru_rows[:, 4],
        "lru_conv_w": my_cols(tot[K_CONVW:K_CONVW + 4 * n_lru].reshape(n_lru, 4, LRU_WIDTH), shard),
        "lru_b_a": my_cols(tot[K_BA:K_BA + n_lru].reshape(n_lru, HEADS, HEAD_DIM), hshard),
        "lru_b_x": my_cols(tot[K_BX:K_BX + n_lru].reshape(n_lru, HEADS, HEAD_DIM), hshard),
        "pool_scale": my_cols(tot[K_PS:K_PS + n_lru], shard),
        "final_norm_g": tot[K_FIN],
    }
    given = dict(b_mod=(b_mod, m_b_mod, v_b_mod), norm_mix_g=(norm_mix_g, m_norm_mix_g, v_norm_mix_g),
                 norm_ffn_g=(norm_ffn_g, m_norm_ffn_g, v_norm_ffn_g), lru_b_y=(lru_b_y, m_lru_b_y, v_lru_b_y),
                 lru_b_in=(lru_b_in, m_lru_b_in, v_lru_b_in), lru_conv_w=(lru_conv_w, m_lru_conv_w, v_lru_conv_w),
                 lru_conv_b=(lru_conv_b, m_lru_conv_b, v_lru_conv_b), lru_b_a=(lru_b_a, m_lru_b_a, v_lru_b_a),
                 lru_b_x=(lru_b_x, m_lru_b_x, v_lru_b_x), lru_lambda=(lru_lambda, m_lru_lambda, v_lru_lambda),
                 lru_b_out=(lru_b_out, m_lru_b_out, v_lru_b_out), pool_scale=(pool_scale, m_pool_scale, v_pool_scale),
                 final_norm_g=(final_norm_g, m_final_norm_g, v_final_norm_g))
    for name, g in small_grads.items():
        results[name] = (g,) + _adamw_small(name, g, *given[name])

    order = ["w_mod", "b_mod", "norm_mix_g", "norm_ffn_g", "lru_w_y", "lru_b_y", "lru_w_in", "lru_b_in", "lru_conv_w",
             "lru_conv_b", "lru_w_a", "lru_b_a", "lru_w_x", "lru_b_x", "lru_lambda", "lru_w_out", "lru_b_out", "pool_w",
             "pool_scale", "ffn_w1", "ffn_w2", "final_norm_g"]
    return (loss, grad_x, *[results[n][0] for n in order], *[results[n][1] for n in order],
            *[results[n][2] for n in order], *[results[n][3] for n in order])
```

```python
import functools
import math

import jax
import jax.numpy as jnp
from jax import lax
from jax.experimental import pallas as pl
from jax.experimental.pallas import tpu as pltpu

f32, bf16 = jnp.float32, jnp.bfloat16

D_MODEL = 1024
LRU_WIDTH = 1024
HEADS = 4
HEAD_DIM = 256
D_FF = 4096
DEPTH = 4
N_MOD = 6
N_DEV = 8
FF_CHUNK = D_FF // N_DEV
POOL_WINDOWS = (2, 4, 8, 16)
POOL_HALO = 16
EPS = 1e-6
LRU_C = 8.0

ADAM_LR = 0.001
ADAM_B1 = 0.9
ADAM_B2 = 0.999
ADAM_EPS = 1e-08
ADAM_WD = 0.01
ADAM_STEP = 10

V7X_VMEM_BYTES = 64 * 1024 * 1024
SUBLANES = 8
BF16_ROWS = 16

R_SH_M, R_SC_M, R_GT_M, R_SH_F, R_SC_F, R_GT_F, R_GS_M, R_GS_F = range(8)
P_BY, P_BIN, P_CONVB, P_BA, P_BX, P_LAM, P_BOUT, P_CW0 = 0, 1, 2, 3, 4, 5, 6, 8
G_SH, G_GS, G_GT, G_BY, G_BIN, G_CONVB, G_BA, G_BX, G_LS, G_BOUT, G_CW0 = 0, 1, 2, 3, 4, 5, 6, 7, 8, 9, 10
K_MOD, K_NMIX, K_NFFN, K_LRUB, K_CONVW, K_BA, K_BX, K_PS, K_FIN, K_ROWS = 0, 24, 28, 32, 42, 50, 52, 54, 56, 64


def _params(semantics=None, vmem_mb=48):
    return pltpu.CompilerParams(dimension_semantics=semantics, vmem_limit_bytes=vmem_mb * 1024 * 1024)


def _mm(a, b):
    return jnp.dot(a, b, preferred_element_type=f32)


def _mm_nt(a, b):
    return lax.dot_general(a, b, (((1,), (1,)), ((), ())), preferred_element_type=f32)


def _mm_tn(a, b):
    return lax.dot_general(a, b, (((0,), (0,)), ((), ())), preferred_element_type=f32)


def _rms(x):
    r = lax.rsqrt(jnp.mean(x * x, axis=-1, keepdims=True) + EPS)
    return x * r, r


def _norm_bwd(dh, n, r, gs):
    dn = dh * gs
    return r * (dn - n * jnp.mean(dn * n, axis=-1, keepdims=True))


def _colsum(v):
    return jnp.sum(v, axis=0, keepdims=True)


def _sigmoid(v):
    return 1.0 / (1.0 + jnp.exp(-v))


def _log_sigmoid(v):
    return jnp.minimum(v, 0.0) - jnp.log1p(jnp.exp(-jnp.abs(v)))


_GELU_C = 0.7978845608028654
_GELU_A = 0.044715


def _gelu(v):
    return 0.5 * v * (1.0 + jnp.tanh(_GELU_C * (v + _GELU_A * v * v * v)))


def _gelu_grad(v):
    t = jnp.tanh(_GELU_C * (v + _GELU_A * v * v * v))
    return 0.5 * (1.0 + t) + 0.5 * v * (1.0 - t * t) * _GELU_C * (1.0 + 3.0 * _GELU_A * v * v)


def _neg_expm1(v):
    series = -v * (1.0 + v * (0.5 + v * (1.0 / 6 + v * (1.0 / 24 + v * (1.0 / 120 + v * (1.0 / 720))))))
    return jnp.where(v > -0.1, series, 1.0 - jnp.exp(v))


def _rows_before(halo, v, shifts):
    hr = halo.shape[0]
    ext = jnp.concatenate([halo, v], axis=0)
    return [pltpu.roll(ext, k, 0)[hr:] for k in shifts]


def _rows_after(v, halo, shifts):
    n = v.shape[0]
    ext = jnp.concatenate([v, halo], axis=0)
    return [pltpu.roll(ext, ext.shape[0] - k, 0)[:n] for k in shifts]


def _block_diag(v, w_ref, kind):
    return jnp.concatenate(
        [_mm(v[:, h * HEAD_DIM:(h + 1) * HEAD_DIM], w_ref[kind, h]) for h in range(HEADS)], axis=1)


def _block_diag_t(v, w_ref, kind):
    return jnp.concatenate(
        [_mm_nt(v[:, h * HEAD_DIM:(h + 1) * HEAD_DIM], w_ref[kind, h]) for h in range(HEADS)], axis=1)


def _scan(a_ref, u_ref, out_ref, carry, reverse):
    groups = a_ref.shape[0] // SUBLANES
    width = a_ref.shape[1]
    row = lax.broadcasted_iota(jnp.int32, (SUBLANES, width), 0)

    def step(j, c):
        g = groups - 1 - j if reverse else j
        off = pl.multiple_of(g * SUBLANES, SUBLANES)
        a = a_ref[pl.ds(off, SUBLANES), :]
        u = u_ref[pl.ds(off, SUBLANES), :]
        for k in (1, 2, 4):
            if reverse:
                valid, shift = row < SUBLANES - k, SUBLANES - k
            else:
                valid, shift = row >= k, k
            a_s = jnp.where(valid, pltpu.roll(a, shift, 0), 1.0)
            u_s = jnp.where(valid, pltpu.roll(u, shift, 0), 0.0)
            u = u + a * u_s
            a = a * a_s
        h = u + a * c
        out_ref[pl.ds(off, SUBLANES), :] = h
        last = h[0:1, :] if reverse else h[SUBLANES - 1:SUBLANES, :]
        return jnp.broadcast_to(last, (SUBLANES, width))

    return lax.fori_loop(0, groups, step, carry)


def _exchange(arrays, gather, name):
    n = len(arrays)
    peers = N_DEV - 1

    def body(*refs):
        ins, outs = refs[:n], refs[n:2 * n]
        send_sems, recv_sems, local_sems = refs[2 * n:]
        x, y, c = lax.axis_index("x"), lax.axis_index("y"), lax.axis_index("c")
        me = 4 * x + 2 * y + c
        local = []
        for k in range(n):
            cp = pltpu.make_async_copy(ins[k] if gather else ins[k].at[me], outs[k].at[me], local_sems.at[k])
            cp.start()
            local.append(cp)
        remote = []
        for p in range(1, N_DEV):
            px = 1 - x if p & 4 else x
            py = 1 - y if p & 2 else y
            pc = 1 - c if p & 1 else c
            for k in range(n):
                cp = pltpu.make_async_remote_copy(
                    src_ref=ins[k] if gather else ins[k].at[4 * px + 2 * py + pc],
                    dst_ref=outs[k].at[me],
                    send_sem=send_sems.at[k * peers + p - 1],
                    recv_sem=recv_sems.at[k * peers + p - 1],
                    device_id=(px, py, pc), device_id_type=pl.DeviceIdType.MESH)
                cp.start()
                remote.append(cp)
        for cp in remote:
            cp.wait()
        for cp in local:
            cp.wait()

    out_shape = tuple(
        jax.ShapeDtypeStruct(((N_DEV,) + a.shape) if gather else a.shape, a.dtype) for a in arrays)
    outs = pl.pallas_call(
        body, name=name, out_shape=out_shape,
        in_specs=[pl.BlockSpec(memory_space=pl.ANY)] * n,
        out_specs=tuple(pl.BlockSpec(memory_space=pl.ANY) for _ in range(n)),
        scratch_shapes=[pltpu.SemaphoreType.DMA((n * peers,)), pltpu.SemaphoreType.DMA((n * peers,)),
                        pltpu.SemaphoreType.DMA((n,))],
        compiler_params=pltpu.CompilerParams(has_side_effects=True),
    )(*arrays)
    return list(outs)


_HBM = pl.BlockSpec(memory_space=pltpu.HBM)
_SEM = pl.BlockSpec(memory_space=pltpu.SEMAPHORE)
_DATAFLOW = pltpu.SideEffectType.DATAFLOW_SIDE_EFFECTING


def _peer_copies(src_refs, land_refs, send_sems, recv_sems, gather):
    x, y, c = lax.axis_index("x"), lax.axis_index("y"), lax.axis_index("c")
    me = 4 * x + 2 * y + c
    peers = N_DEV - 1
    copies = []
    for p in range(1, N_DEV):
        px = 1 - x if p & 4 else x
        py = 1 - y if p & 2 else y
        pc = 1 - c if p & 1 else c
        for k in range(len(src_refs)):
            copies.append(pltpu.make_async_remote_copy(
                src_ref=src_refs[k] if gather else src_refs[k].at[4 * px + 2 * py + pc],
                dst_ref=land_refs[k].at[me],
                send_sem=send_sems.at[k * peers + p - 1], recv_sem=recv_sems.at[k * peers + p - 1],
                device_id=(px, py, pc), device_id_type=pl.DeviceIdType.MESH))
    return copies


def _landing(srcs, gather, me):
    out = []
    for a in srcs:
        own = a if gather else lax.dynamic_index_in_dim(a, me, 0, keepdims=False)
        out.append(lax.dynamic_update_index_in_dim(lax.empty((N_DEV,) + own.shape, own.dtype), own, me, 0))
    return out


def _send_start(name, srcs, gather, me):
    n = len(srcs)
    lands = _landing(srcs, gather, me)

    def body(*refs):
        src_refs, land_refs = refs[:n], refs[n:2 * n]
        send_sems, recv_sems, token = refs[2 * n], refs[2 * n + 1], refs[-1]
        for cp in _peer_copies(src_refs, land_refs, send_sems, recv_sems, gather):
            cp.start()
        token[...] = jnp.zeros_like(token)

    sems = pltpu.SemaphoreType.DMA((n * (N_DEV - 1),))
    outs = pl.pallas_call(
        body, name=name,
        out_shape=(sems, sems, *[pltpu.HBM(a.shape, a.dtype) for a in (*srcs, *lands)], jax.ShapeDtypeStruct((8, 128), f32)),
        in_specs=[_HBM] * (2 * n),
        out_specs=(_SEM, _SEM, *[_HBM] * (2 * n), pl.BlockSpec(memory_space=pltpu.VMEM)),
        input_output_aliases={k: 2 + k for k in range(2 * n)},
        compiler_params=pltpu.CompilerParams(has_side_effects=_DATAFLOW),
    )(*[pltpu.with_memory_space_constraint(a, pltpu.HBM) for a in (*srcs, *lands)])
    return (outs[0], outs[1], list(outs[2:2 + n]), list(outs[2 + n:2 + 2 * n]), gather), outs[-1]


def _send_wait(name, handle, after):
    send_sems, recv_sems, srcs, lands, gather = handle
    n = len(srcs)

    def body(*refs):
        src_refs, land_refs = refs[:n], refs[n:2 * n]
        for cp in _peer_copies(src_refs, land_refs, refs[2 * n], refs[2 * n + 1], gather):
            cp.wait_send()
            cp.wait_recv()

    outs = pl.pallas_call(
        body, name=name, out_shape=tuple(pltpu.HBM(a.shape, a.dtype) for a in (*srcs, *lands)),
        in_specs=[_HBM] * (2 * n) + [_SEM, _SEM, pl.BlockSpec(memory_space=pl.ANY)],
        out_specs=tuple([_HBM] * (2 * n)), input_output_aliases={k: k for k in range(2 * n)},
        compiler_params=pltpu.CompilerParams(has_side_effects=_DATAFLOW),
    )(*srcs, *lands, send_sems, recv_sems, after)
    return list(outs[n:])


def _mod_part(c_all, w_mod):
    depth, d, cols = w_mod.shape

    def body(c_ref, w_ref, o_ref):
        cv = c_ref[...]
        cond = cv * _sigmoid(cv)
        o_ref[...] = jnp.dot(cond, w_ref[...], preferred_element_type=f32, precision=lax.Precision.HIGHEST)

    return pl.pallas_call(
        body, name="mod_part", grid=(depth,),
        out_shape=jax.ShapeDtypeStruct((depth, N_DEV, cols), f32),
        in_specs=[pl.BlockSpec((N_DEV, d), lambda i: (0, 0)), pl.BlockSpec((None, d, cols), lambda i: (i, 0, 0))],
        out_specs=pl.BlockSpec((None, N_DEV, cols), lambda i: (i, 0, 0)),
        compiler_params=_params(("arbitrary",), 32),
    )(c_all, w_mod)


def _mod_table(mod_row, b_mod, g_mix, g_ffn):
    def body(m_ref, b_ref, gm_ref, gf_ref, o_ref, token_ref):
        for i in range(DEPTH):
            for k in range(N_MOD):
                o_ref[i, k:k + 1, :] = m_ref[i:i + 1, k * D_MODEL:(k + 1) * D_MODEL] + b_ref[i:i + 1, k * D_MODEL:(k + 1) * D_MODEL]
            o_ref[i, R_GS_M:R_GS_M + 1, :] = gm_ref[i:i + 1, :] * (1.0 + o_ref[i, R_SC_M:R_SC_M + 1, :])
            o_ref[i, R_GS_F:R_GS_F + 1, :] = gf_ref[i:i + 1, :] * (1.0 + o_ref[i, R_SC_F:R_SC_F + 1, :])
        token_ref[...] = jnp.zeros_like(token_ref)

    return pl.pallas_call(
        body, name="mod_table",
        out_shape=(jax.ShapeDtypeStruct((DEPTH, 8, D_MODEL), f32), jax.ShapeDtypeStruct((8, 128), f32)))(
        mod_row, b_mod, g_mix, g_ffn)


def _ffn_tile(s):
    return min(512, s)


def _layer_weights(shape):
    return pl.BlockSpec((N_DEV,) + shape, lambda i: (0, 0, 0))


def _ffn_fwd(x, vec, w1g, w2g, layer):
    s = x.shape[0]
    ts = _ffn_tile(s)

    def body(x_ref, vec_ref, w1_ref, w2_ref, xo_ref, u_ref, y_ref, hb_ref):
        xv = x_ref[...]
        n, _ = _rms(xv)
        hb = (n * vec_ref[R_GS_F:R_GS_F + 1, :] + vec_ref[R_SH_F:R_SH_F + 1, :]).astype(bf16)
        hb_ref[...] = hb
        yv = jnp.zeros((ts, D_MODEL), f32)
        for f in range(N_DEV):
            u = jnp.maximum(_mm(hb, w1_ref[f]), 0.0)
            u_ref[:, f * FF_CHUNK:(f + 1) * FF_CHUNK] = u.astype(bf16)
            yv = yv + _mm((u * u).astype(bf16), w2_ref[f])
        y_ref[...] = yv.astype(bf16)
        xo_ref[...] = xv + vec_ref[R_GT_F:R_GT_F + 1, :] * yv

    row = pl.BlockSpec((ts, D_MODEL), lambda i: (i, 0))
    return pl.pallas_call(
        body, name=f"ffn_fwd_{layer}", grid=(s // ts,),
        out_shape=(jax.ShapeDtypeStruct((s, D_MODEL), f32), jax.ShapeDtypeStruct((s, D_FF), bf16),
                   jax.ShapeDtypeStruct((s, D_MODEL), bf16), jax.ShapeDtypeStruct((s, D_MODEL), bf16)),
        in_specs=[row, pl.BlockSpec((8, D_MODEL), lambda i: (0, 0)),
                  _layer_weights((D_MODEL, FF_CHUNK)), _layer_weights((FF_CHUNK, D_MODEL))],
        out_specs=(row, pl.BlockSpec((ts, D_FF), lambda i: (i, 0)), row, row),
        compiler_params=_params(("arbitrary",), 56),
    )(x, vec, w1g, w2g)


def _ffn_bwd_act(x, dx, u, y, vec, w1g, w2g, layer):
    s = x.shape[0]
    ts = _ffn_tile(s)

    def body(x_ref, dx_ref, u_ref, y_ref, vec_ref, w1_ref, w2_ref, dxo_ref, da_ref, dyb_ref, sm_ref):
        @pl.when(pl.program_id(0) == 0)
        def _():
            sm_ref[...] = jnp.zeros_like(sm_ref)

        dxv = dx_ref[...]
        dyb = (dxv * vec_ref[R_GT_F:R_GT_F + 1, :]).astype(bf16)
        dyb_ref[...] = dyb
        sm_ref[G_GT:G_GT + 1, :] += _colsum(dxv * y_ref[...].astype(f32))
        dh = jnp.zeros((ts, D_MODEL), f32)
        for f in range(N_DEV):
            cols = slice(f * FF_CHUNK, (f + 1) * FF_CHUNK)
            dz = _mm_nt(dyb, w2_ref[f])
            dab = (dz * (2.0 * u_ref[:, cols].astype(f32))).astype(bf16)
            da_ref[:, cols] = dab
            dh = dh + _mm_nt(dab, w1_ref[f])
        n, r = _rms(x_ref[...])
        sm_ref[G_SH:G_SH + 1, :] += _colsum(dh)
        sm_ref[G_GS:G_GS + 1, :] += _colsum(dh * n)
        dxo_ref[...] = dxv + _norm_bwd(dh, n, r, vec_ref[R_GS_F:R_GS_F + 1, :])

    row = pl.BlockSpec((ts, D_MODEL), lambda i: (i, 0))
    wide = pl.BlockSpec((ts, D_FF), lambda i: (i, 0))
    return pl.pallas_call(
        body, name=f"ffn_bwd_act_{layer}", grid=(s // ts,),
        out_shape=(jax.ShapeDtypeStruct((s, D_MODEL), f32), jax.ShapeDtypeStruct((s, D_FF), bf16),
                   jax.ShapeDtypeStruct((s, D_MODEL), bf16), jax.ShapeDtypeStruct((8, D_MODEL), f32)),
        in_specs=[row, row, wide, row, pl.BlockSpec((8, D_MODEL), lambda i: (0, 0)),
                  _layer_weights((D_MODEL, FF_CHUNK)), _layer_weights((FF_CHUNK, D_MODEL))],
        out_specs=(row, wide, row, pl.BlockSpec((8, D_MODEL), lambda i: (0, 0))),
        compiler_params=_params(("arbitrary",), 58),
    )(x, dx, u, y, vec, w1g, w2g)


def _ffn_bwd_w1(hb, da, layer):
    s = hb.shape[0]
    ts = _ffn_tile(s)
    nt = s // ts

    def body(hb_ref, da_ref, dw_ref, acc_ref):
        i = pl.program_id(0)

        @pl.when(i == 0)
        def _():
            acc_ref[...] = jnp.zeros_like(acc_ref)

        hb = hb_ref[...]
        for f in range(N_DEV):
            acc_ref[f] += _mm_tn(hb, da_ref[:, f * FF_CHUNK:(f + 1) * FF_CHUNK])

        @pl.when(i == nt - 1)
        def _():
            dw_ref[...] = acc_ref[...].astype(bf16)

    return pl.pallas_call(
        body, name=f"ffn_bwd_w1_{layer}", grid=(nt,),
        out_shape=jax.ShapeDtypeStruct((N_DEV, D_MODEL, FF_CHUNK), bf16),
        in_specs=[pl.BlockSpec((ts, D_MODEL), lambda i: (i, 0)), pl.BlockSpec((ts, D_FF), lambda i: (i, 0))],
        out_specs=pl.BlockSpec((N_DEV, D_MODEL, FF_CHUNK), lambda i: (0, 0, 0)),
        scratch_shapes=[pltpu.VMEM((N_DEV, D_MODEL, FF_CHUNK), f32)],
        compiler_params=_params(("arbitrary",), 56),
    )(hb, da)


def _ffn_bwd_w2(u, dyb, layer):
    s = u.shape[0]
    ts = _ffn_tile(s)
    nt = s // ts

    def body(u_ref, dyb_ref, dw_ref, acc_ref):
        i = pl.program_id(0)

        @pl.when(i == 0)
        def _():
            acc_ref[...] = jnp.zeros_like(acc_ref)

        dyb = dyb_ref[...]
        for f in range(N_DEV):
            uv = u_ref[:, f * FF_CHUNK:(f + 1) * FF_CHUNK].astype(f32)
            acc_ref[f] += _mm_tn((uv * uv).astype(bf16), dyb)

        @pl.when(i == nt - 1)
        def _():
            dw_ref[...] = acc_ref[...].astype(bf16)

    return pl.pallas_call(
        body, name=f"ffn_bwd_w2_{layer}", grid=(nt,),
        out_shape=jax.ShapeDtypeStruct((N_DEV, FF_CHUNK, D_MODEL), bf16),
        in_specs=[pl.BlockSpec((ts, D_FF), lambda i: (i, 0)), pl.BlockSpec((ts, D_MODEL), lambda i: (i, 0))],
        out_specs=pl.BlockSpec((N_DEV, FF_CHUNK, D_MODEL), lambda i: (0, 0, 0)),
        scratch_shapes=[pltpu.VMEM((N_DEV, FF_CHUNK, D_MODEL), f32)],
        compiler_params=_params(("arbitrary",), 56),
    )(u, dyb)


def _lru_tile(s):
    return min(256, s)


def _lru_gates(xc, wsm_ref, pv_ref):
    xcb = xc.astype(bf16)
    gr = _sigmoid(_block_diag(xcb, wsm_ref, 0) + pv_ref[P_BA:P_BA + 1, :])
    gi = _sigmoid(_block_diag(xcb, wsm_ref, 1) + pv_ref[P_BX:P_BX + 1, :])
    ls = _log_sigmoid(pv_ref[P_LAM:P_LAM + 1, :])
    log_a = LRU_C * gr * ls
    return xcb, gr, gi, jnp.exp(log_a), jnp.sqrt(_neg_expm1(2.0 * log_a)), ls


def _conv(xr, taps_before, pv_ref):
    xc = xr * pv_ref[P_CW0 + 3:P_CW0 + 4, :] + pv_ref[P_CONVB:P_CONVB + 1, :]
    for k, v in zip((2, 1, 0), taps_before):
        xc = xc + v * pv_ref[P_CW0 + k:P_CW0 + k + 1, :]
    return xc


def _lru_fwd(x, vec, wbig, wsm, pvec, layer):
    s = x.shape[0]
    ts = _lru_tile(s)
    w = LRU_WIDTH

    def body(x_ref, vec_ref, wb_ref, wsm_ref, pv_ref, xo_ref, gpre_ref, xr_ref, hs_ref, y_ref,
             tail_ref, carry_ref, a_scr, u_scr):
        @pl.when(pl.program_id(0) == 0)
        def _():
            tail_ref[...] = jnp.zeros_like(tail_ref)
            carry_ref[...] = jnp.zeros_like(carry_ref)

        xv = x_ref[...]
        n, _ = _rms(xv)
        hb = (n * vec_ref[R_GS_M:R_GS_M + 1, :] + vec_ref[R_SH_M:R_SH_M + 1, :]).astype(bf16)
        gpre = _mm(hb, wb_ref[0]) + pv_ref[P_BY:P_BY + 1, :]
        xr = _mm(hb, wb_ref[1]) + pv_ref[P_BIN:P_BIN + 1, :]
        gpre_ref[...] = gpre.astype(bf16)
        xr_ref[...] = xr.astype(bf16)
        xc = _conv(xr, _rows_before(tail_ref[...], xr, (1, 2, 3)), pv_ref)
        tail_ref[...] = xr[ts - SUBLANES:, :]
        _, _, gi, a, mult, _ = _lru_gates(xc, wsm_ref, pv_ref)
        a_scr[...] = a
        u_scr[...] = mult * (gi * xc)
        carry_ref[...] = _scan(a_scr, u_scr, hs_ref, carry_ref[...], reverse=False)
        m = hs_ref[...] * _gelu(gpre)
        yv = _mm(m.astype(bf16), wb_ref[2]) + pv_ref[P_BOUT:P_BOUT + 1, :]
        y_ref[...] = yv.astype(bf16)
        xo_ref[...] = xv + vec_ref[R_GT_M:R_GT_M + 1, :] * yv

    row = pl.BlockSpec((ts, D_MODEL), lambda i: (i, 0))
    roww = pl.BlockSpec((ts, w), lambda i: (i, 0))
    return pl.pallas_call(
        body, name=f"lru_fwd_{layer}", grid=(s // ts,),
        out_shape=(jax.ShapeDtypeStruct((s, D_MODEL), f32), jax.ShapeDtypeStruct((s, w), bf16),
                   jax.ShapeDtypeStruct((s, w), bf16), jax.ShapeDtypeStruct((s, w), f32),
                   jax.ShapeDtypeStruct((s, D_MODEL), bf16)),
        in_specs=[row, pl.BlockSpec((8, D_MODEL), lambda i: (0, 0)),
                  pl.BlockSpec((3, w, w), lambda i: (0, 0, 0)),
                  pl.BlockSpec((2, HEADS, HEAD_DIM, HEAD_DIM), lambda i: (0, 0, 0, 0)),
                  pl.BlockSpec((16, w), lambda i: (0, 0))],
        out_specs=(row, roww, roww, roww, row),
        scratch_shapes=[pltpu.VMEM((SUBLANES, w), f32), pltpu.VMEM((SUBLANES, w), f32),
                        pltpu.VMEM((ts, w), f32), pltpu.VMEM((ts, w), f32)],
        compiler_params=_params(("arbitrary",)),
    )(x, vec, wbig, wsm, pvec)


def _lru_bwd(x, dx, gpre, xr, hs, y, vec, wbig, wsm, pvec, layer):
    s = x.shape[0]
    ts = min(128, s)
    nt = s // ts
    w = LRU_WIDTH
    shard = w // N_DEV
    hshard = HEAD_DIM // N_DEV

    def body(x_ref, dx_ref, gpre_ref, xr_ref, xrh_ref, hs_ref, hsh_ref, y_ref, vec_ref, wb_ref, wsm_ref, pv_ref,
             dxo_ref, dwb_ref, dwsm_ref, sm_ref,
             accb_ref, accs_ref, eps8_ref, dxc8_ref, a_scr, u_scr, e_scr):
        i = pl.program_id(0)
        first_tile = i == nt - 1

        @pl.when(i == 0)
        def _():
            accb_ref[...] = jnp.zeros_like(accb_ref)
            accs_ref[...] = jnp.zeros_like(accs_ref)
            sm_ref[...] = jnp.zeros_like(sm_ref)
            eps8_ref[...] = jnp.zeros_like(eps8_ref)
            dxc8_ref[...] = jnp.zeros_like(dxc8_ref)

        gs = vec_ref[R_GS_M:R_GS_M + 1, :]
        xv = x_ref[...]
        dxv = dx_ref[...]
        n, r = _rms(xv)
        hb = (n * gs + vec_ref[R_SH_M:R_SH_M + 1, :]).astype(bf16)
        gpre_v = gpre_ref[...].astype(f32)
        xrv = xr_ref[...].astype(f32)
        hsv = hs_ref[...]
        xr_halo = jnp.where(first_tile, 0.0, xrh_ref[...].astype(f32))
        hs_halo = jnp.where(first_tile, 0.0, hsh_ref[...])
        xs1, xs2, xs3 = _rows_before(xr_halo, xrv, (1, 2, 3))
        xc = _conv(xrv, (xs1, xs2, xs3), pv_ref)
        xcb, gr, gi, a, mult, ls = _lru_gates(xc, wsm_ref, pv_ref)
        gelu_v = _gelu(gpre_v)

        dy = dxv * vec_ref[R_GT_M:R_GT_M + 1, :]
        dyb = dy.astype(bf16)
        sm_ref[G_GT:G_GT + 1, :] += _colsum(dxv * y_ref[...].astype(f32))
        sm_ref[G_BOUT:G_BOUT + 1, :] += _colsum(dy)
        accb_ref[2] += _mm_tn((hsv * gelu_v).astype(bf16), dyb)
        dm = _mm_nt(dyb, wb_ref[2])
        dhs = dm * gelu_v
        dgpre = dm * hsv * _gelu_grad(gpre_v)

        a_scr[...] = a
        u_scr[...] = a * dhs
        _scan(a_scr, u_scr, e_scr, jnp.broadcast_to(eps8_ref[0:1, :], (SUBLANES, w)), reverse=True)
        (eps_next,) = _rows_after(e_scr[...], eps8_ref[...], (1,))
        eps8_ref[...] = e_scr[0:SUBLANES, :]
        delta = dhs + eps_next
        (h_prev,) = _rows_before(hs_halo, hsv, (1,))
        da = delta * h_prev
        dgi = delta * mult * xc
        dxc = delta * mult * gi
        dla = da * a - (delta * gi * xc) * (a * a) / mult
        sm_ref[G_LS:G_LS + 1, :] += _colsum(dla * (LRU_C * gr))
        dra = dla * (LRU_C * ls) * gr * (1.0 - gr)
        drx = dgi * gi * (1.0 - gi)
        drab, drxb = dra.astype(bf16), drx.astype(bf16)
        sm_ref[G_BA:G_BA + 1, :] += _colsum(dra)
        sm_ref[G_BX:G_BX + 1, :] += _colsum(drx)
        for h in range(HEADS):
            cols = slice(h * HEAD_DIM, (h + 1) * HEAD_DIM)
            accs_ref[0, h] += _mm_tn(xcb[:, cols], drab[:, cols])
            accs_ref[1, h] += _mm_tn(xcb[:, cols], drxb[:, cols])
        dxc = dxc + _block_diag_t(drab, wsm_ref, 0) + _block_diag_t(drxb, wsm_ref, 1)

        sm_ref[G_CONVB:G_CONVB + 1, :] += _colsum(dxc)
        for k, v in zip((3, 2, 1, 0), (xrv, xs1, xs2, xs3)):
            sm_ref[G_CW0 + k:G_CW0 + k + 1, :] += _colsum(dxc * v)
        ups = _rows_after(dxc, dxc8_ref[...], (1, 2, 3))
        dxc8_ref[...] = dxc[0:SUBLANES, :]
        dxr = dxc * pv_ref[P_CW0 + 3:P_CW0 + 4, :]
        for k, v in zip((2, 1, 0), ups):
            dxr = dxr + v * pv_ref[P_CW0 + k:P_CW0 + k + 1, :]

        dgb, dxrb = dgpre.astype(bf16), dxr.astype(bf16)
        sm_ref[G_BY:G_BY + 1, :] += _colsum(dgpre)
        sm_ref[G_BIN:G_BIN + 1, :] += _colsum(dxr)
        accb_ref[0] += _mm_tn(hb, dgb)
        accb_ref[1] += _mm_tn(hb, dxrb)
        dh = _mm_nt(dgb, wb_ref[0]) + _mm_nt(dxrb, wb_ref[1])
        sm_ref[G_SH:G_SH + 1, :] += _colsum(dh)
        sm_ref[G_GS:G_GS + 1, :] += _colsum(dh * n)
        dxo_ref[...] = dxv + _norm_bwd(dh, n, r, gs)

        @pl.when(i == nt - 1)
        def _():
            for k in range(3):
                dwb_ref[:, k] = accb_ref[k].astype(bf16).reshape(N_DEV, shard, w)
            for k in range(2):
                for h in range(HEADS):
                    dwsm_ref[:, k, h] = accs_ref[k, h].astype(bf16).reshape(N_DEV, hshard, HEAD_DIM)

    rev = lambda i: (nt - 1 - i, 0)
    row = pl.BlockSpec((ts, D_MODEL), rev)
    roww = pl.BlockSpec((ts, w), rev)
    halo16 = pl.BlockSpec((BF16_ROWS, w), lambda i: (jnp.maximum((nt - 1 - i) * (ts // BF16_ROWS) - 1, 0), 0))
    halo8 = pl.BlockSpec((SUBLANES, w), lambda i: (jnp.maximum((nt - 1 - i) * (ts // SUBLANES) - 1, 0), 0))
    const = lambda *shape: pl.BlockSpec(shape, lambda i: (0,) * len(shape), pipeline_mode=pl.Buffered(1))
    return pl.pallas_call(
        body, name=f"lru_bwd_{layer}", grid=(nt,),
        out_shape=(jax.ShapeDtypeStruct((s, D_MODEL), f32),
                   jax.ShapeDtypeStruct((N_DEV, 3, shard, w), bf16),
                   jax.ShapeDtypeStruct((N_DEV, 2, HEADS, hshard, HEAD_DIM), bf16),
                   jax.ShapeDtypeStruct((16, w), f32)),
        in_specs=[row, row, roww, roww, halo16, roww, halo8, row, const(8, D_MODEL), const(3, w, w),
                  const(2, HEADS, HEAD_DIM, HEAD_DIM), const(16, w)],
        out_specs=(row, const(N_DEV, 3, shard, w), const(N_DEV, 2, HEADS, hshard, HEAD_DIM), const(16, w)),
        scratch_shapes=[pltpu.VMEM((3, w, w), f32), pltpu.VMEM((2, HEADS, HEAD_DIM, HEAD_DIM), f32),
                        pltpu.VMEM((SUBLANES, w), f32), pltpu.VMEM((SUBLANES, w), f32),
                        pltpu.VMEM((ts, w), f32), pltpu.VMEM((ts, w), f32), pltpu.VMEM((ts, w), f32)],
        compiler_params=_params(("arbitrary",), 56),
    )(x, dx, gpre, xr, xr, hs, hs, y, vec, wbig, wsm, pvec)


def _pool_tile(s):
    return min(256, s)


def _pool_counts(tile_index, ts):
    t = (tile_index * ts + lax.broadcasted_iota(jnp.int32, (ts, 1), 0) + 1).astype(f32)
    return [1.0 / jnp.minimum(t, float(win)) for win in POOL_WINDOWS]


def _pooled(h, halo, inv):
    ext = jnp.concatenate([halo, h], axis=0)
    out = []
    for g in range(len(POOL_WINDOWS)):
        acc = ext[:, g * HEAD_DIM:(g + 1) * HEAD_DIM]
        for step in range(g + 1):
            acc = acc + pltpu.roll(acc, 1 << step, 0)
        out.append(acc[POOL_HALO:] * inv[g] - h[:, g * HEAD_DIM:(g + 1) * HEAD_DIM])
    return out


def _pool_fwd(x, vec, pw, ps, layer):
    s = x.shape[0]
    ts = _pool_tile(s)

    def body(x_ref, vec_ref, pw_ref, ps_ref, xo_ref, y_ref, halo_ref):
        i = pl.program_id(0)

        @pl.when(i == 0)
        def _():
            halo_ref[...] = jnp.zeros_like(halo_ref)

        xv = x_ref[...]
        n, _ = _rms(xv)
        h = n * vec_ref[R_GS_M:R_GS_M + 1, :] + vec_ref[R_SH_M:R_SH_M + 1, :]
        pooled = _pooled(h, halo_ref[...], _pool_counts(i, ts))
        halo_ref[...] = h[ts - POOL_HALO:, :]
        mixed = jnp.concatenate([_mm(pooled[g].astype(bf16), pw_ref[g]) for g in range(HEADS)], axis=1)
        yv = mixed * ps_ref[0:1, :]
        y_ref[...] = yv.astype(bf16)
        xo_ref[...] = xv + vec_ref[R_GT_M:R_GT_M + 1, :] * yv

    row = pl.BlockSpec((ts, D_MODEL), lambda i: (i, 0))
    return pl.pallas_call(
        body, name=f"pool_fwd_{layer}", grid=(s // ts,),
        out_shape=(jax.ShapeDtypeStruct((s, D_MODEL), f32), jax.ShapeDtypeStruct((s, D_MODEL), bf16)),
        in_specs=[row, pl.BlockSpec((8, D_MODEL), lambda i: (0, 0)),
                  pl.BlockSpec((HEADS, HEAD_DIM, HEAD_DIM), lambda i: (0, 0, 0)),
                  pl.BlockSpec((8, D_MODEL), lambda i: (0, 0))],
        out_specs=(row, row),
        scratch_shapes=[pltpu.VMEM((POOL_HALO, D_MODEL), f32)],
        compiler_params=_params(("arbitrary",)),
    )(x, vec, pw, ps)


def _pool_bwd(x, dx, y, vec, pw, ps, layer):
    s = x.shape[0]
    ts = _pool_tile(s)
    nt = s // ts
    hshard = HEAD_DIM // N_DEV

    def body(x_ref, xh_ref, dx_ref, y_ref, vec_ref, pw_ref, ps_ref, dxo_ref, dpw_ref, sm_ref, acc_ref, q16_ref):
        i = pl.program_id(0)
        tile = nt - 1 - i

        @pl.when(i == 0)
        def _():
            acc_ref[...] = jnp.zeros_like(acc_ref)
            sm_ref[...] = jnp.zeros_like(sm_ref)
            q16_ref[...] = jnp.zeros_like(q16_ref)

        gs, sh = vec_ref[R_GS_M:R_GS_M + 1, :], vec_ref[R_SH_M:R_SH_M + 1, :]
        xv = x_ref[...]
        dxv = dx_ref[...]
        n, r = _rms(xv)
        h = n * gs + sh
        nh, _ = _rms(xh_ref[...])
        halo = jnp.where(tile == 0, 0.0, nh * gs + sh)
        inv = _pool_counts(tile, ts)
        pooled = _pooled(h, halo, inv)
        mixed = jnp.concatenate([_mm(pooled[g].astype(bf16), pw_ref[g]) for g in range(HEADS)], axis=1)

        dy = dxv * vec_ref[R_GT_M:R_GT_M + 1, :]
        sm_ref[G_GT:G_GT + 1, :] += _colsum(dxv * y_ref[...].astype(f32))
        sm_ref[3:4, :] += _colsum(dy * mixed)
        dmixed = (dy * ps_ref[0:1, :]).astype(bf16)
        dh_parts = []
        for g in range(HEADS):
            cols = slice(g * HEAD_DIM, (g + 1) * HEAD_DIM)
            acc_ref[g] += _mm_tn(pooled[g].astype(bf16), dmixed[:, cols])
            dpooled = _mm_nt(dmixed[:, cols], pw_ref[g])
            q = dpooled * inv[g]
            ext = jnp.concatenate([q, q16_ref[:, cols]], axis=0)
            q16_ref[:, cols] = q[0:POOL_HALO, :]
            for step in range(g + 1):
                ext = ext + pltpu.roll(ext, ext.shape[0] - (1 << step), 0)
            dh_parts.append(ext[:ts] - dpooled)
        dh = jnp.concatenate(dh_parts, axis=1)
        sm_ref[G_SH:G_SH + 1, :] += _colsum(dh)
        sm_ref[G_GS:G_GS + 1, :] += _colsum(dh * n)
        dxo_ref[...] = dxv + _norm_bwd(dh, n, r, gs)

        @pl.when(i == nt - 1)
        def _():
            for g in range(HEADS):
                dpw_ref[:, g] = acc_ref[g].astype(bf16).reshape(N_DEV, hshard, HEAD_DIM)

    rev = lambda i: (nt - 1 - i, 0)
    row = pl.BlockSpec((ts, D_MODEL), rev)
    halo16 = pl.BlockSpec((POOL_HALO, D_MODEL), lambda i: (jnp.maximum((nt - 1 - i) * (ts // POOL_HALO) - 1, 0), 0))
    const = lambda *shape: pl.BlockSpec(shape, lambda i: (0,) * len(shape))
    return pl.pallas_call(
        body, name=f"pool_bwd_{layer}", grid=(nt,),
        out_shape=(jax.ShapeDtypeStruct((s, D_MODEL), f32),
                   jax.ShapeDtypeStruct((N_DEV, HEADS, hshard, HEAD_DIM), bf16),
                   jax.ShapeDtypeStruct((8, D_MODEL), f32)),
        in_specs=[row, halo16, row, row, const(8, D_MODEL), const(HEADS, HEAD_DIM, HEAD_DIM), const(8, D_MODEL)],
        out_specs=(row, const(N_DEV, HEADS, hshard, HEAD_DIM), const(8, D_MODEL)),
        scratch_shapes=[pltpu.VMEM((HEADS, HEAD_DIM, HEAD_DIM), f32), pltpu.VMEM((POOL_HALO, D_MODEL), f32)],
        compiler_params=_params(("arbitrary",)),
    )(x, x, dx, y, vec, pw, ps)


def _final(x, target, g_fin):
    s = x.shape[0]
    ts = min(512, s)

    def body(x_ref, t_ref, g_ref, dx_ref, loss_ref, sm_ref):
        @pl.when(pl.program_id(0) == 0)
        def _():
            loss_ref[...] = jnp.zeros_like(loss_ref)
            sm_ref[...] = jnp.zeros_like(sm_ref)

        g = g_ref[0:1, :]
        n, r = _rms(x_ref[...])
        err = n * g - t_ref[...]
        loss_ref[...] += 0.5 * jnp.sum(jnp.mean(err * err, axis=-1, keepdims=True), axis=0, keepdims=True)
        dyv = err * (1.0 / D_MODEL)
        sm_ref[0:1, :] += _colsum(dyv * n)
        dx_ref[...] = _norm_bwd(dyv, n, r, g)

    row = pl.BlockSpec((ts, D_MODEL), lambda i: (i, 0))
    return pl.pallas_call(
        body, name="final_loss", grid=(s // ts,),
        out_shape=(jax.ShapeDtypeStruct((s, D_MODEL), f32), jax.ShapeDtypeStruct((8, 128), f32),
                   jax.ShapeDtypeStruct((8, D_MODEL), f32)),
        in_specs=[row, row, pl.BlockSpec((8, D_MODEL), lambda i: (0, 0))],
        out_specs=(row, pl.BlockSpec((8, 128), lambda i: (0, 0)), pl.BlockSpec((8, D_MODEL), lambda i: (0, 0))),
        compiler_params=_params(("arbitrary",)),
    )(x, target, g_fin)


def _small_pack(sm_ffn, sm_mix, sm_fin, table, g_mix, g_ffn, lam):
    def body(*refs):
        ffn, mix = refs[0:DEPTH], refs[DEPTH:2 * DEPTH]
        fin_ref, tab_ref, gm_ref, gf_ref, lam_ref, o_ref = refs[2 * DEPTH:]
        o_ref[...] = jnp.zeros_like(o_ref)
        for i in range(DEPTH):
            base = K_MOD + i * N_MOD
            o_ref[base + 0:base + 1, :] = mix[i][G_SH:G_SH + 1, :]
            o_ref[base + 1:base + 2, :] = mix[i][G_GS:G_GS + 1, :] * gm_ref[i:i + 1, :]
            o_ref[base + 2:base + 3, :] = mix[i][G_GT:G_GT + 1, :]
            o_ref[base + 3:base + 4, :] = ffn[i][G_SH:G_SH + 1, :]
            o_ref[base + 4:base + 5, :] = ffn[i][G_GS:G_GS + 1, :] * gf_ref[i:i + 1, :]
            o_ref[base + 5:base + 6, :] = ffn[i][G_GT:G_GT + 1, :]
            o_ref[K_NMIX + i:K_NMIX + i + 1, :] = mix[i][G_GS:G_GS + 1, :] * (1.0 + tab_ref[i, R_SC_M:R_SC_M + 1, :])
            o_ref[K_NFFN + i:K_NFFN + i + 1, :] = ffn[i][G_GS:G_GS + 1, :] * (1.0 + tab_ref[i, R_SC_F:R_SC_F + 1, :])
            j = i // 2
            if i % 2 == 0:
                for k, src in enumerate((G_BY, G_BIN, G_CONVB, None, G_BOUT)):
                    dst = K_LRUB + j * 5 + k
                    if src is None:
                        o_ref[dst:dst + 1, :] = mix[i][G_LS:G_LS + 1, :] * _sigmoid(-lam_ref[j:j + 1, :])
                    else:
                        o_ref[dst:dst + 1, :] = mix[i][src:src + 1, :]
                o_ref[K_CONVW + j * 4:K_CONVW + j * 4 + 4, :] = mix[i][G_CW0:G_CW0 + 4, :]
                o_ref[K_BA + j:K_BA + j + 1, :] = mix[i][G_BA:G_BA + 1, :]
                o_ref[K_BX + j:K_BX + j + 1, :] = mix[i][G_BX:G_BX + 1, :]
            else:
                o_ref[K_PS + j:K_PS + j + 1, :] = mix[i][3:4, :]
        o_ref[K_FIN:K_FIN + 1, :] = fin_ref[0:1, :]

    return pl.pallas_call(body, name="small_pack", out_shape=jax.ShapeDtypeStruct((K_ROWS, D_MODEL), f32))(
        *sm_ffn, *sm_mix, sm_fin, table, g_mix, g_ffn, lam)


def _small_sum(gathered):
    def body(g_ref, o_ref, token_ref):
        tot = g_ref[0]
        for src in range(1, N_DEV):
            tot = tot + g_ref[src]
        o_ref[...] = tot
        token_ref[...] = jnp.zeros_like(token_ref)

    return pl.pallas_call(
        body, name="small_sum",
        out_shape=(jax.ShapeDtypeStruct(gathered.shape[1:], f32), jax.ShapeDtypeStruct((8, 128), f32)))(gathered)


def _adamw_math(g, w, m, v):
    m = ADAM_B1 * m + (1.0 - ADAM_B1) * g
    v = ADAM_B2 * v + (1.0 - ADAM_B2) * (g * g)
    m_hat = m / (1.0 - ADAM_B1 ** ADAM_STEP)
    v_hat = v / (1.0 - ADAM_B2 ** ADAM_STEP)
    delta = -ADAM_LR * (m_hat / (jnp.sqrt(v_hat) + ADAM_EPS) + ADAM_WD * w)
    return delta, m, v


def _adamw_small(name, g, w, m, v):
    shape = w.shape
    two_d = (1, shape[0]) if len(shape) == 1 else (math.prod(shape[:-1]), shape[-1])

    def body(g_ref, w_ref, m_ref, v_ref, d_ref, mo_ref, vo_ref):
        d_ref[...], mo_ref[...], vo_ref[...] = _adamw_math(g_ref[...], w_ref[...], m_ref[...], v_ref[...])

    outs = pl.pallas_call(body, name=f"adamw_{name}", out_shape=tuple(jax.ShapeDtypeStruct(two_d, f32) for _ in range(3)))(
        *(t.reshape(two_d) for t in (g, w, m, v)))
    return tuple(t.reshape(shape) for t in outs)


def _block_rows(rows, cols):
    tr = max(SUBLANES, min(rows, (512 * 1024) // (4 * cols)))
    while rows % tr:
        tr //= 2
    return tr


def _adamw_reduce(name, landings, kind, w, m, v):
    nl = len(landings)
    rows, cols = landings[0].shape[2:]
    tr = _block_rows(rows, cols)
    per_layer = rows // tr

    def body(*refs):
        l_refs = refs[:nl]
        w_ref, m_ref, v_ref, g_ref, d_ref, mo_ref, vo_ref = refs[nl:]
        layer = pl.program_id(0)
        for k in range(nl):
            @pl.when(layer == k)
            def _(k=k):
                g = l_refs[k][0].astype(f32)
                for src in range(1, N_DEV):
                    g = g + l_refs[k][src].astype(f32)
                g_ref[...] = g
        d_ref[...], mo_ref[...], vo_ref[...] = _adamw_math(g_ref[...], w_ref[...], m_ref[...], v_ref[...])

    blk = pl.BlockSpec((tr, cols), lambda l, r: (l * per_layer + r, 0))
    land = [pl.BlockSpec((N_DEV, None, tr, cols), lambda l, r, k=k: (0, kind, jnp.where(l == k, r, 0), 0)) for k in range(nl)]
    return pl.pallas_call(
        body, name=f"adamw_{name}", grid=(nl, per_layer),
        out_shape=tuple(jax.ShapeDtypeStruct((nl * rows, cols), f32) for _ in range(4)),
        in_specs=land + [blk, blk, blk],
        out_specs=(blk, blk, blk, blk),
        compiler_params=_params(("arbitrary", "arbitrary"), 32),
    )(*landings, w, m, v)


def _adamw_w_mod(c_all, dmod_all, w, m, v):
    depth, d, cols = w.shape
    tr = 256

    def body(c_ref, dm_ref, w_ref, m_ref, v_ref, g_ref, d_ref, mo_ref, vo_ref):
        cv = c_ref[...]
        cond = cv * _sigmoid(cv)
        g = lax.dot_general(cond, dm_ref[...], (((0,), (0,)), ((), ())), preferred_element_type=f32,
                            precision=lax.Precision.HIGHEST)
        g_ref[...] = g
        d_ref[...], mo_ref[...], vo_ref[...] = _adamw_math(g, w_ref[...], m_ref[...], v_ref[...])

    blk = pl.BlockSpec((None, tr, cols), lambda i, r: (i, r, 0))
    return pl.pallas_call(
        body, name="adamw_w_mod", grid=(depth, d // tr),
        out_shape=tuple(jax.ShapeDtypeStruct(w.shape, f32) for _ in range(4)),
        in_specs=[pl.BlockSpec((N_DEV, tr), lambda i, r: (0, r)),
                  pl.BlockSpec((None, N_DEV, cols), lambda i, r: (i, 0, 0)), blk, blk, blk],
        out_specs=(blk, blk, blk, blk),
        compiler_params=_params(("arbitrary", "arbitrary"), 32),
    )(c_all, dmod_all, w, m, v)


def kernel(x, c, w_mod, b_mod, norm_mix_g, norm_ffn_g, lru_w_y, lru_b_y, lru_w_in, lru_b_in, lru_conv_w, lru_conv_b, lru_w_a, lru_b_a, lru_w_x, lru_b_x, lru_lambda, lru_w_out, lru_b_out, pool_w, pool_scale, ffn_w1, ffn_w2, final_norm_g, loss_target, m_w_mod, m_b_mod, m_norm_mix_g, m_norm_ffn_g, m_lru_w_y, m_lru_b_y, m_lru_w_in, m_lru_b_in, m_lru_conv_w, m_lru_conv_b, m_lru_w_a, m_lru_b_a, m_lru_w_x, m_lru_b_x, m_lru_lambda, m_lru_w_out, m_lru_b_out, m_pool_w, m_pool_scale, m_ffn_w1, m_ffn_w2, m_final_norm_g, v_w_mod, v_b_mod, v_norm_mix_g, v_norm_ffn_g, v_lru_w_y, v_lru_b_y, v_lru_w_in, v_lru_b_in, v_lru_conv_w, v_lru_conv_b, v_lru_w_a, v_lru_b_a, v_lru_w_x, v_lru_b_x, v_lru_lambda, v_lru_w_out, v_lru_b_out, v_pool_w, v_pool_scale, v_ffn_w1, v_ffn_w2, v_final_norm_g):
    me = 4 * lax.axis_index("x") + 2 * lax.axis_index("y") + lax.axis_index("c")
    n_lru = lru_w_y.shape[0]
    shard = LRU_WIDTH // N_DEV
    hshard = HEAD_DIM // N_DEV
    xs = x[0]
    target = loss_target[0]

    small_vecs = jnp.concatenate([
        lru_conv_w.reshape(n_lru * 4, shard), lru_b_a.reshape(n_lru, HEADS * hshard),
        lru_b_x.reshape(n_lru, HEADS * hshard), pool_scale, jnp.zeros((2, shard), f32)], axis=0)
    sv_g, c_g = _exchange([small_vecs, c], True, "gather_cond")
    conv_w_full = sv_g[:, 0:8].reshape(N_DEV, n_lru, 4, shard).transpose(1, 2, 0, 3).reshape(n_lru, 4, LRU_WIDTH)
    b_a_full = sv_g[:, 8:10].reshape(N_DEV, n_lru, HEADS, hshard).transpose(1, 2, 0, 3).reshape(n_lru, LRU_WIDTH)
    b_x_full = sv_g[:, 10:12].reshape(N_DEV, n_lru, HEADS, hshard).transpose(1, 2, 0, 3).reshape(n_lru, LRU_WIDTH)
    ps_full = sv_g[:, 12:14].transpose(1, 0, 2).reshape(n_lru, D_MODEL)
    c_all = c_g.reshape(N_DEV, D_MODEL)

    (mod_g,) = _exchange([_mod_part(c_all, w_mod)], True, "gather_mod")
    mod_row = lax.dynamic_index_in_dim(mod_g, me, axis=2, keepdims=False)
    mod_row = mod_row.transpose(1, 0, 2).reshape(DEPTH, N_MOD * D_MODEL)
    table, token = _mod_table(mod_row, b_mod, norm_mix_g, norm_ffn_g)

    h_mix, h_ffn = [], []
    for i in range(DEPTH):
        j = i // 2
        if i % 2 == 0:
            mix_src = [(jnp.stack([lru_w_y[j], lru_w_in[j], lru_w_out[j]]) + token[0, 0]).astype(bf16),
                       (jnp.stack([lru_w_a[j], lru_w_x[j]]) + token[0, 0]).astype(bf16)]
        else:
            mix_src = [(pool_w[j] + token[0, 0]).astype(bf16)]
        handle, token = _send_start(f"gather_mix_start_{i}", mix_src, True, me)
        h_mix.append(handle)
        handle, token = _send_start(f"gather_ffn_start_{i}", [(ffn_w1[i] + token[0, 0]).astype(bf16),
                                                               (ffn_w2[i] + token[0, 0]).astype(bf16)], True, me)
        h_ffn.append(handle)

    zero_row = jnp.zeros((1, LRU_WIDTH), f32)
    pvecs = [jnp.concatenate([lru_b_y[j:j + 1], lru_b_in[j:j + 1], lru_conv_b[j:j + 1], b_a_full[j:j + 1],
                              b_x_full[j:j + 1], lru_lambda[j:j + 1], lru_b_out[j:j + 1], zero_row,
                              conv_w_full[j], zero_row, zero_row, zero_row, zero_row], axis=0) for j in range(n_lru)]
    ps_rows = [jnp.concatenate([ps_full[j:j + 1], jnp.zeros((7, D_MODEL), f32)], axis=0) for j in range(n_lru)]

    saved = []
    ffn_w, mix_w = [], []
    h = xs
    for i in range(DEPTH):
        j = i // 2
        got = _send_wait(f"gather_mix_wait_{i}", h_mix[i], h)
        if i % 2 == 0:
            mix_w.append((got[0].transpose(1, 0, 2, 3).reshape(3, LRU_WIDTH, LRU_WIDTH),
                          got[1].transpose(1, 2, 0, 3, 4).reshape(2, HEADS, HEAD_DIM, HEAD_DIM)))
            h_mid, gpre, xr, hs, y_mix = _lru_fwd(h, table[i], mix_w[i][0], mix_w[i][1], pvecs[j], i)
            mix_saved = (h, gpre, xr, hs, y_mix)
        else:
            mix_w.append((got[0].transpose(1, 0, 2, 3).reshape(HEADS, HEAD_DIM, HEAD_DIM),))
            h_mid, y_mix = _pool_fwd(h, table[i], mix_w[i][0], ps_rows[j], i)
            mix_saved = (h, y_mix)
        ffn_w.append(_send_wait(f"gather_ffn_wait_{i}", h_ffn[i], h_mid))
        h_out, u, y_ffn, hb = _ffn_fwd(h_mid, table[i], ffn_w[i][0], ffn_w[i][1], i)
        saved.append((mix_saved, (h_mid, u, y_ffn, hb)))
        h = h_out
    fin_rows = jnp.concatenate([final_norm_g[None, :], jnp.zeros((7, D_MODEL), f32)], axis=0)
    dx, loss_part, sm_fin = _final(h, target, fin_rows)
    loss = lax.psum(loss_part[0, 0], ("x", "y", "c"))

    sm_ffn, sm_mix = [None] * DEPTH, [None] * DEPTH
    x_ffn, x_mix = [None] * DEPTH, [None] * DEPTH
    token = jnp.zeros((8, 128), f32)
    for i in reversed(range(DEPTH)):
        j = i // 2
        mix_saved, (h_mid, u, y_ffn, hb) = saved[i]
        dx, da, dyb, sm_ffn[i] = _ffn_bwd_act(h_mid, dx, u, y_ffn, table[i] + token[0, 0], ffn_w[i][0], ffn_w[i][1], i)
        x_ffn[i], token = _send_start(f"grads_ffn_start_{i}", [_ffn_bwd_w1(hb, da, i), _ffn_bwd_w2(u, dyb, i)], False, me)
        if i % 2 == 0:
            h_in, gpre, xr, hs, y_mix = mix_saved
            dx, dbig, dsmall, sm_mix[i] = _lru_bwd(
                h_in, dx, gpre, xr, hs, y_mix, table[i] + token[0, 0], mix_w[i][0], mix_w[i][1], pvecs[j], i)
            last_mix = [dbig, dsmall]
        else:
            h_in, y_mix = mix_saved
            dx, dpool, sm = _pool_bwd(h_in, dx, y_mix, table[i] + token[0, 0], mix_w[i][0], ps_rows[j], i)
            sm_mix[i] = jnp.concatenate([sm, jnp.zeros((8, D_MODEL), f32)], axis=0)
            last_mix = [dpool]
        if i > 0:
            x_mix[i], token = _send_start(f"grads_mix_start_{i}", last_mix, False, me)
    grad_x = dx[None]

    pack = _small_pack(sm_ffn, sm_mix, sm_fin, table + token[0, 0], norm_mix_g, norm_ffn_g, lru_lambda)
    (pack_g,) = _exchange([pack], True, "gather_small_grads")
    tot, token = _small_sum(pack_g)
    x_mix[0], _ = _send_start("grads_mix_start_0", [t + token[0, 0].astype(bf16) for t in last_mix], False, me)
    cols = w_mod.shape[2]
    dmod_all = lax.dynamic_slice_in_dim(pack_g[:, K_MOD:K_MOD + DEPTH * N_MOD].reshape(N_DEV, DEPTH, N_MOD * D_MODEL),
                                        me * cols, cols, axis=2).transpose(1, 0, 2)
    results = {"w_mod": _adamw_w_mod(c_all, dmod_all, w_mod, m_w_mod, v_w_mod)}

    after = results["w_mod"][1]
    l_ffn = [_send_wait(f"grads_ffn_wait_{i}", x_ffn[i], after) for i in reversed(range(DEPTH))][::-1]

    def reduce_update(name, landings, kind, w, m, v):
        rows = w.size // w.shape[-1]
        two_d = (rows, w.shape[-1])
        lands = [t.reshape(N_DEV, -1, rows // len(landings), w.shape[-1]) for t in landings]
        outs = _adamw_reduce(name, lands, kind, w.reshape(two_d), m.reshape(two_d), v.reshape(two_d))
        return tuple(t.reshape(w.shape) for t in outs)

    results["ffn_w1"] = reduce_update("ffn_w1", [t[0] for t in l_ffn], 0, ffn_w1, m_ffn_w1, v_ffn_w1)
    results["ffn_w2"] = reduce_update("ffn_w2", [t[1] for t in l_ffn], 0, ffn_w2, m_ffn_w2, v_ffn_w2)
    after = results["ffn_w2"][1]
    l_mix = [_send_wait(f"grads_mix_wait_{i}", x_mix[i], after) for i in reversed(range(DEPTH))][::-1]
    l_lru_big = [l_mix[i][0] for i in range(0, DEPTH, 2)]
    l_lru_small = [l_mix[i][1] for i in range(0, DEPTH, 2)]
    l_pool = [l_mix[i][0] for i in range(1, DEPTH, 2)]
    results["lru_w_y"] = reduce_update("lru_w_y", l_lru_big, 0, lru_w_y, m_lru_w_y, v_lru_w_y)
    results["lru_w_in"] = reduce_update("lru_w_in", l_lru_big, 1, lru_w_in, m_lru_w_in, v_lru_w_in)
    results["lru_w_out"] = reduce_update("lru_w_out", l_lru_big, 2, lru_w_out, m_lru_w_out, v_lru_w_out)
    results["lru_w_a"] = reduce_update("lru_w_a", l_lru_small, 0, lru_w_a, m_lru_w_a, v_lru_w_a)
    results["lru_w_x"] = reduce_update("lru_w_x", l_lru_small, 1, lru_w_x, m_lru_w_x, v_lru_w_x)
    results["pool_w"] = reduce_update("pool_w", l_pool, 0, pool_w, m_pool_w, v_pool_w)

    def my_cols(full, width):
        return lax.dynamic_slice_in_dim(full, me * width, width, axis=full.ndim - 1)

    lru_rows = tot[K_LRUB:K_LRUB + 5 * n_lru].reshape(n_lru, 5, LRU_WIDTH)
    small_grads = {
        "b_mod": tot[K_MOD:K_MOD + DEPTH * N_MOD].reshape(DEPTH, N_MOD * D_MODEL),
        "norm_mix_g": tot[K_NMIX:K_NMIX + DEPTH],
        "norm_ffn_g": tot[K_NFFN:K_NFFN + DEPTH],
        "lru_b_y": lru_rows[:, 0], "lru_b_in": lru_rows[:, 1], "lru_conv_b": lru_rows[:, 2],
        "lru_lambda": lru_rows[:, 3], "lru_b_out": lru_rows[:, 4],
        "lru_conv_w": my_cols(tot[K_CONVW:K_CONVW + 4 * n_lru].reshape(n_lru, 4, LRU_WIDTH), shard),
        "lru_b_a": my_cols(tot[K_BA:K_BA + n_lru].reshape(n_lru, HEADS, HEAD_DIM), hshard),
        "lru_b_x": my_cols(tot[K_BX:K_BX + n_lru].reshape(n_lru, HEADS, HEAD_DIM), hshard),
        "pool_scale": my_cols(tot[K_PS:K_PS + n_lru], shard),
        "final_norm_g": tot[K_FIN],
    }
    given = dict(b_mod=(b_mod, m_b_mod, v_b_mod), norm_mix_g=(norm_mix_g, m_norm_mix_g, v_norm_mix_g),
                 norm_ffn_g=(norm_ffn_g, m_norm_ffn_g, v_norm_ffn_g), lru_b_y=(lru_b_y, m_lru_b_y, v_lru_b_y),
                 lru_b_in=(lru_b_in, m_lru_b_in, v_lru_b_in), lru_conv_w=(lru_conv_w, m_lru_conv_w, v_lru_conv_w),
                 lru_conv_b=(lru_conv_b, m_lru_conv_b, v_lru_conv_b), lru_b_a=(lru_b_a, m_lru_b_a, v_lru_b_a),
                 lru_b_x=(lru_b_x, m_lru_b_x, v_lru_b_x), lru_lambda=(lru_lambda, m_lru_lambda, v_lru_lambda),
                 lru_b_out=(lru_b_out, m_lru_b_out, v_lru_b_out), pool_scale=(pool_scale, m_pool_scale, v_pool_scale),
                 final_norm_g=(final_norm_g, m_final_norm_g, v_final_norm_g))
    for name, g in small_grads.items():
        results[name] = (g,) + _adamw_small(name, g, *given[name])

    order = ["w_mod", "b_mod", "norm_mix_g", "norm_ffn_g", "lru_w_y", "lru_b_y", "lru_w_in", "lru_b_in", "lru_conv_w",
             "lru_conv_b", "lru_w_a", "lru_b_a", "lru_w_x", "lru_b_x", "lru_lambda", "lru_w_out", "lru_b_out", "pool_w",
             "pool_scale", "ffn_w1", "ffn_w2", "final_norm_g"]
    return (loss, grad_x, *[results[n][0] for n in order], *[results[n][1] for n in order],
            *[results[n][2] for n in order], *[results[n][3] for n in order])
```

```python
import functools
import math

import jax
import jax.numpy as jnp
from jax import lax
from jax.experimental import pallas as pl
from jax.experimental.pallas import tpu as pltpu

f32, bf16 = jnp.float32, jnp.bfloat16

D_MODEL = 1024
LRU_WIDTH = 1024
HEADS = 4
HEAD_DIM = 256
D_FF = 4096
DEPTH = 4
N_MOD = 6
N_DEV = 8
FF_CHUNK = D_FF // N_DEV
POOL_WINDOWS = (2, 4, 8, 16)
POOL_HALO = 16
EPS = 1e-6
LRU_C = 8.0

ADAM_LR = 0.001
ADAM_B1 = 0.9
ADAM_B2 = 0.999
ADAM_EPS = 1e-08
ADAM_WD = 0.01
ADAM_STEP = 10

V7X_VMEM_BYTES = 64 * 1024 * 1024
SUBLANES = 8
BF16_ROWS = 16

R_SH_M, R_SC_M, R_GT_M, R_SH_F, R_SC_F, R_GT_F, R_GS_M, R_GS_F = range(8)
P_BY, P_BIN, P_CONVB, P_BA, P_BX, P_LAM, P_BOUT, P_CW0 = 0, 1, 2, 3, 4, 5, 6, 8
G_SH, G_GS, G_GT, G_BY, G_BIN, G_CONVB, G_BA, G_BX, G_LS, G_BOUT, G_CW0 = 0, 1, 2, 3, 4, 5, 6, 7, 8, 9, 10
K_MOD, K_NMIX, K_NFFN, K_LRUB, K_CONVW, K_BA, K_BX, K_PS, K_FIN, K_ROWS = 0, 24, 28, 32, 42, 50, 52, 54, 56, 64


def _params(semantics=None, vmem_mb=48):
    return pltpu.CompilerParams(dimension_semantics=semantics, vmem_limit_bytes=vmem_mb * 1024 * 1024)


def _mm(a, b):
    return jnp.dot(a, b, preferred_element_type=f32)


def _mm_nt(a, b):
    return lax.dot_general(a, b, (((1,), (1,)), ((), ())), preferred_element_type=f32)


def _mm_tn(a, b):
    return lax.dot_general(a, b, (((0,), (0,)), ((), ())), preferred_element_type=f32)


def _rms(x):
    r = lax.rsqrt(jnp.mean(x * x, axis=-1, keepdims=True) + EPS)
    return x * r, r


def _norm_bwd(dh, n, r, gs):
    dn = dh * gs
    return r * (dn - n * jnp.mean(dn * n, axis=-1, keepdims=True))


def _colsum(v):
    return jnp.sum(v, axis=0, keepdims=True)


def _sigmoid(v):
    return 0.5 * jnp.tanh(0.5 * v) + 0.5


def _log_sigmoid(v):
    return jnp.minimum(v, 0.0) - jnp.log1p(jnp.exp(-jnp.abs(v)))


_GELU_C = 0.7978845608028654
_GELU_A = 0.044715


def _gelu_and_grad(v):
    v2 = v * v
    t = jnp.tanh(_GELU_C * v * (1.0 + _GELU_A * v2))
    p = 0.5 + 0.5 * t
    return v * p, p + (0.5 * v) * (1.0 - t * t) * (_GELU_C + (3.0 * _GELU_A * _GELU_C) * v2)


def _neg_expm1(v, exp_v):
    series = -v * (1.0 + v * (0.5 + v * (1.0 / 6 + v * (1.0 / 24 + v * (1.0 / 120)))))
    return jnp.where(v > -0.1, series, 1.0 - exp_v)


def _rows_before(halo, v, shifts):
    hr = halo.shape[0]
    ext = jnp.concatenate([halo, v], axis=0)
    return [pltpu.roll(ext, k, 0)[hr:] for k in shifts]


def _rows_after(v, halo, shifts):
    n = v.shape[0]
    ext = jnp.concatenate([v, halo], axis=0)
    return [pltpu.roll(ext, ext.shape[0] - k, 0)[:n] for k in shifts]


def _block_diag(v, w_ref, kind):
    return jnp.concatenate(
        [_mm(v[:, h * HEAD_DIM:(h + 1) * HEAD_DIM], w_ref[kind, h]) for h in range(HEADS)], axis=1)


def _block_diag_t(v, w_ref, kind):
    return jnp.concatenate(
        [_mm_nt(v[:, h * HEAD_DIM:(h + 1) * HEAD_DIM], w_ref[kind, h]) for h in range(HEADS)], axis=1)


def _scan(a_ref, u_ref, out_ref, carry, reverse):
    groups = a_ref.shape[0] // SUBLANES
    width = a_ref.shape[1]
    row = lax.broadcasted_iota(jnp.int32, (SUBLANES, width), 0)

    def step(j, c):
        g = groups - 1 - j if reverse else j
        off = pl.multiple_of(g * SUBLANES, SUBLANES)
        a = a_ref[pl.ds(off, SUBLANES), :]
        u = u_ref[pl.ds(off, SUBLANES), :]
        for k in (1, 2, 4):
            if reverse:
                valid, shift = row < SUBLANES - k, SUBLANES - k
            else:
                valid, shift = row >= k, k
            a_s = jnp.where(valid, pltpu.roll(a, shift, 0), 1.0)
            u_s = jnp.where(valid, pltpu.roll(u, shift, 0), 0.0)
            u = u + a * u_s
            a = a * a_s
        h = u + a * c
        out_ref[pl.ds(off, SUBLANES), :] = h
        last = h[0:1, :] if reverse else h[SUBLANES - 1:SUBLANES, :]
        return jnp.broadcast_to(last, (SUBLANES, width))

    return lax.fori_loop(0, groups, step, carry)


def _exchange(arrays, gather, name):
    n = len(arrays)
    peers = N_DEV - 1

    def body(*refs):
        ins, outs = refs[:n], refs[n:2 * n]
        send_sems, recv_sems, local_sems = refs[2 * n:]
        x, y, c = lax.axis_index("x"), lax.axis_index("y"), lax.axis_index("c")
        me = 4 * x + 2 * y + c
        local = []
        for k in range(n):
            cp = pltpu.make_async_copy(ins[k] if gather else ins[k].at[me], outs[k].at[me], local_sems.at[k])
            cp.start()
            local.append(cp)
        remote = []
        for p in range(1, N_DEV):
            px = 1 - x if p & 4 else x
            py = 1 - y if p & 2 else y
            pc = 1 - c if p & 1 else c
            for k in range(n):
                cp = pltpu.make_async_remote_copy(
                    src_ref=ins[k] if gather else ins[k].at[4 * px + 2 * py + pc],
                    dst_ref=outs[k].at[me],
                    send_sem=send_sems.at[k * peers + p - 1],
                    recv_sem=recv_sems.at[k * peers + p - 1],
                    device_id=(px, py, pc), device_id_type=pl.DeviceIdType.MESH)
                cp.start()
                remote.append(cp)
        for cp in remote:
            cp.wait()
        for cp in local:
            cp.wait()

    out_shape = tuple(
        jax.ShapeDtypeStruct(((N_DEV,) + a.shape) if gather else a.shape, a.dtype) for a in arrays)
    outs = pl.pallas_call(
        body, name=name, out_shape=out_shape,
        in_specs=[pl.BlockSpec(memory_space=pl.ANY)] * n,
        out_specs=tuple(pl.BlockSpec(memory_space=pl.ANY) for _ in range(n)),
        scratch_shapes=[pltpu.SemaphoreType.DMA((n * peers,)), pltpu.SemaphoreType.DMA((n * peers,)),
                        pltpu.SemaphoreType.DMA((n,))],
        compiler_params=pltpu.CompilerParams(has_side_effects=True),
    )(*arrays)
    return list(outs)


_HBM = pl.BlockSpec(memory_space=pltpu.HBM)
_SEM = pl.BlockSpec(memory_space=pltpu.SEMAPHORE)
_DATAFLOW = pltpu.SideEffectType.DATAFLOW_SIDE_EFFECTING


def _peer_copies(src_refs, land_refs, send_sems, recv_sems, gather):
    x, y, c = lax.axis_index("x"), lax.axis_index("y"), lax.axis_index("c")
    me = 4 * x + 2 * y + c
    peers = N_DEV - 1
    copies = []
    for p in range(1, N_DEV):
        px = 1 - x if p & 4 else x
        py = 1 - y if p & 2 else y
        pc = 1 - c if p & 1 else c
        for k in range(len(src_refs)):
            copies.append(pltpu.make_async_remote_copy(
                src_ref=src_refs[k] if gather else src_refs[k].at[4 * px + 2 * py + pc],
                dst_ref=land_refs[k].at[me],
                send_sem=send_sems.at[k * peers + p - 1], recv_sem=recv_sems.at[k * peers + p - 1],
                device_id=(px, py, pc), device_id_type=pl.DeviceIdType.MESH))
    return copies


def _landing(srcs, gather, me):
    out = []
    for a in srcs:
        own = a if gather else lax.dynamic_index_in_dim(a, me, 0, keepdims=False)
        out.append(lax.dynamic_update_index_in_dim(lax.empty((N_DEV,) + own.shape, own.dtype), own, me, 0))
    return out


def _send_start(name, srcs, gather, me):
    n = len(srcs)
    lands = _landing(srcs, gather, me)

    def body(*refs):
        src_refs, land_refs = refs[:n], refs[n:2 * n]
        send_sems, recv_sems, token = refs[2 * n], refs[2 * n + 1], refs[-1]
        for cp in _peer_copies(src_refs, land_refs, send_sems, recv_sems, gather):
            cp.start()
        token[...] = jnp.zeros_like(token)

    sems = pltpu.SemaphoreType.DMA((n * (N_DEV - 1),))
    outs = pl.pallas_call(
        body, name=name,
        out_shape=(sems, sems, *[pltpu.HBM(a.shape, a.dtype) for a in (*srcs, *lands)], jax.ShapeDtypeStruct((8, 128), f32)),
        in_specs=[_HBM] * (2 * n),
        out_specs=(_SEM, _SEM, *[_HBM] * (2 * n), pl.BlockSpec(memory_space=pltpu.VMEM)),
        input_output_aliases={k: 2 + k for k in range(2 * n)},
        compiler_params=pltpu.CompilerParams(has_side_effects=_DATAFLOW),
    )(*[pltpu.with_memory_space_constraint(a, pltpu.HBM) for a in (*srcs, *lands)])
    return (outs[0], outs[1], list(outs[2:2 + n]), list(outs[2 + n:2 + 2 * n]), gather), outs[-1]


def _send_wait(name, handle, after):
    send_sems, recv_sems, srcs, lands, gather = handle
    n = len(srcs)

    def body(*refs):
        src_refs, land_refs = refs[:n], refs[n:2 * n]
        for cp in _peer_copies(src_refs, land_refs, refs[2 * n], refs[2 * n + 1], gather):
            cp.wait_send()
            cp.wait_recv()

    outs = pl.pallas_call(
        body, name=name, out_shape=tuple(pltpu.HBM(a.shape, a.dtype) for a in (*srcs, *lands)),
        in_specs=[_HBM] * (2 * n) + [_SEM, _SEM, pl.BlockSpec(memory_space=pl.ANY)],
        out_specs=tuple([_HBM] * (2 * n)), input_output_aliases={k: k for k in range(2 * n)},
        compiler_params=pltpu.CompilerParams(has_side_effects=_DATAFLOW),
    )(*srcs, *lands, send_sems, recv_sems, after)
    return list(outs[n:])


def _mod_part(c_all, w_mod):
    depth, d, cols = w_mod.shape

    def body(c_ref, w_ref, o_ref):
        cv = c_ref[...]
        cond = cv * _sigmoid(cv)
        o_ref[...] = jnp.dot(cond, w_ref[...], preferred_element_type=f32, precision=lax.Precision.HIGHEST)

    return pl.pallas_call(
        body, name="mod_part", grid=(depth,),
        out_shape=jax.ShapeDtypeStruct((depth, N_DEV, cols), f32),
        in_specs=[pl.BlockSpec((N_DEV, d), lambda i: (0, 0)), pl.BlockSpec((None, d, cols), lambda i: (i, 0, 0))],
        out_specs=pl.BlockSpec((None, N_DEV, cols), lambda i: (i, 0, 0)),
        compiler_params=_params(("arbitrary",), 32),
    )(c_all, w_mod)


def _mod_table(mod_row, b_mod, g_mix, g_ffn):
    def body(m_ref, b_ref, gm_ref, gf_ref, o_ref, token_ref):
        for i in range(DEPTH):
            for k in range(N_MOD):
                o_ref[i, k:k + 1, :] = m_ref[i:i + 1, k * D_MODEL:(k + 1) * D_MODEL] + b_ref[i:i + 1, k * D_MODEL:(k + 1) * D_MODEL]
            o_ref[i, R_GS_M:R_GS_M + 1, :] = gm_ref[i:i + 1, :] * (1.0 + o_ref[i, R_SC_M:R_SC_M + 1, :])
            o_ref[i, R_GS_F:R_GS_F + 1, :] = gf_ref[i:i + 1, :] * (1.0 + o_ref[i, R_SC_F:R_SC_F + 1, :])
        token_ref[...] = jnp.zeros_like(token_ref)

    return pl.pallas_call(
        body, name="mod_table",
        out_shape=(jax.ShapeDtypeStruct((DEPTH, 8, D_MODEL), f32), jax.ShapeDtypeStruct((8, 128), f32)))(
        mod_row, b_mod, g_mix, g_ffn)


def _ffn_tile(s):
    return min(512, s)


def _layer_weights(shape):
    return pl.BlockSpec((N_DEV,) + shape, lambda i: (0, 0, 0))


def _ffn_fwd(x, vec, w1g, w2g, layer):
    s = x.shape[0]
    ts = _ffn_tile(s)

    def body(x_ref, vec_ref, w1_ref, w2_ref, xo_ref, u_ref, y_ref, hb_ref):
        xv = x_ref[...]
        n, _ = _rms(xv)
        hb = (n * vec_ref[R_GS_F:R_GS_F + 1, :] + vec_ref[R_SH_F:R_SH_F + 1, :]).astype(bf16)
        hb_ref[...] = hb
        yv = jnp.zeros((ts, D_MODEL), f32)
        for f in range(N_DEV):
            u = jnp.maximum(_mm(hb, w1_ref[f]), 0.0)
            u_ref[:, f * FF_CHUNK:(f + 1) * FF_CHUNK] = u.astype(bf16)
            yv = yv + _mm((u * u).astype(bf16), w2_ref[f])
        y_ref[...] = yv.astype(bf16)
        xo_ref[...] = xv + vec_ref[R_GT_F:R_GT_F + 1, :] * yv

    row = pl.BlockSpec((ts, D_MODEL), lambda i: (i, 0))
    return pl.pallas_call(
        body, name=f"ffn_fwd_{layer}", grid=(s // ts,),
        out_shape=(jax.ShapeDtypeStruct((s, D_MODEL), f32), jax.ShapeDtypeStruct((s, D_FF), bf16),
                   jax.ShapeDtypeStruct((s, D_MODEL), bf16), jax.ShapeDtypeStruct((s, D_MODEL), bf16)),
        in_specs=[row, pl.BlockSpec((8, D_MODEL), lambda i: (0, 0)),
                  _layer_weights((D_MODEL, FF_CHUNK)), _layer_weights((FF_CHUNK, D_MODEL))],
        out_specs=(row, pl.BlockSpec((ts, D_FF), lambda i: (i, 0)), row, row),
        compiler_params=_params(("arbitrary",), 56),
    )(x, vec, w1g, w2g)


def _ffn_bwd_act(x, dx, u, y, vec, w1g, w2g, layer):
    s = x.shape[0]
    ts = _ffn_tile(s)

    def body(x_ref, dx_ref, u_ref, y_ref, vec_ref, w1_ref, w2_ref, dxo_ref, da_ref, dyb_ref, sm_ref):
        @pl.when(pl.program_id(0) == 0)
        def _():
            sm_ref[...] = jnp.zeros_like(sm_ref)

        dxv = dx_ref[...]
        dyb = (dxv * vec_ref[R_GT_F:R_GT_F + 1, :]).astype(bf16)
        dyb_ref[...] = dyb
        sm_ref[G_GT:G_GT + 1, :] += _colsum(dxv * y_ref[...].astype(f32))
        dh = jnp.zeros((ts, D_MODEL), f32)
        for f in range(N_DEV):
            cols = slice(f * FF_CHUNK, (f + 1) * FF_CHUNK)
            dz = _mm_nt(dyb, w2_ref[f])
            dab = (dz * (2.0 * u_ref[:, cols].astype(f32))).astype(bf16)
            da_ref[:, cols] = dab
            dh = dh + _mm_nt(dab, w1_ref[f])
        n, r = _rms(x_ref[...])
        sm_ref[G_SH:G_SH + 1, :] += _colsum(dh)
        sm_ref[G_GS:G_GS + 1, :] += _colsum(dh * n)
        dxo_ref[...] = dxv + _norm_bwd(dh, n, r, vec_ref[R_GS_F:R_GS_F + 1, :])

    row = pl.BlockSpec((ts, D_MODEL), lambda i: (i, 0))
    wide = pl.BlockSpec((ts, D_FF), lambda i: (i, 0))
    return pl.pallas_call(
        body, name=f"ffn_bwd_act_{layer}", grid=(s // ts,),
        out_shape=(jax.ShapeDtypeStruct((s, D_MODEL), f32), jax.ShapeDtypeStruct((s, D_FF), bf16),
                   jax.ShapeDtypeStruct((s, D_MODEL), bf16), jax.ShapeDtypeStruct((8, D_MODEL), f32)),
        in_specs=[row, row, wide, row, pl.BlockSpec((8, D_MODEL), lambda i: (0, 0)),
                  _layer_weights((D_MODEL, FF_CHUNK)), _layer_weights((FF_CHUNK, D_MODEL))],
        out_specs=(row, wide, row, pl.BlockSpec((8, D_MODEL), lambda i: (0, 0))),
        compiler_params=_params(("arbitrary",), 58),
    )(x, dx, u, y, vec, w1g, w2g)


def _ffn_bwd_w1(hb, da, layer):
    s = hb.shape[0]
    ts = _ffn_tile(s)
    nt = s // ts

    def body(hb_ref, da_ref, dw_ref, acc_ref):
        i = pl.program_id(0)

        @pl.when(i == 0)
        def _():
            acc_ref[...] = jnp.zeros_like(acc_ref)

        hb = hb_ref[...]
        for f in range(N_DEV):
            acc_ref[f] += _mm_tn(hb, da_ref[:, f * FF_CHUNK:(f + 1) * FF_CHUNK])

        @pl.when(i == nt - 1)
        def _():
            dw_ref[...] = acc_ref[...].astype(bf16)

    return pl.pallas_call(
        body, name=f"ffn_bwd_w1_{layer}", grid=(nt,),
        out_shape=jax.ShapeDtypeStruct((N_DEV, D_MODEL, FF_CHUNK), bf16),
        in_specs=[pl.BlockSpec((ts, D_MODEL), lambda i: (i, 0)), pl.BlockSpec((ts, D_FF), lambda i: (i, 0))],
        out_specs=pl.BlockSpec((N_DEV, D_MODEL, FF_CHUNK), lambda i: (0, 0, 0)),
        scratch_shapes=[pltpu.VMEM((N_DEV, D_MODEL, FF_CHUNK), f32)],
        compiler_params=_params(("arbitrary",), 56),
    )(hb, da)


def _ffn_bwd_w2(u, dyb, layer):
    s = u.shape[0]
    ts = _ffn_tile(s)
    nt = s // ts

    def body(u_ref, dyb_ref, dw_ref, acc_ref):
        i = pl.program_id(0)

        @pl.when(i == 0)
        def _():
            acc_ref[...] = jnp.zeros_like(acc_ref)

        dyb = dyb_ref[...]
        for f in range(N_DEV):
            uv = u_ref[:, f * FF_CHUNK:(f + 1) * FF_CHUNK].astype(f32)
            acc_ref[f] += _mm_tn((uv * uv).astype(bf16), dyb)

        @pl.when(i == nt - 1)
        def _():
            dw_ref[...] = acc_ref[...].astype(bf16)

    return pl.pallas_call(
        body, name=f"ffn_bwd_w2_{layer}", grid=(nt,),
        out_shape=jax.ShapeDtypeStruct((N_DEV, FF_CHUNK, D_MODEL), bf16),
        in_specs=[pl.BlockSpec((ts, D_FF), lambda i: (i, 0)), pl.BlockSpec((ts, D_MODEL), lambda i: (i, 0))],
        out_specs=pl.BlockSpec((N_DEV, FF_CHUNK, D_MODEL), lambda i: (0, 0, 0)),
        scratch_shapes=[pltpu.VMEM((N_DEV, FF_CHUNK, D_MODEL), f32)],
        compiler_params=_params(("arbitrary",), 56),
    )(u, dyb)


def _lru_tile(s):
    return min(256, s)


def _lru_gates(xc, wsm_ref, pv_ref):
    xcb = xc.astype(bf16)
    gr = _sigmoid(_block_diag(xcb, wsm_ref, 0) + pv_ref[P_BA:P_BA + 1, :])
    gi = _sigmoid(_block_diag(xcb, wsm_ref, 1) + pv_ref[P_BX:P_BX + 1, :])
    log_a = (LRU_C * _log_sigmoid(pv_ref[P_LAM:P_LAM + 1, :])) * gr
    a = jnp.exp(log_a)
    return gr, gi, a, jnp.sqrt(_neg_expm1(2.0 * log_a, a * a))


def _conv(xr, taps_before, pv_ref):
    xc = xr * pv_ref[P_CW0 + 3:P_CW0 + 4, :] + pv_ref[P_CONVB:P_CONVB + 1, :]
    for k, v in zip((2, 1, 0), taps_before):
        xc = xc + v * pv_ref[P_CW0 + k:P_CW0 + k + 1, :]
    return xc


def _lru_fwd(x, vec, wbig, wsm, pvec, layer):
    s = x.shape[0]
    ts = _lru_tile(s)
    w = LRU_WIDTH

    def body(x_ref, vec_ref, wb_ref, wsm_ref, pv_ref, xo_ref, xr_ref, hs_ref, a_ref, mult_ref, gr_ref, gi_ref,
             gel_ref, geld_ref, y_ref, tail_ref, carry_ref, u_scr):
        @pl.when(pl.program_id(0) == 0)
        def _():
            tail_ref[...] = jnp.zeros_like(tail_ref)
            carry_ref[...] = jnp.zeros_like(carry_ref)

        xv = x_ref[...]
        n, _ = _rms(xv)
        hb = (n * vec_ref[R_GS_M:R_GS_M + 1, :] + vec_ref[R_SH_M:R_SH_M + 1, :]).astype(bf16)
        gelu_v, gelu_d = _gelu_and_grad(_mm(hb, wb_ref[0]) + pv_ref[P_BY:P_BY + 1, :])
        gel_ref[...] = gelu_v.astype(bf16)
        geld_ref[...] = gelu_d.astype(bf16)
        xr = _mm(hb, wb_ref[1]) + pv_ref[P_BIN:P_BIN + 1, :]
        xr_ref[...] = xr.astype(bf16)
        xc = _conv(xr, _rows_before(tail_ref[...], xr, (1, 2, 3)), pv_ref)
        tail_ref[...] = xr[ts - SUBLANES:, :]
        gr, gi, a, mult = _lru_gates(xc, wsm_ref, pv_ref)
        gr_ref[...] = gr.astype(bf16)
        gi_ref[...] = gi.astype(bf16)
        a_ref[...] = a
        mult_ref[...] = mult
        u_scr[...] = mult * (gi * xc)
        carry_ref[...] = _scan(a_ref, u_scr, hs_ref, carry_ref[...], reverse=False)
        m = hs_ref[...] * gelu_v
        yv = _mm(m.astype(bf16), wb_ref[2]) + pv_ref[P_BOUT:P_BOUT + 1, :]
        y_ref[...] = yv.astype(bf16)
        xo_ref[...] = xv + vec_ref[R_GT_M:R_GT_M + 1, :] * yv

    row = pl.BlockSpec((ts, D_MODEL), lambda i: (i, 0))
    roww = pl.BlockSpec((ts, w), lambda i: (i, 0))
    wide = lambda dt: jax.ShapeDtypeStruct((s, w), dt)
    return pl.pallas_call(
        body, name=f"lru_fwd_{layer}", grid=(s // ts,),
        out_shape=(jax.ShapeDtypeStruct((s, D_MODEL), f32), wide(bf16), wide(f32), wide(f32), wide(f32),
                   wide(bf16), wide(bf16), wide(bf16), wide(bf16), jax.ShapeDtypeStruct((s, D_MODEL), bf16)),
        in_specs=[row, pl.BlockSpec((8, D_MODEL), lambda i: (0, 0)),
                  pl.BlockSpec((3, w, w), lambda i: (0, 0, 0)),
                  pl.BlockSpec((2, HEADS, HEAD_DIM, HEAD_DIM), lambda i: (0, 0, 0, 0)),
                  pl.BlockSpec((16, w), lambda i: (0, 0))],
        out_specs=(row, roww, roww, roww, roww, roww, roww, roww, roww, row),
        scratch_shapes=[pltpu.VMEM((SUBLANES, w), f32), pltpu.VMEM((SUBLANES, w), f32), pltpu.VMEM((ts, w), f32)],
        compiler_params=_params(("arbitrary",)),
    )(x, vec, wbig, wsm, pvec)


LRU_BWD_SUB = 128
LRU_BWD_SUBS = 2


def _scan_rows(a, u, carry, reverse):
    groups = a.shape[0] // SUBLANES
    row = lax.broadcasted_iota(jnp.int32, (SUBLANES, a.shape[1]), 0)
    outs = [None] * groups
    for j in range(groups):
        g = groups - 1 - j if reverse else j
        av, uv = a[g * SUBLANES:(g + 1) * SUBLANES], u[g * SUBLANES:(g + 1) * SUBLANES]
        for k in (1, 2, 4):
            if reverse:
                valid, shift = row < SUBLANES - k, SUBLANES - k
            else:
                valid, shift = row >= k, k
            a_s = jnp.where(valid, pltpu.roll(av, shift, 0), 1.0)
            u_s = jnp.where(valid, pltpu.roll(uv, shift, 0), 0.0)
            uv = uv + av * u_s
            av = av * a_s
        h = uv + av * carry
        outs[g] = h
        carry = h[0:1, :] if reverse else h[SUBLANES - 1:SUBLANES, :]
    return jnp.concatenate(outs, axis=0), carry


def _lru_bwd(x, dx, saved, vec, wbig, wsm, pvec, layer):
    xr, hs, a_all, mult_all, gr_all, gi_all, gel_all, geld_all, y = saved
    s = x.shape[0]
    sub = min(LRU_BWD_SUB, s)
    ts = min(sub * LRU_BWD_SUBS, s)
    nsub = ts // sub
    nt = s // ts
    w = LRU_WIDTH
    shard = w // N_DEV
    hshard = HEAD_DIM // N_DEV

    def body(x_ref, dx_ref, xr_ref, xrh_ref, hs_ref, hsh_ref, a_ref, mult_ref, gr_ref, gi_ref, gel_ref, geld_ref,
             y_ref, vec_ref, wb_ref, wsm_ref, pv_ref,
             dxo_ref, dwb_ref, dwsm_ref, sm_ref, accb_ref, accs_ref, eps_ref, dxc8_ref,
             hb_scr, dgb_scr, dxrb_scr, mb_scr, dyb_scr, xcb_scr, drab_scr, drxb_scr):
        i = pl.program_id(0)
        first_tile = i == nt - 1

        @pl.when(i == 0)
        def _():
            accb_ref[...] = jnp.zeros_like(accb_ref)
            accs_ref[...] = jnp.zeros_like(accs_ref)
            sm_ref[...] = jnp.zeros_like(sm_ref)
            eps_ref[...] = jnp.zeros_like(eps_ref)
            dxc8_ref[...] = jnp.zeros_like(dxc8_ref)

        gs = vec_ref[R_GS_M:R_GS_M + 1, :]
        c_ls = LRU_C * _log_sigmoid(pv_ref[P_LAM:P_LAM + 1, :])
        for k in reversed(range(nsub)):
            rows = slice(k * sub, (k + 1) * sub)
            xv = x_ref[rows, :]
            dxv = dx_ref[rows, :]
            n, r = _rms(xv)
            hb_scr[rows, :] = (n * gs + vec_ref[R_SH_M:R_SH_M + 1, :]).astype(bf16)
            xrv = xr_ref[rows, :].astype(f32)
            hsv = hs_ref[rows, :]
            if k == 0:
                xr_halo = jnp.where(first_tile, 0.0, xrh_ref[...].astype(f32))
                hs_halo = jnp.where(first_tile, 0.0, hsh_ref[...])
            else:
                xr_halo = xr_ref[k * sub - BF16_ROWS:k * sub, :].astype(f32)
                hs_halo = hs_ref[k * sub - SUBLANES:k * sub, :]
            xs1, xs2, xs3 = _rows_before(xr_halo, xrv, (1, 2, 3))
            xc = _conv(xrv, (xs1, xs2, xs3), pv_ref)
            xcb_scr[rows, :] = xc.astype(bf16)
            a, mult = a_ref[rows, :], mult_ref[rows, :]
            gr, gi = gr_ref[rows, :].astype(f32), gi_ref[rows, :].astype(f32)
            gelu_v = gel_ref[rows, :].astype(f32)

            dy = dxv * vec_ref[R_GT_M:R_GT_M + 1, :]
            dyb = dy.astype(bf16)
            dyb_scr[rows, :] = dyb
            sm_ref[G_GT:G_GT + 1, :] += _colsum(dxv * y_ref[rows, :].astype(f32))
            sm_ref[G_BOUT:G_BOUT + 1, :] += _colsum(dy)
            mb_scr[rows, :] = (hsv * gelu_v).astype(bf16)
            dm = _mm_nt(dyb, wb_ref[2])
            dhs = dm * gelu_v
            dgpre = dm * hsv * geld_ref[rows, :].astype(f32)
            dgb = dgpre.astype(bf16)
            dgb_scr[rows, :] = dgb
            sm_ref[G_BY:G_BY + 1, :] += _colsum(dgpre)

            eps_in = eps_ref[0:1, :]
            eps, eps_out = _scan_rows(a, a * dhs, eps_in, reverse=True)
            eps_ref[0:1, :] = eps_out
            (eps_next,) = _rows_after(eps, jnp.broadcast_to(eps_in, (SUBLANES, w)), (1,))
            delta = dhs + eps_next
            (h_prev,) = _rows_before(hs_halo, hsv, (1,))
            dxi = delta * xc
            dgi = dxi * mult
            dla = (delta * h_prev) * a - (dxi * gi) * (a * a) / mult
            sm_ref[G_LS:G_LS + 1, :] += _colsum(dla * gr)
            dra = (dla * c_ls) * (gr - gr * gr)
            drx = dgi * (gi - gi * gi)
            drab, drxb = dra.astype(bf16), drx.astype(bf16)
            drab_scr[rows, :] = drab
            drxb_scr[rows, :] = drxb
            sm_ref[G_BA:G_BA + 1, :] += _colsum(dra)
            sm_ref[G_BX:G_BX + 1, :] += _colsum(drx)
            dxc = (delta * mult) * gi + _block_diag_t(drab, wsm_ref, 0) + _block_diag_t(drxb, wsm_ref, 1)

            sm_ref[G_CONVB:G_CONVB + 1, :] += _colsum(dxc)
            for kk, v in zip((3, 2, 1, 0), (xrv, xs1, xs2, xs3)):
                sm_ref[G_CW0 + kk:G_CW0 + kk + 1, :] += _colsum(dxc * v)
            ups = _rows_after(dxc, dxc8_ref[...], (1, 2, 3))
            dxc8_ref[...] = dxc[0:SUBLANES, :]
            dxr = dxc * pv_ref[P_CW0 + 3:P_CW0 + 4, :]
            for kk, v in zip((2, 1, 0), ups):
                dxr = dxr + v * pv_ref[P_CW0 + kk:P_CW0 + kk + 1, :]
            dxrb = dxr.astype(bf16)
            dxrb_scr[rows, :] = dxrb
            sm_ref[G_BIN:G_BIN + 1, :] += _colsum(dxr)
            dh = _mm_nt(dgb, wb_ref[0]) + _mm_nt(dxrb, wb_ref[1])
            sm_ref[G_SH:G_SH + 1, :] += _colsum(dh)
            sm_ref[G_GS:G_GS + 1, :] += _colsum(dh * n)
            dxo_ref[rows, :] = dxv + _norm_bwd(dh, n, r, gs)

        hb = hb_scr[...]
        accb_ref[0] += _mm_tn(hb, dgb_scr[...])
        accb_ref[1] += _mm_tn(hb, dxrb_scr[...])
        accb_ref[2] += _mm_tn(mb_scr[...], dyb_scr[...])
        for h in range(HEADS):
            cols = slice(h * HEAD_DIM, (h + 1) * HEAD_DIM)
            accs_ref[0, h] += _mm_tn(xcb_scr[:, cols], drab_scr[:, cols])
            accs_ref[1, h] += _mm_tn(xcb_scr[:, cols], drxb_scr[:, cols])

        @pl.when(i == nt - 1)
        def _():
            sm_ref[G_LS:G_LS + 1, :] = sm_ref[G_LS:G_LS + 1, :] * LRU_C
            for k in range(3):
                dwb_ref[:, k] = accb_ref[k].astype(bf16).reshape(N_DEV, shard, w)
            for k in range(2):
                for h in range(HEADS):
                    dwsm_ref[:, k, h] = accs_ref[k, h].astype(bf16).reshape(N_DEV, hshard, HEAD_DIM)

    rev = lambda i: (nt - 1 - i, 0)
    row = pl.BlockSpec((ts, D_MODEL), rev)
    roww = pl.BlockSpec((ts, w), rev)
    halo16 = pl.BlockSpec((BF16_ROWS, w), lambda i: (jnp.maximum((nt - 1 - i) * (ts // BF16_ROWS) - 1, 0), 0))
    halo8 = pl.BlockSpec((SUBLANES, w), lambda i: (jnp.maximum((nt - 1 - i) * (ts // SUBLANES) - 1, 0), 0))
    const = lambda *shape: pl.BlockSpec(shape, lambda i: (0,) * len(shape))
    operand = pltpu.VMEM((ts, w), bf16)
    return pl.pallas_call(
        body, name=f"lru_bwd_{layer}", grid=(nt,),
        out_shape=(jax.ShapeDtypeStruct((s, D_MODEL), f32),
                   jax.ShapeDtypeStruct((N_DEV, 3, shard, w), bf16),
                   jax.ShapeDtypeStruct((N_DEV, 2, HEADS, hshard, HEAD_DIM), bf16),
                   jax.ShapeDtypeStruct((16, w), f32)),
        in_specs=[row, row, roww, halo16, roww, halo8, roww, roww, roww, roww, roww, roww, row, const(8, D_MODEL),
                  const(3, w, w), const(2, HEADS, HEAD_DIM, HEAD_DIM), const(16, w)],
        out_specs=(row, const(N_DEV, 3, shard, w), const(N_DEV, 2, HEADS, hshard, HEAD_DIM), const(16, w)),
        scratch_shapes=[pltpu.VMEM((3, w, w), f32), pltpu.VMEM((2, HEADS, HEAD_DIM, HEAD_DIM), f32),
                        pltpu.VMEM((SUBLANES, w), f32), pltpu.VMEM((SUBLANES, w), f32)] + [operand] * 8,
        compiler_params=_params(("arbitrary",), 58),
    )(x, dx, xr, xr, hs, hs, a_all, mult_all, gr_all, gi_all, gel_all, geld_all, y, vec, wbig, wsm, pvec)


def _pool_tile(s):
    return min(256, s)


def _pool_counts(tile_index, ts):
    t = (tile_index * ts + lax.broadcasted_iota(jnp.int32, (ts, 1), 0) + 1).astype(f32)
    return [1.0 / jnp.minimum(t, float(win)) for win in POOL_WINDOWS]


def _pooled(h, halo, inv):
    ext = jnp.concatenate([halo, h], axis=0)
    out = []
    for g in range(len(POOL_WINDOWS)):
        acc = ext[:, g * HEAD_DIM:(g + 1) * HEAD_DIM]
        for step in range(g + 1):
            acc = acc + pltpu.roll(acc, 1 << step, 0)
        out.append(acc[POOL_HALO:] * inv[g] - h[:, g * HEAD_DIM:(g + 1) * HEAD_DIM])
    return out


def _pool_fwd(x, vec, pw, ps, layer):
    s = x.shape[0]
    ts = _pool_tile(s)

    def body(x_ref, vec_ref, pw_ref, ps_ref, xo_ref, y_ref, halo_ref):
        i = pl.program_id(0)

        @pl.when(i == 0)
        def _():
            halo_ref[...] = jnp.zeros_like(halo_ref)

        xv = x_ref[...]
        n, _ = _rms(xv)
        h = n * vec_ref[R_GS_M:R_GS_M + 1, :] + vec_ref[R_SH_M:R_SH_M + 1, :]
        pooled = _pooled(h, halo_ref[...], _pool_counts(i, ts))
        halo_ref[...] = h[ts - POOL_HALO:, :]
        mixed = jnp.concatenate([_mm(pooled[g].astype(bf16), pw_ref[g]) for g in range(HEADS)], axis=1)
        yv = mixed * ps_ref[0:1, :]
        y_ref[...] = yv.astype(bf16)
        xo_ref[...] = xv + vec_ref[R_GT_M:R_GT_M + 1, :] * yv

    row = pl.BlockSpec((ts, D_MODEL), lambda i: (i, 0))
    return pl.pallas_call(
        body, name=f"pool_fwd_{layer}", grid=(s // ts,),
        out_shape=(jax.ShapeDtypeStruct((s, D_MODEL), f32), jax.ShapeDtypeStruct((s, D_MODEL), bf16)),
        in_specs=[row, pl.BlockSpec((8, D_MODEL), lambda i: (0, 0)),
                  pl.BlockSpec((HEADS, HEAD_DIM, HEAD_DIM), lambda i: (0, 0, 0)),
                  pl.BlockSpec((8, D_MODEL), lambda i: (0, 0))],
        out_specs=(row, row),
        scratch_shapes=[pltpu.VMEM((POOL_HALO, D_MODEL), f32)],
        compiler_params=_params(("arbitrary",)),
    )(x, vec, pw, ps)


def _pool_bwd(x, dx, y, vec, pw, ps, layer):
    s = x.shape[0]
    ts = _pool_tile(s)
    nt = s // ts
    hshard = HEAD_DIM // N_DEV

    def body(x_ref, xh_ref, dx_ref, y_ref, vec_ref, pw_ref, ps_ref, dxo_ref, dpw_ref, sm_ref, acc_ref, q16_ref):
        i = pl.program_id(0)
        tile = nt - 1 - i

        @pl.when(i == 0)
        def _():
            acc_ref[...] = jnp.zeros_like(acc_ref)
            sm_ref[...] = jnp.zeros_like(sm_ref)
            q16_ref[...] = jnp.zeros_like(q16_ref)

        gs, sh = vec_ref[R_GS_M:R_GS_M + 1, :], vec_ref[R_SH_M:R_SH_M + 1, :]
        xv = x_ref[...]
        dxv = dx_ref[...]
        n, r = _rms(xv)
        h = n * gs + sh
        nh, _ = _rms(xh_ref[...])
        halo = jnp.where(tile == 0, 0.0, nh * gs + sh)
        inv = _pool_counts(tile, ts)
        pooled = _pooled(h, halo, inv)
        mixed = jnp.concatenate([_mm(pooled[g].astype(bf16), pw_ref[g]) for g in range(HEADS)], axis=1)

        dy = dxv * vec_ref[R_GT_M:R_GT_M + 1, :]
        sm_ref[G_GT:G_GT + 1, :] += _colsum(dxv * y_ref[...].astype(f32))
        sm_ref[3:4, :] += _colsum(dy * mixed)
        dmixed = (dy * ps_ref[0:1, :]).astype(bf16)
        dh_parts = []
        for g in range(HEADS):
            cols = slice(g * HEAD_DIM, (g + 1) * HEAD_DIM)
            acc_ref[g] += _mm_tn(pooled[g].astype(bf16), dmixed[:, cols])
            dpooled = _mm_nt(dmixed[:, cols], pw_ref[g])
            q = dpooled * inv[g]
            ext = jnp.concatenate([q, q16_ref[:, cols]], axis=0)
            q16_ref[:, cols] = q[0:POOL_HALO, :]
            for step in range(g + 1):
                ext = ext + pltpu.roll(ext, ext.shape[0] - (1 << step), 0)
            dh_parts.append(ext[:ts] - dpooled)
        dh = jnp.concatenate(dh_parts, axis=1)
        sm_ref[G_SH:G_SH + 1, :] += _colsum(dh)
        sm_ref[G_GS:G_GS + 1, :] += _colsum(dh * n)
        dxo_ref[...] = dxv + _norm_bwd(dh, n, r, gs)

        @pl.when(i == nt - 1)
        def _():
            for g in range(HEADS):
                dpw_ref[:, g] = acc_ref[g].astype(bf16).reshape(N_DEV, hshard, HEAD_DIM)

    rev = lambda i: (nt - 1 - i, 0)
    row = pl.BlockSpec((ts, D_MODEL), rev)
    halo16 = pl.BlockSpec((POOL_HALO, D_MODEL), lambda i: (jnp.maximum((nt - 1 - i) * (ts // POOL_HALO) - 1, 0), 0))
    const = lambda *shape: pl.BlockSpec(shape, lambda i: (0,) * len(shape))
    return pl.pallas_call(
        body, name=f"pool_bwd_{layer}", grid=(nt,),
        out_shape=(jax.ShapeDtypeStruct((s, D_MODEL), f32),
                   jax.ShapeDtypeStruct((N_DEV, HEADS, hshard, HEAD_DIM), bf16),
                   jax.ShapeDtypeStruct((8, D_MODEL), f32)),
        in_specs=[row, halo16, row, row, const(8, D_MODEL), const(HEADS, HEAD_DIM, HEAD_DIM), const(8, D_MODEL)],
        out_specs=(row, const(N_DEV, HEADS, hshard, HEAD_DIM), const(8, D_MODEL)),
        scratch_shapes=[pltpu.VMEM((HEADS, HEAD_DIM, HEAD_DIM), f32), pltpu.VMEM((POOL_HALO, D_MODEL), f32)],
        compiler_params=_params(("arbitrary",)),
    )(x, x, dx, y, vec, pw, ps)


def _final(x, target, g_fin):
    s = x.shape[0]
    ts = min(512, s)

    def body(x_ref, t_ref, g_ref, dx_ref, loss_ref, sm_ref):
        @pl.when(pl.program_id(0) == 0)
        def _():
            loss_ref[...] = jnp.zeros_like(loss_ref)
            sm_ref[...] = jnp.zeros_like(sm_ref)

        g = g_ref[0:1, :]
        n, r = _rms(x_ref[...])
        err = n * g - t_ref[...]
        loss_ref[...] += 0.5 * jnp.sum(jnp.mean(err * err, axis=-1, keepdims=True), axis=0, keepdims=True)
        dyv = err * (1.0 / D_MODEL)
        sm_ref[0:1, :] += _colsum(dyv * n)
        dx_ref[...] = _norm_bwd(dyv, n, r, g)

    row = pl.BlockSpec((ts, D_MODEL), lambda i: (i, 0))
    return pl.pallas_call(
        body, name="final_loss", grid=(s // ts,),
        out_shape=(jax.ShapeDtypeStruct((s, D_MODEL), f32), jax.ShapeDtypeStruct((8, 128), f32),
                   jax.ShapeDtypeStruct((8, D_MODEL), f32)),
        in_specs=[row, row, pl.BlockSpec((8, D_MODEL), lambda i: (0, 0))],
        out_specs=(row, pl.BlockSpec((8, 128), lambda i: (0, 0)), pl.BlockSpec((8, D_MODEL), lambda i: (0, 0))),
        compiler_params=_params(("arbitrary",)),
    )(x, target, g_fin)


def _small_pack(sm_ffn, sm_mix, sm_fin, table, g_mix, g_ffn, lam):
    def body(*refs):
        ffn, mix = refs[0:DEPTH], refs[DEPTH:2 * DEPTH]
        fin_ref, tab_ref, gm_ref, gf_ref, lam_ref, o_ref = refs[2 * DEPTH:]
        o_ref[...] = jnp.zeros_like(o_ref)
        for i in range(DEPTH):
            base = K_MOD + i * N_MOD
            o_ref[base + 0:base + 1, :] = mix[i][G_SH:G_SH + 1, :]
            o_ref[base + 1:base + 2, :] = mix[i][G_GS:G_GS + 1, :] * gm_ref[i:i + 1, :]
            o_ref[base + 2:base + 3, :] = mix[i][G_GT:G_GT + 1, :]
            o_ref[base + 3:base + 4, :] = ffn[i][G_SH:G_SH + 1, :]
            o_ref[base + 4:base + 5, :] = ffn[i][G_GS:G_GS + 1, :] * gf_ref[i:i + 1, :]
            o_ref[base + 5:base + 6, :] = ffn[i][G_GT:G_GT + 1, :]
            o_ref[K_NMIX + i:K_NMIX + i + 1, :] = mix[i][G_GS:G_GS + 1, :] * (1.0 + tab_ref[i, R_SC_M:R_SC_M + 1, :])
            o_ref[K_NFFN + i:K_NFFN + i + 1, :] = ffn[i][G_GS:G_GS + 1, :] * (1.0 + tab_ref[i, R_SC_F:R_SC_F + 1, :])
            j = i // 2
            if i % 2 == 0:
                for k, src in enumerate((G_BY, G_BIN, G_CONVB, None, G_BOUT)):
                    dst = K_LRUB + j * 5 + k
                    if src is None:
                        o_ref[dst:dst + 1, :] = mix[i][G_LS:G_LS + 1, :] * _sigmoid(-lam_ref[j:j + 1, :])
                    else:
                        o_ref[dst:dst + 1, :] = mix[i][src:src + 1, :]
                o_ref[K_CONVW + j * 4:K_CONVW + j * 4 + 4, :] = mix[i][G_CW0:G_CW0 + 4, :]
                o_ref[K_BA + j:K_BA + j + 1, :] = mix[i][G_BA:G_BA + 1, :]
                o_ref[K_BX + j:K_BX + j + 1, :] = mix[i][G_BX:G_BX + 1, :]
            else:
                o_ref[K_PS + j:K_PS + j + 1, :] = mix[i][3:4, :]
        o_ref[K_FIN:K_FIN + 1, :] = fin_ref[0:1, :]

    return pl.pallas_call(body, name="small_pack", out_shape=jax.ShapeDtypeStruct((K_ROWS, D_MODEL), f32))(
        *sm_ffn, *sm_mix, sm_fin, table, g_mix, g_ffn, lam)


def _small_sum(gathered):
    def body(g_ref, o_ref, token_ref):
        tot = g_ref[0]
        for src in range(1, N_DEV):
            tot = tot + g_ref[src]
        o_ref[...] = tot
        token_ref[...] = jnp.zeros_like(token_ref)

    return pl.pallas_call(
        body, name="small_sum",
        out_shape=(jax.ShapeDtypeStruct(gathered.shape[1:], f32), jax.ShapeDtypeStruct((8, 128), f32)))(gathered)


def _adamw_math(g, w, m, v):
    m = ADAM_B1 * m + (1.0 - ADAM_B1) * g
    v = ADAM_B2 * v + (1.0 - ADAM_B2) * (g * g)
    m_hat = m / (1.0 - ADAM_B1 ** ADAM_STEP)
    v_hat = v / (1.0 - ADAM_B2 ** ADAM_STEP)
    delta = -ADAM_LR * (m_hat / (jnp.sqrt(v_hat) + ADAM_EPS) + ADAM_WD * w)
    return delta, m, v


def _adamw_small(name, g, w, m, v):
    shape = w.shape
    two_d = (1, shape[0]) if len(shape) == 1 else (math.prod(shape[:-1]), shape[-1])

    def body(g_ref, w_ref, m_ref, v_ref, d_ref, mo_ref, vo_ref):
        d_ref[...], mo_ref[...], vo_ref[...] = _adamw_math(g_ref[...], w_ref[...], m_ref[...], v_ref[...])

    outs = pl.pallas_call(body, name=f"adamw_{name}", out_shape=tuple(jax.ShapeDtypeStruct(two_d, f32) for _ in range(3)))(
        *(t.reshape(two_d) for t in (g, w, m, v)))
    return tuple(t.reshape(shape) for t in outs)


def _block_rows(rows, cols):
    tr = max(SUBLANES, min(rows, (512 * 1024) // (4 * cols)))
    while rows % tr:
        tr //= 2
    return tr


def _adamw_reduce(name, landings, kind, w, m, v):
    nl = len(landings)
    rows, cols = landings[0].shape[2:]
    tr = _block_rows(rows, cols)
    per_layer = rows // tr

    def body(*refs):
        l_refs = refs[:nl]
        w_ref, m_ref, v_ref, g_ref, d_ref, mo_ref, vo_ref = refs[nl:]
        layer = pl.program_id(0)
        for k in range(nl):
            @pl.when(layer == k)
            def _(k=k):
                g = l_refs[k][0].astype(f32)
                for src in range(1, N_DEV):
                    g = g + l_refs[k][src].astype(f32)
                g_ref[...] = g
        d_ref[...], mo_ref[...], vo_ref[...] = _adamw_math(g_ref[...], w_ref[...], m_ref[...], v_ref[...])

    blk = pl.BlockSpec((tr, cols), lambda l, r: (l * per_layer + r, 0))
    land = [pl.BlockSpec((N_DEV, None, tr, cols), lambda l, r, k=k: (0, kind, jnp.where(l == k, r, 0), 0)) for k in range(nl)]
    return pl.pallas_call(
        body, name=f"adamw_{name}", grid=(nl, per_layer),
        out_shape=tuple(jax.ShapeDtypeStruct((nl * rows, cols), f32) for _ in range(4)),
        in_specs=land + [blk, blk, blk],
        out_specs=(blk, blk, blk, blk),
        compiler_params=_params(("arbitrary", "arbitrary"), 32),
    )(*landings, w, m, v)


def _adamw_w_mod(c_all, dmod_all, w, m, v):
    depth, d, cols = w.shape
    tr = 256

    def body(c_ref, dm_ref, w_ref, m_ref, v_ref, g_ref, d_ref, mo_ref, vo_ref):
        cv = c_ref[...]
        cond = cv * _sigmoid(cv)
        g = lax.dot_general(cond, dm_ref[...], (((0,), (0,)), ((), ())), preferred_element_type=f32,
                            precision=lax.Precision.HIGHEST)
        g_ref[...] = g
        d_ref[...], mo_ref[...], vo_ref[...] = _adamw_math(g, w_ref[...], m_ref[...], v_ref[...])

    blk = pl.BlockSpec((None, tr, cols), lambda i, r: (i, r, 0))
    return pl.pallas_call(
        body, name="adamw_w_mod", grid=(depth, d // tr),
        out_shape=tuple(jax.ShapeDtypeStruct(w.shape, f32) for _ in range(4)),
        in_specs=[pl.BlockSpec((N_DEV, tr), lambda i, r: (0, r)),
                  pl.BlockSpec((None, N_DEV, cols), lambda i, r: (i, 0, 0)), blk, blk, blk],
        out_specs=(blk, blk, blk, blk),
        compiler_params=_params(("arbitrary", "arbitrary"), 32),
    )(c_all, dmod_all, w, m, v)


def kernel(x, c, w_mod, b_mod, norm_mix_g, norm_ffn_g, lru_w_y, lru_b_y, lru_w_in, lru_b_in, lru_conv_w, lru_conv_b, lru_w_a, lru_b_a, lru_w_x, lru_b_x, lru_lambda, lru_w_out, lru_b_out, pool_w, pool_scale, ffn_w1, ffn_w2, final_norm_g, loss_target, m_w_mod, m_b_mod, m_norm_mix_g, m_norm_ffn_g, m_lru_w_y, m_lru_b_y, m_lru_w_in, m_lru_b_in, m_lru_conv_w, m_lru_conv_b, m_lru_w_a, m_lru_b_a, m_lru_w_x, m_lru_b_x, m_lru_lambda, m_lru_w_out, m_lru_b_out, m_pool_w, m_pool_scale, m_ffn_w1, m_ffn_w2, m_final_norm_g, v_w_mod, v_b_mod, v_norm_mix_g, v_norm_ffn_g, v_lru_w_y, v_lru_b_y, v_lru_w_in, v_lru_b_in, v_lru_conv_w, v_lru_conv_b, v_lru_w_a, v_lru_b_a, v_lru_w_x, v_lru_b_x, v_lru_lambda, v_lru_w_out, v_lru_b_out, v_pool_w, v_pool_scale, v_ffn_w1, v_ffn_w2, v_final_norm_g):
    me = 4 * lax.axis_index("x") + 2 * lax.axis_index("y") + lax.axis_index("c")
    n_lru = lru_w_y.shape[0]
    shard = LRU_WIDTH // N_DEV
    hshard = HEAD_DIM // N_DEV
    xs = x[0]
    target = loss_target[0]

    small_vecs = jnp.concatenate([
        lru_conv_w.reshape(n_lru * 4, shard), lru_b_a.reshape(n_lru, HEADS * hshard),
        lru_b_x.reshape(n_lru, HEADS * hshard), pool_scale, jnp.zeros((2, shard), f32)], axis=0)
    first_mix, token = _send_start("gather_mix_start_0", [jnp.stack([lru_w_y[0], lru_w_in[0], lru_w_out[0]]).astype(bf16),
                                                          jnp.stack([lru_w_a[0], lru_w_x[0]]).astype(bf16)], True, me)
    sv_g, c_g = _exchange([small_vecs + token[0, 0], c], True, "gather_cond")
    conv_w_full = sv_g[:, 0:8].reshape(N_DEV, n_lru, 4, shard).transpose(1, 2, 0, 3).reshape(n_lru, 4, LRU_WIDTH)
    b_a_full = sv_g[:, 8:10].reshape(N_DEV, n_lru, HEADS, hshard).transpose(1, 2, 0, 3).reshape(n_lru, LRU_WIDTH)
    b_x_full = sv_g[:, 10:12].reshape(N_DEV, n_lru, HEADS, hshard).transpose(1, 2, 0, 3).reshape(n_lru, LRU_WIDTH)
    ps_full = sv_g[:, 12:14].transpose(1, 0, 2).reshape(n_lru, D_MODEL)
    c_all = c_g.reshape(N_DEV, D_MODEL)

    (mod_g,) = _exchange([_mod_part(c_all, w_mod)], True, "gather_mod")
    mod_row = lax.dynamic_index_in_dim(mod_g, me, axis=2, keepdims=False)
    mod_row = mod_row.transpose(1, 0, 2).reshape(DEPTH, N_MOD * D_MODEL)
    table, token = _mod_table(mod_row, b_mod, norm_mix_g, norm_ffn_g)

    h_mix, h_ffn = [first_mix], []
    for i in range(DEPTH):
        j = i // 2
        if i > 0:
            if i % 2 == 0:
                mix_src = [(jnp.stack([lru_w_y[j], lru_w_in[j], lru_w_out[j]]) + token[0, 0]).astype(bf16),
                           (jnp.stack([lru_w_a[j], lru_w_x[j]]) + token[0, 0]).astype(bf16)]
            else:
                mix_src = [(pool_w[j] + token[0, 0]).astype(bf16)]
            handle, token = _send_start(f"gather_mix_start_{i}", mix_src, True, me)
            h_mix.append(handle)
        handle, token = _send_start(f"gather_ffn_start_{i}", [(ffn_w1[i] + token[0, 0]).astype(bf16),
                                                               (ffn_w2[i] + token[0, 0]).astype(bf16)], True, me)
        h_ffn.append(handle)

    zero_row = jnp.zeros((1, LRU_WIDTH), f32)
    pvecs = [jnp.concatenate([lru_b_y[j:j + 1], lru_b_in[j:j + 1], lru_conv_b[j:j + 1], b_a_full[j:j + 1],
                              b_x_full[j:j + 1], lru_lambda[j:j + 1], lru_b_out[j:j + 1], zero_row,
                              conv_w_full[j], zero_row, zero_row, zero_row, zero_row], axis=0) for j in range(n_lru)]
    ps_rows = [jnp.concatenate([ps_full[j:j + 1], jnp.zeros((7, D_MODEL), f32)], axis=0) for j in range(n_lru)]

    saved = []
    ffn_w, mix_w = [], []
    h = xs
    for i in range(DEPTH):
        j = i // 2
        got = _send_wait(f"gather_mix_wait_{i}", h_mix[i], h)
        if i % 2 == 0:
            mix_w.append((got[0].transpose(1, 0, 2, 3).reshape(3, LRU_WIDTH, LRU_WIDTH),
                          got[1].transpose(1, 2, 0, 3, 4).reshape(2, HEADS, HEAD_DIM, HEAD_DIM)))
            h_mid, *lru_saved = _lru_fwd(h, table[i], mix_w[i][0], mix_w[i][1], pvecs[j], i)
            mix_saved = (h, tuple(lru_saved))
        else:
            mix_w.append((got[0].transpose(1, 0, 2, 3).reshape(HEADS, HEAD_DIM, HEAD_DIM),))
            h_mid, y_mix = _pool_fwd(h, table[i], mix_w[i][0], ps_rows[j], i)
            mix_saved = (h, y_mix)
        ffn_w.append(_send_wait(f"gather_ffn_wait_{i}", h_ffn[i], h_mid))
        h_out, u, y_ffn, hb = _ffn_fwd(h_mid, table[i], ffn_w[i][0], ffn_w[i][1], i)
        saved.append((mix_saved, (h_mid, u, y_ffn, hb)))
        h = h_out
    fin_rows = jnp.concatenate([final_norm_g[None, :], jnp.zeros((7, D_MODEL), f32)], axis=0)
    dx, loss_part, sm_fin = _final(h, target, fin_rows)
    loss = lax.psum(loss_part[0, 0], ("x", "y", "c"))

    sm_ffn, sm_mix = [None] * DEPTH, [None] * DEPTH
    x_ffn, x_mix = [None] * DEPTH, [None] * DEPTH
    token = jnp.zeros((8, 128), f32)
    for i in reversed(range(DEPTH)):
        j = i // 2
        mix_saved, (h_mid, u, y_ffn, hb) = saved[i]
        dx, da, dyb, sm_ffn[i] = _ffn_bwd_act(h_mid, dx, u, y_ffn, table[i] + token[0, 0], ffn_w[i][0], ffn_w[i][1], i)
        x_ffn[i], token = _send_start(f"grads_ffn_start_{i}", [_ffn_bwd_w1(hb, da, i), _ffn_bwd_w2(u, dyb, i)], False, me)
        if i % 2 == 0:
            h_in, lru_saved = mix_saved
            dx, dbig, dsmall, sm_mix[i] = _lru_bwd(
                h_in, dx, lru_saved, table[i] + token[0, 0], mix_w[i][0], mix_w[i][1], pvecs[j], i)
            last_mix = [dbig, dsmall]
        else:
            h_in, y_mix = mix_saved
            dx, dpool, sm = _pool_bwd(h_in, dx, y_mix, table[i] + token[0, 0], mix_w[i][0], ps_rows[j], i)
            sm_mix[i] = jnp.concatenate([sm, jnp.zeros((8, D_MODEL), f32)], axis=0)
            last_mix = [dpool]
        if i > 0:
            x_mix[i], token = _send_start(f"grads_mix_start_{i}", last_mix, False, me)
    grad_x = dx[None]

    pack = _small_pack(sm_ffn, sm_mix, sm_fin, table + token[0, 0], norm_mix_g, norm_ffn_g, lru_lambda)
    (pack_g,) = _exchange([pack], True, "gather_small_grads")
    tot, token = _small_sum(pack_g)
    x_mix[0], _ = _send_start("grads_mix_start_0", [t + token[0, 0].astype(bf16) for t in last_mix], False, me)
    cols = w_mod.shape[2]
    dmod_all = lax.dynamic_slice_in_dim(pack_g[:, K_MOD:K_MOD + DEPTH * N_MOD].reshape(N_DEV, DEPTH, N_MOD * D_MODEL),
                                        me * cols, cols, axis=2).transpose(1, 0, 2)
    results = {"w_mod": _adamw_w_mod(c_all, dmod_all, w_mod, m_w_mod, v_w_mod)}

    after = results["w_mod"][1]
    l_ffn = [_send_wait(f"grads_ffn_wait_{i}", x_ffn[i], after) for i in reversed(range(DEPTH))][::-1]

    def reduce_update(name, landings, kind, w, m, v):
        rows = w.size // w.shape[-1]
        two_d = (rows, w.shape[-1])
        lands = [t.reshape(N_DEV, -1, rows // len(landings), w.shape[-1]) for t in landings]
        outs = _adamw_reduce(name, lands, kind, w.reshape(two_d), m.reshape(two_d), v.reshape(two_d))
        return tuple(t.reshape(w.shape) for t in outs)

    results["ffn_w1"] = reduce_update("ffn_w1", [t[0] for t in l_ffn], 0, ffn_w1, m_ffn_w1, v_ffn_w1)
    results["ffn_w2"] = reduce_update("ffn_w2", [t[1] for t in l_ffn], 0, ffn_w2, m_ffn_w2, v_ffn_w2)
    after = results["ffn_w2"][1]
    l_mix = [_send_wait(f"grads_mix_wait_{i}", x_mix[i], after) for i in reversed(range(DEPTH))][::-1]
    l_lru_big = [l_mix[i][0] for i in range(0, DEPTH, 2)]
    l_lru_small = [l_mix[i][1] for i in range(0, DEPTH, 2)]
    l_pool = [l_mix[i][0] for i in range(1, DEPTH, 2)]
    results["lru_w_y"] = reduce_update("lru_w_y", l_lru_big, 0, lru_w_y, m_lru_w_y, v_lru_w_y)
    results["lru_w_in"] = reduce_update("lru_w_in", l_lru_big, 1, lru_w_in, m_lru_w_in, v_lru_w_in)
    results["lru_w_out"] = reduce_update("lru_w_out", l_lru_big, 2, lru_w_out, m_lru_w_out, v_lru_w_out)
    results["lru_w_a"] = reduce_update("lru_w_a", l_lru_small, 0, lru_w_a, m_lru_w_a, v_lru_w_a)
    results["lru_w_x"] = reduce_update("lru_w_x", l_lru_small, 1, lru_w_x, m_lru_w_x, v_lru_w_x)
    results["pool_w"] = reduce_update("pool_w", l_pool, 0, pool_w, m_pool_w, v_pool_w)

    def my_cols(full, width):
        return lax.dynamic_slice_in_dim(full, me * width, width, axis=full.ndim - 1)

    lru_rows = tot[K_LRUB:K_LRUB + 5 * n_lru].reshape(n_lru, 5, LRU_WIDTH)
    small_grads = {
        "b_mod": tot[K_MOD:K_MOD + DEPTH * N_MOD].reshape(DEPTH, N_MOD * D_MODEL),
        "norm_mix_g": tot[K_NMIX:K_NMIX + DEPTH],
        "norm_ffn_g": tot[K_NFFN:K_NFFN + DEPTH],
        "lru_b_y": lru_rows[:, 0], "lru_b_in": lru_rows[:, 1], "lru_conv_b": lru_rows[:, 2],
        "lru_lambda": lru_rows[:, 3], "lru_b_out": lru_rows[:, 4],
        "lru_conv_w": my_cols(tot[K_CONVW:K_CONVW + 4 * n_lru].reshape(n_lru, 4, LRU_WIDTH), shard),
        "lru_b_a": my_cols(tot[K_BA:K_BA + n_lru].reshape(n_lru, HEADS, HEAD_DIM), hshard),
        "lru_b_x": my_cols(tot[K_BX:K_BX + n_lru].reshape(n_lru, HEADS, HEAD_DIM), hshard),
        "pool_scale": my_cols(tot[K_PS:K_PS + n_lru], shard),
        "final_norm_g": tot[K_FIN],
    }
    given = dict(b_mod=(b_mod, m_b_mod, v_b_mod), norm_mix_g=(norm_mix_g, m_norm_mix_g, v_norm_mix_g),
                 norm_ffn_g=(norm_ffn_g, m_norm_ffn_g, v_norm_ffn_g), lru_b_y=(lru_b_y, m_lru_b_y, v_lru_b_y),
                 lru_b_in=(lru_b_in, m_lru_b_in, v_lru_b_in), lru_conv_w=(lru_conv_w, m_lru_conv_w, v_lru_conv_w),
                 lru_conv_b=(lru_conv_b, m_lru_conv_b, v_lru_conv_b), lru_b_a=(lru_b_a, m_lru_b_a, v_lru_b_a),
                 lru_b_x=(lru_b_x, m_lru_b_x, v_lru_b_x), lru_lambda=(lru_lambda, m_lru_lambda, v_lru_lambda),
                 lru_b_out=(lru_b_out, m_lru_b_out, v_lru_b_out), pool_scale=(pool_scale, m_pool_scale, v_pool_scale),
                 final_norm_g=(final_norm_g, m_final_norm_g, v_final_norm_g))
    for name, g in small_grads.items():
        results[name] = (g,) + _adamw_small(name, g, *given[name])

    order = ["w_mod", "b_mod", "norm_mix_g", "norm_ffn_g", "lru_w_y", "lru_b_y", "lru_w_in", "lru_b_in", "lru_conv_w",
             "lru_conv_b", "lru_w_a", "lru_b_a", "lru_w_x", "lru_b_x", "lru_lambda", "lru_w_out", "lru_b_out", "pool_w",
             "pool_scale", "ffn_w1", "ffn_w2", "final_norm_g"]
    return (loss, grad_x, *[results[n][0] for n in order], *[results[n][1] for n in order],
            *[results[n][2] for n in order], *[results[n][3] for n in order])
```

```python
import functools
import math

import jax
import jax.numpy as jnp
from jax import lax
from jax.experimental import pallas as pl
from jax.experimental.pallas import tpu as pltpu

f32, bf16 = jnp.float32, jnp.bfloat16

D_MODEL = 1024
LRU_WIDTH = 1024
HEADS = 4
HEAD_DIM = 256
D_FF = 4096
DEPTH = 4
N_MOD = 6
N_DEV = 8
FF_CHUNK = D_FF // N_DEV
POOL_WINDOWS = (2, 4, 8, 16)
POOL_HALO = 16
EPS = 1e-6
LRU_C = 8.0

ADAM_LR = 0.001
ADAM_B1 = 0.9
ADAM_B2 = 0.999
ADAM_EPS = 1e-08
ADAM_WD = 0.01
ADAM_STEP = 10

V7X_VMEM_BYTES = 64 * 1024 * 1024
SUBLANES = 8
BF16_ROWS = 16

R_SH_M, R_SC_M, R_GT_M, R_SH_F, R_SC_F, R_GT_F, R_GS_M, R_GS_F = range(8)
P_BY, P_BIN, P_CONVB, P_BA, P_BX, P_LAM, P_BOUT, P_CW0 = 0, 1, 2, 3, 4, 5, 6, 8
G_SH, G_GS, G_GT, G_BY, G_BIN, G_CONVB, G_BA, G_BX, G_LS, G_BOUT, G_CW0 = 0, 1, 2, 3, 4, 5, 6, 7, 8, 9, 10
K_MOD, K_NMIX, K_NFFN, K_LRUB, K_CONVW, K_BA, K_BX, K_PS, K_FIN, K_ROWS = 0, 24, 28, 32, 42, 50, 52, 54, 56, 64


def _params(semantics=None, vmem_mb=48):
    return pltpu.CompilerParams(dimension_semantics=semantics, vmem_limit_bytes=vmem_mb * 1024 * 1024)


def _mm(a, b):
    return jnp.dot(a, b, preferred_element_type=f32)


def _mm_nt(a, b):
    return lax.dot_general(a, b, (((1,), (1,)), ((), ())), preferred_element_type=f32)


def _mm_tn(a, b):
    return lax.dot_general(a, b, (((0,), (0,)), ((), ())), preferred_element_type=f32)


def _rms(x):
    r = lax.rsqrt(jnp.mean(x * x, axis=-1, keepdims=True) + EPS)
    return x * r, r


def _norm_bwd(dh, n, r, gs):
    dn = dh * gs
    return r * (dn - n * jnp.mean(dn * n, axis=-1, keepdims=True))


def _colsum(v):
    return jnp.sum(v, axis=0, keepdims=True)


def _sigmoid(v):
    return 0.5 * jnp.tanh(0.5 * v) + 0.5


def _log_sigmoid(v):
    return jnp.minimum(v, 0.0) - jnp.log1p(jnp.exp(-jnp.abs(v)))


_GELU_C = 0.7978845608028654
_GELU_A = 0.044715


def _gelu_and_grad(v):
    v2 = v * v
    t = jnp.tanh(_GELU_C * v * (1.0 + _GELU_A * v2))
    p = 0.5 + 0.5 * t
    return v * p, p + (0.5 * v) * (1.0 - t * t) * (_GELU_C + (3.0 * _GELU_A * _GELU_C) * v2)


def _rows_before(halo, v, shifts):
    hr = halo.shape[0]
    ext = jnp.concatenate([halo, v], axis=0)
    return [pltpu.roll(ext, k, 0)[hr:] for k in shifts]


def _rows_after(v, halo, shifts):
    n = v.shape[0]
    ext = jnp.concatenate([v, halo], axis=0)
    return [pltpu.roll(ext, ext.shape[0] - k, 0)[:n] for k in shifts]


def _shift_matrix(n, halo_rows, shifts):
    rows = lax.broadcasted_iota(jnp.int32, (n, n + halo_rows), 0)
    cols = lax.broadcasted_iota(jnp.int32, (n, n + halo_rows), 1)
    return jnp.concatenate([(cols == rows + halo_rows - k).astype(bf16) for k in shifts], axis=0)


def _shifted_rows(sel, halo, v):
    n = v.shape[0]
    out = _mm(sel, jnp.concatenate([halo, v], axis=0))
    return [out[j * n:(j + 1) * n] for j in range(sel.shape[0] // n)]


def _block_diag(v, w_ref, kind):
    return jnp.concatenate(
        [_mm(v[:, h * HEAD_DIM:(h + 1) * HEAD_DIM], w_ref[kind, h]) for h in range(HEADS)], axis=1)


def _block_diag_t(v, w_ref, kind):
    return jnp.concatenate(
        [_mm_nt(v[:, h * HEAD_DIM:(h + 1) * HEAD_DIM], w_ref[kind, h]) for h in range(HEADS)], axis=1)


def _exchange(arrays, gather, name):
    n = len(arrays)
    peers = N_DEV - 1

    def body(*refs):
        ins, outs = refs[:n], refs[n:2 * n]
        send_sems, recv_sems, local_sems = refs[2 * n:]
        x, y, c = lax.axis_index("x"), lax.axis_index("y"), lax.axis_index("c")
        me = 4 * x + 2 * y + c
        local = []
        for k in range(n):
            cp = pltpu.make_async_copy(ins[k] if gather else ins[k].at[me], outs[k].at[me], local_sems.at[k])
            cp.start()
            local.append(cp)
        remote = []
        for p in range(1, N_DEV):
            px = 1 - x if p & 4 else x
            py = 1 - y if p & 2 else y
            pc = 1 - c if p & 1 else c
            for k in range(n):
                cp = pltpu.make_async_remote_copy(
                    src_ref=ins[k] if gather else ins[k].at[4 * px + 2 * py + pc],
                    dst_ref=outs[k].at[me],
                    send_sem=send_sems.at[k * peers + p - 1],
                    recv_sem=recv_sems.at[k * peers + p - 1],
                    device_id=(px, py, pc), device_id_type=pl.DeviceIdType.MESH)
                cp.start()
                remote.append(cp)
        for cp in remote:
            cp.wait()
        for cp in local:
            cp.wait()

    out_shape = tuple(
        jax.ShapeDtypeStruct(((N_DEV,) + a.shape) if gather else a.shape, a.dtype) for a in arrays)
    outs = pl.pallas_call(
        body, name=name, out_shape=out_shape,
        in_specs=[pl.BlockSpec(memory_space=pl.ANY)] * n,
        out_specs=tuple(pl.BlockSpec(memory_space=pl.ANY) for _ in range(n)),
        scratch_shapes=[pltpu.SemaphoreType.DMA((n * peers,)), pltpu.SemaphoreType.DMA((n * peers,)),
                        pltpu.SemaphoreType.DMA((n,))],
        compiler_params=pltpu.CompilerParams(has_side_effects=True),
    )(*arrays)
    return list(outs)


_HBM = pl.BlockSpec(memory_space=pltpu.HBM)
_SEM = pl.BlockSpec(memory_space=pltpu.SEMAPHORE)
_DATAFLOW = pltpu.SideEffectType.DATAFLOW_SIDE_EFFECTING


def _peer_copies(src_refs, land_refs, send_sems, recv_sems, gather):
    x, y, c = lax.axis_index("x"), lax.axis_index("y"), lax.axis_index("c")
    me = 4 * x + 2 * y + c
    peers = N_DEV - 1
    copies = []
    for p in range(1, N_DEV):
        px = 1 - x if p & 4 else x
        py = 1 - y if p & 2 else y
        pc = 1 - c if p & 1 else c
        for k in range(len(src_refs)):
            copies.append(pltpu.make_async_remote_copy(
                src_ref=src_refs[k] if gather else src_refs[k].at[4 * px + 2 * py + pc],
                dst_ref=land_refs[k].at[me],
                send_sem=send_sems.at[k * peers + p - 1], recv_sem=recv_sems.at[k * peers + p - 1],
                device_id=(px, py, pc), device_id_type=pl.DeviceIdType.MESH))
    return copies


def _landing(srcs, gather, me):
    out = []
    for a in srcs:
        own = a if gather else lax.dynamic_index_in_dim(a, me, 0, keepdims=False)
        out.append(lax.dynamic_update_index_in_dim(lax.empty((N_DEV,) + own.shape, own.dtype), own, me, 0))
    return out


def _send_start(name, srcs, gather, me):
    n = len(srcs)
    lands = _landing(srcs, gather, me)

    def body(*refs):
        src_refs, land_refs = refs[:n], refs[n:2 * n]
        send_sems, recv_sems, token = refs[2 * n], refs[2 * n + 1], refs[-1]
        for cp in _peer_copies(src_refs, land_refs, send_sems, recv_sems, gather):
            cp.start()
        token[...] = jnp.zeros_like(token)

    sems = pltpu.SemaphoreType.DMA((n * (N_DEV - 1),))
    outs = pl.pallas_call(
        body, name=name,
        out_shape=(sems, sems, *[pltpu.HBM(a.shape, a.dtype) for a in (*srcs, *lands)], jax.ShapeDtypeStruct((8, 128), f32)),
        in_specs=[_HBM] * (2 * n),
        out_specs=(_SEM, _SEM, *[_HBM] * (2 * n), pl.BlockSpec(memory_space=pltpu.VMEM)),
        input_output_aliases={k: 2 + k for k in range(2 * n)},
        compiler_params=pltpu.CompilerParams(has_side_effects=_DATAFLOW),
    )(*[pltpu.with_memory_space_constraint(a, pltpu.HBM) for a in (*srcs, *lands)])
    return (outs[0], outs[1], list(outs[2:2 + n]), list(outs[2 + n:2 + 2 * n]), gather), outs[-1]


def _send_wait(name, handle, after):
    send_sems, recv_sems, srcs, lands, gather = handle
    n = len(srcs)

    def body(*refs):
        src_refs, land_refs = refs[:n], refs[n:2 * n]
        for cp in _peer_copies(src_refs, land_refs, refs[2 * n], refs[2 * n + 1], gather):
            cp.wait_send()
            cp.wait_recv()
        refs[-1][...] = jnp.zeros_like(refs[-1])

    outs = pl.pallas_call(
        body, name=name,
        out_shape=(*[pltpu.HBM(a.shape, a.dtype) for a in (*srcs, *lands)], jax.ShapeDtypeStruct((8, 128), f32)),
        in_specs=[_HBM] * (2 * n) + [_SEM, _SEM, pl.BlockSpec(memory_space=pl.ANY)],
        out_specs=(*[_HBM] * (2 * n), pl.BlockSpec(memory_space=pltpu.VMEM)),
        input_output_aliases={k: k for k in range(2 * n)},
        compiler_params=pltpu.CompilerParams(has_side_effects=_DATAFLOW),
    )(*srcs, *lands, send_sems, recv_sems, after)
    return list(outs[n:2 * n]), outs[-1]


def _mod_part(c_all, w_mod):
    depth, d, cols = w_mod.shape

    def body(c_ref, w_ref, o_ref):
        cv = c_ref[...]
        cond = cv * _sigmoid(cv)
        o_ref[...] = jnp.dot(cond, w_ref[...], preferred_element_type=f32, precision=lax.Precision.HIGHEST)

    return pl.pallas_call(
        body, name="mod_part", grid=(depth,),
        out_shape=jax.ShapeDtypeStruct((depth, N_DEV, cols), f32),
        in_specs=[pl.BlockSpec((N_DEV, d), lambda i: (0, 0)), pl.BlockSpec((None, d, cols), lambda i: (i, 0, 0))],
        out_specs=pl.BlockSpec((None, N_DEV, cols), lambda i: (i, 0, 0)),
        compiler_params=_params(("arbitrary",), 32),
    )(c_all, w_mod)


def _mod_table(mod_row, b_mod, g_mix, g_ffn):
    def body(m_ref, b_ref, gm_ref, gf_ref, o_ref, token_ref):
        for i in range(DEPTH):
            for k in range(N_MOD):
                o_ref[i, k:k + 1, :] = m_ref[i:i + 1, k * D_MODEL:(k + 1) * D_MODEL] + b_ref[i:i + 1, k * D_MODEL:(k + 1) * D_MODEL]
            o_ref[i, R_GS_M:R_GS_M + 1, :] = gm_ref[i:i + 1, :] * (1.0 + o_ref[i, R_SC_M:R_SC_M + 1, :])
            o_ref[i, R_GS_F:R_GS_F + 1, :] = gf_ref[i:i + 1, :] * (1.0 + o_ref[i, R_SC_F:R_SC_F + 1, :])
        token_ref[...] = jnp.zeros_like(token_ref)

    return pl.pallas_call(
        body, name="mod_table",
        out_shape=(jax.ShapeDtypeStruct((DEPTH, 8, D_MODEL), f32), jax.ShapeDtypeStruct((8, 128), f32)))(
        mod_row, b_mod, g_mix, g_ffn)


def _ffn_tile(s):
    return min(512, s)


def _layer_weights(shape):
    return pl.BlockSpec((N_DEV,) + shape, lambda i: (0, 0, 0))


def _ffn_fwd(x, vec, w1g, w2g, layer):
    s = x.shape[0]
    ts = _ffn_tile(s)

    def body(x_ref, vec_ref, w1_ref, w2_ref, xo_ref, u_ref, y_ref, hb_ref):
        xv = x_ref[...]
        n, _ = _rms(xv)
        hb = (n * vec_ref[R_GS_F:R_GS_F + 1, :] + vec_ref[R_SH_F:R_SH_F + 1, :]).astype(bf16)
        hb_ref[...] = hb
        yv = jnp.zeros((ts, D_MODEL), f32)
        for f in range(N_DEV):
            u = jnp.maximum(_mm(hb, w1_ref[f]), 0.0)
            u_ref[:, f * FF_CHUNK:(f + 1) * FF_CHUNK] = u.astype(bf16)
            yv = yv + _mm((u * u).astype(bf16), w2_ref[f])
        y_ref[...] = yv.astype(bf16)
        xo_ref[...] = xv + vec_ref[R_GT_F:R_GT_F + 1, :] * yv

    row = pl.BlockSpec((ts, D_MODEL), lambda i: (i, 0))
    return pl.pallas_call(
        body, name=f"ffn_fwd_{layer}", grid=(s // ts,),
        out_shape=(jax.ShapeDtypeStruct((s, D_MODEL), f32), jax.ShapeDtypeStruct((s, D_FF), bf16),
                   jax.ShapeDtypeStruct((s, D_MODEL), bf16), jax.ShapeDtypeStruct((s, D_MODEL), bf16)),
        in_specs=[row, pl.BlockSpec((8, D_MODEL), lambda i: (0, 0)),
                  _layer_weights((D_MODEL, FF_CHUNK)), _layer_weights((FF_CHUNK, D_MODEL))],
        out_specs=(row, pl.BlockSpec((ts, D_FF), lambda i: (i, 0)), row, row),
        compiler_params=_params(("arbitrary",), 56),
    )(x, vec, w1g, w2g)


def _ffn_bwd_act(x, dx, u, y, vec, w1g, w2g, layer):
    s = x.shape[0]
    ts = _ffn_tile(s)

    def body(x_ref, dx_ref, u_ref, y_ref, vec_ref, w1_ref, w2_ref, dxo_ref, da_ref, dyb_ref, sm_ref):
        @pl.when(pl.program_id(0) == 0)
        def _():
            sm_ref[...] = jnp.zeros_like(sm_ref)

        dxv = dx_ref[...]
        dyb = (dxv * vec_ref[R_GT_F:R_GT_F + 1, :]).astype(bf16)
        dyb_ref[...] = dyb
        sm_ref[G_GT:G_GT + 1, :] += _colsum(dxv * y_ref[...].astype(f32))
        dh = jnp.zeros((ts, D_MODEL), f32)
        for f in range(N_DEV):
            cols = slice(f * FF_CHUNK, (f + 1) * FF_CHUNK)
            dz = _mm_nt(dyb, w2_ref[f])
            dab = (dz * (2.0 * u_ref[:, cols].astype(f32))).astype(bf16)
            da_ref[:, cols] = dab
            dh = dh + _mm_nt(dab, w1_ref[f])
        n, r = _rms(x_ref[...])
        sm_ref[G_SH:G_SH + 1, :] += _colsum(dh)
        sm_ref[G_GS:G_GS + 1, :] += _colsum(dh * n)
        dxo_ref[...] = dxv + _norm_bwd(dh, n, r, vec_ref[R_GS_F:R_GS_F + 1, :])

    row = pl.BlockSpec((ts, D_MODEL), lambda i: (i, 0))
    wide = pl.BlockSpec((ts, D_FF), lambda i: (i, 0))
    return pl.pallas_call(
        body, name=f"ffn_bwd_act_{layer}", grid=(s // ts,),
        out_shape=(jax.ShapeDtypeStruct((s, D_MODEL), f32), jax.ShapeDtypeStruct((s, D_FF), bf16),
                   jax.ShapeDtypeStruct((s, D_MODEL), bf16), jax.ShapeDtypeStruct((8, D_MODEL), f32)),
        in_specs=[row, row, wide, row, pl.BlockSpec((8, D_MODEL), lambda i: (0, 0)),
                  _layer_weights((D_MODEL, FF_CHUNK)), _layer_weights((FF_CHUNK, D_MODEL))],
        out_specs=(row, wide, row, pl.BlockSpec((8, D_MODEL), lambda i: (0, 0))),
        compiler_params=_params(("arbitrary",), 58),
    )(x, dx, u, y, vec, w1g, w2g)


def _ffn_bwd_w1(hb, da, layer):
    s = hb.shape[0]
    ts = _ffn_tile(s)
    nt = s // ts

    def body(hb_ref, da_ref, dw_ref, acc_ref):
        i = pl.program_id(0)

        @pl.when(i == 0)
        def _():
            acc_ref[...] = jnp.zeros_like(acc_ref)

        hb = hb_ref[...]
        for f in range(N_DEV):
            acc_ref[f] += _mm_tn(hb, da_ref[:, f * FF_CHUNK:(f + 1) * FF_CHUNK])

        @pl.when(i == nt - 1)
        def _():
            dw_ref[...] = acc_ref[...].astype(bf16)

    return pl.pallas_call(
        body, name=f"ffn_bwd_w1_{layer}", grid=(nt,),
        out_shape=jax.ShapeDtypeStruct((N_DEV, D_MODEL, FF_CHUNK), bf16),
        in_specs=[pl.BlockSpec((ts, D_MODEL), lambda i: (i, 0)), pl.BlockSpec((ts, D_FF), lambda i: (i, 0))],
        out_specs=pl.BlockSpec((N_DEV, D_MODEL, FF_CHUNK), lambda i: (0, 0, 0)),
        scratch_shapes=[pltpu.VMEM((N_DEV, D_MODEL, FF_CHUNK), f32)],
        compiler_params=_params(("arbitrary",), 56),
    )(hb, da)


def _ffn_bwd_w2(u, dyb, layer):
    s = u.shape[0]
    ts = _ffn_tile(s)
    nt = s // ts

    def body(u_ref, dyb_ref, dw_ref, acc_ref):
        i = pl.program_id(0)

        @pl.when(i == 0)
        def _():
            acc_ref[...] = jnp.zeros_like(acc_ref)

        dyb = dyb_ref[...]
        for f in range(N_DEV):
            uv = u_ref[:, f * FF_CHUNK:(f + 1) * FF_CHUNK].astype(f32)
            acc_ref[f] += _mm_tn((uv * uv).astype(bf16), dyb)

        @pl.when(i == nt - 1)
        def _():
            dw_ref[...] = acc_ref[...].astype(bf16)

    return pl.pallas_call(
        body, name=f"ffn_bwd_w2_{layer}", grid=(nt,),
        out_shape=jax.ShapeDtypeStruct((N_DEV, FF_CHUNK, D_MODEL), bf16),
        in_specs=[pl.BlockSpec((ts, D_FF), lambda i: (i, 0)), pl.BlockSpec((ts, D_MODEL), lambda i: (i, 0))],
        out_specs=pl.BlockSpec((N_DEV, FF_CHUNK, D_MODEL), lambda i: (0, 0, 0)),
        scratch_shapes=[pltpu.VMEM((N_DEV, FF_CHUNK, D_MODEL), f32)],
        compiler_params=_params(("arbitrary",), 56),
    )(u, dyb)


def _lru_gates(xc, wsm_ref, pv_ref):
    xcb = xc.astype(bf16)
    gr = _sigmoid(_block_diag(xcb, wsm_ref, 0) + pv_ref[P_BA:P_BA + 1, :])
    gi = _sigmoid(_block_diag(xcb, wsm_ref, 1) + pv_ref[P_BX:P_BX + 1, :])
    log_a = (LRU_C * _log_sigmoid(pv_ref[P_LAM:P_LAM + 1, :])) * gr
    t = jnp.tanh(log_a)
    return gr, gi, jnp.exp(log_a), jnp.sqrt((-2.0 * t) / (1.0 - t))


def _conv(xr, taps_before, pv_ref):
    xc = xr * pv_ref[P_CW0 + 3:P_CW0 + 4, :] + pv_ref[P_CONVB:P_CONVB + 1, :]
    for k, v in zip((2, 1, 0), taps_before):
        xc = xc + v * pv_ref[P_CW0 + k:P_CW0 + k + 1, :]
    return xc


LRU_SUB = 128
LRU_SUBS = 2


def _scan_rows(a, u, carry, reverse):
    groups = a.shape[0] // SUBLANES
    row = lax.broadcasted_iota(jnp.int32, (SUBLANES, a.shape[1]), 0)
    outs = [None] * groups
    for j in range(groups):
        g = groups - 1 - j if reverse else j
        av, uv = a[g * SUBLANES:(g + 1) * SUBLANES], u[g * SUBLANES:(g + 1) * SUBLANES]
        for k in (1, 2, 4):
            if reverse:
                valid, shift = row < SUBLANES - k, SUBLANES - k
            else:
                valid, shift = row >= k, k
            a_s = jnp.where(valid, pltpu.roll(av, shift, 0), 1.0)
            u_s = jnp.where(valid, pltpu.roll(uv, shift, 0), 0.0)
            uv = uv + av * u_s
            av = av * a_s
        h = uv + av * carry
        outs[g] = h
        carry = h[0:1, :] if reverse else h[SUBLANES - 1:SUBLANES, :]
    return jnp.concatenate(outs, axis=0), carry


def _lru_fwd(x, vec, wbig, wsm, pvec, layer):
    s = x.shape[0]
    sub = min(LRU_SUB, s)
    ts = min(sub * LRU_SUBS, s)
    nsub = ts // sub
    w = LRU_WIDTH

    def body(x_ref, vec_ref, wb_ref, wsm_ref, pv_ref, xo_ref, xr_ref, hs_ref, a_ref, mult_ref, gr_ref, gi_ref,
             gel_ref, geld_ref, y_ref, tail_ref, carry_ref):
        @pl.when(pl.program_id(0) == 0)
        def _():
            tail_ref[...] = jnp.zeros_like(tail_ref)
            carry_ref[...] = jnp.zeros_like(carry_ref)

        sel = _shift_matrix(sub, BF16_ROWS, (1, 2, 3))
        for k in range(nsub):
            rows = slice(k * sub, (k + 1) * sub)
            xv = x_ref[rows, :]
            n, _ = _rms(xv)
            hb = (n * vec_ref[R_GS_M:R_GS_M + 1, :] + vec_ref[R_SH_M:R_SH_M + 1, :]).astype(bf16)
            gelu_v, gelu_d = _gelu_and_grad(_mm(hb, wb_ref[0]) + pv_ref[P_BY:P_BY + 1, :])
            gel_ref[rows, :] = gelu_v.astype(bf16)
            geld_ref[rows, :] = gelu_d.astype(bf16)
            xrb = (_mm(hb, wb_ref[1]) + pv_ref[P_BIN:P_BIN + 1, :]).astype(bf16)
            xr_ref[rows, :] = xrb
            xc = _conv(xrb.astype(f32), _shifted_rows(sel, tail_ref[...], xrb), pv_ref)
            tail_ref[...] = xrb[sub - BF16_ROWS:, :]
            gr, gi, a, mult = _lru_gates(xc, wsm_ref, pv_ref)
            gr_ref[rows, :] = gr.astype(bf16)
            gi_ref[rows, :] = gi.astype(bf16)
            a_ref[rows, :] = a
            mult_ref[rows, :] = mult
            hs, carry = _scan_rows(a, mult * (gi * xc), carry_ref[0:1, :], reverse=False)
            carry_ref[0:1, :] = carry
            hs_ref[rows, :] = hs
            yv = _mm((hs * gelu_v).astype(bf16), wb_ref[2]) + pv_ref[P_BOUT:P_BOUT + 1, :]
            y_ref[rows, :] = yv.astype(bf16)
            xo_ref[rows, :] = xv + vec_ref[R_GT_M:R_GT_M + 1, :] * yv

    row = pl.BlockSpec((ts, D_MODEL), lambda i: (i, 0))
    roww = pl.BlockSpec((ts, w), lambda i: (i, 0))
    wide = lambda dt: jax.ShapeDtypeStruct((s, w), dt)
    return pl.pallas_call(
        body, name=f"lru_fwd_{layer}", grid=(s // ts,),
        out_shape=(jax.ShapeDtypeStruct((s, D_MODEL), f32), wide(bf16), wide(f32), wide(f32), wide(f32),
                   wide(bf16), wide(bf16), wide(bf16), wide(bf16), jax.ShapeDtypeStruct((s, D_MODEL), bf16)),
        in_specs=[row, pl.BlockSpec((8, D_MODEL), lambda i: (0, 0)),
                  pl.BlockSpec((3, w, w), lambda i: (0, 0, 0)),
                  pl.BlockSpec((2, HEADS, HEAD_DIM, HEAD_DIM), lambda i: (0, 0, 0, 0)),
                  pl.BlockSpec((16, w), lambda i: (0, 0))],
        out_specs=(row, roww, roww, roww, roww, roww, roww, roww, roww, row),
        scratch_shapes=[pltpu.VMEM((BF16_ROWS, w), bf16), pltpu.VMEM((SUBLANES, w), f32)],
        compiler_params=_params(("arbitrary",)),
    )(x, vec, wbig, wsm, pvec)


def _lru_bwd(x, dx, saved, vec, wbig, wsm, pvec, layer):
    xr, hs, a_all, mult_all, gr_all, gi_all, gel_all, geld_all, y = saved
    s = x.shape[0]
    sub = min(LRU_SUB, s)
    ts = min(sub * LRU_SUBS, s)
    nsub = ts // sub
    nt = s // ts
    w = LRU_WIDTH
    shard = w // N_DEV
    hshard = HEAD_DIM // N_DEV

    def body(x_ref, dx_ref, xr_ref, xrh_ref, hs_ref, hsh_ref, a_ref, mult_ref, gr_ref, gi_ref, gel_ref, geld_ref,
             y_ref, vec_ref, wb_ref, wsm_ref, pv_ref,
             dxo_ref, dwb_ref, dwsm_ref, sm_ref, accb_ref, accs_ref, eps_ref, dxc8_ref,
             hb_scr, dgb_scr, dxrb_scr, mb_scr, dyb_scr, xcb_scr, drab_scr, drxb_scr):
        i = pl.program_id(0)
        first_tile = i == nt - 1

        @pl.when(i == 0)
        def _():
            accb_ref[...] = jnp.zeros_like(accb_ref)
            accs_ref[...] = jnp.zeros_like(accs_ref)
            sm_ref[...] = jnp.zeros_like(sm_ref)
            eps_ref[...] = jnp.zeros_like(eps_ref)
            dxc8_ref[...] = jnp.zeros_like(dxc8_ref)

        gs = vec_ref[R_GS_M:R_GS_M + 1, :]
        c_ls = LRU_C * _log_sigmoid(pv_ref[P_LAM:P_LAM + 1, :])
        for k in reversed(range(nsub)):
            rows = slice(k * sub, (k + 1) * sub)
            xv = x_ref[rows, :]
            dxv = dx_ref[rows, :]
            n, r = _rms(xv)
            hb_scr[rows, :] = (n * gs + vec_ref[R_SH_M:R_SH_M + 1, :]).astype(bf16)
            xrv = xr_ref[rows, :].astype(f32)
            hsv = hs_ref[rows, :]
            if k == 0:
                xr_halo = jnp.where(first_tile, 0.0, xrh_ref[...].astype(f32))
                hs_halo = jnp.where(first_tile, 0.0, hsh_ref[...])
            else:
                xr_halo = xr_ref[k * sub - BF16_ROWS:k * sub, :].astype(f32)
                hs_halo = hs_ref[k * sub - SUBLANES:k * sub, :]
            xs1, xs2, xs3 = _rows_before(xr_halo, xrv, (1, 2, 3))
            xc = _conv(xrv, (xs1, xs2, xs3), pv_ref)
            xcb_scr[rows, :] = xc.astype(bf16)
            a, mult = a_ref[rows, :], mult_ref[rows, :]
            gr, gi = gr_ref[rows, :].astype(f32), gi_ref[rows, :].astype(f32)
            gelu_v = gel_ref[rows, :].astype(f32)

            dy = dxv * vec_ref[R_GT_M:R_GT_M + 1, :]
            dyb = dy.astype(bf16)
            dyb_scr[rows, :] = dyb
            sm_ref[G_GT:G_GT + 1, :] += _colsum(dxv * y_ref[rows, :].astype(f32))
            sm_ref[G_BOUT:G_BOUT + 1, :] += _colsum(dy)
            mb_scr[rows, :] = (hsv * gelu_v).astype(bf16)
            dm = _mm_nt(dyb, wb_ref[2])
            dhs = dm * gelu_v
            dgpre = dm * hsv * geld_ref[rows, :].astype(f32)
            dgb = dgpre.astype(bf16)
            dgb_scr[rows, :] = dgb
            sm_ref[G_BY:G_BY + 1, :] += _colsum(dgpre)

            eps_in = eps_ref[0:1, :]
            eps, eps_out = _scan_rows(a, a * dhs, eps_in, reverse=True)
            eps_ref[0:1, :] = eps_out
            (eps_next,) = _rows_after(eps, jnp.broadcast_to(eps_in, (SUBLANES, w)), (1,))
            delta = dhs + eps_next
            (h_prev,) = _rows_before(hs_halo, hsv, (1,))
            dxi = delta * xc
            dgi = dxi * mult
            dla = (delta * h_prev) * a - (dxi * gi) * (a * a) / mult
            sm_ref[G_LS:G_LS + 1, :] += _colsum(dla * gr)
            dra = (dla * c_ls) * (gr - gr * gr)
            drx = dgi * (gi - gi * gi)
            drab, drxb = dra.astype(bf16), drx.astype(bf16)
            drab_scr[rows, :] = drab
            drxb_scr[rows, :] = drxb
            sm_ref[G_BA:G_BA + 1, :] += _colsum(dra)
            sm_ref[G_BX:G_BX + 1, :] += _colsum(drx)
            dxc = (delta * mult) * gi + _block_diag_t(drab, wsm_ref, 0) + _block_diag_t(drxb, wsm_ref, 1)

            sm_ref[G_CONVB:G_CONVB + 1, :] += _colsum(dxc)
            for kk, v in zip((3, 2, 1, 0), (xrv, xs1, xs2, xs3)):
                sm_ref[G_CW0 + kk:G_CW0 + kk + 1, :] += _colsum(dxc * v)
            ups = _rows_after(dxc, dxc8_ref[...], (1, 2, 3))
            dxc8_ref[...] = dxc[0:SUBLANES, :]
            dxr = dxc * pv_ref[P_CW0 + 3:P_CW0 + 4, :]
            for kk, v in zip((2, 1, 0), ups):
                dxr = dxr + v * pv_ref[P_CW0 + kk:P_CW0 + kk + 1, :]
            dxrb = dxr.astype(bf16)
            dxrb_scr[rows, :] = dxrb
            sm_ref[G_BIN:G_BIN + 1, :] += _colsum(dxr)
            dh = _mm_nt(dgb, wb_ref[0]) + _mm_nt(dxrb, wb_ref[1])
            sm_ref[G_SH:G_SH + 1, :] += _colsum(dh)
            sm_ref[G_GS:G_GS + 1, :] += _colsum(dh * n)
            dxo_ref[rows, :] = dxv + _norm_bwd(dh, n, r, gs)

        hb = hb_scr[...]
        accb_ref[0] += _mm_tn(hb, dgb_scr[...])
        accb_ref[1] += _mm_tn(hb, dxrb_scr[...])
        accb_ref[2] += _mm_tn(mb_scr[...], dyb_scr[...])
        for h in range(HEADS):
            cols = slice(h * HEAD_DIM, (h + 1) * HEAD_DIM)
            accs_ref[0, h] += _mm_tn(xcb_scr[:, cols], drab_scr[:, cols])
            accs_ref[1, h] += _mm_tn(xcb_scr[:, cols], drxb_scr[:, cols])

        @pl.when(i == nt - 1)
        def _():
            sm_ref[G_LS:G_LS + 1, :] = sm_ref[G_LS:G_LS + 1, :] * LRU_C
            for k in range(3):
                dwb_ref[:, k] = accb_ref[k].astype(bf16).reshape(N_DEV, shard, w)
            for k in range(2):
                for h in range(HEADS):
                    dwsm_ref[:, k, h] = accs_ref[k, h].astype(bf16).reshape(N_DEV, hshard, HEAD_DIM)

    rev = lambda i: (nt - 1 - i, 0)
    row = pl.BlockSpec((ts, D_MODEL), rev)
    roww = pl.BlockSpec((ts, w), rev)
    halo16 = pl.BlockSpec((BF16_ROWS, w), lambda i: (jnp.maximum((nt - 1 - i) * (ts // BF16_ROWS) - 1, 0), 0))
    halo8 = pl.BlockSpec((SUBLANES, w), lambda i: (jnp.maximum((nt - 1 - i) * (ts // SUBLANES) - 1, 0), 0))
    const = lambda *shape: pl.BlockSpec(shape, lambda i: (0,) * len(shape))
    operand = pltpu.VMEM((ts, w), bf16)
    return pl.pallas_call(
        body, name=f"lru_bwd_{layer}", grid=(nt,),
        out_shape=(jax.ShapeDtypeStruct((s, D_MODEL), f32),
                   jax.ShapeDtypeStruct((N_DEV, 3, shard, w), bf16),
                   jax.ShapeDtypeStruct((N_DEV, 2, HEADS, hshard, HEAD_DIM), bf16),
                   jax.ShapeDtypeStruct((16, w), f32)),
        in_specs=[row, row, roww, halo16, roww, halo8, roww, roww, roww, roww, roww, roww, row, const(8, D_MODEL),
                  const(3, w, w), const(2, HEADS, HEAD_DIM, HEAD_DIM), const(16, w)],
        out_specs=(row, const(N_DEV, 3, shard, w), const(N_DEV, 2, HEADS, hshard, HEAD_DIM), const(16, w)),
        scratch_shapes=[pltpu.VMEM((3, w, w), f32), pltpu.VMEM((2, HEADS, HEAD_DIM, HEAD_DIM), f32),
                        pltpu.VMEM((SUBLANES, w), f32), pltpu.VMEM((SUBLANES, w), f32)] + [operand] * 8,
        compiler_params=_params(("arbitrary",), 58),
    )(x, dx, xr, xr, hs, hs, a_all, mult_all, gr_all, gi_all, gel_all, geld_all, y, vec, wbig, wsm, pvec)


def _pool_tile(s):
    return min(256, s)


def _pool_counts(tile_index, ts):
    t = (tile_index * ts + lax.broadcasted_iota(jnp.int32, (ts, 1), 0) + 1).astype(f32)
    return [1.0 / jnp.minimum(t, float(win)) for win in POOL_WINDOWS]


def _pooled(h, halo, inv):
    ext = jnp.concatenate([halo, h], axis=0)
    out = []
    for g in range(len(POOL_WINDOWS)):
        acc = ext[:, g * HEAD_DIM:(g + 1) * HEAD_DIM]
        for step in range(g + 1):
            acc = acc + pltpu.roll(acc, 1 << step, 0)
        out.append(acc[POOL_HALO:] * inv[g] - h[:, g * HEAD_DIM:(g + 1) * HEAD_DIM])
    return out


def _pool_fwd(x, vec, pw, ps, layer):
    s = x.shape[0]
    ts = _pool_tile(s)

    def body(x_ref, vec_ref, pw_ref, ps_ref, xo_ref, y_ref, halo_ref):
        i = pl.program_id(0)

        @pl.when(i == 0)
        def _():
            halo_ref[...] = jnp.zeros_like(halo_ref)

        xv = x_ref[...]
        n, _ = _rms(xv)
        h = n * vec_ref[R_GS_M:R_GS_M + 1, :] + vec_ref[R_SH_M:R_SH_M + 1, :]
        pooled = _pooled(h, halo_ref[...], _pool_counts(i, ts))
        halo_ref[...] = h[ts - POOL_HALO:, :]
        mixed = jnp.concatenate([_mm(pooled[g].astype(bf16), pw_ref[g]) for g in range(HEADS)], axis=1)
        yv = mixed * ps_ref[0:1, :]
        y_ref[...] = yv.astype(bf16)
        xo_ref[...] = xv + vec_ref[R_GT_M:R_GT_M + 1, :] * yv

    row = pl.BlockSpec((ts, D_MODEL), lambda i: (i, 0))
    return pl.pallas_call(
        body, name=f"pool_fwd_{layer}", grid=(s // ts,),
        out_shape=(jax.ShapeDtypeStruct((s, D_MODEL), f32), jax.ShapeDtypeStruct((s, D_MODEL), bf16)),
        in_specs=[row, pl.BlockSpec((8, D_MODEL), lambda i: (0, 0)),
                  pl.BlockSpec((HEADS, HEAD_DIM, HEAD_DIM), lambda i: (0, 0, 0)),
                  pl.BlockSpec((8, D_MODEL), lambda i: (0, 0))],
        out_specs=(row, row),
        scratch_shapes=[pltpu.VMEM((POOL_HALO, D_MODEL), f32)],
        compiler_params=_params(("arbitrary",)),
    )(x, vec, pw, ps)


def _pool_bwd(x, dx, y, vec, pw, ps, layer):
    s = x.shape[0]
    ts = _pool_tile(s)
    nt = s // ts
    hshard = HEAD_DIM // N_DEV

    def body(x_ref, xh_ref, dx_ref, y_ref, vec_ref, pw_ref, ps_ref, dxo_ref, dpw_ref, sm_ref, acc_ref, q16_ref):
        i = pl.program_id(0)
        tile = nt - 1 - i

        @pl.when(i == 0)
        def _():
            acc_ref[...] = jnp.zeros_like(acc_ref)
            sm_ref[...] = jnp.zeros_like(sm_ref)
            q16_ref[...] = jnp.zeros_like(q16_ref)

        gs, sh = vec_ref[R_GS_M:R_GS_M + 1, :], vec_ref[R_SH_M:R_SH_M + 1, :]
        xv = x_ref[...]
        dxv = dx_ref[...]
        n, r = _rms(xv)
        h = n * gs + sh
        nh, _ = _rms(xh_ref[...])
        halo = jnp.where(tile == 0, 0.0, nh * gs + sh)
        inv = _pool_counts(tile, ts)
        pooled = _pooled(h, halo, inv)
        mixed = jnp.concatenate([_mm(pooled[g].astype(bf16), pw_ref[g]) for g in range(HEADS)], axis=1)

        dy = dxv * vec_ref[R_GT_M:R_GT_M + 1, :]
        sm_ref[G_GT:G_GT + 1, :] += _colsum(dxv * y_ref[...].astype(f32))
        sm_ref[3:4, :] += _colsum(dy * mixed)
        dmixed = (dy * ps_ref[0:1, :]).astype(bf16)
        dh_parts = []
        for g in range(HEADS):
            cols = slice(g * HEAD_DIM, (g + 1) * HEAD_DIM)
            acc_ref[g] += _mm_tn(pooled[g].astype(bf16), dmixed[:, cols])
            dpooled = _mm_nt(dmixed[:, cols], pw_ref[g])
            q = dpooled * inv[g]
            ext = jnp.concatenate([q, q16_ref[:, cols]], axis=0)
            q16_ref[:, cols] = q[0:POOL_HALO, :]
            for step in range(g + 1):
                ext = ext + pltpu.roll(ext, ext.shape[0] - (1 << step), 0)
            dh_parts.append(ext[:ts] - dpooled)
        dh = jnp.concatenate(dh_parts, axis=1)
        sm_ref[G_SH:G_SH + 1, :] += _colsum(dh)
        sm_ref[G_GS:G_GS + 1, :] += _colsum(dh * n)
        dxo_ref[...] = dxv + _norm_bwd(dh, n, r, gs)

        @pl.when(i == nt - 1)
        def _():
            for g in range(HEADS):
                dpw_ref[:, g] = acc_ref[g].astype(bf16).reshape(N_DEV, hshard, HEAD_DIM)

    rev = lambda i: (nt - 1 - i, 0)
    row = pl.BlockSpec((ts, D_MODEL), rev)
    halo16 = pl.BlockSpec((POOL_HALO, D_MODEL), lambda i: (jnp.maximum((nt - 1 - i) * (ts // POOL_HALO) - 1, 0), 0))
    const = lambda *shape: pl.BlockSpec(shape, lambda i: (0,) * len(shape))
    return pl.pallas_call(
        body, name=f"pool_bwd_{layer}", grid=(nt,),
        out_shape=(jax.ShapeDtypeStruct((s, D_MODEL), f32),
                   jax.ShapeDtypeStruct((N_DEV, HEADS, hshard, HEAD_DIM), bf16),
                   jax.ShapeDtypeStruct((8, D_MODEL), f32)),
        in_specs=[row, halo16, row, row, const(8, D_MODEL), const(HEADS, HEAD_DIM, HEAD_DIM), const(8, D_MODEL)],
        out_specs=(row, const(N_DEV, HEADS, hshard, HEAD_DIM), const(8, D_MODEL)),
        scratch_shapes=[pltpu.VMEM((HEADS, HEAD_DIM, HEAD_DIM), f32), pltpu.VMEM((POOL_HALO, D_MODEL), f32)],
        compiler_params=_params(("arbitrary",)),
    )(x, x, dx, y, vec, pw, ps)


def _final(x, target, g_fin):
    s = x.shape[0]
    ts = min(512, s)

    def body(x_ref, t_ref, g_ref, dx_ref, loss_ref, sm_ref):
        @pl.when(pl.program_id(0) == 0)
        def _():
            loss_ref[...] = jnp.zeros_like(loss_ref)
            sm_ref[...] = jnp.zeros_like(sm_ref)

        g = g_ref[0:1, :]
        n, r = _rms(x_ref[...])
        err = n * g - t_ref[...]
        loss_ref[...] += 0.5 * jnp.sum(jnp.mean(err * err, axis=-1, keepdims=True), axis=0, keepdims=True)
        dyv = err * (1.0 / D_MODEL)
        sm_ref[0:1, :] += _colsum(dyv * n)
        dx_ref[...] = _norm_bwd(dyv, n, r, g)

    row = pl.BlockSpec((ts, D_MODEL), lambda i: (i, 0))
    return pl.pallas_call(
        body, name="final_loss", grid=(s // ts,),
        out_shape=(jax.ShapeDtypeStruct((s, D_MODEL), f32), jax.ShapeDtypeStruct((8, 128), f32),
                   jax.ShapeDtypeStruct((8, D_MODEL), f32)),
        in_specs=[row, row, pl.BlockSpec((8, D_MODEL), lambda i: (0, 0))],
        out_specs=(row, pl.BlockSpec((8, 128), lambda i: (0, 0)), pl.BlockSpec((8, D_MODEL), lambda i: (0, 0))),
        compiler_params=_params(("arbitrary",)),
    )(x, target, g_fin)


def _small_pack(sm_ffn, sm_mix, sm_fin, table, g_mix, g_ffn, lam):
    def body(*refs):
        ffn, mix = refs[0:DEPTH], refs[DEPTH:2 * DEPTH]
        fin_ref, tab_ref, gm_ref, gf_ref, lam_ref, o_ref = refs[2 * DEPTH:]
        o_ref[...] = jnp.zeros_like(o_ref)
        for i in range(DEPTH):
            base = K_MOD + i * N_MOD
            o_ref[base + 0:base + 1, :] = mix[i][G_SH:G_SH + 1, :]
            o_ref[base + 1:base + 2, :] = mix[i][G_GS:G_GS + 1, :] * gm_ref[i:i + 1, :]
            o_ref[base + 2:base + 3, :] = mix[i][G_GT:G_GT + 1, :]
            o_ref[base + 3:base + 4, :] = ffn[i][G_SH:G_SH + 1, :]
            o_ref[base + 4:base + 5, :] = ffn[i][G_GS:G_GS + 1, :] * gf_ref[i:i + 1, :]
            o_ref[base + 5:base + 6, :] = ffn[i][G_GT:G_GT + 1, :]
            o_ref[K_NMIX + i:K_NMIX + i + 1, :] = mix[i][G_GS:G_GS + 1, :] * (1.0 + tab_ref[i, R_SC_M:R_SC_M + 1, :])
            o_ref[K_NFFN + i:K_NFFN + i + 1, :] = ffn[i][G_GS:G_GS + 1, :] * (1.0 + tab_ref[i, R_SC_F:R_SC_F + 1, :])
            j = i // 2
            if i % 2 == 0:
                for k, src in enumerate((G_BY, G_BIN, G_CONVB, None, G_BOUT)):
                    dst = K_LRUB + j * 5 + k
                    if src is None:
                        o_ref[dst:dst + 1, :] = mix[i][G_LS:G_LS + 1, :] * _sigmoid(-lam_ref[j:j + 1, :])
                    else:
                        o_ref[dst:dst + 1, :] = mix[i][src:src + 1, :]
                o_ref[K_CONVW + j * 4:K_CONVW + j * 4 + 4, :] = mix[i][G_CW0:G_CW0 + 4, :]
                o_ref[K_BA + j:K_BA + j + 1, :] = mix[i][G_BA:G_BA + 1, :]
                o_ref[K_BX + j:K_BX + j + 1, :] = mix[i][G_BX:G_BX + 1, :]
            else:
                o_ref[K_PS + j:K_PS + j + 1, :] = mix[i][3:4, :]
        o_ref[K_FIN:K_FIN + 1, :] = fin_ref[0:1, :]

    return pl.pallas_call(body, name="small_pack", out_shape=jax.ShapeDtypeStruct((K_ROWS, D_MODEL), f32))(
        *sm_ffn, *sm_mix, sm_fin, table, g_mix, g_ffn, lam)


def _small_sum(gathered):
    def body(g_ref, o_ref, token_ref):
        tot = g_ref[0]
        for src in range(1, N_DEV):
            tot = tot + g_ref[src]
        o_ref[...] = tot
        token_ref[...] = jnp.zeros_like(token_ref)

    return pl.pallas_call(
        body, name="small_sum",
        out_shape=(jax.ShapeDtypeStruct(gathered.shape[1:], f32), jax.ShapeDtypeStruct((8, 128), f32)))(gathered)


def _adamw_math(g, w, m, v):
    m = ADAM_B1 * m + (1.0 - ADAM_B1) * g
    v = ADAM_B2 * v + (1.0 - ADAM_B2) * (g * g)
    m_hat = m / (1.0 - ADAM_B1 ** ADAM_STEP)
    v_hat = v / (1.0 - ADAM_B2 ** ADAM_STEP)
    delta = -ADAM_LR * (m_hat / (jnp.sqrt(v_hat) + ADAM_EPS) + ADAM_WD * w)
    return delta, m, v


def _adamw_small(name, g, w, m, v):
    shape = w.shape
    two_d = (1, shape[0]) if len(shape) == 1 else (math.prod(shape[:-1]), shape[-1])

    def body(g_ref, w_ref, m_ref, v_ref, d_ref, mo_ref, vo_ref):
        d_ref[...], mo_ref[...], vo_ref[...] = _adamw_math(g_ref[...], w_ref[...], m_ref[...], v_ref[...])

    outs = pl.pallas_call(body, name=f"adamw_{name}", out_shape=tuple(jax.ShapeDtypeStruct(two_d, f32) for _ in range(3)))(
        *(t.reshape(two_d) for t in (g, w, m, v)))
    return tuple(t.reshape(shape) for t in outs)


def _block_rows(rows, cols):
    tr = max(SUBLANES, min(rows, (512 * 1024) // (4 * cols)))
    while rows % tr:
        tr //= 2
    return tr


def _adamw_reduce(name, landings, kind, w, m, v):
    nl = len(landings)
    rows, cols = landings[0].shape[2:]
    tr = _block_rows(rows, cols)
    per_layer = rows // tr

    def body(*refs):
        l_refs = refs[:nl]
        w_ref, m_ref, v_ref, g_ref, d_ref, mo_ref, vo_ref = refs[nl:]
        layer = pl.program_id(0)
        for k in range(nl):
            @pl.when(layer == k)
            def _(k=k):
                g = l_refs[k][0].astype(f32)
                for src in range(1, N_DEV):
                    g = g + l_refs[k][src].astype(f32)
                g_ref[...] = g
        d_ref[...], mo_ref[...], vo_ref[...] = _adamw_math(g_ref[...], w_ref[...], m_ref[...], v_ref[...])

    blk = pl.BlockSpec((tr, cols), lambda l, r: (l * per_layer + r, 0))
    land = [pl.BlockSpec((N_DEV, None, tr, cols), lambda l, r, k=k: (0, kind, jnp.where(l == k, r, 0), 0)) for k in range(nl)]
    return pl.pallas_call(
        body, name=f"adamw_{name}", grid=(nl, per_layer),
        out_shape=tuple(jax.ShapeDtypeStruct((nl * rows, cols), f32) for _ in range(4)),
        in_specs=land + [blk, blk, blk],
        out_specs=(blk, blk, blk, blk),
        compiler_params=_params(("arbitrary", "arbitrary"), 32),
    )(*landings, w, m, v)


def _adamw_w_mod(c_all, dmod_all, w, m, v):
    depth, d, cols = w.shape
    tr = 256

    def body(c_ref, dm_ref, w_ref, m_ref, v_ref, g_ref, d_ref, mo_ref, vo_ref):
        cv = c_ref[...]
        cond = cv * _sigmoid(cv)
        g = lax.dot_general(cond, dm_ref[...], (((0,), (0,)), ((), ())), preferred_element_type=f32,
                            precision=lax.Precision.HIGHEST)
        g_ref[...] = g
        d_ref[...], mo_ref[...], vo_ref[...] = _adamw_math(g, w_ref[...], m_ref[...], v_ref[...])

    blk = pl.BlockSpec((None, tr, cols), lambda i, r: (i, r, 0))
    return pl.pallas_call(
        body, name="adamw_w_mod", grid=(depth, d // tr),
        out_shape=tuple(jax.ShapeDtypeStruct(w.shape, f32) for _ in range(4)),
        in_specs=[pl.BlockSpec((N_DEV, tr), lambda i, r: (0, r)),
                  pl.BlockSpec((None, N_DEV, cols), lambda i, r: (i, 0, 0)), blk, blk, blk],
        out_specs=(blk, blk, blk, blk),
        compiler_params=_params(("arbitrary", "arbitrary"), 32),
    )(c_all, dmod_all, w, m, v)


def kernel(x, c, w_mod, b_mod, norm_mix_g, norm_ffn_g, lru_w_y, lru_b_y, lru_w_in, lru_b_in, lru_conv_w, lru_conv_b, lru_w_a, lru_b_a, lru_w_x, lru_b_x, lru_lambda, lru_w_out, lru_b_out, pool_w, pool_scale, ffn_w1, ffn_w2, final_norm_g, loss_target, m_w_mod, m_b_mod, m_norm_mix_g, m_norm_ffn_g, m_lru_w_y, m_lru_b_y, m_lru_w_in, m_lru_b_in, m_lru_conv_w, m_lru_conv_b, m_lru_w_a, m_lru_b_a, m_lru_w_x, m_lru_b_x, m_lru_lambda, m_lru_w_out, m_lru_b_out, m_pool_w, m_pool_scale, m_ffn_w1, m_ffn_w2, m_final_norm_g, v_w_mod, v_b_mod, v_norm_mix_g, v_norm_ffn_g, v_lru_w_y, v_lru_b_y, v_lru_w_in, v_lru_b_in, v_lru_conv_w, v_lru_conv_b, v_lru_w_a, v_lru_b_a, v_lru_w_x, v_lru_b_x, v_lru_lambda, v_lru_w_out, v_lru_b_out, v_pool_w, v_pool_scale, v_ffn_w1, v_ffn_w2, v_final_norm_g):
    me = 4 * lax.axis_index("x") + 2 * lax.axis_index("y") + lax.axis_index("c")
    n_lru = lru_w_y.shape[0]
    shard = LRU_WIDTH // N_DEV
    hshard = HEAD_DIM // N_DEV
    xs = x[0]
    target = loss_target[0]

    small_vecs = jnp.concatenate([
        lru_conv_w.reshape(n_lru * 4, shard), lru_b_a.reshape(n_lru, HEADS * hshard),
        lru_b_x.reshape(n_lru, HEADS * hshard), pool_scale, jnp.zeros((2, shard), f32)], axis=0)
    first_mix, token = _send_start("gather_mix_start_0", [jnp.stack([lru_w_y[0], lru_w_in[0], lru_w_out[0]]).astype(bf16),
                                                          jnp.stack([lru_w_a[0], lru_w_x[0]]).astype(bf16)], True, me)
    sv_g, c_g = _exchange([small_vecs + token[0, 0], c], True, "gather_cond")
    conv_w_full = sv_g[:, 0:8].reshape(N_DEV, n_lru, 4, shard).transpose(1, 2, 0, 3).reshape(n_lru, 4, LRU_WIDTH)
    b_a_full = sv_g[:, 8:10].reshape(N_DEV, n_lru, HEADS, hshard).transpose(1, 2, 0, 3).reshape(n_lru, LRU_WIDTH)
    b_x_full = sv_g[:, 10:12].reshape(N_DEV, n_lru, HEADS, hshard).transpose(1, 2, 0, 3).reshape(n_lru, LRU_WIDTH)
    ps_full = sv_g[:, 12:14].transpose(1, 0, 2).reshape(n_lru, D_MODEL)
    c_all = c_g.reshape(N_DEV, D_MODEL)

    (mod_g,) = _exchange([_mod_part(c_all, w_mod)], True, "gather_mod")
    mod_row = lax.dynamic_index_in_dim(mod_g, me, axis=2, keepdims=False)
    mod_row = mod_row.transpose(1, 0, 2).reshape(DEPTH, N_MOD * D_MODEL)
    table, token = _mod_table(mod_row, b_mod, norm_mix_g, norm_ffn_g)

    def gather_layer(i, token):
        j = i // 2
        if i == 0:
            mix = first_mix
        elif i % 2 == 0:
            mix, token = _send_start(f"gather_mix_start_{i}", [
                (jnp.stack([lru_w_y[j], lru_w_in[j], lru_w_out[j]]) + token[0, 0]).astype(bf16),
                (jnp.stack([lru_w_a[j], lru_w_x[j]]) + token[0, 0]).astype(bf16)], True, me)
        else:
            mix, token = _send_start(f"gather_mix_start_{i}", [(pool_w[j] + token[0, 0]).astype(bf16)], True, me)
        ffn, _ = _send_start(f"gather_ffn_start_{i}", [(ffn_w1[i] + token[0, 0]).astype(bf16),
                                                       (ffn_w2[i] + token[0, 0]).astype(bf16)], True, me)
        return mix, ffn

    coming = gather_layer(0, token)

    zero_row = jnp.zeros((1, LRU_WIDTH), f32)
    pvecs = [jnp.concatenate([lru_b_y[j:j + 1], lru_b_in[j:j + 1], lru_conv_b[j:j + 1], b_a_full[j:j + 1],
                              b_x_full[j:j + 1], lru_lambda[j:j + 1], lru_b_out[j:j + 1], zero_row,
                              conv_w_full[j], zero_row, zero_row, zero_row, zero_row], axis=0) for j in range(n_lru)]
    ps_rows = [jnp.concatenate([ps_full[j:j + 1], jnp.zeros((7, D_MODEL), f32)], axis=0) for j in range(n_lru)]

    saved = []
    ffn_w, mix_w = [], []
    h = xs
    for i in range(DEPTH):
        j = i // 2
        h_mix, h_ffn = coming
        got, _ = _send_wait(f"gather_mix_wait_{i}", h_mix, h)
        if i % 2 == 0:
            mix_w.append((got[0].transpose(1, 0, 2, 3).reshape(3, LRU_WIDTH, LRU_WIDTH),
                          got[1].transpose(1, 2, 0, 3, 4).reshape(2, HEADS, HEAD_DIM, HEAD_DIM)))
            h_mid, *lru_saved = _lru_fwd(h, table[i], mix_w[i][0], mix_w[i][1], pvecs[j], i)
            mix_saved = (h, tuple(lru_saved))
        else:
            mix_w.append((got[0].transpose(1, 0, 2, 3).reshape(HEADS, HEAD_DIM, HEAD_DIM),))
            h_mid, y_mix = _pool_fwd(h, table[i], mix_w[i][0], ps_rows[j], i)
            mix_saved = (h, y_mix)
        got, token = _send_wait(f"gather_ffn_wait_{i}", h_ffn, h_mid)
        ffn_w.append(got)
        if i + 1 < DEPTH:
            coming = gather_layer(i + 1, token)
        h_out, u, y_ffn, hb = _ffn_fwd(h_mid, table[i], ffn_w[i][0], ffn_w[i][1], i)
        saved.append((mix_saved, (h_mid, u, y_ffn, hb)))
        h = h_out
    fin_rows = jnp.concatenate([final_norm_g[None, :], jnp.zeros((7, D_MODEL), f32)], axis=0)
    dx, loss_part, sm_fin = _final(h, target, fin_rows)
    loss = lax.psum(loss_part[0, 0], ("x", "y", "c"))

    sm_ffn, sm_mix = [None] * DEPTH, [None] * DEPTH
    x_ffn, x_mix = [None] * DEPTH, [None] * DEPTH
    token = jnp.zeros((8, 128), f32)
    for i in reversed(range(DEPTH)):
        j = i // 2
        mix_saved, (h_mid, u, y_ffn, hb) = saved[i]
        dx, da, dyb, sm_ffn[i] = _ffn_bwd_act(h_mid, dx, u, y_ffn, table[i] + token[0, 0], ffn_w[i][0], ffn_w[i][1], i)
        x_ffn[i], token = _send_start(f"grads_ffn_start_{i}", [_ffn_bwd_w1(hb, da, i), _ffn_bwd_w2(u, dyb, i)], False, me)
        if i % 2 == 0:
            h_in, lru_saved = mix_saved
            dx, dbig, dsmall, sm_mix[i] = _lru_bwd(
                h_in, dx, lru_saved, table[i] + token[0, 0], mix_w[i][0], mix_w[i][1], pvecs[j], i)
            last_mix = [dbig, dsmall]
        else:
            h_in, y_mix = mix_saved
            dx, dpool, sm = _pool_bwd(h_in, dx, y_mix, table[i] + token[0, 0], mix_w[i][0], ps_rows[j], i)
            sm_mix[i] = jnp.concatenate([sm, jnp.zeros((8, D_MODEL), f32)], axis=0)
            last_mix = [dpool]
        if i > 0:
            x_mix[i], token = _send_start(f"grads_mix_start_{i}", last_mix, False, me)
    grad_x = dx[None]

    pack = _small_pack(sm_ffn, sm_mix, sm_fin, table + token[0, 0], norm_mix_g, norm_ffn_g, lru_lambda)
    (pack_g,) = _exchange([pack], True, "gather_small_grads")
    tot, token = _small_sum(pack_g)
    x_mix[0], _ = _send_start("grads_mix_start_0", [t + token[0, 0].astype(bf16) for t in last_mix], False, me)
    cols = w_mod.shape[2]
    dmod_all = lax.dynamic_slice_in_dim(pack_g[:, K_MOD:K_MOD + DEPTH * N_MOD].reshape(N_DEV, DEPTH, N_MOD * D_MODEL),
                                        me * cols, cols, axis=2).transpose(1, 0, 2)
    results = {"w_mod": _adamw_w_mod(c_all, dmod_all, w_mod, m_w_mod, v_w_mod)}

    after = results["w_mod"][1]
    l_ffn = [_send_wait(f"grads_ffn_wait_{i}", x_ffn[i], after)[0] for i in reversed(range(DEPTH))][::-1]

    def reduce_update(name, landings, kind, w, m, v):
        rows = w.size // w.shape[-1]
        two_d = (rows, w.shape[-1])
        lands = [t.reshape(N_DEV, -1, rows // len(landings), w.shape[-1]) for t in landings]
        outs = _adamw_reduce(name, lands, kind, w.reshape(two_d), m.reshape(two_d), v.reshape(two_d))
        return tuple(t.reshape(w.shape) for t in outs)

    results["ffn_w1"] = reduce_update("ffn_w1", [t[0] for t in l_ffn], 0, ffn_w1, m_ffn_w1, v_ffn_w1)
    results["ffn_w2"] = reduce_update("ffn_w2", [t[1] for t in l_ffn], 0, ffn_w2, m_ffn_w2, v_ffn_w2)
    after = results["ffn_w2"][1]
    l_mix = [_send_wait(f"grads_mix_wait_{i}", x_mix[i], after)[0] for i in reversed(range(DEPTH))][::-1]
    l_lru_big = [l_mix[i][0] for i in range(0, DEPTH, 2)]
    l_lru_small = [l_mix[i][1] for i in range(0, DEPTH, 2)]
    l_pool = [l_mix[i][0] for i in range(1, DEPTH, 2)]
    results["lru_w_y"] = reduce_update("lru_w_y", l_lru_big, 0, lru_w_y, m_lru_w_y, v_lru_w_y)
    results["lru_w_in"] = reduce_update("lru_w_in", l_lru_big, 1, lru_w_in, m_lru_w_in, v_lru_w_in)
    results["lru_w_out"] = reduce_update("lru_w_out", l_lru_big, 2, lru_w_out, m_lru_w_out, v_lru_w_out)
    results["lru_w_a"] = reduce_update("lru_w_a", l_lru_small, 0, lru_w_a, m_lru_w_a, v_lru_w_a)
    results["lru_w_x"] = reduce_update("lru_w_x", l_lru_small, 1, lru_w_x, m_lru_w_x, v_lru_w_x)
    results["pool_w"] = reduce_update("pool_w", l_pool, 0, pool_w, m_pool_w, v_pool_w)

    def my_cols(full, width):
        return lax.dynamic_slice_in_dim(full, me * width, width, axis=full.ndim - 1)

    lru_rows = tot[K_LRUB:K_LRUB + 5 * n_lru].reshape(n_lru, 5, LRU_WIDTH)
    small_grads = {
        "b_mod": tot[K_MOD:K_MOD + DEPTH * N_MOD].reshape(DEPTH, N_MOD * D_MODEL),
        "norm_mix_g": tot[K_NMIX:K_NMIX + DEPTH],
        "norm_ffn_g": tot[K_NFFN:K_NFFN + DEPTH],
        "lru_b_y": lru_rows[:, 0], "lru_b_in": lru_rows[:, 1], "lru_conv_b": lru_rows[:, 2],
        "lru_lambda": lru_rows[:, 3], "lru_b_out": lru_rows[:, 4],
        "lru_conv_w": my_cols(tot[K_CONVW:K_CONVW + 4 * n_lru].reshape(n_lru, 4, LRU_WIDTH), shard),
        "lru_b_a": my_cols(tot[K_BA:K_BA + n_lru].reshape(n_lru, HEADS, HEAD_DIM), hshard),
        "lru_b_x": my_cols(tot[K_BX:K_BX + n_lru].reshape(n_lru, HEADS, HEAD_DIM), hshard),
        "pool_scale": my_cols(tot[K_PS:K_PS + n_lru], shard),
        "final_norm_g": tot[K_FIN],
    }
    given = dict(b_mod=(b_mod, m_b_mod, v_b_mod), norm_mix_g=(norm_mix_g, m_norm_mix_g, v_norm_mix_g),
                 norm_ffn_g=(norm_ffn_g, m_norm_ffn_g, v_norm_ffn_g), lru_b_y=(lru_b_y, m_lru_b_y, v_lru_b_y),
                 lru_b_in=(lru_b_in, m_lru_b_in, v_lru_b_in), lru_conv_w=(lru_conv_w, m_lru_conv_w, v_lru_conv_w),
                 lru_conv_b=(lru_conv_b, m_lru_conv_b, v_lru_conv_b), lru_b_a=(lru_b_a, m_lru_b_a, v_lru_b_a),
                 lru_b_x=(lru_b_x, m_lru_b_x, v_lru_b_x), lru_lambda=(lru_lambda, m_lru_lambda, v_lru_lambda),
                 lru_b_out=(lru_b_out, m_lru_b_out, v_lru_b_out), pool_scale=(pool_scale, m_pool_scale, v_pool_scale),
                 final_norm_g=(final_norm_g, m_final_norm_g, v_final_norm_g))
    for name, g in small_grads.items():
        results[name] = (g,) + _adamw_small(name, g, *given[name])

    order = ["w_mod", "b_mod", "norm_mix_g", "norm_ffn_g", "lru_w_y", "lru_b_y", "lru_w_in", "lru_b_in", "lru_conv_w",
             "lru_conv_b", "lru_w_a", "lru_b_a", "lru_w_x", "lru_b_x", "lru_lambda", "lru_w_out", "lru_b_out", "pool_w",
             "pool_scale", "ffn_w1", "ffn_w2", "final_norm_g"]
    return (loss, grad_x, *[results[n][0] for n in order], *[results[n][1] for n in order],
            *[results[n][2] for n in order], *[results[n][3] for n in order])
```

```python
import functools
import math

import jax
import jax.numpy as jnp
from jax import lax
from jax.experimental import pallas as pl
from jax.experimental.pallas import tpu as pltpu

f32, bf16 = jnp.float32, jnp.bfloat16

D_MODEL = 1024
LRU_WIDTH = 1024
HEADS = 4
HEAD_DIM = 256
D_FF = 4096
DEPTH = 4
N_MOD = 6
N_DEV = 8
FF_CHUNK = D_FF // N_DEV
POOL_WINDOWS = (2, 4, 8, 16)
POOL_HALO = 16
EPS = 1e-6
LRU_C = 8.0

ADAM_LR = 0.001
ADAM_B1 = 0.9
ADAM_B2 = 0.999
ADAM_EPS = 1e-08
ADAM_WD = 0.01
ADAM_STEP = 10

V7X_VMEM_BYTES = 64 * 1024 * 1024
SUBLANES = 8
BF16_ROWS = 16

R_SH_M, R_SC_M, R_GT_M, R_SH_F, R_SC_F, R_GT_F, R_GS_M, R_GS_F = range(8)
P_BY, P_BIN, P_CONVB, P_BA, P_BX, P_LAM, P_BOUT, P_CW0 = 0, 1, 2, 3, 4, 5, 6, 8
G_SH, G_GS, G_GT, G_BY, G_BIN, G_CONVB, G_BA, G_BX, G_LS, G_BOUT, G_CW0 = 0, 1, 2, 3, 4, 5, 6, 7, 8, 9, 10
K_MOD, K_NMIX, K_NFFN, K_LRUB, K_CONVW, K_BA, K_BX, K_PS, K_FIN, K_ROWS = 0, 24, 28, 32, 42, 50, 52, 54, 56, 64


def _params(semantics=None, vmem_mb=48):
    return pltpu.CompilerParams(dimension_semantics=semantics, vmem_limit_bytes=vmem_mb * 1024 * 1024)


def _mm(a, b):
    return jnp.dot(a, b, preferred_element_type=f32)


def _mm_nt(a, b):
    return lax.dot_general(a, b, (((1,), (1,)), ((), ())), preferred_element_type=f32)


def _mm_tn(a, b):
    return lax.dot_general(a, b, (((0,), (0,)), ((), ())), preferred_element_type=f32)


def _rms(x):
    r = lax.rsqrt(jnp.mean(x * x, axis=-1, keepdims=True) + EPS)
    return x * r, r


def _norm_bwd(dh, n, r, gs):
    dn = dh * gs
    return r * (dn - n * jnp.mean(dn * n, axis=-1, keepdims=True))


def _colsum(v):
    return jnp.sum(v, axis=0, keepdims=True)


def _sigmoid(v):
    return 0.5 * jnp.tanh(0.5 * v) + 0.5


def _log_sigmoid(v):
    return jnp.minimum(v, 0.0) - jnp.log1p(jnp.exp(-jnp.abs(v)))


_GELU_C = 0.7978845608028654
_GELU_A = 0.044715


def _gelu_and_grad(v):
    v2 = v * v
    t = jnp.tanh(_GELU_C * v * (1.0 + _GELU_A * v2))
    p = 0.5 + 0.5 * t
    return v * p, p + (0.5 * v) * (1.0 - t * t) * (_GELU_C + (3.0 * _GELU_A * _GELU_C) * v2)


def _rows_before(halo, v, shifts):
    hr = halo.shape[0]
    ext = jnp.concatenate([halo, v], axis=0)
    return [pltpu.roll(ext, k, 0)[hr:] for k in shifts]


def _rows_after(v, halo, shifts):
    n = v.shape[0]
    ext = jnp.concatenate([v, halo], axis=0)
    return [pltpu.roll(ext, ext.shape[0] - k, 0)[:n] for k in shifts]


def _shift_matrix(n, halo_rows, shifts):
    rows = lax.broadcasted_iota(jnp.int32, (n, n + halo_rows), 0)
    cols = lax.broadcasted_iota(jnp.int32, (n, n + halo_rows), 1)
    return jnp.concatenate([(cols == rows + halo_rows - k).astype(bf16) for k in shifts], axis=0)


def _shifted_rows(sel, halo, v):
    n = v.shape[0]
    out = _mm(sel, jnp.concatenate([halo, v], axis=0))
    return [out[j * n:(j + 1) * n] for j in range(sel.shape[0] // n)]


def _block_diag(v, w_ref, kind):
    return jnp.concatenate(
        [_mm(v[:, h * HEAD_DIM:(h + 1) * HEAD_DIM], w_ref[kind, h]) for h in range(HEADS)], axis=1)


def _block_diag_t(v, w_ref, kind):
    return jnp.concatenate(
        [_mm_nt(v[:, h * HEAD_DIM:(h + 1) * HEAD_DIM], w_ref[kind, h]) for h in range(HEADS)], axis=1)


def _exchange(arrays, gather, name):
    n = len(arrays)
    peers = N_DEV - 1

    def body(*refs):
        ins, outs = refs[:n], refs[n:2 * n]
        send_sems, recv_sems, local_sems = refs[2 * n:]
        x, y, c = lax.axis_index("x"), lax.axis_index("y"), lax.axis_index("c")
        me = 4 * x + 2 * y + c
        local = []
        for k in range(n):
            cp = pltpu.make_async_copy(ins[k] if gather else ins[k].at[me], outs[k].at[me], local_sems.at[k])
            cp.start()
            local.append(cp)
        remote = []
        for p in range(1, N_DEV):
            px = 1 - x if p & 4 else x
            py = 1 - y if p & 2 else y
            pc = 1 - c if p & 1 else c
            for k in range(n):
                cp = pltpu.make_async_remote_copy(
                    src_ref=ins[k] if gather else ins[k].at[4 * px + 2 * py + pc],
                    dst_ref=outs[k].at[me],
                    send_sem=send_sems.at[k * peers + p - 1],
                    recv_sem=recv_sems.at[k * peers + p - 1],
                    device_id=(px, py, pc), device_id_type=pl.DeviceIdType.MESH)
                cp.start()
                remote.append(cp)
        for cp in remote:
            cp.wait()
        for cp in local:
            cp.wait()

    out_shape = tuple(
        jax.ShapeDtypeStruct(((N_DEV,) + a.shape) if gather else a.shape, a.dtype) for a in arrays)
    outs = pl.pallas_call(
        body, name=name, out_shape=out_shape,
        in_specs=[pl.BlockSpec(memory_space=pl.ANY)] * n,
        out_specs=tuple(pl.BlockSpec(memory_space=pl.ANY) for _ in range(n)),
        scratch_shapes=[pltpu.SemaphoreType.DMA((n * peers,)), pltpu.SemaphoreType.DMA((n * peers,)),
                        pltpu.SemaphoreType.DMA((n,))],
        compiler_params=pltpu.CompilerParams(has_side_effects=True),
    )(*arrays)
    return list(outs)


_HBM = pl.BlockSpec(memory_space=pltpu.HBM)
_SEM = pl.BlockSpec(memory_space=pltpu.SEMAPHORE)
_DATAFLOW = pltpu.SideEffectType.DATAFLOW_SIDE_EFFECTING


def _peer_copies(src_refs, land_refs, send_sems, recv_sems, gather):
    x, y, c = lax.axis_index("x"), lax.axis_index("y"), lax.axis_index("c")
    me = 4 * x + 2 * y + c
    peers = N_DEV - 1
    copies = []
    for p in range(1, N_DEV):
        px = 1 - x if p & 4 else x
        py = 1 - y if p & 2 else y
        pc = 1 - c if p & 1 else c
        for k in range(len(src_refs)):
            copies.append(pltpu.make_async_remote_copy(
                src_ref=src_refs[k] if gather else src_refs[k].at[4 * px + 2 * py + pc],
                dst_ref=land_refs[k].at[me],
                send_sem=send_sems.at[k * peers + p - 1], recv_sem=recv_sems.at[k * peers + p - 1],
                device_id=(px, py, pc), device_id_type=pl.DeviceIdType.MESH))
    return copies


def _landing(srcs, gather, me):
    out = []
    for a in srcs:
        own = a if gather else lax.dynamic_index_in_dim(a, me, 0, keepdims=False)
        out.append(lax.dynamic_update_index_in_dim(lax.empty((N_DEV,) + own.shape, own.dtype), own, me, 0))
    return out


def _send_start(name, srcs, gather, me):
    n = len(srcs)
    lands = _landing(srcs, gather, me)

    def body(*refs):
        src_refs, land_refs = refs[:n], refs[n:2 * n]
        send_sems, recv_sems, token = refs[2 * n], refs[2 * n + 1], refs[-1]
        for cp in _peer_copies(src_refs, land_refs, send_sems, recv_sems, gather):
            cp.start()
        token[...] = jnp.zeros_like(token)

    sems = pltpu.SemaphoreType.DMA((n * (N_DEV - 1),))
    outs = pl.pallas_call(
        body, name=name,
        out_shape=(sems, sems, *[pltpu.HBM(a.shape, a.dtype) for a in (*srcs, *lands)], jax.ShapeDtypeStruct((8, 128), f32)),
        in_specs=[_HBM] * (2 * n),
        out_specs=(_SEM, _SEM, *[_HBM] * (2 * n), pl.BlockSpec(memory_space=pltpu.VMEM)),
        input_output_aliases={k: 2 + k for k in range(2 * n)},
        compiler_params=pltpu.CompilerParams(has_side_effects=_DATAFLOW),
    )(*[pltpu.with_memory_space_constraint(a, pltpu.HBM) for a in (*srcs, *lands)])
    return (outs[0], outs[1], list(outs[2:2 + n]), list(outs[2 + n:2 + 2 * n]), gather), outs[-1]


def _send_wait(name, handle, after):
    send_sems, recv_sems, srcs, lands, gather = handle
    n = len(srcs)

    def body(*refs):
        src_refs, land_refs = refs[:n], refs[n:2 * n]
        for cp in _peer_copies(src_refs, land_refs, refs[2 * n], refs[2 * n + 1], gather):
            cp.wait_send()
            cp.wait_recv()
        refs[-1][...] = jnp.zeros_like(refs[-1])

    outs = pl.pallas_call(
        body, name=name,
        out_shape=(*[pltpu.HBM(a.shape, a.dtype) for a in (*srcs, *lands)], jax.ShapeDtypeStruct((8, 128), f32)),
        in_specs=[_HBM] * (2 * n) + [_SEM, _SEM, pl.BlockSpec(memory_space=pl.ANY)],
        out_specs=(*[_HBM] * (2 * n), pl.BlockSpec(memory_space=pltpu.VMEM)),
        input_output_aliases={k: k for k in range(2 * n)},
        compiler_params=pltpu.CompilerParams(has_side_effects=_DATAFLOW),
    )(*srcs, *lands, send_sems, recv_sems, after)
    return list(outs[n:2 * n]), outs[-1]


def _mod_part(c_all, w_mod):
    depth, d, cols = w_mod.shape

    def body(c_ref, w_ref, o_ref):
        cv = c_ref[...]
        cond = cv * _sigmoid(cv)
        o_ref[...] = jnp.dot(cond, w_ref[...], preferred_element_type=f32, precision=lax.Precision.HIGHEST)

    return pl.pallas_call(
        body, name="mod_part", grid=(depth,),
        out_shape=jax.ShapeDtypeStruct((depth, N_DEV, cols), f32),
        in_specs=[pl.BlockSpec((N_DEV, d), lambda i: (0, 0)), pl.BlockSpec((None, d, cols), lambda i: (i, 0, 0))],
        out_specs=pl.BlockSpec((None, N_DEV, cols), lambda i: (i, 0, 0)),
        compiler_params=_params(("arbitrary",), 32),
    )(c_all, w_mod)


def _mod_table(mod_row, b_mod, g_mix, g_ffn):
    def body(m_ref, b_ref, gm_ref, gf_ref, o_ref, token_ref):
        for i in range(DEPTH):
            for k in range(N_MOD):
                o_ref[i, k:k + 1, :] = m_ref[i:i + 1, k * D_MODEL:(k + 1) * D_MODEL] + b_ref[i:i + 1, k * D_MODEL:(k + 1) * D_MODEL]
            o_ref[i, R_GS_M:R_GS_M + 1, :] = gm_ref[i:i + 1, :] * (1.0 + o_ref[i, R_SC_M:R_SC_M + 1, :])
            o_ref[i, R_GS_F:R_GS_F + 1, :] = gf_ref[i:i + 1, :] * (1.0 + o_ref[i, R_SC_F:R_SC_F + 1, :])
        token_ref[...] = jnp.zeros_like(token_ref)

    return pl.pallas_call(
        body, name="mod_table",
        out_shape=(jax.ShapeDtypeStruct((DEPTH, 8, D_MODEL), f32), jax.ShapeDtypeStruct((8, 128), f32)))(
        mod_row, b_mod, g_mix, g_ffn)


def _ffn_tile(s):
    return min(512, s)


def _layer_weights(shape):
    return pl.BlockSpec((N_DEV,) + shape, lambda i: (0, 0, 0))


def _ffn_fwd(x, vec, w1g, w2g, layer):
    s = x.shape[0]
    ts = _ffn_tile(s)

    def body(x_ref, vec_ref, w1_ref, w2_ref, xo_ref, u_ref, y_ref, hb_ref):
        xv = x_ref[...]
        n, _ = _rms(xv)
        hb = (n * vec_ref[R_GS_F:R_GS_F + 1, :] + vec_ref[R_SH_F:R_SH_F + 1, :]).astype(bf16)
        hb_ref[...] = hb
        yv = jnp.zeros((ts, D_MODEL), f32)
        for f in range(N_DEV):
            u = jnp.maximum(_mm(hb, w1_ref[f]), 0.0)
            u_ref[:, f * FF_CHUNK:(f + 1) * FF_CHUNK] = u.astype(bf16)
            yv = yv + _mm((u * u).astype(bf16), w2_ref[f])
        y_ref[...] = yv.astype(bf16)
        xo_ref[...] = xv + vec_ref[R_GT_F:R_GT_F + 1, :] * yv

    row = pl.BlockSpec((ts, D_MODEL), lambda i: (i, 0))
    return pl.pallas_call(
        body, name=f"ffn_fwd_{layer}", grid=(s // ts,),
        out_shape=(jax.ShapeDtypeStruct((s, D_MODEL), f32), jax.ShapeDtypeStruct((s, D_FF), bf16),
                   jax.ShapeDtypeStruct((s, D_MODEL), bf16), jax.ShapeDtypeStruct((s, D_MODEL), bf16)),
        in_specs=[row, pl.BlockSpec((8, D_MODEL), lambda i: (0, 0)),
                  _layer_weights((D_MODEL, FF_CHUNK)), _layer_weights((FF_CHUNK, D_MODEL))],
        out_specs=(row, pl.BlockSpec((ts, D_FF), lambda i: (i, 0)), row, row),
        compiler_params=_params(("arbitrary",), 56),
    )(x, vec, w1g, w2g)


def _ffn_bwd_act(x, dx, u, y, vec, w1g, w2g, layer):
    s = x.shape[0]
    ts = _ffn_tile(s)

    def body(x_ref, dx_ref, u_ref, y_ref, vec_ref, w1_ref, w2_ref, dxo_ref, da_ref, dyb_ref, sm_ref):
        @pl.when(pl.program_id(0) == 0)
        def _():
            sm_ref[...] = jnp.zeros_like(sm_ref)

        dxv = dx_ref[...]
        dyb = (dxv * vec_ref[R_GT_F:R_GT_F + 1, :]).astype(bf16)
        dyb_ref[...] = dyb
        sm_ref[G_GT:G_GT + 1, :] += _colsum(dxv * y_ref[...].astype(f32))
        dh = jnp.zeros((ts, D_MODEL), f32)
        for f in range(N_DEV):
            cols = slice(f * FF_CHUNK, (f + 1) * FF_CHUNK)
            dz = _mm_nt(dyb, w2_ref[f])
            dab = (dz * (2.0 * u_ref[:, cols].astype(f32))).astype(bf16)
            da_ref[:, cols] = dab
            dh = dh + _mm_nt(dab, w1_ref[f])
        n, r = _rms(x_ref[...])
        sm_ref[G_SH:G_SH + 1, :] += _colsum(dh)
        sm_ref[G_GS:G_GS + 1, :] += _colsum(dh * n)
        dxo_ref[...] = dxv + _norm_bwd(dh, n, r, vec_ref[R_GS_F:R_GS_F + 1, :])

    row = pl.BlockSpec((ts, D_MODEL), lambda i: (i, 0))
    wide = pl.BlockSpec((ts, D_FF), lambda i: (i, 0))
    return pl.pallas_call(
        body, name=f"ffn_bwd_act_{layer}", grid=(s // ts,),
        out_shape=(jax.ShapeDtypeStruct((s, D_MODEL), f32), jax.ShapeDtypeStruct((s, D_FF), bf16),
                   jax.ShapeDtypeStruct((s, D_MODEL), bf16), jax.ShapeDtypeStruct((8, D_MODEL), f32)),
        in_specs=[row, row, wide, row, pl.BlockSpec((8, D_MODEL), lambda i: (0, 0)),
                  _layer_weights((D_MODEL, FF_CHUNK)), _layer_weights((FF_CHUNK, D_MODEL))],
        out_specs=(row, wide, row, pl.BlockSpec((8, D_MODEL), lambda i: (0, 0))),
        compiler_params=_params(("arbitrary",), 58),
    )(x, dx, u, y, vec, w1g, w2g)


def _ffn_bwd_w1(hb, da, layer):
    s = hb.shape[0]
    ts = _ffn_tile(s)
    nt = s // ts

    def body(hb_ref, da_ref, dw_ref, acc_ref):
        i = pl.program_id(0)

        @pl.when(i == 0)
        def _():
            acc_ref[...] = jnp.zeros_like(acc_ref)

        hb = hb_ref[...]
        for f in range(N_DEV):
            acc_ref[f] += _mm_tn(hb, da_ref[:, f * FF_CHUNK:(f + 1) * FF_CHUNK])

        @pl.when(i == nt - 1)
        def _():
            dw_ref[...] = acc_ref[...].astype(bf16)

    return pl.pallas_call(
        body, name=f"ffn_bwd_w1_{layer}", grid=(nt,),
        out_shape=jax.ShapeDtypeStruct((N_DEV, D_MODEL, FF_CHUNK), bf16),
        in_specs=[pl.BlockSpec((ts, D_MODEL), lambda i: (i, 0)), pl.BlockSpec((ts, D_FF), lambda i: (i, 0))],
        out_specs=pl.BlockSpec((N_DEV, D_MODEL, FF_CHUNK), lambda i: (0, 0, 0)),
        scratch_shapes=[pltpu.VMEM((N_DEV, D_MODEL, FF_CHUNK), f32)],
        compiler_params=_params(("arbitrary",), 56),
    )(hb, da)


def _ffn_bwd_w2(u, dyb, layer):
    s = u.shape[0]
    ts = _ffn_tile(s)
    nt = s // ts

    def body(u_ref, dyb_ref, dw_ref, acc_ref):
        i = pl.program_id(0)

        @pl.when(i == 0)
        def _():
            acc_ref[...] = jnp.zeros_like(acc_ref)

        dyb = dyb_ref[...]
        for f in range(N_DEV):
            uv = u_ref[:, f * FF_CHUNK:(f + 1) * FF_CHUNK].astype(f32)
            acc_ref[f] += _mm_tn((uv * uv).astype(bf16), dyb)

        @pl.when(i == nt - 1)
        def _():
            dw_ref[...] = acc_ref[...].astype(bf16)

    return pl.pallas_call(
        body, name=f"ffn_bwd_w2_{layer}", grid=(nt,),
        out_shape=jax.ShapeDtypeStruct((N_DEV, FF_CHUNK, D_MODEL), bf16),
        in_specs=[pl.BlockSpec((ts, D_FF), lambda i: (i, 0)), pl.BlockSpec((ts, D_MODEL), lambda i: (i, 0))],
        out_specs=pl.BlockSpec((N_DEV, FF_CHUNK, D_MODEL), lambda i: (0, 0, 0)),
        scratch_shapes=[pltpu.VMEM((N_DEV, FF_CHUNK, D_MODEL), f32)],
        compiler_params=_params(("arbitrary",), 56),
    )(u, dyb)


def _lru_gates(xc, wsm_ref, pv_ref):
    xcb = xc.astype(bf16)
    gr = _sigmoid(_block_diag(xcb, wsm_ref, 0) + pv_ref[P_BA:P_BA + 1, :])
    gi = _sigmoid(_block_diag(xcb, wsm_ref, 1) + pv_ref[P_BX:P_BX + 1, :])
    log_a = (LRU_C * _log_sigmoid(pv_ref[P_LAM:P_LAM + 1, :])) * gr
    t = jnp.tanh(log_a)
    return gr, gi, jnp.exp(log_a), jnp.sqrt((-2.0 * t) / (1.0 - t))


def _conv(xr, taps_before, pv_ref):
    xc = xr * pv_ref[P_CW0 + 3:P_CW0 + 4, :] + pv_ref[P_CONVB:P_CONVB + 1, :]
    for k, v in zip((2, 1, 0), taps_before):
        xc = xc + v * pv_ref[P_CW0 + k:P_CW0 + k + 1, :]
    return xc


LRU_FWD_SUB, LRU_FWD_SUBS = 128, 2
LRU_BWD_SUB, LRU_BWD_SUBS = 256, 1


def _scan_rows(a, u, carry, reverse):
    groups = a.shape[0] // SUBLANES
    row = lax.broadcasted_iota(jnp.int32, (SUBLANES, a.shape[1]), 0)
    outs = [None] * groups
    for j in range(groups):
        g = groups - 1 - j if reverse else j
        av, uv = a[g * SUBLANES:(g + 1) * SUBLANES], u[g * SUBLANES:(g + 1) * SUBLANES]
        for k in (1, 2, 4):
            if reverse:
                valid, shift = row < SUBLANES - k, SUBLANES - k
            else:
                valid, shift = row >= k, k
            a_s = jnp.where(valid, pltpu.roll(av, shift, 0), 1.0)
            u_s = jnp.where(valid, pltpu.roll(uv, shift, 0), 0.0)
            uv = uv + av * u_s
            av = av * a_s
        h = uv + av * carry
        outs[g] = h
        carry = h[0:1, :] if reverse else h[SUBLANES - 1:SUBLANES, :]
    return jnp.concatenate(outs, axis=0), carry


def _lru_fwd(x, vec, wbig, wsm, pvec, layer):
    s = x.shape[0]
    sub = min(LRU_FWD_SUB, s)
    ts = min(sub * LRU_FWD_SUBS, s)
    nsub = ts // sub
    w = LRU_WIDTH

    def body(x_ref, vec_ref, wb_ref, wsm_ref, pv_ref, xo_ref, xr_ref, hs_ref, a_ref, mult_ref, gr_ref, gi_ref,
             gel_ref, geld_ref, y_ref, tail_ref, carry_ref):
        @pl.when(pl.program_id(0) == 0)
        def _():
            tail_ref[...] = jnp.zeros_like(tail_ref)
            carry_ref[...] = jnp.zeros_like(carry_ref)

        sel = _shift_matrix(sub, BF16_ROWS, (1, 2, 3))
        for k in range(nsub):
            rows = slice(k * sub, (k + 1) * sub)
            xv = x_ref[rows, :]
            n, _ = _rms(xv)
            hb = (n * vec_ref[R_GS_M:R_GS_M + 1, :] + vec_ref[R_SH_M:R_SH_M + 1, :]).astype(bf16)
            gelu_v, gelu_d = _gelu_and_grad(_mm(hb, wb_ref[0]) + pv_ref[P_BY:P_BY + 1, :])
            gel_ref[rows, :] = gelu_v.astype(bf16)
            geld_ref[rows, :] = gelu_d.astype(bf16)
            xrb = (_mm(hb, wb_ref[1]) + pv_ref[P_BIN:P_BIN + 1, :]).astype(bf16)
            xr_ref[rows, :] = xrb
            xc = _conv(xrb.astype(f32), _shifted_rows(sel, tail_ref[...], xrb), pv_ref)
            tail_ref[...] = xrb[sub - BF16_ROWS:, :]
            gr, gi, a, mult = _lru_gates(xc, wsm_ref, pv_ref)
            gr_ref[rows, :] = gr.astype(bf16)
            gi_ref[rows, :] = gi.astype(bf16)
            a_ref[rows, :] = a
            mult_ref[rows, :] = mult
            hs, carry = _scan_rows(a, mult * (gi * xc), carry_ref[0:1, :], reverse=False)
            carry_ref[0:1, :] = carry
            hs_ref[rows, :] = hs
            yv = _mm((hs * gelu_v).astype(bf16), wb_ref[2]) + pv_ref[P_BOUT:P_BOUT + 1, :]
            y_ref[rows, :] = yv.astype(bf16)
            xo_ref[rows, :] = xv + vec_ref[R_GT_M:R_GT_M + 1, :] * yv

    row = pl.BlockSpec((ts, D_MODEL), lambda i: (i, 0))
    roww = pl.BlockSpec((ts, w), lambda i: (i, 0))
    wide = lambda dt: jax.ShapeDtypeStruct((s, w), dt)
    return pl.pallas_call(
        body, name=f"lru_fwd_{layer}", grid=(s // ts,),
        out_shape=(jax.ShapeDtypeStruct((s, D_MODEL), f32), wide(bf16), wide(f32), wide(f32), wide(f32),
                   wide(bf16), wide(bf16), wide(bf16), wide(bf16), jax.ShapeDtypeStruct((s, D_MODEL), bf16)),
        in_specs=[row, pl.BlockSpec((8, D_MODEL), lambda i: (0, 0)),
                  pl.BlockSpec((3, w, w), lambda i: (0, 0, 0)),
                  pl.BlockSpec((2, HEADS, HEAD_DIM, HEAD_DIM), lambda i: (0, 0, 0, 0)),
                  pl.BlockSpec((16, w), lambda i: (0, 0))],
        out_specs=(row, roww, roww, roww, roww, roww, roww, roww, roww, row),
        scratch_shapes=[pltpu.VMEM((BF16_ROWS, w), bf16), pltpu.VMEM((SUBLANES, w), f32)],
        compiler_params=_params(("arbitrary",)),
    )(x, vec, wbig, wsm, pvec)


def _lru_bwd(x, dx, saved, vec, wbig, wsm, pvec, layer):
    xr, hs, a_all, mult_all, gr_all, gi_all, gel_all, geld_all, y = saved
    s = x.shape[0]
    sub = min(LRU_BWD_SUB, s)
    ts = min(sub * LRU_BWD_SUBS, s)
    nsub = ts // sub
    nt = s // ts
    w = LRU_WIDTH
    shard = w // N_DEV
    hshard = HEAD_DIM // N_DEV

    def body(x_ref, dx_ref, xr_ref, xrh_ref, hs_ref, hsh_ref, a_ref, mult_ref, gr_ref, gi_ref, gel_ref, geld_ref,
             y_ref, vec_ref, wb_ref, wsm_ref, pv_ref,
             dxo_ref, dwb_ref, dwsm_ref, sm_ref, accb_ref, accs_ref, eps_ref, dxc8_ref,
             hb_scr, dgb_scr, dxrb_scr, mb_scr, dyb_scr, xcb_scr, drab_scr, drxb_scr):
        i = pl.program_id(0)
        first_tile = i == nt - 1

        @pl.when(i == 0)
        def _():
            accb_ref[...] = jnp.zeros_like(accb_ref)
            accs_ref[...] = jnp.zeros_like(accs_ref)
            sm_ref[...] = jnp.zeros_like(sm_ref)
            eps_ref[...] = jnp.zeros_like(eps_ref)
            dxc8_ref[...] = jnp.zeros_like(dxc8_ref)

        gs = vec_ref[R_GS_M:R_GS_M + 1, :]
        c_ls = LRU_C * _log_sigmoid(pv_ref[P_LAM:P_LAM + 1, :])
        for k in reversed(range(nsub)):
            rows = slice(k * sub, (k + 1) * sub)
            xv = x_ref[rows, :]
            dxv = dx_ref[rows, :]
            n, r = _rms(xv)
            hb_scr[rows, :] = (n * gs + vec_ref[R_SH_M:R_SH_M + 1, :]).astype(bf16)
            xrv = xr_ref[rows, :].astype(f32)
            hsv = hs_ref[rows, :]
            if k == 0:
                xr_halo = jnp.where(first_tile, 0.0, xrh_ref[...].astype(f32))
                hs_halo = jnp.where(first_tile, 0.0, hsh_ref[...])
            else:
                xr_halo = xr_ref[k * sub - BF16_ROWS:k * sub, :].astype(f32)
                hs_halo = hs_ref[k * sub - SUBLANES:k * sub, :]
            xs1, xs2, xs3 = _rows_before(xr_halo, xrv, (1, 2, 3))
            xc = _conv(xrv, (xs1, xs2, xs3), pv_ref)
            xcb_scr[rows, :] = xc.astype(bf16)
            a, mult = a_ref[rows, :], mult_ref[rows, :]
            gr, gi = gr_ref[rows, :].astype(f32), gi_ref[rows, :].astype(f32)
            gelu_v = gel_ref[rows, :].astype(f32)

            dy = dxv * vec_ref[R_GT_M:R_GT_M + 1, :]
            dyb = dy.astype(bf16)
            dyb_scr[rows, :] = dyb
            sm_ref[G_GT:G_GT + 1, :] += _colsum(dxv * y_ref[rows, :].astype(f32))
            sm_ref[G_BOUT:G_BOUT + 1, :] += _colsum(dy)
            mb_scr[rows, :] = (hsv * gelu_v).astype(bf16)
            dm = _mm_nt(dyb, wb_ref[2])
            dhs = dm * gelu_v
            dgpre = dm * hsv * geld_ref[rows, :].astype(f32)
            dgb = dgpre.astype(bf16)
            dgb_scr[rows, :] = dgb
            sm_ref[G_BY:G_BY + 1, :] += _colsum(dgpre)

            eps_in = eps_ref[0:1, :]
            eps, eps_out = _scan_rows(a, a * dhs, eps_in, reverse=True)
            eps_ref[0:1, :] = eps_out
            (eps_next,) = _rows_after(eps, jnp.broadcast_to(eps_in, (SUBLANES, w)), (1,))
            delta = dhs + eps_next
            (h_prev,) = _rows_before(hs_halo, hsv, (1,))
            dxi = delta * xc
            dgi = dxi * mult
            dla = (delta * h_prev) * a - (dxi * gi) * (a * a) / mult
            sm_ref[G_LS:G_LS + 1, :] += _colsum(dla * gr)
            dra = (dla * c_ls) * (gr - gr * gr)
            drx = dgi * (gi - gi * gi)
            drab, drxb = dra.astype(bf16), drx.astype(bf16)
            drab_scr[rows, :] = drab
            drxb_scr[rows, :] = drxb
            sm_ref[G_BA:G_BA + 1, :] += _colsum(dra)
            sm_ref[G_BX:G_BX + 1, :] += _colsum(drx)
            dxc = (delta * mult) * gi + _block_diag_t(drab, wsm_ref, 0) + _block_diag_t(drxb, wsm_ref, 1)

            sm_ref[G_CONVB:G_CONVB + 1, :] += _colsum(dxc)
            for kk, v in zip((3, 2, 1, 0), (xrv, xs1, xs2, xs3)):
                sm_ref[G_CW0 + kk:G_CW0 + kk + 1, :] += _colsum(dxc * v)
            ups = _rows_after(dxc, dxc8_ref[...], (1, 2, 3))
            dxc8_ref[...] = dxc[0:SUBLANES, :]
            dxr = dxc * pv_ref[P_CW0 + 3:P_CW0 + 4, :]
            for kk, v in zip((2, 1, 0), ups):
                dxr = dxr + v * pv_ref[P_CW0 + kk:P_CW0 + kk + 1, :]
            dxrb = dxr.astype(bf16)
            dxrb_scr[rows, :] = dxrb
            sm_ref[G_BIN:G_BIN + 1, :] += _colsum(dxr)
            dh = _mm_nt(dgb, wb_ref[0]) + _mm_nt(dxrb, wb_ref[1])
            sm_ref[G_SH:G_SH + 1, :] += _colsum(dh)
            sm_ref[G_GS:G_GS + 1, :] += _colsum(dh * n)
            dxo_ref[rows, :] = dxv + _norm_bwd(dh, n, r, gs)

        hb = hb_scr[...]
        accb_ref[0] += _mm_tn(hb, dgb_scr[...])
        accb_ref[1] += _mm_tn(hb, dxrb_scr[...])
        accb_ref[2] += _mm_tn(mb_scr[...], dyb_scr[...])
        for h in range(HEADS):
            cols = slice(h * HEAD_DIM, (h + 1) * HEAD_DIM)
            accs_ref[0, h] += _mm_tn(xcb_scr[:, cols], drab_scr[:, cols])
            accs_ref[1, h] += _mm_tn(xcb_scr[:, cols], drxb_scr[:, cols])

        @pl.when(i == nt - 1)
        def _():
            sm_ref[G_LS:G_LS + 1, :] = sm_ref[G_LS:G_LS + 1, :] * LRU_C
            for k in range(3):
                dwb_ref[:, k] = accb_ref[k].astype(bf16).reshape(N_DEV, shard, w)
            for k in range(2):
                for h in range(HEADS):
                    dwsm_ref[:, k, h] = accs_ref[k, h].astype(bf16).reshape(N_DEV, hshard, HEAD_DIM)

    rev = lambda i: (nt - 1 - i, 0)
    row = pl.BlockSpec((ts, D_MODEL), rev)
    roww = pl.BlockSpec((ts, w), rev)
    halo16 = pl.BlockSpec((BF16_ROWS, w), lambda i: (jnp.maximum((nt - 1 - i) * (ts // BF16_ROWS) - 1, 0), 0))
    halo8 = pl.BlockSpec((SUBLANES, w), lambda i: (jnp.maximum((nt - 1 - i) * (ts // SUBLANES) - 1, 0), 0))
    const = lambda *shape: pl.BlockSpec(shape, lambda i: (0,) * len(shape))
    operand = pltpu.VMEM((ts, w), bf16)
    return pl.pallas_call(
        body, name=f"lru_bwd_{layer}", grid=(nt,),
        out_shape=(jax.ShapeDtypeStruct((s, D_MODEL), f32),
                   jax.ShapeDtypeStruct((N_DEV, 3, shard, w), bf16),
                   jax.ShapeDtypeStruct((N_DEV, 2, HEADS, hshard, HEAD_DIM), bf16),
                   jax.ShapeDtypeStruct((16, w), f32)),
        in_specs=[row, row, roww, halo16, roww, halo8, roww, roww, roww, roww, roww, roww, row, const(8, D_MODEL),
                  const(3, w, w), const(2, HEADS, HEAD_DIM, HEAD_DIM), const(16, w)],
        out_specs=(row, const(N_DEV, 3, shard, w), const(N_DEV, 2, HEADS, hshard, HEAD_DIM), const(16, w)),
        scratch_shapes=[pltpu.VMEM((3, w, w), f32), pltpu.VMEM((2, HEADS, HEAD_DIM, HEAD_DIM), f32),
                        pltpu.VMEM((SUBLANES, w), f32), pltpu.VMEM((SUBLANES, w), f32)] + [operand] * 8,
        compiler_params=_params(("arbitrary",), 58),
    )(x, dx, xr, xr, hs, hs, a_all, mult_all, gr_all, gi_all, gel_all, geld_all, y, vec, wbig, wsm, pvec)


def _pool_tile(s):
    return min(256, s)


def _pool_counts(tile_index, ts):
    t = (tile_index * ts + lax.broadcasted_iota(jnp.int32, (ts, 1), 0) + 1).astype(f32)
    return [1.0 / jnp.minimum(t, float(win)) for win in POOL_WINDOWS]


def _pooled(h, halo, inv):
    ext = jnp.concatenate([halo, h], axis=0)
    out = []
    for g in range(len(POOL_WINDOWS)):
        acc = ext[:, g * HEAD_DIM:(g + 1) * HEAD_DIM]
        for step in range(g + 1):
            acc = acc + pltpu.roll(acc, 1 << step, 0)
        out.append(acc[POOL_HALO:] * inv[g] - h[:, g * HEAD_DIM:(g + 1) * HEAD_DIM])
    return out


def _pool_fwd(x, vec, pw, ps, layer):
    s = x.shape[0]
    ts = _pool_tile(s)

    def body(x_ref, vec_ref, pw_ref, ps_ref, xo_ref, y_ref, halo_ref):
        i = pl.program_id(0)

        @pl.when(i == 0)
        def _():
            halo_ref[...] = jnp.zeros_like(halo_ref)

        xv = x_ref[...]
        n, _ = _rms(xv)
        h = n * vec_ref[R_GS_M:R_GS_M + 1, :] + vec_ref[R_SH_M:R_SH_M + 1, :]
        pooled = _pooled(h, halo_ref[...], _pool_counts(i, ts))
        halo_ref[...] = h[ts - POOL_HALO:, :]
        mixed = jnp.concatenate([_mm(pooled[g].astype(bf16), pw_ref[g]) for g in range(HEADS)], axis=1)
        yv = mixed * ps_ref[0:1, :]
        y_ref[...] = yv.astype(bf16)
        xo_ref[...] = xv + vec_ref[R_GT_M:R_GT_M + 1, :] * yv

    row = pl.BlockSpec((ts, D_MODEL), lambda i: (i, 0))
    return pl.pallas_call(
        body, name=f"pool_fwd_{layer}", grid=(s // ts,),
        out_shape=(jax.ShapeDtypeStruct((s, D_MODEL), f32), jax.ShapeDtypeStruct((s, D_MODEL), bf16)),
        in_specs=[row, pl.BlockSpec((8, D_MODEL), lambda i: (0, 0)),
                  pl.BlockSpec((HEADS, HEAD_DIM, HEAD_DIM), lambda i: (0, 0, 0)),
                  pl.BlockSpec((8, D_MODEL), lambda i: (0, 0))],
        out_specs=(row, row),
        scratch_shapes=[pltpu.VMEM((POOL_HALO, D_MODEL), f32)],
        compiler_params=_params(("arbitrary",)),
    )(x, vec, pw, ps)


def _pool_bwd(x, dx, y, vec, pw, ps, layer):
    s = x.shape[0]
    ts = _pool_tile(s)
    nt = s // ts
    hshard = HEAD_DIM // N_DEV

    def body(x_ref, xh_ref, dx_ref, y_ref, vec_ref, pw_ref, ps_ref, dxo_ref, dpw_ref, sm_ref, acc_ref, q16_ref):
        i = pl.program_id(0)
        tile = nt - 1 - i

        @pl.when(i == 0)
        def _():
            acc_ref[...] = jnp.zeros_like(acc_ref)
            sm_ref[...] = jnp.zeros_like(sm_ref)
            q16_ref[...] = jnp.zeros_like(q16_ref)

        gs, sh = vec_ref[R_GS_M:R_GS_M + 1, :], vec_ref[R_SH_M:R_SH_M + 1, :]
        xv = x_ref[...]
        dxv = dx_ref[...]
        n, r = _rms(xv)
        h = n * gs + sh
        nh, _ = _rms(xh_ref[...])
        halo = jnp.where(tile == 0, 0.0, nh * gs + sh)
        inv = _pool_counts(tile, ts)
        pooled = _pooled(h, halo, inv)
        mixed = jnp.concatenate([_mm(pooled[g].astype(bf16), pw_ref[g]) for g in range(HEADS)], axis=1)

        dy = dxv * vec_ref[R_GT_M:R_GT_M + 1, :]
        sm_ref[G_GT:G_GT + 1, :] += _colsum(dxv * y_ref[...].astype(f32))
        sm_ref[3:4, :] += _colsum(dy * mixed)
        dmixed = (dy * ps_ref[0:1, :]).astype(bf16)
        dh_parts = []
        for g in range(HEADS):
            cols = slice(g * HEAD_DIM, (g + 1) * HEAD_DIM)
            acc_ref[g] += _mm_tn(pooled[g].astype(bf16), dmixed[:, cols])
            dpooled = _mm_nt(dmixed[:, cols], pw_ref[g])
            q = dpooled * inv[g]
            ext = jnp.concatenate([q, q16_ref[:, cols]], axis=0)
            q16_ref[:, cols] = q[0:POOL_HALO, :]
            for step in range(g + 1):
                ext = ext + pltpu.roll(ext, ext.shape[0] - (1 << step), 0)
            dh_parts.append(ext[:ts] - dpooled)
        dh = jnp.concatenate(dh_parts, axis=1)
        sm_ref[G_SH:G_SH + 1, :] += _colsum(dh)
        sm_ref[G_GS:G_GS + 1, :] += _colsum(dh * n)
        dxo_ref[...] = dxv + _norm_bwd(dh, n, r, gs)

        @pl.when(i == nt - 1)
        def _():
            for g in range(HEADS):
                dpw_ref[:, g] = acc_ref[g].astype(bf16).reshape(N_DEV, hshard, HEAD_DIM)

    rev = lambda i: (nt - 1 - i, 0)
    row = pl.BlockSpec((ts, D_MODEL), rev)
    halo16 = pl.BlockSpec((POOL_HALO, D_MODEL), lambda i: (jnp.maximum((nt - 1 - i) * (ts // POOL_HALO) - 1, 0), 0))
    const = lambda *shape: pl.BlockSpec(shape, lambda i: (0,) * len(shape))
    return pl.pallas_call(
        body, name=f"pool_bwd_{layer}", grid=(nt,),
        out_shape=(jax.ShapeDtypeStruct((s, D_MODEL), f32),
                   jax.ShapeDtypeStruct((N_DEV, HEADS, hshard, HEAD_DIM), bf16),
                   jax.ShapeDtypeStruct((8, D_MODEL), f32)),
        in_specs=[row, halo16, row, row, const(8, D_MODEL), const(HEADS, HEAD_DIM, HEAD_DIM), const(8, D_MODEL)],
        out_specs=(row, const(N_DEV, HEADS, hshard, HEAD_DIM), const(8, D_MODEL)),
        scratch_shapes=[pltpu.VMEM((HEADS, HEAD_DIM, HEAD_DIM), f32), pltpu.VMEM((POOL_HALO, D_MODEL), f32)],
        compiler_params=_params(("arbitrary",)),
    )(x, x, dx, y, vec, pw, ps)


def _final(x, target, g_fin):
    s = x.shape[0]
    ts = min(512, s)

    def body(x_ref, t_ref, g_ref, dx_ref, loss_ref, sm_ref):
        @pl.when(pl.program_id(0) == 0)
        def _():
            loss_ref[...] = jnp.zeros_like(loss_ref)
            sm_ref[...] = jnp.zeros_like(sm_ref)

        g = g_ref[0:1, :]
        n, r = _rms(x_ref[...])
        err = n * g - t_ref[...]
        loss_ref[...] += 0.5 * jnp.sum(jnp.mean(err * err, axis=-1, keepdims=True), axis=0, keepdims=True)
        dyv = err * (1.0 / D_MODEL)
        sm_ref[0:1, :] += _colsum(dyv * n)
        dx_ref[...] = _norm_bwd(dyv, n, r, g)

    row = pl.BlockSpec((ts, D_MODEL), lambda i: (i, 0))
    return pl.pallas_call(
        body, name="final_loss", grid=(s // ts,),
        out_shape=(jax.ShapeDtypeStruct((s, D_MODEL), f32), jax.ShapeDtypeStruct((8, 128), f32),
                   jax.ShapeDtypeStruct((8, D_MODEL), f32)),
        in_specs=[row, row, pl.BlockSpec((8, D_MODEL), lambda i: (0, 0))],
        out_specs=(row, pl.BlockSpec((8, 128), lambda i: (0, 0)), pl.BlockSpec((8, D_MODEL), lambda i: (0, 0))),
        compiler_params=_params(("arbitrary",)),
    )(x, target, g_fin)


def _small_pack(sm_ffn, sm_mix, sm_fin, table, g_mix, g_ffn, lam):
    def body(*refs):
        ffn, mix = refs[0:DEPTH], refs[DEPTH:2 * DEPTH]
        fin_ref, tab_ref, gm_ref, gf_ref, lam_ref, o_ref = refs[2 * DEPTH:]
        o_ref[...] = jnp.zeros_like(o_ref)
        for i in range(DEPTH):
            base = K_MOD + i * N_MOD
            o_ref[base + 0:base + 1, :] = mix[i][G_SH:G_SH + 1, :]
            o_ref[base + 1:base + 2, :] = mix[i][G_GS:G_GS + 1, :] * gm_ref[i:i + 1, :]
            o_ref[base + 2:base + 3, :] = mix[i][G_GT:G_GT + 1, :]
            o_ref[base + 3:base + 4, :] = ffn[i][G_SH:G_SH + 1, :]
            o_ref[base + 4:base + 5, :] = ffn[i][G_GS:G_GS + 1, :] * gf_ref[i:i + 1, :]
            o_ref[base + 5:base + 6, :] = ffn[i][G_GT:G_GT + 1, :]
            o_ref[K_NMIX + i:K_NMIX + i + 1, :] = mix[i][G_GS:G_GS + 1, :] * (1.0 + tab_ref[i, R_SC_M:R_SC_M + 1, :])
            o_ref[K_NFFN + i:K_NFFN + i + 1, :] = ffn[i][G_GS:G_GS + 1, :] * (1.0 + tab_ref[i, R_SC_F:R_SC_F + 1, :])
            j = i // 2
            if i % 2 == 0:
                for k, src in enumerate((G_BY, G_BIN, G_CONVB, None, G_BOUT)):
                    dst = K_LRUB + j * 5 + k
                    if src is None:
                        o_ref[dst:dst + 1, :] = mix[i][G_LS:G_LS + 1, :] * _sigmoid(-lam_ref[j:j + 1, :])
                    else:
                        o_ref[dst:dst + 1, :] = mix[i][src:src + 1, :]
                o_ref[K_CONVW + j * 4:K_CONVW + j * 4 + 4, :] = mix[i][G_CW0:G_CW0 + 4, :]
                o_ref[K_BA + j:K_BA + j + 1, :] = mix[i][G_BA:G_BA + 1, :]
                o_ref[K_BX + j:K_BX + j + 1, :] = mix[i][G_BX:G_BX + 1, :]
            else:
                o_ref[K_PS + j:K_PS + j + 1, :] = mix[i][3:4, :]
        o_ref[K_FIN:K_FIN + 1, :] = fin_ref[0:1, :]

    return pl.pallas_call(body, name="small_pack", out_shape=jax.ShapeDtypeStruct((K_ROWS, D_MODEL), f32))(
        *sm_ffn, *sm_mix, sm_fin, table, g_mix, g_ffn, lam)


def _small_sum(gathered):
    def body(g_ref, o_ref, token_ref):
        tot = g_ref[0]
        for src in range(1, N_DEV):
            tot = tot + g_ref[src]
        o_ref[...] = tot
        token_ref[...] = jnp.zeros_like(token_ref)

    return pl.pallas_call(
        body, name="small_sum",
        out_shape=(jax.ShapeDtypeStruct(gathered.shape[1:], f32), jax.ShapeDtypeStruct((8, 128), f32)))(gathered)


def _adamw_math(g, w, m, v):
    m = ADAM_B1 * m + (1.0 - ADAM_B1) * g
    v = ADAM_B2 * v + (1.0 - ADAM_B2) * (g * g)
    m_hat = m / (1.0 - ADAM_B1 ** ADAM_STEP)
    v_hat = v / (1.0 - ADAM_B2 ** ADAM_STEP)
    delta = -ADAM_LR * (m_hat / (jnp.sqrt(v_hat) + ADAM_EPS) + ADAM_WD * w)
    return delta, m, v


def _adamw_small(name, g, w, m, v):
    shape = w.shape
    two_d = (1, shape[0]) if len(shape) == 1 else (math.prod(shape[:-1]), shape[-1])

    def body(g_ref, w_ref, m_ref, v_ref, d_ref, mo_ref, vo_ref):
        d_ref[...], mo_ref[...], vo_ref[...] = _adamw_math(g_ref[...], w_ref[...], m_ref[...], v_ref[...])

    outs = pl.pallas_call(body, name=f"adamw_{name}", out_shape=tuple(jax.ShapeDtypeStruct(two_d, f32) for _ in range(3)))(
        *(t.reshape(two_d) for t in (g, w, m, v)))
    return tuple(t.reshape(shape) for t in outs)


def _block_rows(rows, cols):
    tr = max(SUBLANES, min(rows, (512 * 1024) // (4 * cols)))
    while rows % tr:
        tr //= 2
    return tr


def _adamw_reduce(name, landings, kind, w, m, v):
    nl = len(landings)
    rows, cols = landings[0].shape[2:]
    tr = _block_rows(rows, cols)
    per_layer = rows // tr

    def body(*refs):
        l_refs = refs[:nl]
        w_ref, m_ref, v_ref, g_ref, d_ref, mo_ref, vo_ref = refs[nl:]
        layer = pl.program_id(0)
        for k in range(nl):
            @pl.when(layer == k)
            def _(k=k):
                g = l_refs[k][0].astype(f32)
                for src in range(1, N_DEV):
                    g = g + l_refs[k][src].astype(f32)
                g_ref[...] = g
        d_ref[...], mo_ref[...], vo_ref[...] = _adamw_math(g_ref[...], w_ref[...], m_ref[...], v_ref[...])

    blk = pl.BlockSpec((tr, cols), lambda l, r: (l * per_layer + r, 0))
    land = [pl.BlockSpec((N_DEV, None, tr, cols), lambda l, r, k=k: (0, kind, jnp.where(l == k, r, 0), 0)) for k in range(nl)]
    return pl.pallas_call(
        body, name=f"adamw_{name}", grid=(nl, per_layer),
        out_shape=tuple(jax.ShapeDtypeStruct((nl * rows, cols), f32) for _ in range(4)),
        in_specs=land + [blk, blk, blk],
        out_specs=(blk, blk, blk, blk),
        compiler_params=_params(("arbitrary", "arbitrary"), 32),
    )(*landings, w, m, v)


def _adamw_w_mod(c_all, dmod_all, w, m, v):
    depth, d, cols = w.shape
    tr = 256

    def body(c_ref, dm_ref, w_ref, m_ref, v_ref, g_ref, d_ref, mo_ref, vo_ref):
        cv = c_ref[...]
        cond = cv * _sigmoid(cv)
        g = lax.dot_general(cond, dm_ref[...], (((0,), (0,)), ((), ())), preferred_element_type=f32,
                            precision=lax.Precision.HIGHEST)
        g_ref[...] = g
        d_ref[...], mo_ref[...], vo_ref[...] = _adamw_math(g, w_ref[...], m_ref[...], v_ref[...])

    blk = pl.BlockSpec((None, tr, cols), lambda i, r: (i, r, 0))
    return pl.pallas_call(
        body, name="adamw_w_mod", grid=(depth, d // tr),
        out_shape=tuple(jax.ShapeDtypeStruct(w.shape, f32) for _ in range(4)),
        in_specs=[pl.BlockSpec((N_DEV, tr), lambda i, r: (0, r)),
                  pl.BlockSpec((None, N_DEV, cols), lambda i, r: (i, 0, 0)), blk, blk, blk],
        out_specs=(blk, blk, blk, blk),
        compiler_params=_params(("arbitrary", "arbitrary"), 32),
    )(c_all, dmod_all, w, m, v)


def kernel(x, c, w_mod, b_mod, norm_mix_g, norm_ffn_g, lru_w_y, lru_b_y, lru_w_in, lru_b_in, lru_conv_w, lru_conv_b, lru_w_a, lru_b_a, lru_w_x, lru_b_x, lru_lambda, lru_w_out, lru_b_out, pool_w, pool_scale, ffn_w1, ffn_w2, final_norm_g, loss_target, m_w_mod, m_b_mod, m_norm_mix_g, m_norm_ffn_g, m_lru_w_y, m_lru_b_y, m_lru_w_in, m_lru_b_in, m_lru_conv_w, m_lru_conv_b, m_lru_w_a, m_lru_b_a, m_lru_w_x, m_lru_b_x, m_lru_lambda, m_lru_w_out, m_lru_b_out, m_pool_w, m_pool_scale, m_ffn_w1, m_ffn_w2, m_final_norm_g, v_w_mod, v_b_mod, v_norm_mix_g, v_norm_ffn_g, v_lru_w_y, v_lru_b_y, v_lru_w_in, v_lru_b_in, v_lru_conv_w, v_lru_conv_b, v_lru_w_a, v_lru_b_a, v_lru_w_x, v_lru_b_x, v_lru_lambda, v_lru_w_out, v_lru_b_out, v_pool_w, v_pool_scale, v_ffn_w1, v_ffn_w2, v_final_norm_g):
    me = 4 * lax.axis_index("x") + 2 * lax.axis_index("y") + lax.axis_index("c")
    n_lru = lru_w_y.shape[0]
    shard = LRU_WIDTH // N_DEV
    hshard = HEAD_DIM // N_DEV
    xs = x[0]
    target = loss_target[0]

    small_vecs = jnp.concatenate([
        lru_conv_w.reshape(n_lru * 4, shard), lru_b_a.reshape(n_lru, HEADS * hshard),
        lru_b_x.reshape(n_lru, HEADS * hshard), pool_scale, jnp.zeros((2, shard), f32)], axis=0)
    first_mix, token = _send_start("gather_mix_start_0", [jnp.stack([lru_w_y[0], lru_w_in[0], lru_w_out[0]]).astype(bf16),
                                                          jnp.stack([lru_w_a[0], lru_w_x[0]]).astype(bf16)], True, me)
    sv_g, c_g = _exchange([small_vecs + token[0, 0], c], True, "gather_cond")
    conv_w_full = sv_g[:, 0:8].reshape(N_DEV, n_lru, 4, shard).transpose(1, 2, 0, 3).reshape(n_lru, 4, LRU_WIDTH)
    b_a_full = sv_g[:, 8:10].reshape(N_DEV, n_lru, HEADS, hshard).transpose(1, 2, 0, 3).reshape(n_lru, LRU_WIDTH)
    b_x_full = sv_g[:, 10:12].reshape(N_DEV, n_lru, HEADS, hshard).transpose(1, 2, 0, 3).reshape(n_lru, LRU_WIDTH)
    ps_full = sv_g[:, 12:14].transpose(1, 0, 2).reshape(n_lru, D_MODEL)
    c_all = c_g.reshape(N_DEV, D_MODEL)

    (mod_g,) = _exchange([_mod_part(c_all, w_mod)], True, "gather_mod")
    mod_row = lax.dynamic_index_in_dim(mod_g, me, axis=2, keepdims=False)
    mod_row = mod_row.transpose(1, 0, 2).reshape(DEPTH, N_MOD * D_MODEL)
    table, token = _mod_table(mod_row, b_mod, norm_mix_g, norm_ffn_g)

    h_mix, h_ffn = [first_mix], []
    for i in range(DEPTH):
        j = i // 2
        if i > 0:
            if i % 2 == 0:
                mix_src = [(jnp.stack([lru_w_y[j], lru_w_in[j], lru_w_out[j]]) + token[0, 0]).astype(bf16),
                           (jnp.stack([lru_w_a[j], lru_w_x[j]]) + token[0, 0]).astype(bf16)]
            else:
                mix_src = [(pool_w[j] + token[0, 0]).astype(bf16)]
            handle, token = _send_start(f"gather_mix_start_{i}", mix_src, True, me)
            h_mix.append(handle)
        handle, token = _send_start(f"gather_ffn_start_{i}", [(ffn_w1[i] + token[0, 0]).astype(bf16),
                                                               (ffn_w2[i] + token[0, 0]).astype(bf16)], True, me)
        h_ffn.append(handle)

    zero_row = jnp.zeros((1, LRU_WIDTH), f32)
    pvecs = [jnp.concatenate([lru_b_y[j:j + 1], lru_b_in[j:j + 1], lru_conv_b[j:j + 1], b_a_full[j:j + 1],
                              b_x_full[j:j + 1], lru_lambda[j:j + 1], lru_b_out[j:j + 1], zero_row,
                              conv_w_full[j], zero_row, zero_row, zero_row, zero_row], axis=0) for j in range(n_lru)]
    ps_rows = [jnp.concatenate([ps_full[j:j + 1], jnp.zeros((7, D_MODEL), f32)], axis=0) for j in range(n_lru)]

    saved = []
    ffn_w, mix_w = [], []
    h = xs
    for i in range(DEPTH):
        j = i // 2
        got, _ = _send_wait(f"gather_mix_wait_{i}", h_mix[i], h)
        if i % 2 == 0:
            mix_w.append((got[0].transpose(1, 0, 2, 3).reshape(3, LRU_WIDTH, LRU_WIDTH),
                          got[1].transpose(1, 2, 0, 3, 4).reshape(2, HEADS, HEAD_DIM, HEAD_DIM)))
            h_mid, *lru_saved = _lru_fwd(h, table[i], mix_w[i][0], mix_w[i][1], pvecs[j], i)
            mix_saved = (h, tuple(lru_saved))
        else:
            mix_w.append((got[0].transpose(1, 0, 2, 3).reshape(HEADS, HEAD_DIM, HEAD_DIM),))
            h_mid, y_mix = _pool_fwd(h, table[i], mix_w[i][0], ps_rows[j], i)
            mix_saved = (h, y_mix)
        ffn_w.append(_send_wait(f"gather_ffn_wait_{i}", h_ffn[i], h_mid)[0])
        h_out, u, y_ffn, hb = _ffn_fwd(h_mid, table[i], ffn_w[i][0], ffn_w[i][1], i)
        saved.append((mix_saved, (h_mid, u, y_ffn, hb)))
        h = h_out
    fin_rows = jnp.concatenate([final_norm_g[None, :], jnp.zeros((7, D_MODEL), f32)], axis=0)
    dx, loss_part, sm_fin = _final(h, target, fin_rows)
    loss = lax.psum(loss_part[0, 0], ("x", "y", "c"))

    sm_ffn, sm_mix = [None] * DEPTH, [None] * DEPTH
    x_ffn, x_mix = [None] * DEPTH, [None] * DEPTH
    token = jnp.zeros((8, 128), f32)
    for i in reversed(range(DEPTH)):
        j = i // 2
        mix_saved, (h_mid, u, y_ffn, hb) = saved[i]
        dx, da, dyb, sm_ffn[i] = _ffn_bwd_act(h_mid, dx, u, y_ffn, table[i] + token[0, 0], ffn_w[i][0], ffn_w[i][1], i)
        x_ffn[i], token = _send_start(f"grads_ffn_start_{i}", [_ffn_bwd_w1(hb, da, i), _ffn_bwd_w2(u, dyb, i)], False, me)
        if i % 2 == 0:
            h_in, lru_saved = mix_saved
            dx, dbig, dsmall, sm_mix[i] = _lru_bwd(
                h_in, dx, lru_saved, table[i] + token[0, 0], mix_w[i][0], mix_w[i][1], pvecs[j], i)
            last_mix = [dbig, dsmall]
        else:
            h_in, y_mix = mix_saved
            dx, dpool, sm = _pool_bwd(h_in, dx, y_mix, table[i] + token[0, 0], mix_w[i][0], ps_rows[j], i)
            sm_mix[i] = jnp.concatenate([sm, jnp.zeros((8, D_MODEL), f32)], axis=0)
            last_mix = [dpool]
        if i > 0:
            x_mix[i], token = _send_start(f"grads_mix_start_{i}", last_mix, False, me)
    grad_x = dx[None]

    pack = _small_pack(sm_ffn, sm_mix, sm_fin, table + token[0, 0], norm_mix_g, norm_ffn_g, lru_lambda)
    (pack_g,) = _exchange([pack], True, "gather_small_grads")
    tot, token = _small_sum(pack_g)
    x_mix[0], _ = _send_start("grads_mix_start_0", [t + token[0, 0].astype(bf16) for t in last_mix], False, me)
    cols = w_mod.shape[2]
    dmod_all = lax.dynamic_slice_in_dim(pack_g[:, K_MOD:K_MOD + DEPTH * N_MOD].reshape(N_DEV, DEPTH, N_MOD * D_MODEL),
                                        me * cols, cols, axis=2).transpose(1, 0, 2)
    results = {"w_mod": _adamw_w_mod(c_all, dmod_all, w_mod, m_w_mod, v_w_mod)}

    after = results["w_mod"][1]
    l_ffn = [_send_wait(f"grads_ffn_wait_{i}", x_ffn[i], after)[0] for i in reversed(range(DEPTH))][::-1]

    def reduce_update(name, landings, kind, w, m, v):
        rows = w.size // w.shape[-1]
        two_d = (rows, w.shape[-1])
        lands = [t.reshape(N_DEV, -1, rows // len(landings), w.shape[-1]) for t in landings]
        outs = _adamw_reduce(name, lands, kind, w.reshape(two_d), m.reshape(two_d), v.reshape(two_d))
        return tuple(t.reshape(w.shape) for t in outs)

    results["ffn_w1"] = reduce_update("ffn_w1", [t[0] for t in l_ffn], 0, ffn_w1, m_ffn_w1, v_ffn_w1)
    results["ffn_w2"] = reduce_update("ffn_w2", [t[1] for t in l_ffn], 0, ffn_w2, m_ffn_w2, v_ffn_w2)
    after = results["ffn_w2"][1]
    l_mix = [_send_wait(f"grads_mix_wait_{i}", x_mix[i], after)[0] for i in reversed(range(DEPTH))][::-1]
    l_lru_big = [l_mix[i][0] for i in range(0, DEPTH, 2)]
    l_lru_small = [l_mix[i][1] for i in range(0, DEPTH, 2)]
    l_pool = [l_mix[i][0] for i in range(1, DEPTH, 2)]
    results["lru_w_y"] = reduce_update("lru_w_y", l_lru_big, 0, lru_w_y, m_lru_w_y, v_lru_w_y)
    results["lru_w_in"] = reduce_update("lru_w_in", l_lru_big, 1, lru_w_in, m_lru_w_in, v_lru_w_in)
    results["lru_w_out"] = reduce_update("lru_w_out", l_lru_big, 2, lru_w_out, m_lru_w_out, v_lru_w_out)
    results["lru_w_a"] = reduce_update("lru_w_a", l_lru_small, 0, lru_w_a, m_lru_w_a, v_lru_w_a)
    results["lru_w_x"] = reduce_update("lru_w_x", l_lru_small, 1, lru_w_x, m_lru_w_x, v_lru_w_x)
    results["pool_w"] = reduce_update("pool_w", l_pool, 0, pool_w, m_pool_w, v_pool_w)

    def my_cols(full, width):
        return lax.dynamic_slice_in_dim(full, me * width, width, axis=full.ndim - 1)

    lru_rows = tot[K_LRUB:K_LRUB + 5 * n_lru].reshape(n_lru, 5, LRU_WIDTH)
    small_grads = {
        "b_mod": tot[K_MOD:K_MOD + DEPTH * N_MOD].reshape(DEPTH, N_MOD * D_MODEL),
        "norm_mix_g": tot[K_NMIX:K_NMIX + DEPTH],
        "norm_ffn_g": tot[K_NFFN:K_NFFN + DEPTH],
        "lru_b_y": lru_rows[:, 0], "lru_b_in": lru_rows[:, 1], "lru_conv_b": lru_rows[:, 2],
        "lru_lambda": lru_rows[:, 3], "lru_b_out": lru_rows[:, 4],
        "lru_conv_w": my_cols(tot[K_CONVW:K_CONVW + 4 * n_lru].reshape(n_lru, 4, LRU_WIDTH), shard),
        "lru_b_a": my_cols(tot[K_BA:K_BA + n_lru].reshape(n_lru, HEADS, HEAD_DIM), hshard),
        "lru_b_x": my_cols(tot[K_BX:K_BX + n_lru].reshape(n_lru, HEADS, HEAD_DIM), hshard),
        "pool_scale": my_cols(tot[K_PS:K_PS + n_lru], shard),
        "final_norm_g": tot[K_FIN],
    }
    given = dict(b_mod=(b_mod, m_b_mod, v_b_mod), norm_mix_g=(norm_mix_g, m_norm_mix_g, v_norm_mix_g),
                 norm_ffn_g=(norm_ffn_g, m_norm_ffn_g, v_norm_ffn_g), lru_b_y=(lru_b_y, m_lru_b_y, v_lru_b_y),
                 lru_b_in=(lru_b_in, m_lru_b_in, v_lru_b_in), lru_conv_w=(lru_conv_w, m_lru_conv_w, v_lru_conv_w),
                 lru_conv_b=(lru_conv_b, m_lru_conv_b, v_lru_conv_b), lru_b_a=(lru_b_a, m_lru_b_a, v_lru_b_a),
                 lru_b_x=(lru_b_x, m_lru_b_x, v_lru_b_x), lru_lambda=(lru_lambda, m_lru_lambda, v_lru_lambda),
                 lru_b_out=(lru_b_out, m_lru_b_out, v_lru_b_out), pool_scale=(pool_scale, m_pool_scale, v_pool_scale),
                 final_norm_g=(final_norm_g, m_final_norm_g, v_final_norm_g))
    for name, g in small_grads.items():
        results[name] = (g,) + _adamw_small(name, g, *given[name])

    order = ["w_mod", "b_mod", "norm_mix_g", "norm_ffn_g", "lru_w_y", "lru_b_y", "lru_w_in", "lru_b_in", "lru_conv_w",
             "lru_conv_b", "lru_w_a", "lru_b_a", "lru_w_x", "lru_b_x", "lru_lambda", "lru_w_out", "lru_b_out", "pool_w",
             "pool_scale", "ffn_w1", "ffn_w2", "final_norm_g"]
    return (loss, grad_x, *[results[n][0] for n in order], *[results[n][1] for n in order],
            *[results[n][2] for n in order], *[results[n][3] for n in order])
```

```python
import functools
import math

import jax
import jax.numpy as jnp
from jax import lax
from jax.experimental import pallas as pl
from jax.experimental.pallas import tpu as pltpu

f32, bf16 = jnp.float32, jnp.bfloat16

D_MODEL = 1024
LRU_WIDTH = 1024
HEADS = 4
HEAD_DIM = 256
D_FF = 4096
DEPTH = 4
N_MOD = 6
N_DEV = 8
FF_CHUNK = D_FF // N_DEV
POOL_WINDOWS = (2, 4, 8, 16)
POOL_HALO = 16
EPS = 1e-6
LRU_C = 8.0

ADAM_LR = 0.001
ADAM_B1 = 0.9
ADAM_B2 = 0.999
ADAM_EPS = 1e-08
ADAM_WD = 0.01
ADAM_STEP = 10

V7X_VMEM_BYTES = 64 * 1024 * 1024
SUBLANES = 8
BF16_ROWS = 16

R_SH_M, R_SC_M, R_GT_M, R_SH_F, R_SC_F, R_GT_F, R_GS_M, R_GS_F = range(8)
P_BY, P_BIN, P_CONVB, P_BA, P_BX, P_LAM, P_BOUT, P_CW0 = 0, 1, 2, 3, 4, 5, 6, 8
G_SH, G_GS, G_GT, G_BY, G_BIN, G_CONVB, G_BA, G_BX, G_LS, G_BOUT, G_CW0 = 0, 1, 2, 3, 4, 5, 6, 7, 8, 9, 10
K_MOD, K_NMIX, K_NFFN, K_LRUB, K_CONVW, K_BA, K_BX, K_PS, K_FIN, K_ROWS = 0, 24, 28, 32, 42, 50, 52, 54, 56, 64


def _params(semantics=None, vmem_mb=48):
    return pltpu.CompilerParams(dimension_semantics=semantics, vmem_limit_bytes=vmem_mb * 1024 * 1024)


def _mm(a, b):
    return jnp.dot(a, b, preferred_element_type=f32)


def _mm_nt(a, b):
    return lax.dot_general(a, b, (((1,), (1,)), ((), ())), preferred_element_type=f32)


def _mm_tn(a, b):
    return lax.dot_general(a, b, (((0,), (0,)), ((), ())), preferred_element_type=f32)


def _rms(x):
    r = lax.rsqrt(jnp.mean(x * x, axis=-1, keepdims=True) + EPS)
    return x * r, r


def _norm_bwd(dh, n, r, gs):
    dn = dh * gs
    return r * (dn - n * jnp.mean(dn * n, axis=-1, keepdims=True))


def _colsum(v):
    return jnp.sum(v, axis=0, keepdims=True)


def _sigmoid(v):
    return 0.5 * jnp.tanh(0.5 * v) + 0.5


def _log_sigmoid(v):
    return jnp.minimum(v, 0.0) - jnp.log1p(jnp.exp(-jnp.abs(v)))


_GELU_C = 0.7978845608028654
_GELU_A = 0.044715


def _gelu_and_grad(v):
    v2 = v * v
    t = jnp.tanh(_GELU_C * v * (1.0 + _GELU_A * v2))
    p = 0.5 + 0.5 * t
    return v * p, p + (0.5 * v) * (1.0 - t * t) * (_GELU_C + (3.0 * _GELU_A * _GELU_C) * v2)


def _rows_before(halo, v, shifts):
    hr = halo.shape[0]
    ext = jnp.concatenate([halo, v], axis=0)
    return [pltpu.roll(ext, k, 0)[hr:] for k in shifts]


def _rows_after(v, halo, shifts):
    n = v.shape[0]
    ext = jnp.concatenate([v, halo], axis=0)
    return [pltpu.roll(ext, ext.shape[0] - k, 0)[:n] for k in shifts]


def _shift_matrix(n, halo_rows, shifts):
    rows = lax.broadcasted_iota(jnp.int32, (n, n + halo_rows), 0)
    cols = lax.broadcasted_iota(jnp.int32, (n, n + halo_rows), 1)
    return jnp.concatenate([(cols == rows + halo_rows - k).astype(bf16) for k in shifts], axis=0)


def _shifted_rows(sel, halo, v):
    n = v.shape[0]
    out = _mm(sel, jnp.concatenate([halo, v], axis=0))
    return [out[j * n:(j + 1) * n] for j in range(sel.shape[0] // n)]


def _block_diag(v, w_ref, kind):
    return jnp.concatenate(
        [_mm(v[:, h * HEAD_DIM:(h + 1) * HEAD_DIM], w_ref[kind, h]) for h in range(HEADS)], axis=1)


def _block_diag_t(v, w_ref, kind):
    return jnp.concatenate(
        [_mm_nt(v[:, h * HEAD_DIM:(h + 1) * HEAD_DIM], w_ref[kind, h]) for h in range(HEADS)], axis=1)


def _exchange(arrays, gather, name):
    n = len(arrays)
    peers = N_DEV - 1

    def body(*refs):
        ins, outs = refs[:n], refs[n:2 * n]
        send_sems, recv_sems, local_sems = refs[2 * n:]
        x, y, c = lax.axis_index("x"), lax.axis_index("y"), lax.axis_index("c")
        me = 4 * x + 2 * y + c
        local = []
        for k in range(n):
            cp = pltpu.make_async_copy(ins[k] if gather else ins[k].at[me], outs[k].at[me], local_sems.at[k])
            cp.start()
            local.append(cp)
        remote = []
        for p in range(1, N_DEV):
            px = 1 - x if p & 4 else x
            py = 1 - y if p & 2 else y
            pc = 1 - c if p & 1 else c
            for k in range(n):
                cp = pltpu.make_async_remote_copy(
                    src_ref=ins[k] if gather else ins[k].at[4 * px + 2 * py + pc],
                    dst_ref=outs[k].at[me],
                    send_sem=send_sems.at[k * peers + p - 1],
                    recv_sem=recv_sems.at[k * peers + p - 1],
                    device_id=(px, py, pc), device_id_type=pl.DeviceIdType.MESH)
                cp.start()
                remote.append(cp)
        for cp in remote:
            cp.wait()
        for cp in local:
            cp.wait()

    out_shape = tuple(
        jax.ShapeDtypeStruct(((N_DEV,) + a.shape) if gather else a.shape, a.dtype) for a in arrays)
    outs = pl.pallas_call(
        body, name=name, out_shape=out_shape,
        in_specs=[pl.BlockSpec(memory_space=pl.ANY)] * n,
        out_specs=tuple(pl.BlockSpec(memory_space=pl.ANY) for _ in range(n)),
        scratch_shapes=[pltpu.SemaphoreType.DMA((n * peers,)), pltpu.SemaphoreType.DMA((n * peers,)),
                        pltpu.SemaphoreType.DMA((n,))],
        compiler_params=pltpu.CompilerParams(has_side_effects=True),
    )(*arrays)
    return list(outs)


_HBM = pl.BlockSpec(memory_space=pltpu.HBM)
_SEM = pl.BlockSpec(memory_space=pltpu.SEMAPHORE)
_DATAFLOW = pltpu.SideEffectType.DATAFLOW_SIDE_EFFECTING


def _peer_copies(src_refs, land_refs, send_sems, recv_sems, gather):
    x, y, c = lax.axis_index("x"), lax.axis_index("y"), lax.axis_index("c")
    me = 4 * x + 2 * y + c
    peers = N_DEV - 1
    copies = []
    for p in range(1, N_DEV):
        px = 1 - x if p & 4 else x
        py = 1 - y if p & 2 else y
        pc = 1 - c if p & 1 else c
        for k in range(len(src_refs)):
            copies.append(pltpu.make_async_remote_copy(
                src_ref=src_refs[k] if gather else src_refs[k].at[4 * px + 2 * py + pc],
                dst_ref=land_refs[k].at[me],
                send_sem=send_sems.at[k * peers + p - 1], recv_sem=recv_sems.at[k * peers + p - 1],
                device_id=(px, py, pc), device_id_type=pl.DeviceIdType.MESH))
    return copies


def _landing(srcs, gather, me):
    out = []
    for a in srcs:
        own = a if gather else lax.dynamic_index_in_dim(a, me, 0, keepdims=False)
        out.append(lax.dynamic_update_index_in_dim(lax.empty((N_DEV,) + own.shape, own.dtype), own, me, 0))
    return out


def _send_start(name, srcs, gather, me):
    n = len(srcs)
    lands = _landing(srcs, gather, me)

    def body(*refs):
        src_refs, land_refs = refs[:n], refs[n:2 * n]
        send_sems, recv_sems, token = refs[2 * n], refs[2 * n + 1], refs[-1]
        for cp in _peer_copies(src_refs, land_refs, send_sems, recv_sems, gather):
            cp.start()
        token[...] = jnp.zeros_like(token)

    sems = pltpu.SemaphoreType.DMA((n * (N_DEV - 1),))
    outs = pl.pallas_call(
        body, name=name,
        out_shape=(sems, sems, *[pltpu.HBM(a.shape, a.dtype) for a in (*srcs, *lands)], jax.ShapeDtypeStruct((8, 128), f32)),
        in_specs=[_HBM] * (2 * n),
        out_specs=(_SEM, _SEM, *[_HBM] * (2 * n), pl.BlockSpec(memory_space=pltpu.VMEM)),
        input_output_aliases={k: 2 + k for k in range(2 * n)},
        compiler_params=pltpu.CompilerParams(has_side_effects=_DATAFLOW),
    )(*[pltpu.with_memory_space_constraint(a, pltpu.HBM) for a in (*srcs, *lands)])
    return (outs[0], outs[1], list(outs[2:2 + n]), list(outs[2 + n:2 + 2 * n]), gather), outs[-1]


def _send_wait(name, handle, after):
    send_sems, recv_sems, srcs, lands, gather = handle
    n = len(srcs)

    def body(*refs):
        src_refs, land_refs = refs[:n], refs[n:2 * n]
        for cp in _peer_copies(src_refs, land_refs, refs[2 * n], refs[2 * n + 1], gather):
            cp.wait_send()
            cp.wait_recv()
        refs[-1][...] = jnp.zeros_like(refs[-1])

    outs = pl.pallas_call(
        body, name=name,
        out_shape=(*[pltpu.HBM(a.shape, a.dtype) for a in (*srcs, *lands)], jax.ShapeDtypeStruct((8, 128), f32)),
        in_specs=[_HBM] * (2 * n) + [_SEM, _SEM, pl.BlockSpec(memory_space=pl.ANY)],
        out_specs=(*[_HBM] * (2 * n), pl.BlockSpec(memory_space=pltpu.VMEM)),
        input_output_aliases={k: k for k in range(2 * n)},
        compiler_params=pltpu.CompilerParams(has_side_effects=_DATAFLOW),
    )(*srcs, *lands, send_sems, recv_sems, after)
    return list(outs[n:2 * n]), outs[-1]


def _mod_part(c_all, w_mod):
    depth, d, cols = w_mod.shape

    def body(c_ref, w_ref, o_ref):
        cv = c_ref[...]
        cond = cv * _sigmoid(cv)
        o_ref[...] = jnp.dot(cond, w_ref[...], preferred_element_type=f32, precision=lax.Precision.HIGHEST)

    return pl.pallas_call(
        body, name="mod_part", grid=(depth,),
        out_shape=jax.ShapeDtypeStruct((depth, N_DEV, cols), f32),
        in_specs=[pl.BlockSpec((N_DEV, d), lambda i: (0, 0)), pl.BlockSpec((None, d, cols), lambda i: (i, 0, 0))],
        out_specs=pl.BlockSpec((None, N_DEV, cols), lambda i: (i, 0, 0)),
        compiler_params=_params(("arbitrary",), 32),
    )(c_all, w_mod)


def _mod_table(mod_row, b_mod, g_mix, g_ffn):
    def body(m_ref, b_ref, gm_ref, gf_ref, o_ref, token_ref):
        for i in range(DEPTH):
            for k in range(N_MOD):
                o_ref[i, k:k + 1, :] = m_ref[i:i + 1, k * D_MODEL:(k + 1) * D_MODEL] + b_ref[i:i + 1, k * D_MODEL:(k + 1) * D_MODEL]
            o_ref[i, R_GS_M:R_GS_M + 1, :] = gm_ref[i:i + 1, :] * (1.0 + o_ref[i, R_SC_M:R_SC_M + 1, :])
            o_ref[i, R_GS_F:R_GS_F + 1, :] = gf_ref[i:i + 1, :] * (1.0 + o_ref[i, R_SC_F:R_SC_F + 1, :])
        token_ref[...] = jnp.zeros_like(token_ref)

    return pl.pallas_call(
        body, name="mod_table",
        out_shape=(jax.ShapeDtypeStruct((DEPTH, 8, D_MODEL), f32), jax.ShapeDtypeStruct((8, 128), f32)))(
        mod_row, b_mod, g_mix, g_ffn)


def _ffn_tile(s):
    return min(512, s)


FFN_GRAD_TILE = 1024


def _layer_weights(shape):
    return pl.BlockSpec((N_DEV,) + shape, lambda i: (0, 0, 0))


def _ffn_fwd(x, vec, w1g, w2g, layer):
    s = x.shape[0]
    ts = _ffn_tile(s)

    def body(x_ref, vec_ref, w1_ref, w2_ref, xo_ref, u_ref, y_ref, hb_ref):
        xv = x_ref[...]
        n, _ = _rms(xv)
        hb = (n * vec_ref[R_GS_F:R_GS_F + 1, :] + vec_ref[R_SH_F:R_SH_F + 1, :]).astype(bf16)
        hb_ref[...] = hb
        yv = jnp.zeros((ts, D_MODEL), f32)
        for f in range(N_DEV):
            u = jnp.maximum(_mm(hb, w1_ref[f]), 0.0)
            u_ref[:, f * FF_CHUNK:(f + 1) * FF_CHUNK] = u.astype(bf16)
            yv = yv + _mm((u * u).astype(bf16), w2_ref[f])
        y_ref[...] = yv.astype(bf16)
        xo_ref[...] = xv + vec_ref[R_GT_F:R_GT_F + 1, :] * yv

    row = pl.BlockSpec((ts, D_MODEL), lambda i: (i, 0))
    return pl.pallas_call(
        body, name=f"ffn_fwd_{layer}", grid=(s // ts,),
        out_shape=(jax.ShapeDtypeStruct((s, D_MODEL), f32), jax.ShapeDtypeStruct((s, D_FF), bf16),
                   jax.ShapeDtypeStruct((s, D_MODEL), bf16), jax.ShapeDtypeStruct((s, D_MODEL), bf16)),
        in_specs=[row, pl.BlockSpec((8, D_MODEL), lambda i: (0, 0)),
                  _layer_weights((D_MODEL, FF_CHUNK)), _layer_weights((FF_CHUNK, D_MODEL))],
        out_specs=(row, pl.BlockSpec((ts, D_FF), lambda i: (i, 0)), row, row),
        compiler_params=_params(("arbitrary",), 56),
    )(x, vec, w1g, w2g)


def _ffn_bwd_act(x, dx, u, y, vec, w1g, w2g, layer):
    s = x.shape[0]
    ts = _ffn_tile(s)

    def body(x_ref, dx_ref, u_ref, y_ref, vec_ref, w1_ref, w2_ref, dxo_ref, da_ref, dyb_ref, sm_ref):
        @pl.when(pl.program_id(0) == 0)
        def _():
            sm_ref[...] = jnp.zeros_like(sm_ref)

        dxv = dx_ref[...]
        dyb = (dxv * vec_ref[R_GT_F:R_GT_F + 1, :]).astype(bf16)
        dyb_ref[...] = dyb
        sm_ref[G_GT:G_GT + 1, :] += _colsum(dxv * y_ref[...].astype(f32))
        dh = jnp.zeros((ts, D_MODEL), f32)
        for f in range(N_DEV):
            cols = slice(f * FF_CHUNK, (f + 1) * FF_CHUNK)
            dz = _mm_nt(dyb, w2_ref[f])
            dab = (dz * (2.0 * u_ref[:, cols].astype(f32))).astype(bf16)
            da_ref[:, cols] = dab
            dh = dh + _mm_nt(dab, w1_ref[f])
        n, r = _rms(x_ref[...])
        sm_ref[G_SH:G_SH + 1, :] += _colsum(dh)
        sm_ref[G_GS:G_GS + 1, :] += _colsum(dh * n)
        dxo_ref[...] = dxv + _norm_bwd(dh, n, r, vec_ref[R_GS_F:R_GS_F + 1, :])

    row = pl.BlockSpec((ts, D_MODEL), lambda i: (i, 0))
    wide = pl.BlockSpec((ts, D_FF), lambda i: (i, 0))
    return pl.pallas_call(
        body, name=f"ffn_bwd_act_{layer}", grid=(s // ts,),
        out_shape=(jax.ShapeDtypeStruct((s, D_MODEL), f32), jax.ShapeDtypeStruct((s, D_FF), bf16),
                   jax.ShapeDtypeStruct((s, D_MODEL), bf16), jax.ShapeDtypeStruct((8, D_MODEL), f32)),
        in_specs=[row, row, wide, row, pl.BlockSpec((8, D_MODEL), lambda i: (0, 0)),
                  _layer_weights((D_MODEL, FF_CHUNK)), _layer_weights((FF_CHUNK, D_MODEL))],
        out_specs=(row, wide, row, pl.BlockSpec((8, D_MODEL), lambda i: (0, 0))),
        compiler_params=_params(("arbitrary",), 58),
    )(x, dx, u, y, vec, w1g, w2g)


def _ffn_bwd_w1(hb, da, layer):
    s = hb.shape[0]
    ts = min(FFN_GRAD_TILE, s)
    nt = s // ts

    def body(hb_ref, da_ref, dw_ref, acc_ref):
        i = pl.program_id(0)

        @pl.when(i == 0)
        def _():
            acc_ref[...] = jnp.zeros_like(acc_ref)

        hb = hb_ref[...]
        for f in range(N_DEV):
            acc_ref[f] += _mm_tn(hb, da_ref[:, f * FF_CHUNK:(f + 1) * FF_CHUNK])

        @pl.when(i == nt - 1)
        def _():
            dw_ref[...] = acc_ref[...].astype(bf16)

    return pl.pallas_call(
        body, name=f"ffn_bwd_w1_{layer}", grid=(nt,),
        out_shape=jax.ShapeDtypeStruct((N_DEV, D_MODEL, FF_CHUNK), bf16),
        in_specs=[pl.BlockSpec((ts, D_MODEL), lambda i: (i, 0)), pl.BlockSpec((ts, D_FF), lambda i: (i, 0))],
        out_specs=pl.BlockSpec((N_DEV, D_MODEL, FF_CHUNK), lambda i: (0, 0, 0)),
        scratch_shapes=[pltpu.VMEM((N_DEV, D_MODEL, FF_CHUNK), f32)],
        compiler_params=_params(("arbitrary",), 56),
    )(hb, da)


def _ffn_bwd_w2(u, dyb, layer):
    s = u.shape[0]
    ts = min(FFN_GRAD_TILE, s)
    nt = s // ts

    def body(u_ref, dyb_ref, dw_ref, acc_ref):
        i = pl.program_id(0)

        @pl.when(i == 0)
        def _():
            acc_ref[...] = jnp.zeros_like(acc_ref)

        dyb = dyb_ref[...]
        for f in range(N_DEV):
            ub = u_ref[:, f * FF_CHUNK:(f + 1) * FF_CHUNK]
            acc_ref[f] += _mm_tn(ub * ub, dyb)

        @pl.when(i == nt - 1)
        def _():
            dw_ref[...] = acc_ref[...].astype(bf16)

    return pl.pallas_call(
        body, name=f"ffn_bwd_w2_{layer}", grid=(nt,),
        out_shape=jax.ShapeDtypeStruct((N_DEV, FF_CHUNK, D_MODEL), bf16),
        in_specs=[pl.BlockSpec((ts, D_FF), lambda i: (i, 0)), pl.BlockSpec((ts, D_MODEL), lambda i: (i, 0))],
        out_specs=pl.BlockSpec((N_DEV, FF_CHUNK, D_MODEL), lambda i: (0, 0, 0)),
        scratch_shapes=[pltpu.VMEM((N_DEV, FF_CHUNK, D_MODEL), f32)],
        compiler_params=_params(("arbitrary",), 56),
    )(u, dyb)


def _lru_gates(xc, wsm_ref, pv_ref):
    xcb = xc.astype(bf16)
    gr = _sigmoid(_block_diag(xcb, wsm_ref, 0) + pv_ref[P_BA:P_BA + 1, :])
    gi = _sigmoid(_block_diag(xcb, wsm_ref, 1) + pv_ref[P_BX:P_BX + 1, :])
    log_a = (LRU_C * _log_sigmoid(pv_ref[P_LAM:P_LAM + 1, :])) * gr
    t = jnp.tanh(log_a)
    return gr, gi, jnp.exp(log_a), jnp.sqrt((-2.0 * t) / (1.0 - t))


def _conv(xr, taps_before, pv_ref):
    xc = xr * pv_ref[P_CW0 + 3:P_CW0 + 4, :] + pv_ref[P_CONVB:P_CONVB + 1, :]
    for k, v in zip((2, 1, 0), taps_before):
        xc = xc + v * pv_ref[P_CW0 + k:P_CW0 + k + 1, :]
    return xc


LRU_FWD_SUB, LRU_FWD_SUBS = 128, 2
LRU_BWD_SUB, LRU_BWD_SUBS = 256, 1


def _scan_rows(a, u, carry, reverse):
    groups = a.shape[0] // SUBLANES
    row = lax.broadcasted_iota(jnp.int32, (SUBLANES, a.shape[1]), 0)
    outs = [None] * groups
    for j in range(groups):
        g = groups - 1 - j if reverse else j
        av, uv = a[g * SUBLANES:(g + 1) * SUBLANES], u[g * SUBLANES:(g + 1) * SUBLANES]
        for k in (1, 2, 4):
            if reverse:
                valid, shift = row < SUBLANES - k, SUBLANES - k
            else:
                valid, shift = row >= k, k
            a_s = jnp.where(valid, pltpu.roll(av, shift, 0), 1.0)
            u_s = jnp.where(valid, pltpu.roll(uv, shift, 0), 0.0)
            uv = uv + av * u_s
            av = av * a_s
        h = uv + av * carry
        outs[g] = h
        carry = h[0:1, :] if reverse else h[SUBLANES - 1:SUBLANES, :]
    return jnp.concatenate(outs, axis=0), carry


def _lru_fwd(x, vec, wbig, wsm, pvec, layer):
    s = x.shape[0]
    sub = min(LRU_FWD_SUB, s)
    ts = min(sub * LRU_FWD_SUBS, s)
    nsub = ts // sub
    w = LRU_WIDTH

    def body(x_ref, vec_ref, wb_ref, wsm_ref, pv_ref, xo_ref, xr_ref, hs_ref, a_ref, mult_ref, gr_ref, gi_ref,
             gel_ref, geld_ref, y_ref, tail_ref, carry_ref):
        @pl.when(pl.program_id(0) == 0)
        def _():
            tail_ref[...] = jnp.zeros_like(tail_ref)
            carry_ref[...] = jnp.zeros_like(carry_ref)

        sel = _shift_matrix(sub, BF16_ROWS, (1, 2, 3))
        for k in range(nsub):
            rows = slice(k * sub, (k + 1) * sub)
            xv = x_ref[rows, :]
            n, _ = _rms(xv)
            hb = (n * vec_ref[R_GS_M:R_GS_M + 1, :] + vec_ref[R_SH_M:R_SH_M + 1, :]).astype(bf16)
            gelu_v, gelu_d = _gelu_and_grad(_mm(hb, wb_ref[0]) + pv_ref[P_BY:P_BY + 1, :])
            gel_ref[rows, :] = gelu_v.astype(bf16)
            geld_ref[rows, :] = gelu_d.astype(bf16)
            xrb = (_mm(hb, wb_ref[1]) + pv_ref[P_BIN:P_BIN + 1, :]).astype(bf16)
            xr_ref[rows, :] = xrb
            xc = _conv(xrb.astype(f32), _shifted_rows(sel, tail_ref[...], xrb), pv_ref)
            tail_ref[...] = xrb[sub - BF16_ROWS:, :]
            gr, gi, a, mult = _lru_gates(xc, wsm_ref, pv_ref)
            gr_ref[rows, :] = gr.astype(bf16)
            gi_ref[rows, :] = gi.astype(bf16)
            a_ref[rows, :] = a
            mult_ref[rows, :] = mult
            hs, carry = _scan_rows(a, mult * (gi * xc), carry_ref[0:1, :], reverse=False)
            carry_ref[0:1, :] = carry
            hs_ref[rows, :] = hs
            yv = _mm((hs * gelu_v).astype(bf16), wb_ref[2]) + pv_ref[P_BOUT:P_BOUT + 1, :]
            y_ref[rows, :] = yv.astype(bf16)
            xo_ref[rows, :] = xv + vec_ref[R_GT_M:R_GT_M + 1, :] * yv

    row = pl.BlockSpec((ts, D_MODEL), lambda i: (i, 0))
    roww = pl.BlockSpec((ts, w), lambda i: (i, 0))
    wide = lambda dt: jax.ShapeDtypeStruct((s, w), dt)
    return pl.pallas_call(
        body, name=f"lru_fwd_{layer}", grid=(s // ts,),
        out_shape=(jax.ShapeDtypeStruct((s, D_MODEL), f32), wide(bf16), wide(f32), wide(f32), wide(f32),
                   wide(bf16), wide(bf16), wide(bf16), wide(bf16), jax.ShapeDtypeStruct((s, D_MODEL), bf16)),
        in_specs=[row, pl.BlockSpec((8, D_MODEL), lambda i: (0, 0)),
                  pl.BlockSpec((3, w, w), lambda i: (0, 0, 0)),
                  pl.BlockSpec((2, HEADS, HEAD_DIM, HEAD_DIM), lambda i: (0, 0, 0, 0)),
                  pl.BlockSpec((16, w), lambda i: (0, 0))],
        out_specs=(row, roww, roww, roww, roww, roww, roww, roww, roww, row),
        scratch_shapes=[pltpu.VMEM((BF16_ROWS, w), bf16), pltpu.VMEM((SUBLANES, w), f32)],
        compiler_params=_params(("arbitrary",)),
    )(x, vec, wbig, wsm, pvec)


def _lru_bwd(x, dx, saved, vec, wbig, wsm, pvec, layer):
    xr, hs, a_all, mult_all, gr_all, gi_all, gel_all, geld_all, y = saved
    s = x.shape[0]
    sub = min(LRU_BWD_SUB, s)
    ts = min(sub * LRU_BWD_SUBS, s)
    nsub = ts // sub
    nt = s // ts
    w = LRU_WIDTH
    shard = w // N_DEV
    hshard = HEAD_DIM // N_DEV

    def body(x_ref, dx_ref, xr_ref, xrh_ref, hs_ref, hsh_ref, a_ref, mult_ref, gr_ref, gi_ref, gel_ref, geld_ref,
             y_ref, vec_ref, wb_ref, wsm_ref, pv_ref,
             dxo_ref, dwb_ref, dwsm_ref, sm_ref, accb_ref, accs_ref, eps_ref, dxc8_ref,
             hb_scr, dgb_scr, dxrb_scr, mb_scr, dyb_scr, xcb_scr, drab_scr, drxb_scr):
        i = pl.program_id(0)
        first_tile = i == nt - 1

        @pl.when(i == 0)
        def _():
            accb_ref[...] = jnp.zeros_like(accb_ref)
            accs_ref[...] = jnp.zeros_like(accs_ref)
            sm_ref[...] = jnp.zeros_like(sm_ref)
            eps_ref[...] = jnp.zeros_like(eps_ref)
            dxc8_ref[...] = jnp.zeros_like(dxc8_ref)

        gs = vec_ref[R_GS_M:R_GS_M + 1, :]
        c_ls = LRU_C * _log_sigmoid(pv_ref[P_LAM:P_LAM + 1, :])
        for k in reversed(range(nsub)):
            rows = slice(k * sub, (k + 1) * sub)
            xv = x_ref[rows, :]
            dxv = dx_ref[rows, :]
            n, r = _rms(xv)
            hb_scr[rows, :] = (n * gs + vec_ref[R_SH_M:R_SH_M + 1, :]).astype(bf16)
            xrv = xr_ref[rows, :].astype(f32)
            hsv = hs_ref[rows, :]
            if k == 0:
                xr_halo = jnp.where(first_tile, 0.0, xrh_ref[...].astype(f32))
                hs_halo = jnp.where(first_tile, 0.0, hsh_ref[...])
            else:
                xr_halo = xr_ref[k * sub - BF16_ROWS:k * sub, :].astype(f32)
                hs_halo = hs_ref[k * sub - SUBLANES:k * sub, :]
            xs1, xs2, xs3 = _rows_before(xr_halo, xrv, (1, 2, 3))
            xc = _conv(xrv, (xs1, xs2, xs3), pv_ref)
            xcb_scr[rows, :] = xc.astype(bf16)
            a, mult = a_ref[rows, :], mult_ref[rows, :]
            gr, gi = gr_ref[rows, :].astype(f32), gi_ref[rows, :].astype(f32)
            gelu_v = gel_ref[rows, :].astype(f32)

            dy = dxv * vec_ref[R_GT_M:R_GT_M + 1, :]
            dyb = dy.astype(bf16)
            dyb_scr[rows, :] = dyb
            sm_ref[G_GT:G_GT + 1, :] += _colsum(dxv * y_ref[rows, :].astype(f32))
            sm_ref[G_BOUT:G_BOUT + 1, :] += _colsum(dy)
            mb_scr[rows, :] = (hsv * gelu_v).astype(bf16)
            dm = _mm_nt(dyb, wb_ref[2])
            dhs = dm * gelu_v
            dgpre = dm * hsv * geld_ref[rows, :].astype(f32)
            dgb = dgpre.astype(bf16)
            dgb_scr[rows, :] = dgb
            sm_ref[G_BY:G_BY + 1, :] += _colsum(dgpre)

            eps_in = eps_ref[0:1, :]
            eps, eps_out = _scan_rows(a, a * dhs, eps_in, reverse=True)
            eps_ref[0:1, :] = eps_out
            (eps_next,) = _rows_after(eps, jnp.broadcast_to(eps_in, (SUBLANES, w)), (1,))
            delta = dhs + eps_next
            (h_prev,) = _rows_before(hs_halo, hsv, (1,))
            dxi = delta * xc
            dgi = dxi * mult
            dla = (delta * h_prev) * a - (dxi * gi) * (a * a) / mult
            sm_ref[G_LS:G_LS + 1, :] += _colsum(dla * gr)
            dra = (dla * c_ls) * (gr - gr * gr)
            drx = dgi * (gi - gi * gi)
            drab, drxb = dra.astype(bf16), drx.astype(bf16)
            drab_scr[rows, :] = drab
            drxb_scr[rows, :] = drxb
            sm_ref[G_BA:G_BA + 1, :] += _colsum(dra)
            sm_ref[G_BX:G_BX + 1, :] += _colsum(drx)
            dxc = (delta * mult) * gi + _block_diag_t(drab, wsm_ref, 0) + _block_diag_t(drxb, wsm_ref, 1)

            sm_ref[G_CONVB:G_CONVB + 1, :] += _colsum(dxc)
            for kk, v in zip((3, 2, 1, 0), (xrv, xs1, xs2, xs3)):
                sm_ref[G_CW0 + kk:G_CW0 + kk + 1, :] += _colsum(dxc * v)
            ups = _rows_after(dxc, dxc8_ref[...], (1, 2, 3))
            dxc8_ref[...] = dxc[0:SUBLANES, :]
            dxr = dxc * pv_ref[P_CW0 + 3:P_CW0 + 4, :]
            for kk, v in zip((2, 1, 0), ups):
                dxr = dxr + v * pv_ref[P_CW0 + kk:P_CW0 + kk + 1, :]
            dxrb = dxr.astype(bf16)
            dxrb_scr[rows, :] = dxrb
            sm_ref[G_BIN:G_BIN + 1, :] += _colsum(dxr)
            dh = _mm_nt(dgb, wb_ref[0]) + _mm_nt(dxrb, wb_ref[1])
            sm_ref[G_SH:G_SH + 1, :] += _colsum(dh)
            sm_ref[G_GS:G_GS + 1, :] += _colsum(dh * n)
            dxo_ref[rows, :] = dxv + _norm_bwd(dh, n, r, gs)

        hb = hb_scr[...]
        accb_ref[0] += _mm_tn(hb, dgb_scr[...])
        accb_ref[1] += _mm_tn(hb, dxrb_scr[...])
        accb_ref[2] += _mm_tn(mb_scr[...], dyb_scr[...])
        for h in range(HEADS):
            cols = slice(h * HEAD_DIM, (h + 1) * HEAD_DIM)
            accs_ref[0, h] += _mm_tn(xcb_scr[:, cols], drab_scr[:, cols])
            accs_ref[1, h] += _mm_tn(xcb_scr[:, cols], drxb_scr[:, cols])

        @pl.when(i == nt - 1)
        def _():
            sm_ref[G_LS:G_LS + 1, :] = sm_ref[G_LS:G_LS + 1, :] * LRU_C
            for k in range(3):
                dwb_ref[:, k] = accb_ref[k].astype(bf16).reshape(N_DEV, shard, w)
            for k in range(2):
                for h in range(HEADS):
                    dwsm_ref[:, k, h] = accs_ref[k, h].astype(bf16).reshape(N_DEV, hshard, HEAD_DIM)

    rev = lambda i: (nt - 1 - i, 0)
    row = pl.BlockSpec((ts, D_MODEL), rev)
    roww = pl.BlockSpec((ts, w), rev)
    halo16 = pl.BlockSpec((BF16_ROWS, w), lambda i: (jnp.maximum((nt - 1 - i) * (ts // BF16_ROWS) - 1, 0), 0))
    halo8 = pl.BlockSpec((SUBLANES, w), lambda i: (jnp.maximum((nt - 1 - i) * (ts // SUBLANES) - 1, 0), 0))
    const = lambda *shape: pl.BlockSpec(shape, lambda i: (0,) * len(shape))
    operand = pltpu.VMEM((ts, w), bf16)
    return pl.pallas_call(
        body, name=f"lru_bwd_{layer}", grid=(nt,),
        out_shape=(jax.ShapeDtypeStruct((s, D_MODEL), f32),
                   jax.ShapeDtypeStruct((N_DEV, 3, shard, w), bf16),
                   jax.ShapeDtypeStruct((N_DEV, 2, HEADS, hshard, HEAD_DIM), bf16),
                   jax.ShapeDtypeStruct((16, w), f32)),
        in_specs=[row, row, roww, halo16, roww, halo8, roww, roww, roww, roww, roww, roww, row, const(8, D_MODEL),
                  const(3, w, w), const(2, HEADS, HEAD_DIM, HEAD_DIM), const(16, w)],
        out_specs=(row, const(N_DEV, 3, shard, w), const(N_DEV, 2, HEADS, hshard, HEAD_DIM), const(16, w)),
        scratch_shapes=[pltpu.VMEM((3, w, w), f32), pltpu.VMEM((2, HEADS, HEAD_DIM, HEAD_DIM), f32),
                        pltpu.VMEM((SUBLANES, w), f32), pltpu.VMEM((SUBLANES, w), f32)] + [operand] * 8,
        compiler_params=_params(("arbitrary",), 58),
    )(x, dx, xr, xr, hs, hs, a_all, mult_all, gr_all, gi_all, gel_all, geld_all, y, vec, wbig, wsm, pvec)


def _pool_tile(s):
    return min(256, s)


def _pool_counts(tile_index, ts):
    t = (tile_index * ts + lax.broadcasted_iota(jnp.int32, (ts, 1), 0) + 1).astype(f32)
    return [1.0 / jnp.minimum(t, float(win)) for win in POOL_WINDOWS]


def _pooled(h, halo, inv):
    ext = jnp.concatenate([halo, h], axis=0)
    out = []
    for g in range(len(POOL_WINDOWS)):
        acc = ext[:, g * HEAD_DIM:(g + 1) * HEAD_DIM]
        for step in range(g + 1):
            acc = acc + pltpu.roll(acc, 1 << step, 0)
        out.append(acc[POOL_HALO:] * inv[g] - h[:, g * HEAD_DIM:(g + 1) * HEAD_DIM])
    return out


def _pool_fwd(x, vec, pw, ps, layer):
    s = x.shape[0]
    ts = _pool_tile(s)

    def body(x_ref, vec_ref, pw_ref, ps_ref, xo_ref, y_ref, halo_ref):
        i = pl.program_id(0)

        @pl.when(i == 0)
        def _():
            halo_ref[...] = jnp.zeros_like(halo_ref)

        xv = x_ref[...]
        n, _ = _rms(xv)
        h = n * vec_ref[R_GS_M:R_GS_M + 1, :] + vec_ref[R_SH_M:R_SH_M + 1, :]
        pooled = _pooled(h, halo_ref[...], _pool_counts(i, ts))
        halo_ref[...] = h[ts - POOL_HALO:, :]
        mixed = jnp.concatenate([_mm(pooled[g].astype(bf16), pw_ref[g]) for g in range(HEADS)], axis=1)
        yv = mixed * ps_ref[0:1, :]
        y_ref[...] = yv.astype(bf16)
        xo_ref[...] = xv + vec_ref[R_GT_M:R_GT_M + 1, :] * yv

    row = pl.BlockSpec((ts, D_MODEL), lambda i: (i, 0))
    return pl.pallas_call(
        body, name=f"pool_fwd_{layer}", grid=(s // ts,),
        out_shape=(jax.ShapeDtypeStruct((s, D_MODEL), f32), jax.ShapeDtypeStruct((s, D_MODEL), bf16)),
        in_specs=[row, pl.BlockSpec((8, D_MODEL), lambda i: (0, 0)),
                  pl.BlockSpec((HEADS, HEAD_DIM, HEAD_DIM), lambda i: (0, 0, 0)),
                  pl.BlockSpec((8, D_MODEL), lambda i: (0, 0))],
        out_specs=(row, row),
        scratch_shapes=[pltpu.VMEM((POOL_HALO, D_MODEL), f32)],
        compiler_params=_params(("arbitrary",)),
    )(x, vec, pw, ps)


def _pool_bwd(x, dx, y, vec, pw, ps, u, dyb, layer):
    s = x.shape[0]
    ts = _pool_tile(s)
    nt = s // ts
    hshard = HEAD_DIM // N_DEV

    def body(x_ref, xh_ref, dx_ref, y_ref, vec_ref, pw_ref, ps_ref, u_ref, dyb_ref,
             dxo_ref, dpw_ref, sm_ref, dw2_ref, acc_ref, q16_ref, acc2_ref):
        i = pl.program_id(0)
        tile = nt - 1 - i

        @pl.when(i == 0)
        def _():
            acc_ref[...] = jnp.zeros_like(acc_ref)
            acc2_ref[...] = jnp.zeros_like(acc2_ref)
            sm_ref[...] = jnp.zeros_like(sm_ref)
            q16_ref[...] = jnp.zeros_like(q16_ref)

        dybv = dyb_ref[...]
        for f in range(N_DEV):
            ub = u_ref[:, f * FF_CHUNK:(f + 1) * FF_CHUNK]
            acc2_ref[f] += _mm_tn(ub * ub, dybv)

        gs, sh = vec_ref[R_GS_M:R_GS_M + 1, :], vec_ref[R_SH_M:R_SH_M + 1, :]
        xv = x_ref[...]
        dxv = dx_ref[...]
        n, r = _rms(xv)
        h = n * gs + sh
        nh, _ = _rms(xh_ref[...])
        halo = jnp.where(tile == 0, 0.0, nh * gs + sh)
        inv = _pool_counts(tile, ts)
        pooled = _pooled(h, halo, inv)
        mixed = jnp.concatenate([_mm(pooled[g].astype(bf16), pw_ref[g]) for g in range(HEADS)], axis=1)

        dy = dxv * vec_ref[R_GT_M:R_GT_M + 1, :]
        sm_ref[G_GT:G_GT + 1, :] += _colsum(dxv * y_ref[...].astype(f32))
        sm_ref[3:4, :] += _colsum(dy * mixed)
        dmixed = (dy * ps_ref[0:1, :]).astype(bf16)
        dh_parts = []
        for g in range(HEADS):
            cols = slice(g * HEAD_DIM, (g + 1) * HEAD_DIM)
            acc_ref[g] += _mm_tn(pooled[g].astype(bf16), dmixed[:, cols])
            dpooled = _mm_nt(dmixed[:, cols], pw_ref[g])
            q = dpooled * inv[g]
            ext = jnp.concatenate([q, q16_ref[:, cols]], axis=0)
            q16_ref[:, cols] = q[0:POOL_HALO, :]
            for step in range(g + 1):
                ext = ext + pltpu.roll(ext, ext.shape[0] - (1 << step), 0)
            dh_parts.append(ext[:ts] - dpooled)
        dh = jnp.concatenate(dh_parts, axis=1)
        sm_ref[G_SH:G_SH + 1, :] += _colsum(dh)
        sm_ref[G_GS:G_GS + 1, :] += _colsum(dh * n)
        dxo_ref[...] = dxv + _norm_bwd(dh, n, r, gs)

        @pl.when(i == nt - 1)
        def _():
            for g in range(HEADS):
                dpw_ref[:, g] = acc_ref[g].astype(bf16).reshape(N_DEV, hshard, HEAD_DIM)
            dw2_ref[...] = acc2_ref[...].astype(bf16)

    rev = lambda i: (nt - 1 - i, 0)
    row = pl.BlockSpec((ts, D_MODEL), rev)
    halo16 = pl.BlockSpec((POOL_HALO, D_MODEL), lambda i: (jnp.maximum((nt - 1 - i) * (ts // POOL_HALO) - 1, 0), 0))
    const = lambda *shape: pl.BlockSpec(shape, lambda i: (0,) * len(shape))
    return pl.pallas_call(
        body, name=f"pool_bwd_{layer}", grid=(nt,),
        out_shape=(jax.ShapeDtypeStruct((s, D_MODEL), f32),
                   jax.ShapeDtypeStruct((N_DEV, HEADS, hshard, HEAD_DIM), bf16),
                   jax.ShapeDtypeStruct((8, D_MODEL), f32),
                   jax.ShapeDtypeStruct((N_DEV, FF_CHUNK, D_MODEL), bf16)),
        in_specs=[row, halo16, row, row, const(8, D_MODEL), const(HEADS, HEAD_DIM, HEAD_DIM), const(8, D_MODEL),
                  pl.BlockSpec((ts, D_FF), rev), row],
        out_specs=(row, const(N_DEV, HEADS, hshard, HEAD_DIM), const(8, D_MODEL), const(N_DEV, FF_CHUNK, D_MODEL)),
        scratch_shapes=[pltpu.VMEM((HEADS, HEAD_DIM, HEAD_DIM), f32), pltpu.VMEM((POOL_HALO, D_MODEL), f32),
                        pltpu.VMEM((N_DEV, FF_CHUNK, D_MODEL), f32)],
        compiler_params=_params(("arbitrary",), 56),
    )(x, x, dx, y, vec, pw, ps, u, dyb)


def _final(x, target, g_fin):
    s = x.shape[0]
    ts = min(512, s)

    def body(x_ref, t_ref, g_ref, dx_ref, loss_ref, sm_ref):
        @pl.when(pl.program_id(0) == 0)
        def _():
            loss_ref[...] = jnp.zeros_like(loss_ref)
            sm_ref[...] = jnp.zeros_like(sm_ref)

        g = g_ref[0:1, :]
        n, r = _rms(x_ref[...])
        err = n * g - t_ref[...]
        loss_ref[...] += 0.5 * jnp.sum(jnp.mean(err * err, axis=-1, keepdims=True), axis=0, keepdims=True)
        dyv = err * (1.0 / D_MODEL)
        sm_ref[0:1, :] += _colsum(dyv * n)
        dx_ref[...] = _norm_bwd(dyv, n, r, g)

    row = pl.BlockSpec((ts, D_MODEL), lambda i: (i, 0))
    return pl.pallas_call(
        body, name="final_loss", grid=(s // ts,),
        out_shape=(jax.ShapeDtypeStruct((s, D_MODEL), f32), jax.ShapeDtypeStruct((8, 128), f32),
                   jax.ShapeDtypeStruct((8, D_MODEL), f32)),
        in_specs=[row, row, pl.BlockSpec((8, D_MODEL), lambda i: (0, 0))],
        out_specs=(row, pl.BlockSpec((8, 128), lambda i: (0, 0)), pl.BlockSpec((8, D_MODEL), lambda i: (0, 0))),
        compiler_params=_params(("arbitrary",)),
    )(x, target, g_fin)


def _small_pack(sm_ffn, sm_mix, sm_fin, table, g_mix, g_ffn, lam):
    def body(*refs):
        ffn, mix = refs[0:DEPTH], refs[DEPTH:2 * DEPTH]
        fin_ref, tab_ref, gm_ref, gf_ref, lam_ref, o_ref = refs[2 * DEPTH:]
        o_ref[...] = jnp.zeros_like(o_ref)
        for i in range(DEPTH):
            base = K_MOD + i * N_MOD
            o_ref[base + 0:base + 1, :] = mix[i][G_SH:G_SH + 1, :]
            o_ref[base + 1:base + 2, :] = mix[i][G_GS:G_GS + 1, :] * gm_ref[i:i + 1, :]
            o_ref[base + 2:base + 3, :] = mix[i][G_GT:G_GT + 1, :]
            o_ref[base + 3:base + 4, :] = ffn[i][G_SH:G_SH + 1, :]
            o_ref[base + 4:base + 5, :] = ffn[i][G_GS:G_GS + 1, :] * gf_ref[i:i + 1, :]
            o_ref[base + 5:base + 6, :] = ffn[i][G_GT:G_GT + 1, :]
            o_ref[K_NMIX + i:K_NMIX + i + 1, :] = mix[i][G_GS:G_GS + 1, :] * (1.0 + tab_ref[i, R_SC_M:R_SC_M + 1, :])
            o_ref[K_NFFN + i:K_NFFN + i + 1, :] = ffn[i][G_GS:G_GS + 1, :] * (1.0 + tab_ref[i, R_SC_F:R_SC_F + 1, :])
            j = i // 2
            if i % 2 == 0:
                for k, src in enumerate((G_BY, G_BIN, G_CONVB, None, G_BOUT)):
                    dst = K_LRUB + j * 5 + k
                    if src is None:
                        o_ref[dst:dst + 1, :] = mix[i][G_LS:G_LS + 1, :] * _sigmoid(-lam_ref[j:j + 1, :])
                    else:
                        o_ref[dst:dst + 1, :] = mix[i][src:src + 1, :]
                o_ref[K_CONVW + j * 4:K_CONVW + j * 4 + 4, :] = mix[i][G_CW0:G_CW0 + 4, :]
                o_ref[K_BA + j:K_BA + j + 1, :] = mix[i][G_BA:G_BA + 1, :]
                o_ref[K_BX + j:K_BX + j + 1, :] = mix[i][G_BX:G_BX + 1, :]
            else:
                o_ref[K_PS + j:K_PS + j + 1, :] = mix[i][3:4, :]
        o_ref[K_FIN:K_FIN + 1, :] = fin_ref[0:1, :]

    return pl.pallas_call(body, name="small_pack", out_shape=jax.ShapeDtypeStruct((K_ROWS, D_MODEL), f32))(
        *sm_ffn, *sm_mix, sm_fin, table, g_mix, g_ffn, lam)


def _small_sum(gathered):
    def body(g_ref, o_ref, token_ref):
        tot = g_ref[0]
        for src in range(1, N_DEV):
            tot = tot + g_ref[src]
        o_ref[...] = tot
        token_ref[...] = jnp.zeros_like(token_ref)

    return pl.pallas_call(
        body, name="small_sum",
        out_shape=(jax.ShapeDtypeStruct(gathered.shape[1:], f32), jax.ShapeDtypeStruct((8, 128), f32)))(gathered)


def _adamw_math(g, w, m, v):
    m = ADAM_B1 * m + (1.0 - ADAM_B1) * g
    v = ADAM_B2 * v + (1.0 - ADAM_B2) * (g * g)
    m_hat = m / (1.0 - ADAM_B1 ** ADAM_STEP)
    v_hat = v / (1.0 - ADAM_B2 ** ADAM_STEP)
    delta = -ADAM_LR * (m_hat / (jnp.sqrt(v_hat) + ADAM_EPS) + ADAM_WD * w)
    return delta, m, v


def _adamw_small(name, g, w, m, v):
    shape = w.shape
    two_d = (1, shape[0]) if len(shape) == 1 else (math.prod(shape[:-1]), shape[-1])

    def body(g_ref, w_ref, m_ref, v_ref, d_ref, mo_ref, vo_ref):
        d_ref[...], mo_ref[...], vo_ref[...] = _adamw_math(g_ref[...], w_ref[...], m_ref[...], v_ref[...])

    outs = pl.pallas_call(body, name=f"adamw_{name}", out_shape=tuple(jax.ShapeDtypeStruct(two_d, f32) for _ in range(3)))(
        *(t.reshape(two_d) for t in (g, w, m, v)))
    return tuple(t.reshape(shape) for t in outs)


def _block_rows(rows, cols):
    tr = max(SUBLANES, min(rows, (512 * 1024) // (4 * cols)))
    while rows % tr:
        tr //= 2
    return tr


def _adamw_reduce(name, landings, kind, w, m, v):
    nl = len(landings)
    rows, cols = landings[0].shape[2:]
    tr = _block_rows(rows, cols)
    per_layer = rows // tr

    def body(*refs):
        l_refs = refs[:nl]
        w_ref, m_ref, v_ref, g_ref, d_ref, mo_ref, vo_ref = refs[nl:]
        layer = pl.program_id(0)
        for k in range(nl):
            @pl.when(layer == k)
            def _(k=k):
                g = l_refs[k][0].astype(f32)
                for src in range(1, N_DEV):
                    g = g + l_refs[k][src].astype(f32)
                g_ref[...] = g
        d_ref[...], mo_ref[...], vo_ref[...] = _adamw_math(g_ref[...], w_ref[...], m_ref[...], v_ref[...])

    blk = pl.BlockSpec((tr, cols), lambda l, r: (l * per_layer + r, 0))
    land = [pl.BlockSpec((N_DEV, None, tr, cols), lambda l, r, k=k: (0, kind, jnp.where(l == k, r, 0), 0)) for k in range(nl)]
    return pl.pallas_call(
        body, name=f"adamw_{name}", grid=(nl, per_layer),
        out_shape=tuple(jax.ShapeDtypeStruct((nl * rows, cols), f32) for _ in range(4)),
        in_specs=land + [blk, blk, blk],
        out_specs=(blk, blk, blk, blk),
        compiler_params=_params(("arbitrary", "arbitrary"), 32),
    )(*landings, w, m, v)


def _adamw_w_mod(c_all, dmod_all, w, m, v):
    depth, d, cols = w.shape
    tr = 256

    def body(c_ref, dm_ref, w_ref, m_ref, v_ref, g_ref, d_ref, mo_ref, vo_ref):
        cv = c_ref[...]
        cond = cv * _sigmoid(cv)
        g = lax.dot_general(cond, dm_ref[...], (((0,), (0,)), ((), ())), preferred_element_type=f32,
                            precision=lax.Precision.HIGHEST)
        g_ref[...] = g
        d_ref[...], mo_ref[...], vo_ref[...] = _adamw_math(g, w_ref[...], m_ref[...], v_ref[...])

    blk = pl.BlockSpec((None, tr, cols), lambda i, r: (i, r, 0))
    return pl.pallas_call(
        body, name="adamw_w_mod", grid=(depth, d // tr),
        out_shape=tuple(jax.ShapeDtypeStruct(w.shape, f32) for _ in range(4)),
        in_specs=[pl.BlockSpec((N_DEV, tr), lambda i, r: (0, r)),
                  pl.BlockSpec((None, N_DEV, cols), lambda i, r: (i, 0, 0)), blk, blk, blk],
        out_specs=(blk, blk, blk, blk),
        compiler_params=_params(("arbitrary", "arbitrary"), 32),
    )(c_all, dmod_all, w, m, v)


def kernel(x, c, w_mod, b_mod, norm_mix_g, norm_ffn_g, lru_w_y, lru_b_y, lru_w_in, lru_b_in, lru_conv_w, lru_conv_b, lru_w_a, lru_b_a, lru_w_x, lru_b_x, lru_lambda, lru_w_out, lru_b_out, pool_w, pool_scale, ffn_w1, ffn_w2, final_norm_g, loss_target, m_w_mod, m_b_mod, m_norm_mix_g, m_norm_ffn_g, m_lru_w_y, m_lru_b_y, m_lru_w_in, m_lru_b_in, m_lru_conv_w, m_lru_conv_b, m_lru_w_a, m_lru_b_a, m_lru_w_x, m_lru_b_x, m_lru_lambda, m_lru_w_out, m_lru_b_out, m_pool_w, m_pool_scale, m_ffn_w1, m_ffn_w2, m_final_norm_g, v_w_mod, v_b_mod, v_norm_mix_g, v_norm_ffn_g, v_lru_w_y, v_lru_b_y, v_lru_w_in, v_lru_b_in, v_lru_conv_w, v_lru_conv_b, v_lru_w_a, v_lru_b_a, v_lru_w_x, v_lru_b_x, v_lru_lambda, v_lru_w_out, v_lru_b_out, v_pool_w, v_pool_scale, v_ffn_w1, v_ffn_w2, v_final_norm_g):
    me = 4 * lax.axis_index("x") + 2 * lax.axis_index("y") + lax.axis_index("c")
    n_lru = lru_w_y.shape[0]
    shard = LRU_WIDTH // N_DEV
    hshard = HEAD_DIM // N_DEV
    xs = x[0]
    target = loss_target[0]

    small_vecs = jnp.concatenate([
        lru_conv_w.reshape(n_lru * 4, shard), lru_b_a.reshape(n_lru, HEADS * hshard),
        lru_b_x.reshape(n_lru, HEADS * hshard), pool_scale, jnp.zeros((2, shard), f32)], axis=0)
    first_mix, token = _send_start("gather_mix_start_0", [jnp.stack([lru_w_y[0], lru_w_in[0], lru_w_out[0]]).astype(bf16),
                                                          jnp.stack([lru_w_a[0], lru_w_x[0]]).astype(bf16)], True, me)
    sv_g, c_g = _exchange([small_vecs + token[0, 0], c], True, "gather_cond")
    conv_w_full = sv_g[:, 0:8].reshape(N_DEV, n_lru, 4, shard).transpose(1, 2, 0, 3).reshape(n_lru, 4, LRU_WIDTH)
    b_a_full = sv_g[:, 8:10].reshape(N_DEV, n_lru, HEADS, hshard).transpose(1, 2, 0, 3).reshape(n_lru, LRU_WIDTH)
    b_x_full = sv_g[:, 10:12].reshape(N_DEV, n_lru, HEADS, hshard).transpose(1, 2, 0, 3).reshape(n_lru, LRU_WIDTH)
    ps_full = sv_g[:, 12:14].transpose(1, 0, 2).reshape(n_lru, D_MODEL)
    c_all = c_g.reshape(N_DEV, D_MODEL)

    (mod_g,) = _exchange([_mod_part(c_all, w_mod)], True, "gather_mod")
    mod_row = lax.dynamic_index_in_dim(mod_g, me, axis=2, keepdims=False)
    mod_row = mod_row.transpose(1, 0, 2).reshape(DEPTH, N_MOD * D_MODEL)
    table, token = _mod_table(mod_row, b_mod, norm_mix_g, norm_ffn_g)

    h_mix, h_ffn = [first_mix], []
    for i in range(DEPTH):
        j = i // 2
        if i > 0:
            if i % 2 == 0:
                mix_src = [(jnp.stack([lru_w_y[j], lru_w_in[j], lru_w_out[j]]) + token[0, 0]).astype(bf16),
                           (jnp.stack([lru_w_a[j], lru_w_x[j]]) + token[0, 0]).astype(bf16)]
            else:
                mix_src = [(pool_w[j] + token[0, 0]).astype(bf16)]
            handle, token = _send_start(f"gather_mix_start_{i}", mix_src, True, me)
            h_mix.append(handle)
        handle, token = _send_start(f"gather_ffn_start_{i}", [(ffn_w1[i] + token[0, 0]).astype(bf16),
                                                               (ffn_w2[i] + token[0, 0]).astype(bf16)], True, me)
        h_ffn.append(handle)

    zero_row = jnp.zeros((1, LRU_WIDTH), f32)
    pvecs = [jnp.concatenate([lru_b_y[j:j + 1], lru_b_in[j:j + 1], lru_conv_b[j:j + 1], b_a_full[j:j + 1],
                              b_x_full[j:j + 1], lru_lambda[j:j + 1], lru_b_out[j:j + 1], zero_row,
                              conv_w_full[j], zero_row, zero_row, zero_row, zero_row], axis=0) for j in range(n_lru)]
    ps_rows = [jnp.concatenate([ps_full[j:j + 1], jnp.zeros((7, D_MODEL), f32)], axis=0) for j in range(n_lru)]

    saved = []
    ffn_w, mix_w = [], []
    h = xs
    for i in range(DEPTH):
        j = i // 2
        got, _ = _send_wait(f"gather_mix_wait_{i}", h_mix[i], h)
        if i % 2 == 0:
            mix_w.append((got[0].transpose(1, 0, 2, 3).reshape(3, LRU_WIDTH, LRU_WIDTH),
                          got[1].transpose(1, 2, 0, 3, 4).reshape(2, HEADS, HEAD_DIM, HEAD_DIM)))
            h_mid, *lru_saved = _lru_fwd(h, table[i], mix_w[i][0], mix_w[i][1], pvecs[j], i)
            mix_saved = (h, tuple(lru_saved))
        else:
            mix_w.append((got[0].transpose(1, 0, 2, 3).reshape(HEADS, HEAD_DIM, HEAD_DIM),))
            h_mid, y_mix = _pool_fwd(h, table[i], mix_w[i][0], ps_rows[j], i)
            mix_saved = (h, y_mix)
        ffn_w.append(_send_wait(f"gather_ffn_wait_{i}", h_ffn[i], h_mid)[0])
        h_out, u, y_ffn, hb = _ffn_fwd(h_mid, table[i], ffn_w[i][0], ffn_w[i][1], i)
        saved.append((mix_saved, (h_mid, u, y_ffn, hb)))
        h = h_out
    fin_rows = jnp.concatenate([final_norm_g[None, :], jnp.zeros((7, D_MODEL), f32)], axis=0)
    dx, loss_part, sm_fin = _final(h, target, fin_rows)
    loss = lax.psum(loss_part[0, 0], ("x", "y", "c"))

    sm_ffn, sm_mix = [None] * DEPTH, [None] * DEPTH
    x_ffn, x_mix = [None] * DEPTH, [None] * DEPTH
    token = jnp.zeros((8, 128), f32)
    for i in reversed(range(DEPTH)):
        j = i // 2
        mix_saved, (h_mid, u, y_ffn, hb) = saved[i]
        dx, da, dyb, sm_ffn[i] = _ffn_bwd_act(h_mid, dx, u, y_ffn, table[i] + token[0, 0], ffn_w[i][0], ffn_w[i][1], i)
        if i % 2 == 0:
            x_ffn[i], token = _send_start(f"grads_ffn_start_{i}", [_ffn_bwd_w1(hb, da, i), _ffn_bwd_w2(u, dyb, i)], False, me)
            h_in, lru_saved = mix_saved
            dx, dbig, dsmall, sm_mix[i] = _lru_bwd(
                h_in, dx, lru_saved, table[i] + token[0, 0], mix_w[i][0], mix_w[i][1], pvecs[j], i)
            last_mix = [dbig, dsmall]
        else:
            h_in, y_mix = mix_saved
            dx, dpool, sm, dw2 = _pool_bwd(h_in, dx, y_mix, table[i], mix_w[i][0], ps_rows[j], u, dyb, i)
            x_ffn[i], token = _send_start(f"grads_ffn_start_{i}", [_ffn_bwd_w1(hb, da, i), dw2], False, me)
            sm_mix[i] = jnp.concatenate([sm, jnp.zeros((8, D_MODEL), f32)], axis=0)
            last_mix = [dpool]
        if i > 0:
            x_mix[i], token = _send_start(f"grads_mix_start_{i}", last_mix, False, me)
    grad_x = dx[None]

    pack = _small_pack(sm_ffn, sm_mix, sm_fin, table + token[0, 0], norm_mix_g, norm_ffn_g, lru_lambda)
    (pack_g,) = _exchange([pack], True, "gather_small_grads")
    tot, token = _small_sum(pack_g)
    x_mix[0], _ = _send_start("grads_mix_start_0", [t + token[0, 0].astype(bf16) for t in last_mix], False, me)
    cols = w_mod.shape[2]
    dmod_all = lax.dynamic_slice_in_dim(pack_g[:, K_MOD:K_MOD + DEPTH * N_MOD].reshape(N_DEV, DEPTH, N_MOD * D_MODEL),
                                        me * cols, cols, axis=2).transpose(1, 0, 2)
    results = {"w_mod": _adamw_w_mod(c_all, dmod_all, w_mod, m_w_mod, v_w_mod)}

    after = results["w_mod"][1]
    l_ffn = [_send_wait(f"grads_ffn_wait_{i}", x_ffn[i], after)[0] for i in reversed(range(DEPTH))][::-1]

    def reduce_update(name, landings, kind, w, m, v):
        rows = w.size // w.shape[-1]
        two_d = (rows, w.shape[-1])
        lands = [t.reshape(N_DEV, -1, rows // len(landings), w.shape[-1]) for t in landings]
        outs = _adamw_reduce(name, lands, kind, w.reshape(two_d), m.reshape(two_d), v.reshape(two_d))
        return tuple(t.reshape(w.shape) for t in outs)

    results["ffn_w1"] = reduce_update("ffn_w1", [t[0] for t in l_ffn], 0, ffn_w1, m_ffn_w1, v_ffn_w1)
    results["ffn_w2"] = reduce_update("ffn_w2", [t[1] for t in l_ffn], 0, ffn_w2, m_ffn_w2, v_ffn_w2)
    after = results["ffn_w2"][1]
    l_mix = [_send_wait(f"grads_mix_wait_{i}", x_mix[i], after)[0] for i in reversed(range(DEPTH))][::-1]
    l_lru_big = [l_mix[i][0] for i in range(0, DEPTH, 2)]
    l_lru_small = [l_mix[i][1] for i in range(0, DEPTH, 2)]
    l_pool = [l_mix[i][0] for i in range(1, DEPTH, 2)]
    results["lru_w_y"] = reduce_update("lru_w_y", l_lru_big, 0, lru_w_y, m_lru_w_y, v_lru_w_y)
    results["lru_w_in"] = reduce_update("lru_w_in", l_lru_big, 1, lru_w_in, m_lru_w_in, v_lru_w_in)
    results["lru_w_out"] = reduce_update("lru_w_out", l_lru_big, 2, lru_w_out, m_lru_w_out, v_lru_w_out)
    results["lru_w_a"] = reduce_update("lru_w_a", l_lru_small, 0, lru_w_a, m_lru_w_a, v_lru_w_a)
    results["lru_w_x"] = reduce_update("lru_w_x", l_lru_small, 1, lru_w_x, m_lru_w_x, v_lru_w_x)
    results["pool_w"] = reduce_update("pool_w", l_pool, 0, pool_w, m_pool_w, v_pool_w)

    def my_cols(full, width):
        return lax.dynamic_slice_in_dim(full, me * width, width, axis=full.ndim - 1)

    lru_rows = tot[K_LRUB:K_LRUB + 5 * n_lru].reshape(n_lru, 5, LRU_WIDTH)
    small_grads = {
        "b_mod": tot[K_MOD:K_MOD + DEPTH * N_MOD].reshape(DEPTH, N_MOD * D_MODEL),
        "norm_mix_g": tot[K_NMIX:K_NMIX + DEPTH],
        "norm_ffn_g": tot[K_NFFN:K_NFFN + DEPTH],
        "lru_b_y": lru_rows[:, 0], "lru_b_in": lru_rows[:, 1], "lru_conv_b": lru_rows[:, 2],
        "lru_lambda": lru_rows[:, 3], "lru_b_out": lru_rows[:, 4],
        "lru_conv_w": my_cols(tot[K_CONVW:K_CONVW + 4 * n_lru].reshape(n_lru, 4, LRU_WIDTH), shard),
        "lru_b_a": my_cols(tot[K_BA:K_BA + n_lru].reshape(n_lru, HEADS, HEAD_DIM), hshard),
        "lru_b_x": my_cols(tot[K_BX:K_BX + n_lru].reshape(n_lru, HEADS, HEAD_DIM), hshard),
        "pool_scale": my_cols(tot[K_PS:K_PS + n_lru], shard),
        "final_norm_g": tot[K_FIN],
    }
    given = dict(b_mod=(b_mod, m_b_mod, v_b_mod), norm_mix_g=(norm_mix_g, m_norm_mix_g, v_norm_mix_g),
                 norm_ffn_g=(norm_ffn_g, m_norm_ffn_g, v_norm_ffn_g), lru_b_y=(lru_b_y, m_lru_b_y, v_lru_b_y),
                 lru_b_in=(lru_b_in, m_lru_b_in, v_lru_b_in), lru_conv_w=(lru_conv_w, m_lru_conv_w, v_lru_conv_w),
                 lru_conv_b=(lru_conv_b, m_lru_conv_b, v_lru_conv_b), lru_b_a=(lru_b_a, m_lru_b_a, v_lru_b_a),
                 lru_b_x=(lru_b_x, m_lru_b_x, v_lru_b_x), lru_lambda=(lru_lambda, m_lru_lambda, v_lru_lambda),
                 lru_b_out=(lru_b_out, m_lru_b_out, v_lru_b_out), pool_scale=(pool_scale, m_pool_scale, v_pool_scale),
                 final_norm_g=(final_norm_g, m_final_norm_g, v_final_norm_g))
    for name, g in small_grads.items():
        results[name] = (g,) + _adamw_small(name, g, *given[name])

    order = ["w_mod", "b_mod", "norm_mix_g", "norm_ffn_g", "lru_w_y", "lru_b_y", "lru_w_in", "lru_b_in", "lru_conv_w",
             "lru_conv_b", "lru_w_a", "lru_b_a", "lru_w_x", "lru_b_x", "lru_lambda", "lru_w_out", "lru_b_out", "pool_w",
             "pool_scale", "ffn_w1", "ffn_w2", "final_norm_g"]
    return (loss, grad_x, *[results[n][0] for n in order], *[results[n][1] for n in order],
            *[results[n][2] for n in order], *[results[n][3] for n in order])
```

```python
import functools
import math

import jax
import jax.numpy as jnp
from jax import lax
from jax.experimental import pallas as pl
from jax.experimental.pallas import tpu as pltpu

f32, bf16 = jnp.float32, jnp.bfloat16

D_MODEL = 1024
LRU_WIDTH = 1024
HEADS = 4
HEAD_DIM = 256
D_FF = 4096
DEPTH = 4
N_MOD = 6
N_DEV = 8
FF_CHUNK = D_FF // N_DEV
POOL_WINDOWS = (2, 4, 8, 16)
POOL_HALO = 16
EPS = 1e-6
LRU_C = 8.0

ADAM_LR = 0.001
ADAM_B1 = 0.9
ADAM_B2 = 0.999
ADAM_EPS = 1e-08
ADAM_WD = 0.01
ADAM_STEP = 10

V7X_VMEM_BYTES = 64 * 1024 * 1024
SUBLANES = 8
BF16_ROWS = 16

R_SH_M, R_SC_M, R_GT_M, R_SH_F, R_SC_F, R_GT_F, R_GS_M, R_GS_F = range(8)
P_BY, P_BIN, P_CONVB, P_BA, P_BX, P_LAM, P_BOUT, P_CW0 = 0, 1, 2, 3, 4, 5, 6, 8
G_SH, G_GS, G_GT, G_BY, G_BIN, G_CONVB, G_BA, G_BX, G_LS, G_BOUT, G_CW0 = 0, 1, 2, 3, 4, 5, 6, 7, 8, 9, 10
K_MOD, K_NMIX, K_NFFN, K_LRUB, K_CONVW, K_BA, K_BX, K_PS, K_FIN, K_ROWS = 0, 24, 28, 32, 42, 50, 52, 54, 56, 64


def _params(semantics=None, vmem_mb=48):
    return pltpu.CompilerParams(dimension_semantics=semantics, vmem_limit_bytes=vmem_mb * 1024 * 1024)


def _mm(a, b):
    return jnp.dot(a, b, preferred_element_type=f32)


def _mm_nt(a, b):
    return lax.dot_general(a, b, (((1,), (1,)), ((), ())), preferred_element_type=f32)


def _mm_tn(a, b):
    return lax.dot_general(a, b, (((0,), (0,)), ((), ())), preferred_element_type=f32)


def _rms(x):
    r = lax.rsqrt(jnp.mean(x * x, axis=-1, keepdims=True) + EPS)
    return x * r, r


def _norm_bwd(dh, n, r, gs):
    dn = dh * gs
    return r * (dn - n * jnp.mean(dn * n, axis=-1, keepdims=True))


def _colsum(v):
    return jnp.sum(v, axis=0, keepdims=True)


def _sigmoid(v):
    return 0.5 * jnp.tanh(0.5 * v) + 0.5


def _log_sigmoid(v):
    return jnp.minimum(v, 0.0) - jnp.log1p(jnp.exp(-jnp.abs(v)))


_GELU_C = 0.7978845608028654
_GELU_A = 0.044715


def _gelu_and_grad(v):
    v2 = v * v
    t = jnp.tanh(_GELU_C * v * (1.0 + _GELU_A * v2))
    p = 0.5 + 0.5 * t
    return v * p, p + (0.5 * v) * (1.0 - t * t) * (_GELU_C + (3.0 * _GELU_A * _GELU_C) * v2)


def _rows_before(halo, v, shifts):
    hr = halo.shape[0]
    ext = jnp.concatenate([halo, v], axis=0)
    return [pltpu.roll(ext, k, 0)[hr:] for k in shifts]


def _rows_after(v, halo, shifts):
    n = v.shape[0]
    ext = jnp.concatenate([v, halo], axis=0)
    return [pltpu.roll(ext, ext.shape[0] - k, 0)[:n] for k in shifts]


def _shift_matrix(n, halo_rows, shifts):
    rows = lax.broadcasted_iota(jnp.int32, (n, n + halo_rows), 0)
    cols = lax.broadcasted_iota(jnp.int32, (n, n + halo_rows), 1)
    return jnp.concatenate([(cols == rows + halo_rows - k).astype(bf16) for k in shifts], axis=0)


def _shifted_rows(sel, halo, v):
    n = v.shape[0]
    out = _mm(sel, jnp.concatenate([halo, v], axis=0))
    return [out[j * n:(j + 1) * n] for j in range(sel.shape[0] // n)]


def _block_diag(v, w_ref, kind):
    return jnp.concatenate(
        [_mm(v[:, h * HEAD_DIM:(h + 1) * HEAD_DIM], w_ref[kind, h]) for h in range(HEADS)], axis=1)


def _block_diag_t(v, w_ref, kind):
    return jnp.concatenate(
        [_mm_nt(v[:, h * HEAD_DIM:(h + 1) * HEAD_DIM], w_ref[kind, h]) for h in range(HEADS)], axis=1)


def _exchange(arrays, gather, name):
    n = len(arrays)
    peers = N_DEV - 1

    def body(*refs):
        ins, outs = refs[:n], refs[n:2 * n]
        send_sems, recv_sems, local_sems = refs[2 * n:]
        x, y, c = lax.axis_index("x"), lax.axis_index("y"), lax.axis_index("c")
        me = 4 * x + 2 * y + c
        local = []
        for k in range(n):
            cp = pltpu.make_async_copy(ins[k] if gather else ins[k].at[me], outs[k].at[me], local_sems.at[k])
            cp.start()
            local.append(cp)
        remote = []
        for p in range(1, N_DEV):
            px = 1 - x if p & 4 else x
            py = 1 - y if p & 2 else y
            pc = 1 - c if p & 1 else c
            for k in range(n):
                cp = pltpu.make_async_remote_copy(
                    src_ref=ins[k] if gather else ins[k].at[4 * px + 2 * py + pc],
                    dst_ref=outs[k].at[me],
                    send_sem=send_sems.at[k * peers + p - 1],
                    recv_sem=recv_sems.at[k * peers + p - 1],
                    device_id=(px, py, pc), device_id_type=pl.DeviceIdType.MESH)
                cp.start()
                remote.append(cp)
        for cp in remote:
            cp.wait()
        for cp in local:
            cp.wait()

    out_shape = tuple(
        jax.ShapeDtypeStruct(((N_DEV,) + a.shape) if gather else a.shape, a.dtype) for a in arrays)
    outs = pl.pallas_call(
        body, name=name, out_shape=out_shape,
        in_specs=[pl.BlockSpec(memory_space=pl.ANY)] * n,
        out_specs=tuple(pl.BlockSpec(memory_space=pl.ANY) for _ in range(n)),
        scratch_shapes=[pltpu.SemaphoreType.DMA((n * peers,)), pltpu.SemaphoreType.DMA((n * peers,)),
                        pltpu.SemaphoreType.DMA((n,))],
        compiler_params=pltpu.CompilerParams(has_side_effects=True),
    )(*arrays)
    return list(outs)


_HBM = pl.BlockSpec(memory_space=pltpu.HBM)
_SEM = pl.BlockSpec(memory_space=pltpu.SEMAPHORE)
_DATAFLOW = pltpu.SideEffectType.DATAFLOW_SIDE_EFFECTING


def _peer_copies(src_refs, land_refs, send_sems, recv_sems, gather):
    x, y, c = lax.axis_index("x"), lax.axis_index("y"), lax.axis_index("c")
    me = 4 * x + 2 * y + c
    peers = N_DEV - 1
    copies = []
    for p in range(1, N_DEV):
        px = 1 - x if p & 4 else x
        py = 1 - y if p & 2 else y
        pc = 1 - c if p & 1 else c
        for k in range(len(src_refs)):
            copies.append(pltpu.make_async_remote_copy(
                src_ref=src_refs[k] if gather else src_refs[k].at[4 * px + 2 * py + pc],
                dst_ref=land_refs[k].at[me],
                send_sem=send_sems.at[k * peers + p - 1], recv_sem=recv_sems.at[k * peers + p - 1],
                device_id=(px, py, pc), device_id_type=pl.DeviceIdType.MESH))
    return copies


def _landing(srcs, gather, me):
    out = []
    for a in srcs:
        own = a if gather else lax.dynamic_index_in_dim(a, me, 0, keepdims=False)
        out.append(lax.dynamic_update_index_in_dim(lax.empty((N_DEV,) + own.shape, own.dtype), own, me, 0))
    return out


def _send_start(name, groups, gather, me):
    sizes = [len(g) for g in groups]
    srcs = [a for g in groups for a in g]
    n = len(srcs)
    lands = _landing(srcs, gather, me)
    ng = len(groups)
    first = [sum(sizes[:g]) for g in range(ng)]

    def body(*refs):
        src_refs, land_refs = refs[:n], refs[n:2 * n]
        sems, token = refs[2 * n:2 * n + 2 * ng], refs[-1]
        for g in range(ng):
            part = slice(first[g], first[g] + sizes[g])
            for cp in _peer_copies(src_refs[part], land_refs[part], sems[2 * g], sems[2 * g + 1], gather):
                cp.start()
        token[...] = jnp.zeros_like(token)

    sem_shapes = [pltpu.SemaphoreType.DMA((sizes[g // 2] * (N_DEV - 1),)) for g in range(2 * ng)]
    outs = pl.pallas_call(
        body, name=name,
        out_shape=(*sem_shapes, *[pltpu.HBM(a.shape, a.dtype) for a in (*srcs, *lands)], jax.ShapeDtypeStruct((8, 128), f32)),
        in_specs=[_HBM] * (2 * n),
        out_specs=(*[_SEM] * (2 * ng), *[_HBM] * (2 * n), pl.BlockSpec(memory_space=pltpu.VMEM)),
        input_output_aliases={k: 2 * ng + k for k in range(2 * n)},
        compiler_params=pltpu.CompilerParams(has_side_effects=_DATAFLOW),
    )(*[pltpu.with_memory_space_constraint(a, pltpu.HBM) for a in (*srcs, *lands)])
    srcs_thru, lands_thru = outs[2 * ng:2 * ng + n], outs[2 * ng + n:2 * ng + 2 * n]
    handles = [(outs[2 * g], outs[2 * g + 1], list(srcs_thru[first[g]:first[g] + sizes[g]]),
                list(lands_thru[first[g]:first[g] + sizes[g]]), gather) for g in range(ng)]
    return handles, outs[-1]


def _send_wait(name, handle, after):
    send_sems, recv_sems, srcs, lands, gather = handle
    n = len(srcs)

    def body(*refs):
        src_refs, land_refs = refs[:n], refs[n:2 * n]
        for cp in _peer_copies(src_refs, land_refs, refs[2 * n], refs[2 * n + 1], gather):
            cp.wait_send()
            cp.wait_recv()
        refs[-1][...] = jnp.zeros_like(refs[-1])

    outs = pl.pallas_call(
        body, name=name,
        out_shape=(*[pltpu.HBM(a.shape, a.dtype) for a in (*srcs, *lands)], jax.ShapeDtypeStruct((8, 128), f32)),
        in_specs=[_HBM] * (2 * n) + [_SEM, _SEM, pl.BlockSpec(memory_space=pl.ANY)],
        out_specs=(*[_HBM] * (2 * n), pl.BlockSpec(memory_space=pltpu.VMEM)),
        input_output_aliases={k: k for k in range(2 * n)},
        compiler_params=pltpu.CompilerParams(has_side_effects=_DATAFLOW),
    )(*srcs, *lands, send_sems, recv_sems, after)
    return list(outs[n:2 * n]), outs[-1]


def _mod_part(c_all, w_mod):
    depth, d, cols = w_mod.shape

    def body(c_ref, w_ref, o_ref):
        cv = c_ref[...]
        cond = cv * _sigmoid(cv)
        o_ref[...] = jnp.dot(cond, w_ref[...], preferred_element_type=f32, precision=lax.Precision.HIGHEST)

    return pl.pallas_call(
        body, name="mod_part", grid=(depth,),
        out_shape=jax.ShapeDtypeStruct((depth, N_DEV, cols), f32),
        in_specs=[pl.BlockSpec((N_DEV, d), lambda i: (0, 0)), pl.BlockSpec((None, d, cols), lambda i: (i, 0, 0))],
        out_specs=pl.BlockSpec((None, N_DEV, cols), lambda i: (i, 0, 0)),
        compiler_params=_params(("arbitrary",), 32),
    )(c_all, w_mod)


def _mod_table(mod_row, b_mod, g_mix, g_ffn):
    def body(m_ref, b_ref, gm_ref, gf_ref, o_ref, token_ref):
        for i in range(DEPTH):
            for k in range(N_MOD):
                o_ref[i, k:k + 1, :] = m_ref[i:i + 1, k * D_MODEL:(k + 1) * D_MODEL] + b_ref[i:i + 1, k * D_MODEL:(k + 1) * D_MODEL]
            o_ref[i, R_GS_M:R_GS_M + 1, :] = gm_ref[i:i + 1, :] * (1.0 + o_ref[i, R_SC_M:R_SC_M + 1, :])
            o_ref[i, R_GS_F:R_GS_F + 1, :] = gf_ref[i:i + 1, :] * (1.0 + o_ref[i, R_SC_F:R_SC_F + 1, :])
        token_ref[...] = jnp.zeros_like(token_ref)

    return pl.pallas_call(
        body, name="mod_table",
        out_shape=(jax.ShapeDtypeStruct((DEPTH, 8, D_MODEL), f32), jax.ShapeDtypeStruct((8, 128), f32)))(
        mod_row, b_mod, g_mix, g_ffn)


def _ffn_tile(s):
    return min(512, s)


def _layer_weights(shape):
    return pl.BlockSpec((N_DEV,) + shape, lambda i: (0, 0, 0))


def _ffn_fwd(x, vec, w1g, w2g, layer):
    s = x.shape[0]
    ts = _ffn_tile(s)

    def body(x_ref, vec_ref, w1_ref, w2_ref, xo_ref, u_ref, y_ref, hb_ref):
        xv = x_ref[...]
        n, _ = _rms(xv)
        hb = (n * vec_ref[R_GS_F:R_GS_F + 1, :] + vec_ref[R_SH_F:R_SH_F + 1, :]).astype(bf16)
        hb_ref[...] = hb
        yv = jnp.zeros((ts, D_MODEL), f32)
        for f in range(N_DEV):
            u = jnp.maximum(_mm(hb, w1_ref[f]), 0.0)
            u_ref[:, f * FF_CHUNK:(f + 1) * FF_CHUNK] = u.astype(bf16)
            yv = yv + _mm((u * u).astype(bf16), w2_ref[f])
        y_ref[...] = yv.astype(bf16)
        xo_ref[...] = xv + vec_ref[R_GT_F:R_GT_F + 1, :] * yv

    row = pl.BlockSpec((ts, D_MODEL), lambda i: (i, 0))
    return pl.pallas_call(
        body, name=f"ffn_fwd_{layer}", grid=(s // ts,),
        out_shape=(jax.ShapeDtypeStruct((s, D_MODEL), f32), jax.ShapeDtypeStruct((s, D_FF), bf16),
                   jax.ShapeDtypeStruct((s, D_MODEL), bf16), jax.ShapeDtypeStruct((s, D_MODEL), bf16)),
        in_specs=[row, pl.BlockSpec((8, D_MODEL), lambda i: (0, 0)),
                  _layer_weights((D_MODEL, FF_CHUNK)), _layer_weights((FF_CHUNK, D_MODEL))],
        out_specs=(row, pl.BlockSpec((ts, D_FF), lambda i: (i, 0)), row, row),
        compiler_params=_params(("arbitrary",), 56),
    )(x, vec, w1g, w2g)


def _ffn_bwd_act(x, dx, u, y, vec, w1g, w2g, layer):
    s = x.shape[0]
    ts = _ffn_tile(s)

    def body(x_ref, dx_ref, u_ref, y_ref, vec_ref, w1_ref, w2_ref, dxo_ref, da_ref, dyb_ref, sm_ref):
        @pl.when(pl.program_id(0) == 0)
        def _():
            sm_ref[...] = jnp.zeros_like(sm_ref)

        dxv = dx_ref[...]
        dyb = (dxv * vec_ref[R_GT_F:R_GT_F + 1, :]).astype(bf16)
        dyb_ref[...] = dyb
        sm_ref[G_GT:G_GT + 1, :] += _colsum(dxv * y_ref[...].astype(f32))
        dh = jnp.zeros((ts, D_MODEL), f32)
        for f in range(N_DEV):
            cols = slice(f * FF_CHUNK, (f + 1) * FF_CHUNK)
            dz = _mm_nt(dyb, w2_ref[f])
            dab = (dz * (2.0 * u_ref[:, cols].astype(f32))).astype(bf16)
            da_ref[:, cols] = dab
            dh = dh + _mm_nt(dab, w1_ref[f])
        n, r = _rms(x_ref[...])
        sm_ref[G_SH:G_SH + 1, :] += _colsum(dh)
        sm_ref[G_GS:G_GS + 1, :] += _colsum(dh * n)
        dxo_ref[...] = dxv + _norm_bwd(dh, n, r, vec_ref[R_GS_F:R_GS_F + 1, :])

    row = pl.BlockSpec((ts, D_MODEL), lambda i: (i, 0))
    wide = pl.BlockSpec((ts, D_FF), lambda i: (i, 0))
    return pl.pallas_call(
        body, name=f"ffn_bwd_act_{layer}", grid=(s // ts,),
        out_shape=(jax.ShapeDtypeStruct((s, D_MODEL), f32), jax.ShapeDtypeStruct((s, D_FF), bf16),
                   jax.ShapeDtypeStruct((s, D_MODEL), bf16), jax.ShapeDtypeStruct((8, D_MODEL), f32)),
        in_specs=[row, row, wide, row, pl.BlockSpec((8, D_MODEL), lambda i: (0, 0)),
                  _layer_weights((D_MODEL, FF_CHUNK)), _layer_weights((FF_CHUNK, D_MODEL))],
        out_specs=(row, wide, row, pl.BlockSpec((8, D_MODEL), lambda i: (0, 0))),
        compiler_params=_params(("arbitrary",), 58),
    )(x, dx, u, y, vec, w1g, w2g)


def _ffn_bwd_w1(hb, da, layer):
    s = hb.shape[0]
    ts = _ffn_tile(s)
    nt = s // ts

    def body(hb_ref, da_ref, dw_ref, acc_ref):
        i = pl.program_id(0)

        @pl.when(i == 0)
        def _():
            acc_ref[...] = jnp.zeros_like(acc_ref)

        hb = hb_ref[...]
        for f in range(N_DEV):
            acc_ref[f] += _mm_tn(hb, da_ref[:, f * FF_CHUNK:(f + 1) * FF_CHUNK])

        @pl.when(i == nt - 1)
        def _():
            dw_ref[...] = acc_ref[...].astype(bf16)

    return pl.pallas_call(
        body, name=f"ffn_bwd_w1_{layer}", grid=(nt,),
        out_shape=jax.ShapeDtypeStruct((N_DEV, D_MODEL, FF_CHUNK), bf16),
        in_specs=[pl.BlockSpec((ts, D_MODEL), lambda i: (i, 0)), pl.BlockSpec((ts, D_FF), lambda i: (i, 0))],
        out_specs=pl.BlockSpec((N_DEV, D_MODEL, FF_CHUNK), lambda i: (0, 0, 0)),
        scratch_shapes=[pltpu.VMEM((N_DEV, D_MODEL, FF_CHUNK), f32)],
        compiler_params=_params(("arbitrary",), 56),
    )(hb, da)


def _ffn_bwd_w2(u, dyb, layer):
    s = u.shape[0]
    ts = _ffn_tile(s)
    nt = s // ts

    def body(u_ref, dyb_ref, dw_ref, acc_ref):
        i = pl.program_id(0)

        @pl.when(i == 0)
        def _():
            acc_ref[...] = jnp.zeros_like(acc_ref)

        dyb = dyb_ref[...]
        for f in range(N_DEV):
            uv = u_ref[:, f * FF_CHUNK:(f + 1) * FF_CHUNK].astype(f32)
            acc_ref[f] += _mm_tn((uv * uv).astype(bf16), dyb)

        @pl.when(i == nt - 1)
        def _():
            dw_ref[...] = acc_ref[...].astype(bf16)

    return pl.pallas_call(
        body, name=f"ffn_bwd_w2_{layer}", grid=(nt,),
        out_shape=jax.ShapeDtypeStruct((N_DEV, FF_CHUNK, D_MODEL), bf16),
        in_specs=[pl.BlockSpec((ts, D_FF), lambda i: (i, 0)), pl.BlockSpec((ts, D_MODEL), lambda i: (i, 0))],
        out_specs=pl.BlockSpec((N_DEV, FF_CHUNK, D_MODEL), lambda i: (0, 0, 0)),
        scratch_shapes=[pltpu.VMEM((N_DEV, FF_CHUNK, D_MODEL), f32)],
        compiler_params=_params(("arbitrary",), 56),
    )(u, dyb)


def _lru_gates(xc, wsm_ref, pv_ref):
    xcb = xc.astype(bf16)
    gr = _sigmoid(_block_diag(xcb, wsm_ref, 0) + pv_ref[P_BA:P_BA + 1, :])
    gi = _sigmoid(_block_diag(xcb, wsm_ref, 1) + pv_ref[P_BX:P_BX + 1, :])
    log_a = (LRU_C * _log_sigmoid(pv_ref[P_LAM:P_LAM + 1, :])) * gr
    t = jnp.tanh(log_a)
    return gr, gi, jnp.exp(log_a), jnp.sqrt((-2.0 * t) / (1.0 - t))


def _conv(xr, taps_before, pv_ref):
    xc = xr * pv_ref[P_CW0 + 3:P_CW0 + 4, :] + pv_ref[P_CONVB:P_CONVB + 1, :]
    for k, v in zip((2, 1, 0), taps_before):
        xc = xc + v * pv_ref[P_CW0 + k:P_CW0 + k + 1, :]
    return xc


LRU_FWD_SUB, LRU_FWD_SUBS = 128, 2
LRU_BWD_SUB, LRU_BWD_SUBS = 256, 1


def _scan_rows(a, u, carry, reverse):
    groups = a.shape[0] // SUBLANES
    row = lax.broadcasted_iota(jnp.int32, (SUBLANES, a.shape[1]), 0)
    outs = [None] * groups
    for j in range(groups):
        g = groups - 1 - j if reverse else j
        av, uv = a[g * SUBLANES:(g + 1) * SUBLANES], u[g * SUBLANES:(g + 1) * SUBLANES]
        for k in (1, 2, 4):
            if reverse:
                valid, shift = row < SUBLANES - k, SUBLANES - k
            else:
                valid, shift = row >= k, k
            a_s = jnp.where(valid, pltpu.roll(av, shift, 0), 1.0)
            u_s = jnp.where(valid, pltpu.roll(uv, shift, 0), 0.0)
            uv = uv + av * u_s
            av = av * a_s
        h = uv + av * carry
        outs[g] = h
        carry = h[0:1, :] if reverse else h[SUBLANES - 1:SUBLANES, :]
    return jnp.concatenate(outs, axis=0), carry


def _lru_fwd(x, vec, wbig, wsm, pvec, layer):
    s = x.shape[0]
    sub = min(LRU_FWD_SUB, s)
    ts = min(sub * LRU_FWD_SUBS, s)
    nsub = ts // sub
    w = LRU_WIDTH

    def body(x_ref, vec_ref, wb_ref, wsm_ref, pv_ref, xo_ref, xr_ref, hs_ref, a_ref, mult_ref, gr_ref, gi_ref,
             gel_ref, geld_ref, y_ref, tail_ref, carry_ref):
        @pl.when(pl.program_id(0) == 0)
        def _():
            tail_ref[...] = jnp.zeros_like(tail_ref)
            carry_ref[...] = jnp.zeros_like(carry_ref)

        sel = _shift_matrix(sub, BF16_ROWS, (1, 2, 3))
        for k in range(nsub):
            rows = slice(k * sub, (k + 1) * sub)
            xv = x_ref[rows, :]
            n, _ = _rms(xv)
            hb = (n * vec_ref[R_GS_M:R_GS_M + 1, :] + vec_ref[R_SH_M:R_SH_M + 1, :]).astype(bf16)
            gelu_v, gelu_d = _gelu_and_grad(_mm(hb, wb_ref[0]) + pv_ref[P_BY:P_BY + 1, :])
            gel_ref[rows, :] = gelu_v.astype(bf16)
            geld_ref[rows, :] = gelu_d.astype(bf16)
            xrb = (_mm(hb, wb_ref[1]) + pv_ref[P_BIN:P_BIN + 1, :]).astype(bf16)
            xr_ref[rows, :] = xrb
            xc = _conv(xrb.astype(f32), _shifted_rows(sel, tail_ref[...], xrb), pv_ref)
            tail_ref[...] = xrb[sub - BF16_ROWS:, :]
            gr, gi, a, mult = _lru_gates(xc, wsm_ref, pv_ref)
            gr_ref[rows, :] = gr.astype(bf16)
            gi_ref[rows, :] = gi.astype(bf16)
            a_ref[rows, :] = a
            mult_ref[rows, :] = mult
            hs, carry = _scan_rows(a, mult * (gi * xc), carry_ref[0:1, :], reverse=False)
            carry_ref[0:1, :] = carry
            hs_ref[rows, :] = hs
            yv = _mm((hs * gelu_v).astype(bf16), wb_ref[2]) + pv_ref[P_BOUT:P_BOUT + 1, :]
            y_ref[rows, :] = yv.astype(bf16)
            xo_ref[rows, :] = xv + vec_ref[R_GT_M:R_GT_M + 1, :] * yv

    row = pl.BlockSpec((ts, D_MODEL), lambda i: (i, 0))
    roww = pl.BlockSpec((ts, w), lambda i: (i, 0))
    wide = lambda dt: jax.ShapeDtypeStruct((s, w), dt)
    return pl.pallas_call(
        body, name=f"lru_fwd_{layer}", grid=(s // ts,),
        out_shape=(jax.ShapeDtypeStruct((s, D_MODEL), f32), wide(bf16), wide(f32), wide(f32), wide(f32),
                   wide(bf16), wide(bf16), wide(bf16), wide(bf16), jax.ShapeDtypeStruct((s, D_MODEL), bf16)),
        in_specs=[row, pl.BlockSpec((8, D_MODEL), lambda i: (0, 0)),
                  pl.BlockSpec((3, w, w), lambda i: (0, 0, 0)),
                  pl.BlockSpec((2, HEADS, HEAD_DIM, HEAD_DIM), lambda i: (0, 0, 0, 0)),
                  pl.BlockSpec((16, w), lambda i: (0, 0))],
        out_specs=(row, roww, roww, roww, roww, roww, roww, roww, roww, row),
        scratch_shapes=[pltpu.VMEM((BF16_ROWS, w), bf16), pltpu.VMEM((SUBLANES, w), f32)],
        compiler_params=_params(("arbitrary",)),
    )(x, vec, wbig, wsm, pvec)


def _lru_bwd(x, dx, saved, vec, wbig, wsm, pvec, layer):
    xr, hs, a_all, mult_all, gr_all, gi_all, gel_all, geld_all, y = saved
    s = x.shape[0]
    sub = min(LRU_BWD_SUB, s)
    ts = min(sub * LRU_BWD_SUBS, s)
    nsub = ts // sub
    nt = s // ts
    w = LRU_WIDTH
    shard = w // N_DEV
    hshard = HEAD_DIM // N_DEV

    def body(x_ref, dx_ref, xr_ref, xrh_ref, hs_ref, hsh_ref, a_ref, mult_ref, gr_ref, gi_ref, gel_ref, geld_ref,
             y_ref, vec_ref, wb_ref, wsm_ref, pv_ref,
             dxo_ref, dwb_ref, dwsm_ref, sm_ref, accb_ref, accs_ref, eps_ref, dxc8_ref,
             hb_scr, dgb_scr, dxrb_scr, mb_scr, dyb_scr, xcb_scr, drab_scr, drxb_scr):
        i = pl.program_id(0)
        first_tile = i == nt - 1

        @pl.when(i == 0)
        def _():
            accb_ref[...] = jnp.zeros_like(accb_ref)
            accs_ref[...] = jnp.zeros_like(accs_ref)
            sm_ref[...] = jnp.zeros_like(sm_ref)
            eps_ref[...] = jnp.zeros_like(eps_ref)
            dxc8_ref[...] = jnp.zeros_like(dxc8_ref)

        gs = vec_ref[R_GS_M:R_GS_M + 1, :]
        c_ls = LRU_C * _log_sigmoid(pv_ref[P_LAM:P_LAM + 1, :])
        for k in reversed(range(nsub)):
            rows = slice(k * sub, (k + 1) * sub)
            xv = x_ref[rows, :]
            dxv = dx_ref[rows, :]
            n, r = _rms(xv)
            hb_scr[rows, :] = (n * gs + vec_ref[R_SH_M:R_SH_M + 1, :]).astype(bf16)
            xrv = xr_ref[rows, :].astype(f32)
            hsv = hs_ref[rows, :]
            if k == 0:
                xr_halo = jnp.where(first_tile, 0.0, xrh_ref[...].astype(f32))
                hs_halo = jnp.where(first_tile, 0.0, hsh_ref[...])
            else:
                xr_halo = xr_ref[k * sub - BF16_ROWS:k * sub, :].astype(f32)
                hs_halo = hs_ref[k * sub - SUBLANES:k * sub, :]
            xs1, xs2, xs3 = _rows_before(xr_halo, xrv, (1, 2, 3))
            xc = _conv(xrv, (xs1, xs2, xs3), pv_ref)
            xcb_scr[rows, :] = xc.astype(bf16)
            a, mult = a_ref[rows, :], mult_ref[rows, :]
            gr, gi = gr_ref[rows, :].astype(f32), gi_ref[rows, :].astype(f32)
            gelu_v = gel_ref[rows, :].astype(f32)

            dy = dxv * vec_ref[R_GT_M:R_GT_M + 1, :]
            dyb = dy.astype(bf16)
            dyb_scr[rows, :] = dyb
            sm_ref[G_GT:G_GT + 1, :] += _colsum(dxv * y_ref[rows, :].astype(f32))
            sm_ref[G_BOUT:G_BOUT + 1, :] += _colsum(dy)
            mb_scr[rows, :] = (hsv * gelu_v).astype(bf16)
            dm = _mm_nt(dyb, wb_ref[2])
            dhs = dm * gelu_v
            dgpre = dm * hsv * geld_ref[rows, :].astype(f32)
            dgb = dgpre.astype(bf16)
            dgb_scr[rows, :] = dgb
            sm_ref[G_BY:G_BY + 1, :] += _colsum(dgpre)

            eps_in = eps_ref[0:1, :]
            eps, eps_out = _scan_rows(a, a * dhs, eps_in, reverse=True)
            eps_ref[0:1, :] = eps_out
            (eps_next,) = _rows_after(eps, jnp.broadcast_to(eps_in, (SUBLANES, w)), (1,))
            delta = dhs + eps_next
            (h_prev,) = _rows_before(hs_halo, hsv, (1,))
            dxi = delta * xc
            dgi = dxi * mult
            dla = (delta * h_prev) * a - (dxi * gi) * (a * a) / mult
            sm_ref[G_LS:G_LS + 1, :] += _colsum(dla * gr)
            dra = (dla * c_ls) * (gr - gr * gr)
            drx = dgi * (gi - gi * gi)
            drab, drxb = dra.astype(bf16), drx.astype(bf16)
            drab_scr[rows, :] = drab
            drxb_scr[rows, :] = drxb
            sm_ref[G_BA:G_BA + 1, :] += _colsum(dra)
            sm_ref[G_BX:G_BX + 1, :] += _colsum(drx)
            dxc = (delta * mult) * gi + _block_diag_t(drab, wsm_ref, 0) + _block_diag_t(drxb, wsm_ref, 1)

            sm_ref[G_CONVB:G_CONVB + 1, :] += _colsum(dxc)
            for kk, v in zip((3, 2, 1, 0), (xrv, xs1, xs2, xs3)):
                sm_ref[G_CW0 + kk:G_CW0 + kk + 1, :] += _colsum(dxc * v)
            ups = _rows_after(dxc, dxc8_ref[...], (1, 2, 3))
            dxc8_ref[...] = dxc[0:SUBLANES, :]
            dxr = dxc * pv_ref[P_CW0 + 3:P_CW0 + 4, :]
            for kk, v in zip((2, 1, 0), ups):
                dxr = dxr + v * pv_ref[P_CW0 + kk:P_CW0 + kk + 1, :]
            dxrb = dxr.astype(bf16)
            dxrb_scr[rows, :] = dxrb
            sm_ref[G_BIN:G_BIN + 1, :] += _colsum(dxr)
            dh = _mm_nt(dgb, wb_ref[0]) + _mm_nt(dxrb, wb_ref[1])
            sm_ref[G_SH:G_SH + 1, :] += _colsum(dh)
            sm_ref[G_GS:G_GS + 1, :] += _colsum(dh * n)
            dxo_ref[rows, :] = dxv + _norm_bwd(dh, n, r, gs)

        hb = hb_scr[...]
        accb_ref[0] += _mm_tn(hb, dgb_scr[...])
        accb_ref[1] += _mm_tn(hb, dxrb_scr[...])
        accb_ref[2] += _mm_tn(mb_scr[...], dyb_scr[...])
        for h in range(HEADS):
            cols = slice(h * HEAD_DIM, (h + 1) * HEAD_DIM)
            accs_ref[0, h] += _mm_tn(xcb_scr[:, cols], drab_scr[:, cols])
            accs_ref[1, h] += _mm_tn(xcb_scr[:, cols], drxb_scr[:, cols])

        @pl.when(i == nt - 1)
        def _():
            sm_ref[G_LS:G_LS + 1, :] = sm_ref[G_LS:G_LS + 1, :] * LRU_C
            for k in range(3):
                dwb_ref[:, k] = accb_ref[k].astype(bf16).reshape(N_DEV, shard, w)
            for k in range(2):
                for h in range(HEADS):
                    dwsm_ref[:, k, h] = accs_ref[k, h].astype(bf16).reshape(N_DEV, hshard, HEAD_DIM)

    rev = lambda i: (nt - 1 - i, 0)
    row = pl.BlockSpec((ts, D_MODEL), rev)
    roww = pl.BlockSpec((ts, w), rev)
    halo16 = pl.BlockSpec((BF16_ROWS, w), lambda i: (jnp.maximum((nt - 1 - i) * (ts // BF16_ROWS) - 1, 0), 0))
    halo8 = pl.BlockSpec((SUBLANES, w), lambda i: (jnp.maximum((nt - 1 - i) * (ts // SUBLANES) - 1, 0), 0))
    const = lambda *shape: pl.BlockSpec(shape, lambda i: (0,) * len(shape))
    operand = pltpu.VMEM((ts, w), bf16)
    return pl.pallas_call(
        body, name=f"lru_bwd_{layer}", grid=(nt,),
        out_shape=(jax.ShapeDtypeStruct((s, D_MODEL), f32),
                   jax.ShapeDtypeStruct((N_DEV, 3, shard, w), bf16),
                   jax.ShapeDtypeStruct((N_DEV, 2, HEADS, hshard, HEAD_DIM), bf16),
                   jax.ShapeDtypeStruct((16, w), f32)),
        in_specs=[row, row, roww, halo16, roww, halo8, roww, roww, roww, roww, roww, roww, row, const(8, D_MODEL),
                  const(3, w, w), const(2, HEADS, HEAD_DIM, HEAD_DIM), const(16, w)],
        out_specs=(row, const(N_DEV, 3, shard, w), const(N_DEV, 2, HEADS, hshard, HEAD_DIM), const(16, w)),
        scratch_shapes=[pltpu.VMEM((3, w, w), f32), pltpu.VMEM((2, HEADS, HEAD_DIM, HEAD_DIM), f32),
                        pltpu.VMEM((SUBLANES, w), f32), pltpu.VMEM((SUBLANES, w), f32)] + [operand] * 8,
        compiler_params=_params(("arbitrary",), 58),
    )(x, dx, xr, xr, hs, hs, a_all, mult_all, gr_all, gi_all, gel_all, geld_all, y, vec, wbig, wsm, pvec)


def _pool_tile(s):
    return min(256, s)


def _pool_counts(tile_index, ts):
    t = (tile_index * ts + lax.broadcasted_iota(jnp.int32, (ts, 1), 0) + 1).astype(f32)
    return [1.0 / jnp.minimum(t, float(win)) for win in POOL_WINDOWS]


def _pooled(h, halo, inv):
    ext = jnp.concatenate([halo, h], axis=0)
    out = []
    for g in range(len(POOL_WINDOWS)):
        acc = ext[:, g * HEAD_DIM:(g + 1) * HEAD_DIM]
        for step in range(g + 1):
            acc = acc + pltpu.roll(acc, 1 << step, 0)
        out.append(acc[POOL_HALO:] * inv[g] - h[:, g * HEAD_DIM:(g + 1) * HEAD_DIM])
    return out


def _pool_fwd(x, vec, pw, ps, layer):
    s = x.shape[0]
    ts = _pool_tile(s)

    def body(x_ref, vec_ref, pw_ref, ps_ref, xo_ref, y_ref, halo_ref):
        i = pl.program_id(0)

        @pl.when(i == 0)
        def _():
            halo_ref[...] = jnp.zeros_like(halo_ref)

        xv = x_ref[...]
        n, _ = _rms(xv)
        h = n * vec_ref[R_GS_M:R_GS_M + 1, :] + vec_ref[R_SH_M:R_SH_M + 1, :]
        pooled = _pooled(h, halo_ref[...], _pool_counts(i, ts))
        halo_ref[...] = h[ts - POOL_HALO:, :]
        mixed = jnp.concatenate([_mm(pooled[g].astype(bf16), pw_ref[g]) for g in range(HEADS)], axis=1)
        yv = mixed * ps_ref[0:1, :]
        y_ref[...] = yv.astype(bf16)
        xo_ref[...] = xv + vec_ref[R_GT_M:R_GT_M + 1, :] * yv

    row = pl.BlockSpec((ts, D_MODEL), lambda i: (i, 0))
    return pl.pallas_call(
        body, name=f"pool_fwd_{layer}", grid=(s // ts,),
        out_shape=(jax.ShapeDtypeStruct((s, D_MODEL), f32), jax.ShapeDtypeStruct((s, D_MODEL), bf16)),
        in_specs=[row, pl.BlockSpec((8, D_MODEL), lambda i: (0, 0)),
                  pl.BlockSpec((HEADS, HEAD_DIM, HEAD_DIM), lambda i: (0, 0, 0)),
                  pl.BlockSpec((8, D_MODEL), lambda i: (0, 0))],
        out_specs=(row, row),
        scratch_shapes=[pltpu.VMEM((POOL_HALO, D_MODEL), f32)],
        compiler_params=_params(("arbitrary",)),
    )(x, vec, pw, ps)


def _pool_bwd(x, dx, y, vec, pw, ps, layer):
    s = x.shape[0]
    ts = _pool_tile(s)
    nt = s // ts
    hshard = HEAD_DIM // N_DEV

    def body(x_ref, xh_ref, dx_ref, y_ref, vec_ref, pw_ref, ps_ref, dxo_ref, dpw_ref, sm_ref, acc_ref, q16_ref):
        i = pl.program_id(0)
        tile = nt - 1 - i

        @pl.when(i == 0)
        def _():
            acc_ref[...] = jnp.zeros_like(acc_ref)
            sm_ref[...] = jnp.zeros_like(sm_ref)
            q16_ref[...] = jnp.zeros_like(q16_ref)

        gs, sh = vec_ref[R_GS_M:R_GS_M + 1, :], vec_ref[R_SH_M:R_SH_M + 1, :]
        xv = x_ref[...]
        dxv = dx_ref[...]
        n, r = _rms(xv)
        h = n * gs + sh
        nh, _ = _rms(xh_ref[...])
        halo = jnp.where(tile == 0, 0.0, nh * gs + sh)
        inv = _pool_counts(tile, ts)
        pooled = _pooled(h, halo, inv)
        mixed = jnp.concatenate([_mm(pooled[g].astype(bf16), pw_ref[g]) for g in range(HEADS)], axis=1)

        dy = dxv * vec_ref[R_GT_M:R_GT_M + 1, :]
        sm_ref[G_GT:G_GT + 1, :] += _colsum(dxv * y_ref[...].astype(f32))
        sm_ref[3:4, :] += _colsum(dy * mixed)
        dmixed = (dy * ps_ref[0:1, :]).astype(bf16)
        dh_parts = []
        for g in range(HEADS):
            cols = slice(g * HEAD_DIM, (g + 1) * HEAD_DIM)
            acc_ref[g] += _mm_tn(pooled[g].astype(bf16), dmixed[:, cols])
            dpooled = _mm_nt(dmixed[:, cols], pw_ref[g])
            q = dpooled * inv[g]
            ext = jnp.concatenate([q, q16_ref[:, cols]], axis=0)
            q16_ref[:, cols] = q[0:POOL_HALO, :]
            for step in range(g + 1):
                ext = ext + pltpu.roll(ext, ext.shape[0] - (1 << step), 0)
            dh_parts.append(ext[:ts] - dpooled)
        dh = jnp.concatenate(dh_parts, axis=1)
        sm_ref[G_SH:G_SH + 1, :] += _colsum(dh)
        sm_ref[G_GS:G_GS + 1, :] += _colsum(dh * n)
        dxo_ref[...] = dxv + _norm_bwd(dh, n, r, gs)

        @pl.when(i == nt - 1)
        def _():
            for g in range(HEADS):
                dpw_ref[:, g] = acc_ref[g].astype(bf16).reshape(N_DEV, hshard, HEAD_DIM)

    rev = lambda i: (nt - 1 - i, 0)
    row = pl.BlockSpec((ts, D_MODEL), rev)
    halo16 = pl.BlockSpec((POOL_HALO, D_MODEL), lambda i: (jnp.maximum((nt - 1 - i) * (ts // POOL_HALO) - 1, 0), 0))
    const = lambda *shape: pl.BlockSpec(shape, lambda i: (0,) * len(shape))
    return pl.pallas_call(
        body, name=f"pool_bwd_{layer}", grid=(nt,),
        out_shape=(jax.ShapeDtypeStruct((s, D_MODEL), f32),
                   jax.ShapeDtypeStruct((N_DEV, HEADS, hshard, HEAD_DIM), bf16),
                   jax.ShapeDtypeStruct((8, D_MODEL), f32)),
        in_specs=[row, halo16, row, row, const(8, D_MODEL), const(HEADS, HEAD_DIM, HEAD_DIM), const(8, D_MODEL)],
        out_specs=(row, const(N_DEV, HEADS, hshard, HEAD_DIM), const(8, D_MODEL)),
        scratch_shapes=[pltpu.VMEM((HEADS, HEAD_DIM, HEAD_DIM), f32), pltpu.VMEM((POOL_HALO, D_MODEL), f32)],
        compiler_params=_params(("arbitrary",)),
    )(x, x, dx, y, vec, pw, ps)


def _final(x, target, g_fin):
    s = x.shape[0]
    ts = min(512, s)

    def body(x_ref, t_ref, g_ref, dx_ref, loss_ref, sm_ref):
        @pl.when(pl.program_id(0) == 0)
        def _():
            loss_ref[...] = jnp.zeros_like(loss_ref)
            sm_ref[...] = jnp.zeros_like(sm_ref)

        g = g_ref[0:1, :]
        n, r = _rms(x_ref[...])
        err = n * g - t_ref[...]
        loss_ref[...] += 0.5 * jnp.sum(jnp.mean(err * err, axis=-1, keepdims=True), axis=0, keepdims=True)
        dyv = err * (1.0 / D_MODEL)
        sm_ref[0:1, :] += _colsum(dyv * n)
        dx_ref[...] = _norm_bwd(dyv, n, r, g)

    row = pl.BlockSpec((ts, D_MODEL), lambda i: (i, 0))
    return pl.pallas_call(
        body, name="final_loss", grid=(s // ts,),
        out_shape=(jax.ShapeDtypeStruct((s, D_MODEL), f32), jax.ShapeDtypeStruct((8, 128), f32),
                   jax.ShapeDtypeStruct((8, D_MODEL), f32)),
        in_specs=[row, row, pl.BlockSpec((8, D_MODEL), lambda i: (0, 0))],
        out_specs=(row, pl.BlockSpec((8, 128), lambda i: (0, 0)), pl.BlockSpec((8, D_MODEL), lambda i: (0, 0))),
        compiler_params=_params(("arbitrary",)),
    )(x, target, g_fin)


def _small_pack(sm_ffn, sm_mix, sm_fin, table, g_mix, g_ffn, lam):
    def body(*refs):
        ffn, mix = refs[0:DEPTH], refs[DEPTH:2 * DEPTH]
        fin_ref, tab_ref, gm_ref, gf_ref, lam_ref, o_ref = refs[2 * DEPTH:]
        o_ref[...] = jnp.zeros_like(o_ref)
        for i in range(DEPTH):
            base = K_MOD + i * N_MOD
            o_ref[base + 0:base + 1, :] = mix[i][G_SH:G_SH + 1, :]
            o_ref[base + 1:base + 2, :] = mix[i][G_GS:G_GS + 1, :] * gm_ref[i:i + 1, :]
            o_ref[base + 2:base + 3, :] = mix[i][G_GT:G_GT + 1, :]
            o_ref[base + 3:base + 4, :] = ffn[i][G_SH:G_SH + 1, :]
            o_ref[base + 4:base + 5, :] = ffn[i][G_GS:G_GS + 1, :] * gf_ref[i:i + 1, :]
            o_ref[base + 5:base + 6, :] = ffn[i][G_GT:G_GT + 1, :]
            o_ref[K_NMIX + i:K_NMIX + i + 1, :] = mix[i][G_GS:G_GS + 1, :] * (1.0 + tab_ref[i, R_SC_M:R_SC_M + 1, :])
            o_ref[K_NFFN + i:K_NFFN + i + 1, :] = ffn[i][G_GS:G_GS + 1, :] * (1.0 + tab_ref[i, R_SC_F:R_SC_F + 1, :])
            j = i // 2
            if i % 2 == 0:
                for k, src in enumerate((G_BY, G_BIN, G_CONVB, None, G_BOUT)):
                    dst = K_LRUB + j * 5 + k
                    if src is None:
                        o_ref[dst:dst + 1, :] = mix[i][G_LS:G_LS + 1, :] * _sigmoid(-lam_ref[j:j + 1, :])
                    else:
                        o_ref[dst:dst + 1, :] = mix[i][src:src + 1, :]
                o_ref[K_CONVW + j * 4:K_CONVW + j * 4 + 4, :] = mix[i][G_CW0:G_CW0 + 4, :]
                o_ref[K_BA + j:K_BA + j + 1, :] = mix[i][G_BA:G_BA + 1, :]
                o_ref[K_BX + j:K_BX + j + 1, :] = mix[i][G_BX:G_BX + 1, :]
            else:
                o_ref[K_PS + j:K_PS + j + 1, :] = mix[i][3:4, :]
        o_ref[K_FIN:K_FIN + 1, :] = fin_ref[0:1, :]

    return pl.pallas_call(body, name="small_pack", out_shape=jax.ShapeDtypeStruct((K_ROWS, D_MODEL), f32))(
        *sm_ffn, *sm_mix, sm_fin, table, g_mix, g_ffn, lam)


def _small_sum(gathered):
    def body(g_ref, o_ref, token_ref):
        tot = g_ref[0]
        for src in range(1, N_DEV):
            tot = tot + g_ref[src]
        o_ref[...] = tot
        token_ref[...] = jnp.zeros_like(token_ref)

    return pl.pallas_call(
        body, name="small_sum",
        out_shape=(jax.ShapeDtypeStruct(gathered.shape[1:], f32), jax.ShapeDtypeStruct((8, 128), f32)))(gathered)


def _adamw_math(g, w, m, v):
    m = ADAM_B1 * m + (1.0 - ADAM_B1) * g
    v = ADAM_B2 * v + (1.0 - ADAM_B2) * (g * g)
    m_hat = m / (1.0 - ADAM_B1 ** ADAM_STEP)
    v_hat = v / (1.0 - ADAM_B2 ** ADAM_STEP)
    delta = -ADAM_LR * (m_hat / (jnp.sqrt(v_hat) + ADAM_EPS) + ADAM_WD * w)
    return delta, m, v


def _adamw_small(name, g, w, m, v):
    shape = w.shape
    two_d = (1, shape[0]) if len(shape) == 1 else (math.prod(shape[:-1]), shape[-1])

    def body(g_ref, w_ref, m_ref, v_ref, d_ref, mo_ref, vo_ref):
        d_ref[...], mo_ref[...], vo_ref[...] = _adamw_math(g_ref[...], w_ref[...], m_ref[...], v_ref[...])

    outs = pl.pallas_call(body, name=f"adamw_{name}", out_shape=tuple(jax.ShapeDtypeStruct(two_d, f32) for _ in range(3)))(
        *(t.reshape(two_d) for t in (g, w, m, v)))
    return tuple(t.reshape(shape) for t in outs)


def _block_rows(rows, cols):
    tr = max(SUBLANES, min(rows, (512 * 1024) // (4 * cols)))
    while rows % tr:
        tr //= 2
    return tr


def _adamw_reduce(name, landings, kind, w, m, v):
    nl = len(landings)
    rows, cols = landings[0].shape[2:]
    tr = _block_rows(rows, cols)
    per_layer = rows // tr

    def body(*refs):
        l_refs = refs[:nl]
        w_ref, m_ref, v_ref, g_ref, d_ref, mo_ref, vo_ref = refs[nl:]
        layer = pl.program_id(0)
        for k in range(nl):
            @pl.when(layer == k)
            def _(k=k):
                g = l_refs[k][0].astype(f32)
                for src in range(1, N_DEV):
                    g = g + l_refs[k][src].astype(f32)
                g_ref[...] = g
        d_ref[...], mo_ref[...], vo_ref[...] = _adamw_math(g_ref[...], w_ref[...], m_ref[...], v_ref[...])

    blk = pl.BlockSpec((tr, cols), lambda l, r: (l * per_layer + r, 0))
    land = [pl.BlockSpec((N_DEV, None, tr, cols), lambda l, r, k=k: (0, kind, jnp.where(l == k, r, 0), 0)) for k in range(nl)]
    return pl.pallas_call(
        body, name=f"adamw_{name}", grid=(nl, per_layer),
        out_shape=tuple(jax.ShapeDtypeStruct((nl * rows, cols), f32) for _ in range(4)),
        in_specs=land + [blk, blk, blk],
        out_specs=(blk, blk, blk, blk),
        compiler_params=_params(("arbitrary", "arbitrary"), 32),
    )(*landings, w, m, v)


def _adamw_w_mod(c_all, dmod_all, w, m, v):
    depth, d, cols = w.shape
    tr = 256

    def body(c_ref, dm_ref, w_ref, m_ref, v_ref, g_ref, d_ref, mo_ref, vo_ref):
        cv = c_ref[...]
        cond = cv * _sigmoid(cv)
        g = lax.dot_general(cond, dm_ref[...], (((0,), (0,)), ((), ())), preferred_element_type=f32,
                            precision=lax.Precision.HIGHEST)
        g_ref[...] = g
        d_ref[...], mo_ref[...], vo_ref[...] = _adamw_math(g, w_ref[...], m_ref[...], v_ref[...])

    blk = pl.BlockSpec((None, tr, cols), lambda i, r: (i, r, 0))
    return pl.pallas_call(
        body, name="adamw_w_mod", grid=(depth, d // tr),
        out_shape=tuple(jax.ShapeDtypeStruct(w.shape, f32) for _ in range(4)),
        in_specs=[pl.BlockSpec((N_DEV, tr), lambda i, r: (0, r)),
                  pl.BlockSpec((None, N_DEV, cols), lambda i, r: (i, 0, 0)), blk, blk, blk],
        out_specs=(blk, blk, blk, blk),
        compiler_params=_params(("arbitrary", "arbitrary"), 32),
    )(c_all, dmod_all, w, m, v)


def kernel(x, c, w_mod, b_mod, norm_mix_g, norm_ffn_g, lru_w_y, lru_b_y, lru_w_in, lru_b_in, lru_conv_w, lru_conv_b, lru_w_a, lru_b_a, lru_w_x, lru_b_x, lru_lambda, lru_w_out, lru_b_out, pool_w, pool_scale, ffn_w1, ffn_w2, final_norm_g, loss_target, m_w_mod, m_b_mod, m_norm_mix_g, m_norm_ffn_g, m_lru_w_y, m_lru_b_y, m_lru_w_in, m_lru_b_in, m_lru_conv_w, m_lru_conv_b, m_lru_w_a, m_lru_b_a, m_lru_w_x, m_lru_b_x, m_lru_lambda, m_lru_w_out, m_lru_b_out, m_pool_w, m_pool_scale, m_ffn_w1, m_ffn_w2, m_final_norm_g, v_w_mod, v_b_mod, v_norm_mix_g, v_norm_ffn_g, v_lru_w_y, v_lru_b_y, v_lru_w_in, v_lru_b_in, v_lru_conv_w, v_lru_conv_b, v_lru_w_a, v_lru_b_a, v_lru_w_x, v_lru_b_x, v_lru_lambda, v_lru_w_out, v_lru_b_out, v_pool_w, v_pool_scale, v_ffn_w1, v_ffn_w2, v_final_norm_g):
    me = 4 * lax.axis_index("x") + 2 * lax.axis_index("y") + lax.axis_index("c")
    n_lru = lru_w_y.shape[0]
    shard = LRU_WIDTH // N_DEV
    hshard = HEAD_DIM // N_DEV
    xs = x[0]
    target = loss_target[0]

    small_vecs = jnp.concatenate([
        lru_conv_w.reshape(n_lru * 4, shard), lru_b_a.reshape(n_lru, HEADS * hshard),
        lru_b_x.reshape(n_lru, HEADS * hshard), pool_scale, jnp.zeros((2, shard), f32)], axis=0)
    (first_mix,), token = _send_start("gather_first_start", [[
        jnp.stack([lru_w_y[0], lru_w_in[0], lru_w_out[0]]).astype(bf16),
        jnp.stack([lru_w_a[0], lru_w_x[0]]).astype(bf16)]], True, me)
    sv_g, c_g = _exchange([small_vecs + token[0, 0], c], True, "gather_cond")
    conv_w_full = sv_g[:, 0:8].reshape(N_DEV, n_lru, 4, shard).transpose(1, 2, 0, 3).reshape(n_lru, 4, LRU_WIDTH)
    b_a_full = sv_g[:, 8:10].reshape(N_DEV, n_lru, HEADS, hshard).transpose(1, 2, 0, 3).reshape(n_lru, LRU_WIDTH)
    b_x_full = sv_g[:, 10:12].reshape(N_DEV, n_lru, HEADS, hshard).transpose(1, 2, 0, 3).reshape(n_lru, LRU_WIDTH)
    ps_full = sv_g[:, 12:14].transpose(1, 0, 2).reshape(n_lru, D_MODEL)
    c_all = c_g.reshape(N_DEV, D_MODEL)

    (mod_g,) = _exchange([_mod_part(c_all, w_mod)], True, "gather_mod")
    mod_row = lax.dynamic_index_in_dim(mod_g, me, axis=2, keepdims=False)
    mod_row = mod_row.transpose(1, 0, 2).reshape(DEPTH, N_MOD * D_MODEL)
    table, token = _mod_table(mod_row, b_mod, norm_mix_g, norm_ffn_g)

    parts = []
    for i in range(DEPTH):
        j = i // 2
        if i > 0 and i % 2 == 0:
            parts.append([(jnp.stack([lru_w_y[j], lru_w_in[j], lru_w_out[j]]) + token[0, 0]).astype(bf16),
                          (jnp.stack([lru_w_a[j], lru_w_x[j]]) + token[0, 0]).astype(bf16)])
        elif i % 2 == 1:
            parts.append([(pool_w[j] + token[0, 0]).astype(bf16)])
        parts.append([(ffn_w1[i] + token[0, 0]).astype(bf16), (ffn_w2[i] + token[0, 0]).astype(bf16)])
    handles, token = _send_start("gather_rest_start", parts, True, me)
    h_ffn = [handles[0], handles[2], handles[4], handles[6]]
    h_mix = [first_mix, handles[1], handles[3], handles[5]]

    zero_row = jnp.zeros((1, LRU_WIDTH), f32)
    pvecs = [jnp.concatenate([lru_b_y[j:j + 1], lru_b_in[j:j + 1], lru_conv_b[j:j + 1], b_a_full[j:j + 1],
                              b_x_full[j:j + 1], lru_lambda[j:j + 1], lru_b_out[j:j + 1], zero_row,
                              conv_w_full[j], zero_row, zero_row, zero_row, zero_row], axis=0) for j in range(n_lru)]
    ps_rows = [jnp.concatenate([ps_full[j:j + 1], jnp.zeros((7, D_MODEL), f32)], axis=0) for j in range(n_lru)]

    saved = []
    ffn_w, mix_w = [], []
    h = xs
    for i in range(DEPTH):
        j = i // 2
        got, _ = _send_wait(f"gather_mix_wait_{i}", h_mix[i], h)
        if i % 2 == 0:
            mix_w.append((got[0].transpose(1, 0, 2, 3).reshape(3, LRU_WIDTH, LRU_WIDTH),
                          got[1].transpose(1, 2, 0, 3, 4).reshape(2, HEADS, HEAD_DIM, HEAD_DIM)))
            h_mid, *lru_saved = _lru_fwd(h, table[i] + token[0, 0], mix_w[i][0], mix_w[i][1], pvecs[j], i)
            mix_saved = (h, tuple(lru_saved))
        else:
            mix_w.append((got[0].transpose(1, 0, 2, 3).reshape(HEADS, HEAD_DIM, HEAD_DIM),))
            h_mid, y_mix = _pool_fwd(h, table[i], mix_w[i][0], ps_rows[j], i)
            mix_saved = (h, y_mix)
        ffn_w.append(_send_wait(f"gather_ffn_wait_{i}", h_ffn[i], h_mid)[0])
        h_out, u, y_ffn, hb = _ffn_fwd(h_mid, table[i], ffn_w[i][0], ffn_w[i][1], i)
        saved.append((mix_saved, (h_mid, u, y_ffn, hb)))
        h = h_out
    fin_rows = jnp.concatenate([final_norm_g[None, :], jnp.zeros((7, D_MODEL), f32)], axis=0)
    dx, loss_part, sm_fin = _final(h, target, fin_rows)
    loss = lax.psum(loss_part[0, 0], ("x", "y", "c"))

    sm_ffn, sm_mix = [None] * DEPTH, [None] * DEPTH
    x_ffn, x_mix = [None] * DEPTH, [None] * DEPTH
    token = jnp.zeros((8, 128), f32)
    last_mix = None
    for i in reversed(range(DEPTH)):
        j = i // 2
        mix_saved, (h_mid, u, y_ffn, hb) = saved[i]
        dx, da, dyb, sm_ffn[i] = _ffn_bwd_act(h_mid, dx, u, y_ffn, table[i] + token[0, 0], ffn_w[i][0], ffn_w[i][1], i)
        ffn_grads = [_ffn_bwd_w1(hb, da, i), _ffn_bwd_w2(u, dyb, i)]
        if last_mix is None:
            (x_ffn[i],), token = _send_start(f"grads_start_{i}", [ffn_grads], False, me)
        else:
            (x_mix[i + 1], x_ffn[i]), token = _send_start(f"grads_start_{i}", [last_mix, ffn_grads], False, me)
        if i % 2 == 0:
            h_in, lru_saved = mix_saved
            dx, dbig, dsmall, sm_mix[i] = _lru_bwd(
                h_in, dx, lru_saved, table[i] + token[0, 0], mix_w[i][0], mix_w[i][1], pvecs[j], i)
            last_mix = [dbig, dsmall]
        else:
            h_in, y_mix = mix_saved
            dx, dpool, sm = _pool_bwd(h_in, dx, y_mix, table[i] + token[0, 0], mix_w[i][0], ps_rows[j], i)
            sm_mix[i] = jnp.concatenate([sm, jnp.zeros((8, D_MODEL), f32)], axis=0)
            last_mix = [dpool]
    grad_x = dx[None]

    pack = _small_pack(sm_ffn, sm_mix, sm_fin, table + token[0, 0], norm_mix_g, norm_ffn_g, lru_lambda)
    (pack_g,) = _exchange([pack], True, "gather_small_grads")
    tot, token = _small_sum(pack_g)
    (x_mix[0],), _ = _send_start("grads_last_start", [[t + token[0, 0].astype(bf16) for t in last_mix]], False, me)
    cols = w_mod.shape[2]
    dmod_all = lax.dynamic_slice_in_dim(pack_g[:, K_MOD:K_MOD + DEPTH * N_MOD].reshape(N_DEV, DEPTH, N_MOD * D_MODEL),
                                        me * cols, cols, axis=2).transpose(1, 0, 2)
    results = {"w_mod": _adamw_w_mod(c_all, dmod_all, w_mod, m_w_mod, v_w_mod)}

    after = results["w_mod"][1]
    l_ffn = [_send_wait(f"grads_ffn_wait_{i}", x_ffn[i], after)[0] for i in reversed(range(DEPTH))][::-1]

    def reduce_update(name, landings, kind, w, m, v):
        rows = w.size // w.shape[-1]
        two_d = (rows, w.shape[-1])
        lands = [t.reshape(N_DEV, -1, rows // len(landings), w.shape[-1]) for t in landings]
        outs = _adamw_reduce(name, lands, kind, w.reshape(two_d), m.reshape(two_d), v.reshape(two_d))
        return tuple(t.reshape(w.shape) for t in outs)

    results["ffn_w1"] = reduce_update("ffn_w1", [t[0] for t in l_ffn], 0, ffn_w1, m_ffn_w1, v_ffn_w1)
    results["ffn_w2"] = reduce_update("ffn_w2", [t[1] for t in l_ffn], 0, ffn_w2, m_ffn_w2, v_ffn_w2)
    after = results["ffn_w2"][1]
    l_mix = [_send_wait(f"grads_mix_wait_{i}", x_mix[i], after)[0] for i in reversed(range(DEPTH))][::-1]
    l_lru_big = [l_mix[i][0] for i in range(0, DEPTH, 2)]
    l_lru_small = [l_mix[i][1] for i in range(0, DEPTH, 2)]
    l_pool = [l_mix[i][0] for i in range(1, DEPTH, 2)]
    results["lru_w_y"] = reduce_update("lru_w_y", l_lru_big, 0, lru_w_y, m_lru_w_y, v_lru_w_y)
    results["lru_w_in"] = reduce_update("lru_w_in", l_lru_big, 1, lru_w_in, m_lru_w_in, v_lru_w_in)
    results["lru_w_out"] = reduce_update("lru_w_out", l_lru_big, 2, lru_w_out, m_lru_w_out, v_lru_w_out)
    results["lru_w_a"] = reduce_update("lru_w_a", l_lru_small, 0, lru_w_a, m_lru_w_a, v_lru_w_a)
    results["lru_w_x"] = reduce_update("lru_w_x", l_lru_small, 1, lru_w_x, m_lru_w_x, v_lru_w_x)
    results["pool_w"] = reduce_update("pool_w", l_pool, 0, pool_w, m_pool_w, v_pool_w)

    def my_cols(full, width):
        return lax.dynamic_slice_in_dim(full, me * width, width, axis=full.ndim - 1)

    lru_rows = tot[K_LRUB:K_LRUB + 5 * n_lru].reshape(n_lru, 5, LRU_WIDTH)
    small_grads = {
        "b_mod": tot[K_MOD:K_MOD + DEPTH * N_MOD].reshape(DEPTH, N_MOD * D_MODEL),
        "norm_mix_g": tot[K_NMIX:K_NMIX + DEPTH],
        "norm_ffn_g": tot[K_NFFN:K_NFFN + DEPTH],
        "lru_b_y": lru_rows[:, 0], "lru_b_in": lru_rows[:, 1], "lru_conv_b": lru_rows[:, 2],
        "lru_lambda": lru_rows[:, 3], "lru_b_out": lru_rows[:, 4],
        "lru_conv_w": my_cols(tot[K_CONVW:K_CONVW + 4 * n_lru].reshape(n_lru, 4, LRU_WIDTH), shard),
        "lru_b_a": my_cols(tot[K_BA:K_BA + n_lru].reshape(n_lru, HEADS, HEAD_DIM), hshard),
        "lru_b_x": my_cols(tot[K_BX:K_BX + n_lru].reshape(n_lru, HEADS, HEAD_DIM), hshard),
        "pool_scale": my_cols(tot[K_PS:K_PS + n_lru], shard),
        "final_norm_g": tot[K_FIN],
    }
    given = dict(b_mod=(b_mod, m_b_mod, v_b_mod), norm_mix_g=(norm_mix_g, m_norm_mix_g, v_norm_mix_g),
                 norm_ffn_g=(norm_ffn_g, m_norm_ffn_g, v_norm_ffn_g), lru_b_y=(lru_b_y, m_lru_b_y, v_lru_b_y),
                 lru_b_in=(lru_b_in, m_lru_b_in, v_lru_b_in), lru_conv_w=(lru_conv_w, m_lru_conv_w, v_lru_conv_w),
                 lru_conv_b=(lru_conv_b, m_lru_conv_b, v_lru_conv_b), lru_b_a=(lru_b_a, m_lru_b_a, v_lru_b_a),
                 lru_b_x=(lru_b_x, m_lru_b_x, v_lru_b_x), lru_lambda=(lru_lambda, m_lru_lambda, v_lru_lambda),
                 lru_b_out=(lru_b_out, m_lru_b_out, v_lru_b_out), pool_scale=(pool_scale, m_pool_scale, v_pool_scale),
                 final_norm_g=(final_norm_g, m_final_norm_g, v_final_norm_g))
    for name, g in small_grads.items():
        results[name] = (g,) + _adamw_small(name, g, *given[name])

    order = ["w_mod", "b_mod", "norm_mix_g", "norm_ffn_g", "lru_w_y", "lru_b_y", "lru_w_in", "lru_b_in", "lru_conv_w",
             "lru_conv_b", "lru_w_a", "lru_b_a", "lru_w_x", "lru_b_x", "lru_lambda", "lru_w_out", "lru_b_out", "pool_w",
             "pool_scale", "ffn_w1", "ffn_w2", "final_norm_g"]
    return (loss, grad_x, *[results[n][0] for n in order], *[results[n][1] for n in order],
            *[results[n][2] for n in order], *[results[n][3] for n in order])
```

```python
import functools
import math

import jax
import jax.numpy as jnp
from jax import lax
from jax.experimental import pallas as pl
from jax.experimental.pallas import tpu as pltpu

f32, bf16 = jnp.float32, jnp.bfloat16

D_MODEL = 1024
LRU_WIDTH = 1024
HEADS = 4
HEAD_DIM = 256
D_FF = 4096
DEPTH = 4
N_MOD = 6
N_DEV = 8
FF_CHUNK = D_FF // N_DEV
POOL_WINDOWS = (2, 4, 8, 16)
POOL_HALO = 16
EPS = 1e-6
LRU_C = 8.0

ADAM_LR = 0.001
ADAM_B1 = 0.9
ADAM_B2 = 0.999
ADAM_EPS = 1e-08
ADAM_WD = 0.01
ADAM_STEP = 10

V7X_VMEM_BYTES = 64 * 1024 * 1024
SUBLANES = 8
BF16_ROWS = 16

R_SH_M, R_SC_M, R_GT_M, R_SH_F, R_SC_F, R_GT_F, R_GS_M, R_GS_F = range(8)
P_BY, P_BIN, P_CONVB, P_BA, P_BX, P_LAM, P_BOUT, P_CW0 = 0, 1, 2, 3, 4, 5, 6, 8
G_SH, G_GS, G_GT, G_BY, G_BIN, G_CONVB, G_BA, G_BX, G_LS, G_BOUT, G_CW0 = 0, 1, 2, 3, 4, 5, 6, 7, 8, 9, 10
K_MOD, K_NMIX, K_NFFN, K_LRUB, K_CONVW, K_BA, K_BX, K_PS, K_FIN, K_ROWS = 0, 24, 28, 32, 42, 50, 52, 54, 56, 64


def _params(semantics=None, vmem_mb=48):
    return pltpu.CompilerParams(dimension_semantics=semantics, vmem_limit_bytes=vmem_mb * 1024 * 1024)


def _mm(a, b):
    return jnp.dot(a, b, preferred_element_type=f32)


def _mm_nt(a, b):
    return lax.dot_general(a, b, (((1,), (1,)), ((), ())), preferred_element_type=f32)


def _mm_tn(a, b):
    return lax.dot_general(a, b, (((0,), (0,)), ((), ())), preferred_element_type=f32)


def _rms(x):
    r = lax.rsqrt(jnp.mean(x * x, axis=-1, keepdims=True) + EPS)
    return x * r, r


def _norm_bwd(dh, n, r, gs):
    dn = dh * gs
    return r * (dn - n * jnp.mean(dn * n, axis=-1, keepdims=True))


def _colsum(v):
    return jnp.sum(v, axis=0, keepdims=True)


def _sigmoid(v):
    return 0.5 * jnp.tanh(0.5 * v) + 0.5


def _log_sigmoid(v):
    return jnp.minimum(v, 0.0) - jnp.log1p(jnp.exp(-jnp.abs(v)))


_GELU_C = 0.7978845608028654
_GELU_A = 0.044715


def _gelu_and_grad(v):
    v2 = v * v
    t = jnp.tanh(_GELU_C * v * (1.0 + _GELU_A * v2))
    p = 0.5 + 0.5 * t
    return v * p, p + (0.5 * v) * (1.0 - t * t) * (_GELU_C + (3.0 * _GELU_A * _GELU_C) * v2)


def _rows_before(halo, v, shifts):
    hr = halo.shape[0]
    ext = jnp.concatenate([halo, v], axis=0)
    return [pltpu.roll(ext, k, 0)[hr:] for k in shifts]


def _rows_after(v, halo, shifts):
    n = v.shape[0]
    ext = jnp.concatenate([v, halo], axis=0)
    return [pltpu.roll(ext, ext.shape[0] - k, 0)[:n] for k in shifts]


def _shift_matrix(n, halo_rows, shifts):
    rows = lax.broadcasted_iota(jnp.int32, (n, n + halo_rows), 0)
    cols = lax.broadcasted_iota(jnp.int32, (n, n + halo_rows), 1)
    return jnp.concatenate([(cols == rows + halo_rows - k).astype(bf16) for k in shifts], axis=0)


def _shifted_rows(sel, halo, v):
    n = v.shape[0]
    out = _mm(sel, jnp.concatenate([halo, v], axis=0))
    return [out[j * n:(j + 1) * n] for j in range(sel.shape[0] // n)]


def _block_diag(v, w_ref, kind):
    return jnp.concatenate(
        [_mm(v[:, h * HEAD_DIM:(h + 1) * HEAD_DIM], w_ref[kind, h]) for h in range(HEADS)], axis=1)


def _block_diag_t(v, w_ref, kind):
    return jnp.concatenate(
        [_mm_nt(v[:, h * HEAD_DIM:(h + 1) * HEAD_DIM], w_ref[kind, h]) for h in range(HEADS)], axis=1)


def _exchange(arrays, gather, name, pieces=1):
    n = len(arrays)
    peers = N_DEV - 1

    def body(*refs):
        ins, outs = refs[:n], refs[n:2 * n]
        send_sems, recv_sems, local_sems = refs[2 * n:]
        x, y, c = lax.axis_index("x"), lax.axis_index("y"), lax.axis_index("c")
        me = 4 * x + 2 * y + c
        local = []
        for k in range(n):
            cp = pltpu.make_async_copy(ins[k] if gather else ins[k].at[me], outs[k].at[me], local_sems.at[k])
            cp.start()
            local.append(cp)
        remote = _peer_copies(ins, outs, send_sems, recv_sems, gather, pieces)
        for cp in remote:
            cp.start()
        for cp in remote:
            cp.wait()
        for cp in local:
            cp.wait()

    out_shape = tuple(
        jax.ShapeDtypeStruct(((N_DEV,) + a.shape) if gather else a.shape, a.dtype) for a in arrays)
    outs = pl.pallas_call(
        body, name=name, out_shape=out_shape,
        in_specs=[pl.BlockSpec(memory_space=pl.ANY)] * n,
        out_specs=tuple(pl.BlockSpec(memory_space=pl.ANY) for _ in range(n)),
        scratch_shapes=[pltpu.SemaphoreType.DMA((n * pieces * peers,)), pltpu.SemaphoreType.DMA((n * pieces * peers,)),
                        pltpu.SemaphoreType.DMA((n,))],
        compiler_params=pltpu.CompilerParams(has_side_effects=True),
    )(*arrays)
    return list(outs)


_HBM = pl.BlockSpec(memory_space=pltpu.HBM)
_SEM = pl.BlockSpec(memory_space=pltpu.SEMAPHORE)
_DATAFLOW = pltpu.SideEffectType.DATAFLOW_SIDE_EFFECTING


def _peer_copies(src_refs, land_refs, send_sems, recv_sems, gather, pieces=1):
    x, y, c = lax.axis_index("x"), lax.axis_index("y"), lax.axis_index("c")
    me = 4 * x + 2 * y + c
    peers = N_DEV - 1
    copies = []
    for p in range(1, N_DEV):
        px = 1 - x if p & 4 else x
        py = 1 - y if p & 2 else y
        pc = 1 - c if p & 1 else c
        for k in range(len(src_refs)):
            block = src_refs[k] if gather else src_refs[k].at[4 * px + 2 * py + pc]
            dst = land_refs[k].at[me]
            rows = block.shape[0] // pieces
            for r in range(pieces):
                part = pl.ds(r * rows, rows)
                sem = (k * pieces + r) * peers + p - 1
                copies.append(pltpu.make_async_remote_copy(
                    src_ref=block.at[part] if pieces > 1 else block, dst_ref=dst.at[part] if pieces > 1 else dst,
                    send_sem=send_sems.at[sem], recv_sem=recv_sems.at[sem],
                    device_id=(px, py, pc), device_id_type=pl.DeviceIdType.MESH))
    return copies


def _landing(srcs, gather, me):
    out = []
    for a in srcs:
        own = a if gather else lax.dynamic_index_in_dim(a, me, 0, keepdims=False)
        out.append(lax.dynamic_update_index_in_dim(lax.empty((N_DEV,) + own.shape, own.dtype), own, me, 0))
    return out


def _send_start(name, groups, gather, me, pieces=1):
    sizes = [len(g) for g in groups]
    srcs = [a for g in groups for a in g]
    n = len(srcs)
    lands = _landing(srcs, gather, me)
    ng = len(groups)
    first = [sum(sizes[:g]) for g in range(ng)]

    def body(*refs):
        src_refs, land_refs = refs[:n], refs[n:2 * n]
        sems, token = refs[2 * n:2 * n + 2 * ng], refs[-1]
        for g in range(ng):
            part = slice(first[g], first[g] + sizes[g])
            for cp in _peer_copies(src_refs[part], land_refs[part], sems[2 * g], sems[2 * g + 1], gather, pieces):
                cp.start()
        token[...] = jnp.zeros_like(token)

    sem_shapes = [pltpu.SemaphoreType.DMA((sizes[g // 2] * pieces * (N_DEV - 1),)) for g in range(2 * ng)]
    outs = pl.pallas_call(
        body, name=name,
        out_shape=(*sem_shapes, *[pltpu.HBM(a.shape, a.dtype) for a in (*srcs, *lands)], jax.ShapeDtypeStruct((8, 128), f32)),
        in_specs=[_HBM] * (2 * n),
        out_specs=(*[_SEM] * (2 * ng), *[_HBM] * (2 * n), pl.BlockSpec(memory_space=pltpu.VMEM)),
        input_output_aliases={k: 2 * ng + k for k in range(2 * n)},
        compiler_params=pltpu.CompilerParams(has_side_effects=_DATAFLOW),
    )(*[pltpu.with_memory_space_constraint(a, pltpu.HBM) for a in (*srcs, *lands)])
    srcs_thru, lands_thru = outs[2 * ng:2 * ng + n], outs[2 * ng + n:2 * ng + 2 * n]
    handles = [(outs[2 * g], outs[2 * g + 1], list(srcs_thru[first[g]:first[g] + sizes[g]]),
                list(lands_thru[first[g]:first[g] + sizes[g]]), gather, pieces) for g in range(ng)]
    return handles, outs[-1]


def _send_wait(name, handle, after):
    send_sems, recv_sems, srcs, lands, gather, pieces = handle
    n = len(srcs)

    def body(*refs):
        src_refs, land_refs = refs[:n], refs[n:2 * n]
        for cp in _peer_copies(src_refs, land_refs, refs[2 * n], refs[2 * n + 1], gather, pieces):
            cp.wait_send()
            cp.wait_recv()
        refs[-1][...] = jnp.zeros_like(refs[-1])

    outs = pl.pallas_call(
        body, name=name,
        out_shape=(*[pltpu.HBM(a.shape, a.dtype) for a in (*srcs, *lands)], jax.ShapeDtypeStruct((8, 128), f32)),
        in_specs=[_HBM] * (2 * n) + [_SEM, _SEM, pl.BlockSpec(memory_space=pl.ANY)],
        out_specs=(*[_HBM] * (2 * n), pl.BlockSpec(memory_space=pltpu.VMEM)),
        input_output_aliases={k: k for k in range(2 * n)},
        compiler_params=pltpu.CompilerParams(has_side_effects=_DATAFLOW),
    )(*srcs, *lands, send_sems, recv_sems, after)
    return list(outs[n:2 * n]), outs[-1]


def _mod_part(c_all, w_mod):
    depth, d, cols = w_mod.shape

    def body(c_ref, w_ref, o_ref):
        cv = c_ref[...]
        cond = cv * _sigmoid(cv)
        o_ref[...] = jnp.dot(cond, w_ref[...], preferred_element_type=f32, precision=lax.Precision.HIGHEST)

    return pl.pallas_call(
        body, name="mod_part", grid=(depth,),
        out_shape=jax.ShapeDtypeStruct((depth, N_DEV, cols), f32),
        in_specs=[pl.BlockSpec((N_DEV, d), lambda i: (0, 0)), pl.BlockSpec((None, d, cols), lambda i: (i, 0, 0))],
        out_specs=pl.BlockSpec((None, N_DEV, cols), lambda i: (i, 0, 0)),
        compiler_params=_params(("arbitrary",), 32),
    )(c_all, w_mod)


def _mod_table(mod_row, b_mod, g_mix, g_ffn):
    def body(m_ref, b_ref, gm_ref, gf_ref, o_ref, token_ref):
        for i in range(DEPTH):
            for k in range(N_MOD):
                o_ref[i, k:k + 1, :] = m_ref[i:i + 1, k * D_MODEL:(k + 1) * D_MODEL] + b_ref[i:i + 1, k * D_MODEL:(k + 1) * D_MODEL]
            o_ref[i, R_GS_M:R_GS_M + 1, :] = gm_ref[i:i + 1, :] * (1.0 + o_ref[i, R_SC_M:R_SC_M + 1, :])
            o_ref[i, R_GS_F:R_GS_F + 1, :] = gf_ref[i:i + 1, :] * (1.0 + o_ref[i, R_SC_F:R_SC_F + 1, :])
        token_ref[...] = jnp.zeros_like(token_ref)

    return pl.pallas_call(
        body, name="mod_table",
        out_shape=(jax.ShapeDtypeStruct((DEPTH, 8, D_MODEL), f32), jax.ShapeDtypeStruct((8, 128), f32)))(
        mod_row, b_mod, g_mix, g_ffn)


def _ffn_tile(s):
    return min(512, s)


def _layer_weights(shape):
    return pl.BlockSpec((N_DEV,) + shape, lambda i: (0, 0, 0))


def _ffn_fwd(x, vec, w1g, w2g, layer):
    s = x.shape[0]
    ts = _ffn_tile(s)

    def body(x_ref, vec_ref, w1_ref, w2_ref, xo_ref, u_ref, y_ref, hb_ref):
        xv = x_ref[...]
        n, _ = _rms(xv)
        hb = (n * vec_ref[R_GS_F:R_GS_F + 1, :] + vec_ref[R_SH_F:R_SH_F + 1, :]).astype(bf16)
        hb_ref[...] = hb
        yv = jnp.zeros((ts, D_MODEL), f32)
        for f in range(N_DEV):
            u = jnp.maximum(_mm(hb, w1_ref[f]), 0.0)
            u_ref[:, f * FF_CHUNK:(f + 1) * FF_CHUNK] = u.astype(bf16)
            yv = yv + _mm((u * u).astype(bf16), w2_ref[f])
        y_ref[...] = yv.astype(bf16)
        xo_ref[...] = xv + vec_ref[R_GT_F:R_GT_F + 1, :] * yv

    row = pl.BlockSpec((ts, D_MODEL), lambda i: (i, 0))
    return pl.pallas_call(
        body, name=f"ffn_fwd_{layer}", grid=(s // ts,),
        out_shape=(jax.ShapeDtypeStruct((s, D_MODEL), f32), jax.ShapeDtypeStruct((s, D_FF), bf16),
                   jax.ShapeDtypeStruct((s, D_MODEL), bf16), jax.ShapeDtypeStruct((s, D_MODEL), bf16)),
        in_specs=[row, pl.BlockSpec((8, D_MODEL), lambda i: (0, 0)),
                  _layer_weights((D_MODEL, FF_CHUNK)), _layer_weights((FF_CHUNK, D_MODEL))],
        out_specs=(row, pl.BlockSpec((ts, D_FF), lambda i: (i, 0)), row, row),
        compiler_params=_params(("arbitrary",), 56),
    )(x, vec, w1g, w2g)


def _ffn_bwd_act(x, dx, u, y, vec, w1g, w2g, layer):
    s = x.shape[0]
    ts = _ffn_tile(s)

    def body(x_ref, dx_ref, u_ref, y_ref, vec_ref, w1_ref, w2_ref, dxo_ref, da_ref, dyb_ref, sm_ref):
        @pl.when(pl.program_id(0) == 0)
        def _():
            sm_ref[...] = jnp.zeros_like(sm_ref)

        dxv = dx_ref[...]
        dyb = (dxv * vec_ref[R_GT_F:R_GT_F + 1, :]).astype(bf16)
        dyb_ref[...] = dyb
        sm_ref[G_GT:G_GT + 1, :] += _colsum(dxv * y_ref[...].astype(f32))
        dh = jnp.zeros((ts, D_MODEL), f32)
        for f in range(N_DEV):
            cols = slice(f * FF_CHUNK, (f + 1) * FF_CHUNK)
            dz = _mm_nt(dyb, w2_ref[f])
            dab = (dz * (2.0 * u_ref[:, cols].astype(f32))).astype(bf16)
            da_ref[:, cols] = dab
            dh = dh + _mm_nt(dab, w1_ref[f])
        n, r = _rms(x_ref[...])
        sm_ref[G_SH:G_SH + 1, :] += _colsum(dh)
        sm_ref[G_GS:G_GS + 1, :] += _colsum(dh * n)
        dxo_ref[...] = dxv + _norm_bwd(dh, n, r, vec_ref[R_GS_F:R_GS_F + 1, :])

    row = pl.BlockSpec((ts, D_MODEL), lambda i: (i, 0))
    wide = pl.BlockSpec((ts, D_FF), lambda i: (i, 0))
    return pl.pallas_call(
        body, name=f"ffn_bwd_act_{layer}", grid=(s // ts,),
        out_shape=(jax.ShapeDtypeStruct((s, D_MODEL), f32), jax.ShapeDtypeStruct((s, D_FF), bf16),
                   jax.ShapeDtypeStruct((s, D_MODEL), bf16), jax.ShapeDtypeStruct((8, D_MODEL), f32)),
        in_specs=[row, row, wide, row, pl.BlockSpec((8, D_MODEL), lambda i: (0, 0)),
                  _layer_weights((D_MODEL, FF_CHUNK)), _layer_weights((FF_CHUNK, D_MODEL))],
        out_specs=(row, wide, row, pl.BlockSpec((8, D_MODEL), lambda i: (0, 0))),
        compiler_params=_params(("arbitrary",), 58),
    )(x, dx, u, y, vec, w1g, w2g)


def _ffn_bwd_w1(hb, da, layer):
    s = hb.shape[0]
    ts = _ffn_tile(s)
    nt = s // ts

    def body(hb_ref, da_ref, dw_ref, acc_ref):
        i = pl.program_id(0)

        @pl.when(i == 0)
        def _():
            acc_ref[...] = jnp.zeros_like(acc_ref)

        hb = hb_ref[...]
        for f in range(N_DEV):
            acc_ref[f] += _mm_tn(hb, da_ref[:, f * FF_CHUNK:(f + 1) * FF_CHUNK])

        @pl.when(i == nt - 1)
        def _():
            dw_ref[...] = acc_ref[...].astype(bf16)

    return pl.pallas_call(
        body, name=f"ffn_bwd_w1_{layer}", grid=(nt,),
        out_shape=jax.ShapeDtypeStruct((N_DEV, D_MODEL, FF_CHUNK), bf16),
        in_specs=[pl.BlockSpec((ts, D_MODEL), lambda i: (i, 0)), pl.BlockSpec((ts, D_FF), lambda i: (i, 0))],
        out_specs=pl.BlockSpec((N_DEV, D_MODEL, FF_CHUNK), lambda i: (0, 0, 0)),
        scratch_shapes=[pltpu.VMEM((N_DEV, D_MODEL, FF_CHUNK), f32)],
        compiler_params=_params(("arbitrary",), 56),
    )(hb, da)


def _ffn_bwd_w2(u, dyb, layer):
    s = u.shape[0]
    ts = _ffn_tile(s)
    nt = s // ts

    def body(u_ref, dyb_ref, dw_ref, acc_ref):
        i = pl.program_id(0)

        @pl.when(i == 0)
        def _():
            acc_ref[...] = jnp.zeros_like(acc_ref)

        dyb = dyb_ref[...]
        for f in range(N_DEV):
            uv = u_ref[:, f * FF_CHUNK:(f + 1) * FF_CHUNK].astype(f32)
            acc_ref[f] += _mm_tn((uv * uv).astype(bf16), dyb)

        @pl.when(i == nt - 1)
        def _():
            dw_ref[...] = acc_ref[...].astype(bf16)

    return pl.pallas_call(
        body, name=f"ffn_bwd_w2_{layer}", grid=(nt,),
        out_shape=jax.ShapeDtypeStruct((N_DEV, FF_CHUNK, D_MODEL), bf16),
        in_specs=[pl.BlockSpec((ts, D_FF), lambda i: (i, 0)), pl.BlockSpec((ts, D_MODEL), lambda i: (i, 0))],
        out_specs=pl.BlockSpec((N_DEV, FF_CHUNK, D_MODEL), lambda i: (0, 0, 0)),
        scratch_shapes=[pltpu.VMEM((N_DEV, FF_CHUNK, D_MODEL), f32)],
        compiler_params=_params(("arbitrary",), 56),
    )(u, dyb)


def _lru_gates(xc, wsm_ref, pv_ref):
    xcb = xc.astype(bf16)
    gr = _sigmoid(_block_diag(xcb, wsm_ref, 0) + pv_ref[P_BA:P_BA + 1, :])
    gi = _sigmoid(_block_diag(xcb, wsm_ref, 1) + pv_ref[P_BX:P_BX + 1, :])
    log_a = (LRU_C * _log_sigmoid(pv_ref[P_LAM:P_LAM + 1, :])) * gr
    t = jnp.tanh(log_a)
    return gr, gi, jnp.exp(log_a), jnp.sqrt((-2.0 * t) / (1.0 - t))


def _conv(xr, taps_before, pv_ref):
    xc = xr * pv_ref[P_CW0 + 3:P_CW0 + 4, :] + pv_ref[P_CONVB:P_CONVB + 1, :]
    for k, v in zip((2, 1, 0), taps_before):
        xc = xc + v * pv_ref[P_CW0 + k:P_CW0 + k + 1, :]
    return xc


LRU_FWD_SUB, LRU_FWD_SUBS = 128, 2
LRU_BWD_SUB, LRU_BWD_SUBS = 256, 1


def _scan_rows(a, u, carry, reverse):
    groups = a.shape[0] // SUBLANES
    row = lax.broadcasted_iota(jnp.int32, (SUBLANES, a.shape[1]), 0)
    outs = [None] * groups
    for j in range(groups):
        g = groups - 1 - j if reverse else j
        av, uv = a[g * SUBLANES:(g + 1) * SUBLANES], u[g * SUBLANES:(g + 1) * SUBLANES]
        for k in (1, 2, 4):
            if reverse:
                valid, shift = row < SUBLANES - k, SUBLANES - k
            else:
                valid, shift = row >= k, k
            a_s = jnp.where(valid, pltpu.roll(av, shift, 0), 1.0)
            u_s = jnp.where(valid, pltpu.roll(uv, shift, 0), 0.0)
            uv = uv + av * u_s
            av = av * a_s
        h = uv + av * carry
        outs[g] = h
        carry = h[0:1, :] if reverse else h[SUBLANES - 1:SUBLANES, :]
    return jnp.concatenate(outs, axis=0), carry


def _lru_fwd(x, vec, wbig, wsm, pvec, layer):
    s = x.shape[0]
    sub = min(LRU_FWD_SUB, s)
    ts = min(sub * LRU_FWD_SUBS, s)
    nsub = ts // sub
    w = LRU_WIDTH

    def body(x_ref, vec_ref, wb_ref, wsm_ref, pv_ref, xo_ref, xr_ref, hs_ref, a_ref, mult_ref, gr_ref, gi_ref,
             gel_ref, geld_ref, y_ref, tail_ref, carry_ref):
        @pl.when(pl.program_id(0) == 0)
        def _():
            tail_ref[...] = jnp.zeros_like(tail_ref)
            carry_ref[...] = jnp.zeros_like(carry_ref)

        sel = _shift_matrix(sub, BF16_ROWS, (1, 2, 3))
        for k in range(nsub):
            rows = slice(k * sub, (k + 1) * sub)
            xv = x_ref[rows, :]
            n, _ = _rms(xv)
            hb = (n * vec_ref[R_GS_M:R_GS_M + 1, :] + vec_ref[R_SH_M:R_SH_M + 1, :]).astype(bf16)
            gelu_v, gelu_d = _gelu_and_grad(_mm(hb, wb_ref[0]) + pv_ref[P_BY:P_BY + 1, :])
            gel_ref[rows, :] = gelu_v.astype(bf16)
            geld_ref[rows, :] = gelu_d.astype(bf16)
            xrb = (_mm(hb, wb_ref[1]) + pv_ref[P_BIN:P_BIN + 1, :]).astype(bf16)
            xr_ref[rows, :] = xrb
            xc = _conv(xrb.astype(f32), _shifted_rows(sel, tail_ref[...], xrb), pv_ref)
            tail_ref[...] = xrb[sub - BF16_ROWS:, :]
            gr, gi, a, mult = _lru_gates(xc, wsm_ref, pv_ref)
            gr_ref[rows, :] = gr.astype(bf16)
            gi_ref[rows, :] = gi.astype(bf16)
            a_ref[rows, :] = a
            mult_ref[rows, :] = mult
            hs, carry = _scan_rows(a, mult * (gi * xc), carry_ref[0:1, :], reverse=False)
            carry_ref[0:1, :] = carry
            hs_ref[rows, :] = hs
            yv = _mm((hs * gelu_v).astype(bf16), wb_ref[2]) + pv_ref[P_BOUT:P_BOUT + 1, :]
            y_ref[rows, :] = yv.astype(bf16)
            xo_ref[rows, :] = xv + vec_ref[R_GT_M:R_GT_M + 1, :] * yv

    row = pl.BlockSpec((ts, D_MODEL), lambda i: (i, 0))
    roww = pl.BlockSpec((ts, w), lambda i: (i, 0))
    wide = lambda dt: jax.ShapeDtypeStruct((s, w), dt)
    return pl.pallas_call(
        body, name=f"lru_fwd_{layer}", grid=(s // ts,),
        out_shape=(jax.ShapeDtypeStruct((s, D_MODEL), f32), wide(bf16), wide(f32), wide(f32), wide(f32),
                   wide(bf16), wide(bf16), wide(bf16), wide(bf16), jax.ShapeDtypeStruct((s, D_MODEL), bf16)),
        in_specs=[row, pl.BlockSpec((8, D_MODEL), lambda i: (0, 0)),
                  pl.BlockSpec((3, w, w), lambda i: (0, 0, 0)),
                  pl.BlockSpec((2, HEADS, HEAD_DIM, HEAD_DIM), lambda i: (0, 0, 0, 0)),
                  pl.BlockSpec((16, w), lambda i: (0, 0))],
        out_specs=(row, roww, roww, roww, roww, roww, roww, roww, roww, row),
        scratch_shapes=[pltpu.VMEM((BF16_ROWS, w), bf16), pltpu.VMEM((SUBLANES, w), f32)],
        compiler_params=_params(("arbitrary",)),
    )(x, vec, wbig, wsm, pvec)


def _lru_bwd(x, dx, saved, vec, wbig, wsm, pvec, layer):
    xr, hs, a_all, mult_all, gr_all, gi_all, gel_all, geld_all, y = saved
    s = x.shape[0]
    sub = min(LRU_BWD_SUB, s)
    ts = min(sub * LRU_BWD_SUBS, s)
    nsub = ts // sub
    nt = s // ts
    w = LRU_WIDTH
    shard = w // N_DEV
    hshard = HEAD_DIM // N_DEV

    def body(x_ref, dx_ref, xr_ref, xrh_ref, hs_ref, hsh_ref, a_ref, mult_ref, gr_ref, gi_ref, gel_ref, geld_ref,
             y_ref, vec_ref, wb_ref, wsm_ref, pv_ref,
             dxo_ref, dwb_ref, dwsm_ref, sm_ref, accb_ref, accs_ref, eps_ref, dxc8_ref,
             hb_scr, dgb_scr, dxrb_scr, mb_scr, dyb_scr, xcb_scr, drab_scr, drxb_scr):
        i = pl.program_id(0)
        first_tile = i == nt - 1

        @pl.when(i == 0)
        def _():
            accb_ref[...] = jnp.zeros_like(accb_ref)
            accs_ref[...] = jnp.zeros_like(accs_ref)
            sm_ref[...] = jnp.zeros_like(sm_ref)
            eps_ref[...] = jnp.zeros_like(eps_ref)
            dxc8_ref[...] = jnp.zeros_like(dxc8_ref)

        gs = vec_ref[R_GS_M:R_GS_M + 1, :]
        c_ls = LRU_C * _log_sigmoid(pv_ref[P_LAM:P_LAM + 1, :])
        for k in reversed(range(nsub)):
            rows = slice(k * sub, (k + 1) * sub)
            xv = x_ref[rows, :]
            dxv = dx_ref[rows, :]
            n, r = _rms(xv)
            hb_scr[rows, :] = (n * gs + vec_ref[R_SH_M:R_SH_M + 1, :]).astype(bf16)
            xrv = xr_ref[rows, :].astype(f32)
            hsv = hs_ref[rows, :]
            if k == 0:
                xr_halo = jnp.where(first_tile, 0.0, xrh_ref[...].astype(f32))
                hs_halo = jnp.where(first_tile, 0.0, hsh_ref[...])
            else:
                xr_halo = xr_ref[k * sub - BF16_ROWS:k * sub, :].astype(f32)
                hs_halo = hs_ref[k * sub - SUBLANES:k * sub, :]
            xs1, xs2, xs3 = _rows_before(xr_halo, xrv, (1, 2, 3))
            xc = _conv(xrv, (xs1, xs2, xs3), pv_ref)
            xcb_scr[rows, :] = xc.astype(bf16)
            a, mult = a_ref[rows, :], mult_ref[rows, :]
            gr, gi = gr_ref[rows, :].astype(f32), gi_ref[rows, :].astype(f32)
            gelu_v = gel_ref[rows, :].astype(f32)

            dy = dxv * vec_ref[R_GT_M:R_GT_M + 1, :]
            dyb = dy.astype(bf16)
            dyb_scr[rows, :] = dyb
            sm_ref[G_GT:G_GT + 1, :] += _colsum(dxv * y_ref[rows, :].astype(f32))
            sm_ref[G_BOUT:G_BOUT + 1, :] += _colsum(dy)
            mb_scr[rows, :] = (hsv * gelu_v).astype(bf16)
            dm = _mm_nt(dyb, wb_ref[2])
            dhs = dm * gelu_v
            dgpre = dm * hsv * geld_ref[rows, :].astype(f32)
            dgb = dgpre.astype(bf16)
            dgb_scr[rows, :] = dgb
            sm_ref[G_BY:G_BY + 1, :] += _colsum(dgpre)

            eps_in = eps_ref[0:1, :]
            eps, eps_out = _scan_rows(a, a * dhs, eps_in, reverse=True)
            eps_ref[0:1, :] = eps_out
            (eps_next,) = _rows_after(eps, jnp.broadcast_to(eps_in, (SUBLANES, w)), (1,))
            delta = dhs + eps_next
            (h_prev,) = _rows_before(hs_halo, hsv, (1,))
            dxi = delta * xc
            dgi = dxi * mult
            dla = (delta * h_prev) * a - (dxi * gi) * (a * a) / mult
            sm_ref[G_LS:G_LS + 1, :] += _colsum(dla * gr)
            dra = (dla * c_ls) * (gr - gr * gr)
            drx = dgi * (gi - gi * gi)
            drab, drxb = dra.astype(bf16), drx.astype(bf16)
            drab_scr[rows, :] = drab
            drxb_scr[rows, :] = drxb
            sm_ref[G_BA:G_BA + 1, :] += _colsum(dra)
            sm_ref[G_BX:G_BX + 1, :] += _colsum(drx)
            dxc = (delta * mult) * gi + _block_diag_t(drab, wsm_ref, 0) + _block_diag_t(drxb, wsm_ref, 1)

            sm_ref[G_CONVB:G_CONVB + 1, :] += _colsum(dxc)
            for kk, v in zip((3, 2, 1, 0), (xrv, xs1, xs2, xs3)):
                sm_ref[G_CW0 + kk:G_CW0 + kk + 1, :] += _colsum(dxc * v)
            ups = _rows_after(dxc, dxc8_ref[...], (1, 2, 3))
            dxc8_ref[...] = dxc[0:SUBLANES, :]
            dxr = dxc * pv_ref[P_CW0 + 3:P_CW0 + 4, :]
            for kk, v in zip((2, 1, 0), ups):
                dxr = dxr + v * pv_ref[P_CW0 + kk:P_CW0 + kk + 1, :]
            dxrb = dxr.astype(bf16)
            dxrb_scr[rows, :] = dxrb
            sm_ref[G_BIN:G_BIN + 1, :] += _colsum(dxr)
            dh = _mm_nt(dgb, wb_ref[0]) + _mm_nt(dxrb, wb_ref[1])
            sm_ref[G_SH:G_SH + 1, :] += _colsum(dh)
            sm_ref[G_GS:G_GS + 1, :] += _colsum(dh * n)
            dxo_ref[rows, :] = dxv + _norm_bwd(dh, n, r, gs)

        hb = hb_scr[...]
        accb_ref[0] += _mm_tn(hb, dgb_scr[...])
        accb_ref[1] += _mm_tn(hb, dxrb_scr[...])
        accb_ref[2] += _mm_tn(mb_scr[...], dyb_scr[...])
        for h in range(HEADS):
            cols = slice(h * HEAD_DIM, (h + 1) * HEAD_DIM)
            accs_ref[0, h] += _mm_tn(xcb_scr[:, cols], drab_scr[:, cols])
            accs_ref[1, h] += _mm_tn(xcb_scr[:, cols], drxb_scr[:, cols])

        @pl.when(i == nt - 1)
        def _():
            sm_ref[G_LS:G_LS + 1, :] = sm_ref[G_LS:G_LS + 1, :] * LRU_C
            for k in range(3):
                dwb_ref[:, k] = accb_ref[k].astype(bf16).reshape(N_DEV, shard, w)
            for k in range(2):
                for h in range(HEADS):
                    dwsm_ref[:, k, h] = accs_ref[k, h].astype(bf16).reshape(N_DEV, hshard, HEAD_DIM)

    rev = lambda i: (nt - 1 - i, 0)
    row = pl.BlockSpec((ts, D_MODEL), rev)
    roww = pl.BlockSpec((ts, w), rev)
    halo16 = pl.BlockSpec((BF16_ROWS, w), lambda i: (jnp.maximum((nt - 1 - i) * (ts // BF16_ROWS) - 1, 0), 0))
    halo8 = pl.BlockSpec((SUBLANES, w), lambda i: (jnp.maximum((nt - 1 - i) * (ts // SUBLANES) - 1, 0), 0))
    const = lambda *shape: pl.BlockSpec(shape, lambda i: (0,) * len(shape))
    operand = pltpu.VMEM((ts, w), bf16)
    return pl.pallas_call(
        body, name=f"lru_bwd_{layer}", grid=(nt,),
        out_shape=(jax.ShapeDtypeStruct((s, D_MODEL), f32),
                   jax.ShapeDtypeStruct((N_DEV, 3, shard, w), bf16),
                   jax.ShapeDtypeStruct((N_DEV, 2, HEADS, hshard, HEAD_DIM), bf16),
                   jax.ShapeDtypeStruct((16, w), f32)),
        in_specs=[row, row, roww, halo16, roww, halo8, roww, roww, roww, roww, roww, roww, row, const(8, D_MODEL),
                  const(3, w, w), const(2, HEADS, HEAD_DIM, HEAD_DIM), const(16, w)],
        out_specs=(row, const(N_DEV, 3, shard, w), const(N_DEV, 2, HEADS, hshard, HEAD_DIM), const(16, w)),
        scratch_shapes=[pltpu.VMEM((3, w, w), f32), pltpu.VMEM((2, HEADS, HEAD_DIM, HEAD_DIM), f32),
                        pltpu.VMEM((SUBLANES, w), f32), pltpu.VMEM((SUBLANES, w), f32)] + [operand] * 8,
        compiler_params=_params(("arbitrary",), 58),
    )(x, dx, xr, xr, hs, hs, a_all, mult_all, gr_all, gi_all, gel_all, geld_all, y, vec, wbig, wsm, pvec)


def _pool_tile(s):
    return min(256, s)


def _pool_counts(tile_index, ts):
    t = (tile_index * ts + lax.broadcasted_iota(jnp.int32, (ts, 1), 0) + 1).astype(f32)
    return [1.0 / jnp.minimum(t, float(win)) for win in POOL_WINDOWS]


def _pooled(h, halo, inv):
    ext = jnp.concatenate([halo, h], axis=0)
    out = []
    for g in range(len(POOL_WINDOWS)):
        acc = ext[:, g * HEAD_DIM:(g + 1) * HEAD_DIM]
        for step in range(g + 1):
            acc = acc + pltpu.roll(acc, 1 << step, 0)
        out.append(acc[POOL_HALO:] * inv[g] - h[:, g * HEAD_DIM:(g + 1) * HEAD_DIM])
    return out


def _pool_fwd(x, vec, pw, ps, layer):
    s = x.shape[0]
    ts = _pool_tile(s)

    def body(x_ref, vec_ref, pw_ref, ps_ref, xo_ref, y_ref, halo_ref):
        i = pl.program_id(0)

        @pl.when(i == 0)
        def _():
            halo_ref[...] = jnp.zeros_like(halo_ref)

        xv = x_ref[...]
        n, _ = _rms(xv)
        h = n * vec_ref[R_GS_M:R_GS_M + 1, :] + vec_ref[R_SH_M:R_SH_M + 1, :]
        pooled = _pooled(h, halo_ref[...], _pool_counts(i, ts))
        halo_ref[...] = h[ts - POOL_HALO:, :]
        mixed = jnp.concatenate([_mm(pooled[g].astype(bf16), pw_ref[g]) for g in range(HEADS)], axis=1)
        yv = mixed * ps_ref[0:1, :]
        y_ref[...] = yv.astype(bf16)
        xo_ref[...] = xv + vec_ref[R_GT_M:R_GT_M + 1, :] * yv

    row = pl.BlockSpec((ts, D_MODEL), lambda i: (i, 0))
    return pl.pallas_call(
        body, name=f"pool_fwd_{layer}", grid=(s // ts,),
        out_shape=(jax.ShapeDtypeStruct((s, D_MODEL), f32), jax.ShapeDtypeStruct((s, D_MODEL), bf16)),
        in_specs=[row, pl.BlockSpec((8, D_MODEL), lambda i: (0, 0)),
                  pl.BlockSpec((HEADS, HEAD_DIM, HEAD_DIM), lambda i: (0, 0, 0)),
                  pl.BlockSpec((8, D_MODEL), lambda i: (0, 0))],
        out_specs=(row, row),
        scratch_shapes=[pltpu.VMEM((POOL_HALO, D_MODEL), f32)],
        compiler_params=_params(("arbitrary",)),
    )(x, vec, pw, ps)


def _pool_bwd(x, dx, y, vec, pw, ps, layer):
    s = x.shape[0]
    ts = _pool_tile(s)
    nt = s // ts
    hshard = HEAD_DIM // N_DEV

    def body(x_ref, xh_ref, dx_ref, y_ref, vec_ref, pw_ref, ps_ref, dxo_ref, dpw_ref, sm_ref, acc_ref, q16_ref):
        i = pl.program_id(0)
        tile = nt - 1 - i

        @pl.when(i == 0)
        def _():
            acc_ref[...] = jnp.zeros_like(acc_ref)
            sm_ref[...] = jnp.zeros_like(sm_ref)
            q16_ref[...] = jnp.zeros_like(q16_ref)

        gs, sh = vec_ref[R_GS_M:R_GS_M + 1, :], vec_ref[R_SH_M:R_SH_M + 1, :]
        xv = x_ref[...]
        dxv = dx_ref[...]
        n, r = _rms(xv)
        h = n * gs + sh
        nh, _ = _rms(xh_ref[...])
        halo = jnp.where(tile == 0, 0.0, nh * gs + sh)
        inv = _pool_counts(tile, ts)
        pooled = _pooled(h, halo, inv)
        mixed = jnp.concatenate([_mm(pooled[g].astype(bf16), pw_ref[g]) for g in range(HEADS)], axis=1)

        dy = dxv * vec_ref[R_GT_M:R_GT_M + 1, :]
        sm_ref[G_GT:G_GT + 1, :] += _colsum(dxv * y_ref[...].astype(f32))
        sm_ref[3:4, :] += _colsum(dy * mixed)
        dmixed = (dy * ps_ref[0:1, :]).astype(bf16)
        dh_parts = []
        for g in range(HEADS):
            cols = slice(g * HEAD_DIM, (g + 1) * HEAD_DIM)
            acc_ref[g] += _mm_tn(pooled[g].astype(bf16), dmixed[:, cols])
            dpooled = _mm_nt(dmixed[:, cols], pw_ref[g])
            q = dpooled * inv[g]
            ext = jnp.concatenate([q, q16_ref[:, cols]], axis=0)
            q16_ref[:, cols] = q[0:POOL_HALO, :]
            for step in range(g + 1):
                ext = ext + pltpu.roll(ext, ext.shape[0] - (1 << step), 0)
            dh_parts.append(ext[:ts] - dpooled)
        dh = jnp.concatenate(dh_parts, axis=1)
        sm_ref[G_SH:G_SH + 1, :] += _colsum(dh)
        sm_ref[G_GS:G_GS + 1, :] += _colsum(dh * n)
        dxo_ref[...] = dxv + _norm_bwd(dh, n, r, gs)

        @pl.when(i == nt - 1)
        def _():
            for g in range(HEADS):
                dpw_ref[:, g] = acc_ref[g].astype(bf16).reshape(N_DEV, hshard, HEAD_DIM)

    rev = lambda i: (nt - 1 - i, 0)
    row = pl.BlockSpec((ts, D_MODEL), rev)
    halo16 = pl.BlockSpec((POOL_HALO, D_MODEL), lambda i: (jnp.maximum((nt - 1 - i) * (ts // POOL_HALO) - 1, 0), 0))
    const = lambda *shape: pl.BlockSpec(shape, lambda i: (0,) * len(shape))
    return pl.pallas_call(
        body, name=f"pool_bwd_{layer}", grid=(nt,),
        out_shape=(jax.ShapeDtypeStruct((s, D_MODEL), f32),
                   jax.ShapeDtypeStruct((N_DEV, HEADS, hshard, HEAD_DIM), bf16),
                   jax.ShapeDtypeStruct((8, D_MODEL), f32)),
        in_specs=[row, halo16, row, row, const(8, D_MODEL), const(HEADS, HEAD_DIM, HEAD_DIM), const(8, D_MODEL)],
        out_specs=(row, const(N_DEV, HEADS, hshard, HEAD_DIM), const(8, D_MODEL)),
        scratch_shapes=[pltpu.VMEM((HEADS, HEAD_DIM, HEAD_DIM), f32), pltpu.VMEM((POOL_HALO, D_MODEL), f32)],
        compiler_params=_params(("arbitrary",)),
    )(x, x, dx, y, vec, pw, ps)


def _final(x, target, g_fin):
    s = x.shape[0]
    ts = min(512, s)

    def body(x_ref, t_ref, g_ref, dx_ref, loss_ref, sm_ref):
        @pl.when(pl.program_id(0) == 0)
        def _():
            loss_ref[...] = jnp.zeros_like(loss_ref)
            sm_ref[...] = jnp.zeros_like(sm_ref)

        g = g_ref[0:1, :]
        n, r = _rms(x_ref[...])
        err = n * g - t_ref[...]
        loss_ref[...] += 0.5 * jnp.sum(jnp.mean(err * err, axis=-1, keepdims=True), axis=0, keepdims=True)
        dyv = err * (1.0 / D_MODEL)
        sm_ref[0:1, :] += _colsum(dyv * n)
        dx_ref[...] = _norm_bwd(dyv, n, r, g)

    row = pl.BlockSpec((ts, D_MODEL), lambda i: (i, 0))
    return pl.pallas_call(
        body, name="final_loss", grid=(s // ts,),
        out_shape=(jax.ShapeDtypeStruct((s, D_MODEL), f32), jax.ShapeDtypeStruct((8, 128), f32),
                   jax.ShapeDtypeStruct((8, D_MODEL), f32)),
        in_specs=[row, row, pl.BlockSpec((8, D_MODEL), lambda i: (0, 0))],
        out_specs=(row, pl.BlockSpec((8, 128), lambda i: (0, 0)), pl.BlockSpec((8, D_MODEL), lambda i: (0, 0))),
        compiler_params=_params(("arbitrary",)),
    )(x, target, g_fin)


def _small_pack(sm_ffn, sm_mix, sm_fin, table, g_mix, g_ffn, lam):
    def body(*refs):
        ffn, mix = refs[0:DEPTH], refs[DEPTH:2 * DEPTH]
        fin_ref, tab_ref, gm_ref, gf_ref, lam_ref, o_ref = refs[2 * DEPTH:]
        o_ref[...] = jnp.zeros_like(o_ref)
        for i in range(DEPTH):
            base = K_MOD + i * N_MOD
            o_ref[base + 0:base + 1, :] = mix[i][G_SH:G_SH + 1, :]
            o_ref[base + 1:base + 2, :] = mix[i][G_GS:G_GS + 1, :] * gm_ref[i:i + 1, :]
            o_ref[base + 2:base + 3, :] = mix[i][G_GT:G_GT + 1, :]
            o_ref[base + 3:base + 4, :] = ffn[i][G_SH:G_SH + 1, :]
            o_ref[base + 4:base + 5, :] = ffn[i][G_GS:G_GS + 1, :] * gf_ref[i:i + 1, :]
            o_ref[base + 5:base + 6, :] = ffn[i][G_GT:G_GT + 1, :]
            o_ref[K_NMIX + i:K_NMIX + i + 1, :] = mix[i][G_GS:G_GS + 1, :] * (1.0 + tab_ref[i, R_SC_M:R_SC_M + 1, :])
            o_ref[K_NFFN + i:K_NFFN + i + 1, :] = ffn[i][G_GS:G_GS + 1, :] * (1.0 + tab_ref[i, R_SC_F:R_SC_F + 1, :])
            j = i // 2
            if i % 2 == 0:
                for k, src in enumerate((G_BY, G_BIN, G_CONVB, None, G_BOUT)):
                    dst = K_LRUB + j * 5 + k
                    if src is None:
                        o_ref[dst:dst + 1, :] = mix[i][G_LS:G_LS + 1, :] * _sigmoid(-lam_ref[j:j + 1, :])
                    else:
                        o_ref[dst:dst + 1, :] = mix[i][src:src + 1, :]
                o_ref[K_CONVW + j * 4:K_CONVW + j * 4 + 4, :] = mix[i][G_CW0:G_CW0 + 4, :]
                o_ref[K_BA + j:K_BA + j + 1, :] = mix[i][G_BA:G_BA + 1, :]
                o_ref[K_BX + j:K_BX + j + 1, :] = mix[i][G_BX:G_BX + 1, :]
            else:
                o_ref[K_PS + j:K_PS + j + 1, :] = mix[i][3:4, :]
        o_ref[K_FIN:K_FIN + 1, :] = fin_ref[0:1, :]

    return pl.pallas_call(body, name="small_pack", out_shape=jax.ShapeDtypeStruct((K_ROWS, D_MODEL), f32))(
        *sm_ffn, *sm_mix, sm_fin, table, g_mix, g_ffn, lam)


def _small_sum(gathered):
    def body(g_ref, o_ref, token_ref):
        tot = g_ref[0]
        for src in range(1, N_DEV):
            tot = tot + g_ref[src]
        o_ref[...] = tot
        token_ref[...] = jnp.zeros_like(token_ref)

    return pl.pallas_call(
        body, name="small_sum",
        out_shape=(jax.ShapeDtypeStruct(gathered.shape[1:], f32), jax.ShapeDtypeStruct((8, 128), f32)))(gathered)


def _adamw_math(g, w, m, v):
    m = ADAM_B1 * m + (1.0 - ADAM_B1) * g
    v = ADAM_B2 * v + (1.0 - ADAM_B2) * (g * g)
    m_hat = m / (1.0 - ADAM_B1 ** ADAM_STEP)
    v_hat = v / (1.0 - ADAM_B2 ** ADAM_STEP)
    delta = -ADAM_LR * (m_hat / (jnp.sqrt(v_hat) + ADAM_EPS) + ADAM_WD * w)
    return delta, m, v


def _adamw_small(name, g, w, m, v):
    shape = w.shape
    two_d = (1, shape[0]) if len(shape) == 1 else (math.prod(shape[:-1]), shape[-1])

    def body(g_ref, w_ref, m_ref, v_ref, d_ref, mo_ref, vo_ref):
        d_ref[...], mo_ref[...], vo_ref[...] = _adamw_math(g_ref[...], w_ref[...], m_ref[...], v_ref[...])

    outs = pl.pallas_call(body, name=f"adamw_{name}", out_shape=tuple(jax.ShapeDtypeStruct(two_d, f32) for _ in range(3)))(
        *(t.reshape(two_d) for t in (g, w, m, v)))
    return tuple(t.reshape(shape) for t in outs)


def _block_rows(rows, cols):
    tr = max(SUBLANES, min(rows, (512 * 1024) // (4 * cols)))
    while rows % tr:
        tr //= 2
    return tr


def _adamw_reduce(name, landings, kind, w, m, v):
    nl = len(landings)
    rows, cols = landings[0].shape[2:]
    tr = _block_rows(rows, cols)
    per_layer = rows // tr

    def body(*refs):
        l_refs = refs[:nl]
        w_ref, m_ref, v_ref, g_ref, d_ref, mo_ref, vo_ref = refs[nl:]
        layer = pl.program_id(0)
        for k in range(nl):
            @pl.when(layer == k)
            def _(k=k):
                g = l_refs[k][0].astype(f32)
                for src in range(1, N_DEV):
                    g = g + l_refs[k][src].astype(f32)
                g_ref[...] = g
        d_ref[...], mo_ref[...], vo_ref[...] = _adamw_math(g_ref[...], w_ref[...], m_ref[...], v_ref[...])

    blk = pl.BlockSpec((tr, cols), lambda l, r: (l * per_layer + r, 0))
    land = [pl.BlockSpec((N_DEV, None, tr, cols), lambda l, r, k=k: (0, kind, jnp.where(l == k, r, 0), 0)) for k in range(nl)]
    return pl.pallas_call(
        body, name=f"adamw_{name}", grid=(nl, per_layer),
        out_shape=tuple(jax.ShapeDtypeStruct((nl * rows, cols), f32) for _ in range(4)),
        in_specs=land + [blk, blk, blk],
        out_specs=(blk, blk, blk, blk),
        compiler_params=_params(("arbitrary", "arbitrary"), 32),
    )(*landings, w, m, v)


def _adamw_w_mod(c_all, dmod_all, w, m, v):
    depth, d, cols = w.shape
    tr = 256

    def body(c_ref, dm_ref, w_ref, m_ref, v_ref, g_ref, d_ref, mo_ref, vo_ref):
        cv = c_ref[...]
        cond = cv * _sigmoid(cv)
        g = lax.dot_general(cond, dm_ref[...], (((0,), (0,)), ((), ())), preferred_element_type=f32,
                            precision=lax.Precision.HIGHEST)
        g_ref[...] = g
        d_ref[...], mo_ref[...], vo_ref[...] = _adamw_math(g, w_ref[...], m_ref[...], v_ref[...])

    blk = pl.BlockSpec((None, tr, cols), lambda i, r: (i, r, 0))
    return pl.pallas_call(
        body, name="adamw_w_mod", grid=(depth, d // tr),
        out_shape=tuple(jax.ShapeDtypeStruct(w.shape, f32) for _ in range(4)),
        in_specs=[pl.BlockSpec((N_DEV, tr), lambda i, r: (0, r)),
                  pl.BlockSpec((None, N_DEV, cols), lambda i, r: (i, 0, 0)), blk, blk, blk],
        out_specs=(blk, blk, blk, blk),
        compiler_params=_params(("arbitrary", "arbitrary"), 32),
    )(c_all, dmod_all, w, m, v)


def kernel(x, c, w_mod, b_mod, norm_mix_g, norm_ffn_g, lru_w_y, lru_b_y, lru_w_in, lru_b_in, lru_conv_w, lru_conv_b, lru_w_a, lru_b_a, lru_w_x, lru_b_x, lru_lambda, lru_w_out, lru_b_out, pool_w, pool_scale, ffn_w1, ffn_w2, final_norm_g, loss_target, m_w_mod, m_b_mod, m_norm_mix_g, m_norm_ffn_g, m_lru_w_y, m_lru_b_y, m_lru_w_in, m_lru_b_in, m_lru_conv_w, m_lru_conv_b, m_lru_w_a, m_lru_b_a, m_lru_w_x, m_lru_b_x, m_lru_lambda, m_lru_w_out, m_lru_b_out, m_pool_w, m_pool_scale, m_ffn_w1, m_ffn_w2, m_final_norm_g, v_w_mod, v_b_mod, v_norm_mix_g, v_norm_ffn_g, v_lru_w_y, v_lru_b_y, v_lru_w_in, v_lru_b_in, v_lru_conv_w, v_lru_conv_b, v_lru_w_a, v_lru_b_a, v_lru_w_x, v_lru_b_x, v_lru_lambda, v_lru_w_out, v_lru_b_out, v_pool_w, v_pool_scale, v_ffn_w1, v_ffn_w2, v_final_norm_g):
    me = 4 * lax.axis_index("x") + 2 * lax.axis_index("y") + lax.axis_index("c")
    n_lru = lru_w_y.shape[0]
    shard = LRU_WIDTH // N_DEV
    hshard = HEAD_DIM // N_DEV
    xs = x[0]
    target = loss_target[0]

    small_vecs = jnp.concatenate([
        lru_conv_w.reshape(n_lru * 4, shard), lru_b_a.reshape(n_lru, HEADS * hshard),
        lru_b_x.reshape(n_lru, HEADS * hshard), pool_scale, jnp.zeros((2, shard), f32)], axis=0)
    first_pieces = 4
    (first_mix,), token = _send_start("gather_first_start", [[
        jnp.stack([lru_w_y[0], lru_w_in[0], lru_w_out[0]]).astype(bf16).reshape(3 * first_pieces, -1, LRU_WIDTH),
        jnp.stack([lru_w_a[0], lru_w_x[0]]).astype(bf16).reshape(first_pieces, -1, HEAD_DIM)]], True, me,
        pieces=first_pieces)
    sv_g, c_g = _exchange([small_vecs + token[0, 0], c], True, "gather_cond")
    conv_w_full = sv_g[:, 0:8].reshape(N_DEV, n_lru, 4, shard).transpose(1, 2, 0, 3).reshape(n_lru, 4, LRU_WIDTH)
    b_a_full = sv_g[:, 8:10].reshape(N_DEV, n_lru, HEADS, hshard).transpose(1, 2, 0, 3).reshape(n_lru, LRU_WIDTH)
    b_x_full = sv_g[:, 10:12].reshape(N_DEV, n_lru, HEADS, hshard).transpose(1, 2, 0, 3).reshape(n_lru, LRU_WIDTH)
    ps_full = sv_g[:, 12:14].transpose(1, 0, 2).reshape(n_lru, D_MODEL)
    c_all = c_g.reshape(N_DEV, D_MODEL)

    (mod_g,) = _exchange([_mod_part(c_all, w_mod)], True, "gather_mod", pieces=DEPTH)
    mod_row = lax.dynamic_index_in_dim(mod_g, me, axis=2, keepdims=False)
    mod_row = mod_row.transpose(1, 0, 2).reshape(DEPTH, N_MOD * D_MODEL)
    table, token = _mod_table(mod_row, b_mod, norm_mix_g, norm_ffn_g)

    parts = []
    for i in range(DEPTH):
        j = i // 2
        if i > 0 and i % 2 == 0:
            parts.append([(jnp.stack([lru_w_y[j], lru_w_in[j], lru_w_out[j]]) + token[0, 0]).astype(bf16),
                          (jnp.stack([lru_w_a[j], lru_w_x[j]]) + token[0, 0]).astype(bf16)])
        elif i % 2 == 1:
            parts.append([(pool_w[j] + token[0, 0]).astype(bf16)])
        parts.append([(ffn_w1[i] + token[0, 0]).astype(bf16), (ffn_w2[i] + token[0, 0]).astype(bf16)])
    handles, token = _send_start("gather_rest_start", parts, True, me)
    h_ffn = [handles[0], handles[2], handles[4], handles[6]]
    h_mix = [first_mix, handles[1], handles[3], handles[5]]

    zero_row = jnp.zeros((1, LRU_WIDTH), f32)
    pvecs = [jnp.concatenate([lru_b_y[j:j + 1], lru_b_in[j:j + 1], lru_conv_b[j:j + 1], b_a_full[j:j + 1],
                              b_x_full[j:j + 1], lru_lambda[j:j + 1], lru_b_out[j:j + 1], zero_row,
                              conv_w_full[j], zero_row, zero_row, zero_row, zero_row], axis=0) for j in range(n_lru)]
    ps_rows = [jnp.concatenate([ps_full[j:j + 1], jnp.zeros((7, D_MODEL), f32)], axis=0) for j in range(n_lru)]

    saved = []
    ffn_w, mix_w = [], []
    h = xs
    for i in range(DEPTH):
        j = i // 2
        got, _ = _send_wait(f"gather_mix_wait_{i}", h_mix[i], h)
        if i % 2 == 0:
            got = [got[0].reshape(N_DEV, 3, shard, LRU_WIDTH), got[1].reshape(N_DEV, 2, HEADS, hshard, HEAD_DIM)]
            mix_w.append((got[0].transpose(1, 0, 2, 3).reshape(3, LRU_WIDTH, LRU_WIDTH),
                          got[1].transpose(1, 2, 0, 3, 4).reshape(2, HEADS, HEAD_DIM, HEAD_DIM)))
            h_mid, *lru_saved = _lru_fwd(h, table[i] + token[0, 0], mix_w[i][0], mix_w[i][1], pvecs[j], i)
            mix_saved = (h, tuple(lru_saved))
        else:
            mix_w.append((got[0].transpose(1, 0, 2, 3).reshape(HEADS, HEAD_DIM, HEAD_DIM),))
            h_mid, y_mix = _pool_fwd(h, table[i], mix_w[i][0], ps_rows[j], i)
            mix_saved = (h, y_mix)
        ffn_w.append(_send_wait(f"gather_ffn_wait_{i}", h_ffn[i], h_mid)[0])
        h_out, u, y_ffn, hb = _ffn_fwd(h_mid, table[i], ffn_w[i][0], ffn_w[i][1], i)
        saved.append((mix_saved, (h_mid, u, y_ffn, hb)))
        h = h_out
    fin_rows = jnp.concatenate([final_norm_g[None, :], jnp.zeros((7, D_MODEL), f32)], axis=0)
    dx, loss_part, sm_fin = _final(h, target, fin_rows)
    loss = lax.psum(loss_part[0, 0], ("x", "y", "c"))

    sm_ffn, sm_mix = [None] * DEPTH, [None] * DEPTH
    x_ffn, x_mix = [None] * DEPTH, [None] * DEPTH
    token = jnp.zeros((8, 128), f32)
    last_mix = None
    for i in reversed(range(DEPTH)):
        j = i // 2
        mix_saved, (h_mid, u, y_ffn, hb) = saved[i]
        dx, da, dyb, sm_ffn[i] = _ffn_bwd_act(h_mid, dx, u, y_ffn, table[i] + token[0, 0], ffn_w[i][0], ffn_w[i][1], i)
        ffn_grads = [_ffn_bwd_w1(hb, da, i), _ffn_bwd_w2(u, dyb, i)]
        if last_mix is None:
            (x_ffn[i],), token = _send_start(f"grads_start_{i}", [ffn_grads], False, me)
        else:
            (x_mix[i + 1], x_ffn[i]), token = _send_start(f"grads_start_{i}", [last_mix, ffn_grads], False, me)
        if i % 2 == 0:
            h_in, lru_saved = mix_saved
            dx, dbig, dsmall, sm_mix[i] = _lru_bwd(
                h_in, dx, lru_saved, table[i] + token[0, 0], mix_w[i][0], mix_w[i][1], pvecs[j], i)
            last_mix = [dbig, dsmall]
        else:
            h_in, y_mix = mix_saved
            dx, dpool, sm = _pool_bwd(h_in, dx, y_mix, table[i] + token[0, 0], mix_w[i][0], ps_rows[j], i)
            sm_mix[i] = jnp.concatenate([sm, jnp.zeros((8, D_MODEL), f32)], axis=0)
            last_mix = [dpool]
    grad_x = dx[None]

    pack = _small_pack(sm_ffn, sm_mix, sm_fin, table + token[0, 0], norm_mix_g, norm_ffn_g, lru_lambda)
    (pack_g,) = _exchange([pack], True, "gather_small_grads", pieces=4)
    tot, token = _small_sum(pack_g)
    (x_mix[0],), _ = _send_start("grads_last_start", [[t + token[0, 0].astype(bf16) for t in last_mix]], False, me)
    cols = w_mod.shape[2]
    dmod_all = lax.dynamic_slice_in_dim(pack_g[:, K_MOD:K_MOD + DEPTH * N_MOD].reshape(N_DEV, DEPTH, N_MOD * D_MODEL),
                                        me * cols, cols, axis=2).transpose(1, 0, 2)
    results = {"w_mod": _adamw_w_mod(c_all, dmod_all, w_mod, m_w_mod, v_w_mod)}

    after = results["w_mod"][1]
    l_ffn = [_send_wait(f"grads_ffn_wait_{i}", x_ffn[i], after)[0] for i in reversed(range(DEPTH))][::-1]

    def reduce_update(name, landings, kind, w, m, v):
        rows = w.size // w.shape[-1]
        two_d = (rows, w.shape[-1])
        lands = [t.reshape(N_DEV, -1, rows // len(landings), w.shape[-1]) for t in landings]
        outs = _adamw_reduce(name, lands, kind, w.reshape(two_d), m.reshape(two_d), v.reshape(two_d))
        return tuple(t.reshape(w.shape) for t in outs)

    results["ffn_w1"] = reduce_update("ffn_w1", [t[0] for t in l_ffn], 0, ffn_w1, m_ffn_w1, v_ffn_w1)
    results["ffn_w2"] = reduce_update("ffn_w2", [t[1] for t in l_ffn], 0, ffn_w2, m_ffn_w2, v_ffn_w2)
    after = results["ffn_w2"][1]
    l_mix = [_send_wait(f"grads_mix_wait_{i}", x_mix[i], after)[0] for i in reversed(range(DEPTH))][::-1]
    l_lru_big = [l_mix[i][0] for i in range(0, DEPTH, 2)]
    l_lru_small = [l_mix[i][1] for i in range(0, DEPTH, 2)]
    l_pool = [l_mix[i][0] for i in range(1, DEPTH, 2)]
    results["lru_w_y"] = reduce_update("lru_w_y", l_lru_big, 0, lru_w_y, m_lru_w_y, v_lru_w_y)
    results["lru_w_in"] = reduce_update("lru_w_in", l_lru_big, 1, lru_w_in, m_lru_w_in, v_lru_w_in)
    results["lru_w_out"] = reduce_update("lru_w_out", l_lru_big, 2, lru_w_out, m_lru_w_out, v_lru_w_out)
    results["lru_w_a"] = reduce_update("lru_w_a", l_lru_small, 0, lru_w_a, m_lru_w_a, v_lru_w_a)
    results["lru_w_x"] = reduce_update("lru_w_x", l_lru_small, 1, lru_w_x, m_lru_w_x, v_lru_w_x)
    results["pool_w"] = reduce_update("pool_w", l_pool, 0, pool_w, m_pool_w, v_pool_w)

    def my_cols(full, width):
        return lax.dynamic_slice_in_dim(full, me * width, width, axis=full.ndim - 1)

    lru_rows = tot[K_LRUB:K_LRUB + 5 * n_lru].reshape(n_lru, 5, LRU_WIDTH)
    small_grads = {
        "b_mod": tot[K_MOD:K_MOD + DEPTH * N_MOD].reshape(DEPTH, N_MOD * D_MODEL),
        "norm_mix_g": tot[K_NMIX:K_NMIX + DEPTH],
        "norm_ffn_g": tot[K_NFFN:K_NFFN + DEPTH],
        "lru_b_y": lru_rows[:, 0], "lru_b_in": lru_rows[:, 1], "lru_conv_b": lru_rows[:, 2],
        "lru_lambda": lru_rows[:, 3], "lru_b_out": lru_rows[:, 4],
        "lru_conv_w": my_cols(tot[K_CONVW:K_CONVW + 4 * n_lru].reshape(n_lru, 4, LRU_WIDTH), shard),
        "lru_b_a": my_cols(tot[K_BA:K_BA + n_lru].reshape(n_lru, HEADS, HEAD_DIM), hshard),
        "lru_b_x": my_cols(tot[K_BX:K_BX + n_lru].reshape(n_lru, HEADS, HEAD_DIM), hshard),
        "pool_scale": my_cols(tot[K_PS:K_PS + n_lru], shard),
        "final_norm_g": tot[K_FIN],
    }
    given = dict(b_mod=(b_mod, m_b_mod, v_b_mod), norm_mix_g=(norm_mix_g, m_norm_mix_g, v_norm_mix_g),
                 norm_ffn_g=(norm_ffn_g, m_norm_ffn_g, v_norm_ffn_g), lru_b_y=(lru_b_y, m_lru_b_y, v_lru_b_y),
                 lru_b_in=(lru_b_in, m_lru_b_in, v_lru_b_in), lru_conv_w=(lru_conv_w, m_lru_conv_w, v_lru_conv_w),
                 lru_conv_b=(lru_conv_b, m_lru_conv_b, v_lru_conv_b), lru_b_a=(lru_b_a, m_lru_b_a, v_lru_b_a),
                 lru_b_x=(lru_b_x, m_lru_b_x, v_lru_b_x), lru_lambda=(lru_lambda, m_lru_lambda, v_lru_lambda),
                 lru_b_out=(lru_b_out, m_lru_b_out, v_lru_b_out), pool_scale=(pool_scale, m_pool_scale, v_pool_scale),
                 final_norm_g=(final_norm_g, m_final_norm_g, v_final_norm_g))
    for name, g in small_grads.items():
        results[name] = (g,) + _adamw_small(name, g, *given[name])

    order = ["w_mod", "b_mod", "norm_mix_g", "norm_ffn_g", "lru_w_y", "lru_b_y", "lru_w_in", "lru_b_in", "lru_conv_w",
             "lru_conv_b", "lru_w_a", "lru_b_a", "lru_w_x", "lru_b_x", "lru_lambda", "lru_w_out", "lru_b_out", "pool_w",
             "pool_scale", "ffn_w1", "ffn_w2", "final_norm_g"]
    return (loss, grad_x, *[results[n][0] for n in order], *[results[n][1] for n in order],
            *[results[n][2] for n in order], *[results[n][3] for n in order])
```

```python
import functools
import math

import jax
import jax.numpy as jnp
from jax import lax
from jax.experimental import pallas as pl
from jax.experimental.pallas import tpu as pltpu

f32, bf16 = jnp.float32, jnp.bfloat16

D_MODEL = 1024
LRU_WIDTH = 1024
HEADS = 4
HEAD_DIM = 256
D_FF = 4096
DEPTH = 4
N_MOD = 6
N_DEV = 8
FF_CHUNK = D_FF // N_DEV
POOL_WINDOWS = (2, 4, 8, 16)
POOL_HALO = 16
EPS = 1e-6
LRU_C = 8.0

ADAM_LR = 0.001
ADAM_B1 = 0.9
ADAM_B2 = 0.999
ADAM_EPS = 1e-08
ADAM_WD = 0.01
ADAM_STEP = 10

V7X_VMEM_BYTES = 64 * 1024 * 1024
SUBLANES = 8
BF16_ROWS = 16

R_SH_M, R_SC_M, R_GT_M, R_SH_F, R_SC_F, R_GT_F, R_GS_M, R_GS_F = range(8)
P_BY, P_BIN, P_CONVB, P_BA, P_BX, P_LAM, P_BOUT, P_CW0 = 0, 1, 2, 3, 4, 5, 6, 8
G_SH, G_GS, G_GT, G_BY, G_BIN, G_CONVB, G_BA, G_BX, G_LS, G_BOUT, G_CW0 = 0, 1, 2, 3, 4, 5, 6, 7, 8, 9, 10
K_MOD, K_NMIX, K_NFFN, K_LRUB, K_CONVW, K_BA, K_BX, K_PS, K_FIN, K_ROWS = 0, 24, 28, 32, 42, 50, 52, 54, 56, 64


def _params(semantics=None, vmem_mb=48):
    return pltpu.CompilerParams(dimension_semantics=semantics, vmem_limit_bytes=vmem_mb * 1024 * 1024)


def _mm(a, b):
    return jnp.dot(a, b, preferred_element_type=f32)


def _mm_nt(a, b):
    return lax.dot_general(a, b, (((1,), (1,)), ((), ())), preferred_element_type=f32)


def _mm_tn(a, b):
    return lax.dot_general(a, b, (((0,), (0,)), ((), ())), preferred_element_type=f32)


def _rms(x):
    r = lax.rsqrt(jnp.mean(x * x, axis=-1, keepdims=True) + EPS)
    return x * r, r


def _norm_bwd(dh, n, r, gs):
    dn = dh * gs
    return r * (dn - n * jnp.mean(dn * n, axis=-1, keepdims=True))


def _colsum(v):
    return jnp.sum(v, axis=0, keepdims=True)


def _sigmoid(v):
    return 0.5 * jnp.tanh(0.5 * v) + 0.5


def _log_sigmoid(v):
    return jnp.minimum(v, 0.0) - jnp.log1p(jnp.exp(-jnp.abs(v)))


_GELU_C = 0.7978845608028654
_GELU_A = 0.044715


def _gelu_and_grad(v):
    v2 = v * v
    t = jnp.tanh(_GELU_C * v * (1.0 + _GELU_A * v2))
    p = 0.5 + 0.5 * t
    return v * p, p + (0.5 * v) * (1.0 - t * t) * (_GELU_C + (3.0 * _GELU_A * _GELU_C) * v2)


def _rows_before(halo, v, shifts):
    hr = halo.shape[0]
    ext = jnp.concatenate([halo, v], axis=0)
    return [pltpu.roll(ext, k, 0)[hr:] for k in shifts]


def _rows_after(v, halo, shifts):
    n = v.shape[0]
    ext = jnp.concatenate([v, halo], axis=0)
    return [pltpu.roll(ext, ext.shape[0] - k, 0)[:n] for k in shifts]


def _shift_matrix(n, halo_rows, shifts):
    rows = lax.broadcasted_iota(jnp.int32, (n, n + halo_rows), 0)
    cols = lax.broadcasted_iota(jnp.int32, (n, n + halo_rows), 1)
    return jnp.concatenate([(cols == rows + halo_rows - k).astype(bf16) for k in shifts], axis=0)


def _shifted_rows(sel, halo, v):
    n = v.shape[0]
    out = _mm(sel, jnp.concatenate([halo, v], axis=0))
    return [out[j * n:(j + 1) * n] for j in range(sel.shape[0] // n)]


def _block_diag(v, w_ref, kind):
    return jnp.concatenate(
        [_mm(v[:, h * HEAD_DIM:(h + 1) * HEAD_DIM], w_ref[kind, h]) for h in range(HEADS)], axis=1)


def _block_diag_t(v, w_ref, kind):
    return jnp.concatenate(
        [_mm_nt(v[:, h * HEAD_DIM:(h + 1) * HEAD_DIM], w_ref[kind, h]) for h in range(HEADS)], axis=1)


def _exchange(arrays, gather, name, pieces=1):
    n = len(arrays)
    peers = N_DEV - 1

    def body(*refs):
        ins, outs = refs[:n], refs[n:2 * n]
        send_sems, recv_sems, local_sems = refs[2 * n:]
        x, y, c = lax.axis_index("x"), lax.axis_index("y"), lax.axis_index("c")
        me = 4 * x + 2 * y + c
        local = []
        for k in range(n):
            cp = pltpu.make_async_copy(ins[k] if gather else ins[k].at[me], outs[k].at[me], local_sems.at[k])
            cp.start()
            local.append(cp)
        remote = _peer_copies(ins, outs, send_sems, recv_sems, gather, pieces)
        for cp in remote:
            cp.start()
        for cp in remote:
            cp.wait()
        for cp in local:
            cp.wait()

    out_shape = tuple(
        jax.ShapeDtypeStruct(((N_DEV,) + a.shape) if gather else a.shape, a.dtype) for a in arrays)
    outs = pl.pallas_call(
        body, name=name, out_shape=out_shape,
        in_specs=[pl.BlockSpec(memory_space=pl.ANY)] * n,
        out_specs=tuple(pl.BlockSpec(memory_space=pl.ANY) for _ in range(n)),
        scratch_shapes=[pltpu.SemaphoreType.DMA((n * pieces * peers,)), pltpu.SemaphoreType.DMA((n * pieces * peers,)),
                        pltpu.SemaphoreType.DMA((n,))],
        compiler_params=pltpu.CompilerParams(has_side_effects=True),
    )(*arrays)
    return list(outs)


_HBM = pl.BlockSpec(memory_space=pltpu.HBM)
_SEM = pl.BlockSpec(memory_space=pltpu.SEMAPHORE)
_DATAFLOW = pltpu.SideEffectType.DATAFLOW_SIDE_EFFECTING


def _peer_copies(src_refs, land_refs, send_sems, recv_sems, gather, pieces=1):
    x, y, c = lax.axis_index("x"), lax.axis_index("y"), lax.axis_index("c")
    me = 4 * x + 2 * y + c
    peers = N_DEV - 1
    copies = []
    for p in range(1, N_DEV):
        px = 1 - x if p & 4 else x
        py = 1 - y if p & 2 else y
        pc = 1 - c if p & 1 else c
        for k in range(len(src_refs)):
            block = src_refs[k] if gather else src_refs[k].at[4 * px + 2 * py + pc]
            dst = land_refs[k].at[me]
            rows = block.shape[0] // pieces
            for r in range(pieces):
                part = pl.ds(r * rows, rows)
                sem = (k * pieces + r) * peers + p - 1
                copies.append(pltpu.make_async_remote_copy(
                    src_ref=block.at[part] if pieces > 1 else block, dst_ref=dst.at[part] if pieces > 1 else dst,
                    send_sem=send_sems.at[sem], recv_sem=recv_sems.at[sem],
                    device_id=(px, py, pc), device_id_type=pl.DeviceIdType.MESH))
    return copies


def _landing(srcs, gather, me):
    out = []
    for a in srcs:
        own = a if gather else lax.dynamic_index_in_dim(a, me, 0, keepdims=False)
        out.append(lax.dynamic_update_index_in_dim(lax.empty((N_DEV,) + own.shape, own.dtype), own, me, 0))
    return out


def _send_start(name, groups, gather, me, pieces=1, after=None):
    sizes = [len(g) for g in groups]
    srcs = [a for g in groups for a in g]
    n = len(srcs)
    lands = _landing(srcs, gather, me)
    ng = len(groups)
    first = [sum(sizes[:g]) for g in range(ng)]
    extra = [] if after is None else [after]

    def body(*refs):
        src_refs, land_refs = refs[:n], refs[n:2 * n]
        sems, token = refs[2 * n + len(extra):2 * n + len(extra) + 2 * ng], refs[-1]
        for g in range(ng):
            part = slice(first[g], first[g] + sizes[g])
            for cp in _peer_copies(src_refs[part], land_refs[part], sems[2 * g], sems[2 * g + 1], gather, pieces):
                cp.start()
        token[...] = jnp.zeros_like(token)

    sem_shapes = [pltpu.SemaphoreType.DMA((sizes[g // 2] * pieces * (N_DEV - 1),)) for g in range(2 * ng)]
    outs = pl.pallas_call(
        body, name=name,
        out_shape=(*sem_shapes, *[pltpu.HBM(a.shape, a.dtype) for a in (*srcs, *lands)], jax.ShapeDtypeStruct((8, 128), f32)),
        in_specs=[_HBM] * (2 * n) + [pl.BlockSpec(memory_space=pl.ANY)] * len(extra),
        out_specs=(*[_SEM] * (2 * ng), *[_HBM] * (2 * n), pl.BlockSpec(memory_space=pltpu.VMEM)),
        input_output_aliases={k: 2 * ng + k for k in range(2 * n)},
        compiler_params=pltpu.CompilerParams(has_side_effects=_DATAFLOW),
    )(*[pltpu.with_memory_space_constraint(a, pltpu.HBM) for a in (*srcs, *lands)], *extra)
    srcs_thru, lands_thru = outs[2 * ng:2 * ng + n], outs[2 * ng + n:2 * ng + 2 * n]
    handles = [(outs[2 * g], outs[2 * g + 1], list(srcs_thru[first[g]:first[g] + sizes[g]]),
                list(lands_thru[first[g]:first[g] + sizes[g]]), gather, pieces) for g in range(ng)]
    return handles, outs[-1]


def _send_wait(name, handle, after):
    send_sems, recv_sems, srcs, lands, gather, pieces = handle
    n = len(srcs)

    def body(*refs):
        src_refs, land_refs = refs[:n], refs[n:2 * n]
        for cp in _peer_copies(src_refs, land_refs, refs[2 * n], refs[2 * n + 1], gather, pieces):
            cp.wait_send()
            cp.wait_recv()
        refs[-1][...] = jnp.zeros_like(refs[-1])

    outs = pl.pallas_call(
        body, name=name,
        out_shape=(*[pltpu.HBM(a.shape, a.dtype) for a in (*srcs, *lands)], jax.ShapeDtypeStruct((8, 128), f32)),
        in_specs=[_HBM] * (2 * n) + [_SEM, _SEM, pl.BlockSpec(memory_space=pl.ANY)],
        out_specs=(*[_HBM] * (2 * n), pl.BlockSpec(memory_space=pltpu.VMEM)),
        input_output_aliases={k: k for k in range(2 * n)},
        compiler_params=pltpu.CompilerParams(has_side_effects=_DATAFLOW),
    )(*srcs, *lands, send_sems, recv_sems, after)
    return list(outs[n:2 * n]), outs[-1]


def _mod_part(c_all, w_mod):
    depth, d, cols = w_mod.shape

    def body(c_ref, w_ref, o_ref):
        cv = c_ref[...]
        cond = cv * _sigmoid(cv)
        o_ref[...] = jnp.dot(cond, w_ref[...], preferred_element_type=f32, precision=lax.Precision.HIGHEST)

    return pl.pallas_call(
        body, name="mod_part", grid=(depth,),
        out_shape=jax.ShapeDtypeStruct((depth, N_DEV, cols), f32),
        in_specs=[pl.BlockSpec((N_DEV, d), lambda i: (0, 0)), pl.BlockSpec((None, d, cols), lambda i: (i, 0, 0))],
        out_specs=pl.BlockSpec((None, N_DEV, cols), lambda i: (i, 0, 0)),
        compiler_params=_params(("arbitrary",), 32),
    )(c_all, w_mod)


def _mod_table(mod_row, b_mod, g_mix, g_ffn):
    def body(m_ref, b_ref, gm_ref, gf_ref, o_ref, token_ref):
        for i in range(DEPTH):
            for k in range(N_MOD):
                o_ref[i, k:k + 1, :] = m_ref[i:i + 1, k * D_MODEL:(k + 1) * D_MODEL] + b_ref[i:i + 1, k * D_MODEL:(k + 1) * D_MODEL]
            o_ref[i, R_GS_M:R_GS_M + 1, :] = gm_ref[i:i + 1, :] * (1.0 + o_ref[i, R_SC_M:R_SC_M + 1, :])
            o_ref[i, R_GS_F:R_GS_F + 1, :] = gf_ref[i:i + 1, :] * (1.0 + o_ref[i, R_SC_F:R_SC_F + 1, :])
        token_ref[...] = jnp.zeros_like(token_ref)

    return pl.pallas_call(
        body, name="mod_table",
        out_shape=(jax.ShapeDtypeStruct((DEPTH, 8, D_MODEL), f32), jax.ShapeDtypeStruct((8, 128), f32)))(
        mod_row, b_mod, g_mix, g_ffn)


def _ffn_tile(s):
    return min(512, s)


def _layer_weights(shape):
    return pl.BlockSpec((N_DEV,) + shape, lambda i: (0, 0, 0))


def _ffn_fwd(x, vec, w1g, w2g, layer):
    s = x.shape[0]
    ts = _ffn_tile(s)

    def body(x_ref, vec_ref, w1_ref, w2_ref, xo_ref, u_ref, y_ref, hb_ref):
        xv = x_ref[...]
        n, _ = _rms(xv)
        hb = (n * vec_ref[R_GS_F:R_GS_F + 1, :] + vec_ref[R_SH_F:R_SH_F + 1, :]).astype(bf16)
        hb_ref[...] = hb
        yv = jnp.zeros((ts, D_MODEL), f32)
        for f in range(N_DEV):
            u = jnp.maximum(_mm(hb, w1_ref[f]), 0.0)
            u_ref[:, f * FF_CHUNK:(f + 1) * FF_CHUNK] = u.astype(bf16)
            yv = yv + _mm((u * u).astype(bf16), w2_ref[f])
        y_ref[...] = yv.astype(bf16)
        xo_ref[...] = xv + vec_ref[R_GT_F:R_GT_F + 1, :] * yv

    row = pl.BlockSpec((ts, D_MODEL), lambda i: (i, 0))
    return pl.pallas_call(
        body, name=f"ffn_fwd_{layer}", grid=(s // ts,),
        out_shape=(jax.ShapeDtypeStruct((s, D_MODEL), f32), jax.ShapeDtypeStruct((s, D_FF), bf16),
                   jax.ShapeDtypeStruct((s, D_MODEL), bf16), jax.ShapeDtypeStruct((s, D_MODEL), bf16)),
        in_specs=[row, pl.BlockSpec((8, D_MODEL), lambda i: (0, 0)),
                  _layer_weights((D_MODEL, FF_CHUNK)), _layer_weights((FF_CHUNK, D_MODEL))],
        out_specs=(row, pl.BlockSpec((ts, D_FF), lambda i: (i, 0)), row, row),
        compiler_params=_params(("arbitrary",), 56),
    )(x, vec, w1g, w2g)


def _ffn_bwd_act(x, dx, u, y, vec, w1g, w2g, layer):
    s = x.shape[0]
    ts = _ffn_tile(s)

    def body(x_ref, dx_ref, u_ref, y_ref, vec_ref, w1_ref, w2_ref, dxo_ref, da_ref, dyb_ref, sm_ref):
        @pl.when(pl.program_id(0) == 0)
        def _():
            sm_ref[...] = jnp.zeros_like(sm_ref)

        dxv = dx_ref[...]
        dyb = (dxv * vec_ref[R_GT_F:R_GT_F + 1, :]).astype(bf16)
        dyb_ref[...] = dyb
        sm_ref[G_GT:G_GT + 1, :] += _colsum(dxv * y_ref[...].astype(f32))
        dh = jnp.zeros((ts, D_MODEL), f32)
        for f in range(N_DEV):
            cols = slice(f * FF_CHUNK, (f + 1) * FF_CHUNK)
            dz = _mm_nt(dyb, w2_ref[f])
            dab = (dz * (2.0 * u_ref[:, cols].astype(f32))).astype(bf16)
            da_ref[:, cols] = dab
            dh = dh + _mm_nt(dab, w1_ref[f])
        n, r = _rms(x_ref[...])
        sm_ref[G_SH:G_SH + 1, :] += _colsum(dh)
        sm_ref[G_GS:G_GS + 1, :] += _colsum(dh * n)
        dxo_ref[...] = dxv + _norm_bwd(dh, n, r, vec_ref[R_GS_F:R_GS_F + 1, :])

    row = pl.BlockSpec((ts, D_MODEL), lambda i: (i, 0))
    wide = pl.BlockSpec((ts, D_FF), lambda i: (i, 0))
    return pl.pallas_call(
        body, name=f"ffn_bwd_act_{layer}", grid=(s // ts,),
        out_shape=(jax.ShapeDtypeStruct((s, D_MODEL), f32), jax.ShapeDtypeStruct((s, D_FF), bf16),
                   jax.ShapeDtypeStruct((s, D_MODEL), bf16), jax.ShapeDtypeStruct((8, D_MODEL), f32)),
        in_specs=[row, row, wide, row, pl.BlockSpec((8, D_MODEL), lambda i: (0, 0)),
                  _layer_weights((D_MODEL, FF_CHUNK)), _layer_weights((FF_CHUNK, D_MODEL))],
        out_specs=(row, wide, row, pl.BlockSpec((8, D_MODEL), lambda i: (0, 0))),
        compiler_params=_params(("arbitrary",), 58),
    )(x, dx, u, y, vec, w1g, w2g)


def _ffn_bwd_w1(hb, da, layer):
    s = hb.shape[0]
    ts = _ffn_tile(s)
    nt = s // ts

    def body(hb_ref, da_ref, dw_ref, acc_ref):
        i = pl.program_id(0)

        @pl.when(i == 0)
        def _():
            acc_ref[...] = jnp.zeros_like(acc_ref)

        hb = hb_ref[...]
        for f in range(N_DEV):
            acc_ref[f] += _mm_tn(hb, da_ref[:, f * FF_CHUNK:(f + 1) * FF_CHUNK])

        @pl.when(i == nt - 1)
        def _():
            dw_ref[...] = acc_ref[...].astype(bf16)

    return pl.pallas_call(
        body, name=f"ffn_bwd_w1_{layer}", grid=(nt,),
        out_shape=jax.ShapeDtypeStruct((N_DEV, D_MODEL, FF_CHUNK), bf16),
        in_specs=[pl.BlockSpec((ts, D_MODEL), lambda i: (i, 0)), pl.BlockSpec((ts, D_FF), lambda i: (i, 0))],
        out_specs=pl.BlockSpec((N_DEV, D_MODEL, FF_CHUNK), lambda i: (0, 0, 0)),
        scratch_shapes=[pltpu.VMEM((N_DEV, D_MODEL, FF_CHUNK), f32)],
        compiler_params=_params(("arbitrary",), 56),
    )(hb, da)


def _ffn_bwd_w2(u, dyb, layer):
    s = u.shape[0]
    ts = _ffn_tile(s)
    nt = s // ts

    def body(u_ref, dyb_ref, dw_ref, acc_ref):
        i = pl.program_id(0)

        @pl.when(i == 0)
        def _():
            acc_ref[...] = jnp.zeros_like(acc_ref)

        dyb = dyb_ref[...]
        for f in range(N_DEV):
            uv = u_ref[:, f * FF_CHUNK:(f + 1) * FF_CHUNK].astype(f32)
            acc_ref[f] += _mm_tn((uv * uv).astype(bf16), dyb)

        @pl.when(i == nt - 1)
        def _():
            dw_ref[...] = acc_ref[...].astype(bf16)

    return pl.pallas_call(
        body, name=f"ffn_bwd_w2_{layer}", grid=(nt,),
        out_shape=jax.ShapeDtypeStruct((N_DEV, FF_CHUNK, D_MODEL), bf16),
        in_specs=[pl.BlockSpec((ts, D_FF), lambda i: (i, 0)), pl.BlockSpec((ts, D_MODEL), lambda i: (i, 0))],
        out_specs=pl.BlockSpec((N_DEV, FF_CHUNK, D_MODEL), lambda i: (0, 0, 0)),
        scratch_shapes=[pltpu.VMEM((N_DEV, FF_CHUNK, D_MODEL), f32)],
        compiler_params=_params(("arbitrary",), 56),
    )(u, dyb)


def _lru_gates(xc, wsm_ref, pv_ref):
    xcb = xc.astype(bf16)
    gr = _sigmoid(_block_diag(xcb, wsm_ref, 0) + pv_ref[P_BA:P_BA + 1, :])
    gi = _sigmoid(_block_diag(xcb, wsm_ref, 1) + pv_ref[P_BX:P_BX + 1, :])
    log_a = (LRU_C * _log_sigmoid(pv_ref[P_LAM:P_LAM + 1, :])) * gr
    t = jnp.tanh(log_a)
    return gr, gi, jnp.exp(log_a), jnp.sqrt((-2.0 * t) / (1.0 - t))


def _conv(xr, taps_before, pv_ref):
    xc = xr * pv_ref[P_CW0 + 3:P_CW0 + 4, :] + pv_ref[P_CONVB:P_CONVB + 1, :]
    for k, v in zip((2, 1, 0), taps_before):
        xc = xc + v * pv_ref[P_CW0 + k:P_CW0 + k + 1, :]
    return xc


LRU_FWD_SUB, LRU_FWD_SUBS = 128, 2
LRU_BWD_SUB, LRU_BWD_SUBS = 256, 1


def _scan_rows(a, u, carry, reverse):
    groups = a.shape[0] // SUBLANES
    row = lax.broadcasted_iota(jnp.int32, (SUBLANES, a.shape[1]), 0)
    outs = [None] * groups
    for j in range(groups):
        g = groups - 1 - j if reverse else j
        av, uv = a[g * SUBLANES:(g + 1) * SUBLANES], u[g * SUBLANES:(g + 1) * SUBLANES]
        for k in (1, 2, 4):
            if reverse:
                valid, shift = row < SUBLANES - k, SUBLANES - k
            else:
                valid, shift = row >= k, k
            a_s = jnp.where(valid, pltpu.roll(av, shift, 0), 1.0)
            u_s = jnp.where(valid, pltpu.roll(uv, shift, 0), 0.0)
            uv = uv + av * u_s
            av = av * a_s
        h = uv + av * carry
        outs[g] = h
        carry = h[0:1, :] if reverse else h[SUBLANES - 1:SUBLANES, :]
    return jnp.concatenate(outs, axis=0), carry


def _lru_fwd(x, vec, wbig, wsm, pvec, layer):
    s = x.shape[0]
    sub = min(LRU_FWD_SUB, s)
    ts = min(sub * LRU_FWD_SUBS, s)
    nsub = ts // sub
    w = LRU_WIDTH

    def body(x_ref, vec_ref, wb_ref, wsm_ref, pv_ref, xo_ref, xr_ref, hs_ref, a_ref, mult_ref, gr_ref, gi_ref,
             gel_ref, geld_ref, y_ref, tail_ref, carry_ref):
        @pl.when(pl.program_id(0) == 0)
        def _():
            tail_ref[...] = jnp.zeros_like(tail_ref)
            carry_ref[...] = jnp.zeros_like(carry_ref)

        sel = _shift_matrix(sub, BF16_ROWS, (1, 2, 3))
        for k in range(nsub):
            rows = slice(k * sub, (k + 1) * sub)
            xv = x_ref[rows, :]
            n, _ = _rms(xv)
            hb = (n * vec_ref[R_GS_M:R_GS_M + 1, :] + vec_ref[R_SH_M:R_SH_M + 1, :]).astype(bf16)
            gelu_v, gelu_d = _gelu_and_grad(_mm(hb, wb_ref[0]) + pv_ref[P_BY:P_BY + 1, :])
            gel_ref[rows, :] = gelu_v.astype(bf16)
            geld_ref[rows, :] = gelu_d.astype(bf16)
            xrb = (_mm(hb, wb_ref[1]) + pv_ref[P_BIN:P_BIN + 1, :]).astype(bf16)
            xr_ref[rows, :] = xrb
            xc = _conv(xrb.astype(f32), _shifted_rows(sel, tail_ref[...], xrb), pv_ref)
            tail_ref[...] = xrb[sub - BF16_ROWS:, :]
            gr, gi, a, mult = _lru_gates(xc, wsm_ref, pv_ref)
            gr_ref[rows, :] = gr.astype(bf16)
            gi_ref[rows, :] = gi.astype(bf16)
            a_ref[rows, :] = a
            mult_ref[rows, :] = mult
            hs, carry = _scan_rows(a, mult * (gi * xc), carry_ref[0:1, :], reverse=False)
            carry_ref[0:1, :] = carry
            hs_ref[rows, :] = hs
            yv = _mm((hs * gelu_v).astype(bf16), wb_ref[2]) + pv_ref[P_BOUT:P_BOUT + 1, :]
            y_ref[rows, :] = yv.astype(bf16)
            xo_ref[rows, :] = xv + vec_ref[R_GT_M:R_GT_M + 1, :] * yv

    row = pl.BlockSpec((ts, D_MODEL), lambda i: (i, 0))
    roww = pl.BlockSpec((ts, w), lambda i: (i, 0))
    wide = lambda dt: jax.ShapeDtypeStruct((s, w), dt)
    return pl.pallas_call(
        body, name=f"lru_fwd_{layer}", grid=(s // ts,),
        out_shape=(jax.ShapeDtypeStruct((s, D_MODEL), f32), wide(bf16), wide(f32), wide(f32), wide(f32),
                   wide(bf16), wide(bf16), wide(bf16), wide(bf16), jax.ShapeDtypeStruct((s, D_MODEL), bf16)),
        in_specs=[row, pl.BlockSpec((8, D_MODEL), lambda i: (0, 0)),
                  pl.BlockSpec((3, w, w), lambda i: (0, 0, 0)),
                  pl.BlockSpec((2, HEADS, HEAD_DIM, HEAD_DIM), lambda i: (0, 0, 0, 0)),
                  pl.BlockSpec((16, w), lambda i: (0, 0))],
        out_specs=(row, roww, roww, roww, roww, roww, roww, roww, roww, row),
        scratch_shapes=[pltpu.VMEM((BF16_ROWS, w), bf16), pltpu.VMEM((SUBLANES, w), f32)],
        compiler_params=_params(("arbitrary",)),
    )(x, vec, wbig, wsm, pvec)


def _lru_bwd(x, dx, saved, vec, wbig, wsm, pvec, layer):
    xr, hs, a_all, mult_all, gr_all, gi_all, gel_all, geld_all, y = saved
    s = x.shape[0]
    sub = min(LRU_BWD_SUB, s)
    ts = min(sub * LRU_BWD_SUBS, s)
    nsub = ts // sub
    nt = s // ts
    w = LRU_WIDTH
    shard = w // N_DEV
    hshard = HEAD_DIM // N_DEV

    def body(x_ref, dx_ref, xr_ref, xrh_ref, hs_ref, hsh_ref, a_ref, mult_ref, gr_ref, gi_ref, gel_ref, geld_ref,
             y_ref, vec_ref, wb_ref, wsm_ref, pv_ref,
             dxo_ref, dwb_ref, dwsm_ref, sm_ref, accb_ref, accs_ref, eps_ref, dxc8_ref,
             hb_scr, dgb_scr, dxrb_scr, mb_scr, dyb_scr, xcb_scr, drab_scr, drxb_scr):
        i = pl.program_id(0)
        first_tile = i == nt - 1

        @pl.when(i == 0)
        def _():
            accb_ref[...] = jnp.zeros_like(accb_ref)
            accs_ref[...] = jnp.zeros_like(accs_ref)
            sm_ref[...] = jnp.zeros_like(sm_ref)
            eps_ref[...] = jnp.zeros_like(eps_ref)
            dxc8_ref[...] = jnp.zeros_like(dxc8_ref)

        gs = vec_ref[R_GS_M:R_GS_M + 1, :]
        c_ls = LRU_C * _log_sigmoid(pv_ref[P_LAM:P_LAM + 1, :])
        for k in reversed(range(nsub)):
            rows = slice(k * sub, (k + 1) * sub)
            xv = x_ref[rows, :]
            dxv = dx_ref[rows, :]
            n, r = _rms(xv)
            hb_scr[rows, :] = (n * gs + vec_ref[R_SH_M:R_SH_M + 1, :]).astype(bf16)
            xrv = xr_ref[rows, :].astype(f32)
            hsv = hs_ref[rows, :]
            if k == 0:
                xr_halo = jnp.where(first_tile, 0.0, xrh_ref[...].astype(f32))
                hs_halo = jnp.where(first_tile, 0.0, hsh_ref[...])
            else:
                xr_halo = xr_ref[k * sub - BF16_ROWS:k * sub, :].astype(f32)
                hs_halo = hs_ref[k * sub - SUBLANES:k * sub, :]
            xs1, xs2, xs3 = _rows_before(xr_halo, xrv, (1, 2, 3))
            xc = _conv(xrv, (xs1, xs2, xs3), pv_ref)
            xcb_scr[rows, :] = xc.astype(bf16)
            a, mult = a_ref[rows, :], mult_ref[rows, :]
            gr, gi = gr_ref[rows, :].astype(f32), gi_ref[rows, :].astype(f32)
            gelu_v = gel_ref[rows, :].astype(f32)

            dy = dxv * vec_ref[R_GT_M:R_GT_M + 1, :]
            dyb = dy.astype(bf16)
            dyb_scr[rows, :] = dyb
            sm_ref[G_GT:G_GT + 1, :] += _colsum(dxv * y_ref[rows, :].astype(f32))
            sm_ref[G_BOUT:G_BOUT + 1, :] += _colsum(dy)
            mb_scr[rows, :] = (hsv * gelu_v).astype(bf16)
            dm = _mm_nt(dyb, wb_ref[2])
            dhs = dm * gelu_v
            dgpre = dm * hsv * geld_ref[rows, :].astype(f32)
            dgb = dgpre.astype(bf16)
            dgb_scr[rows, :] = dgb
            sm_ref[G_BY:G_BY + 1, :] += _colsum(dgpre)

            eps_in = eps_ref[0:1, :]
            eps, eps_out = _scan_rows(a, a * dhs, eps_in, reverse=True)
            eps_ref[0:1, :] = eps_out
            (eps_next,) = _rows_after(eps, jnp.broadcast_to(eps_in, (SUBLANES, w)), (1,))
            delta = dhs + eps_next
            (h_prev,) = _rows_before(hs_halo, hsv, (1,))
            dxi = delta * xc
            dgi = dxi * mult
            dla = (delta * h_prev) * a - (dxi * gi) * (a * a) / mult
            sm_ref[G_LS:G_LS + 1, :] += _colsum(dla * gr)
            dra = (dla * c_ls) * (gr - gr * gr)
            drx = dgi * (gi - gi * gi)
            drab, drxb = dra.astype(bf16), drx.astype(bf16)
            drab_scr[rows, :] = drab
            drxb_scr[rows, :] = drxb
            sm_ref[G_BA:G_BA + 1, :] += _colsum(dra)
            sm_ref[G_BX:G_BX + 1, :] += _colsum(drx)
            dxc = (delta * mult) * gi + _block_diag_t(drab, wsm_ref, 0) + _block_diag_t(drxb, wsm_ref, 1)

            sm_ref[G_CONVB:G_CONVB + 1, :] += _colsum(dxc)
            for kk, v in zip((3, 2, 1, 0), (xrv, xs1, xs2, xs3)):
                sm_ref[G_CW0 + kk:G_CW0 + kk + 1, :] += _colsum(dxc * v)
            ups = _rows_after(dxc, dxc8_ref[...], (1, 2, 3))
            dxc8_ref[...] = dxc[0:SUBLANES, :]
            dxr = dxc * pv_ref[P_CW0 + 3:P_CW0 + 4, :]
            for kk, v in zip((2, 1, 0), ups):
                dxr = dxr + v * pv_ref[P_CW0 + kk:P_CW0 + kk + 1, :]
            dxrb = dxr.astype(bf16)
            dxrb_scr[rows, :] = dxrb
            sm_ref[G_BIN:G_BIN + 1, :] += _colsum(dxr)
            dh = _mm_nt(dgb, wb_ref[0]) + _mm_nt(dxrb, wb_ref[1])
            sm_ref[G_SH:G_SH + 1, :] += _colsum(dh)
            sm_ref[G_GS:G_GS + 1, :] += _colsum(dh * n)
            dxo_ref[rows, :] = dxv + _norm_bwd(dh, n, r, gs)

        hb = hb_scr[...]
        accb_ref[0] += _mm_tn(hb, dgb_scr[...])
        accb_ref[1] += _mm_tn(hb, dxrb_scr[...])
        accb_ref[2] += _mm_tn(mb_scr[...], dyb_scr[...])
        for h in range(HEADS):
            cols = slice(h * HEAD_DIM, (h + 1) * HEAD_DIM)
            accs_ref[0, h] += _mm_tn(xcb_scr[:, cols], drab_scr[:, cols])
            accs_ref[1, h] += _mm_tn(xcb_scr[:, cols], drxb_scr[:, cols])

        @pl.when(i == nt - 1)
        def _():
            sm_ref[G_LS:G_LS + 1, :] = sm_ref[G_LS:G_LS + 1, :] * LRU_C
            for k in range(3):
                dwb_ref[:, k] = accb_ref[k].astype(bf16).reshape(N_DEV, shard, w)
            for k in range(2):
                for h in range(HEADS):
                    dwsm_ref[:, k, h] = accs_ref[k, h].astype(bf16).reshape(N_DEV, hshard, HEAD_DIM)

    rev = lambda i: (nt - 1 - i, 0)
    row = pl.BlockSpec((ts, D_MODEL), rev)
    roww = pl.BlockSpec((ts, w), rev)
    halo16 = pl.BlockSpec((BF16_ROWS, w), lambda i: (jnp.maximum((nt - 1 - i) * (ts // BF16_ROWS) - 1, 0), 0))
    halo8 = pl.BlockSpec((SUBLANES, w), lambda i: (jnp.maximum((nt - 1 - i) * (ts // SUBLANES) - 1, 0), 0))
    const = lambda *shape: pl.BlockSpec(shape, lambda i: (0,) * len(shape))
    operand = pltpu.VMEM((ts, w), bf16)
    return pl.pallas_call(
        body, name=f"lru_bwd_{layer}", grid=(nt,),
        out_shape=(jax.ShapeDtypeStruct((s, D_MODEL), f32),
                   jax.ShapeDtypeStruct((N_DEV, 3, shard, w), bf16),
                   jax.ShapeDtypeStruct((N_DEV, 2, HEADS, hshard, HEAD_DIM), bf16),
                   jax.ShapeDtypeStruct((16, w), f32)),
        in_specs=[row, row, roww, halo16, roww, halo8, roww, roww, roww, roww, roww, roww, row, const(8, D_MODEL),
                  const(3, w, w), const(2, HEADS, HEAD_DIM, HEAD_DIM), const(16, w)],
        out_specs=(row, const(N_DEV, 3, shard, w), const(N_DEV, 2, HEADS, hshard, HEAD_DIM), const(16, w)),
        scratch_shapes=[pltpu.VMEM((3, w, w), f32), pltpu.VMEM((2, HEADS, HEAD_DIM, HEAD_DIM), f32),
                        pltpu.VMEM((SUBLANES, w), f32), pltpu.VMEM((SUBLANES, w), f32)] + [operand] * 8,
        compiler_params=_params(("arbitrary",), 58),
    )(x, dx, xr, xr, hs, hs, a_all, mult_all, gr_all, gi_all, gel_all, geld_all, y, vec, wbig, wsm, pvec)


def _pool_tile(s):
    return min(256, s)


def _pool_counts(tile_index, ts):
    t = (tile_index * ts + lax.broadcasted_iota(jnp.int32, (ts, 1), 0) + 1).astype(f32)
    return [1.0 / jnp.minimum(t, float(win)) for win in POOL_WINDOWS]


def _pooled(h, halo, inv):
    ext = jnp.concatenate([halo, h], axis=0)
    out = []
    for g in range(len(POOL_WINDOWS)):
        acc = ext[:, g * HEAD_DIM:(g + 1) * HEAD_DIM]
        for step in range(g + 1):
            acc = acc + pltpu.roll(acc, 1 << step, 0)
        out.append(acc[POOL_HALO:] * inv[g] - h[:, g * HEAD_DIM:(g + 1) * HEAD_DIM])
    return out


def _pool_fwd(x, vec, pw, ps, layer):
    s = x.shape[0]
    ts = _pool_tile(s)

    def body(x_ref, vec_ref, pw_ref, ps_ref, xo_ref, y_ref, halo_ref):
        i = pl.program_id(0)

        @pl.when(i == 0)
        def _():
            halo_ref[...] = jnp.zeros_like(halo_ref)

        xv = x_ref[...]
        n, _ = _rms(xv)
        h = n * vec_ref[R_GS_M:R_GS_M + 1, :] + vec_ref[R_SH_M:R_SH_M + 1, :]
        pooled = _pooled(h, halo_ref[...], _pool_counts(i, ts))
        halo_ref[...] = h[ts - POOL_HALO:, :]
        mixed = jnp.concatenate([_mm(pooled[g].astype(bf16), pw_ref[g]) for g in range(HEADS)], axis=1)
        yv = mixed * ps_ref[0:1, :]
        y_ref[...] = yv.astype(bf16)
        xo_ref[...] = xv + vec_ref[R_GT_M:R_GT_M + 1, :] * yv

    row = pl.BlockSpec((ts, D_MODEL), lambda i: (i, 0))
    return pl.pallas_call(
        body, name=f"pool_fwd_{layer}", grid=(s // ts,),
        out_shape=(jax.ShapeDtypeStruct((s, D_MODEL), f32), jax.ShapeDtypeStruct((s, D_MODEL), bf16)),
        in_specs=[row, pl.BlockSpec((8, D_MODEL), lambda i: (0, 0)),
                  pl.BlockSpec((HEADS, HEAD_DIM, HEAD_DIM), lambda i: (0, 0, 0)),
                  pl.BlockSpec((8, D_MODEL), lambda i: (0, 0))],
        out_specs=(row, row),
        scratch_shapes=[pltpu.VMEM((POOL_HALO, D_MODEL), f32)],
        compiler_params=_params(("arbitrary",)),
    )(x, vec, pw, ps)


def _pool_bwd(x, dx, y, vec, pw, ps, layer):
    s = x.shape[0]
    ts = _pool_tile(s)
    nt = s // ts
    hshard = HEAD_DIM // N_DEV

    def body(x_ref, xh_ref, dx_ref, y_ref, vec_ref, pw_ref, ps_ref, dxo_ref, dpw_ref, sm_ref, acc_ref, q16_ref):
        i = pl.program_id(0)
        tile = nt - 1 - i

        @pl.when(i == 0)
        def _():
            acc_ref[...] = jnp.zeros_like(acc_ref)
            sm_ref[...] = jnp.zeros_like(sm_ref)
            q16_ref[...] = jnp.zeros_like(q16_ref)

        gs, sh = vec_ref[R_GS_M:R_GS_M + 1, :], vec_ref[R_SH_M:R_SH_M + 1, :]
        xv = x_ref[...]
        dxv = dx_ref[...]
        n, r = _rms(xv)
        h = n * gs + sh
        nh, _ = _rms(xh_ref[...])
        halo = jnp.where(tile == 0, 0.0, nh * gs + sh)
        inv = _pool_counts(tile, ts)
        pooled = _pooled(h, halo, inv)
        mixed = jnp.concatenate([_mm(pooled[g].astype(bf16), pw_ref[g]) for g in range(HEADS)], axis=1)

        dy = dxv * vec_ref[R_GT_M:R_GT_M + 1, :]
        sm_ref[G_GT:G_GT + 1, :] += _colsum(dxv * y_ref[...].astype(f32))
        sm_ref[3:4, :] += _colsum(dy * mixed)
        dmixed = (dy * ps_ref[0:1, :]).astype(bf16)
        dh_parts = []
        for g in range(HEADS):
            cols = slice(g * HEAD_DIM, (g + 1) * HEAD_DIM)
            acc_ref[g] += _mm_tn(pooled[g].astype(bf16), dmixed[:, cols])
            dpooled = _mm_nt(dmixed[:, cols], pw_ref[g])
            q = dpooled * inv[g]
            ext = jnp.concatenate([q, q16_ref[:, cols]], axis=0)
            q16_ref[:, cols] = q[0:POOL_HALO, :]
            for step in range(g + 1):
                ext = ext + pltpu.roll(ext, ext.shape[0] - (1 << step), 0)
            dh_parts.append(ext[:ts] - dpooled)
        dh = jnp.concatenate(dh_parts, axis=1)
        sm_ref[G_SH:G_SH + 1, :] += _colsum(dh)
        sm_ref[G_GS:G_GS + 1, :] += _colsum(dh * n)
        dxo_ref[...] = dxv + _norm_bwd(dh, n, r, gs)

        @pl.when(i == nt - 1)
        def _():
            for g in range(HEADS):
                dpw_ref[:, g] = acc_ref[g].astype(bf16).reshape(N_DEV, hshard, HEAD_DIM)

    rev = lambda i: (nt - 1 - i, 0)
    row = pl.BlockSpec((ts, D_MODEL), rev)
    halo16 = pl.BlockSpec((POOL_HALO, D_MODEL), lambda i: (jnp.maximum((nt - 1 - i) * (ts // POOL_HALO) - 1, 0), 0))
    const = lambda *shape: pl.BlockSpec(shape, lambda i: (0,) * len(shape))
    return pl.pallas_call(
        body, name=f"pool_bwd_{layer}", grid=(nt,),
        out_shape=(jax.ShapeDtypeStruct((s, D_MODEL), f32),
                   jax.ShapeDtypeStruct((N_DEV, HEADS, hshard, HEAD_DIM), bf16),
                   jax.ShapeDtypeStruct((8, D_MODEL), f32)),
        in_specs=[row, halo16, row, row, const(8, D_MODEL), const(HEADS, HEAD_DIM, HEAD_DIM), const(8, D_MODEL)],
        out_specs=(row, const(N_DEV, HEADS, hshard, HEAD_DIM), const(8, D_MODEL)),
        scratch_shapes=[pltpu.VMEM((HEADS, HEAD_DIM, HEAD_DIM), f32), pltpu.VMEM((POOL_HALO, D_MODEL), f32)],
        compiler_params=_params(("arbitrary",)),
    )(x, x, dx, y, vec, pw, ps)


def _final(x, target, g_fin):
    s = x.shape[0]
    ts = min(512, s)

    def body(x_ref, t_ref, g_ref, dx_ref, loss_ref, sm_ref):
        @pl.when(pl.program_id(0) == 0)
        def _():
            loss_ref[...] = jnp.zeros_like(loss_ref)
            sm_ref[...] = jnp.zeros_like(sm_ref)

        g = g_ref[0:1, :]
        n, r = _rms(x_ref[...])
        err = n * g - t_ref[...]
        loss_ref[...] += 0.5 * jnp.sum(jnp.mean(err * err, axis=-1, keepdims=True), axis=0, keepdims=True)
        dyv = err * (1.0 / D_MODEL)
        sm_ref[0:1, :] += _colsum(dyv * n)
        dx_ref[...] = _norm_bwd(dyv, n, r, g)

    row = pl.BlockSpec((ts, D_MODEL), lambda i: (i, 0))
    return pl.pallas_call(
        body, name="final_loss", grid=(s // ts,),
        out_shape=(jax.ShapeDtypeStruct((s, D_MODEL), f32), jax.ShapeDtypeStruct((8, 128), f32),
                   jax.ShapeDtypeStruct((8, D_MODEL), f32)),
        in_specs=[row, row, pl.BlockSpec((8, D_MODEL), lambda i: (0, 0))],
        out_specs=(row, pl.BlockSpec((8, 128), lambda i: (0, 0)), pl.BlockSpec((8, D_MODEL), lambda i: (0, 0))),
        compiler_params=_params(("arbitrary",)),
    )(x, target, g_fin)


def _small_pack(sm_ffn, sm_mix, sm_fin, table, g_mix, g_ffn, lam):
    def body(*refs):
        ffn, mix = refs[0:DEPTH], refs[DEPTH:2 * DEPTH]
        fin_ref, tab_ref, gm_ref, gf_ref, lam_ref, o_ref = refs[2 * DEPTH:]
        o_ref[...] = jnp.zeros_like(o_ref)
        for i in range(DEPTH):
            base = K_MOD + i * N_MOD
            o_ref[base + 0:base + 1, :] = mix[i][G_SH:G_SH + 1, :]
            o_ref[base + 1:base + 2, :] = mix[i][G_GS:G_GS + 1, :] * gm_ref[i:i + 1, :]
            o_ref[base + 2:base + 3, :] = mix[i][G_GT:G_GT + 1, :]
            o_ref[base + 3:base + 4, :] = ffn[i][G_SH:G_SH + 1, :]
            o_ref[base + 4:base + 5, :] = ffn[i][G_GS:G_GS + 1, :] * gf_ref[i:i + 1, :]
            o_ref[base + 5:base + 6, :] = ffn[i][G_GT:G_GT + 1, :]
            o_ref[K_NMIX + i:K_NMIX + i + 1, :] = mix[i][G_GS:G_GS + 1, :] * (1.0 + tab_ref[i, R_SC_M:R_SC_M + 1, :])
            o_ref[K_NFFN + i:K_NFFN + i + 1, :] = ffn[i][G_GS:G_GS + 1, :] * (1.0 + tab_ref[i, R_SC_F:R_SC_F + 1, :])
            j = i // 2
            if i % 2 == 0:
                for k, src in enumerate((G_BY, G_BIN, G_CONVB, None, G_BOUT)):
                    dst = K_LRUB + j * 5 + k
                    if src is None:
                        o_ref[dst:dst + 1, :] = mix[i][G_LS:G_LS + 1, :] * _sigmoid(-lam_ref[j:j + 1, :])
                    else:
                        o_ref[dst:dst + 1, :] = mix[i][src:src + 1, :]
                o_ref[K_CONVW + j * 4:K_CONVW + j * 4 + 4, :] = mix[i][G_CW0:G_CW0 + 4, :]
                o_ref[K_BA + j:K_BA + j + 1, :] = mix[i][G_BA:G_BA + 1, :]
                o_ref[K_BX + j:K_BX + j + 1, :] = mix[i][G_BX:G_BX + 1, :]
            else:
                o_ref[K_PS + j:K_PS + j + 1, :] = mix[i][3:4, :]
        o_ref[K_FIN:K_FIN + 1, :] = fin_ref[0:1, :]

    return pl.pallas_call(body, name="small_pack", out_shape=jax.ShapeDtypeStruct((K_ROWS, D_MODEL), f32))(
        *sm_ffn, *sm_mix, sm_fin, table, g_mix, g_ffn, lam)


def _small_sum(gathered):
    def body(g_ref, o_ref, token_ref):
        tot = g_ref[0]
        for src in range(1, N_DEV):
            tot = tot + g_ref[src]
        o_ref[...] = tot
        token_ref[...] = jnp.zeros_like(token_ref)

    return pl.pallas_call(
        body, name="small_sum",
        out_shape=(jax.ShapeDtypeStruct(gathered.shape[1:], f32), jax.ShapeDtypeStruct((8, 128), f32)))(gathered)


def _adamw_math(g, w, m, v):
    m = ADAM_B1 * m + (1.0 - ADAM_B1) * g
    v = ADAM_B2 * v + (1.0 - ADAM_B2) * (g * g)
    m_hat = m / (1.0 - ADAM_B1 ** ADAM_STEP)
    v_hat = v / (1.0 - ADAM_B2 ** ADAM_STEP)
    delta = -ADAM_LR * (m_hat / (jnp.sqrt(v_hat) + ADAM_EPS) + ADAM_WD * w)
    return delta, m, v


def _adamw_small(name, g, w, m, v):
    shape = w.shape
    two_d = (1, shape[0]) if len(shape) == 1 else (math.prod(shape[:-1]), shape[-1])

    def body(g_ref, w_ref, m_ref, v_ref, d_ref, mo_ref, vo_ref):
        d_ref[...], mo_ref[...], vo_ref[...] = _adamw_math(g_ref[...], w_ref[...], m_ref[...], v_ref[...])

    outs = pl.pallas_call(body, name=f"adamw_{name}", out_shape=tuple(jax.ShapeDtypeStruct(two_d, f32) for _ in range(3)))(
        *(t.reshape(two_d) for t in (g, w, m, v)))
    return tuple(t.reshape(shape) for t in outs)


def _block_rows(rows, cols):
    tr = max(SUBLANES, min(rows, (512 * 1024) // (4 * cols)))
    while rows % tr:
        tr //= 2
    return tr


def _adamw_reduce(name, landings, kind, w, m, v):
    nl = len(landings)
    rows, cols = landings[0].shape[2:]
    tr = _block_rows(rows, cols)
    per_layer = rows // tr

    def body(*refs):
        l_refs = refs[:nl]
        w_ref, m_ref, v_ref, g_ref, d_ref, mo_ref, vo_ref = refs[nl:]
        layer = pl.program_id(0)
        for k in range(nl):
            @pl.when(layer == k)
            def _(k=k):
                g = l_refs[k][0].astype(f32)
                for src in range(1, N_DEV):
                    g = g + l_refs[k][src].astype(f32)
                g_ref[...] = g
        d_ref[...], mo_ref[...], vo_ref[...] = _adamw_math(g_ref[...], w_ref[...], m_ref[...], v_ref[...])

    blk = pl.BlockSpec((tr, cols), lambda l, r: (l * per_layer + r, 0))
    land = [pl.BlockSpec((N_DEV, None, tr, cols), lambda l, r, k=k: (0, kind, jnp.where(l == k, r, 0), 0)) for k in range(nl)]
    return pl.pallas_call(
        body, name=f"adamw_{name}", grid=(nl, per_layer),
        out_shape=tuple(jax.ShapeDtypeStruct((nl * rows, cols), f32) for _ in range(4)),
        in_specs=land + [blk, blk, blk],
        out_specs=(blk, blk, blk, blk),
        compiler_params=_params(("arbitrary", "arbitrary"), 32),
    )(*landings, w, m, v)


def _adamw_w_mod(c_all, dmod_all, w, m, v):
    depth, d, cols = w.shape
    tr = 256

    def body(c_ref, dm_ref, w_ref, m_ref, v_ref, g_ref, d_ref, mo_ref, vo_ref):
        cv = c_ref[...]
        cond = cv * _sigmoid(cv)
        g = lax.dot_general(cond, dm_ref[...], (((0,), (0,)), ((), ())), preferred_element_type=f32,
                            precision=lax.Precision.HIGHEST)
        g_ref[...] = g
        d_ref[...], mo_ref[...], vo_ref[...] = _adamw_math(g, w_ref[...], m_ref[...], v_ref[...])

    blk = pl.BlockSpec((None, tr, cols), lambda i, r: (i, r, 0))
    return pl.pallas_call(
        body, name="adamw_w_mod", grid=(depth, d // tr),
        out_shape=tuple(jax.ShapeDtypeStruct(w.shape, f32) for _ in range(4)),
        in_specs=[pl.BlockSpec((N_DEV, tr), lambda i, r: (0, r)),
                  pl.BlockSpec((None, N_DEV, cols), lambda i, r: (i, 0, 0)), blk, blk, blk],
        out_specs=(blk, blk, blk, blk),
        compiler_params=_params(("arbitrary", "arbitrary"), 32),
    )(c_all, dmod_all, w, m, v)


def kernel(x, c, w_mod, b_mod, norm_mix_g, norm_ffn_g, lru_w_y, lru_b_y, lru_w_in, lru_b_in, lru_conv_w, lru_conv_b, lru_w_a, lru_b_a, lru_w_x, lru_b_x, lru_lambda, lru_w_out, lru_b_out, pool_w, pool_scale, ffn_w1, ffn_w2, final_norm_g, loss_target, m_w_mod, m_b_mod, m_norm_mix_g, m_norm_ffn_g, m_lru_w_y, m_lru_b_y, m_lru_w_in, m_lru_b_in, m_lru_conv_w, m_lru_conv_b, m_lru_w_a, m_lru_b_a, m_lru_w_x, m_lru_b_x, m_lru_lambda, m_lru_w_out, m_lru_b_out, m_pool_w, m_pool_scale, m_ffn_w1, m_ffn_w2, m_final_norm_g, v_w_mod, v_b_mod, v_norm_mix_g, v_norm_ffn_g, v_lru_w_y, v_lru_b_y, v_lru_w_in, v_lru_b_in, v_lru_conv_w, v_lru_conv_b, v_lru_w_a, v_lru_b_a, v_lru_w_x, v_lru_b_x, v_lru_lambda, v_lru_w_out, v_lru_b_out, v_pool_w, v_pool_scale, v_ffn_w1, v_ffn_w2, v_final_norm_g):
    me = 4 * lax.axis_index("x") + 2 * lax.axis_index("y") + lax.axis_index("c")
    n_lru = lru_w_y.shape[0]
    shard = LRU_WIDTH // N_DEV
    hshard = HEAD_DIM // N_DEV
    xs = x[0]
    target = loss_target[0]

    small_vecs = jnp.concatenate([
        lru_conv_w.reshape(n_lru * 4, shard), lru_b_a.reshape(n_lru, HEADS * hshard),
        lru_b_x.reshape(n_lru, HEADS * hshard), pool_scale, jnp.zeros((2, shard), f32)], axis=0)
    first_pieces = 4
    (first_mix,), token = _send_start("gather_first_start", [[
        jnp.stack([lru_w_y[0], lru_w_in[0], lru_w_out[0]]).astype(bf16).reshape(3 * first_pieces, -1, LRU_WIDTH),
        jnp.stack([lru_w_a[0], lru_w_x[0]]).astype(bf16).reshape(first_pieces, -1, HEAD_DIM)]], True, me,
        pieces=first_pieces)
    sv_g, c_g = _exchange([small_vecs + token[0, 0], c], True, "gather_cond")
    conv_w_full = sv_g[:, 0:8].reshape(N_DEV, n_lru, 4, shard).transpose(1, 2, 0, 3).reshape(n_lru, 4, LRU_WIDTH)
    b_a_full = sv_g[:, 8:10].reshape(N_DEV, n_lru, HEADS, hshard).transpose(1, 2, 0, 3).reshape(n_lru, LRU_WIDTH)
    b_x_full = sv_g[:, 10:12].reshape(N_DEV, n_lru, HEADS, hshard).transpose(1, 2, 0, 3).reshape(n_lru, LRU_WIDTH)
    ps_full = sv_g[:, 12:14].transpose(1, 0, 2).reshape(n_lru, D_MODEL)
    c_all = c_g.reshape(N_DEV, D_MODEL)

    (mod_g,) = _exchange([_mod_part(c_all, w_mod)], True, "gather_mod", pieces=DEPTH)
    mod_row = lax.dynamic_index_in_dim(mod_g, me, axis=2, keepdims=False)
    mod_row = mod_row.transpose(1, 0, 2).reshape(DEPTH, N_MOD * D_MODEL)
    table, token = _mod_table(mod_row, b_mod, norm_mix_g, norm_ffn_g)

    parts = []
    for i in range(DEPTH):
        j = i // 2
        if i > 0 and i % 2 == 0:
            parts.append([(jnp.stack([lru_w_y[j], lru_w_in[j], lru_w_out[j]]) + token[0, 0]).astype(bf16),
                          (jnp.stack([lru_w_a[j], lru_w_x[j]]) + token[0, 0]).astype(bf16)])
        elif i % 2 == 1:
            parts.append([(pool_w[j] + token[0, 0]).astype(bf16)])
        parts.append([(ffn_w1[i] + token[0, 0]).astype(bf16), (ffn_w2[i] + token[0, 0]).astype(bf16)])
    first_got, token = _send_wait("gather_mix_wait_0", first_mix, parts[-1][-1])
    handles, token = _send_start("gather_rest_start", parts, True, me, after=token)
    h_ffn = [handles[0], handles[2], handles[4], handles[6]]
    h_mix = [None, handles[1], handles[3], handles[5]]

    zero_row = jnp.zeros((1, LRU_WIDTH), f32)
    pvecs = [jnp.concatenate([lru_b_y[j:j + 1], lru_b_in[j:j + 1], lru_conv_b[j:j + 1], b_a_full[j:j + 1],
                              b_x_full[j:j + 1], lru_lambda[j:j + 1], lru_b_out[j:j + 1], zero_row,
                              conv_w_full[j], zero_row, zero_row, zero_row, zero_row], axis=0) for j in range(n_lru)]
    ps_rows = [jnp.concatenate([ps_full[j:j + 1], jnp.zeros((7, D_MODEL), f32)], axis=0) for j in range(n_lru)]

    saved = []
    ffn_w, mix_w = [], []
    h = xs
    for i in range(DEPTH):
        j = i // 2
        got = first_got if i == 0 else _send_wait(f"gather_mix_wait_{i}", h_mix[i], h)[0]
        if i % 2 == 0:
            got = [got[0].reshape(N_DEV, 3, shard, LRU_WIDTH), got[1].reshape(N_DEV, 2, HEADS, hshard, HEAD_DIM)]
            mix_w.append((got[0].transpose(1, 0, 2, 3).reshape(3, LRU_WIDTH, LRU_WIDTH),
                          got[1].transpose(1, 2, 0, 3, 4).reshape(2, HEADS, HEAD_DIM, HEAD_DIM)))
            h_mid, *lru_saved = _lru_fwd(h, table[i] + token[0, 0], mix_w[i][0], mix_w[i][1], pvecs[j], i)
            mix_saved = (h, tuple(lru_saved))
        else:
            mix_w.append((got[0].transpose(1, 0, 2, 3).reshape(HEADS, HEAD_DIM, HEAD_DIM),))
            h_mid, y_mix = _pool_fwd(h, table[i], mix_w[i][0], ps_rows[j], i)
            mix_saved = (h, y_mix)
        ffn_w.append(_send_wait(f"gather_ffn_wait_{i}", h_ffn[i], h_mid)[0])
        h_out, u, y_ffn, hb = _ffn_fwd(h_mid, table[i], ffn_w[i][0], ffn_w[i][1], i)
        saved.append((mix_saved, (h_mid, u, y_ffn, hb)))
        h = h_out
    fin_rows = jnp.concatenate([final_norm_g[None, :], jnp.zeros((7, D_MODEL), f32)], axis=0)
    dx, loss_part, sm_fin = _final(h, target, fin_rows)
    loss = lax.psum(loss_part[0, 0], ("x", "y", "c"))

    sm_ffn, sm_mix = [None] * DEPTH, [None] * DEPTH
    x_ffn, x_mix = [None] * DEPTH, [None] * DEPTH
    token = jnp.zeros((8, 128), f32)
    last_mix = None
    for i in reversed(range(DEPTH)):
        j = i // 2
        mix_saved, (h_mid, u, y_ffn, hb) = saved[i]
        dx, da, dyb, sm_ffn[i] = _ffn_bwd_act(h_mid, dx, u, y_ffn, table[i] + token[0, 0], ffn_w[i][0], ffn_w[i][1], i)
        ffn_grads = [_ffn_bwd_w1(hb, da, i), _ffn_bwd_w2(u, dyb, i)]
        if last_mix is None:
            (x_ffn[i],), token = _send_start(f"grads_start_{i}", [ffn_grads], False, me)
        else:
            (x_mix[i + 1], x_ffn[i]), token = _send_start(f"grads_start_{i}", [last_mix, ffn_grads], False, me)
        if i % 2 == 0:
            h_in, lru_saved = mix_saved
            dx, dbig, dsmall, sm_mix[i] = _lru_bwd(
                h_in, dx, lru_saved, table[i] + token[0, 0], mix_w[i][0], mix_w[i][1], pvecs[j], i)
            last_mix = [dbig, dsmall]
        else:
            h_in, y_mix = mix_saved
            dx, dpool, sm = _pool_bwd(h_in, dx, y_mix, table[i] + token[0, 0], mix_w[i][0], ps_rows[j], i)
            sm_mix[i] = jnp.concatenate([sm, jnp.zeros((8, D_MODEL), f32)], axis=0)
            last_mix = [dpool]
    grad_x = dx[None]

    pack = _small_pack(sm_ffn, sm_mix, sm_fin, table + token[0, 0], norm_mix_g, norm_ffn_g, lru_lambda)
    (pack_g,) = _exchange([pack], True, "gather_small_grads", pieces=4)
    tot, token = _small_sum(pack_g)
    (x_mix[0],), _ = _send_start("grads_last_start", [[t + token[0, 0].astype(bf16) for t in last_mix]], False, me)
    cols = w_mod.shape[2]
    dmod_all = lax.dynamic_slice_in_dim(pack_g[:, K_MOD:K_MOD + DEPTH * N_MOD].reshape(N_DEV, DEPTH, N_MOD * D_MODEL),
                                        me * cols, cols, axis=2).transpose(1, 0, 2)
    results = {"w_mod": _adamw_w_mod(c_all, dmod_all, w_mod, m_w_mod, v_w_mod)}

    after = results["w_mod"][1]
    l_ffn = [_send_wait(f"grads_ffn_wait_{i}", x_ffn[i], after)[0] for i in reversed(range(DEPTH))][::-1]

    def reduce_update(name, landings, kind, w, m, v):
        rows = w.size // w.shape[-1]
        two_d = (rows, w.shape[-1])
        lands = [t.reshape(N_DEV, -1, rows // len(landings), w.shape[-1]) for t in landings]
        outs = _adamw_reduce(name, lands, kind, w.reshape(two_d), m.reshape(two_d), v.reshape(two_d))
        return tuple(t.reshape(w.shape) for t in outs)

    results["ffn_w1"] = reduce_update("ffn_w1", [t[0] for t in l_ffn], 0, ffn_w1, m_ffn_w1, v_ffn_w1)
    results["ffn_w2"] = reduce_update("ffn_w2", [t[1] for t in l_ffn], 0, ffn_w2, m_ffn_w2, v_ffn_w2)
    after = results["ffn_w2"][1]
    l_mix = [_send_wait(f"grads_mix_wait_{i}", x_mix[i], after)[0] for i in reversed(range(DEPTH))][::-1]
    l_lru_big = [l_mix[i][0] for i in range(0, DEPTH, 2)]
    l_lru_small = [l_mix[i][1] for i in range(0, DEPTH, 2)]
    l_pool = [l_mix[i][0] for i in range(1, DEPTH, 2)]
    results["lru_w_y"] = reduce_update("lru_w_y", l_lru_big, 0, lru_w_y, m_lru_w_y, v_lru_w_y)
    results["lru_w_in"] = reduce_update("lru_w_in", l_lru_big, 1, lru_w_in, m_lru_w_in, v_lru_w_in)
    results["lru_w_out"] = reduce_update("lru_w_out", l_lru_big, 2, lru_w_out, m_lru_w_out, v_lru_w_out)
    results["lru_w_a"] = reduce_update("lru_w_a", l_lru_small, 0, lru_w_a, m_lru_w_a, v_lru_w_a)
    results["lru_w_x"] = reduce_update("lru_w_x", l_lru_small, 1, lru_w_x, m_lru_w_x, v_lru_w_x)
    results["pool_w"] = reduce_update("pool_w", l_pool, 0, pool_w, m_pool_w, v_pool_w)

    def my_cols(full, width):
        return lax.dynamic_slice_in_dim(full, me * width, width, axis=full.ndim - 1)

    lru_rows = tot[K_LRUB:K_LRUB + 5 * n_lru].reshape(n_lru, 5, LRU_WIDTH)
    small_grads = {
        "b_mod": tot[K_MOD:K_MOD + DEPTH * N_MOD].reshape(DEPTH, N_MOD * D_MODEL),
        "norm_mix_g": tot[K_NMIX:K_NMIX + DEPTH],
        "norm_ffn_g": tot[K_NFFN:K_NFFN + DEPTH],
        "lru_b_y": lru_rows[:, 0], "lru_b_in": lru_rows[:, 1], "lru_conv_b": lru_rows[:, 2],
        "lru_lambda": lru_rows[:, 3], "lru_b_out": lru_rows[:, 4],
        "lru_conv_w": my_cols(tot[K_CONVW:K_CONVW + 4 * n_lru].reshape(n_lru, 4, LRU_WIDTH), shard),
        "lru_b_a": my_cols(tot[K_BA:K_BA + n_lru].reshape(n_lru, HEADS, HEAD_DIM), hshard),
        "lru_b_x": my_cols(tot[K_BX:K_BX + n_lru].reshape(n_lru, HEADS, HEAD_DIM), hshard),
        "pool_scale": my_cols(tot[K_PS:K_PS + n_lru], shard),
        "final_norm_g": tot[K_FIN],
    }
    given = dict(b_mod=(b_mod, m_b_mod, v_b_mod), norm_mix_g=(norm_mix_g, m_norm_mix_g, v_norm_mix_g),
                 norm_ffn_g=(norm_ffn_g, m_norm_ffn_g, v_norm_ffn_g), lru_b_y=(lru_b_y, m_lru_b_y, v_lru_b_y),
                 lru_b_in=(lru_b_in, m_lru_b_in, v_lru_b_in), lru_conv_w=(lru_conv_w, m_lru_conv_w, v_lru_conv_w),
                 lru_conv_b=(lru_conv_b, m_lru_conv_b, v_lru_conv_b), lru_b_a=(lru_b_a, m_lru_b_a, v_lru_b_a),
                 lru_b_x=(lru_b_x, m_lru_b_x, v_lru_b_x), lru_lambda=(lru_lambda, m_lru_lambda, v_lru_lambda),
                 lru_b_out=(lru_b_out, m_lru_b_out, v_lru_b_out), pool_scale=(pool_scale, m_pool_scale, v_pool_scale),
                 final_norm_g=(final_norm_g, m_final_norm_g, v_final_norm_g))
    for name, g in small_grads.items():
        results[name] = (g,) + _adamw_small(name, g, *given[name])

    order = ["w_mod", "b_mod", "norm_mix_g", "norm_ffn_g", "lru_w_y", "lru_b_y", "lru_w_in", "lru_b_in", "lru_conv_w",
             "lru_conv_b", "lru_w_a", "lru_b_a", "lru_w_x", "lru_b_x", "lru_lambda", "lru_w_out", "lru_b_out", "pool_w",
             "pool_scale", "ffn_w1", "ffn_w2", "final_norm_g"]
    return (loss, grad_x, *[results[n][0] for n in order], *[results[n][1] for n in order],
            *[results[n][2] for n in order], *[results[n][3] for n in order])
```

```python
import functools
import math

import jax
import jax.numpy as jnp
from jax import lax
from jax.experimental import pallas as pl
from jax.experimental.pallas import tpu as pltpu

f32, bf16 = jnp.float32, jnp.bfloat16

D_MODEL = 1024
LRU_WIDTH = 1024
HEADS = 4
HEAD_DIM = 256
D_FF = 4096
DEPTH = 4
N_MOD = 6
N_DEV = 8
FF_CHUNK = D_FF // N_DEV
POOL_WINDOWS = (2, 4, 8, 16)
POOL_HALO = 16
EPS = 1e-6
LRU_C = 8.0

ADAM_LR = 0.001
ADAM_B1 = 0.9
ADAM_B2 = 0.999
ADAM_EPS = 1e-08
ADAM_WD = 0.01
ADAM_STEP = 10

V7X_VMEM_BYTES = 64 * 1024 * 1024
SUBLANES = 8
BF16_ROWS = 16

R_SH_M, R_SC_M, R_GT_M, R_SH_F, R_SC_F, R_GT_F, R_GS_M, R_GS_F = range(8)
P_BY, P_BIN, P_CONVB, P_BA, P_BX, P_LAM, P_BOUT, P_CW0 = 0, 1, 2, 3, 4, 5, 6, 8
G_SH, G_GS, G_GT, G_BY, G_BIN, G_CONVB, G_BA, G_BX, G_LS, G_BOUT, G_CW0 = 0, 1, 2, 3, 4, 5, 6, 7, 8, 9, 10
K_MOD, K_NMIX, K_NFFN, K_LRUB, K_CONVW, K_BA, K_BX, K_PS, K_FIN, K_ROWS = 0, 24, 28, 32, 42, 50, 52, 54, 56, 64


def _params(semantics=None, vmem_mb=48):
    return pltpu.CompilerParams(dimension_semantics=semantics, vmem_limit_bytes=vmem_mb * 1024 * 1024)


def _mm(a, b):
    return jnp.dot(a, b, preferred_element_type=f32)


def _mm_nt(a, b):
    return lax.dot_general(a, b, (((1,), (1,)), ((), ())), preferred_element_type=f32)


def _mm_tn(a, b):
    return lax.dot_general(a, b, (((0,), (0,)), ((), ())), preferred_element_type=f32)


def _rms(x):
    r = lax.rsqrt(jnp.mean(x * x, axis=-1, keepdims=True) + EPS)
    return x * r, r


def _norm_bwd(dh, n, r, gs):
    dn = dh * gs
    return r * (dn - n * jnp.mean(dn * n, axis=-1, keepdims=True))


def _colsum(v):
    return jnp.sum(v, axis=0, keepdims=True)


def _sigmoid(v):
    return 0.5 * jnp.tanh(0.5 * v) + 0.5


def _log_sigmoid(v):
    return jnp.minimum(v, 0.0) - jnp.log1p(jnp.exp(-jnp.abs(v)))


_GELU_C = 0.7978845608028654
_GELU_A = 0.044715


def _gelu_and_grad(v):
    v2 = v * v
    t = jnp.tanh(_GELU_C * v * (1.0 + _GELU_A * v2))
    p = 0.5 + 0.5 * t
    return v * p, p + (0.5 * v) * (1.0 - t * t) * (_GELU_C + (3.0 * _GELU_A * _GELU_C) * v2)


def _rows_before(halo, v, shifts):
    hr = halo.shape[0]
    ext = jnp.concatenate([halo, v], axis=0)
    return [pltpu.roll(ext, k, 0)[hr:] for k in shifts]


def _rows_after(v, halo, shifts):
    n = v.shape[0]
    ext = jnp.concatenate([v, halo], axis=0)
    return [pltpu.roll(ext, ext.shape[0] - k, 0)[:n] for k in shifts]


def _shift_matrix(n, halo_rows, shifts):
    rows = lax.broadcasted_iota(jnp.int32, (n, n + halo_rows), 0)
    cols = lax.broadcasted_iota(jnp.int32, (n, n + halo_rows), 1)
    return jnp.concatenate([(cols == rows + halo_rows - k).astype(bf16) for k in shifts], axis=0)


def _shifted_rows(sel, halo, v):
    n = v.shape[0]
    out = _mm(sel, jnp.concatenate([halo, v], axis=0))
    return [out[j * n:(j + 1) * n] for j in range(sel.shape[0] // n)]


def _block_diag(v, w_ref, kind):
    return jnp.concatenate(
        [_mm(v[:, h * HEAD_DIM:(h + 1) * HEAD_DIM], w_ref[kind, h]) for h in range(HEADS)], axis=1)


def _block_diag_t(v, w_ref, kind):
    return jnp.concatenate(
        [_mm_nt(v[:, h * HEAD_DIM:(h + 1) * HEAD_DIM], w_ref[kind, h]) for h in range(HEADS)], axis=1)


def _exchange(arrays, gather, name, pieces=1):
    n = len(arrays)
    peers = N_DEV - 1

    def body(*refs):
        ins, outs = refs[:n], refs[n:2 * n]
        send_sems, recv_sems, local_sems = refs[2 * n:]
        x, y, c = lax.axis_index("x"), lax.axis_index("y"), lax.axis_index("c")
        me = 4 * x + 2 * y + c
        local = []
        for k in range(n):
            cp = pltpu.make_async_copy(ins[k] if gather else ins[k].at[me], outs[k].at[me], local_sems.at[k])
            cp.start()
            local.append(cp)
        remote = _peer_copies(ins, outs, send_sems, recv_sems, gather, pieces)
        for cp in remote:
            cp.start()
        for cp in remote:
            cp.wait()
        for cp in local:
            cp.wait()

    out_shape = tuple(
        jax.ShapeDtypeStruct(((N_DEV,) + a.shape) if gather else a.shape, a.dtype) for a in arrays)
    outs = pl.pallas_call(
        body, name=name, out_shape=out_shape,
        in_specs=[pl.BlockSpec(memory_space=pl.ANY)] * n,
        out_specs=tuple(pl.BlockSpec(memory_space=pl.ANY) for _ in range(n)),
        scratch_shapes=[pltpu.SemaphoreType.DMA((n * pieces * peers,)), pltpu.SemaphoreType.DMA((n * pieces * peers,)),
                        pltpu.SemaphoreType.DMA((n,))],
        compiler_params=pltpu.CompilerParams(has_side_effects=True),
    )(*arrays)
    return list(outs)


_HBM = pl.BlockSpec(memory_space=pltpu.HBM)
_SEM = pl.BlockSpec(memory_space=pltpu.SEMAPHORE)
_DATAFLOW = pltpu.SideEffectType.DATAFLOW_SIDE_EFFECTING


def _peer_copies(src_refs, land_refs, send_sems, recv_sems, gather, pieces=1):
    x, y, c = lax.axis_index("x"), lax.axis_index("y"), lax.axis_index("c")
    me = 4 * x + 2 * y + c
    peers = N_DEV - 1
    copies = []
    for p in range(1, N_DEV):
        px = 1 - x if p & 4 else x
        py = 1 - y if p & 2 else y
        pc = 1 - c if p & 1 else c
        for k in range(len(src_refs)):
            block = src_refs[k] if gather else src_refs[k].at[4 * px + 2 * py + pc]
            dst = land_refs[k].at[me]
            rows = block.shape[0] // pieces
            for r in range(pieces):
                part = pl.ds(r * rows, rows)
                sem = (k * pieces + r) * peers + p - 1
                copies.append(pltpu.make_async_remote_copy(
                    src_ref=block.at[part] if pieces > 1 else block, dst_ref=dst.at[part] if pieces > 1 else dst,
                    send_sem=send_sems.at[sem], recv_sem=recv_sems.at[sem],
                    device_id=(px, py, pc), device_id_type=pl.DeviceIdType.MESH))
    return copies


def _landing(srcs, gather, me):
    out = []
    for a in srcs:
        own = a if gather else lax.dynamic_index_in_dim(a, me, 0, keepdims=False)
        out.append(lax.dynamic_update_index_in_dim(lax.empty((N_DEV,) + own.shape, own.dtype), own, me, 0))
    return out


def _send_start(name, groups, gather, me, pieces=1, after=None):
    sizes = [len(g) for g in groups]
    srcs = [a for g in groups for a in g]
    n = len(srcs)
    lands = _landing(srcs, gather, me)
    ng = len(groups)
    first = [sum(sizes[:g]) for g in range(ng)]
    extra = [] if after is None else [after]

    def body(*refs):
        src_refs, land_refs = refs[:n], refs[n:2 * n]
        sems, token = refs[2 * n + len(extra):2 * n + len(extra) + 2 * ng], refs[-1]
        for g in range(ng):
            part = slice(first[g], first[g] + sizes[g])
            for cp in _peer_copies(src_refs[part], land_refs[part], sems[2 * g], sems[2 * g + 1], gather, pieces):
                cp.start()
        token[...] = jnp.zeros_like(token)

    sem_shapes = [pltpu.SemaphoreType.DMA((sizes[g // 2] * pieces * (N_DEV - 1),)) for g in range(2 * ng)]
    outs = pl.pallas_call(
        body, name=name,
        out_shape=(*sem_shapes, *[pltpu.HBM(a.shape, a.dtype) for a in (*srcs, *lands)], jax.ShapeDtypeStruct((8, 128), f32)),
        in_specs=[_HBM] * (2 * n) + [pl.BlockSpec(memory_space=pl.ANY)] * len(extra),
        out_specs=(*[_SEM] * (2 * ng), *[_HBM] * (2 * n), pl.BlockSpec(memory_space=pltpu.VMEM)),
        input_output_aliases={k: 2 * ng + k for k in range(2 * n)},
        compiler_params=pltpu.CompilerParams(has_side_effects=_DATAFLOW),
    )(*[pltpu.with_memory_space_constraint(a, pltpu.HBM) for a in (*srcs, *lands)], *extra)
    srcs_thru, lands_thru = outs[2 * ng:2 * ng + n], outs[2 * ng + n:2 * ng + 2 * n]
    handles = [(outs[2 * g], outs[2 * g + 1], list(srcs_thru[first[g]:first[g] + sizes[g]]),
                list(lands_thru[first[g]:first[g] + sizes[g]]), gather, pieces) for g in range(ng)]
    return handles, outs[-1]


def _send_wait(name, handle, after):
    send_sems, recv_sems, srcs, lands, gather, pieces = handle
    n = len(srcs)

    def body(*refs):
        src_refs, land_refs = refs[:n], refs[n:2 * n]
        for cp in _peer_copies(src_refs, land_refs, refs[2 * n], refs[2 * n + 1], gather, pieces):
            cp.wait_send()
            cp.wait_recv()
        refs[-1][...] = jnp.zeros_like(refs[-1])

    outs = pl.pallas_call(
        body, name=name,
        out_shape=(*[pltpu.HBM(a.shape, a.dtype) for a in (*srcs, *lands)], jax.ShapeDtypeStruct((8, 128), f32)),
        in_specs=[_HBM] * (2 * n) + [_SEM, _SEM, pl.BlockSpec(memory_space=pl.ANY)],
        out_specs=(*[_HBM] * (2 * n), pl.BlockSpec(memory_space=pltpu.VMEM)),
        input_output_aliases={k: k for k in range(2 * n)},
        compiler_params=pltpu.CompilerParams(has_side_effects=_DATAFLOW),
    )(*srcs, *lands, send_sems, recv_sems, after)
    return list(outs[n:2 * n]), outs[-1]


def _mod_part(c_all, w_mod):
    depth, d, cols = w_mod.shape

    def body(c_ref, w_ref, o_ref):
        cv = c_ref[...]
        cond = cv * _sigmoid(cv)
        o_ref[...] = jnp.dot(cond, w_ref[...], preferred_element_type=f32, precision=lax.Precision.HIGHEST)

    return pl.pallas_call(
        body, name="mod_part", grid=(depth,),
        out_shape=jax.ShapeDtypeStruct((depth, N_DEV, cols), f32),
        in_specs=[pl.BlockSpec((N_DEV, d), lambda i: (0, 0)), pl.BlockSpec((None, d, cols), lambda i: (i, 0, 0))],
        out_specs=pl.BlockSpec((None, N_DEV, cols), lambda i: (i, 0, 0)),
        compiler_params=_params(("arbitrary",), 32),
    )(c_all, w_mod)


def _mod_table(mod_row, b_mod, g_mix, g_ffn):
    def body(m_ref, b_ref, gm_ref, gf_ref, o_ref, token_ref):
        for i in range(DEPTH):
            for k in range(N_MOD):
                o_ref[i, k:k + 1, :] = m_ref[i:i + 1, k * D_MODEL:(k + 1) * D_MODEL] + b_ref[i:i + 1, k * D_MODEL:(k + 1) * D_MODEL]
            o_ref[i, R_GS_M:R_GS_M + 1, :] = gm_ref[i:i + 1, :] * (1.0 + o_ref[i, R_SC_M:R_SC_M + 1, :])
            o_ref[i, R_GS_F:R_GS_F + 1, :] = gf_ref[i:i + 1, :] * (1.0 + o_ref[i, R_SC_F:R_SC_F + 1, :])
        token_ref[...] = jnp.zeros_like(token_ref)

    return pl.pallas_call(
        body, name="mod_table",
        out_shape=(jax.ShapeDtypeStruct((DEPTH, 8, D_MODEL), f32), jax.ShapeDtypeStruct((8, 128), f32)))(
        mod_row, b_mod, g_mix, g_ffn)


def _ffn_tile(s):
    return min(512, s)


def _layer_weights(shape):
    return pl.BlockSpec((N_DEV,) + shape, lambda i: (0, 0, 0))


def _ffn_fwd(x, vec, w1g, w2g, layer):
    s = x.shape[0]
    ts = _ffn_tile(s)

    def body(x_ref, vec_ref, w1_ref, w2_ref, xo_ref, u_ref, y_ref, hb_ref):
        xv = x_ref[...]
        n, _ = _rms(xv)
        hb = (n * vec_ref[R_GS_F:R_GS_F + 1, :] + vec_ref[R_SH_F:R_SH_F + 1, :]).astype(bf16)
        hb_ref[...] = hb
        yv = jnp.zeros((ts, D_MODEL), f32)
        for f in range(N_DEV):
            u = jnp.maximum(_mm(hb, w1_ref[f]), 0.0)
            u_ref[:, f * FF_CHUNK:(f + 1) * FF_CHUNK] = u.astype(bf16)
            yv = yv + _mm((u * u).astype(bf16), w2_ref[f])
        y_ref[...] = yv.astype(bf16)
        xo_ref[...] = xv + vec_ref[R_GT_F:R_GT_F + 1, :] * yv

    row = pl.BlockSpec((ts, D_MODEL), lambda i: (i, 0))
    return pl.pallas_call(
        body, name=f"ffn_fwd_{layer}", grid=(s // ts,),
        out_shape=(jax.ShapeDtypeStruct((s, D_MODEL), f32), jax.ShapeDtypeStruct((s, D_FF), bf16),
                   jax.ShapeDtypeStruct((s, D_MODEL), bf16), jax.ShapeDtypeStruct((s, D_MODEL), bf16)),
        in_specs=[row, pl.BlockSpec((8, D_MODEL), lambda i: (0, 0)),
                  _layer_weights((D_MODEL, FF_CHUNK)), _layer_weights((FF_CHUNK, D_MODEL))],
        out_specs=(row, pl.BlockSpec((ts, D_FF), lambda i: (i, 0)), row, row),
        compiler_params=_params(("arbitrary",), 56),
    )(x, vec, w1g, w2g)


def _ffn_bwd_act(x, dx, u, y, vec, w1g, w2g, layer):
    s = x.shape[0]
    ts = _ffn_tile(s)

    def body(x_ref, dx_ref, u_ref, y_ref, vec_ref, w1_ref, w2_ref, dxo_ref, da_ref, dyb_ref, sm_ref):
        @pl.when(pl.program_id(0) == 0)
        def _():
            sm_ref[...] = jnp.zeros_like(sm_ref)

        dxv = dx_ref[...]
        dyb = (dxv * vec_ref[R_GT_F:R_GT_F + 1, :]).astype(bf16)
        dyb_ref[...] = dyb
        sm_ref[G_GT:G_GT + 1, :] += _colsum(dxv * y_ref[...].astype(f32))
        dh = jnp.zeros((ts, D_MODEL), f32)
        for f in range(N_DEV):
            cols = slice(f * FF_CHUNK, (f + 1) * FF_CHUNK)
            dz = _mm_nt(dyb, w2_ref[f])
            dab = (dz * (2.0 * u_ref[:, cols].astype(f32))).astype(bf16)
            da_ref[:, cols] = dab
            dh = dh + _mm_nt(dab, w1_ref[f])
        n, r = _rms(x_ref[...])
        sm_ref[G_SH:G_SH + 1, :] += _colsum(dh)
        sm_ref[G_GS:G_GS + 1, :] += _colsum(dh * n)
        dxo_ref[...] = dxv + _norm_bwd(dh, n, r, vec_ref[R_GS_F:R_GS_F + 1, :])

    row = pl.BlockSpec((ts, D_MODEL), lambda i: (i, 0))
    wide = pl.BlockSpec((ts, D_FF), lambda i: (i, 0))
    return pl.pallas_call(
        body, name=f"ffn_bwd_act_{layer}", grid=(s // ts,),
        out_shape=(jax.ShapeDtypeStruct((s, D_MODEL), f32), jax.ShapeDtypeStruct((s, D_FF), bf16),
                   jax.ShapeDtypeStruct((s, D_MODEL), bf16), jax.ShapeDtypeStruct((8, D_MODEL), f32)),
        in_specs=[row, row, wide, row, pl.BlockSpec((8, D_MODEL), lambda i: (0, 0)),
                  _layer_weights((D_MODEL, FF_CHUNK)), _layer_weights((FF_CHUNK, D_MODEL))],
        out_specs=(row, wide, row, pl.BlockSpec((8, D_MODEL), lambda i: (0, 0))),
        compiler_params=_params(("arbitrary",), 58),
    )(x, dx, u, y, vec, w1g, w2g)


def _ffn_bwd_w1(hb, da, layer):
    s = hb.shape[0]
    ts = _ffn_tile(s)
    nt = s // ts

    def body(hb_ref, da_ref, dw_ref, acc_ref):
        i = pl.program_id(0)

        @pl.when(i == 0)
        def _():
            acc_ref[...] = jnp.zeros_like(acc_ref)

        hb = hb_ref[...]
        for f in range(N_DEV):
            acc_ref[f] += _mm_tn(hb, da_ref[:, f * FF_CHUNK:(f + 1) * FF_CHUNK])

        @pl.when(i == nt - 1)
        def _():
            dw_ref[...] = acc_ref[...].astype(bf16)

    return pl.pallas_call(
        body, name=f"ffn_bwd_w1_{layer}", grid=(nt,),
        out_shape=jax.ShapeDtypeStruct((N_DEV, D_MODEL, FF_CHUNK), bf16),
        in_specs=[pl.BlockSpec((ts, D_MODEL), lambda i: (i, 0)), pl.BlockSpec((ts, D_FF), lambda i: (i, 0))],
        out_specs=pl.BlockSpec((N_DEV, D_MODEL, FF_CHUNK), lambda i: (0, 0, 0)),
        scratch_shapes=[pltpu.VMEM((N_DEV, D_MODEL, FF_CHUNK), f32)],
        compiler_params=_params(("arbitrary",), 56),
    )(hb, da)


def _ffn_bwd_w2(u, dyb, layer):
    s = u.shape[0]
    ts = _ffn_tile(s)
    nt = s // ts

    def body(u_ref, dyb_ref, dw_ref, acc_ref):
        i = pl.program_id(0)

        @pl.when(i == 0)
        def _():
            acc_ref[...] = jnp.zeros_like(acc_ref)

        dyb = dyb_ref[...]
        for f in range(N_DEV):
            uv = u_ref[:, f * FF_CHUNK:(f + 1) * FF_CHUNK].astype(f32)
            acc_ref[f] += _mm_tn((uv * uv).astype(bf16), dyb)

        @pl.when(i == nt - 1)
        def _():
            dw_ref[...] = acc_ref[...].astype(bf16)

    return pl.pallas_call(
        body, name=f"ffn_bwd_w2_{layer}", grid=(nt,),
        out_shape=jax.ShapeDtypeStruct((N_DEV, FF_CHUNK, D_MODEL), bf16),
        in_specs=[pl.BlockSpec((ts, D_FF), lambda i: (i, 0)), pl.BlockSpec((ts, D_MODEL), lambda i: (i, 0))],
        out_specs=pl.BlockSpec((N_DEV, FF_CHUNK, D_MODEL), lambda i: (0, 0, 0)),
        scratch_shapes=[pltpu.VMEM((N_DEV, FF_CHUNK, D_MODEL), f32)],
        compiler_params=_params(("arbitrary",), 56),
    )(u, dyb)


def _lru_gates(xc, wsm_ref, pv_ref):
    xcb = xc.astype(bf16)
    gr = _sigmoid(_block_diag(xcb, wsm_ref, 0) + pv_ref[P_BA:P_BA + 1, :])
    gi = _sigmoid(_block_diag(xcb, wsm_ref, 1) + pv_ref[P_BX:P_BX + 1, :])
    log_a = (LRU_C * _log_sigmoid(pv_ref[P_LAM:P_LAM + 1, :])) * gr
    t = jnp.tanh(log_a)
    return gr, gi, jnp.exp(log_a), jnp.sqrt((-2.0 * t) / (1.0 - t))


def _conv(xr, taps_before, pv_ref):
    xc = xr * pv_ref[P_CW0 + 3:P_CW0 + 4, :] + pv_ref[P_CONVB:P_CONVB + 1, :]
    for k, v in zip((2, 1, 0), taps_before):
        xc = xc + v * pv_ref[P_CW0 + k:P_CW0 + k + 1, :]
    return xc


LRU_FWD_SUB, LRU_FWD_SUBS = 128, 2
LRU_BWD_SUB, LRU_BWD_SUBS = 256, 1


def _scan_rows(a, u, carry, reverse):
    groups = a.shape[0] // SUBLANES
    row = lax.broadcasted_iota(jnp.int32, (SUBLANES, a.shape[1]), 0)
    outs = [None] * groups
    for j in range(groups):
        g = groups - 1 - j if reverse else j
        av, uv = a[g * SUBLANES:(g + 1) * SUBLANES], u[g * SUBLANES:(g + 1) * SUBLANES]
        for k in (1, 2, 4):
            if reverse:
                valid, shift = row < SUBLANES - k, SUBLANES - k
            else:
                valid, shift = row >= k, k
            a_s = jnp.where(valid, pltpu.roll(av, shift, 0), 1.0)
            u_s = jnp.where(valid, pltpu.roll(uv, shift, 0), 0.0)
            uv = uv + av * u_s
            av = av * a_s
        h = uv + av * carry
        outs[g] = h
        carry = h[0:1, :] if reverse else h[SUBLANES - 1:SUBLANES, :]
    return jnp.concatenate(outs, axis=0), carry


def _lru_fwd(x, vec, wbig, wsm, pvec, layer):
    s = x.shape[0]
    sub = min(LRU_FWD_SUB, s)
    ts = min(sub * LRU_FWD_SUBS, s)
    nsub = ts // sub
    w = LRU_WIDTH

    def body(x_ref, vec_ref, wb_ref, wsm_ref, pv_ref, xo_ref, xr_ref, hs_ref, a_ref, mult_ref, gr_ref, gi_ref,
             gel_ref, geld_ref, y_ref, tail_ref, carry_ref):
        @pl.when(pl.program_id(0) == 0)
        def _():
            tail_ref[...] = jnp.zeros_like(tail_ref)
            carry_ref[...] = jnp.zeros_like(carry_ref)

        sel = _shift_matrix(sub, BF16_ROWS, (1, 2, 3))
        for k in range(nsub):
            rows = slice(k * sub, (k + 1) * sub)
            xv = x_ref[rows, :]
            n, _ = _rms(xv)
            hb = (n * vec_ref[R_GS_M:R_GS_M + 1, :] + vec_ref[R_SH_M:R_SH_M + 1, :]).astype(bf16)
            gelu_v, gelu_d = _gelu_and_grad(_mm(hb, wb_ref[0]) + pv_ref[P_BY:P_BY + 1, :])
            gel_ref[rows, :] = gelu_v.astype(bf16)
            geld_ref[rows, :] = gelu_d.astype(bf16)
            xrb = (_mm(hb, wb_ref[1]) + pv_ref[P_BIN:P_BIN + 1, :]).astype(bf16)
            xr_ref[rows, :] = xrb
            xc = _conv(xrb.astype(f32), _shifted_rows(sel, tail_ref[...], xrb), pv_ref)
            tail_ref[...] = xrb[sub - BF16_ROWS:, :]
            gr, gi, a, mult = _lru_gates(xc, wsm_ref, pv_ref)
            gr_ref[rows, :] = gr.astype(bf16)
            gi_ref[rows, :] = gi.astype(bf16)
            a_ref[rows, :] = a
            mult_ref[rows, :] = mult
            hs, carry = _scan_rows(a, mult * (gi * xc), carry_ref[0:1, :], reverse=False)
            carry_ref[0:1, :] = carry
            hs_ref[rows, :] = hs
            yv = _mm((hs * gelu_v).astype(bf16), wb_ref[2]) + pv_ref[P_BOUT:P_BOUT + 1, :]
            y_ref[rows, :] = yv.astype(bf16)
            xo_ref[rows, :] = xv + vec_ref[R_GT_M:R_GT_M + 1, :] * yv

    row = pl.BlockSpec((ts, D_MODEL), lambda i: (i, 0))
    roww = pl.BlockSpec((ts, w), lambda i: (i, 0))
    wide = lambda dt: jax.ShapeDtypeStruct((s, w), dt)
    return pl.pallas_call(
        body, name=f"lru_fwd_{layer}", grid=(s // ts,),
        out_shape=(jax.ShapeDtypeStruct((s, D_MODEL), f32), wide(bf16), wide(f32), wide(f32), wide(f32),
                   wide(bf16), wide(bf16), wide(bf16), wide(bf16), jax.ShapeDtypeStruct((s, D_MODEL), bf16)),
        in_specs=[row, pl.BlockSpec((8, D_MODEL), lambda i: (0, 0)),
                  pl.BlockSpec((3, w, w), lambda i: (0, 0, 0)),
                  pl.BlockSpec((2, HEADS, HEAD_DIM, HEAD_DIM), lambda i: (0, 0, 0, 0)),
                  pl.BlockSpec((16, w), lambda i: (0, 0))],
        out_specs=(row, roww, roww, roww, roww, roww, roww, roww, roww, row),
        scratch_shapes=[pltpu.VMEM((BF16_ROWS, w), bf16), pltpu.VMEM((SUBLANES, w), f32)],
        compiler_params=_params(("arbitrary",)),
    )(x, vec, wbig, wsm, pvec)


def _lru_bwd(x, dx, saved, vec, wbig, wsm, pvec, layer):
    xr, hs, a_all, mult_all, gr_all, gi_all, gel_all, geld_all, y = saved
    s = x.shape[0]
    sub = min(LRU_BWD_SUB, s)
    ts = min(sub * LRU_BWD_SUBS, s)
    nsub = ts // sub
    nt = s // ts
    w = LRU_WIDTH
    shard = w // N_DEV
    hshard = HEAD_DIM // N_DEV

    def body(x_ref, dx_ref, xr_ref, xrh_ref, hs_ref, hsh_ref, a_ref, mult_ref, gr_ref, gi_ref, gel_ref, geld_ref,
             y_ref, vec_ref, wb_ref, wsm_ref, pv_ref,
             dxo_ref, dwb_ref, dwsm_ref, sm_ref, accb_ref, accs_ref, eps_ref, dxc8_ref,
             hb_scr, dgb_scr, dxrb_scr, mb_scr, dyb_scr, xcb_scr, drab_scr, drxb_scr):
        i = pl.program_id(0)
        first_tile = i == nt - 1

        @pl.when(i == 0)
        def _():
            accb_ref[...] = jnp.zeros_like(accb_ref)
            accs_ref[...] = jnp.zeros_like(accs_ref)
            sm_ref[...] = jnp.zeros_like(sm_ref)
            eps_ref[...] = jnp.zeros_like(eps_ref)
            dxc8_ref[...] = jnp.zeros_like(dxc8_ref)

        gs = vec_ref[R_GS_M:R_GS_M + 1, :]
        c_ls = LRU_C * _log_sigmoid(pv_ref[P_LAM:P_LAM + 1, :])
        for k in reversed(range(nsub)):
            rows = slice(k * sub, (k + 1) * sub)
            xv = x_ref[rows, :]
            dxv = dx_ref[rows, :]
            n, r = _rms(xv)
            hb_scr[rows, :] = (n * gs + vec_ref[R_SH_M:R_SH_M + 1, :]).astype(bf16)
            xrv = xr_ref[rows, :].astype(f32)
            hsv = hs_ref[rows, :]
            if k == 0:
                xr_halo = jnp.where(first_tile, 0.0, xrh_ref[...].astype(f32))
                hs_halo = jnp.where(first_tile, 0.0, hsh_ref[...])
            else:
                xr_halo = xr_ref[k * sub - BF16_ROWS:k * sub, :].astype(f32)
                hs_halo = hs_ref[k * sub - SUBLANES:k * sub, :]
            xs1, xs2, xs3 = _rows_before(xr_halo, xrv, (1, 2, 3))
            xc = _conv(xrv, (xs1, xs2, xs3), pv_ref)
            xcb_scr[rows, :] = xc.astype(bf16)
            a, mult = a_ref[rows, :], mult_ref[rows, :]
            gr, gi = gr_ref[rows, :].astype(f32), gi_ref[rows, :].astype(f32)
            gelu_v = gel_ref[rows, :].astype(f32)

            dy = dxv * vec_ref[R_GT_M:R_GT_M + 1, :]
            dyb = dy.astype(bf16)
            dyb_scr[rows, :] = dyb
            sm_ref[G_GT:G_GT + 1, :] += _colsum(dxv * y_ref[rows, :].astype(f32))
            sm_ref[G_BOUT:G_BOUT + 1, :] += _colsum(dy)
            mb_scr[rows, :] = (hsv * gelu_v).astype(bf16)
            dm = _mm_nt(dyb, wb_ref[2])
            dhs = dm * gelu_v
            dgpre = dm * hsv * geld_ref[rows, :].astype(f32)
            dgb = dgpre.astype(bf16)
            dgb_scr[rows, :] = dgb
            sm_ref[G_BY:G_BY + 1, :] += _colsum(dgpre)

            eps_in = eps_ref[0:1, :]
            eps, eps_out = _scan_rows(a, a * dhs, eps_in, reverse=True)
            eps_ref[0:1, :] = eps_out
            (eps_next,) = _rows_after(eps, jnp.broadcast_to(eps_in, (SUBLANES, w)), (1,))
            delta = dhs + eps_next
            (h_prev,) = _rows_before(hs_halo, hsv, (1,))
            dxi = delta * xc
            dgi = dxi * mult
            dla = (delta * h_prev) * a - (dxi * gi) * (a * a) / mult
            sm_ref[G_LS:G_LS + 1, :] += _colsum(dla * gr)
            dra = (dla * c_ls) * (gr - gr * gr)
            drx = dgi * (gi - gi * gi)
            drab, drxb = dra.astype(bf16), drx.astype(bf16)
            drab_scr[rows, :] = drab
            drxb_scr[rows, :] = drxb
            sm_ref[G_BA:G_BA + 1, :] += _colsum(dra)
            sm_ref[G_BX:G_BX + 1, :] += _colsum(drx)
            dxc = (delta * mult) * gi + _block_diag_t(drab, wsm_ref, 0) + _block_diag_t(drxb, wsm_ref, 1)

            sm_ref[G_CONVB:G_CONVB + 1, :] += _colsum(dxc)
            for kk, v in zip((3, 2, 1, 0), (xrv, xs1, xs2, xs3)):
                sm_ref[G_CW0 + kk:G_CW0 + kk + 1, :] += _colsum(dxc * v)
            ups = _rows_after(dxc, dxc8_ref[...], (1, 2, 3))
            dxc8_ref[...] = dxc[0:SUBLANES, :]
            dxr = dxc * pv_ref[P_CW0 + 3:P_CW0 + 4, :]
            for kk, v in zip((2, 1, 0), ups):
                dxr = dxr + v * pv_ref[P_CW0 + kk:P_CW0 + kk + 1, :]
            dxrb = dxr.astype(bf16)
            dxrb_scr[rows, :] = dxrb
            sm_ref[G_BIN:G_BIN + 1, :] += _colsum(dxr)
            dh = _mm_nt(dgb, wb_ref[0]) + _mm_nt(dxrb, wb_ref[1])
            sm_ref[G_SH:G_SH + 1, :] += _colsum(dh)
            sm_ref[G_GS:G_GS + 1, :] += _colsum(dh * n)
            dxo_ref[rows, :] = dxv + _norm_bwd(dh, n, r, gs)

        hb = hb_scr[...]
        accb_ref[0] += _mm_tn(hb, dgb_scr[...])
        accb_ref[1] += _mm_tn(hb, dxrb_scr[...])
        accb_ref[2] += _mm_tn(mb_scr[...], dyb_scr[...])
        for h in range(HEADS):
            cols = slice(h * HEAD_DIM, (h + 1) * HEAD_DIM)
            accs_ref[0, h] += _mm_tn(xcb_scr[:, cols], drab_scr[:, cols])
            accs_ref[1, h] += _mm_tn(xcb_scr[:, cols], drxb_scr[:, cols])

        @pl.when(i == nt - 1)
        def _():
            sm_ref[G_LS:G_LS + 1, :] = sm_ref[G_LS:G_LS + 1, :] * LRU_C
            for k in range(3):
                dwb_ref[:, k] = accb_ref[k].astype(bf16).reshape(N_DEV, shard, w)
            for k in range(2):
                for h in range(HEADS):
                    dwsm_ref[:, k, h] = accs_ref[k, h].astype(bf16).reshape(N_DEV, hshard, HEAD_DIM)

    rev = lambda i: (nt - 1 - i, 0)
    row = pl.BlockSpec((ts, D_MODEL), rev)
    roww = pl.BlockSpec((ts, w), rev)
    halo16 = pl.BlockSpec((BF16_ROWS, w), lambda i: (jnp.maximum((nt - 1 - i) * (ts // BF16_ROWS) - 1, 0), 0))
    halo8 = pl.BlockSpec((SUBLANES, w), lambda i: (jnp.maximum((nt - 1 - i) * (ts // SUBLANES) - 1, 0), 0))
    const = lambda *shape: pl.BlockSpec(shape, lambda i: (0,) * len(shape))
    operand = pltpu.VMEM((ts, w), bf16)
    return pl.pallas_call(
        body, name=f"lru_bwd_{layer}", grid=(nt,),
        out_shape=(jax.ShapeDtypeStruct((s, D_MODEL), f32),
                   jax.ShapeDtypeStruct((N_DEV, 3, shard, w), bf16),
                   jax.ShapeDtypeStruct((N_DEV, 2, HEADS, hshard, HEAD_DIM), bf16),
                   jax.ShapeDtypeStruct((16, w), f32)),
        in_specs=[row, row, roww, halo16, roww, halo8, roww, roww, roww, roww, roww, roww, row, const(8, D_MODEL),
                  const(3, w, w), const(2, HEADS, HEAD_DIM, HEAD_DIM), const(16, w)],
        out_specs=(row, const(N_DEV, 3, shard, w), const(N_DEV, 2, HEADS, hshard, HEAD_DIM), const(16, w)),
        scratch_shapes=[pltpu.VMEM((3, w, w), f32), pltpu.VMEM((2, HEADS, HEAD_DIM, HEAD_DIM), f32),
                        pltpu.VMEM((SUBLANES, w), f32), pltpu.VMEM((SUBLANES, w), f32)] + [operand] * 8,
        compiler_params=_params(("arbitrary",), 58),
    )(x, dx, xr, xr, hs, hs, a_all, mult_all, gr_all, gi_all, gel_all, geld_all, y, vec, wbig, wsm, pvec)


def _pool_tile(s):
    return min(256, s)


def _pool_counts(tile_index, ts):
    t = (tile_index * ts + lax.broadcasted_iota(jnp.int32, (ts, 1), 0) + 1).astype(f32)
    return [1.0 / jnp.minimum(t, float(win)) for win in POOL_WINDOWS]


def _pooled(h, halo, inv):
    ext = jnp.concatenate([halo, h], axis=0)
    out = []
    for g in range(len(POOL_WINDOWS)):
        acc = ext[:, g * HEAD_DIM:(g + 1) * HEAD_DIM]
        for step in range(g + 1):
            acc = acc + pltpu.roll(acc, 1 << step, 0)
        out.append(acc[POOL_HALO:] * inv[g] - h[:, g * HEAD_DIM:(g + 1) * HEAD_DIM])
    return out


def _pool_fwd(x, vec, pw, ps, layer):
    s = x.shape[0]
    ts = _pool_tile(s)

    def body(x_ref, vec_ref, pw_ref, ps_ref, xo_ref, y_ref, halo_ref):
        i = pl.program_id(0)

        @pl.when(i == 0)
        def _():
            halo_ref[...] = jnp.zeros_like(halo_ref)

        xv = x_ref[...]
        n, _ = _rms(xv)
        h = n * vec_ref[R_GS_M:R_GS_M + 1, :] + vec_ref[R_SH_M:R_SH_M + 1, :]
        pooled = _pooled(h, halo_ref[...], _pool_counts(i, ts))
        halo_ref[...] = h[ts - POOL_HALO:, :]
        mixed = jnp.concatenate([_mm(pooled[g].astype(bf16), pw_ref[g]) for g in range(HEADS)], axis=1)
        yv = mixed * ps_ref[0:1, :]
        y_ref[...] = yv.astype(bf16)
        xo_ref[...] = xv + vec_ref[R_GT_M:R_GT_M + 1, :] * yv

    row = pl.BlockSpec((ts, D_MODEL), lambda i: (i, 0))
    return pl.pallas_call(
        body, name=f"pool_fwd_{layer}", grid=(s // ts,),
        out_shape=(jax.ShapeDtypeStruct((s, D_MODEL), f32), jax.ShapeDtypeStruct((s, D_MODEL), bf16)),
        in_specs=[row, pl.BlockSpec((8, D_MODEL), lambda i: (0, 0)),
                  pl.BlockSpec((HEADS, HEAD_DIM, HEAD_DIM), lambda i: (0, 0, 0)),
                  pl.BlockSpec((8, D_MODEL), lambda i: (0, 0))],
        out_specs=(row, row),
        scratch_shapes=[pltpu.VMEM((POOL_HALO, D_MODEL), f32)],
        compiler_params=_params(("arbitrary",)),
    )(x, vec, pw, ps)


def _pool_bwd(x, dx, y, vec, pw, ps, layer):
    s = x.shape[0]
    ts = _pool_tile(s)
    nt = s // ts
    hshard = HEAD_DIM // N_DEV

    def body(x_ref, xh_ref, dx_ref, y_ref, vec_ref, pw_ref, ps_ref, dxo_ref, dpw_ref, sm_ref, acc_ref, q16_ref):
        i = pl.program_id(0)
        tile = nt - 1 - i

        @pl.when(i == 0)
        def _():
            acc_ref[...] = jnp.zeros_like(acc_ref)
            sm_ref[...] = jnp.zeros_like(sm_ref)
            q16_ref[...] = jnp.zeros_like(q16_ref)

        gs, sh = vec_ref[R_GS_M:R_GS_M + 1, :], vec_ref[R_SH_M:R_SH_M + 1, :]
        xv = x_ref[...]
        dxv = dx_ref[...]
        n, r = _rms(xv)
        h = n * gs + sh
        nh, _ = _rms(xh_ref[...])
        halo = jnp.where(tile == 0, 0.0, nh * gs + sh)
        inv = _pool_counts(tile, ts)
        pooled = _pooled(h, halo, inv)
        mixed = jnp.concatenate([_mm(pooled[g].astype(bf16), pw_ref[g]) for g in range(HEADS)], axis=1)

        dy = dxv * vec_ref[R_GT_M:R_GT_M + 1, :]
        sm_ref[G_GT:G_GT + 1, :] += _colsum(dxv * y_ref[...].astype(f32))
        sm_ref[3:4, :] += _colsum(dy * mixed)
        dmixed = (dy * ps_ref[0:1, :]).astype(bf16)
        dh_parts = []
        for g in range(HEADS):
            cols = slice(g * HEAD_DIM, (g + 1) * HEAD_DIM)
            acc_ref[g] += _mm_tn(pooled[g].astype(bf16), dmixed[:, cols])
            dpooled = _mm_nt(dmixed[:, cols], pw_ref[g])
            q = dpooled * inv[g]
            ext = jnp.concatenate([q, q16_ref[:, cols]], axis=0)
            q16_ref[:, cols] = q[0:POOL_HALO, :]
            for step in range(g + 1):
                ext = ext + pltpu.roll(ext, ext.shape[0] - (1 << step), 0)
            dh_parts.append(ext[:ts] - dpooled)
        dh = jnp.concatenate(dh_parts, axis=1)
        sm_ref[G_SH:G_SH + 1, :] += _colsum(dh)
        sm_ref[G_GS:G_GS + 1, :] += _colsum(dh * n)
        dxo_ref[...] = dxv + _norm_bwd(dh, n, r, gs)

        @pl.when(i == nt - 1)
        def _():
            for g in range(HEADS):
                dpw_ref[:, g] = acc_ref[g].astype(bf16).reshape(N_DEV, hshard, HEAD_DIM)

    rev = lambda i: (nt - 1 - i, 0)
    row = pl.BlockSpec((ts, D_MODEL), rev)
    halo16 = pl.BlockSpec((POOL_HALO, D_MODEL), lambda i: (jnp.maximum((nt - 1 - i) * (ts // POOL_HALO) - 1, 0), 0))
    const = lambda *shape: pl.BlockSpec(shape, lambda i: (0,) * len(shape))
    return pl.pallas_call(
        body, name=f"pool_bwd_{layer}", grid=(nt,),
        out_shape=(jax.ShapeDtypeStruct((s, D_MODEL), f32),
                   jax.ShapeDtypeStruct((N_DEV, HEADS, hshard, HEAD_DIM), bf16),
                   jax.ShapeDtypeStruct((8, D_MODEL), f32)),
        in_specs=[row, halo16, row, row, const(8, D_MODEL), const(HEADS, HEAD_DIM, HEAD_DIM), const(8, D_MODEL)],
        out_specs=(row, const(N_DEV, HEADS, hshard, HEAD_DIM), const(8, D_MODEL)),
        scratch_shapes=[pltpu.VMEM((HEADS, HEAD_DIM, HEAD_DIM), f32), pltpu.VMEM((POOL_HALO, D_MODEL), f32)],
        compiler_params=_params(("arbitrary",)),
    )(x, x, dx, y, vec, pw, ps)


def _final(x, target, g_fin):
    s = x.shape[0]
    ts = min(512, s)

    def body(x_ref, t_ref, g_ref, dx_ref, loss_ref, sm_ref):
        @pl.when(pl.program_id(0) == 0)
        def _():
            loss_ref[...] = jnp.zeros_like(loss_ref)
            sm_ref[...] = jnp.zeros_like(sm_ref)

        g = g_ref[0:1, :]
        n, r = _rms(x_ref[...])
        err = n * g - t_ref[...]
        loss_ref[...] += 0.5 * jnp.sum(jnp.mean(err * err, axis=-1, keepdims=True), axis=0, keepdims=True)
        dyv = err * (1.0 / D_MODEL)
        sm_ref[0:1, :] += _colsum(dyv * n)
        dx_ref[...] = _norm_bwd(dyv, n, r, g)

    row = pl.BlockSpec((ts, D_MODEL), lambda i: (i, 0))
    return pl.pallas_call(
        body, name="final_loss", grid=(s // ts,),
        out_shape=(jax.ShapeDtypeStruct((s, D_MODEL), f32), jax.ShapeDtypeStruct((8, 128), f32),
                   jax.ShapeDtypeStruct((8, D_MODEL), f32)),
        in_specs=[row, row, pl.BlockSpec((8, D_MODEL), lambda i: (0, 0))],
        out_specs=(row, pl.BlockSpec((8, 128), lambda i: (0, 0)), pl.BlockSpec((8, D_MODEL), lambda i: (0, 0))),
        compiler_params=_params(("arbitrary",)),
    )(x, target, g_fin)


def _small_pack(sm_ffn, sm_mix, sm_fin, table, g_mix, g_ffn, lam):
    def body(*refs):
        ffn, mix = refs[0:DEPTH], refs[DEPTH:2 * DEPTH]
        fin_ref, tab_ref, gm_ref, gf_ref, lam_ref, o_ref = refs[2 * DEPTH:]
        o_ref[...] = jnp.zeros_like(o_ref)
        for i in range(DEPTH):
            base = K_MOD + i * N_MOD
            o_ref[base + 0:base + 1, :] = mix[i][G_SH:G_SH + 1, :]
            o_ref[base + 1:base + 2, :] = mix[i][G_GS:G_GS + 1, :] * gm_ref[i:i + 1, :]
            o_ref[base + 2:base + 3, :] = mix[i][G_GT:G_GT + 1, :]
            o_ref[base + 3:base + 4, :] = ffn[i][G_SH:G_SH + 1, :]
            o_ref[base + 4:base + 5, :] = ffn[i][G_GS:G_GS + 1, :] * gf_ref[i:i + 1, :]
            o_ref[base + 5:base + 6, :] = ffn[i][G_GT:G_GT + 1, :]
            o_ref[K_NMIX + i:K_NMIX + i + 1, :] = mix[i][G_GS:G_GS + 1, :] * (1.0 + tab_ref[i, R_SC_M:R_SC_M + 1, :])
            o_ref[K_NFFN + i:K_NFFN + i + 1, :] = ffn[i][G_GS:G_GS + 1, :] * (1.0 + tab_ref[i, R_SC_F:R_SC_F + 1, :])
            j = i // 2
            if i % 2 == 0:
                for k, src in enumerate((G_BY, G_BIN, G_CONVB, None, G_BOUT)):
                    dst = K_LRUB + j * 5 + k
                    if src is None:
                        o_ref[dst:dst + 1, :] = mix[i][G_LS:G_LS + 1, :] * _sigmoid(-lam_ref[j:j + 1, :])
                    else:
                        o_ref[dst:dst + 1, :] = mix[i][src:src + 1, :]
                o_ref[K_CONVW + j * 4:K_CONVW + j * 4 + 4, :] = mix[i][G_CW0:G_CW0 + 4, :]
                o_ref[K_BA + j:K_BA + j + 1, :] = mix[i][G_BA:G_BA + 1, :]
                o_ref[K_BX + j:K_BX + j + 1, :] = mix[i][G_BX:G_BX + 1, :]
            else:
                o_ref[K_PS + j:K_PS + j + 1, :] = mix[i][3:4, :]
        o_ref[K_FIN:K_FIN + 1, :] = fin_ref[0:1, :]

    return pl.pallas_call(body, name="small_pack", out_shape=jax.ShapeDtypeStruct((K_ROWS, D_MODEL), f32))(
        *sm_ffn, *sm_mix, sm_fin, table, g_mix, g_ffn, lam)


def _small_sum(gathered):
    def body(g_ref, o_ref, token_ref):
        tot = g_ref[0]
        for src in range(1, N_DEV):
            tot = tot + g_ref[src]
        o_ref[...] = tot
        token_ref[...] = jnp.zeros_like(token_ref)

    return pl.pallas_call(
        body, name="small_sum",
        out_shape=(jax.ShapeDtypeStruct(gathered.shape[1:], f32), jax.ShapeDtypeStruct((8, 128), f32)))(gathered)


def _adamw_math(g, w, m, v):
    m = ADAM_B1 * m + (1.0 - ADAM_B1) * g
    v = ADAM_B2 * v + (1.0 - ADAM_B2) * (g * g)
    m_hat = m / (1.0 - ADAM_B1 ** ADAM_STEP)
    v_hat = v / (1.0 - ADAM_B2 ** ADAM_STEP)
    delta = -ADAM_LR * (m_hat / (jnp.sqrt(v_hat) + ADAM_EPS) + ADAM_WD * w)
    return delta, m, v


def _adamw_small(name, g, w, m, v):
    shape = w.shape
    two_d = (1, shape[0]) if len(shape) == 1 else (math.prod(shape[:-1]), shape[-1])

    def body(g_ref, w_ref, m_ref, v_ref, d_ref, mo_ref, vo_ref):
        d_ref[...], mo_ref[...], vo_ref[...] = _adamw_math(g_ref[...], w_ref[...], m_ref[...], v_ref[...])

    outs = pl.pallas_call(body, name=f"adamw_{name}", out_shape=tuple(jax.ShapeDtypeStruct(two_d, f32) for _ in range(3)))(
        *(t.reshape(two_d) for t in (g, w, m, v)))
    return tuple(t.reshape(shape) for t in outs)


def _block_rows(rows, cols):
    tr = max(SUBLANES, min(rows, (512 * 1024) // (4 * cols)))
    while rows % tr:
        tr //= 2
    return tr


def _adamw_reduce(name, landings, kind, w, m, v):
    nl = len(landings)
    rows, cols = landings[0].shape[2:]
    tr = _block_rows(rows, cols)
    per_layer = rows // tr

    def body(*refs):
        l_refs = refs[:nl]
        w_ref, m_ref, v_ref, g_ref, d_ref, mo_ref, vo_ref = refs[nl:]
        layer = pl.program_id(0)
        for k in range(nl):
            @pl.when(layer == k)
            def _(k=k):
                g = l_refs[k][0].astype(f32)
                for src in range(1, N_DEV):
                    g = g + l_refs[k][src].astype(f32)
                g_ref[...] = g
        d_ref[...], mo_ref[...], vo_ref[...] = _adamw_math(g_ref[...], w_ref[...], m_ref[...], v_ref[...])

    blk = pl.BlockSpec((tr, cols), lambda l, r: (l * per_layer + r, 0))
    land = [pl.BlockSpec((N_DEV, None, tr, cols), lambda l, r, k=k: (0, kind, jnp.where(l == k, r, 0), 0)) for k in range(nl)]
    return pl.pallas_call(
        body, name=f"adamw_{name}", grid=(nl, per_layer),
        out_shape=tuple(jax.ShapeDtypeStruct((nl * rows, cols), f32) for _ in range(4)),
        in_specs=land + [blk, blk, blk],
        out_specs=(blk, blk, blk, blk),
        compiler_params=_params(("arbitrary", "arbitrary"), 32),
    )(*landings, w, m, v)


def _adamw_w_mod(c_all, dmod_all, w, m, v):
    depth, d, cols = w.shape
    tr = 256

    def body(c_ref, dm_ref, w_ref, m_ref, v_ref, g_ref, d_ref, mo_ref, vo_ref):
        cv = c_ref[...]
        cond = cv * _sigmoid(cv)
        g = lax.dot_general(cond, dm_ref[...], (((0,), (0,)), ((), ())), preferred_element_type=f32,
                            precision=lax.Precision.HIGHEST)
        g_ref[...] = g
        d_ref[...], mo_ref[...], vo_ref[...] = _adamw_math(g, w_ref[...], m_ref[...], v_ref[...])

    blk = pl.BlockSpec((None, tr, cols), lambda i, r: (i, r, 0))
    return pl.pallas_call(
        body, name="adamw_w_mod", grid=(depth, d // tr),
        out_shape=tuple(jax.ShapeDtypeStruct(w.shape, f32) for _ in range(4)),
        in_specs=[pl.BlockSpec((N_DEV, tr), lambda i, r: (0, r)),
                  pl.BlockSpec((None, N_DEV, cols), lambda i, r: (i, 0, 0)), blk, blk, blk],
        out_specs=(blk, blk, blk, blk),
        compiler_params=_params(("arbitrary", "arbitrary"), 32),
    )(c_all, dmod_all, w, m, v)


def kernel(x, c, w_mod, b_mod, norm_mix_g, norm_ffn_g, lru_w_y, lru_b_y, lru_w_in, lru_b_in, lru_conv_w, lru_conv_b, lru_w_a, lru_b_a, lru_w_x, lru_b_x, lru_lambda, lru_w_out, lru_b_out, pool_w, pool_scale, ffn_w1, ffn_w2, final_norm_g, loss_target, m_w_mod, m_b_mod, m_norm_mix_g, m_norm_ffn_g, m_lru_w_y, m_lru_b_y, m_lru_w_in, m_lru_b_in, m_lru_conv_w, m_lru_conv_b, m_lru_w_a, m_lru_b_a, m_lru_w_x, m_lru_b_x, m_lru_lambda, m_lru_w_out, m_lru_b_out, m_pool_w, m_pool_scale, m_ffn_w1, m_ffn_w2, m_final_norm_g, v_w_mod, v_b_mod, v_norm_mix_g, v_norm_ffn_g, v_lru_w_y, v_lru_b_y, v_lru_w_in, v_lru_b_in, v_lru_conv_w, v_lru_conv_b, v_lru_w_a, v_lru_b_a, v_lru_w_x, v_lru_b_x, v_lru_lambda, v_lru_w_out, v_lru_b_out, v_pool_w, v_pool_scale, v_ffn_w1, v_ffn_w2, v_final_norm_g):
    me = 4 * lax.axis_index("x") + 2 * lax.axis_index("y") + lax.axis_index("c")
    n_lru = lru_w_y.shape[0]
    shard = LRU_WIDTH // N_DEV
    hshard = HEAD_DIM // N_DEV
    xs = x[0]
    target = loss_target[0]

    small_vecs = jnp.concatenate([
        lru_conv_w.reshape(n_lru * 4, shard), lru_b_a.reshape(n_lru, HEADS * hshard),
        lru_b_x.reshape(n_lru, HEADS * hshard), pool_scale, jnp.zeros((2, shard), f32)], axis=0)
    sv_g, c_g = _exchange([small_vecs, c], True, "gather_cond")
    conv_w_full = sv_g[:, 0:8].reshape(N_DEV, n_lru, 4, shard).transpose(1, 2, 0, 3).reshape(n_lru, 4, LRU_WIDTH)
    b_a_full = sv_g[:, 8:10].reshape(N_DEV, n_lru, HEADS, hshard).transpose(1, 2, 0, 3).reshape(n_lru, LRU_WIDTH)
    b_x_full = sv_g[:, 10:12].reshape(N_DEV, n_lru, HEADS, hshard).transpose(1, 2, 0, 3).reshape(n_lru, LRU_WIDTH)
    ps_full = sv_g[:, 12:14].transpose(1, 0, 2).reshape(n_lru, D_MODEL)
    c_all = c_g.reshape(N_DEV, D_MODEL)

    (mod_g,) = _exchange([_mod_part(c_all, w_mod)], True, "gather_mod", pieces=DEPTH)
    mod_row = lax.dynamic_index_in_dim(mod_g, me, axis=2, keepdims=False)
    mod_row = mod_row.transpose(1, 0, 2).reshape(DEPTH, N_MOD * D_MODEL)
    table, token = _mod_table(mod_row, b_mod, norm_mix_g, norm_ffn_g)

    first_pieces = 4
    (first_mix,), token = _send_start("gather_first_start", [[
        jnp.stack([lru_w_y[0], lru_w_in[0], lru_w_out[0]]).astype(bf16).reshape(3 * first_pieces, -1, LRU_WIDTH),
        jnp.stack([lru_w_a[0], lru_w_x[0]]).astype(bf16).reshape(first_pieces, -1, HEAD_DIM)]], True, me,
        pieces=first_pieces, after=token)

    parts = []
    for i in range(DEPTH):
        j = i // 2
        if i > 0 and i % 2 == 0:
            parts.append([(jnp.stack([lru_w_y[j], lru_w_in[j], lru_w_out[j]]) + token[0, 0]).astype(bf16),
                          (jnp.stack([lru_w_a[j], lru_w_x[j]]) + token[0, 0]).astype(bf16)])
        elif i % 2 == 1:
            parts.append([(pool_w[j] + token[0, 0]).astype(bf16)])
        parts.append([(ffn_w1[i] + token[0, 0]).astype(bf16), (ffn_w2[i] + token[0, 0]).astype(bf16)])
    first_got, token = _send_wait("gather_mix_wait_0", first_mix, parts[-1][-1])
    handles, token = _send_start("gather_rest_start", parts, True, me, after=token)
    h_ffn = [handles[0], handles[2], handles[4], handles[6]]
    h_mix = [None, handles[1], handles[3], handles[5]]

    zero_row = jnp.zeros((1, LRU_WIDTH), f32)
    pvecs = [jnp.concatenate([lru_b_y[j:j + 1], lru_b_in[j:j + 1], lru_conv_b[j:j + 1], b_a_full[j:j + 1],
                              b_x_full[j:j + 1], lru_lambda[j:j + 1], lru_b_out[j:j + 1], zero_row,
                              conv_w_full[j], zero_row, zero_row, zero_row, zero_row], axis=0) for j in range(n_lru)]
    ps_rows = [jnp.concatenate([ps_full[j:j + 1], jnp.zeros((7, D_MODEL), f32)], axis=0) for j in range(n_lru)]

    saved = []
    ffn_w, mix_w = [], []
    h = xs
    for i in range(DEPTH):
        j = i // 2
        got = first_got if i == 0 else _send_wait(f"gather_mix_wait_{i}", h_mix[i], h)[0]
        if i % 2 == 0:
            got = [got[0].reshape(N_DEV, 3, shard, LRU_WIDTH), got[1].reshape(N_DEV, 2, HEADS, hshard, HEAD_DIM)]
            mix_w.append((got[0].transpose(1, 0, 2, 3).reshape(3, LRU_WIDTH, LRU_WIDTH),
                          got[1].transpose(1, 2, 0, 3, 4).reshape(2, HEADS, HEAD_DIM, HEAD_DIM)))
            h_mid, *lru_saved = _lru_fwd(h, table[i] + token[0, 0], mix_w[i][0], mix_w[i][1], pvecs[j], i)
            mix_saved = (h, tuple(lru_saved))
        else:
            mix_w.append((got[0].transpose(1, 0, 2, 3).reshape(HEADS, HEAD_DIM, HEAD_DIM),))
            h_mid, y_mix = _pool_fwd(h, table[i], mix_w[i][0], ps_rows[j], i)
            mix_saved = (h, y_mix)
        ffn_w.append(_send_wait(f"gather_ffn_wait_{i}", h_ffn[i], h_mid)[0])
        h_out, u, y_ffn, hb = _ffn_fwd(h_mid, table[i], ffn_w[i][0], ffn_w[i][1], i)
        saved.append((mix_saved, (h_mid, u, y_ffn, hb)))
        h = h_out
    fin_rows = jnp.concatenate([final_norm_g[None, :], jnp.zeros((7, D_MODEL), f32)], axis=0)
    dx, loss_part, sm_fin = _final(h, target, fin_rows)
    loss = lax.psum(loss_part[0, 0], ("x", "y", "c"))

    sm_ffn, sm_mix = [None] * DEPTH, [None] * DEPTH
    x_ffn, x_mix = [None] * DEPTH, [None] * DEPTH
    token = jnp.zeros((8, 128), f32)
    last_mix = None
    for i in reversed(range(DEPTH)):
        j = i // 2
        mix_saved, (h_mid, u, y_ffn, hb) = saved[i]
        dx, da, dyb, sm_ffn[i] = _ffn_bwd_act(h_mid, dx, u, y_ffn, table[i] + token[0, 0], ffn_w[i][0], ffn_w[i][1], i)
        ffn_grads = [_ffn_bwd_w1(hb, da, i), _ffn_bwd_w2(u, dyb, i)]
        if last_mix is None:
            (x_ffn[i],), token = _send_start(f"grads_start_{i}", [ffn_grads], False, me)
        else:
            (x_mix[i + 1], x_ffn[i]), token = _send_start(f"grads_start_{i}", [last_mix, ffn_grads], False, me)
        if i % 2 == 0:
            h_in, lru_saved = mix_saved
            dx, dbig, dsmall, sm_mix[i] = _lru_bwd(
                h_in, dx, lru_saved, table[i] + token[0, 0], mix_w[i][0], mix_w[i][1], pvecs[j], i)
            last_mix = [dbig, dsmall]
        else:
            h_in, y_mix = mix_saved
            dx, dpool, sm = _pool_bwd(h_in, dx, y_mix, table[i] + token[0, 0], mix_w[i][0], ps_rows[j], i)
            sm_mix[i] = jnp.concatenate([sm, jnp.zeros((8, D_MODEL), f32)], axis=0)
            last_mix = [dpool]
    grad_x = dx[None]

    pack = _small_pack(sm_ffn, sm_mix, sm_fin, table + token[0, 0], norm_mix_g, norm_ffn_g, lru_lambda)
    (pack_g,) = _exchange([pack], True, "gather_small_grads", pieces=4)
    tot, token = _small_sum(pack_g)
    (x_mix[0],), _ = _send_start("grads_last_start", [[t + token[0, 0].astype(bf16) for t in last_mix]], False, me)
    cols = w_mod.shape[2]
    dmod_all = lax.dynamic_slice_in_dim(pack_g[:, K_MOD:K_MOD + DEPTH * N_MOD].reshape(N_DEV, DEPTH, N_MOD * D_MODEL),
                                        me * cols, cols, axis=2).transpose(1, 0, 2)
    results = {"w_mod": _adamw_w_mod(c_all, dmod_all, w_mod, m_w_mod, v_w_mod)}

    after = results["w_mod"][1]
    l_ffn = [_send_wait(f"grads_ffn_wait_{i}", x_ffn[i], after)[0] for i in reversed(range(DEPTH))][::-1]

    def reduce_update(name, landings, kind, w, m, v):
        rows = w.size // w.shape[-1]
        two_d = (rows, w.shape[-1])
        lands = [t.reshape(N_DEV, -1, rows // len(landings), w.shape[-1]) for t in landings]
        outs = _adamw_reduce(name, lands, kind, w.reshape(two_d), m.reshape(two_d), v.reshape(two_d))
        return tuple(t.reshape(w.shape) for t in outs)

    results["ffn_w1"] = reduce_update("ffn_w1", [t[0] for t in l_ffn], 0, ffn_w1, m_ffn_w1, v_ffn_w1)
    results["ffn_w2"] = reduce_update("ffn_w2", [t[1] for t in l_ffn], 0, ffn_w2, m_ffn_w2, v_ffn_w2)
    after = results["ffn_w2"][1]
    l_mix = [_send_wait(f"grads_mix_wait_{i}", x_mix[i], after)[0] for i in reversed(range(DEPTH))][::-1]
    l_lru_big = [l_mix[i][0] for i in range(0, DEPTH, 2)]
    l_lru_small = [l_mix[i][1] for i in range(0, DEPTH, 2)]
    l_pool = [l_mix[i][0] for i in range(1, DEPTH, 2)]
    results["lru_w_y"] = reduce_update("lru_w_y", l_lru_big, 0, lru_w_y, m_lru_w_y, v_lru_w_y)
    results["lru_w_in"] = reduce_update("lru_w_in", l_lru_big, 1, lru_w_in, m_lru_w_in, v_lru_w_in)
    results["lru_w_out"] = reduce_update("lru_w_out", l_lru_big, 2, lru_w_out, m_lru_w_out, v_lru_w_out)
    results["lru_w_a"] = reduce_update("lru_w_a", l_lru_small, 0, lru_w_a, m_lru_w_a, v_lru_w_a)
    results["lru_w_x"] = reduce_update("lru_w_x", l_lru_small, 1, lru_w_x, m_lru_w_x, v_lru_w_x)
    results["pool_w"] = reduce_update("pool_w", l_pool, 0, pool_w, m_pool_w, v_pool_w)

    def my_cols(full, width):
        return lax.dynamic_slice_in_dim(full, me * width, width, axis=full.ndim - 1)

    lru_rows = tot[K_LRUB:K_LRUB + 5 * n_lru].reshape(n_lru, 5, LRU_WIDTH)
    small_grads = {
        "b_mod": tot[K_MOD:K_MOD + DEPTH * N_MOD].reshape(DEPTH, N_MOD * D_MODEL),
        "norm_mix_g": tot[K_NMIX:K_NMIX + DEPTH],
        "norm_ffn_g": tot[K_NFFN:K_NFFN + DEPTH],
        "lru_b_y": lru_rows[:, 0], "lru_b_in": lru_rows[:, 1], "lru_conv_b": lru_rows[:, 2],
        "lru_lambda": lru_rows[:, 3], "lru_b_out": lru_rows[:, 4],
        "lru_conv_w": my_cols(tot[K_CONVW:K_CONVW + 4 * n_lru].reshape(n_lru, 4, LRU_WIDTH), shard),
        "lru_b_a": my_cols(tot[K_BA:K_BA + n_lru].reshape(n_lru, HEADS, HEAD_DIM), hshard),
        "lru_b_x": my_cols(tot[K_BX:K_BX + n_lru].reshape(n_lru, HEADS, HEAD_DIM), hshard),
        "pool_scale": my_cols(tot[K_PS:K_PS + n_lru], shard),
        "final_norm_g": tot[K_FIN],
    }
    given = dict(b_mod=(b_mod, m_b_mod, v_b_mod), norm_mix_g=(norm_mix_g, m_norm_mix_g, v_norm_mix_g),
                 norm_ffn_g=(norm_ffn_g, m_norm_ffn_g, v_norm_ffn_g), lru_b_y=(lru_b_y, m_lru_b_y, v_lru_b_y),
                 lru_b_in=(lru_b_in, m_lru_b_in, v_lru_b_in), lru_conv_w=(lru_conv_w, m_lru_conv_w, v_lru_conv_w),
                 lru_conv_b=(lru_conv_b, m_lru_conv_b, v_lru_conv_b), lru_b_a=(lru_b_a, m_lru_b_a, v_lru_b_a),
                 lru_b_x=(lru_b_x, m_lru_b_x, v_lru_b_x), lru_lambda=(lru_lambda, m_lru_lambda, v_lru_lambda),
                 lru_b_out=(lru_b_out, m_lru_b_out, v_lru_b_out), pool_scale=(pool_scale, m_pool_scale, v_pool_scale),
                 final_norm_g=(final_norm_g, m_final_norm_g, v_final_norm_g))
    for name, g in small_grads.items():
        results[name] = (g,) + _adamw_small(name, g, *given[name])

    order = ["w_mod", "b_mod", "norm_mix_g", "norm_ffn_g", "lru_w_y", "lru_b_y", "lru_w_in", "lru_b_in", "lru_conv_w",
             "lru_conv_b", "lru_w_a", "lru_b_a", "lru_w_x", "lru_b_x", "lru_lambda", "lru_w_out", "lru_b_out", "pool_w",
             "pool_scale", "ffn_w1", "ffn_w2", "final_norm_g"]
    return (loss, grad_x, *[results[n][0] for n in order], *[results[n][1] for n in order],
            *[results[n][2] for n in order], *[results[n][3] for n in order])
```

```python
import math

import jax
import jax.numpy as jnp
from jax import lax
from jax.experimental import pallas as pl
from jax.experimental.pallas import tpu as pltpu

f32, bf16 = jnp.float32, jnp.bfloat16

D_MODEL = 1024
LRU_WIDTH = 1024
HEADS = 4
HEAD_DIM = 256
D_FF = 4096
DEPTH = 4
N_MOD = 6
N_DEV = 8
FF_CHUNK = D_FF // N_DEV
POOL_WINDOWS = (2, 4, 8, 16)
POOL_HALO = 16
EPS = 1e-6
LRU_C = 8.0

ADAM_LR = 0.001
ADAM_B1 = 0.9
ADAM_B2 = 0.999
ADAM_EPS = 1e-08
ADAM_WD = 0.01
ADAM_STEP = 10

SUBLANES = 8
BF16_ROWS = 16

R_SH_M, R_SC_M, R_GT_M, R_SH_F, R_SC_F, R_GT_F, R_GS_M, R_GS_F = range(8)
P_BY, P_BIN, P_CONVB, P_BA, P_BX, P_LAM, P_BOUT, P_CW0 = 0, 1, 2, 3, 4, 5, 6, 8
G_SH, G_GS, G_GT, G_BY, G_BIN, G_CONVB, G_BA, G_BX, G_LS, G_BOUT, G_CW0 = 0, 1, 2, 3, 4, 5, 6, 7, 8, 9, 10
K_MOD, K_NMIX, K_NFFN, K_LRUB, K_CONVW, K_BA, K_BX, K_PS, K_FIN, K_LOSS, K_ROWS = 0, 24, 28, 32, 42, 50, 52, 54, 56, 57, 64


def _params(semantics=None, vmem_mb=48):
    return pltpu.CompilerParams(dimension_semantics=semantics, vmem_limit_bytes=vmem_mb * 1024 * 1024)


def _mm(a, b):
    return jnp.dot(a, b, preferred_element_type=f32)


def _mm_nt(a, b):
    return lax.dot_general(a, b, (((1,), (1,)), ((), ())), preferred_element_type=f32)


def _mm_tn(a, b):
    return lax.dot_general(a, b, (((0,), (0,)), ((), ())), preferred_element_type=f32)


def _rms(x):
    r = lax.rsqrt(jnp.mean(x * x, axis=-1, keepdims=True) + EPS)
    return x * r, r


def _norm_bwd(dh, n, r, gs):
    dn = dh * gs
    return r * (dn - n * jnp.mean(dn * n, axis=-1, keepdims=True))


def _colsum(v):
    return jnp.sum(v, axis=0, keepdims=True)


def _sigmoid(v):
    return 0.5 * jnp.tanh(0.5 * v) + 0.5


def _log_sigmoid(v):
    return jnp.minimum(v, 0.0) - jnp.log1p(jnp.exp(-jnp.abs(v)))


_GELU_C = 0.7978845608028654
_GELU_A = 0.044715


def _gelu_and_grad(v):
    v2 = v * v
    t = jnp.tanh(_GELU_C * v * (1.0 + _GELU_A * v2))
    p = 0.5 + 0.5 * t
    return v * p, p + (0.5 * v) * (1.0 - t * t) * (_GELU_C + (3.0 * _GELU_A * _GELU_C) * v2)


def _rows_before(halo, v, shifts):
    hr = halo.shape[0]
    ext = jnp.concatenate([halo, v], axis=0)
    return [pltpu.roll(ext, k, 0)[hr:] for k in shifts]


def _rows_after(v, halo, shifts):
    n = v.shape[0]
    ext = jnp.concatenate([v, halo], axis=0)
    return [pltpu.roll(ext, ext.shape[0] - k, 0)[:n] for k in shifts]


def _shift_matrix(n, halo_rows, shifts):
    rows = lax.broadcasted_iota(jnp.int32, (n, n + halo_rows), 0)
    cols = lax.broadcasted_iota(jnp.int32, (n, n + halo_rows), 1)
    return jnp.concatenate([(cols == rows + halo_rows - k).astype(bf16) for k in shifts], axis=0)


def _shifted_rows(sel, halo, v):
    n = v.shape[0]
    out = _mm(sel, jnp.concatenate([halo, v], axis=0))
    return [out[j * n:(j + 1) * n] for j in range(sel.shape[0] // n)]


def _block_diag(v, w_ref, kind):
    return jnp.concatenate(
        [_mm(v[:, h * HEAD_DIM:(h + 1) * HEAD_DIM], w_ref[kind, h]) for h in range(HEADS)], axis=1)


def _block_diag_t(v, w_ref, kind):
    return jnp.concatenate(
        [_mm_nt(v[:, h * HEAD_DIM:(h + 1) * HEAD_DIM], w_ref[kind, h]) for h in range(HEADS)], axis=1)


def _exchange(arrays, gather, name, pieces=1):
    n = len(arrays)
    peers = N_DEV - 1

    def body(*refs):
        ins, outs = refs[:n], refs[n:2 * n]
        send_sems, recv_sems, local_sems = refs[2 * n:]
        x, y, c = lax.axis_index("x"), lax.axis_index("y"), lax.axis_index("c")
        me = 4 * x + 2 * y + c
        local = []
        for k in range(n):
            cp = pltpu.make_async_copy(ins[k] if gather else ins[k].at[me], outs[k].at[me], local_sems.at[k])
            cp.start()
            local.append(cp)
        remote = _peer_copies(ins, outs, send_sems, recv_sems, gather, pieces)
        for cp in remote:
            cp.start()
        for cp in remote:
            cp.wait()
        for cp in local:
            cp.wait()

    out_shape = tuple(
        jax.ShapeDtypeStruct(((N_DEV,) + a.shape) if gather else a.shape, a.dtype) for a in arrays)
    outs = pl.pallas_call(
        body, name=name, out_shape=out_shape,
        in_specs=[pl.BlockSpec(memory_space=pl.ANY)] * n,
        out_specs=tuple(pl.BlockSpec(memory_space=pl.ANY) for _ in range(n)),
        scratch_shapes=[pltpu.SemaphoreType.DMA((n * pieces * peers,)), pltpu.SemaphoreType.DMA((n * pieces * peers,)),
                        pltpu.SemaphoreType.DMA((n,))],
        compiler_params=pltpu.CompilerParams(has_side_effects=True),
    )(*arrays)
    return list(outs)


_HBM = pl.BlockSpec(memory_space=pltpu.HBM)
_SEM = pl.BlockSpec(memory_space=pltpu.SEMAPHORE)
_DATAFLOW = pltpu.SideEffectType.DATAFLOW_SIDE_EFFECTING


def _peer_copies(src_refs, land_refs, send_sems, recv_sems, gather, pieces=1):
    x, y, c = lax.axis_index("x"), lax.axis_index("y"), lax.axis_index("c")
    me = 4 * x + 2 * y + c
    peers = N_DEV - 1
    copies = []
    for p in range(1, N_DEV):
        px = 1 - x if p & 4 else x
        py = 1 - y if p & 2 else y
        pc = 1 - c if p & 1 else c
        for k in range(len(src_refs)):
            block = src_refs[k] if gather else src_refs[k].at[4 * px + 2 * py + pc]
            dst = land_refs[k].at[me]
            rows = block.shape[0] // pieces
            for r in range(pieces):
                part = pl.ds(r * rows, rows)
                sem = (k * pieces + r) * peers + p - 1
                copies.append(pltpu.make_async_remote_copy(
                    src_ref=block.at[part] if pieces > 1 else block, dst_ref=dst.at[part] if pieces > 1 else dst,
                    send_sem=send_sems.at[sem], recv_sem=recv_sems.at[sem],
                    device_id=(px, py, pc), device_id_type=pl.DeviceIdType.MESH))
    return copies


def _landing(srcs, gather, me):
    out = []
    for a in srcs:
        own = a if gather else lax.dynamic_index_in_dim(a, me, 0, keepdims=False)
        out.append(lax.dynamic_update_index_in_dim(lax.empty((N_DEV,) + own.shape, own.dtype), own, me, 0))
    return out


def _send_start(name, groups, gather, me, pieces=1, after=None):
    sizes = [len(g) for g in groups]
    srcs = [a for g in groups for a in g]
    n = len(srcs)
    lands = _landing(srcs, gather, me)
    ng = len(groups)
    first = [sum(sizes[:g]) for g in range(ng)]
    extra = [] if after is None else [after]

    def body(*refs):
        src_refs, land_refs = refs[:n], refs[n:2 * n]
        sems, token = refs[2 * n + len(extra):2 * n + len(extra) + 2 * ng], refs[-1]
        for g in range(ng):
            part = slice(first[g], first[g] + sizes[g])
            for cp in _peer_copies(src_refs[part], land_refs[part], sems[2 * g], sems[2 * g + 1], gather, pieces):
                cp.start()
        token[...] = jnp.zeros_like(token)

    sem_shapes = [pltpu.SemaphoreType.DMA((sizes[g // 2] * pieces * (N_DEV - 1),)) for g in range(2 * ng)]
    outs = pl.pallas_call(
        body, name=name,
        out_shape=(*sem_shapes, *[pltpu.HBM(a.shape, a.dtype) for a in (*srcs, *lands)], jax.ShapeDtypeStruct((8, 128), f32)),
        in_specs=[_HBM] * (2 * n) + [pl.BlockSpec(memory_space=pl.ANY)] * len(extra),
        out_specs=(*[_SEM] * (2 * ng), *[_HBM] * (2 * n), pl.BlockSpec(memory_space=pltpu.VMEM)),
        input_output_aliases={k: 2 * ng + k for k in range(2 * n)},
        compiler_params=pltpu.CompilerParams(has_side_effects=_DATAFLOW),
    )(*[pltpu.with_memory_space_constraint(a, pltpu.HBM) for a in (*srcs, *lands)], *extra)
    srcs_thru, lands_thru = outs[2 * ng:2 * ng + n], outs[2 * ng + n:2 * ng + 2 * n]
    handles = [(outs[2 * g], outs[2 * g + 1], list(srcs_thru[first[g]:first[g] + sizes[g]]),
                list(lands_thru[first[g]:first[g] + sizes[g]]), gather, pieces) for g in range(ng)]
    return handles, outs[-1]


def _send_wait(name, handle, after):
    send_sems, recv_sems, srcs, lands, gather, pieces = handle
    n = len(srcs)
    after = list(after) if isinstance(after, (list, tuple)) else [after]

    def body(*refs):
        src_refs, land_refs = refs[:n], refs[n:2 * n]
        for cp in _peer_copies(src_refs, land_refs, refs[2 * n], refs[2 * n + 1], gather, pieces):
            cp.wait_send()
            cp.wait_recv()
        refs[-1][...] = jnp.zeros_like(refs[-1])

    outs = pl.pallas_call(
        body, name=name,
        out_shape=(*[pltpu.HBM(a.shape, a.dtype) for a in (*srcs, *lands)], jax.ShapeDtypeStruct((8, 128), f32)),
        in_specs=[_HBM] * (2 * n) + [_SEM, _SEM] + [pl.BlockSpec(memory_space=pl.ANY)] * len(after),
        out_specs=(*[_HBM] * (2 * n), pl.BlockSpec(memory_space=pltpu.VMEM)),
        input_output_aliases={k: k for k in range(2 * n)},
        compiler_params=pltpu.CompilerParams(has_side_effects=_DATAFLOW),
    )(*srcs, *lands, send_sems, recv_sems, *after)
    return list(outs[n:2 * n]), outs[-1]


def _mod_part(c_all, w_mod):
    depth, d, cols = w_mod.shape

    def body(c_ref, w_ref, o_ref):
        cv = c_ref[...]
        cond = cv * _sigmoid(cv)
        o_ref[...] = jnp.dot(cond, w_ref[...], preferred_element_type=f32, precision=lax.Precision.HIGHEST)

    return pl.pallas_call(
        body, name="mod_part", grid=(depth,),
        out_shape=jax.ShapeDtypeStruct((depth, N_DEV, cols), f32),
        in_specs=[pl.BlockSpec((N_DEV, d), lambda i: (0, 0)), pl.BlockSpec((None, d, cols), lambda i: (i, 0, 0))],
        out_specs=pl.BlockSpec((None, N_DEV, cols), lambda i: (i, 0, 0)),
        compiler_params=_params(("arbitrary",), 32),
    )(c_all, w_mod)


def _mod_table(mod_row, b_mod, g_mix, g_ffn):
    def body(m_ref, b_ref, gm_ref, gf_ref, o_ref, token_ref):
        for i in range(DEPTH):
            for k in range(N_MOD):
                o_ref[i, k:k + 1, :] = m_ref[i:i + 1, k * D_MODEL:(k + 1) * D_MODEL] + b_ref[i:i + 1, k * D_MODEL:(k + 1) * D_MODEL]
            o_ref[i, R_GS_M:R_GS_M + 1, :] = gm_ref[i:i + 1, :] * (1.0 + o_ref[i, R_SC_M:R_SC_M + 1, :])
            o_ref[i, R_GS_F:R_GS_F + 1, :] = gf_ref[i:i + 1, :] * (1.0 + o_ref[i, R_SC_F:R_SC_F + 1, :])
        token_ref[...] = jnp.zeros_like(token_ref)

    return pl.pallas_call(
        body, name="mod_table",
        out_shape=(jax.ShapeDtypeStruct((DEPTH, 8, D_MODEL), f32), jax.ShapeDtypeStruct((8, 128), f32)))(
        mod_row, b_mod, g_mix, g_ffn)


def _ffn_tile(s):
    return min(512, s)


def _layer_weights(shape):
    return pl.BlockSpec((N_DEV,) + shape, lambda i: (0, 0, 0))


def _ffn_fwd(x, vec, w1g, w2g, layer):
    s = x.shape[0]
    ts = _ffn_tile(s)

    def body(x_ref, vec_ref, w1_ref, w2_ref, xo_ref, u_ref, y_ref, hb_ref):
        xv = x_ref[...]
        n, _ = _rms(xv)
        hb = (n * vec_ref[R_GS_F:R_GS_F + 1, :] + vec_ref[R_SH_F:R_SH_F + 1, :]).astype(bf16)
        hb_ref[...] = hb
        yv = jnp.zeros((ts, D_MODEL), f32)
        for f in range(N_DEV):
            u = jnp.maximum(_mm(hb, w1_ref[f]), 0.0)
            u_ref[:, f * FF_CHUNK:(f + 1) * FF_CHUNK] = u.astype(bf16)
            yv = yv + _mm((u * u).astype(bf16), w2_ref[f])
        y_ref[...] = yv.astype(bf16)
        xo_ref[...] = xv + vec_ref[R_GT_F:R_GT_F + 1, :] * yv

    row = pl.BlockSpec((ts, D_MODEL), lambda i: (i, 0))
    return pl.pallas_call(
        body, name=f"ffn_fwd_{layer}", grid=(s // ts,),
        out_shape=(jax.ShapeDtypeStruct((s, D_MODEL), f32), jax.ShapeDtypeStruct((s, D_FF), bf16),
                   jax.ShapeDtypeStruct((s, D_MODEL), bf16), jax.ShapeDtypeStruct((s, D_MODEL), bf16)),
        in_specs=[row, pl.BlockSpec((8, D_MODEL), lambda i: (0, 0)),
                  _layer_weights((D_MODEL, FF_CHUNK)), _layer_weights((FF_CHUNK, D_MODEL))],
        out_specs=(row, pl.BlockSpec((ts, D_FF), lambda i: (i, 0)), row, row),
        compiler_params=_params(("arbitrary",), 56),
    )(x, vec, w1g, w2g)


def _ffn_bwd_act(x, dx, u, y, vec, w1g, w2g, layer):
    s = x.shape[0]
    ts = _ffn_tile(s)

    def body(x_ref, dx_ref, u_ref, y_ref, vec_ref, w1_ref, w2_ref, dxo_ref, da_ref, dyb_ref, sm_ref):
        @pl.when(pl.program_id(0) == 0)
        def _():
            sm_ref[...] = jnp.zeros_like(sm_ref)

        dxv = dx_ref[...]
        dyb = (dxv * vec_ref[R_GT_F:R_GT_F + 1, :]).astype(bf16)
        dyb_ref[...] = dyb
        sm_ref[G_GT:G_GT + 1, :] += _colsum(dxv * y_ref[...].astype(f32))
        dh = jnp.zeros((ts, D_MODEL), f32)
        for f in range(N_DEV):
            cols = slice(f * FF_CHUNK, (f + 1) * FF_CHUNK)
            dz = _mm_nt(dyb, w2_ref[f])
            dab = (dz * (2.0 * u_ref[:, cols].astype(f32))).astype(bf16)
            da_ref[:, cols] = dab
            dh = dh + _mm_nt(dab, w1_ref[f])
        n, r = _rms(x_ref[...])
        sm_ref[G_SH:G_SH + 1, :] += _colsum(dh)
        sm_ref[G_GS:G_GS + 1, :] += _colsum(dh * n)
        dxo_ref[...] = dxv + _norm_bwd(dh, n, r, vec_ref[R_GS_F:R_GS_F + 1, :])

    row = pl.BlockSpec((ts, D_MODEL), lambda i: (i, 0))
    wide = pl.BlockSpec((ts, D_FF), lambda i: (i, 0))
    return pl.pallas_call(
        body, name=f"ffn_bwd_act_{layer}", grid=(s // ts,),
        out_shape=(jax.ShapeDtypeStruct((s, D_MODEL), f32), jax.ShapeDtypeStruct((s, D_FF), bf16),
                   jax.ShapeDtypeStruct((s, D_MODEL), bf16), jax.ShapeDtypeStruct((8, D_MODEL), f32)),
        in_specs=[row, row, wide, row, pl.BlockSpec((8, D_MODEL), lambda i: (0, 0)),
                  _layer_weights((D_MODEL, FF_CHUNK)), _layer_weights((FF_CHUNK, D_MODEL))],
        out_specs=(row, wide, row, pl.BlockSpec((8, D_MODEL), lambda i: (0, 0))),
        compiler_params=_params(("arbitrary",), 58),
    )(x, dx, u, y, vec, w1g, w2g)


def _ffn_bwd_w1(hb, da, layer):
    s = hb.shape[0]
    ts = _ffn_tile(s)
    nt = s // ts

    def body(hb_ref, da_ref, dw_ref, acc_ref):
        i = pl.program_id(0)

        @pl.when(i == 0)
        def _():
            acc_ref[...] = jnp.zeros_like(acc_ref)

        hb = hb_ref[...]
        for f in range(N_DEV):
            acc_ref[f] += _mm_tn(hb, da_ref[:, f * FF_CHUNK:(f + 1) * FF_CHUNK])

        @pl.when(i == nt - 1)
        def _():
            dw_ref[...] = acc_ref[...].astype(bf16)

    return pl.pallas_call(
        body, name=f"ffn_bwd_w1_{layer}", grid=(nt,),
        out_shape=jax.ShapeDtypeStruct((N_DEV, D_MODEL, FF_CHUNK), bf16),
        in_specs=[pl.BlockSpec((ts, D_MODEL), lambda i: (i, 0)), pl.BlockSpec((ts, D_FF), lambda i: (i, 0))],
        out_specs=pl.BlockSpec((N_DEV, D_MODEL, FF_CHUNK), lambda i: (0, 0, 0)),
        scratch_shapes=[pltpu.VMEM((N_DEV, D_MODEL, FF_CHUNK), f32)],
        compiler_params=_params(("arbitrary",), 56),
    )(hb, da)


def _ffn_bwd_w2(u, dyb, layer):
    s = u.shape[0]
    ts = _ffn_tile(s)
    nt = s // ts

    def body(u_ref, dyb_ref, dw_ref, acc_ref):
        i = pl.program_id(0)

        @pl.when(i == 0)
        def _():
            acc_ref[...] = jnp.zeros_like(acc_ref)

        dyb = dyb_ref[...]
        for f in range(N_DEV):
            uv = u_ref[:, f * FF_CHUNK:(f + 1) * FF_CHUNK].astype(f32)
            acc_ref[f] += _mm_tn((uv * uv).astype(bf16), dyb)

        @pl.when(i == nt - 1)
        def _():
            dw_ref[...] = acc_ref[...].astype(bf16)

    return pl.pallas_call(
        body, name=f"ffn_bwd_w2_{layer}", grid=(nt,),
        out_shape=jax.ShapeDtypeStruct((N_DEV, FF_CHUNK, D_MODEL), bf16),
        in_specs=[pl.BlockSpec((ts, D_FF), lambda i: (i, 0)), pl.BlockSpec((ts, D_MODEL), lambda i: (i, 0))],
        out_specs=pl.BlockSpec((N_DEV, FF_CHUNK, D_MODEL), lambda i: (0, 0, 0)),
        scratch_shapes=[pltpu.VMEM((N_DEV, FF_CHUNK, D_MODEL), f32)],
        compiler_params=_params(("arbitrary",), 56),
    )(u, dyb)


def _lru_gates(xc, wsm_ref, pv_ref):
    xcb = xc.astype(bf16)
    gr = _sigmoid(_block_diag(xcb, wsm_ref, 0) + pv_ref[P_BA:P_BA + 1, :])
    gi = _sigmoid(_block_diag(xcb, wsm_ref, 1) + pv_ref[P_BX:P_BX + 1, :])
    log_a = (LRU_C * _log_sigmoid(pv_ref[P_LAM:P_LAM + 1, :])) * gr
    t = jnp.tanh(log_a)
    return gr, gi, jnp.exp(log_a), jnp.sqrt((-2.0 * t) / (1.0 - t))


def _conv(xr, taps_before, pv_ref):
    xc = xr * pv_ref[P_CW0 + 3:P_CW0 + 4, :] + pv_ref[P_CONVB:P_CONVB + 1, :]
    for k, v in zip((2, 1, 0), taps_before):
        xc = xc + v * pv_ref[P_CW0 + k:P_CW0 + k + 1, :]
    return xc


LRU_FWD_SUB, LRU_FWD_SUBS = 128, 2
LRU_BWD_SUB, LRU_BWD_SUBS = 256, 1


def _scan_rows(a, u, carry, reverse):
    groups = a.shape[0] // SUBLANES
    row = lax.broadcasted_iota(jnp.int32, (SUBLANES, a.shape[1]), 0)
    outs = [None] * groups
    for j in range(groups):
        g = groups - 1 - j if reverse else j
        av, uv = a[g * SUBLANES:(g + 1) * SUBLANES], u[g * SUBLANES:(g + 1) * SUBLANES]
        for k in (1, 2, 4):
            if reverse:
                valid, shift = row < SUBLANES - k, SUBLANES - k
            else:
                valid, shift = row >= k, k
            a_s = jnp.where(valid, pltpu.roll(av, shift, 0), 1.0)
            u_s = jnp.where(valid, pltpu.roll(uv, shift, 0), 0.0)
            uv = uv + av * u_s
            av = av * a_s
        h = uv + av * carry
        outs[g] = h
        carry = h[0:1, :] if reverse else h[SUBLANES - 1:SUBLANES, :]
    return jnp.concatenate(outs, axis=0), carry


def _lru_fwd(x, vec, wbig, wsm, pvec, layer):
    s = x.shape[0]
    sub = min(LRU_FWD_SUB, s)
    ts = min(sub * LRU_FWD_SUBS, s)
    nsub = ts // sub
    w = LRU_WIDTH

    def body(x_ref, vec_ref, wb_ref, wsm_ref, pv_ref, xo_ref, xr_ref, hs_ref, a_ref, mult_ref, gr_ref, gi_ref,
             gel_ref, geld_ref, y_ref, tail_ref, carry_ref):
        @pl.when(pl.program_id(0) == 0)
        def _():
            tail_ref[...] = jnp.zeros_like(tail_ref)
            carry_ref[...] = jnp.zeros_like(carry_ref)

        sel = _shift_matrix(sub, BF16_ROWS, (1, 2, 3))
        for k in range(nsub):
            rows = slice(k * sub, (k + 1) * sub)
            xv = x_ref[rows, :]
            n, _ = _rms(xv)
            hb = (n * vec_ref[R_GS_M:R_GS_M + 1, :] + vec_ref[R_SH_M:R_SH_M + 1, :]).astype(bf16)
            gelu_v, gelu_d = _gelu_and_grad(_mm(hb, wb_ref[0]) + pv_ref[P_BY:P_BY + 1, :])
            gel_ref[rows, :] = gelu_v.astype(bf16)
            geld_ref[rows, :] = gelu_d.astype(bf16)
            xrb = (_mm(hb, wb_ref[1]) + pv_ref[P_BIN:P_BIN + 1, :]).astype(bf16)
            xr_ref[rows, :] = xrb
            xc = _conv(xrb.astype(f32), _shifted_rows(sel, tail_ref[...], xrb), pv_ref)
            tail_ref[...] = xrb[sub - BF16_ROWS:, :]
            gr, gi, a, mult = _lru_gates(xc, wsm_ref, pv_ref)
            gr_ref[rows, :] = gr.astype(bf16)
            gi_ref[rows, :] = gi.astype(bf16)
            a_ref[rows, :] = a
            mult_ref[rows, :] = mult
            hs, carry = _scan_rows(a, mult * (gi * xc), carry_ref[0:1, :], reverse=False)
            carry_ref[0:1, :] = carry
            hs_ref[rows, :] = hs
            yv = _mm((hs * gelu_v).astype(bf16), wb_ref[2]) + pv_ref[P_BOUT:P_BOUT + 1, :]
            y_ref[rows, :] = yv.astype(bf16)
            xo_ref[rows, :] = xv + vec_ref[R_GT_M:R_GT_M + 1, :] * yv

    row = pl.BlockSpec((ts, D_MODEL), lambda i: (i, 0))
    roww = pl.BlockSpec((ts, w), lambda i: (i, 0))
    wide = lambda dt: jax.ShapeDtypeStruct((s, w), dt)
    return pl.pallas_call(
        body, name=f"lru_fwd_{layer}", grid=(s // ts,),
        out_shape=(jax.ShapeDtypeStruct((s, D_MODEL), f32), wide(bf16), wide(f32), wide(f32), wide(f32),
                   wide(bf16), wide(bf16), wide(bf16), wide(bf16), jax.ShapeDtypeStruct((s, D_MODEL), bf16)),
        in_specs=[row, pl.BlockSpec((8, D_MODEL), lambda i: (0, 0)),
                  pl.BlockSpec((3, w, w), lambda i: (0, 0, 0)),
                  pl.BlockSpec((2, HEADS, HEAD_DIM, HEAD_DIM), lambda i: (0, 0, 0, 0)),
                  pl.BlockSpec((16, w), lambda i: (0, 0))],
        out_specs=(row, roww, roww, roww, roww, roww, roww, roww, roww, row),
        scratch_shapes=[pltpu.VMEM((BF16_ROWS, w), bf16), pltpu.VMEM((SUBLANES, w), f32)],
        compiler_params=_params(("arbitrary",)),
    )(x, vec, wbig, wsm, pvec)


def _lru_bwd(x, dx, saved, vec, wbig, wsm, pvec, layer):
    xr, hs, a_all, mult_all, gr_all, gi_all, gel_all, geld_all, y = saved
    s = x.shape[0]
    sub = min(LRU_BWD_SUB, s)
    ts = min(sub * LRU_BWD_SUBS, s)
    nsub = ts // sub
    nt = s // ts
    w = LRU_WIDTH
    shard = w // N_DEV
    hshard = HEAD_DIM // N_DEV

    def body(x_ref, dx_ref, xr_ref, xrh_ref, hs_ref, hsh_ref, a_ref, mult_ref, gr_ref, gi_ref, gel_ref, geld_ref,
             y_ref, vec_ref, wb_ref, wsm_ref, pv_ref,
             dxo_ref, dwb_ref, dwsm_ref, sm_ref, accb_ref, accs_ref, eps_ref, dxc8_ref,
             hb_scr, dgb_scr, dxrb_scr, mb_scr, dyb_scr, xcb_scr, drab_scr, drxb_scr):
        i = pl.program_id(0)
        first_tile = i == nt - 1

        @pl.when(i == 0)
        def _():
            accb_ref[...] = jnp.zeros_like(accb_ref)
            accs_ref[...] = jnp.zeros_like(accs_ref)
            sm_ref[...] = jnp.zeros_like(sm_ref)
            eps_ref[...] = jnp.zeros_like(eps_ref)
            dxc8_ref[...] = jnp.zeros_like(dxc8_ref)

        gs = vec_ref[R_GS_M:R_GS_M + 1, :]
        c_ls = LRU_C * _log_sigmoid(pv_ref[P_LAM:P_LAM + 1, :])
        for k in reversed(range(nsub)):
            rows = slice(k * sub, (k + 1) * sub)
            xv = x_ref[rows, :]
            dxv = dx_ref[rows, :]
            n, r = _rms(xv)
            hb_scr[rows, :] = (n * gs + vec_ref[R_SH_M:R_SH_M + 1, :]).astype(bf16)
            xrv = xr_ref[rows, :].astype(f32)
            hsv = hs_ref[rows, :]
            if k == 0:
                xr_halo = jnp.where(first_tile, 0.0, xrh_ref[...].astype(f32))
                hs_halo = jnp.where(first_tile, 0.0, hsh_ref[...])
            else:
                xr_halo = xr_ref[k * sub - BF16_ROWS:k * sub, :].astype(f32)
                hs_halo = hs_ref[k * sub - SUBLANES:k * sub, :]
            xs1, xs2, xs3 = _rows_before(xr_halo, xrv, (1, 2, 3))
            xc = _conv(xrv, (xs1, xs2, xs3), pv_ref)
            xcb_scr[rows, :] = xc.astype(bf16)
            a, mult = a_ref[rows, :], mult_ref[rows, :]
            gr, gi = gr_ref[rows, :].astype(f32), gi_ref[rows, :].astype(f32)
            gelu_v = gel_ref[rows, :].astype(f32)

            dy = dxv * vec_ref[R_GT_M:R_GT_M + 1, :]
            dyb = dy.astype(bf16)
            dyb_scr[rows, :] = dyb
            sm_ref[G_GT:G_GT + 1, :] += _colsum(dxv * y_ref[rows, :].astype(f32))
            sm_ref[G_BOUT:G_BOUT + 1, :] += _colsum(dy)
            mb_scr[rows, :] = (hsv * gelu_v).astype(bf16)
            dm = _mm_nt(dyb, wb_ref[2])
            dhs = dm * gelu_v
            dgpre = dm * hsv * geld_ref[rows, :].astype(f32)
            dgb = dgpre.astype(bf16)
            dgb_scr[rows, :] = dgb
            sm_ref[G_BY:G_BY + 1, :] += _colsum(dgpre)

            eps_in = eps_ref[0:1, :]
            eps, eps_out = _scan_rows(a, a * dhs, eps_in, reverse=True)
            eps_ref[0:1, :] = eps_out
            (eps_next,) = _rows_after(eps, jnp.broadcast_to(eps_in, (SUBLANES, w)), (1,))
            delta = dhs + eps_next
            (h_prev,) = _rows_before(hs_halo, hsv, (1,))
            dxi = delta * xc
            dgi = dxi * mult
            dla = (delta * h_prev) * a - (dxi * gi) * (a * a) / mult
            sm_ref[G_LS:G_LS + 1, :] += _colsum(dla * gr)
            dra = (dla * c_ls) * (gr - gr * gr)
            drx = dgi * (gi - gi * gi)
            drab, drxb = dra.astype(bf16), drx.astype(bf16)
            drab_scr[rows, :] = drab
            drxb_scr[rows, :] = drxb
            sm_ref[G_BA:G_BA + 1, :] += _colsum(dra)
            sm_ref[G_BX:G_BX + 1, :] += _colsum(drx)
            dxc = (delta * mult) * gi + _block_diag_t(drab, wsm_ref, 0) + _block_diag_t(drxb, wsm_ref, 1)

            sm_ref[G_CONVB:G_CONVB + 1, :] += _colsum(dxc)
            for kk, v in zip((3, 2, 1, 0), (xrv, xs1, xs2, xs3)):
                sm_ref[G_CW0 + kk:G_CW0 + kk + 1, :] += _colsum(dxc * v)
            ups = _rows_after(dxc, dxc8_ref[...], (1, 2, 3))
            dxc8_ref[...] = dxc[0:SUBLANES, :]
            dxr = dxc * pv_ref[P_CW0 + 3:P_CW0 + 4, :]
            for kk, v in zip((2, 1, 0), ups):
                dxr = dxr + v * pv_ref[P_CW0 + kk:P_CW0 + kk + 1, :]
            dxrb = dxr.astype(bf16)
            dxrb_scr[rows, :] = dxrb
            sm_ref[G_BIN:G_BIN + 1, :] += _colsum(dxr)
            dh = _mm_nt(dgb, wb_ref[0]) + _mm_nt(dxrb, wb_ref[1])
            sm_ref[G_SH:G_SH + 1, :] += _colsum(dh)
            sm_ref[G_GS:G_GS + 1, :] += _colsum(dh * n)
            dxo_ref[rows, :] = dxv + _norm_bwd(dh, n, r, gs)

        hb = hb_scr[...]
        accb_ref[0] += _mm_tn(hb, dgb_scr[...])
        accb_ref[1] += _mm_tn(hb, dxrb_scr[...])
        accb_ref[2] += _mm_tn(mb_scr[...], dyb_scr[...])
        for h in range(HEADS):
            cols = slice(h * HEAD_DIM, (h + 1) * HEAD_DIM)
            accs_ref[0, h] += _mm_tn(xcb_scr[:, cols], drab_scr[:, cols])
            accs_ref[1, h] += _mm_tn(xcb_scr[:, cols], drxb_scr[:, cols])

        @pl.when(i == nt - 1)
        def _():
            sm_ref[G_LS:G_LS + 1, :] = sm_ref[G_LS:G_LS + 1, :] * LRU_C
            for k in range(3):
                dwb_ref[:, k] = accb_ref[k].astype(bf16).reshape(N_DEV, shard, w)
            for k in range(2):
                for h in range(HEADS):
                    dwsm_ref[:, k, h] = accs_ref[k, h].astype(bf16).reshape(N_DEV, hshard, HEAD_DIM)

    rev = lambda i: (nt - 1 - i, 0)
    row = pl.BlockSpec((ts, D_MODEL), rev)
    roww = pl.BlockSpec((ts, w), rev)
    halo16 = pl.BlockSpec((BF16_ROWS, w), lambda i: (jnp.maximum((nt - 1 - i) * (ts // BF16_ROWS) - 1, 0), 0))
    halo8 = pl.BlockSpec((SUBLANES, w), lambda i: (jnp.maximum((nt - 1 - i) * (ts // SUBLANES) - 1, 0), 0))
    const = lambda *shape: pl.BlockSpec(shape, lambda i: (0,) * len(shape))
    operand = pltpu.VMEM((ts, w), bf16)
    return pl.pallas_call(
        body, name=f"lru_bwd_{layer}", grid=(nt,),
        out_shape=(jax.ShapeDtypeStruct((s, D_MODEL), f32),
                   jax.ShapeDtypeStruct((N_DEV, 3, shard, w), bf16),
                   jax.ShapeDtypeStruct((N_DEV, 2, HEADS, hshard, HEAD_DIM), bf16),
                   jax.ShapeDtypeStruct((16, w), f32)),
        in_specs=[row, row, roww, halo16, roww, halo8, roww, roww, roww, roww, roww, roww, row, const(8, D_MODEL),
                  const(3, w, w), const(2, HEADS, HEAD_DIM, HEAD_DIM), const(16, w)],
        out_specs=(row, const(N_DEV, 3, shard, w), const(N_DEV, 2, HEADS, hshard, HEAD_DIM), const(16, w)),
        scratch_shapes=[pltpu.VMEM((3, w, w), f32), pltpu.VMEM((2, HEADS, HEAD_DIM, HEAD_DIM), f32),
                        pltpu.VMEM((SUBLANES, w), f32), pltpu.VMEM((SUBLANES, w), f32)] + [operand] * 8,
        compiler_params=_params(("arbitrary",), 58),
    )(x, dx, xr, xr, hs, hs, a_all, mult_all, gr_all, gi_all, gel_all, geld_all, y, vec, wbig, wsm, pvec)


def _pool_tile(s):
    return min(256, s)


def _pool_counts(tile_index, ts):
    t = (tile_index * ts + lax.broadcasted_iota(jnp.int32, (ts, 1), 0) + 1).astype(f32)
    return [1.0 / jnp.minimum(t, float(win)) for win in POOL_WINDOWS]


def _pooled(h, halo, inv):
    ext = jnp.concatenate([halo, h], axis=0)
    out = []
    for g in range(len(POOL_WINDOWS)):
        acc = ext[:, g * HEAD_DIM:(g + 1) * HEAD_DIM]
        for step in range(g + 1):
            acc = acc + pltpu.roll(acc, 1 << step, 0)
        out.append(acc[POOL_HALO:] * inv[g] - h[:, g * HEAD_DIM:(g + 1) * HEAD_DIM])
    return out


def _pool_fwd(x, vec, pw, ps, layer):
    s = x.shape[0]
    ts = _pool_tile(s)

    def body(x_ref, vec_ref, pw_ref, ps_ref, xo_ref, y_ref, halo_ref):
        i = pl.program_id(0)

        @pl.when(i == 0)
        def _():
            halo_ref[...] = jnp.zeros_like(halo_ref)

        xv = x_ref[...]
        n, _ = _rms(xv)
        h = n * vec_ref[R_GS_M:R_GS_M + 1, :] + vec_ref[R_SH_M:R_SH_M + 1, :]
        pooled = _pooled(h, halo_ref[...], _pool_counts(i, ts))
        halo_ref[...] = h[ts - POOL_HALO:, :]
        mixed = jnp.concatenate([_mm(pooled[g].astype(bf16), pw_ref[g]) for g in range(HEADS)], axis=1)
        yv = mixed * ps_ref[0:1, :]
        y_ref[...] = yv.astype(bf16)
        xo_ref[...] = xv + vec_ref[R_GT_M:R_GT_M + 1, :] * yv

    row = pl.BlockSpec((ts, D_MODEL), lambda i: (i, 0))
    return pl.pallas_call(
        body, name=f"pool_fwd_{layer}", grid=(s // ts,),
        out_shape=(jax.ShapeDtypeStruct((s, D_MODEL), f32), jax.ShapeDtypeStruct((s, D_MODEL), bf16)),
        in_specs=[row, pl.BlockSpec((8, D_MODEL), lambda i: (0, 0)),
                  pl.BlockSpec((HEADS, HEAD_DIM, HEAD_DIM), lambda i: (0, 0, 0)),
                  pl.BlockSpec((8, D_MODEL), lambda i: (0, 0))],
        out_specs=(row, row),
        scratch_shapes=[pltpu.VMEM((POOL_HALO, D_MODEL), f32)],
        compiler_params=_params(("arbitrary",)),
    )(x, vec, pw, ps)


def _pool_bwd(x, dx, y, vec, pw, ps, layer):
    s = x.shape[0]
    ts = _pool_tile(s)
    nt = s // ts
    hshard = HEAD_DIM // N_DEV

    def body(x_ref, xh_ref, dx_ref, y_ref, vec_ref, pw_ref, ps_ref, dxo_ref, dpw_ref, sm_ref, acc_ref, q16_ref):
        i = pl.program_id(0)
        tile = nt - 1 - i

        @pl.when(i == 0)
        def _():
            acc_ref[...] = jnp.zeros_like(acc_ref)
            sm_ref[...] = jnp.zeros_like(sm_ref)
            q16_ref[...] = jnp.zeros_like(q16_ref)

        gs, sh = vec_ref[R_GS_M:R_GS_M + 1, :], vec_ref[R_SH_M:R_SH_M + 1, :]
        xv = x_ref[...]
        dxv = dx_ref[...]
        n, r = _rms(xv)
        h = n * gs + sh
        nh, _ = _rms(xh_ref[...])
        halo = jnp.where(tile == 0, 0.0, nh * gs + sh)
        inv = _pool_counts(tile, ts)
        pooled = _pooled(h, halo, inv)
        mixed = jnp.concatenate([_mm(pooled[g].astype(bf16), pw_ref[g]) for g in range(HEADS)], axis=1)

        dy = dxv * vec_ref[R_GT_M:R_GT_M + 1, :]
        sm_ref[G_GT:G_GT + 1, :] += _colsum(dxv * y_ref[...].astype(f32))
        sm_ref[3:4, :] += _colsum(dy * mixed)
        dmixed = (dy * ps_ref[0:1, :]).astype(bf16)
        dh_parts = []
        for g in range(HEADS):
            cols = slice(g * HEAD_DIM, (g + 1) * HEAD_DIM)
            acc_ref[g] += _mm_tn(pooled[g].astype(bf16), dmixed[:, cols])
            dpooled = _mm_nt(dmixed[:, cols], pw_ref[g])
            q = dpooled * inv[g]
            ext = jnp.concatenate([q, q16_ref[:, cols]], axis=0)
            q16_ref[:, cols] = q[0:POOL_HALO, :]
            for step in range(g + 1):
                ext = ext + pltpu.roll(ext, ext.shape[0] - (1 << step), 0)
            dh_parts.append(ext[:ts] - dpooled)
        dh = jnp.concatenate(dh_parts, axis=1)
        sm_ref[G_SH:G_SH + 1, :] += _colsum(dh)
        sm_ref[G_GS:G_GS + 1, :] += _colsum(dh * n)
        dxo_ref[...] = dxv + _norm_bwd(dh, n, r, gs)

        @pl.when(i == nt - 1)
        def _():
            for g in range(HEADS):
                dpw_ref[:, g] = acc_ref[g].astype(bf16).reshape(N_DEV, hshard, HEAD_DIM)

    rev = lambda i: (nt - 1 - i, 0)
    row = pl.BlockSpec((ts, D_MODEL), rev)
    halo16 = pl.BlockSpec((POOL_HALO, D_MODEL), lambda i: (jnp.maximum((nt - 1 - i) * (ts // POOL_HALO) - 1, 0), 0))
    const = lambda *shape: pl.BlockSpec(shape, lambda i: (0,) * len(shape))
    return pl.pallas_call(
        body, name=f"pool_bwd_{layer}", grid=(nt,),
        out_shape=(jax.ShapeDtypeStruct((s, D_MODEL), f32),
                   jax.ShapeDtypeStruct((N_DEV, HEADS, hshard, HEAD_DIM), bf16),
                   jax.ShapeDtypeStruct((8, D_MODEL), f32)),
        in_specs=[row, halo16, row, row, const(8, D_MODEL), const(HEADS, HEAD_DIM, HEAD_DIM), const(8, D_MODEL)],
        out_specs=(row, const(N_DEV, HEADS, hshard, HEAD_DIM), const(8, D_MODEL)),
        scratch_shapes=[pltpu.VMEM((HEADS, HEAD_DIM, HEAD_DIM), f32), pltpu.VMEM((POOL_HALO, D_MODEL), f32)],
        compiler_params=_params(("arbitrary",)),
    )(x, x, dx, y, vec, pw, ps)


def _final(x, target, g_fin):
    s = x.shape[0]
    ts = min(512, s)

    def body(x_ref, t_ref, g_ref, dx_ref, sm_ref):
        @pl.when(pl.program_id(0) == 0)
        def _():
            sm_ref[...] = jnp.zeros_like(sm_ref)

        g = g_ref[0:1, :]
        n, r = _rms(x_ref[...])
        err = n * g - t_ref[...]
        sm_ref[1:2, :] += 0.5 * jnp.sum(jnp.mean(err * err, axis=-1, keepdims=True), axis=0, keepdims=True)
        dyv = err * (1.0 / D_MODEL)
        sm_ref[0:1, :] += _colsum(dyv * n)
        dx_ref[...] = _norm_bwd(dyv, n, r, g)

    row = pl.BlockSpec((ts, D_MODEL), lambda i: (i, 0))
    return pl.pallas_call(
        body, name="final_loss", grid=(s // ts,),
        out_shape=(jax.ShapeDtypeStruct((s, D_MODEL), f32), jax.ShapeDtypeStruct((8, D_MODEL), f32)),
        in_specs=[row, row, pl.BlockSpec((8, D_MODEL), lambda i: (0, 0))],
        out_specs=(row, pl.BlockSpec((8, D_MODEL), lambda i: (0, 0))),
        compiler_params=_params(("arbitrary",)),
    )(x, target, g_fin)


def _small_pack(sm_ffn, sm_mix, sm_fin, table, g_mix, g_ffn, lam):
    def body(*refs):
        ffn, mix = refs[0:DEPTH], refs[DEPTH:2 * DEPTH]
        fin_ref, tab_ref, gm_ref, gf_ref, lam_ref, o_ref = refs[2 * DEPTH:]
        o_ref[...] = jnp.zeros_like(o_ref)
        for i in range(DEPTH):
            base = K_MOD + i * N_MOD
            o_ref[base + 0:base + 1, :] = mix[i][G_SH:G_SH + 1, :]
            o_ref[base + 1:base + 2, :] = mix[i][G_GS:G_GS + 1, :] * gm_ref[i:i + 1, :]
            o_ref[base + 2:base + 3, :] = mix[i][G_GT:G_GT + 1, :]
            o_ref[base + 3:base + 4, :] = ffn[i][G_SH:G_SH + 1, :]
            o_ref[base + 4:base + 5, :] = ffn[i][G_GS:G_GS + 1, :] * gf_ref[i:i + 1, :]
            o_ref[base + 5:base + 6, :] = ffn[i][G_GT:G_GT + 1, :]
            o_ref[K_NMIX + i:K_NMIX + i + 1, :] = mix[i][G_GS:G_GS + 1, :] * (1.0 + tab_ref[i, R_SC_M:R_SC_M + 1, :])
            o_ref[K_NFFN + i:K_NFFN + i + 1, :] = ffn[i][G_GS:G_GS + 1, :] * (1.0 + tab_ref[i, R_SC_F:R_SC_F + 1, :])
            j = i // 2
            if i % 2 == 0:
                for k, src in enumerate((G_BY, G_BIN, G_CONVB, None, G_BOUT)):
                    dst = K_LRUB + j * 5 + k
                    if src is None:
                        o_ref[dst:dst + 1, :] = mix[i][G_LS:G_LS + 1, :] * _sigmoid(-lam_ref[j:j + 1, :])
                    else:
                        o_ref[dst:dst + 1, :] = mix[i][src:src + 1, :]
                o_ref[K_CONVW + j * 4:K_CONVW + j * 4 + 4, :] = mix[i][G_CW0:G_CW0 + 4, :]
                o_ref[K_BA + j:K_BA + j + 1, :] = mix[i][G_BA:G_BA + 1, :]
                o_ref[K_BX + j:K_BX + j + 1, :] = mix[i][G_BX:G_BX + 1, :]
            else:
                o_ref[K_PS + j:K_PS + j + 1, :] = mix[i][3:4, :]
        o_ref[K_FIN:K_FIN + 2, :] = fin_ref[0:2, :]

    return pl.pallas_call(body, name="small_pack", out_shape=jax.ShapeDtypeStruct((K_ROWS, D_MODEL), f32))(
        *sm_ffn, *sm_mix, sm_fin, table, g_mix, g_ffn, lam)


def _small_sum(gathered):
    def body(g_ref, o_ref, token_ref):
        tot = g_ref[0]
        for src in range(1, N_DEV):
            tot = tot + g_ref[src]
        o_ref[...] = tot
        token_ref[...] = jnp.zeros_like(token_ref)

    return pl.pallas_call(
        body, name="small_sum",
        out_shape=(jax.ShapeDtypeStruct(gathered.shape[1:], f32), jax.ShapeDtypeStruct((8, 128), f32)))(gathered)


def _adamw_math(g, w, m, v):
    m = ADAM_B1 * m + (1.0 - ADAM_B1) * g
    v = ADAM_B2 * v + (1.0 - ADAM_B2) * (g * g)
    m_hat = m / (1.0 - ADAM_B1 ** ADAM_STEP)
    v_hat = v / (1.0 - ADAM_B2 ** ADAM_STEP)
    delta = -ADAM_LR * (m_hat / (jnp.sqrt(v_hat) + ADAM_EPS) + ADAM_WD * w)
    return delta, m, v


def _adamw_small(name, g, w, m, v):
    shape = w.shape
    two_d = (1, shape[0]) if len(shape) == 1 else (math.prod(shape[:-1]), shape[-1])

    def body(g_ref, w_ref, m_ref, v_ref, d_ref, mo_ref, vo_ref):
        d_ref[...], mo_ref[...], vo_ref[...] = _adamw_math(g_ref[...], w_ref[...], m_ref[...], v_ref[...])

    outs = pl.pallas_call(body, name=f"adamw_{name}", out_shape=tuple(jax.ShapeDtypeStruct(two_d, f32) for _ in range(3)))(
        *(t.reshape(two_d) for t in (g, w, m, v)))
    return tuple(t.reshape(shape) for t in outs)


def _block_rows(rows, cols):
    tr = max(SUBLANES, min(rows, (512 * 1024) // (4 * cols)))
    while rows % tr:
        tr //= 2
    return tr


def _adamw_reduce(name, landings, kind, w, m, v):
    nl = len(landings)
    rows, cols = landings[0].shape[2:]
    tr = _block_rows(rows, cols)
    per_layer = rows // tr

    def body(*refs):
        l_refs = refs[:nl]
        w_ref, m_ref, v_ref, g_ref, d_ref, mo_ref, vo_ref = refs[nl:]
        layer = pl.program_id(0)
        for k in range(nl):
            @pl.when(layer == k)
            def _(k=k):
                g = l_refs[k][0].astype(f32)
                for src in range(1, N_DEV):
                    g = g + l_refs[k][src].astype(f32)
                g_ref[...] = g
        d_ref[...], mo_ref[...], vo_ref[...] = _adamw_math(g_ref[...], w_ref[...], m_ref[...], v_ref[...])

    blk = pl.BlockSpec((tr, cols), lambda l, r: (l * per_layer + r, 0))
    land = [pl.BlockSpec((N_DEV, None, tr, cols), lambda l, r, k=k: (0, kind, jnp.where(l == k, r, 0), 0)) for k in range(nl)]
    return pl.pallas_call(
        body, name=f"adamw_{name}", grid=(nl, per_layer),
        out_shape=tuple(jax.ShapeDtypeStruct((nl * rows, cols), f32) for _ in range(4)),
        in_specs=land + [blk, blk, blk],
        out_specs=(blk, blk, blk, blk),
        compiler_params=_params(("arbitrary", "arbitrary"), 32),
    )(*landings, w, m, v)


def _adamw_w_mod(c_all, dmod_all, w, m, v):
    depth, d, cols = w.shape
    tr = 256

    def body(c_ref, dm_ref, w_ref, m_ref, v_ref, g_ref, d_ref, mo_ref, vo_ref):
        cv = c_ref[...]
        cond = cv * _sigmoid(cv)
        g = lax.dot_general(cond, dm_ref[...], (((0,), (0,)), ((), ())), preferred_element_type=f32,
                            precision=lax.Precision.HIGHEST)
        g_ref[...] = g
        d_ref[...], mo_ref[...], vo_ref[...] = _adamw_math(g, w_ref[...], m_ref[...], v_ref[...])

    blk = pl.BlockSpec((None, tr, cols), lambda i, r: (i, r, 0))
    return pl.pallas_call(
        body, name="adamw_w_mod", grid=(depth, d // tr),
        out_shape=tuple(jax.ShapeDtypeStruct(w.shape, f32) for _ in range(4)),
        in_specs=[pl.BlockSpec((N_DEV, tr), lambda i, r: (0, r)),
                  pl.BlockSpec((None, N_DEV, cols), lambda i, r: (i, 0, 0)), blk, blk, blk],
        out_specs=(blk, blk, blk, blk),
        compiler_params=_params(("arbitrary", "arbitrary"), 32),
    )(c_all, dmod_all, w, m, v)


def kernel(x, c, w_mod, b_mod, norm_mix_g, norm_ffn_g, lru_w_y, lru_b_y, lru_w_in, lru_b_in, lru_conv_w, lru_conv_b, lru_w_a, lru_b_a, lru_w_x, lru_b_x, lru_lambda, lru_w_out, lru_b_out, pool_w, pool_scale, ffn_w1, ffn_w2, final_norm_g, loss_target, m_w_mod, m_b_mod, m_norm_mix_g, m_norm_ffn_g, m_lru_w_y, m_lru_b_y, m_lru_w_in, m_lru_b_in, m_lru_conv_w, m_lru_conv_b, m_lru_w_a, m_lru_b_a, m_lru_w_x, m_lru_b_x, m_lru_lambda, m_lru_w_out, m_lru_b_out, m_pool_w, m_pool_scale, m_ffn_w1, m_ffn_w2, m_final_norm_g, v_w_mod, v_b_mod, v_norm_mix_g, v_norm_ffn_g, v_lru_w_y, v_lru_b_y, v_lru_w_in, v_lru_b_in, v_lru_conv_w, v_lru_conv_b, v_lru_w_a, v_lru_b_a, v_lru_w_x, v_lru_b_x, v_lru_lambda, v_lru_w_out, v_lru_b_out, v_pool_w, v_pool_scale, v_ffn_w1, v_ffn_w2, v_final_norm_g):
    me = 4 * lax.axis_index("x") + 2 * lax.axis_index("y") + lax.axis_index("c")
    n_lru = lru_w_y.shape[0]
    shard = LRU_WIDTH // N_DEV
    hshard = HEAD_DIM // N_DEV
    xs = x[0]
    target = loss_target[0]

    small_vecs = jnp.concatenate([
        lru_conv_w.reshape(n_lru * 4, shard), lru_b_a.reshape(n_lru, HEADS * hshard),
        lru_b_x.reshape(n_lru, HEADS * hshard), pool_scale, jnp.zeros((2, shard), f32)], axis=0)
    sv_g, c_g = _exchange([small_vecs, c], True, "gather_cond")
    conv_w_full = sv_g[:, 0:8].reshape(N_DEV, n_lru, 4, shard).transpose(1, 2, 0, 3).reshape(n_lru, 4, LRU_WIDTH)
    b_a_full = sv_g[:, 8:10].reshape(N_DEV, n_lru, HEADS, hshard).transpose(1, 2, 0, 3).reshape(n_lru, LRU_WIDTH)
    b_x_full = sv_g[:, 10:12].reshape(N_DEV, n_lru, HEADS, hshard).transpose(1, 2, 0, 3).reshape(n_lru, LRU_WIDTH)
    ps_full = sv_g[:, 12:14].transpose(1, 0, 2).reshape(n_lru, D_MODEL)
    c_all = c_g.reshape(N_DEV, D_MODEL)

    (mod_g,) = _exchange([_mod_part(c_all, w_mod)], True, "gather_mod", pieces=DEPTH)
    mod_row = lax.dynamic_index_in_dim(mod_g, me, axis=2, keepdims=False)
    mod_row = mod_row.transpose(1, 0, 2).reshape(DEPTH, N_MOD * D_MODEL)
    table, token = _mod_table(mod_row, b_mod, norm_mix_g, norm_ffn_g)

    first_pieces = 4
    (first_mix,), token = _send_start("gather_first_start", [[
        jnp.stack([lru_w_y[0], lru_w_in[0], lru_w_out[0]]).astype(bf16).reshape(3 * first_pieces, -1, LRU_WIDTH),
        jnp.stack([lru_w_a[0], lru_w_x[0]]).astype(bf16).reshape(first_pieces, -1, HEAD_DIM)]], True, me,
        pieces=first_pieces, after=token)

    parts = []
    for i in range(DEPTH):
        j = i // 2
        if i > 0 and i % 2 == 0:
            parts.append([(jnp.stack([lru_w_y[j], lru_w_in[j], lru_w_out[j]]) + token[0, 0]).astype(bf16),
                          (jnp.stack([lru_w_a[j], lru_w_x[j]]) + token[0, 0]).astype(bf16)])
        elif i % 2 == 1:
            parts.append([(pool_w[j] + token[0, 0]).astype(bf16)])
        parts.append([(ffn_w1[i] + token[0, 0]).astype(bf16), (ffn_w2[i] + token[0, 0]).astype(bf16)])
    first_got, token = _send_wait("gather_mix_wait_0", first_mix, [a for part in parts for a in part])
    handles, token = _send_start("gather_rest_start", parts, True, me, after=token)
    h_ffn = [handles[0], handles[2], handles[4], handles[6]]
    h_mix = [None, handles[1], handles[3], handles[5]]

    zero_row = jnp.zeros((1, LRU_WIDTH), f32)
    pvecs = [jnp.concatenate([lru_b_y[j:j + 1], lru_b_in[j:j + 1], lru_conv_b[j:j + 1], b_a_full[j:j + 1],
                              b_x_full[j:j + 1], lru_lambda[j:j + 1], lru_b_out[j:j + 1], zero_row,
                              conv_w_full[j], zero_row, zero_row, zero_row, zero_row], axis=0) for j in range(n_lru)]
    ps_rows = [jnp.concatenate([ps_full[j:j + 1], jnp.zeros((7, D_MODEL), f32)], axis=0) for j in range(n_lru)]

    saved = []
    ffn_w, mix_w = [], []
    h = xs
    for i in range(DEPTH):
        j = i // 2
        got = first_got if i == 0 else _send_wait(f"gather_mix_wait_{i}", h_mix[i], h)[0]
        if i % 2 == 0:
            got = [got[0].reshape(N_DEV, 3, shard, LRU_WIDTH), got[1].reshape(N_DEV, 2, HEADS, hshard, HEAD_DIM)]
            mix_w.append((got[0].transpose(1, 0, 2, 3).reshape(3, LRU_WIDTH, LRU_WIDTH),
                          got[1].transpose(1, 2, 0, 3, 4).reshape(2, HEADS, HEAD_DIM, HEAD_DIM)))
            h_mid, *lru_saved = _lru_fwd(h, table[i] + token[0, 0], mix_w[i][0], mix_w[i][1], pvecs[j], i)
            mix_saved = (h, tuple(lru_saved))
        else:
            mix_w.append((got[0].transpose(1, 0, 2, 3).reshape(HEADS, HEAD_DIM, HEAD_DIM),))
            h_mid, y_mix = _pool_fwd(h, table[i], mix_w[i][0], ps_rows[j], i)
            mix_saved = (h, y_mix)
        ffn_w.append(_send_wait(f"gather_ffn_wait_{i}", h_ffn[i], h_mid)[0])
        h_out, u, y_ffn, hb = _ffn_fwd(h_mid, table[i], ffn_w[i][0], ffn_w[i][1], i)
        saved.append((mix_saved, (h_mid, u, y_ffn, hb)))
        h = h_out
    fin_rows = jnp.concatenate([final_norm_g[None, :], jnp.zeros((7, D_MODEL), f32)], axis=0)
    dx, sm_fin = _final(h, target, fin_rows)

    sm_ffn, sm_mix = [None] * DEPTH, [None] * DEPTH
    x_ffn, x_mix = [None] * DEPTH, [None] * DEPTH
    token = jnp.zeros((8, 128), f32)
    last_mix = None
    for i in reversed(range(DEPTH)):
        j = i // 2
        mix_saved, (h_mid, u, y_ffn, hb) = saved[i]
        dx, da, dyb, sm_ffn[i] = _ffn_bwd_act(h_mid, dx, u, y_ffn, table[i] + token[0, 0], ffn_w[i][0], ffn_w[i][1], i)
        ffn_grads = [_ffn_bwd_w1(hb, da, i), _ffn_bwd_w2(u, dyb, i)]
        if last_mix is None:
            (x_ffn[i],), token = _send_start(f"grads_start_{i}", [ffn_grads], False, me)
        else:
            (x_mix[i + 1], x_ffn[i]), token = _send_start(f"grads_start_{i}", [last_mix, ffn_grads], False, me)
        if i % 2 == 0:
            h_in, lru_saved = mix_saved
            dx, dbig, dsmall, sm_mix[i] = _lru_bwd(
                h_in, dx, lru_saved, table[i] + token[0, 0], mix_w[i][0], mix_w[i][1], pvecs[j], i)
            last_mix = [dbig, dsmall]
        else:
            h_in, y_mix = mix_saved
            dx, dpool, sm = _pool_bwd(h_in, dx, y_mix, table[i] + token[0, 0], mix_w[i][0], ps_rows[j], i)
            sm_mix[i] = jnp.concatenate([sm, jnp.zeros((8, D_MODEL), f32)], axis=0)
            last_mix = [dpool]
    grad_x = dx[None]

    pack = _small_pack(sm_ffn, sm_mix, sm_fin, table + token[0, 0], norm_mix_g, norm_ffn_g, lru_lambda)
    (pack_g,) = _exchange([pack], True, "gather_small_grads", pieces=4)
    tot, token = _small_sum(pack_g)
    loss = tot[K_LOSS, 0]
    (x_mix[0],), _ = _send_start("grads_last_start", [[t + token[0, 0].astype(bf16) for t in last_mix]], False, me)
    cols = w_mod.shape[2]
    dmod_all = lax.dynamic_slice_in_dim(pack_g[:, K_MOD:K_MOD + DEPTH * N_MOD].reshape(N_DEV, DEPTH, N_MOD * D_MODEL),
                                        me * cols, cols, axis=2).transpose(1, 0, 2)
    results = {"w_mod": _adamw_w_mod(c_all, dmod_all, w_mod, m_w_mod, v_w_mod)}

    after = results["w_mod"][1]
    l_ffn = [_send_wait(f"grads_ffn_wait_{i}", x_ffn[i], after)[0] for i in reversed(range(DEPTH))][::-1]

    def reduce_update(name, landings, kind, w, m, v):
        rows = w.size // w.shape[-1]
        two_d = (rows, w.shape[-1])
        lands = [t.reshape(N_DEV, -1, rows // len(landings), w.shape[-1]) for t in landings]
        outs = _adamw_reduce(name, lands, kind, w.reshape(two_d), m.reshape(two_d), v.reshape(two_d))
        return tuple(t.reshape(w.shape) for t in outs)

    results["ffn_w1"] = reduce_update("ffn_w1", [t[0] for t in l_ffn], 0, ffn_w1, m_ffn_w1, v_ffn_w1)
    results["ffn_w2"] = reduce_update("ffn_w2", [t[1] for t in l_ffn], 0, ffn_w2, m_ffn_w2, v_ffn_w2)
    after = results["ffn_w2"][1]
    l_mix = [_send_wait(f"grads_mix_wait_{i}", x_mix[i], after)[0] for i in reversed(range(DEPTH))][::-1]
    l_lru_big = [l_mix[i][0] for i in range(0, DEPTH, 2)]
    l_lru_small = [l_mix[i][1] for i in range(0, DEPTH, 2)]
    l_pool = [l_mix[i][0] for i in range(1, DEPTH, 2)]
    results["lru_w_y"] = reduce_update("lru_w_y", l_lru_big, 0, lru_w_y, m_lru_w_y, v_lru_w_y)
    results["lru_w_in"] = reduce_update("lru_w_in", l_lru_big, 1, lru_w_in, m_lru_w_in, v_lru_w_in)
    results["lru_w_out"] = reduce_update("lru_w_out", l_lru_big, 2, lru_w_out, m_lru_w_out, v_lru_w_out)
    results["lru_w_a"] = reduce_update("lru_w_a", l_lru_small, 0, lru_w_a, m_lru_w_a, v_lru_w_a)
    results["lru_w_x"] = reduce_update("lru_w_x", l_lru_small, 1, lru_w_x, m_lru_w_x, v_lru_w_x)
    results["pool_w"] = reduce_update("pool_w", l_pool, 0, pool_w, m_pool_w, v_pool_w)

    def my_cols(full, width):
        return lax.dynamic_slice_in_dim(full, me * width, width, axis=full.ndim - 1)

    lru_rows = tot[K_LRUB:K_LRUB + 5 * n_lru].reshape(n_lru, 5, LRU_WIDTH)
    small_grads = {
        "b_mod": tot[K_MOD:K_MOD + DEPTH * N_MOD].reshape(DEPTH, N_MOD * D_MODEL),
        "norm_mix_g": tot[K_NMIX:K_NMIX + DEPTH],
        "norm_ffn_g": tot[K_NFFN:K_NFFN + DEPTH],
        "lru_b_y": lru_rows[:, 0], "lru_b_in": lru_rows[:, 1], "lru_conv_b": lru_rows[:, 2],
        "lru_lambda": lru_rows[:, 3], "lru_b_out": lru_rows[:, 4],
        "lru_conv_w": my_cols(tot[K_CONVW:K_CONVW + 4 * n_lru].reshape(n_lru, 4, LRU_WIDTH), shard),
        "lru_b_a": my_cols(tot[K_BA:K_BA + n_lru].reshape(n_lru, HEADS, HEAD_DIM), hshard),
        "lru_b_x": my_cols(tot[K_BX:K_BX + n_lru].reshape(n_lru, HEADS, HEAD_DIM), hshard),
        "pool_scale": my_cols(tot[K_PS:K_PS + n_lru], shard),
        "final_norm_g": tot[K_FIN],
    }
    given = dict(b_mod=(b_mod, m_b_mod, v_b_mod), norm_mix_g=(norm_mix_g, m_norm_mix_g, v_norm_mix_g),
                 norm_ffn_g=(norm_ffn_g, m_norm_ffn_g, v_norm_ffn_g), lru_b_y=(lru_b_y, m_lru_b_y, v_lru_b_y),
                 lru_b_in=(lru_b_in, m_lru_b_in, v_lru_b_in), lru_conv_w=(lru_conv_w, m_lru_conv_w, v_lru_conv_w),
                 lru_conv_b=(lru_conv_b, m_lru_conv_b, v_lru_conv_b), lru_b_a=(lru_b_a, m_lru_b_a, v_lru_b_a),
                 lru_b_x=(lru_b_x, m_lru_b_x, v_lru_b_x), lru_lambda=(lru_lambda, m_lru_lambda, v_lru_lambda),
                 lru_b_out=(lru_b_out, m_lru_b_out, v_lru_b_out), pool_scale=(pool_scale, m_pool_scale, v_pool_scale),
                 final_norm_g=(final_norm_g, m_final_norm_g, v_final_norm_g))
    for name, g in small_grads.items():
        results[name] = (g,) + _adamw_small(name, g, *given[name])

    order = ["w_mod", "b_mod", "norm_mix_g", "norm_ffn_g", "lru_w_y", "lru_b_y", "lru_w_in", "lru_b_in", "lru_conv_w",
             "lru_conv_b", "lru_w_a", "lru_b_a", "lru_w_x", "lru_b_x", "lru_lambda", "lru_w_out", "lru_b_out", "pool_w",
             "pool_scale", "ffn_w1", "ffn_w2", "final_norm_g"]
    return (loss, grad_x, *[results[n][0] for n in order], *[results[n][1] for n in order],
            *[results[n][2] for n in order], *[results[n][3] for n in order])
```

```python
import math

import jax
import jax.numpy as jnp
from jax import lax
from jax.experimental import pallas as pl
from jax.experimental.pallas import tpu as pltpu

f32, bf16 = jnp.float32, jnp.bfloat16

D_MODEL = 1024
LRU_WIDTH = 1024
HEADS = 4
HEAD_DIM = 256
D_FF = 4096
DEPTH = 4
N_MOD = 6
N_DEV = 8
FF_CHUNK = D_FF // N_DEV
POOL_WINDOWS = (2, 4, 8, 16)
POOL_HALO = 16
EPS = 1e-6
LRU_C = 8.0

ADAM_LR = 0.001
ADAM_B1 = 0.9
ADAM_B2 = 0.999
ADAM_EPS = 1e-08
ADAM_WD = 0.01
ADAM_STEP = 10

SUBLANES = 8
BF16_ROWS = 16

R_SH_M, R_SC_M, R_GT_M, R_SH_F, R_SC_F, R_GT_F, R_GS_M, R_GS_F = range(8)
P_BY, P_BIN, P_CONVB, P_BA, P_BX, P_LAM, P_BOUT, P_CW0 = 0, 1, 2, 3, 4, 5, 6, 8
G_SH, G_GS, G_GT, G_BY, G_BIN, G_CONVB, G_BA, G_BX, G_LS, G_BOUT, G_CW0 = 0, 1, 2, 3, 4, 5, 6, 7, 8, 9, 10
K_MOD, K_NMIX, K_NFFN, K_LRUB, K_CONVW, K_BA, K_BX, K_PS, K_FIN, K_LOSS, K_ROWS = 0, 24, 28, 32, 42, 50, 52, 54, 56, 57, 64


def _params(semantics=None, vmem_mb=48):
    return pltpu.CompilerParams(dimension_semantics=semantics, vmem_limit_bytes=vmem_mb * 1024 * 1024)


def _in_hbm(a):
    return pltpu.with_memory_space_constraint(a, pltpu.HBM)


def _mm(a, b):
    return jnp.dot(a, b, preferred_element_type=f32)


def _mm_nt(a, b):
    return lax.dot_general(a, b, (((1,), (1,)), ((), ())), preferred_element_type=f32)


def _mm_tn(a, b):
    return lax.dot_general(a, b, (((0,), (0,)), ((), ())), preferred_element_type=f32)


def _rms(x):
    r = lax.rsqrt(jnp.mean(x * x, axis=-1, keepdims=True) + EPS)
    return x * r, r


def _norm_bwd(dh, n, r, gs):
    dn = dh * gs
    return r * (dn - n * jnp.mean(dn * n, axis=-1, keepdims=True))


def _colsum(v):
    return jnp.sum(v, axis=0, keepdims=True)


def _sigmoid(v):
    return 0.5 * jnp.tanh(0.5 * v) + 0.5


def _log_sigmoid(v):
    return jnp.minimum(v, 0.0) - jnp.log1p(jnp.exp(-jnp.abs(v)))


_GELU_C = 0.7978845608028654
_GELU_A = 0.044715


def _gelu_and_grad(v):
    v2 = v * v
    t = jnp.tanh(_GELU_C * v * (1.0 + _GELU_A * v2))
    p = 0.5 + 0.5 * t
    return v * p, p + (0.5 * v) * (1.0 - t * t) * (_GELU_C + (3.0 * _GELU_A * _GELU_C) * v2)


def _rows_before(halo, v, shifts):
    hr = halo.shape[0]
    ext = jnp.concatenate([halo, v], axis=0)
    return [pltpu.roll(ext, k, 0)[hr:] for k in shifts]


def _rows_after(v, halo, shifts):
    n = v.shape[0]
    ext = jnp.concatenate([v, halo], axis=0)
    return [pltpu.roll(ext, ext.shape[0] - k, 0)[:n] for k in shifts]


def _shift_matrix(n, halo_rows, shifts):
    rows = lax.broadcasted_iota(jnp.int32, (n, n + halo_rows), 0)
    cols = lax.broadcasted_iota(jnp.int32, (n, n + halo_rows), 1)
    return jnp.concatenate([(cols == rows + halo_rows - k).astype(bf16) for k in shifts], axis=0)


def _shifted_rows(sel, halo, v):
    n = v.shape[0]
    out = _mm(sel, jnp.concatenate([halo, v], axis=0))
    return [out[j * n:(j + 1) * n] for j in range(sel.shape[0] // n)]


def _block_diag(v, w_ref, kind):
    return jnp.concatenate(
        [_mm(v[:, h * HEAD_DIM:(h + 1) * HEAD_DIM], w_ref[kind, h]) for h in range(HEADS)], axis=1)


def _block_diag_t(v, w_ref, kind):
    return jnp.concatenate(
        [_mm_nt(v[:, h * HEAD_DIM:(h + 1) * HEAD_DIM], w_ref[kind, h]) for h in range(HEADS)], axis=1)


def _exchange(arrays, gather, name, pieces=1):
    n = len(arrays)
    peers = N_DEV - 1

    def body(*refs):
        ins, outs = refs[:n], refs[n:2 * n]
        send_sems, recv_sems, local_sems = refs[2 * n:]
        x, y, c = lax.axis_index("x"), lax.axis_index("y"), lax.axis_index("c")
        me = 4 * x + 2 * y + c
        local = []
        for k in range(n):
            cp = pltpu.make_async_copy(ins[k] if gather else ins[k].at[me], outs[k].at[me], local_sems.at[k])
            cp.start()
            local.append(cp)
        remote = _peer_copies(ins, outs, send_sems, recv_sems, gather, pieces)
        for cp in remote:
            cp.start()
        for cp in remote:
            cp.wait()
        for cp in local:
            cp.wait()

    out_shape = tuple(
        jax.ShapeDtypeStruct(((N_DEV,) + a.shape) if gather else a.shape, a.dtype) for a in arrays)
    outs = pl.pallas_call(
        body, name=name, out_shape=out_shape,
        in_specs=[pl.BlockSpec(memory_space=pl.ANY)] * n,
        out_specs=tuple(pl.BlockSpec(memory_space=pl.ANY) for _ in range(n)),
        scratch_shapes=[pltpu.SemaphoreType.DMA((n * pieces * peers,)), pltpu.SemaphoreType.DMA((n * pieces * peers,)),
                        pltpu.SemaphoreType.DMA((n,))],
        compiler_params=pltpu.CompilerParams(has_side_effects=True),
    )(*arrays)
    return list(outs)


_HBM = pl.BlockSpec(memory_space=pltpu.HBM)
_SEM = pl.BlockSpec(memory_space=pltpu.SEMAPHORE)
_DATAFLOW = pltpu.SideEffectType.DATAFLOW_SIDE_EFFECTING


def _peer_copies(src_refs, land_refs, send_sems, recv_sems, gather, pieces=1):
    x, y, c = lax.axis_index("x"), lax.axis_index("y"), lax.axis_index("c")
    me = 4 * x + 2 * y + c
    peers = N_DEV - 1
    copies = []
    for p in range(1, N_DEV):
        px = 1 - x if p & 4 else x
        py = 1 - y if p & 2 else y
        pc = 1 - c if p & 1 else c
        for k in range(len(src_refs)):
            block = src_refs[k] if gather else src_refs[k].at[4 * px + 2 * py + pc]
            dst = land_refs[k].at[me]
            rows = block.shape[0] // pieces
            for r in range(pieces):
                part = pl.ds(r * rows, rows)
                sem = (k * pieces + r) * peers + p - 1
                copies.append(pltpu.make_async_remote_copy(
                    src_ref=block.at[part] if pieces > 1 else block, dst_ref=dst.at[part] if pieces > 1 else dst,
                    send_sem=send_sems.at[sem], recv_sem=recv_sems.at[sem],
                    device_id=(px, py, pc), device_id_type=pl.DeviceIdType.MESH))
    return copies


def _landing(srcs, gather, me):
    out = []
    for a in srcs:
        own = a if gather else lax.dynamic_index_in_dim(a, me, 0, keepdims=False)
        out.append(lax.dynamic_update_index_in_dim(lax.empty((N_DEV,) + own.shape, own.dtype), own, me, 0))
    return out


def _send_start(name, groups, gather, me, pieces=1, after=None):
    sizes = [len(g) for g in groups]
    srcs = [a for g in groups for a in g]
    n = len(srcs)
    lands = _landing(srcs, gather, me)
    ng = len(groups)
    first = [sum(sizes[:g]) for g in range(ng)]
    extra = [] if after is None else [after]

    def body(*refs):
        src_refs, land_refs = refs[:n], refs[n:2 * n]
        sems, token = refs[2 * n + len(extra):2 * n + len(extra) + 2 * ng], refs[-1]
        for g in range(ng):
            part = slice(first[g], first[g] + sizes[g])
            for cp in _peer_copies(src_refs[part], land_refs[part], sems[2 * g], sems[2 * g + 1], gather, pieces):
                cp.start()
        token[...] = jnp.zeros_like(token)

    sem_shapes = [pltpu.SemaphoreType.DMA((sizes[g // 2] * pieces * (N_DEV - 1),)) for g in range(2 * ng)]
    outs = pl.pallas_call(
        body, name=name,
        out_shape=(*sem_shapes, *[pltpu.HBM(a.shape, a.dtype) for a in (*srcs, *lands)], jax.ShapeDtypeStruct((8, 128), f32)),
        in_specs=[_HBM] * (2 * n) + [pl.BlockSpec(memory_space=pl.ANY)] * len(extra),
        out_specs=(*[_SEM] * (2 * ng), *[_HBM] * (2 * n), pl.BlockSpec(memory_space=pltpu.VMEM)),
        input_output_aliases={k: 2 * ng + k for k in range(2 * n)},
        compiler_params=pltpu.CompilerParams(has_side_effects=_DATAFLOW),
    )(*[pltpu.with_memory_space_constraint(a, pltpu.HBM) for a in (*srcs, *lands)], *extra)
    srcs_thru, lands_thru = outs[2 * ng:2 * ng + n], outs[2 * ng + n:2 * ng + 2 * n]
    handles = [(outs[2 * g], outs[2 * g + 1], list(srcs_thru[first[g]:first[g] + sizes[g]]),
                list(lands_thru[first[g]:first[g] + sizes[g]]), gather, pieces) for g in range(ng)]
    return handles, outs[-1]


def _send_wait(name, handle, after):
    send_sems, recv_sems, srcs, lands, gather, pieces = handle
    n = len(srcs)
    after = list(after) if isinstance(after, (list, tuple)) else [after]

    def body(*refs):
        src_refs, land_refs = refs[:n], refs[n:2 * n]
        for cp in _peer_copies(src_refs, land_refs, refs[2 * n], refs[2 * n + 1], gather, pieces):
            cp.wait_send()
            cp.wait_recv()
        refs[-1][...] = jnp.zeros_like(refs[-1])

    outs = pl.pallas_call(
        body, name=name,
        out_shape=(*[pltpu.HBM(a.shape, a.dtype) for a in (*srcs, *lands)], jax.ShapeDtypeStruct((8, 128), f32)),
        in_specs=[_HBM] * (2 * n) + [_SEM, _SEM] + [pl.BlockSpec(memory_space=pl.ANY)] * len(after),
        out_specs=(*[_HBM] * (2 * n), pl.BlockSpec(memory_space=pltpu.VMEM)),
        input_output_aliases={k: k for k in range(2 * n)},
        compiler_params=pltpu.CompilerParams(has_side_effects=_DATAFLOW),
    )(*srcs, *lands, send_sems, recv_sems, *after)
    return list(outs[n:2 * n]), outs[-1]


def _mod_part(c_all, w_mod):
    depth, d, cols = w_mod.shape

    def body(c_ref, w_ref, o_ref):
        cv = c_ref[...]
        cond = cv * _sigmoid(cv)
        o_ref[...] = jnp.dot(cond, w_ref[...], preferred_element_type=f32, precision=lax.Precision.HIGHEST)

    return pl.pallas_call(
        body, name="mod_part", grid=(depth,),
        out_shape=jax.ShapeDtypeStruct((depth, N_DEV, cols), f32),
        in_specs=[pl.BlockSpec((N_DEV, d), lambda i: (0, 0)), pl.BlockSpec((None, d, cols), lambda i: (i, 0, 0))],
        out_specs=pl.BlockSpec((None, N_DEV, cols), lambda i: (i, 0, 0)),
        compiler_params=_params(("arbitrary",), 32),
    )(c_all, _in_hbm(w_mod))


def _mod_table(mod_row, b_mod, g_mix, g_ffn):
    def body(m_ref, b_ref, gm_ref, gf_ref, o_ref, token_ref):
        for i in range(DEPTH):
            for k in range(N_MOD):
                o_ref[i, k:k + 1, :] = m_ref[i:i + 1, k * D_MODEL:(k + 1) * D_MODEL] + b_ref[i:i + 1, k * D_MODEL:(k + 1) * D_MODEL]
            o_ref[i, R_GS_M:R_GS_M + 1, :] = gm_ref[i:i + 1, :] * (1.0 + o_ref[i, R_SC_M:R_SC_M + 1, :])
            o_ref[i, R_GS_F:R_GS_F + 1, :] = gf_ref[i:i + 1, :] * (1.0 + o_ref[i, R_SC_F:R_SC_F + 1, :])
        token_ref[...] = jnp.zeros_like(token_ref)

    return pl.pallas_call(
        body, name="mod_table",
        out_shape=(jax.ShapeDtypeStruct((DEPTH, 8, D_MODEL), f32), jax.ShapeDtypeStruct((8, 128), f32)))(
        mod_row, b_mod, g_mix, g_ffn)


def _ffn_tile(s):
    return min(512, s)


def _layer_weights(shape):
    return pl.BlockSpec((N_DEV,) + shape, lambda i: (0, 0, 0))


def _ffn_fwd(x, vec, w1g, w2g, layer):
    s = x.shape[0]
    ts = _ffn_tile(s)

    def body(x_ref, vec_ref, w1_ref, w2_ref, xo_ref, u_ref, y_ref, hb_ref):
        xv = x_ref[...]
        n, _ = _rms(xv)
        hb = (n * vec_ref[R_GS_F:R_GS_F + 1, :] + vec_ref[R_SH_F:R_SH_F + 1, :]).astype(bf16)
        hb_ref[...] = hb
        yv = jnp.zeros((ts, D_MODEL), f32)
        for f in range(N_DEV):
            u = jnp.maximum(_mm(hb, w1_ref[f]), 0.0)
            u_ref[:, f * FF_CHUNK:(f + 1) * FF_CHUNK] = u.astype(bf16)
            yv = yv + _mm((u * u).astype(bf16), w2_ref[f])
        y_ref[...] = yv.astype(bf16)
        xo_ref[...] = xv + vec_ref[R_GT_F:R_GT_F + 1, :] * yv

    row = pl.BlockSpec((ts, D_MODEL), lambda i: (i, 0))
    return pl.pallas_call(
        body, name=f"ffn_fwd_{layer}", grid=(s // ts,),
        out_shape=(jax.ShapeDtypeStruct((s, D_MODEL), f32), jax.ShapeDtypeStruct((s, D_FF), bf16),
                   jax.ShapeDtypeStruct((s, D_MODEL), bf16), jax.ShapeDtypeStruct((s, D_MODEL), bf16)),
        in_specs=[row, pl.BlockSpec((8, D_MODEL), lambda i: (0, 0)),
                  _layer_weights((D_MODEL, FF_CHUNK)), _layer_weights((FF_CHUNK, D_MODEL))],
        out_specs=(row, pl.BlockSpec((ts, D_FF), lambda i: (i, 0)), row, row),
        compiler_params=_params(("arbitrary",), 56),
    )(x, vec, w1g, w2g)


def _ffn_bwd_act(x, dx, u, y, vec, w1g, w2g, layer):
    s = x.shape[0]
    ts = _ffn_tile(s)

    def body(x_ref, dx_ref, u_ref, y_ref, vec_ref, w1_ref, w2_ref, dxo_ref, da_ref, dyb_ref, sm_ref):
        @pl.when(pl.program_id(0) == 0)
        def _():
            sm_ref[...] = jnp.zeros_like(sm_ref)

        dxv = dx_ref[...]
        dyb = (dxv * vec_ref[R_GT_F:R_GT_F + 1, :]).astype(bf16)
        dyb_ref[...] = dyb
        sm_ref[G_GT:G_GT + 1, :] += _colsum(dxv * y_ref[...].astype(f32))
        dh = jnp.zeros((ts, D_MODEL), f32)
        for f in range(N_DEV):
            cols = slice(f * FF_CHUNK, (f + 1) * FF_CHUNK)
            dz = _mm_nt(dyb, w2_ref[f])
            dab = (dz * (2.0 * u_ref[:, cols].astype(f32))).astype(bf16)
            da_ref[:, cols] = dab
            dh = dh + _mm_nt(dab, w1_ref[f])
        n, r = _rms(x_ref[...])
        sm_ref[G_SH:G_SH + 1, :] += _colsum(dh)
        sm_ref[G_GS:G_GS + 1, :] += _colsum(dh * n)
        dxo_ref[...] = dxv + _norm_bwd(dh, n, r, vec_ref[R_GS_F:R_GS_F + 1, :])

    row = pl.BlockSpec((ts, D_MODEL), lambda i: (i, 0))
    wide = pl.BlockSpec((ts, D_FF), lambda i: (i, 0))
    return pl.pallas_call(
        body, name=f"ffn_bwd_act_{layer}", grid=(s // ts,),
        out_shape=(jax.ShapeDtypeStruct((s, D_MODEL), f32), jax.ShapeDtypeStruct((s, D_FF), bf16),
                   jax.ShapeDtypeStruct((s, D_MODEL), bf16), jax.ShapeDtypeStruct((8, D_MODEL), f32)),
        in_specs=[row, row, wide, row, pl.BlockSpec((8, D_MODEL), lambda i: (0, 0)),
                  _layer_weights((D_MODEL, FF_CHUNK)), _layer_weights((FF_CHUNK, D_MODEL))],
        out_specs=(row, wide, row, pl.BlockSpec((8, D_MODEL), lambda i: (0, 0))),
        compiler_params=_params(("arbitrary",), 58),
    )(x, dx, u, y, vec, w1g, w2g)


def _ffn_bwd_w1(hb, da, layer):
    s = hb.shape[0]
    ts = _ffn_tile(s)
    nt = s // ts

    def body(hb_ref, da_ref, dw_ref, acc_ref):
        i = pl.program_id(0)

        @pl.when(i == 0)
        def _():
            acc_ref[...] = jnp.zeros_like(acc_ref)

        hb = hb_ref[...]
        for f in range(N_DEV):
            acc_ref[f] += _mm_tn(hb, da_ref[:, f * FF_CHUNK:(f + 1) * FF_CHUNK])

        @pl.when(i == nt - 1)
        def _():
            dw_ref[...] = acc_ref[...].astype(bf16)

    return pl.pallas_call(
        body, name=f"ffn_bwd_w1_{layer}", grid=(nt,),
        out_shape=jax.ShapeDtypeStruct((N_DEV, D_MODEL, FF_CHUNK), bf16),
        in_specs=[pl.BlockSpec((ts, D_MODEL), lambda i: (i, 0)), pl.BlockSpec((ts, D_FF), lambda i: (i, 0))],
        out_specs=pl.BlockSpec((N_DEV, D_MODEL, FF_CHUNK), lambda i: (0, 0, 0)),
        scratch_shapes=[pltpu.VMEM((N_DEV, D_MODEL, FF_CHUNK), f32)],
        compiler_params=_params(("arbitrary",), 56),
    )(hb, da)


def _ffn_bwd_w2(u, dyb, layer):
    s = u.shape[0]
    ts = _ffn_tile(s)
    nt = s // ts

    def body(u_ref, dyb_ref, dw_ref, acc_ref):
        i = pl.program_id(0)

        @pl.when(i == 0)
        def _():
            acc_ref[...] = jnp.zeros_like(acc_ref)

        dyb = dyb_ref[...]
        for f in range(N_DEV):
            uv = u_ref[:, f * FF_CHUNK:(f + 1) * FF_CHUNK].astype(f32)
            acc_ref[f] += _mm_tn((uv * uv).astype(bf16), dyb)

        @pl.when(i == nt - 1)
        def _():
            dw_ref[...] = acc_ref[...].astype(bf16)

    return pl.pallas_call(
        body, name=f"ffn_bwd_w2_{layer}", grid=(nt,),
        out_shape=jax.ShapeDtypeStruct((N_DEV, FF_CHUNK, D_MODEL), bf16),
        in_specs=[pl.BlockSpec((ts, D_FF), lambda i: (i, 0)), pl.BlockSpec((ts, D_MODEL), lambda i: (i, 0))],
        out_specs=pl.BlockSpec((N_DEV, FF_CHUNK, D_MODEL), lambda i: (0, 0, 0)),
        scratch_shapes=[pltpu.VMEM((N_DEV, FF_CHUNK, D_MODEL), f32)],
        compiler_params=_params(("arbitrary",), 56),
    )(u, dyb)


def _lru_gates(xc, wsm_ref, pv_ref):
    xcb = xc.astype(bf16)
    gr = _sigmoid(_block_diag(xcb, wsm_ref, 0) + pv_ref[P_BA:P_BA + 1, :])
    gi = _sigmoid(_block_diag(xcb, wsm_ref, 1) + pv_ref[P_BX:P_BX + 1, :])
    log_a = (LRU_C * _log_sigmoid(pv_ref[P_LAM:P_LAM + 1, :])) * gr
    t = jnp.tanh(log_a)
    return gr, gi, jnp.exp(log_a), jnp.sqrt((-2.0 * t) / (1.0 - t))


def _conv(xr, taps_before, pv_ref):
    xc = xr * pv_ref[P_CW0 + 3:P_CW0 + 4, :] + pv_ref[P_CONVB:P_CONVB + 1, :]
    for k, v in zip((2, 1, 0), taps_before):
        xc = xc + v * pv_ref[P_CW0 + k:P_CW0 + k + 1, :]
    return xc


LRU_FWD_SUB, LRU_FWD_SUBS = 128, 2
LRU_BWD_SUB, LRU_BWD_SUBS = 256, 1


def _scan_rows(a, u, carry, reverse):
    groups = a.shape[0] // SUBLANES
    row = lax.broadcasted_iota(jnp.int32, (SUBLANES, a.shape[1]), 0)
    outs = [None] * groups
    for j in range(groups):
        g = groups - 1 - j if reverse else j
        av, uv = a[g * SUBLANES:(g + 1) * SUBLANES], u[g * SUBLANES:(g + 1) * SUBLANES]
        for k in (1, 2, 4):
            if reverse:
                valid, shift = row < SUBLANES - k, SUBLANES - k
            else:
                valid, shift = row >= k, k
            a_s = jnp.where(valid, pltpu.roll(av, shift, 0), 1.0)
            u_s = jnp.where(valid, pltpu.roll(uv, shift, 0), 0.0)
            uv = uv + av * u_s
            av = av * a_s
        h = uv + av * carry
        outs[g] = h
        carry = h[0:1, :] if reverse else h[SUBLANES - 1:SUBLANES, :]
    return jnp.concatenate(outs, axis=0), carry


def _lru_fwd(x, vec, wbig, wsm, pvec, layer):
    s = x.shape[0]
    sub = min(LRU_FWD_SUB, s)
    ts = min(sub * LRU_FWD_SUBS, s)
    nsub = ts // sub
    w = LRU_WIDTH

    def body(x_ref, vec_ref, wb_ref, wsm_ref, pv_ref, xo_ref, xr_ref, hs_ref, a_ref, mult_ref, gr_ref, gi_ref,
             gel_ref, geld_ref, y_ref, tail_ref, carry_ref):
        @pl.when(pl.program_id(0) == 0)
        def _():
            tail_ref[...] = jnp.zeros_like(tail_ref)
            carry_ref[...] = jnp.zeros_like(carry_ref)

        sel = _shift_matrix(sub, BF16_ROWS, (1, 2, 3))
        for k in range(nsub):
            rows = slice(k * sub, (k + 1) * sub)
            xv = x_ref[rows, :]
            n, _ = _rms(xv)
            hb = (n * vec_ref[R_GS_M:R_GS_M + 1, :] + vec_ref[R_SH_M:R_SH_M + 1, :]).astype(bf16)
            gelu_v, gelu_d = _gelu_and_grad(_mm(hb, wb_ref[0]) + pv_ref[P_BY:P_BY + 1, :])
            gel_ref[rows, :] = gelu_v.astype(bf16)
            geld_ref[rows, :] = gelu_d.astype(bf16)
            xrb = (_mm(hb, wb_ref[1]) + pv_ref[P_BIN:P_BIN + 1, :]).astype(bf16)
            xr_ref[rows, :] = xrb
            xc = _conv(xrb.astype(f32), _shifted_rows(sel, tail_ref[...], xrb), pv_ref)
            tail_ref[...] = xrb[sub - BF16_ROWS:, :]
            gr, gi, a, mult = _lru_gates(xc, wsm_ref, pv_ref)
            gr_ref[rows, :] = gr.astype(bf16)
            gi_ref[rows, :] = gi.astype(bf16)
            a_ref[rows, :] = a
            mult_ref[rows, :] = mult
            hs, carry = _scan_rows(a, mult * (gi * xc), carry_ref[0:1, :], reverse=False)
            carry_ref[0:1, :] = carry
            hs_ref[rows, :] = hs
            yv = _mm((hs * gelu_v).astype(bf16), wb_ref[2]) + pv_ref[P_BOUT:P_BOUT + 1, :]
            y_ref[rows, :] = yv.astype(bf16)
            xo_ref[rows, :] = xv + vec_ref[R_GT_M:R_GT_M + 1, :] * yv

    row = pl.BlockSpec((ts, D_MODEL), lambda i: (i, 0))
    roww = pl.BlockSpec((ts, w), lambda i: (i, 0))
    wide = lambda dt: jax.ShapeDtypeStruct((s, w), dt)
    return pl.pallas_call(
        body, name=f"lru_fwd_{layer}", grid=(s // ts,),
        out_shape=(jax.ShapeDtypeStruct((s, D_MODEL), f32), wide(bf16), wide(f32), wide(f32), wide(f32),
                   wide(bf16), wide(bf16), wide(bf16), wide(bf16), jax.ShapeDtypeStruct((s, D_MODEL), bf16)),
        in_specs=[row, pl.BlockSpec((8, D_MODEL), lambda i: (0, 0)),
                  pl.BlockSpec((3, w, w), lambda i: (0, 0, 0)),
                  pl.BlockSpec((2, HEADS, HEAD_DIM, HEAD_DIM), lambda i: (0, 0, 0, 0)),
                  pl.BlockSpec((16, w), lambda i: (0, 0))],
        out_specs=(row, roww, roww, roww, roww, roww, roww, roww, roww, row),
        scratch_shapes=[pltpu.VMEM((BF16_ROWS, w), bf16), pltpu.VMEM((SUBLANES, w), f32)],
        compiler_params=_params(("arbitrary",)),
    )(x, vec, wbig, wsm, pvec)


def _lru_bwd(x, dx, saved, vec, wbig, wsm, pvec, layer):
    xr, hs, a_all, mult_all, gr_all, gi_all, gel_all, geld_all, y = saved
    s = x.shape[0]
    sub = min(LRU_BWD_SUB, s)
    ts = min(sub * LRU_BWD_SUBS, s)
    nsub = ts // sub
    nt = s // ts
    w = LRU_WIDTH
    shard = w // N_DEV
    hshard = HEAD_DIM // N_DEV

    def body(x_ref, dx_ref, xr_ref, xrh_ref, hs_ref, hsh_ref, a_ref, mult_ref, gr_ref, gi_ref, gel_ref, geld_ref,
             y_ref, vec_ref, wb_ref, wsm_ref, pv_ref,
             dxo_ref, dwb_ref, dwsm_ref, sm_ref, accb_ref, accs_ref, eps_ref, dxc8_ref,
             hb_scr, dgb_scr, dxrb_scr, mb_scr, dyb_scr, xcb_scr, drab_scr, drxb_scr):
        i = pl.program_id(0)
        first_tile = i == nt - 1

        @pl.when(i == 0)
        def _():
            accb_ref[...] = jnp.zeros_like(accb_ref)
            accs_ref[...] = jnp.zeros_like(accs_ref)
            sm_ref[...] = jnp.zeros_like(sm_ref)
            eps_ref[...] = jnp.zeros_like(eps_ref)
            dxc8_ref[...] = jnp.zeros_like(dxc8_ref)

        gs = vec_ref[R_GS_M:R_GS_M + 1, :]
        c_ls = LRU_C * _log_sigmoid(pv_ref[P_LAM:P_LAM + 1, :])
        for k in reversed(range(nsub)):
            rows = slice(k * sub, (k + 1) * sub)
            xv = x_ref[rows, :]
            dxv = dx_ref[rows, :]
            n, r = _rms(xv)
            hb_scr[rows, :] = (n * gs + vec_ref[R_SH_M:R_SH_M + 1, :]).astype(bf16)
            xrv = xr_ref[rows, :].astype(f32)
            hsv = hs_ref[rows, :]
            if k == 0:
                xr_halo = jnp.where(first_tile, 0.0, xrh_ref[...].astype(f32))
                hs_halo = jnp.where(first_tile, 0.0, hsh_ref[...])
            else:
                xr_halo = xr_ref[k * sub - BF16_ROWS:k * sub, :].astype(f32)
                hs_halo = hs_ref[k * sub - SUBLANES:k * sub, :]
            xs1, xs2, xs3 = _rows_before(xr_halo, xrv, (1, 2, 3))
            xc = _conv(xrv, (xs1, xs2, xs3), pv_ref)
            xcb_scr[rows, :] = xc.astype(bf16)
            a, mult = a_ref[rows, :], mult_ref[rows, :]
            gr, gi = gr_ref[rows, :].astype(f32), gi_ref[rows, :].astype(f32)
            gelu_v = gel_ref[rows, :].astype(f32)

            dy = dxv * vec_ref[R_GT_M:R_GT_M + 1, :]
            dyb = dy.astype(bf16)
            dyb_scr[rows, :] = dyb
            sm_ref[G_GT:G_GT + 1, :] += _colsum(dxv * y_ref[rows, :].astype(f32))
            sm_ref[G_BOUT:G_BOUT + 1, :] += _colsum(dy)
            mb_scr[rows, :] = (hsv * gelu_v).astype(bf16)
            dm = _mm_nt(dyb, wb_ref[2])
            dhs = dm * gelu_v
            dgpre = dm * hsv * geld_ref[rows, :].astype(f32)
            dgb = dgpre.astype(bf16)
            dgb_scr[rows, :] = dgb
            sm_ref[G_BY:G_BY + 1, :] += _colsum(dgpre)

            eps_in = eps_ref[0:1, :]
            eps, eps_out = _scan_rows(a, a * dhs, eps_in, reverse=True)
            eps_ref[0:1, :] = eps_out
            (eps_next,) = _rows_after(eps, jnp.broadcast_to(eps_in, (SUBLANES, w)), (1,))
            delta = dhs + eps_next
            (h_prev,) = _rows_before(hs_halo, hsv, (1,))
            dxi = delta * xc
            dgi = dxi * mult
            dla = (delta * h_prev) * a - (dxi * gi) * (a * a) / mult
            sm_ref[G_LS:G_LS + 1, :] += _colsum(dla * gr)
            dra = (dla * c_ls) * (gr - gr * gr)
            drx = dgi * (gi - gi * gi)
            drab, drxb = dra.astype(bf16), drx.astype(bf16)
            drab_scr[rows, :] = drab
            drxb_scr[rows, :] = drxb
            sm_ref[G_BA:G_BA + 1, :] += _colsum(dra)
            sm_ref[G_BX:G_BX + 1, :] += _colsum(drx)
            dxc = (delta * mult) * gi + _block_diag_t(drab, wsm_ref, 0) + _block_diag_t(drxb, wsm_ref, 1)

            sm_ref[G_CONVB:G_CONVB + 1, :] += _colsum(dxc)
            for kk, v in zip((3, 2, 1, 0), (xrv, xs1, xs2, xs3)):
                sm_ref[G_CW0 + kk:G_CW0 + kk + 1, :] += _colsum(dxc * v)
            ups = _rows_after(dxc, dxc8_ref[...], (1, 2, 3))
            dxc8_ref[...] = dxc[0:SUBLANES, :]
            dxr = dxc * pv_ref[P_CW0 + 3:P_CW0 + 4, :]
            for kk, v in zip((2, 1, 0), ups):
                dxr = dxr + v * pv_ref[P_CW0 + kk:P_CW0 + kk + 1, :]
            dxrb = dxr.astype(bf16)
            dxrb_scr[rows, :] = dxrb
            sm_ref[G_BIN:G_BIN + 1, :] += _colsum(dxr)
            dh = _mm_nt(dgb, wb_ref[0]) + _mm_nt(dxrb, wb_ref[1])
            sm_ref[G_SH:G_SH + 1, :] += _colsum(dh)
            sm_ref[G_GS:G_GS + 1, :] += _colsum(dh * n)
            dxo_ref[rows, :] = dxv + _norm_bwd(dh, n, r, gs)

        hb = hb_scr[...]
        accb_ref[0] += _mm_tn(hb, dgb_scr[...])
        accb_ref[1] += _mm_tn(hb, dxrb_scr[...])
        accb_ref[2] += _mm_tn(mb_scr[...], dyb_scr[...])
        for h in range(HEADS):
            cols = slice(h * HEAD_DIM, (h + 1) * HEAD_DIM)
            accs_ref[0, h] += _mm_tn(xcb_scr[:, cols], drab_scr[:, cols])
            accs_ref[1, h] += _mm_tn(xcb_scr[:, cols], drxb_scr[:, cols])

        @pl.when(i == nt - 1)
        def _():
            sm_ref[G_LS:G_LS + 1, :] = sm_ref[G_LS:G_LS + 1, :] * LRU_C
            for k in range(3):
                dwb_ref[:, k] = accb_ref[k].astype(bf16).reshape(N_DEV, shard, w)
            for k in range(2):
                for h in range(HEADS):
                    dwsm_ref[:, k, h] = accs_ref[k, h].astype(bf16).reshape(N_DEV, hshard, HEAD_DIM)

    rev = lambda i: (nt - 1 - i, 0)
    row = pl.BlockSpec((ts, D_MODEL), rev)
    roww = pl.BlockSpec((ts, w), rev)
    halo16 = pl.BlockSpec((BF16_ROWS, w), lambda i: (jnp.maximum((nt - 1 - i) * (ts // BF16_ROWS) - 1, 0), 0))
    halo8 = pl.BlockSpec((SUBLANES, w), lambda i: (jnp.maximum((nt - 1 - i) * (ts // SUBLANES) - 1, 0), 0))
    const = lambda *shape: pl.BlockSpec(shape, lambda i: (0,) * len(shape))
    operand = pltpu.VMEM((ts, w), bf16)
    return pl.pallas_call(
        body, name=f"lru_bwd_{layer}", grid=(nt,),
        out_shape=(jax.ShapeDtypeStruct((s, D_MODEL), f32),
                   jax.ShapeDtypeStruct((N_DEV, 3, shard, w), bf16),
                   jax.ShapeDtypeStruct((N_DEV, 2, HEADS, hshard, HEAD_DIM), bf16),
                   jax.ShapeDtypeStruct((16, w), f32)),
        in_specs=[row, row, roww, halo16, roww, halo8, roww, roww, roww, roww, roww, roww, row, const(8, D_MODEL),
                  const(3, w, w), const(2, HEADS, HEAD_DIM, HEAD_DIM), const(16, w)],
        out_specs=(row, const(N_DEV, 3, shard, w), const(N_DEV, 2, HEADS, hshard, HEAD_DIM), const(16, w)),
        scratch_shapes=[pltpu.VMEM((3, w, w), f32), pltpu.VMEM((2, HEADS, HEAD_DIM, HEAD_DIM), f32),
                        pltpu.VMEM((SUBLANES, w), f32), pltpu.VMEM((SUBLANES, w), f32)] + [operand] * 8,
        compiler_params=_params(("arbitrary",), 58),
    )(x, dx, xr, xr, hs, hs, a_all, mult_all, gr_all, gi_all, gel_all, geld_all, y, vec, wbig, wsm, pvec)


def _pool_tile(s):
    return min(256, s)


def _pool_counts(tile_index, ts):
    t = (tile_index * ts + lax.broadcasted_iota(jnp.int32, (ts, 1), 0) + 1).astype(f32)
    return [1.0 / jnp.minimum(t, float(win)) for win in POOL_WINDOWS]


def _pooled(h, halo, inv):
    ext = jnp.concatenate([halo, h], axis=0)
    out = []
    for g in range(len(POOL_WINDOWS)):
        acc = ext[:, g * HEAD_DIM:(g + 1) * HEAD_DIM]
        for step in range(g + 1):
            acc = acc + pltpu.roll(acc, 1 << step, 0)
        out.append(acc[POOL_HALO:] * inv[g] - h[:, g * HEAD_DIM:(g + 1) * HEAD_DIM])
    return out


def _pool_fwd(x, vec, pw, ps, layer):
    s = x.shape[0]
    ts = _pool_tile(s)

    def body(x_ref, vec_ref, pw_ref, ps_ref, xo_ref, y_ref, halo_ref):
        i = pl.program_id(0)

        @pl.when(i == 0)
        def _():
            halo_ref[...] = jnp.zeros_like(halo_ref)

        xv = x_ref[...]
        n, _ = _rms(xv)
        h = n * vec_ref[R_GS_M:R_GS_M + 1, :] + vec_ref[R_SH_M:R_SH_M + 1, :]
        pooled = _pooled(h, halo_ref[...], _pool_counts(i, ts))
        halo_ref[...] = h[ts - POOL_HALO:, :]
        mixed = jnp.concatenate([_mm(pooled[g].astype(bf16), pw_ref[g]) for g in range(HEADS)], axis=1)
        yv = mixed * ps_ref[0:1, :]
        y_ref[...] = yv.astype(bf16)
        xo_ref[...] = xv + vec_ref[R_GT_M:R_GT_M + 1, :] * yv

    row = pl.BlockSpec((ts, D_MODEL), lambda i: (i, 0))
    return pl.pallas_call(
        body, name=f"pool_fwd_{layer}", grid=(s // ts,),
        out_shape=(jax.ShapeDtypeStruct((s, D_MODEL), f32), jax.ShapeDtypeStruct((s, D_MODEL), bf16)),
        in_specs=[row, pl.BlockSpec((8, D_MODEL), lambda i: (0, 0)),
                  pl.BlockSpec((HEADS, HEAD_DIM, HEAD_DIM), lambda i: (0, 0, 0)),
                  pl.BlockSpec((8, D_MODEL), lambda i: (0, 0))],
        out_specs=(row, row),
        scratch_shapes=[pltpu.VMEM((POOL_HALO, D_MODEL), f32)],
        compiler_params=_params(("arbitrary",)),
    )(x, vec, pw, ps)


def _pool_bwd(x, dx, y, vec, pw, ps, layer):
    s = x.shape[0]
    ts = _pool_tile(s)
    nt = s // ts
    hshard = HEAD_DIM // N_DEV

    def body(x_ref, xh_ref, dx_ref, y_ref, vec_ref, pw_ref, ps_ref, dxo_ref, dpw_ref, sm_ref, acc_ref, q16_ref):
        i = pl.program_id(0)
        tile = nt - 1 - i

        @pl.when(i == 0)
        def _():
            acc_ref[...] = jnp.zeros_like(acc_ref)
            sm_ref[...] = jnp.zeros_like(sm_ref)
            q16_ref[...] = jnp.zeros_like(q16_ref)

        gs, sh = vec_ref[R_GS_M:R_GS_M + 1, :], vec_ref[R_SH_M:R_SH_M + 1, :]
        xv = x_ref[...]
        dxv = dx_ref[...]
        n, r = _rms(xv)
        h = n * gs + sh
        nh, _ = _rms(xh_ref[...])
        halo = jnp.where(tile == 0, 0.0, nh * gs + sh)
        inv = _pool_counts(tile, ts)
        pooled = _pooled(h, halo, inv)
        mixed = jnp.concatenate([_mm(pooled[g].astype(bf16), pw_ref[g]) for g in range(HEADS)], axis=1)

        dy = dxv * vec_ref[R_GT_M:R_GT_M + 1, :]
        sm_ref[G_GT:G_GT + 1, :] += _colsum(dxv * y_ref[...].astype(f32))
        sm_ref[3:4, :] += _colsum(dy * mixed)
        dmixed = (dy * ps_ref[0:1, :]).astype(bf16)
        dh_parts = []
        for g in range(HEADS):
            cols = slice(g * HEAD_DIM, (g + 1) * HEAD_DIM)
            acc_ref[g] += _mm_tn(pooled[g].astype(bf16), dmixed[:, cols])
            dpooled = _mm_nt(dmixed[:, cols], pw_ref[g])
            q = dpooled * inv[g]
            ext = jnp.concatenate([q, q16_ref[:, cols]], axis=0)
            q16_ref[:, cols] = q[0:POOL_HALO, :]
            for step in range(g + 1):
                ext = ext + pltpu.roll(ext, ext.shape[0] - (1 << step), 0)
            dh_parts.append(ext[:ts] - dpooled)
        dh = jnp.concatenate(dh_parts, axis=1)
        sm_ref[G_SH:G_SH + 1, :] += _colsum(dh)
        sm_ref[G_GS:G_GS + 1, :] += _colsum(dh * n)
        dxo_ref[...] = dxv + _norm_bwd(dh, n, r, gs)

        @pl.when(i == nt - 1)
        def _():
            for g in range(HEADS):
                dpw_ref[:, g] = acc_ref[g].astype(bf16).reshape(N_DEV, hshard, HEAD_DIM)

    rev = lambda i: (nt - 1 - i, 0)
    row = pl.BlockSpec((ts, D_MODEL), rev)
    halo16 = pl.BlockSpec((POOL_HALO, D_MODEL), lambda i: (jnp.maximum((nt - 1 - i) * (ts // POOL_HALO) - 1, 0), 0))
    const = lambda *shape: pl.BlockSpec(shape, lambda i: (0,) * len(shape))
    return pl.pallas_call(
        body, name=f"pool_bwd_{layer}", grid=(nt,),
        out_shape=(jax.ShapeDtypeStruct((s, D_MODEL), f32),
                   jax.ShapeDtypeStruct((N_DEV, HEADS, hshard, HEAD_DIM), bf16),
                   jax.ShapeDtypeStruct((8, D_MODEL), f32)),
        in_specs=[row, halo16, row, row, const(8, D_MODEL), const(HEADS, HEAD_DIM, HEAD_DIM), const(8, D_MODEL)],
        out_specs=(row, const(N_DEV, HEADS, hshard, HEAD_DIM), const(8, D_MODEL)),
        scratch_shapes=[pltpu.VMEM((HEADS, HEAD_DIM, HEAD_DIM), f32), pltpu.VMEM((POOL_HALO, D_MODEL), f32)],
        compiler_params=_params(("arbitrary",)),
    )(x, x, dx, y, vec, pw, ps)


def _final(x, target, g_fin):
    s = x.shape[0]
    ts = min(512, s)

    def body(x_ref, t_ref, g_ref, dx_ref, sm_ref):
        @pl.when(pl.program_id(0) == 0)
        def _():
            sm_ref[...] = jnp.zeros_like(sm_ref)

        g = g_ref[0:1, :]
        n, r = _rms(x_ref[...])
        err = n * g - t_ref[...]
        sm_ref[1:2, :] += 0.5 * jnp.sum(jnp.mean(err * err, axis=-1, keepdims=True), axis=0, keepdims=True)
        dyv = err * (1.0 / D_MODEL)
        sm_ref[0:1, :] += _colsum(dyv * n)
        dx_ref[...] = _norm_bwd(dyv, n, r, g)

    row = pl.BlockSpec((ts, D_MODEL), lambda i: (i, 0))
    return pl.pallas_call(
        body, name="final_loss", grid=(s // ts,),
        out_shape=(jax.ShapeDtypeStruct((s, D_MODEL), f32), jax.ShapeDtypeStruct((8, D_MODEL), f32)),
        in_specs=[row, row, pl.BlockSpec((8, D_MODEL), lambda i: (0, 0))],
        out_specs=(row, pl.BlockSpec((8, D_MODEL), lambda i: (0, 0))),
        compiler_params=_params(("arbitrary",)),
    )(x, target, g_fin)


def _small_pack(sm_ffn, sm_mix, sm_fin, table, g_mix, g_ffn, lam):
    def body(*refs):
        ffn, mix = refs[0:DEPTH], refs[DEPTH:2 * DEPTH]
        fin_ref, tab_ref, gm_ref, gf_ref, lam_ref, o_ref = refs[2 * DEPTH:]
        o_ref[...] = jnp.zeros_like(o_ref)
        for i in range(DEPTH):
            base = K_MOD + i * N_MOD
            o_ref[base + 0:base + 1, :] = mix[i][G_SH:G_SH + 1, :]
            o_ref[base + 1:base + 2, :] = mix[i][G_GS:G_GS + 1, :] * gm_ref[i:i + 1, :]
            o_ref[base + 2:base + 3, :] = mix[i][G_GT:G_GT + 1, :]
            o_ref[base + 3:base + 4, :] = ffn[i][G_SH:G_SH + 1, :]
            o_ref[base + 4:base + 5, :] = ffn[i][G_GS:G_GS + 1, :] * gf_ref[i:i + 1, :]
            o_ref[base + 5:base + 6, :] = ffn[i][G_GT:G_GT + 1, :]
            o_ref[K_NMIX + i:K_NMIX + i + 1, :] = mix[i][G_GS:G_GS + 1, :] * (1.0 + tab_ref[i, R_SC_M:R_SC_M + 1, :])
            o_ref[K_NFFN + i:K_NFFN + i + 1, :] = ffn[i][G_GS:G_GS + 1, :] * (1.0 + tab_ref[i, R_SC_F:R_SC_F + 1, :])
            j = i // 2
            if i % 2 == 0:
                for k, src in enumerate((G_BY, G_BIN, G_CONVB, None, G_BOUT)):
                    dst = K_LRUB + j * 5 + k
                    if src is None:
                        o_ref[dst:dst + 1, :] = mix[i][G_LS:G_LS + 1, :] * _sigmoid(-lam_ref[j:j + 1, :])
                    else:
                        o_ref[dst:dst + 1, :] = mix[i][src:src + 1, :]
                o_ref[K_CONVW + j * 4:K_CONVW + j * 4 + 4, :] = mix[i][G_CW0:G_CW0 + 4, :]
                o_ref[K_BA + j:K_BA + j + 1, :] = mix[i][G_BA:G_BA + 1, :]
                o_ref[K_BX + j:K_BX + j + 1, :] = mix[i][G_BX:G_BX + 1, :]
            else:
                o_ref[K_PS + j:K_PS + j + 1, :] = mix[i][3:4, :]
        o_ref[K_FIN:K_FIN + 2, :] = fin_ref[0:2, :]

    return pl.pallas_call(body, name="small_pack", out_shape=jax.ShapeDtypeStruct((K_ROWS, D_MODEL), f32))(
        *sm_ffn, *sm_mix, sm_fin, table, g_mix, g_ffn, lam)


def _small_sum(gathered):
    def body(g_ref, o_ref, token_ref):
        tot = g_ref[0]
        for src in range(1, N_DEV):
            tot = tot + g_ref[src]
        o_ref[...] = tot
        token_ref[...] = jnp.zeros_like(token_ref)

    return pl.pallas_call(
        body, name="small_sum",
        out_shape=(jax.ShapeDtypeStruct(gathered.shape[1:], f32), jax.ShapeDtypeStruct((8, 128), f32)))(gathered)


def _adamw_math(g, w, m, v):
    m = ADAM_B1 * m + (1.0 - ADAM_B1) * g
    v = ADAM_B2 * v + (1.0 - ADAM_B2) * (g * g)
    m_hat = m / (1.0 - ADAM_B1 ** ADAM_STEP)
    v_hat = v / (1.0 - ADAM_B2 ** ADAM_STEP)
    delta = -ADAM_LR * (m_hat / (jnp.sqrt(v_hat) + ADAM_EPS) + ADAM_WD * w)
    return delta, m, v


def _adamw_small(name, g, w, m, v):
    shape = w.shape
    two_d = (1, shape[0]) if len(shape) == 1 else (math.prod(shape[:-1]), shape[-1])

    def body(g_ref, w_ref, m_ref, v_ref, d_ref, mo_ref, vo_ref):
        d_ref[...], mo_ref[...], vo_ref[...] = _adamw_math(g_ref[...], w_ref[...], m_ref[...], v_ref[...])

    outs = pl.pallas_call(body, name=f"adamw_{name}", out_shape=tuple(jax.ShapeDtypeStruct(two_d, f32) for _ in range(3)))(
        *(t.reshape(two_d) for t in (g, w, m, v)))
    return tuple(t.reshape(shape) for t in outs)


def _block_rows(rows, cols):
    tr = max(SUBLANES, min(rows, (512 * 1024) // (4 * cols)))
    while rows % tr:
        tr //= 2
    return tr


def _adamw_reduce(name, landings, kind, w, m, v):
    nl = len(landings)
    rows, cols = landings[0].shape[2:]
    tr = _block_rows(rows, cols)
    per_layer = rows // tr

    def body(*refs):
        l_refs = refs[:nl]
        w_ref, m_ref, v_ref, g_ref, d_ref, mo_ref, vo_ref = refs[nl:]
        layer = pl.program_id(0)
        for k in range(nl):
            @pl.when(layer == k)
            def _(k=k):
                g = l_refs[k][0].astype(f32)
                for src in range(1, N_DEV):
                    g = g + l_refs[k][src].astype(f32)
                g_ref[...] = g
        d_ref[...], mo_ref[...], vo_ref[...] = _adamw_math(g_ref[...], w_ref[...], m_ref[...], v_ref[...])

    blk = pl.BlockSpec((tr, cols), lambda l, r: (l * per_layer + r, 0))
    land = [pl.BlockSpec((N_DEV, None, tr, cols), lambda l, r, k=k: (0, kind, jnp.where(l == k, r, 0), 0)) for k in range(nl)]
    return pl.pallas_call(
        body, name=f"adamw_{name}", grid=(nl, per_layer),
        out_shape=tuple(jax.ShapeDtypeStruct((nl * rows, cols), f32) for _ in range(4)),
        in_specs=land + [blk, blk, blk],
        out_specs=(blk, blk, blk, blk),
        compiler_params=_params(("arbitrary", "arbitrary"), 32),
    )(*[_in_hbm(t) for t in (*landings, w, m, v)])


def _adamw_w_mod(c_all, dmod_all, w, m, v):
    depth, d, cols = w.shape
    tr = 256

    def body(c_ref, dm_ref, w_ref, m_ref, v_ref, g_ref, d_ref, mo_ref, vo_ref):
        cv = c_ref[...]
        cond = cv * _sigmoid(cv)
        g = lax.dot_general(cond, dm_ref[...], (((0,), (0,)), ((), ())), preferred_element_type=f32,
                            precision=lax.Precision.HIGHEST)
        g_ref[...] = g
        d_ref[...], mo_ref[...], vo_ref[...] = _adamw_math(g, w_ref[...], m_ref[...], v_ref[...])

    blk = pl.BlockSpec((None, tr, cols), lambda i, r: (i, r, 0))
    return pl.pallas_call(
        body, name="adamw_w_mod", grid=(depth, d // tr),
        out_shape=tuple(jax.ShapeDtypeStruct(w.shape, f32) for _ in range(4)),
        in_specs=[pl.BlockSpec((N_DEV, tr), lambda i, r: (0, r)),
                  pl.BlockSpec((None, N_DEV, cols), lambda i, r: (i, 0, 0)), blk, blk, blk],
        out_specs=(blk, blk, blk, blk),
        compiler_params=_params(("arbitrary", "arbitrary"), 32),
    )(c_all, dmod_all, _in_hbm(w), _in_hbm(m), _in_hbm(v))


def kernel(x, c, w_mod, b_mod, norm_mix_g, norm_ffn_g, lru_w_y, lru_b_y, lru_w_in, lru_b_in, lru_conv_w, lru_conv_b, lru_w_a, lru_b_a, lru_w_x, lru_b_x, lru_lambda, lru_w_out, lru_b_out, pool_w, pool_scale, ffn_w1, ffn_w2, final_norm_g, loss_target, m_w_mod, m_b_mod, m_norm_mix_g, m_norm_ffn_g, m_lru_w_y, m_lru_b_y, m_lru_w_in, m_lru_b_in, m_lru_conv_w, m_lru_conv_b, m_lru_w_a, m_lru_b_a, m_lru_w_x, m_lru_b_x, m_lru_lambda, m_lru_w_out, m_lru_b_out, m_pool_w, m_pool_scale, m_ffn_w1, m_ffn_w2, m_final_norm_g, v_w_mod, v_b_mod, v_norm_mix_g, v_norm_ffn_g, v_lru_w_y, v_lru_b_y, v_lru_w_in, v_lru_b_in, v_lru_conv_w, v_lru_conv_b, v_lru_w_a, v_lru_b_a, v_lru_w_x, v_lru_b_x, v_lru_lambda, v_lru_w_out, v_lru_b_out, v_pool_w, v_pool_scale, v_ffn_w1, v_ffn_w2, v_final_norm_g):
    me = 4 * lax.axis_index("x") + 2 * lax.axis_index("y") + lax.axis_index("c")
    n_lru = lru_w_y.shape[0]
    shard = LRU_WIDTH // N_DEV
    hshard = HEAD_DIM // N_DEV
    xs = x[0]
    target = loss_target[0]

    small_vecs = jnp.concatenate([
        lru_conv_w.reshape(n_lru * 4, shard), lru_b_a.reshape(n_lru, HEADS * hshard),
        lru_b_x.reshape(n_lru, HEADS * hshard), pool_scale, jnp.zeros((2, shard), f32)], axis=0)
    sv_g, c_g = _exchange([small_vecs, c], True, "gather_cond")
    conv_w_full = sv_g[:, 0:8].reshape(N_DEV, n_lru, 4, shard).transpose(1, 2, 0, 3).reshape(n_lru, 4, LRU_WIDTH)
    b_a_full = sv_g[:, 8:10].reshape(N_DEV, n_lru, HEADS, hshard).transpose(1, 2, 0, 3).reshape(n_lru, LRU_WIDTH)
    b_x_full = sv_g[:, 10:12].reshape(N_DEV, n_lru, HEADS, hshard).transpose(1, 2, 0, 3).reshape(n_lru, LRU_WIDTH)
    ps_full = sv_g[:, 12:14].transpose(1, 0, 2).reshape(n_lru, D_MODEL)
    c_all = c_g.reshape(N_DEV, D_MODEL)

    (mod_g,) = _exchange([_mod_part(c_all, w_mod)], True, "gather_mod", pieces=DEPTH)
    mod_row = lax.dynamic_index_in_dim(mod_g, me, axis=2, keepdims=False)
    mod_row = mod_row.transpose(1, 0, 2).reshape(DEPTH, N_MOD * D_MODEL)
    table, token = _mod_table(mod_row, b_mod, norm_mix_g, norm_ffn_g)

    first_pieces = 4
    (first_mix,), token = _send_start("gather_first_start", [[
        jnp.stack([lru_w_y[0], lru_w_in[0], lru_w_out[0]]).astype(bf16).reshape(3 * first_pieces, -1, LRU_WIDTH),
        jnp.stack([lru_w_a[0], lru_w_x[0]]).astype(bf16).reshape(first_pieces, -1, HEAD_DIM)]], True, me,
        pieces=first_pieces, after=token)

    parts = []
    for i in range(DEPTH):
        j = i // 2
        if i > 0 and i % 2 == 0:
            parts.append([(jnp.stack([lru_w_y[j], lru_w_in[j], lru_w_out[j]]) + token[0, 0]).astype(bf16),
                          (jnp.stack([lru_w_a[j], lru_w_x[j]]) + token[0, 0]).astype(bf16)])
        elif i % 2 == 1:
            parts.append([(pool_w[j] + token[0, 0]).astype(bf16)])
        parts.append([(ffn_w1[i] + token[0, 0]).astype(bf16), (ffn_w2[i] + token[0, 0]).astype(bf16)])
    first_got, token = _send_wait("gather_mix_wait_0", first_mix, [a for part in parts for a in part])
    handles, token = _send_start("gather_rest_start", parts, True, me, after=token)
    h_ffn = [handles[0], handles[2], handles[4], handles[6]]
    h_mix = [None, handles[1], handles[3], handles[5]]

    zero_row = jnp.zeros((1, LRU_WIDTH), f32)
    pvecs = [jnp.concatenate([lru_b_y[j:j + 1], lru_b_in[j:j + 1], lru_conv_b[j:j + 1], b_a_full[j:j + 1],
                              b_x_full[j:j + 1], lru_lambda[j:j + 1], lru_b_out[j:j + 1], zero_row,
                              conv_w_full[j], zero_row, zero_row, zero_row, zero_row], axis=0) for j in range(n_lru)]
    ps_rows = [jnp.concatenate([ps_full[j:j + 1], jnp.zeros((7, D_MODEL), f32)], axis=0) for j in range(n_lru)]

    saved = []
    ffn_w, mix_w = [], []
    h = xs
    for i in range(DEPTH):
        j = i // 2
        got = first_got if i == 0 else _send_wait(f"gather_mix_wait_{i}", h_mix[i], h)[0]
        if i % 2 == 0:
            got = [got[0].reshape(N_DEV, 3, shard, LRU_WIDTH), got[1].reshape(N_DEV, 2, HEADS, hshard, HEAD_DIM)]
            mix_w.append((got[0].transpose(1, 0, 2, 3).reshape(3, LRU_WIDTH, LRU_WIDTH),
                          got[1].transpose(1, 2, 0, 3, 4).reshape(2, HEADS, HEAD_DIM, HEAD_DIM)))
            h_mid, *lru_saved = _lru_fwd(h, table[i] + token[0, 0], mix_w[i][0], mix_w[i][1], pvecs[j], i)
            mix_saved = (h, tuple(lru_saved))
        else:
            mix_w.append((got[0].transpose(1, 0, 2, 3).reshape(HEADS, HEAD_DIM, HEAD_DIM),))
            h_mid, y_mix = _pool_fwd(h, table[i], mix_w[i][0], ps_rows[j], i)
            mix_saved = (h, y_mix)
        ffn_w.append(_send_wait(f"gather_ffn_wait_{i}", h_ffn[i], h_mid)[0])
        h_out, u, y_ffn, hb = _ffn_fwd(h_mid, table[i], ffn_w[i][0], ffn_w[i][1], i)
        saved.append((mix_saved, (h_mid, u, y_ffn, hb)))
        h = h_out
    fin_rows = jnp.concatenate([final_norm_g[None, :], jnp.zeros((7, D_MODEL), f32)], axis=0)
    dx, sm_fin = _final(h, target, fin_rows)

    sm_ffn, sm_mix = [None] * DEPTH, [None] * DEPTH
    x_ffn, x_mix = [None] * DEPTH, [None] * DEPTH
    token = jnp.zeros((8, 128), f32)
    last_mix = None
    for i in reversed(range(DEPTH)):
        j = i // 2
        mix_saved, (h_mid, u, y_ffn, hb) = saved[i]
        dx, da, dyb, sm_ffn[i] = _ffn_bwd_act(h_mid, dx, u, y_ffn, table[i] + token[0, 0], ffn_w[i][0], ffn_w[i][1], i)
        ffn_grads = [_ffn_bwd_w1(hb, da, i), _ffn_bwd_w2(u, dyb, i)]
        if last_mix is None:
            (x_ffn[i],), token = _send_start(f"grads_start_{i}", [ffn_grads], False, me)
        else:
            (x_mix[i + 1], x_ffn[i]), token = _send_start(f"grads_start_{i}", [last_mix, ffn_grads], False, me)
        if i % 2 == 0:
            h_in, lru_saved = mix_saved
            dx, dbig, dsmall, sm_mix[i] = _lru_bwd(
                h_in, dx, lru_saved, table[i] + token[0, 0], mix_w[i][0], mix_w[i][1], pvecs[j], i)
            last_mix = [dbig, dsmall]
        else:
            h_in, y_mix = mix_saved
            dx, dpool, sm = _pool_bwd(h_in, dx, y_mix, table[i] + token[0, 0], mix_w[i][0], ps_rows[j], i)
            sm_mix[i] = jnp.concatenate([sm, jnp.zeros((8, D_MODEL), f32)], axis=0)
            last_mix = [dpool]
    grad_x = dx[None]

    pack = _small_pack(sm_ffn, sm_mix, sm_fin, table + token[0, 0], norm_mix_g, norm_ffn_g, lru_lambda)
    (pack_g,) = _exchange([pack], True, "gather_small_grads", pieces=4)
    tot, token = _small_sum(pack_g)
    loss = tot[K_LOSS, 0]
    (x_mix[0],), _ = _send_start("grads_last_start", [[t + token[0, 0].astype(bf16) for t in last_mix]], False, me)
    cols = w_mod.shape[2]
    dmod_all = lax.dynamic_slice_in_dim(pack_g[:, K_MOD:K_MOD + DEPTH * N_MOD].reshape(N_DEV, DEPTH, N_MOD * D_MODEL),
                                        me * cols, cols, axis=2).transpose(1, 0, 2)
    results = {"w_mod": _adamw_w_mod(c_all, dmod_all, w_mod, m_w_mod, v_w_mod)}

    after = results["w_mod"][1]
    l_ffn = [_send_wait(f"grads_ffn_wait_{i}", x_ffn[i], after)[0] for i in reversed(range(DEPTH))][::-1]

    def reduce_update(name, landings, kind, w, m, v):
        rows = w.size // w.shape[-1]
        two_d = (rows, w.shape[-1])
        lands = [t.reshape(N_DEV, -1, rows // len(landings), w.shape[-1]) for t in landings]
        outs = _adamw_reduce(name, lands, kind, w.reshape(two_d), m.reshape(two_d), v.reshape(two_d))
        return tuple(t.reshape(w.shape) for t in outs)

    results["ffn_w1"] = reduce_update("ffn_w1", [t[0] for t in l_ffn], 0, ffn_w1, m_ffn_w1, v_ffn_w1)
    results["ffn_w2"] = reduce_update("ffn_w2", [t[1] for t in l_ffn], 0, ffn_w2, m_ffn_w2, v_ffn_w2)
    after = results["ffn_w2"][1]
    l_mix = [_send_wait(f"grads_mix_wait_{i}", x_mix[i], after)[0] for i in reversed(range(DEPTH))][::-1]
    l_lru_big = [l_mix[i][0] for i in range(0, DEPTH, 2)]
    l_lru_small = [l_mix[i][1] for i in range(0, DEPTH, 2)]
    l_pool = [l_mix[i][0] for i in range(1, DEPTH, 2)]
    results["lru_w_y"] = reduce_update("lru_w_y", l_lru_big, 0, lru_w_y, m_lru_w_y, v_lru_w_y)
    results["lru_w_in"] = reduce_update("lru_w_in", l_lru_big, 1, lru_w_in, m_lru_w_in, v_lru_w_in)
    results["lru_w_out"] = reduce_update("lru_w_out", l_lru_big, 2, lru_w_out, m_lru_w_out, v_lru_w_out)
    results["lru_w_a"] = reduce_update("lru_w_a", l_lru_small, 0, lru_w_a, m_lru_w_a, v_lru_w_a)
    results["lru_w_x"] = reduce_update("lru_w_x", l_lru_small, 1, lru_w_x, m_lru_w_x, v_lru_w_x)
    results["pool_w"] = reduce_update("pool_w", l_pool, 0, pool_w, m_pool_w, v_pool_w)

    def my_cols(full, width):
        return lax.dynamic_slice_in_dim(full, me * width, width, axis=full.ndim - 1)

    lru_rows = tot[K_LRUB:K_LRUB + 5 * n_lru].reshape(n_lru, 5, LRU_WIDTH)
    small_grads = {
        "b_mod": tot[K_MOD:K_MOD + DEPTH * N_MOD].reshape(DEPTH, N_MOD * D_MODEL),
        "norm_mix_g": tot[K_NMIX:K_NMIX + DEPTH],
        "norm_ffn_g": tot[K_NFFN:K_NFFN + DEPTH],
        "lru_b_y": lru_rows[:, 0], "lru_b_in": lru_rows[:, 1], "lru_conv_b": lru_rows[:, 2],
        "lru_lambda": lru_rows[:, 3], "lru_b_out": lru_rows[:, 4],
        "lru_conv_w": my_cols(tot[K_CONVW:K_CONVW + 4 * n_lru].reshape(n_lru, 4, LRU_WIDTH), shard),
        "lru_b_a": my_cols(tot[K_BA:K_BA + n_lru].reshape(n_lru, HEADS, HEAD_DIM), hshard),
        "lru_b_x": my_cols(tot[K_BX:K_BX + n_lru].reshape(n_lru, HEADS, HEAD_DIM), hshard),
        "pool_scale": my_cols(tot[K_PS:K_PS + n_lru], shard),
        "final_norm_g": tot[K_FIN],
    }
    given = dict(b_mod=(b_mod, m_b_mod, v_b_mod), norm_mix_g=(norm_mix_g, m_norm_mix_g, v_norm_mix_g),
                 norm_ffn_g=(norm_ffn_g, m_norm_ffn_g, v_norm_ffn_g), lru_b_y=(lru_b_y, m_lru_b_y, v_lru_b_y),
                 lru_b_in=(lru_b_in, m_lru_b_in, v_lru_b_in), lru_conv_w=(lru_conv_w, m_lru_conv_w, v_lru_conv_w),
                 lru_conv_b=(lru_conv_b, m_lru_conv_b, v_lru_conv_b), lru_b_a=(lru_b_a, m_lru_b_a, v_lru_b_a),
                 lru_b_x=(lru_b_x, m_lru_b_x, v_lru_b_x), lru_lambda=(lru_lambda, m_lru_lambda, v_lru_lambda),
                 lru_b_out=(lru_b_out, m_lru_b_out, v_lru_b_out), pool_scale=(pool_scale, m_pool_scale, v_pool_scale),
                 final_norm_g=(final_norm_g, m_final_norm_g, v_final_norm_g))
    for name, g in small_grads.items():
        results[name] = (g,) + _adamw_small(name, g, *given[name])

    order = ["w_mod", "b_mod", "norm_mix_g", "norm_ffn_g", "lru_w_y", "lru_b_y", "lru_w_in", "lru_b_in", "lru_conv_w",
             "lru_conv_b", "lru_w_a", "lru_b_a", "lru_w_x", "lru_b_x", "lru_lambda", "lru_w_out", "lru_b_out", "pool_w",
             "pool_scale", "ffn_w1", "ffn_w2", "final_norm_g"]
    return (loss, grad_x, *[results[n][0] for n in order], *[results[n][1] for n in order],
            *[results[n][2] for n in order], *[results[n][3] for n in order])
```

```python
import math

import jax
import jax.numpy as jnp
from jax import lax
from jax.experimental import pallas as pl
from jax.experimental.pallas import tpu as pltpu

f32, bf16 = jnp.float32, jnp.bfloat16

D_MODEL = 1024
LRU_WIDTH = 1024
HEADS = 4
HEAD_DIM = 256
D_FF = 4096
DEPTH = 4
N_MOD = 6
N_DEV = 8
FF_CHUNK = D_FF // N_DEV
POOL_WINDOWS = (2, 4, 8, 16)
POOL_HALO = 16
EPS = 1e-6
LRU_C = 8.0

ADAM_LR = 0.001
ADAM_B1 = 0.9
ADAM_B2 = 0.999
ADAM_EPS = 1e-08
ADAM_WD = 0.01
ADAM_STEP = 10

SUBLANES = 8
BF16_ROWS = 16

R_SH_M, R_SC_M, R_GT_M, R_SH_F, R_SC_F, R_GT_F, R_GS_M, R_GS_F = range(8)
P_BY, P_BIN, P_CONVB, P_BA, P_BX, P_LAM, P_BOUT, P_CW0 = 0, 1, 2, 3, 4, 5, 6, 8
G_SH, G_GS, G_GT, G_BY, G_BIN, G_CONVB, G_BA, G_BX, G_LS, G_BOUT, G_CW0 = 0, 1, 2, 3, 4, 5, 6, 7, 8, 9, 10
K_MOD, K_NMIX, K_NFFN, K_LRUB, K_CONVW, K_BA, K_BX, K_PS, K_FIN, K_LOSS, K_ROWS = 0, 24, 28, 32, 42, 50, 52, 54, 56, 57, 64


def _params(semantics=None, vmem_mb=48):
    return pltpu.CompilerParams(dimension_semantics=semantics, vmem_limit_bytes=vmem_mb * 1024 * 1024)


def _mm(a, b):
    return jnp.dot(a, b, preferred_element_type=f32)


def _mm_nt(a, b):
    return lax.dot_general(a, b, (((1,), (1,)), ((), ())), preferred_element_type=f32)


def _mm_tn(a, b):
    return lax.dot_general(a, b, (((0,), (0,)), ((), ())), preferred_element_type=f32)


def _rms(x):
    r = lax.rsqrt(jnp.mean(x * x, axis=-1, keepdims=True) + EPS)
    return x * r, r


def _norm_bwd(dh, n, r, gs):
    dn = dh * gs
    return r * (dn - n * jnp.mean(dn * n, axis=-1, keepdims=True))


def _colsum(v):
    return jnp.sum(v, axis=0, keepdims=True)


def _sigmoid(v):
    return 0.5 * jnp.tanh(0.5 * v) + 0.5


def _log_sigmoid(v):
    return jnp.minimum(v, 0.0) - jnp.log1p(jnp.exp(-jnp.abs(v)))


_GELU_C = 0.7978845608028654
_GELU_A = 0.044715


def _gelu_and_grad(v):
    v2 = v * v
    t = jnp.tanh(_GELU_C * v * (1.0 + _GELU_A * v2))
    p = 0.5 + 0.5 * t
    return v * p, p + (0.5 * v) * (1.0 - t * t) * (_GELU_C + (3.0 * _GELU_A * _GELU_C) * v2)


def _rows_before(halo, v, shifts):
    hr = halo.shape[0]
    ext = jnp.concatenate([halo, v], axis=0)
    return [pltpu.roll(ext, k, 0)[hr:] for k in shifts]


def _rows_after(v, halo, shifts):
    n = v.shape[0]
    ext = jnp.concatenate([v, halo], axis=0)
    return [pltpu.roll(ext, ext.shape[0] - k, 0)[:n] for k in shifts]


def _shift_matrix(n, halo_rows, shifts):
    rows = lax.broadcasted_iota(jnp.int32, (n, n + halo_rows), 0)
    cols = lax.broadcasted_iota(jnp.int32, (n, n + halo_rows), 1)
    return jnp.concatenate([(cols == rows + halo_rows - k).astype(bf16) for k in shifts], axis=0)


def _shifted_rows(sel, halo, v):
    n = v.shape[0]
    out = _mm(sel, jnp.concatenate([halo, v], axis=0))
    return [out[j * n:(j + 1) * n] for j in range(sel.shape[0] // n)]


def _block_diag(v, w_ref, kind):
    return jnp.concatenate(
        [_mm(v[:, h * HEAD_DIM:(h + 1) * HEAD_DIM], w_ref[kind, h]) for h in range(HEADS)], axis=1)


def _block_diag_t(v, w_ref, kind):
    return jnp.concatenate(
        [_mm_nt(v[:, h * HEAD_DIM:(h + 1) * HEAD_DIM], w_ref[kind, h]) for h in range(HEADS)], axis=1)


def _exchange(arrays, gather, name, pieces=1):
    n = len(arrays)
    peers = N_DEV - 1

    def body(*refs):
        ins, outs = refs[:n], refs[n:2 * n]
        send_sems, recv_sems, local_sems = refs[2 * n:]
        x, y, c = lax.axis_index("x"), lax.axis_index("y"), lax.axis_index("c")
        me = 4 * x + 2 * y + c
        local = []
        for k in range(n):
            cp = pltpu.make_async_copy(ins[k] if gather else ins[k].at[me], outs[k].at[me], local_sems.at[k])
            cp.start()
            local.append(cp)
        remote = _peer_copies(ins, outs, send_sems, recv_sems, gather, pieces)
        for cp in remote:
            cp.start()
        for cp in remote:
            cp.wait()
        for cp in local:
            cp.wait()

    out_shape = tuple(
        jax.ShapeDtypeStruct(((N_DEV,) + a.shape) if gather else a.shape, a.dtype) for a in arrays)
    outs = pl.pallas_call(
        body, name=name, out_shape=out_shape,
        in_specs=[pl.BlockSpec(memory_space=pl.ANY)] * n,
        out_specs=tuple(pl.BlockSpec(memory_space=pl.ANY) for _ in range(n)),
        scratch_shapes=[pltpu.SemaphoreType.DMA((n * pieces * peers,)), pltpu.SemaphoreType.DMA((n * pieces * peers,)),
                        pltpu.SemaphoreType.DMA((n,))],
        compiler_params=pltpu.CompilerParams(has_side_effects=True),
    )(*arrays)
    return list(outs)


_HBM = pl.BlockSpec(memory_space=pltpu.HBM)
_SEM = pl.BlockSpec(memory_space=pltpu.SEMAPHORE)
_DATAFLOW = pltpu.SideEffectType.DATAFLOW_SIDE_EFFECTING


def _peer_copies(src_refs, land_refs, send_sems, recv_sems, gather, pieces=1):
    x, y, c = lax.axis_index("x"), lax.axis_index("y"), lax.axis_index("c")
    me = 4 * x + 2 * y + c
    peers = N_DEV - 1
    copies = []
    for p in range(1, N_DEV):
        px = 1 - x if p & 4 else x
        py = 1 - y if p & 2 else y
        pc = 1 - c if p & 1 else c
        for k in range(len(src_refs)):
            block = src_refs[k] if gather else src_refs[k].at[4 * px + 2 * py + pc]
            dst = land_refs[k].at[me]
            rows = block.shape[0] // pieces
            for r in range(pieces):
                part = pl.ds(r * rows, rows)
                sem = (k * pieces + r) * peers + p - 1
                copies.append(pltpu.make_async_remote_copy(
                    src_ref=block.at[part] if pieces > 1 else block, dst_ref=dst.at[part] if pieces > 1 else dst,
                    send_sem=send_sems.at[sem], recv_sem=recv_sems.at[sem],
                    device_id=(px, py, pc), device_id_type=pl.DeviceIdType.MESH))
    return copies


def _landing(srcs, gather, me):
    out = []
    for a in srcs:
        own = a if gather else lax.dynamic_index_in_dim(a, me, 0, keepdims=False)
        out.append(lax.dynamic_update_index_in_dim(lax.empty((N_DEV,) + own.shape, own.dtype), own, me, 0))
    return out


def _send_start(name, groups, gather, me, pieces=1, after=None):
    sizes = [len(g) for g in groups]
    srcs = [a for g in groups for a in g]
    n = len(srcs)
    lands = _landing(srcs, gather, me)
    ng = len(groups)
    first = [sum(sizes[:g]) for g in range(ng)]
    extra = [] if after is None else [after]

    def body(*refs):
        src_refs, land_refs = refs[:n], refs[n:2 * n]
        sems, token = refs[2 * n + len(extra):2 * n + len(extra) + 2 * ng], refs[-1]
        for g in range(ng):
            part = slice(first[g], first[g] + sizes[g])
            for cp in _peer_copies(src_refs[part], land_refs[part], sems[2 * g], sems[2 * g + 1], gather, pieces):
                cp.start()
        token[...] = jnp.zeros_like(token)

    sem_shapes = [pltpu.SemaphoreType.DMA((sizes[g // 2] * pieces * (N_DEV - 1),)) for g in range(2 * ng)]
    outs = pl.pallas_call(
        body, name=name,
        out_shape=(*sem_shapes, *[pltpu.HBM(a.shape, a.dtype) for a in (*srcs, *lands)], jax.ShapeDtypeStruct((8, 128), f32)),
        in_specs=[_HBM] * (2 * n) + [pl.BlockSpec(memory_space=pl.ANY)] * len(extra),
        out_specs=(*[_SEM] * (2 * ng), *[_HBM] * (2 * n), pl.BlockSpec(memory_space=pltpu.VMEM)),
        input_output_aliases={k: 2 * ng + k for k in range(2 * n)},
        compiler_params=pltpu.CompilerParams(has_side_effects=_DATAFLOW),
    )(*[pltpu.with_memory_space_constraint(a, pltpu.HBM) for a in (*srcs, *lands)], *extra)
    srcs_thru, lands_thru = outs[2 * ng:2 * ng + n], outs[2 * ng + n:2 * ng + 2 * n]
    handles = [(outs[2 * g], outs[2 * g + 1], list(srcs_thru[first[g]:first[g] + sizes[g]]),
                list(lands_thru[first[g]:first[g] + sizes[g]]), gather, pieces) for g in range(ng)]
    return handles, outs[-1]


def _send_wait(name, handle, after):
    send_sems, recv_sems, srcs, lands, gather, pieces = handle
    n = len(srcs)
    after = list(after) if isinstance(after, (list, tuple)) else [after]

    def body(*refs):
        src_refs, land_refs = refs[:n], refs[n:2 * n]
        for cp in _peer_copies(src_refs, land_refs, refs[2 * n], refs[2 * n + 1], gather, pieces):
            cp.wait_send()
            cp.wait_recv()
        refs[-1][...] = jnp.zeros_like(refs[-1])

    outs = pl.pallas_call(
        body, name=name,
        out_shape=(*[pltpu.HBM(a.shape, a.dtype) for a in (*srcs, *lands)], jax.ShapeDtypeStruct((8, 128), f32)),
        in_specs=[_HBM] * (2 * n) + [_SEM, _SEM] + [pl.BlockSpec(memory_space=pl.ANY)] * len(after),
        out_specs=(*[_HBM] * (2 * n), pl.BlockSpec(memory_space=pltpu.VMEM)),
        input_output_aliases={k: k for k in range(2 * n)},
        compiler_params=pltpu.CompilerParams(has_side_effects=_DATAFLOW),
    )(*srcs, *lands, send_sems, recv_sems, *after)
    return list(outs[n:2 * n]), outs[-1]


def _mod_part(c_all, w_mod):
    depth, d, cols = w_mod.shape

    def body(c_ref, w_ref, o_ref):
        cv = c_ref[...]
        cond = cv * _sigmoid(cv)
        o_ref[...] = jnp.dot(cond, w_ref[...], preferred_element_type=f32, precision=lax.Precision.HIGHEST)

    return pl.pallas_call(
        body, name="mod_part", grid=(depth,),
        out_shape=jax.ShapeDtypeStruct((depth, N_DEV, cols), f32),
        in_specs=[pl.BlockSpec((N_DEV, d), lambda i: (0, 0)), pl.BlockSpec((None, d, cols), lambda i: (i, 0, 0))],
        out_specs=pl.BlockSpec((None, N_DEV, cols), lambda i: (i, 0, 0)),
        compiler_params=_params(("arbitrary",), 32),
    )(c_all, w_mod)


def _mod_table(mod_row, b_mod, g_mix, g_ffn):
    def body(m_ref, b_ref, gm_ref, gf_ref, o_ref, token_ref):
        for i in range(DEPTH):
            for k in range(N_MOD):
                o_ref[i, k:k + 1, :] = m_ref[i:i + 1, k * D_MODEL:(k + 1) * D_MODEL] + b_ref[i:i + 1, k * D_MODEL:(k + 1) * D_MODEL]
            o_ref[i, R_GS_M:R_GS_M + 1, :] = gm_ref[i:i + 1, :] * (1.0 + o_ref[i, R_SC_M:R_SC_M + 1, :])
            o_ref[i, R_GS_F:R_GS_F + 1, :] = gf_ref[i:i + 1, :] * (1.0 + o_ref[i, R_SC_F:R_SC_F + 1, :])
        token_ref[...] = jnp.zeros_like(token_ref)

    return pl.pallas_call(
        body, name="mod_table",
        out_shape=(jax.ShapeDtypeStruct((DEPTH, 8, D_MODEL), f32), jax.ShapeDtypeStruct((8, 128), f32)))(
        mod_row, b_mod, g_mix, g_ffn)


def _ffn_tile(s):
    return min(512, s)


def _layer_weights(shape):
    return pl.BlockSpec((N_DEV,) + shape, lambda i: (0, 0, 0))


def _ffn_fwd(x, vec, w1g, w2g, layer):
    s = x.shape[0]
    ts = _ffn_tile(s)

    def body(x_ref, vec_ref, w1_ref, w2_ref, xo_ref, u_ref, y_ref, hb_ref):
        xv = x_ref[...]
        n, _ = _rms(xv)
        hb = (n * vec_ref[R_GS_F:R_GS_F + 1, :] + vec_ref[R_SH_F:R_SH_F + 1, :]).astype(bf16)
        hb_ref[...] = hb
        yv = jnp.zeros((ts, D_MODEL), f32)
        for f in range(N_DEV):
            u = jnp.maximum(_mm(hb, w1_ref[f]), 0.0)
            u_ref[:, f * FF_CHUNK:(f + 1) * FF_CHUNK] = u.astype(bf16)
            yv = yv + _mm((u * u).astype(bf16), w2_ref[f])
        y_ref[...] = yv.astype(bf16)
        xo_ref[...] = xv + vec_ref[R_GT_F:R_GT_F + 1, :] * yv

    row = pl.BlockSpec((ts, D_MODEL), lambda i: (i, 0))
    return pl.pallas_call(
        body, name=f"ffn_fwd_{layer}", grid=(s // ts,),
        out_shape=(jax.ShapeDtypeStruct((s, D_MODEL), f32), jax.ShapeDtypeStruct((s, D_FF), bf16),
                   jax.ShapeDtypeStruct((s, D_MODEL), bf16), jax.ShapeDtypeStruct((s, D_MODEL), bf16)),
        in_specs=[row, pl.BlockSpec((8, D_MODEL), lambda i: (0, 0)),
                  _layer_weights((D_MODEL, FF_CHUNK)), _layer_weights((FF_CHUNK, D_MODEL))],
        out_specs=(row, pl.BlockSpec((ts, D_FF), lambda i: (i, 0)), row, row),
        compiler_params=_params(("arbitrary",), 56),
    )(x, vec, w1g, w2g)


def _ffn_bwd_act(x, dx, u, y, vec, w1g, w2g, layer):
    s = x.shape[0]
    ts = _ffn_tile(s)

    def body(x_ref, dx_ref, u_ref, y_ref, vec_ref, w1_ref, w2_ref, dxo_ref, da_ref, dyb_ref, sm_ref):
        @pl.when(pl.program_id(0) == 0)
        def _():
            sm_ref[...] = jnp.zeros_like(sm_ref)

        dxv = dx_ref[...]
        dyb = (dxv * vec_ref[R_GT_F:R_GT_F + 1, :]).astype(bf16)
        dyb_ref[...] = dyb
        sm_ref[G_GT:G_GT + 1, :] += _colsum(dxv * y_ref[...].astype(f32))
        dh = jnp.zeros((ts, D_MODEL), f32)
        for f in range(N_DEV):
            cols = slice(f * FF_CHUNK, (f + 1) * FF_CHUNK)
            dz = _mm_nt(dyb, w2_ref[f])
            dab = (dz * (2.0 * u_ref[:, cols].astype(f32))).astype(bf16)
            da_ref[:, cols] = dab
            dh = dh + _mm_nt(dab, w1_ref[f])
        n, r = _rms(x_ref[...])
        sm_ref[G_SH:G_SH + 1, :] += _colsum(dh)
        sm_ref[G_GS:G_GS + 1, :] += _colsum(dh * n)
        dxo_ref[...] = dxv + _norm_bwd(dh, n, r, vec_ref[R_GS_F:R_GS_F + 1, :])

    row = pl.BlockSpec((ts, D_MODEL), lambda i: (i, 0))
    wide = pl.BlockSpec((ts, D_FF), lambda i: (i, 0))
    return pl.pallas_call(
        body, name=f"ffn_bwd_act_{layer}", grid=(s // ts,),
        out_shape=(jax.ShapeDtypeStruct((s, D_MODEL), f32), jax.ShapeDtypeStruct((s, D_FF), bf16),
                   jax.ShapeDtypeStruct((s, D_MODEL), bf16), jax.ShapeDtypeStruct((8, D_MODEL), f32)),
        in_specs=[row, row, wide, row, pl.BlockSpec((8, D_MODEL), lambda i: (0, 0)),
                  _layer_weights((D_MODEL, FF_CHUNK)), _layer_weights((FF_CHUNK, D_MODEL))],
        out_specs=(row, wide, row, pl.BlockSpec((8, D_MODEL), lambda i: (0, 0))),
        compiler_params=_params(("arbitrary",), 58),
    )(x, dx, u, y, vec, w1g, w2g)


def _ffn_bwd_w1(hb, da, layer):
    s = hb.shape[0]
    ts = _ffn_tile(s)
    nt = s // ts

    def body(hb_ref, da_ref, dw_ref, acc_ref):
        i = pl.program_id(0)

        @pl.when(i == 0)
        def _():
            acc_ref[...] = jnp.zeros_like(acc_ref)

        hb = hb_ref[...]
        for f in range(N_DEV):
            acc_ref[f] += _mm_tn(hb, da_ref[:, f * FF_CHUNK:(f + 1) * FF_CHUNK])

        @pl.when(i == nt - 1)
        def _():
            dw_ref[...] = acc_ref[...].astype(bf16)

    return pl.pallas_call(
        body, name=f"ffn_bwd_w1_{layer}", grid=(nt,),
        out_shape=jax.ShapeDtypeStruct((N_DEV, D_MODEL, FF_CHUNK), bf16),
        in_specs=[pl.BlockSpec((ts, D_MODEL), lambda i: (i, 0)), pl.BlockSpec((ts, D_FF), lambda i: (i, 0))],
        out_specs=pl.BlockSpec((N_DEV, D_MODEL, FF_CHUNK), lambda i: (0, 0, 0)),
        scratch_shapes=[pltpu.VMEM((N_DEV, D_MODEL, FF_CHUNK), f32)],
        compiler_params=_params(("arbitrary",), 56),
    )(hb, da)


def _ffn_bwd_w2(u, dyb, layer):
    s = u.shape[0]
    ts = _ffn_tile(s)
    nt = s // ts

    def body(u_ref, dyb_ref, dw_ref, acc_ref):
        i = pl.program_id(0)

        @pl.when(i == 0)
        def _():
            acc_ref[...] = jnp.zeros_like(acc_ref)

        dyb = dyb_ref[...]
        for f in range(N_DEV):
            uv = u_ref[:, f * FF_CHUNK:(f + 1) * FF_CHUNK].astype(f32)
            acc_ref[f] += _mm_tn((uv * uv).astype(bf16), dyb)

        @pl.when(i == nt - 1)
        def _():
            dw_ref[...] = acc_ref[...].astype(bf16)

    return pl.pallas_call(
        body, name=f"ffn_bwd_w2_{layer}", grid=(nt,),
        out_shape=jax.ShapeDtypeStruct((N_DEV, FF_CHUNK, D_MODEL), bf16),
        in_specs=[pl.BlockSpec((ts, D_FF), lambda i: (i, 0)), pl.BlockSpec((ts, D_MODEL), lambda i: (i, 0))],
        out_specs=pl.BlockSpec((N_DEV, FF_CHUNK, D_MODEL), lambda i: (0, 0, 0)),
        scratch_shapes=[pltpu.VMEM((N_DEV, FF_CHUNK, D_MODEL), f32)],
        compiler_params=_params(("arbitrary",), 56),
    )(u, dyb)


def _lru_gates(xc, wsm_ref, pv_ref):
    xcb = xc.astype(bf16)
    gr = _sigmoid(_block_diag(xcb, wsm_ref, 0) + pv_ref[P_BA:P_BA + 1, :])
    gi = _sigmoid(_block_diag(xcb, wsm_ref, 1) + pv_ref[P_BX:P_BX + 1, :])
    log_a = (LRU_C * _log_sigmoid(pv_ref[P_LAM:P_LAM + 1, :])) * gr
    t = jnp.tanh(log_a)
    return gr, gi, jnp.exp(log_a), jnp.sqrt((-2.0 * t) / (1.0 - t))


def _conv(xr, taps_before, pv_ref):
    xc = xr * pv_ref[P_CW0 + 3:P_CW0 + 4, :] + pv_ref[P_CONVB:P_CONVB + 1, :]
    for k, v in zip((2, 1, 0), taps_before):
        xc = xc + v * pv_ref[P_CW0 + k:P_CW0 + k + 1, :]
    return xc


LRU_FWD_SUB, LRU_FWD_SUBS = 128, 2
LRU_BWD_SUB, LRU_BWD_SUBS = 256, 1


def _scan_rows(a, u, carry, reverse):
    groups = a.shape[0] // SUBLANES
    row = lax.broadcasted_iota(jnp.int32, (SUBLANES, a.shape[1]), 0)
    outs = [None] * groups
    for j in range(groups):
        g = groups - 1 - j if reverse else j
        av, uv = a[g * SUBLANES:(g + 1) * SUBLANES], u[g * SUBLANES:(g + 1) * SUBLANES]
        for k in (1, 2, 4):
            if reverse:
                valid, shift = row < SUBLANES - k, SUBLANES - k
            else:
                valid, shift = row >= k, k
            a_s = jnp.where(valid, pltpu.roll(av, shift, 0), 1.0)
            u_s = jnp.where(valid, pltpu.roll(uv, shift, 0), 0.0)
            uv = uv + av * u_s
            av = av * a_s
        h = uv + av * carry
        outs[g] = h
        carry = h[0:1, :] if reverse else h[SUBLANES - 1:SUBLANES, :]
    return jnp.concatenate(outs, axis=0), carry


def _lru_fwd(x, vec, wbig, wsm, pvec, layer):
    s = x.shape[0]
    sub = min(LRU_FWD_SUB, s)
    ts = min(sub * LRU_FWD_SUBS, s)
    nsub = ts // sub
    w = LRU_WIDTH

    def body(x_ref, vec_ref, wb_ref, wsm_ref, pv_ref, xo_ref, xr_ref, hs_ref, a_ref, mult_ref, gr_ref, gi_ref,
             gel_ref, geld_ref, y_ref, tail_ref, carry_ref):
        @pl.when(pl.program_id(0) == 0)
        def _():
            tail_ref[...] = jnp.zeros_like(tail_ref)
            carry_ref[...] = jnp.zeros_like(carry_ref)

        sel = _shift_matrix(sub, BF16_ROWS, (1, 2, 3))
        for k in range(nsub):
            rows = slice(k * sub, (k + 1) * sub)
            xv = x_ref[rows, :]
            n, _ = _rms(xv)
            hb = (n * vec_ref[R_GS_M:R_GS_M + 1, :] + vec_ref[R_SH_M:R_SH_M + 1, :]).astype(bf16)
            gelu_v, gelu_d = _gelu_and_grad(_mm(hb, wb_ref[0]) + pv_ref[P_BY:P_BY + 1, :])
            gel_ref[rows, :] = gelu_v.astype(bf16)
            geld_ref[rows, :] = gelu_d.astype(bf16)
            xrb = (_mm(hb, wb_ref[1]) + pv_ref[P_BIN:P_BIN + 1, :]).astype(bf16)
            xr_ref[rows, :] = xrb
            xc = _conv(xrb.astype(f32), _shifted_rows(sel, tail_ref[...], xrb), pv_ref)
            tail_ref[...] = xrb[sub - BF16_ROWS:, :]
            gr, gi, a, mult = _lru_gates(xc, wsm_ref, pv_ref)
            gr_ref[rows, :] = gr.astype(bf16)
            gi_ref[rows, :] = gi.astype(bf16)
            a_ref[rows, :] = a
            mult_ref[rows, :] = mult
            hs, carry = _scan_rows(a, mult * (gi * xc), carry_ref[0:1, :], reverse=False)
            carry_ref[0:1, :] = carry
            hs_ref[rows, :] = hs
            yv = _mm((hs * gelu_v).astype(bf16), wb_ref[2]) + pv_ref[P_BOUT:P_BOUT + 1, :]
            y_ref[rows, :] = yv.astype(bf16)
            xo_ref[rows, :] = xv + vec_ref[R_GT_M:R_GT_M + 1, :] * yv

    row = pl.BlockSpec((ts, D_MODEL), lambda i: (i, 0))
    roww = pl.BlockSpec((ts, w), lambda i: (i, 0))
    wide = lambda dt: jax.ShapeDtypeStruct((s, w), dt)
    return pl.pallas_call(
        body, name=f"lru_fwd_{layer}", grid=(s // ts,),
        out_shape=(jax.ShapeDtypeStruct((s, D_MODEL), f32), wide(bf16), wide(f32), wide(f32), wide(f32),
                   wide(bf16), wide(bf16), wide(bf16), wide(bf16), jax.ShapeDtypeStruct((s, D_MODEL), bf16)),
        in_specs=[row, pl.BlockSpec((8, D_MODEL), lambda i: (0, 0)),
                  pl.BlockSpec((3, w, w), lambda i: (0, 0, 0)),
                  pl.BlockSpec((2, HEADS, HEAD_DIM, HEAD_DIM), lambda i: (0, 0, 0, 0)),
                  pl.BlockSpec((16, w), lambda i: (0, 0))],
        out_specs=(row, roww, roww, roww, roww, roww, roww, roww, roww, row),
        scratch_shapes=[pltpu.VMEM((BF16_ROWS, w), bf16), pltpu.VMEM((SUBLANES, w), f32)],
        compiler_params=_params(("arbitrary",)),
    )(x, vec, wbig, wsm, pvec)


def _lru_bwd(x, dx, saved, vec, wbig, wsm, pvec, layer):
    xr, hs, a_all, mult_all, gr_all, gi_all, gel_all, geld_all, y = saved
    s = x.shape[0]
    sub = min(LRU_BWD_SUB, s)
    ts = min(sub * LRU_BWD_SUBS, s)
    nsub = ts // sub
    nt = s // ts
    w = LRU_WIDTH
    shard = w // N_DEV
    hshard = HEAD_DIM // N_DEV

    def body(x_ref, dx_ref, xr_ref, xrh_ref, hs_ref, hsh_ref, a_ref, mult_ref, gr_ref, gi_ref, gel_ref, geld_ref,
             y_ref, vec_ref, wb_ref, wsm_ref, pv_ref,
             dxo_ref, dwb_ref, dwsm_ref, sm_ref, accb_ref, accs_ref, eps_ref, dxc8_ref,
             hb_scr, dgb_scr, dxrb_scr, mb_scr, dyb_scr, xcb_scr, drab_scr, drxb_scr):
        i = pl.program_id(0)
        first_tile = i == nt - 1

        @pl.when(i == 0)
        def _():
            accb_ref[...] = jnp.zeros_like(accb_ref)
            accs_ref[...] = jnp.zeros_like(accs_ref)
            sm_ref[...] = jnp.zeros_like(sm_ref)
            eps_ref[...] = jnp.zeros_like(eps_ref)
            dxc8_ref[...] = jnp.zeros_like(dxc8_ref)

        gs = vec_ref[R_GS_M:R_GS_M + 1, :]
        c_ls = LRU_C * _log_sigmoid(pv_ref[P_LAM:P_LAM + 1, :])
        for k in reversed(range(nsub)):
            rows = slice(k * sub, (k + 1) * sub)
            xv = x_ref[rows, :]
            dxv = dx_ref[rows, :]
            n, r = _rms(xv)
            hb_scr[rows, :] = (n * gs + vec_ref[R_SH_M:R_SH_M + 1, :]).astype(bf16)
            xrv = xr_ref[rows, :].astype(f32)
            hsv = hs_ref[rows, :]
            if k == 0:
                xr_halo = jnp.where(first_tile, 0.0, xrh_ref[...].astype(f32))
                hs_halo = jnp.where(first_tile, 0.0, hsh_ref[...])
            else:
                xr_halo = xr_ref[k * sub - BF16_ROWS:k * sub, :].astype(f32)
                hs_halo = hs_ref[k * sub - SUBLANES:k * sub, :]
            xs1, xs2, xs3 = _rows_before(xr_halo, xrv, (1, 2, 3))
            xc = _conv(xrv, (xs1, xs2, xs3), pv_ref)
            xcb_scr[rows, :] = xc.astype(bf16)
            a, mult = a_ref[rows, :], mult_ref[rows, :]
            gr, gi = gr_ref[rows, :].astype(f32), gi_ref[rows, :].astype(f32)
            gelu_v = gel_ref[rows, :].astype(f32)

            dy = dxv * vec_ref[R_GT_M:R_GT_M + 1, :]
            dyb = dy.astype(bf16)
            dyb_scr[rows, :] = dyb
            sm_ref[G_GT:G_GT + 1, :] += _colsum(dxv * y_ref[rows, :].astype(f32))
            sm_ref[G_BOUT:G_BOUT + 1, :] += _colsum(dy)
            mb_scr[rows, :] = (hsv * gelu_v).astype(bf16)
            dm = _mm_nt(dyb, wb_ref[2])
            dhs = dm * gelu_v
            dgpre = dm * hsv * geld_ref[rows, :].astype(f32)
            dgb = dgpre.astype(bf16)
            dgb_scr[rows, :] = dgb
            sm_ref[G_BY:G_BY + 1, :] += _colsum(dgpre)

            eps_in = eps_ref[0:1, :]
            eps, eps_out = _scan_rows(a, a * dhs, eps_in, reverse=True)
            eps_ref[0:1, :] = eps_out
            (eps_next,) = _rows_after(eps, jnp.broadcast_to(eps_in, (SUBLANES, w)), (1,))
            delta = dhs + eps_next
            (h_prev,) = _rows_before(hs_halo, hsv, (1,))
            dxi = delta * xc
            dgi = dxi * mult
            dla = (delta * h_prev) * a - (dxi * gi) * (a * a) / mult
            sm_ref[G_LS:G_LS + 1, :] += _colsum(dla * gr)
            dra = (dla * c_ls) * (gr - gr * gr)
            drx = dgi * (gi - gi * gi)
            drab, drxb = dra.astype(bf16), drx.astype(bf16)
            drab_scr[rows, :] = drab
            drxb_scr[rows, :] = drxb
            sm_ref[G_BA:G_BA + 1, :] += _colsum(dra)
            sm_ref[G_BX:G_BX + 1, :] += _colsum(drx)
            dxc = (delta * mult) * gi + _block_diag_t(drab, wsm_ref, 0) + _block_diag_t(drxb, wsm_ref, 1)

            sm_ref[G_CONVB:G_CONVB + 1, :] += _colsum(dxc)
            for kk, v in zip((3, 2, 1, 0), (xrv, xs1, xs2, xs3)):
                sm_ref[G_CW0 + kk:G_CW0 + kk + 1, :] += _colsum(dxc * v)
            ups = _rows_after(dxc, dxc8_ref[...], (1, 2, 3))
            dxc8_ref[...] = dxc[0:SUBLANES, :]
            dxr = dxc * pv_ref[P_CW0 + 3:P_CW0 + 4, :]
            for kk, v in zip((2, 1, 0), ups):
                dxr = dxr + v * pv_ref[P_CW0 + kk:P_CW0 + kk + 1, :]
            dxrb = dxr.astype(bf16)
            dxrb_scr[rows, :] = dxrb
            sm_ref[G_BIN:G_BIN + 1, :] += _colsum(dxr)
            dh = _mm_nt(dgb, wb_ref[0]) + _mm_nt(dxrb, wb_ref[1])
            sm_ref[G_SH:G_SH + 1, :] += _colsum(dh)
            sm_ref[G_GS:G_GS + 1, :] += _colsum(dh * n)
            dxo_ref[rows, :] = dxv + _norm_bwd(dh, n, r, gs)

        hb = hb_scr[...]
        accb_ref[0] += _mm_tn(hb, dgb_scr[...])
        accb_ref[1] += _mm_tn(hb, dxrb_scr[...])
        accb_ref[2] += _mm_tn(mb_scr[...], dyb_scr[...])
        for h in range(HEADS):
            cols = slice(h * HEAD_DIM, (h + 1) * HEAD_DIM)
            accs_ref[0, h] += _mm_tn(xcb_scr[:, cols], drab_scr[:, cols])
            accs_ref[1, h] += _mm_tn(xcb_scr[:, cols], drxb_scr[:, cols])

        @pl.when(i == nt - 1)
        def _():
            sm_ref[G_LS:G_LS + 1, :] = sm_ref[G_LS:G_LS + 1, :] * LRU_C
            for k in range(3):
                dwb_ref[:, k] = accb_ref[k].astype(bf16).reshape(N_DEV, shard, w)
            for k in range(2):
                for h in range(HEADS):
                    dwsm_ref[:, k, h] = accs_ref[k, h].astype(bf16).reshape(N_DEV, hshard, HEAD_DIM)

    rev = lambda i: (nt - 1 - i, 0)
    row = pl.BlockSpec((ts, D_MODEL), rev)
    roww = pl.BlockSpec((ts, w), rev)
    halo16 = pl.BlockSpec((BF16_ROWS, w), lambda i: (jnp.maximum((nt - 1 - i) * (ts // BF16_ROWS) - 1, 0), 0))
    halo8 = pl.BlockSpec((SUBLANES, w), lambda i: (jnp.maximum((nt - 1 - i) * (ts // SUBLANES) - 1, 0), 0))
    const = lambda *shape: pl.BlockSpec(shape, lambda i: (0,) * len(shape))
    operand = pltpu.VMEM((ts, w), bf16)
    return pl.pallas_call(
        body, name=f"lru_bwd_{layer}", grid=(nt,),
        out_shape=(jax.ShapeDtypeStruct((s, D_MODEL), f32),
                   jax.ShapeDtypeStruct((N_DEV, 3, shard, w), bf16),
                   jax.ShapeDtypeStruct((N_DEV, 2, HEADS, hshard, HEAD_DIM), bf16),
                   jax.ShapeDtypeStruct((16, w), f32)),
        in_specs=[row, row, roww, halo16, roww, halo8, roww, roww, roww, roww, roww, roww, row, const(8, D_MODEL),
                  const(3, w, w), const(2, HEADS, HEAD_DIM, HEAD_DIM), const(16, w)],
        out_specs=(row, const(N_DEV, 3, shard, w), const(N_DEV, 2, HEADS, hshard, HEAD_DIM), const(16, w)),
        scratch_shapes=[pltpu.VMEM((3, w, w), f32), pltpu.VMEM((2, HEADS, HEAD_DIM, HEAD_DIM), f32),
                        pltpu.VMEM((SUBLANES, w), f32), pltpu.VMEM((SUBLANES, w), f32)] + [operand] * 8,
        compiler_params=_params(("arbitrary",), 58),
    )(x, dx, xr, xr, hs, hs, a_all, mult_all, gr_all, gi_all, gel_all, geld_all, y, vec, wbig, wsm, pvec)


def _pool_tile(s):
    return min(512, s)


def _pool_counts(tile_index, ts):
    t = (tile_index * ts + lax.broadcasted_iota(jnp.int32, (ts, 1), 0) + 1).astype(f32)
    return [1.0 / jnp.minimum(t, float(win)) for win in POOL_WINDOWS]


def _pooled(h, halo, inv):
    ext = jnp.concatenate([halo, h], axis=0)
    out = []
    for g in range(len(POOL_WINDOWS)):
        acc = ext[:, g * HEAD_DIM:(g + 1) * HEAD_DIM]
        for step in range(g + 1):
            acc = acc + pltpu.roll(acc, 1 << step, 0)
        out.append(acc[POOL_HALO:] * inv[g] - h[:, g * HEAD_DIM:(g + 1) * HEAD_DIM])
    return out


def _pool_fwd(x, vec, pw, ps, layer):
    s = x.shape[0]
    ts = _pool_tile(s)

    def body(x_ref, vec_ref, pw_ref, ps_ref, xo_ref, y_ref, halo_ref):
        i = pl.program_id(0)

        @pl.when(i == 0)
        def _():
            halo_ref[...] = jnp.zeros_like(halo_ref)

        xv = x_ref[...]
        n, _ = _rms(xv)
        h = n * vec_ref[R_GS_M:R_GS_M + 1, :] + vec_ref[R_SH_M:R_SH_M + 1, :]
        pooled = _pooled(h, halo_ref[...], _pool_counts(i, ts))
        halo_ref[...] = h[ts - POOL_HALO:, :]
        mixed = jnp.concatenate([_mm(pooled[g].astype(bf16), pw_ref[g]) for g in range(HEADS)], axis=1)
        yv = mixed * ps_ref[0:1, :]
        y_ref[...] = yv.astype(bf16)
        xo_ref[...] = xv + vec_ref[R_GT_M:R_GT_M + 1, :] * yv

    row = pl.BlockSpec((ts, D_MODEL), lambda i: (i, 0))
    return pl.pallas_call(
        body, name=f"pool_fwd_{layer}", grid=(s // ts,),
        out_shape=(jax.ShapeDtypeStruct((s, D_MODEL), f32), jax.ShapeDtypeStruct((s, D_MODEL), bf16)),
        in_specs=[row, pl.BlockSpec((8, D_MODEL), lambda i: (0, 0)),
                  pl.BlockSpec((HEADS, HEAD_DIM, HEAD_DIM), lambda i: (0, 0, 0)),
                  pl.BlockSpec((8, D_MODEL), lambda i: (0, 0))],
        out_specs=(row, row),
        scratch_shapes=[pltpu.VMEM((POOL_HALO, D_MODEL), f32)],
        compiler_params=_params(("arbitrary",)),
    )(x, vec, pw, ps)


def _pool_bwd(x, dx, y, vec, pw, ps, layer):
    s = x.shape[0]
    ts = _pool_tile(s)
    nt = s // ts
    hshard = HEAD_DIM // N_DEV

    def body(x_ref, xh_ref, dx_ref, y_ref, vec_ref, pw_ref, ps_ref, dxo_ref, dpw_ref, sm_ref, acc_ref, q16_ref):
        i = pl.program_id(0)
        tile = nt - 1 - i

        @pl.when(i == 0)
        def _():
            acc_ref[...] = jnp.zeros_like(acc_ref)
            sm_ref[...] = jnp.zeros_like(sm_ref)
            q16_ref[...] = jnp.zeros_like(q16_ref)

        gs, sh = vec_ref[R_GS_M:R_GS_M + 1, :], vec_ref[R_SH_M:R_SH_M + 1, :]
        xv = x_ref[...]
        dxv = dx_ref[...]
        n, r = _rms(xv)
        h = n * gs + sh
        nh, _ = _rms(xh_ref[...])
        halo = jnp.where(tile == 0, 0.0, nh * gs + sh)
        inv = _pool_counts(tile, ts)
        pooled = _pooled(h, halo, inv)
        mixed = jnp.concatenate([_mm(pooled[g].astype(bf16), pw_ref[g]) for g in range(HEADS)], axis=1)

        dy = dxv * vec_ref[R_GT_M:R_GT_M + 1, :]
        sm_ref[G_GT:G_GT + 1, :] += _colsum(dxv * y_ref[...].astype(f32))
        sm_ref[3:4, :] += _colsum(dy * mixed)
        dmixed = (dy * ps_ref[0:1, :]).astype(bf16)
        dh_parts = []
        for g in range(HEADS):
            cols = slice(g * HEAD_DIM, (g + 1) * HEAD_DIM)
            acc_ref[g] += _mm_tn(pooled[g].astype(bf16), dmixed[:, cols])
            dpooled = _mm_nt(dmixed[:, cols], pw_ref[g])
            q = dpooled * inv[g]
            ext = jnp.concatenate([q, q16_ref[:, cols]], axis=0)
            q16_ref[:, cols] = q[0:POOL_HALO, :]
            for step in range(g + 1):
                ext = ext + pltpu.roll(ext, ext.shape[0] - (1 << step), 0)
            dh_parts.append(ext[:ts] - dpooled)
        dh = jnp.concatenate(dh_parts, axis=1)
        sm_ref[G_SH:G_SH + 1, :] += _colsum(dh)
        sm_ref[G_GS:G_GS + 1, :] += _colsum(dh * n)
        dxo_ref[...] = dxv + _norm_bwd(dh, n, r, gs)

        @pl.when(i == nt - 1)
        def _():
            for g in range(HEADS):
                dpw_ref[:, g] = acc_ref[g].astype(bf16).reshape(N_DEV, hshard, HEAD_DIM)

    rev = lambda i: (nt - 1 - i, 0)
    row = pl.BlockSpec((ts, D_MODEL), rev)
    halo16 = pl.BlockSpec((POOL_HALO, D_MODEL), lambda i: (jnp.maximum((nt - 1 - i) * (ts // POOL_HALO) - 1, 0), 0))
    const = lambda *shape: pl.BlockSpec(shape, lambda i: (0,) * len(shape))
    return pl.pallas_call(
        body, name=f"pool_bwd_{layer}", grid=(nt,),
        out_shape=(jax.ShapeDtypeStruct((s, D_MODEL), f32),
                   jax.ShapeDtypeStruct((N_DEV, HEADS, hshard, HEAD_DIM), bf16),
                   jax.ShapeDtypeStruct((8, D_MODEL), f32)),
        in_specs=[row, halo16, row, row, const(8, D_MODEL), const(HEADS, HEAD_DIM, HEAD_DIM), const(8, D_MODEL)],
        out_specs=(row, const(N_DEV, HEADS, hshard, HEAD_DIM), const(8, D_MODEL)),
        scratch_shapes=[pltpu.VMEM((HEADS, HEAD_DIM, HEAD_DIM), f32), pltpu.VMEM((POOL_HALO, D_MODEL), f32)],
        compiler_params=_params(("arbitrary",)),
    )(x, x, dx, y, vec, pw, ps)


def _final(x, target, g_fin):
    s = x.shape[0]
    ts = min(512, s)

    def body(x_ref, t_ref, g_ref, dx_ref, sm_ref):
        @pl.when(pl.program_id(0) == 0)
        def _():
            sm_ref[...] = jnp.zeros_like(sm_ref)

        g = g_ref[0:1, :]
        n, r = _rms(x_ref[...])
        err = n * g - t_ref[...]
        sm_ref[1:2, :] += 0.5 * jnp.sum(jnp.mean(err * err, axis=-1, keepdims=True), axis=0, keepdims=True)
        dyv = err * (1.0 / D_MODEL)
        sm_ref[0:1, :] += _colsum(dyv * n)
        dx_ref[...] = _norm_bwd(dyv, n, r, g)

    row = pl.BlockSpec((ts, D_MODEL), lambda i: (i, 0))
    return pl.pallas_call(
        body, name="final_loss", grid=(s // ts,),
        out_shape=(jax.ShapeDtypeStruct((s, D_MODEL), f32), jax.ShapeDtypeStruct((8, D_MODEL), f32)),
        in_specs=[row, row, pl.BlockSpec((8, D_MODEL), lambda i: (0, 0))],
        out_specs=(row, pl.BlockSpec((8, D_MODEL), lambda i: (0, 0))),
        compiler_params=_params(("arbitrary",)),
    )(x, target, g_fin)


def _small_pack(sm_ffn, sm_mix, sm_fin, table, g_mix, g_ffn, lam):
    def body(*refs):
        ffn, mix = refs[0:DEPTH], refs[DEPTH:2 * DEPTH]
        fin_ref, tab_ref, gm_ref, gf_ref, lam_ref, o_ref = refs[2 * DEPTH:]
        o_ref[...] = jnp.zeros_like(o_ref)
        for i in range(DEPTH):
            base = K_MOD + i * N_MOD
            o_ref[base + 0:base + 1, :] = mix[i][G_SH:G_SH + 1, :]
            o_ref[base + 1:base + 2, :] = mix[i][G_GS:G_GS + 1, :] * gm_ref[i:i + 1, :]
            o_ref[base + 2:base + 3, :] = mix[i][G_GT:G_GT + 1, :]
            o_ref[base + 3:base + 4, :] = ffn[i][G_SH:G_SH + 1, :]
            o_ref[base + 4:base + 5, :] = ffn[i][G_GS:G_GS + 1, :] * gf_ref[i:i + 1, :]
            o_ref[base + 5:base + 6, :] = ffn[i][G_GT:G_GT + 1, :]
            o_ref[K_NMIX + i:K_NMIX + i + 1, :] = mix[i][G_GS:G_GS + 1, :] * (1.0 + tab_ref[i, R_SC_M:R_SC_M + 1, :])
            o_ref[K_NFFN + i:K_NFFN + i + 1, :] = ffn[i][G_GS:G_GS + 1, :] * (1.0 + tab_ref[i, R_SC_F:R_SC_F + 1, :])
            j = i // 2
            if i % 2 == 0:
                for k, src in enumerate((G_BY, G_BIN, G_CONVB, None, G_BOUT)):
                    dst = K_LRUB + j * 5 + k
                    if src is None:
                        o_ref[dst:dst + 1, :] = mix[i][G_LS:G_LS + 1, :] * _sigmoid(-lam_ref[j:j + 1, :])
                    else:
                        o_ref[dst:dst + 1, :] = mix[i][src:src + 1, :]
                o_ref[K_CONVW + j * 4:K_CONVW + j * 4 + 4, :] = mix[i][G_CW0:G_CW0 + 4, :]
                o_ref[K_BA + j:K_BA + j + 1, :] = mix[i][G_BA:G_BA + 1, :]
                o_ref[K_BX + j:K_BX + j + 1, :] = mix[i][G_BX:G_BX + 1, :]
            else:
                o_ref[K_PS + j:K_PS + j + 1, :] = mix[i][3:4, :]
        o_ref[K_FIN:K_FIN + 2, :] = fin_ref[0:2, :]

    return pl.pallas_call(body, name="small_pack", out_shape=jax.ShapeDtypeStruct((K_ROWS, D_MODEL), f32))(
        *sm_ffn, *sm_mix, sm_fin, table, g_mix, g_ffn, lam)


def _small_sum(gathered):
    def body(g_ref, o_ref, token_ref):
        tot = g_ref[0]
        for src in range(1, N_DEV):
            tot = tot + g_ref[src]
        o_ref[...] = tot
        token_ref[...] = jnp.zeros_like(token_ref)

    return pl.pallas_call(
        body, name="small_sum",
        out_shape=(jax.ShapeDtypeStruct(gathered.shape[1:], f32), jax.ShapeDtypeStruct((8, 128), f32)))(gathered)


def _adamw_math(g, w, m, v):
    m = ADAM_B1 * m + (1.0 - ADAM_B1) * g
    v = ADAM_B2 * v + (1.0 - ADAM_B2) * (g * g)
    m_hat = m / (1.0 - ADAM_B1 ** ADAM_STEP)
    v_hat = v / (1.0 - ADAM_B2 ** ADAM_STEP)
    delta = -ADAM_LR * (m_hat / (jnp.sqrt(v_hat) + ADAM_EPS) + ADAM_WD * w)
    return delta, m, v


def _adamw_small(name, g, w, m, v):
    shape = w.shape
    two_d = (1, shape[0]) if len(shape) == 1 else (math.prod(shape[:-1]), shape[-1])

    def body(g_ref, w_ref, m_ref, v_ref, d_ref, mo_ref, vo_ref):
        d_ref[...], mo_ref[...], vo_ref[...] = _adamw_math(g_ref[...], w_ref[...], m_ref[...], v_ref[...])

    outs = pl.pallas_call(body, name=f"adamw_{name}", out_shape=tuple(jax.ShapeDtypeStruct(two_d, f32) for _ in range(3)))(
        *(t.reshape(two_d) for t in (g, w, m, v)))
    return tuple(t.reshape(shape) for t in outs)


def _block_rows(rows, cols):
    tr = max(SUBLANES, min(rows, (512 * 1024) // (4 * cols)))
    while rows % tr:
        tr //= 2
    return tr


def _adamw_reduce(name, landings, kind, w, m, v):
    nl = len(landings)
    rows, cols = landings[0].shape[2:]
    tr = _block_rows(rows, cols)
    per_layer = rows // tr

    def body(*refs):
        l_refs = refs[:nl]
        w_ref, m_ref, v_ref, g_ref, d_ref, mo_ref, vo_ref = refs[nl:]
        layer = pl.program_id(0)
        for k in range(nl):
            @pl.when(layer == k)
            def _(k=k):
                g = l_refs[k][0].astype(f32)
                for src in range(1, N_DEV):
                    g = g + l_refs[k][src].astype(f32)
                g_ref[...] = g
        d_ref[...], mo_ref[...], vo_ref[...] = _adamw_math(g_ref[...], w_ref[...], m_ref[...], v_ref[...])

    blk = pl.BlockSpec((tr, cols), lambda l, r: (l * per_layer + r, 0))
    land = [pl.BlockSpec((N_DEV, None, tr, cols), lambda l, r, k=k: (0, kind, jnp.where(l == k, r, 0), 0)) for k in range(nl)]
    return pl.pallas_call(
        body, name=f"adamw_{name}", grid=(nl, per_layer),
        out_shape=tuple(jax.ShapeDtypeStruct((nl * rows, cols), f32) for _ in range(4)),
        in_specs=land + [blk, blk, blk],
        out_specs=(blk, blk, blk, blk),
        compiler_params=_params(("arbitrary", "arbitrary"), 32),
    )(*landings, w, m, v)


def _adamw_w_mod(c_all, dmod_all, w, m, v):
    depth, d, cols = w.shape
    tr = 256

    def body(c_ref, dm_ref, w_ref, m_ref, v_ref, g_ref, d_ref, mo_ref, vo_ref):
        cv = c_ref[...]
        cond = cv * _sigmoid(cv)
        g = lax.dot_general(cond, dm_ref[...], (((0,), (0,)), ((), ())), preferred_element_type=f32,
                            precision=lax.Precision.HIGHEST)
        g_ref[...] = g
        d_ref[...], mo_ref[...], vo_ref[...] = _adamw_math(g, w_ref[...], m_ref[...], v_ref[...])

    blk = pl.BlockSpec((None, tr, cols), lambda i, r: (i, r, 0))
    return pl.pallas_call(
        body, name="adamw_w_mod", grid=(depth, d // tr),
        out_shape=tuple(jax.ShapeDtypeStruct(w.shape, f32) for _ in range(4)),
        in_specs=[pl.BlockSpec((N_DEV, tr), lambda i, r: (0, r)),
                  pl.BlockSpec((None, N_DEV, cols), lambda i, r: (i, 0, 0)), blk, blk, blk],
        out_specs=(blk, blk, blk, blk),
        compiler_params=_params(("arbitrary", "arbitrary"), 32),
    )(c_all, dmod_all, w, m, v)


def kernel(x, c, w_mod, b_mod, norm_mix_g, norm_ffn_g, lru_w_y, lru_b_y, lru_w_in, lru_b_in, lru_conv_w, lru_conv_b, lru_w_a, lru_b_a, lru_w_x, lru_b_x, lru_lambda, lru_w_out, lru_b_out, pool_w, pool_scale, ffn_w1, ffn_w2, final_norm_g, loss_target, m_w_mod, m_b_mod, m_norm_mix_g, m_norm_ffn_g, m_lru_w_y, m_lru_b_y, m_lru_w_in, m_lru_b_in, m_lru_conv_w, m_lru_conv_b, m_lru_w_a, m_lru_b_a, m_lru_w_x, m_lru_b_x, m_lru_lambda, m_lru_w_out, m_lru_b_out, m_pool_w, m_pool_scale, m_ffn_w1, m_ffn_w2, m_final_norm_g, v_w_mod, v_b_mod, v_norm_mix_g, v_norm_ffn_g, v_lru_w_y, v_lru_b_y, v_lru_w_in, v_lru_b_in, v_lru_conv_w, v_lru_conv_b, v_lru_w_a, v_lru_b_a, v_lru_w_x, v_lru_b_x, v_lru_lambda, v_lru_w_out, v_lru_b_out, v_pool_w, v_pool_scale, v_ffn_w1, v_ffn_w2, v_final_norm_g):
    me = 4 * lax.axis_index("x") + 2 * lax.axis_index("y") + lax.axis_index("c")
    n_lru = lru_w_y.shape[0]
    shard = LRU_WIDTH // N_DEV
    hshard = HEAD_DIM // N_DEV
    xs = x[0]
    target = loss_target[0]

    small_vecs = jnp.concatenate([
        lru_conv_w.reshape(n_lru * 4, shard), lru_b_a.reshape(n_lru, HEADS * hshard),
        lru_b_x.reshape(n_lru, HEADS * hshard), pool_scale, jnp.zeros((2, shard), f32)], axis=0)
    sv_g, c_g = _exchange([small_vecs, c], True, "gather_cond")
    conv_w_full = sv_g[:, 0:8].reshape(N_DEV, n_lru, 4, shard).transpose(1, 2, 0, 3).reshape(n_lru, 4, LRU_WIDTH)
    b_a_full = sv_g[:, 8:10].reshape(N_DEV, n_lru, HEADS, hshard).transpose(1, 2, 0, 3).reshape(n_lru, LRU_WIDTH)
    b_x_full = sv_g[:, 10:12].reshape(N_DEV, n_lru, HEADS, hshard).transpose(1, 2, 0, 3).reshape(n_lru, LRU_WIDTH)
    ps_full = sv_g[:, 12:14].transpose(1, 0, 2).reshape(n_lru, D_MODEL)
    c_all = c_g.reshape(N_DEV, D_MODEL)

    (mod_g,) = _exchange([_mod_part(c_all, w_mod)], True, "gather_mod", pieces=DEPTH)
    mod_row = lax.dynamic_index_in_dim(mod_g, me, axis=2, keepdims=False)
    mod_row = mod_row.transpose(1, 0, 2).reshape(DEPTH, N_MOD * D_MODEL)
    table, token = _mod_table(mod_row, b_mod, norm_mix_g, norm_ffn_g)

    first_pieces = 4
    (first_mix,), token = _send_start("gather_first_start", [[
        jnp.stack([lru_w_y[0], lru_w_in[0], lru_w_out[0]]).astype(bf16).reshape(3 * first_pieces, -1, LRU_WIDTH),
        jnp.stack([lru_w_a[0], lru_w_x[0]]).astype(bf16).reshape(first_pieces, -1, HEAD_DIM)]], True, me,
        pieces=first_pieces, after=token)

    parts = []
    for i in range(DEPTH):
        j = i // 2
        if i > 0 and i % 2 == 0:
            parts.append([(jnp.stack([lru_w_y[j], lru_w_in[j], lru_w_out[j]]) + token[0, 0]).astype(bf16),
                          (jnp.stack([lru_w_a[j], lru_w_x[j]]) + token[0, 0]).astype(bf16)])
        elif i % 2 == 1:
            parts.append([(pool_w[j] + token[0, 0]).astype(bf16)])
        parts.append([(ffn_w1[i] + token[0, 0]).astype(bf16), (ffn_w2[i] + token[0, 0]).astype(bf16)])
    first_got, token = _send_wait("gather_mix_wait_0", first_mix, [a for part in parts for a in part])
    handles, token = _send_start("gather_rest_start", parts, True, me, after=token)
    h_ffn = [handles[0], handles[2], handles[4], handles[6]]
    h_mix = [None, handles[1], handles[3], handles[5]]

    zero_row = jnp.zeros((1, LRU_WIDTH), f32)
    pvecs = [jnp.concatenate([lru_b_y[j:j + 1], lru_b_in[j:j + 1], lru_conv_b[j:j + 1], b_a_full[j:j + 1],
                              b_x_full[j:j + 1], lru_lambda[j:j + 1], lru_b_out[j:j + 1], zero_row,
                              conv_w_full[j], zero_row, zero_row, zero_row, zero_row], axis=0) for j in range(n_lru)]
    ps_rows = [jnp.concatenate([ps_full[j:j + 1], jnp.zeros((7, D_MODEL), f32)], axis=0) for j in range(n_lru)]

    saved = []
    ffn_w, mix_w = [], []
    h = xs
    for i in range(DEPTH):
        j = i // 2
        got = first_got if i == 0 else _send_wait(f"gather_mix_wait_{i}", h_mix[i], h)[0]
        if i % 2 == 0:
            got = [got[0].reshape(N_DEV, 3, shard, LRU_WIDTH), got[1].reshape(N_DEV, 2, HEADS, hshard, HEAD_DIM)]
            mix_w.append((got[0].transpose(1, 0, 2, 3).reshape(3, LRU_WIDTH, LRU_WIDTH),
                          got[1].transpose(1, 2, 0, 3, 4).reshape(2, HEADS, HEAD_DIM, HEAD_DIM)))
            h_mid, *lru_saved = _lru_fwd(h, table[i] + token[0, 0], mix_w[i][0], mix_w[i][1], pvecs[j], i)
            mix_saved = (h, tuple(lru_saved))
        else:
            mix_w.append((got[0].transpose(1, 0, 2, 3).reshape(HEADS, HEAD_DIM, HEAD_DIM),))
            h_mid, y_mix = _pool_fwd(h, table[i], mix_w[i][0], ps_rows[j], i)
            mix_saved = (h, y_mix)
        ffn_w.append(_send_wait(f"gather_ffn_wait_{i}", h_ffn[i], h_mid)[0])
        h_out, u, y_ffn, hb = _ffn_fwd(h_mid, table[i], ffn_w[i][0], ffn_w[i][1], i)
        saved.append((mix_saved, (h_mid, u, y_ffn, hb)))
        h = h_out
    fin_rows = jnp.concatenate([final_norm_g[None, :], jnp.zeros((7, D_MODEL), f32)], axis=0)
    dx, sm_fin = _final(h, target, fin_rows)

    sm_ffn, sm_mix = [None] * DEPTH, [None] * DEPTH
    x_ffn, x_mix = [None] * DEPTH, [None] * DEPTH
    token = jnp.zeros((8, 128), f32)
    last_mix = None
    for i in reversed(range(DEPTH)):
        j = i // 2
        mix_saved, (h_mid, u, y_ffn, hb) = saved[i]
        dx, da, dyb, sm_ffn[i] = _ffn_bwd_act(h_mid, dx, u, y_ffn, table[i] + token[0, 0], ffn_w[i][0], ffn_w[i][1], i)
        ffn_grads = [_ffn_bwd_w1(hb, da, i), _ffn_bwd_w2(u, dyb, i)]
        if last_mix is None:
            (x_ffn[i],), token = _send_start(f"grads_start_{i}", [ffn_grads], False, me)
        else:
            (x_mix[i + 1], x_ffn[i]), token = _send_start(f"grads_start_{i}", [last_mix, ffn_grads], False, me)
        if i % 2 == 0:
            h_in, lru_saved = mix_saved
            dx, dbig, dsmall, sm_mix[i] = _lru_bwd(
                h_in, dx, lru_saved, table[i] + token[0, 0], mix_w[i][0], mix_w[i][1], pvecs[j], i)
            last_mix = [dbig, dsmall]
        else:
            h_in, y_mix = mix_saved
            dx, dpool, sm = _pool_bwd(h_in, dx, y_mix, table[i] + token[0, 0], mix_w[i][0], ps_rows[j], i)
            sm_mix[i] = jnp.concatenate([sm, jnp.zeros((8, D_MODEL), f32)], axis=0)
            last_mix = [dpool]
    grad_x = dx[None]

    pack = _small_pack(sm_ffn, sm_mix, sm_fin, table + token[0, 0], norm_mix_g, norm_ffn_g, lru_lambda)
    (pack_g,) = _exchange([pack], True, "gather_small_grads", pieces=4)
    tot, token = _small_sum(pack_g)
    loss = tot[K_LOSS, 0]
    (x_mix[0],), _ = _send_start("grads_last_start", [[t + token[0, 0].astype(bf16) for t in last_mix]], False, me)
    cols = w_mod.shape[2]
    dmod_all = lax.dynamic_slice_in_dim(pack_g[:, K_MOD:K_MOD + DEPTH * N_MOD].reshape(N_DEV, DEPTH, N_MOD * D_MODEL),
                                        me * cols, cols, axis=2).transpose(1, 0, 2)
    results = {"w_mod": _adamw_w_mod(c_all, dmod_all, w_mod, m_w_mod, v_w_mod)}

    after = results["w_mod"][1]
    l_ffn = [_send_wait(f"grads_ffn_wait_{i}", x_ffn[i], after)[0] for i in reversed(range(DEPTH))][::-1]

    def reduce_update(name, landings, kind, w, m, v):
        rows = w.size // w.shape[-1]
        two_d = (rows, w.shape[-1])
        lands = [t.reshape(N_DEV, -1, rows // len(landings), w.shape[-1]) for t in landings]
        outs = _adamw_reduce(name, lands, kind, w.reshape(two_d), m.reshape(two_d), v.reshape(two_d))
        return tuple(t.reshape(w.shape) for t in outs)

    results["ffn_w1"] = reduce_update("ffn_w1", [t[0] for t in l_ffn], 0, ffn_w1, m_ffn_w1, v_ffn_w1)
    results["ffn_w2"] = reduce_update("ffn_w2", [t[1] for t in l_ffn], 0, ffn_w2, m_ffn_w2, v_ffn_w2)
    after = results["ffn_w2"][1]
    l_mix = [_send_wait(f"grads_mix_wait_{i}", x_mix[i], after)[0] for i in reversed(range(DEPTH))][::-1]
    l_lru_big = [l_mix[i][0] for i in range(0, DEPTH, 2)]
    l_lru_small = [l_mix[i][1] for i in range(0, DEPTH, 2)]
    l_pool = [l_mix[i][0] for i in range(1, DEPTH, 2)]
    results["lru_w_y"] = reduce_update("lru_w_y", l_lru_big, 0, lru_w_y, m_lru_w_y, v_lru_w_y)
    results["lru_w_in"] = reduce_update("lru_w_in", l_lru_big, 1, lru_w_in, m_lru_w_in, v_lru_w_in)
    results["lru_w_out"] = reduce_update("lru_w_out", l_lru_big, 2, lru_w_out, m_lru_w_out, v_lru_w_out)
    results["lru_w_a"] = reduce_update("lru_w_a", l_lru_small, 0, lru_w_a, m_lru_w_a, v_lru_w_a)
    results["lru_w_x"] = reduce_update("lru_w_x", l_lru_small, 1, lru_w_x, m_lru_w_x, v_lru_w_x)
    results["pool_w"] = reduce_update("pool_w", l_pool, 0, pool_w, m_pool_w, v_pool_w)

    def my_cols(full, width):
        return lax.dynamic_slice_in_dim(full, me * width, width, axis=full.ndim - 1)

    lru_rows = tot[K_LRUB:K_LRUB + 5 * n_lru].reshape(n_lru, 5, LRU_WIDTH)
    small_grads = {
        "b_mod": tot[K_MOD:K_MOD + DEPTH * N_MOD].reshape(DEPTH, N_MOD * D_MODEL),
        "norm_mix_g": tot[K_NMIX:K_NMIX + DEPTH],
        "norm_ffn_g": tot[K_NFFN:K_NFFN + DEPTH],
        "lru_b_y": lru_rows[:, 0], "lru_b_in": lru_rows[:, 1], "lru_conv_b": lru_rows[:, 2],
        "lru_lambda": lru_rows[:, 3], "lru_b_out": lru_rows[:, 4],
        "lru_conv_w": my_cols(tot[K_CONVW:K_CONVW + 4 * n_lru].reshape(n_lru, 4, LRU_WIDTH), shard),
        "lru_b_a": my_cols(tot[K_BA:K_BA + n_lru].reshape(n_lru, HEADS, HEAD_DIM), hshard),
        "lru_b_x": my_cols(tot[K_BX:K_BX + n_lru].reshape(n_lru, HEADS, HEAD_DIM), hshard),
        "pool_scale": my_cols(tot[K_PS:K_PS + n_lru], shard),
        "final_norm_g": tot[K_FIN],
    }
    given = dict(b_mod=(b_mod, m_b_mod, v_b_mod), norm_mix_g=(norm_mix_g, m_norm_mix_g, v_norm_mix_g),
                 norm_ffn_g=(norm_ffn_g, m_norm_ffn_g, v_norm_ffn_g), lru_b_y=(lru_b_y, m_lru_b_y, v_lru_b_y),
                 lru_b_in=(lru_b_in, m_lru_b_in, v_lru_b_in), lru_conv_w=(lru_conv_w, m_lru_conv_w, v_lru_conv_w),
                 lru_conv_b=(lru_conv_b, m_lru_conv_b, v_lru_conv_b), lru_b_a=(lru_b_a, m_lru_b_a, v_lru_b_a),
                 lru_b_x=(lru_b_x, m_lru_b_x, v_lru_b_x), lru_lambda=(lru_lambda, m_lru_lambda, v_lru_lambda),
                 lru_b_out=(lru_b_out, m_lru_b_out, v_lru_b_out), pool_scale=(pool_scale, m_pool_scale, v_pool_scale),
                 final_norm_g=(final_norm_g, m_final_norm_g, v_final_norm_g))
    for name, g in small_grads.items():
        results[name] = (g,) + _adamw_small(name, g, *given[name])

    order = ["w_mod", "b_mod", "norm_mix_g", "norm_ffn_g", "lru_w_y", "lru_b_y", "lru_w_in", "lru_b_in", "lru_conv_w",
             "lru_conv_b", "lru_w_a", "lru_b_a", "lru_w_x", "lru_b_x", "lru_lambda", "lru_w_out", "lru_b_out", "pool_w",
             "pool_scale", "ffn_w1", "ffn_w2", "final_norm_g"]
    return (loss, grad_x, *[results[n][0] for n in order], *[results[n][1] for n in order],
            *[results[n][2] for n in order], *[results[n][3] for n in order])
```

```python
import math

import jax
import jax.numpy as jnp
from jax import lax
from jax.experimental import pallas as pl
from jax.experimental.pallas import tpu as pltpu

f32, bf16 = jnp.float32, jnp.bfloat16

D_MODEL = 1024
LRU_WIDTH = 1024
HEADS = 4
HEAD_DIM = 256
D_FF = 4096
DEPTH = 4
N_MOD = 6
N_DEV = 8
FF_CHUNK = D_FF // N_DEV
POOL_WINDOWS = (2, 4, 8, 16)
POOL_HALO = 16
EPS = 1e-6
LRU_C = 8.0

ADAM_LR = 0.001
ADAM_B1 = 0.9
ADAM_B2 = 0.999
ADAM_EPS = 1e-08
ADAM_WD = 0.01
ADAM_STEP = 10

SUBLANES = 8
BF16_ROWS = 16

R_SH_M, R_SC_M, R_GT_M, R_SH_F, R_SC_F, R_GT_F, R_GS_M, R_GS_F = range(8)
P_BY, P_BIN, P_CONVB, P_BA, P_BX, P_LAM, P_BOUT, P_CW0 = 0, 1, 2, 3, 4, 5, 6, 8
G_SH, G_GS, G_GT, G_BY, G_BIN, G_CONVB, G_BA, G_BX, G_LS, G_BOUT, G_CW0 = 0, 1, 2, 3, 4, 5, 6, 7, 8, 9, 10
K_MOD, K_NMIX, K_NFFN, K_LRUB, K_CONVW, K_BA, K_BX, K_PS, K_FIN, K_LOSS, K_ROWS = 0, 24, 28, 32, 42, 50, 52, 54, 56, 57, 64


def _params(semantics=None, vmem_mb=56):
    return pltpu.CompilerParams(dimension_semantics=semantics, vmem_limit_bytes=vmem_mb * 1024 * 1024)


def _mm(a, b):
    return jnp.dot(a, b, preferred_element_type=f32)


def _mm_nt(a, b):
    return lax.dot_general(a, b, (((1,), (1,)), ((), ())), preferred_element_type=f32)


def _mm_tn(a, b):
    return lax.dot_general(a, b, (((0,), (0,)), ((), ())), preferred_element_type=f32)


def _rms(x):
    r = lax.rsqrt(jnp.mean(x * x, axis=-1, keepdims=True) + EPS)
    return x * r, r


def _norm_bwd(dh, n, r, gs):
    dn = dh * gs
    return r * (dn - n * jnp.mean(dn * n, axis=-1, keepdims=True))


def _colsum(v):
    return jnp.sum(v, axis=0, keepdims=True)


def _sigmoid(v):
    return 0.5 * jnp.tanh(0.5 * v) + 0.5


def _log_sigmoid(v):
    return jnp.minimum(v, 0.0) - jnp.log1p(jnp.exp(-jnp.abs(v)))


_GELU_C = 0.7978845608028654
_GELU_A = 0.044715


def _gelu_and_grad(v):
    v2 = v * v
    t = jnp.tanh(_GELU_C * v * (1.0 + _GELU_A * v2))
    p = 0.5 + 0.5 * t
    return v * p, p + (0.5 * v) * (1.0 - t * t) * (_GELU_C + (3.0 * _GELU_A * _GELU_C) * v2)


def _rows_before(halo, v, shifts):
    hr = halo.shape[0]
    ext = jnp.concatenate([halo, v], axis=0)
    return [pltpu.roll(ext, k, 0)[hr:] for k in shifts]


def _rows_after(v, halo, shifts):
    n = v.shape[0]
    ext = jnp.concatenate([v, halo], axis=0)
    return [pltpu.roll(ext, ext.shape[0] - k, 0)[:n] for k in shifts]


def _shift_matrix(n, halo_rows, shifts):
    rows = lax.broadcasted_iota(jnp.int32, (n, n + halo_rows), 0)
    cols = lax.broadcasted_iota(jnp.int32, (n, n + halo_rows), 1)
    return jnp.concatenate([(cols == rows + halo_rows - k).astype(bf16) for k in shifts], axis=0)


def _shifted_rows(sel, halo, v):
    n = v.shape[0]
    out = _mm(sel, jnp.concatenate([halo, v], axis=0))
    return [out[j * n:(j + 1) * n] for j in range(sel.shape[0] // n)]


def _block_diag(v, w_ref, kind):
    return jnp.concatenate(
        [_mm(v[:, h * HEAD_DIM:(h + 1) * HEAD_DIM], w_ref[kind, h]) for h in range(HEADS)], axis=1)


def _block_diag_t(v, w_ref, kind):
    return jnp.concatenate(
        [_mm_nt(v[:, h * HEAD_DIM:(h + 1) * HEAD_DIM], w_ref[kind, h]) for h in range(HEADS)], axis=1)


def _exchange(arrays, gather, name, pieces=1):
    n = len(arrays)
    peers = N_DEV - 1

    def body(*refs):
        ins, outs = refs[:n], refs[n:2 * n]
        send_sems, recv_sems, local_sems = refs[2 * n:]
        x, y, c = lax.axis_index("x"), lax.axis_index("y"), lax.axis_index("c")
        me = 4 * x + 2 * y + c
        local = []
        for k in range(n):
            cp = pltpu.make_async_copy(ins[k] if gather else ins[k].at[me], outs[k].at[me], local_sems.at[k])
            cp.start()
            local.append(cp)
        remote = _peer_copies(ins, outs, send_sems, recv_sems, gather, pieces)
        for cp in remote:
            cp.start()
        for cp in remote:
            cp.wait()
        for cp in local:
            cp.wait()

    out_shape = tuple(
        jax.ShapeDtypeStruct(((N_DEV,) + a.shape) if gather else a.shape, a.dtype) for a in arrays)
    outs = pl.pallas_call(
        body, name=name, out_shape=out_shape,
        in_specs=[pl.BlockSpec(memory_space=pl.ANY)] * n,
        out_specs=tuple(pl.BlockSpec(memory_space=pl.ANY) for _ in range(n)),
        scratch_shapes=[pltpu.SemaphoreType.DMA((n * pieces * peers,)), pltpu.SemaphoreType.DMA((n * pieces * peers,)),
                        pltpu.SemaphoreType.DMA((n,))],
        compiler_params=pltpu.CompilerParams(has_side_effects=True),
    )(*arrays)
    return list(outs)


_HBM = pl.BlockSpec(memory_space=pltpu.HBM)
_SEM = pl.BlockSpec(memory_space=pltpu.SEMAPHORE)
_DATAFLOW = pltpu.SideEffectType.DATAFLOW_SIDE_EFFECTING


def _peer_copies(src_refs, land_refs, send_sems, recv_sems, gather, pieces=1):
    x, y, c = lax.axis_index("x"), lax.axis_index("y"), lax.axis_index("c")
    me = 4 * x + 2 * y + c
    peers = N_DEV - 1
    copies = []
    for p in range(1, N_DEV):
        px = 1 - x if p & 4 else x
        py = 1 - y if p & 2 else y
        pc = 1 - c if p & 1 else c
        for k in range(len(src_refs)):
            block = src_refs[k] if gather else src_refs[k].at[4 * px + 2 * py + pc]
            dst = land_refs[k].at[me]
            rows = block.shape[0] // pieces
            for r in range(pieces):
                part = pl.ds(r * rows, rows)
                sem = (k * pieces + r) * peers + p - 1
                copies.append(pltpu.make_async_remote_copy(
                    src_ref=block.at[part] if pieces > 1 else block, dst_ref=dst.at[part] if pieces > 1 else dst,
                    send_sem=send_sems.at[sem], recv_sem=recv_sems.at[sem],
                    device_id=(px, py, pc), device_id_type=pl.DeviceIdType.MESH))
    return copies


def _landing(srcs, gather, me):
    out = []
    for a in srcs:
        own = a if gather else lax.dynamic_index_in_dim(a, me, 0, keepdims=False)
        out.append(lax.dynamic_update_index_in_dim(lax.empty((N_DEV,) + own.shape, own.dtype), own, me, 0))
    return out


def _send_start(name, groups, gather, me, pieces=1, after=None):
    sizes = [len(g) for g in groups]
    srcs = [a for g in groups for a in g]
    n = len(srcs)
    lands = _landing(srcs, gather, me)
    ng = len(groups)
    first = [sum(sizes[:g]) for g in range(ng)]
    extra = [] if after is None else [after]

    def body(*refs):
        src_refs, land_refs = refs[:n], refs[n:2 * n]
        sems, token = refs[2 * n + len(extra):2 * n + len(extra) + 2 * ng], refs[-1]
        for g in range(ng):
            part = slice(first[g], first[g] + sizes[g])
            for cp in _peer_copies(src_refs[part], land_refs[part], sems[2 * g], sems[2 * g + 1], gather, pieces):
                cp.start()
        token[...] = jnp.zeros_like(token)

    sem_shapes = [pltpu.SemaphoreType.DMA((sizes[g // 2] * pieces * (N_DEV - 1),)) for g in range(2 * ng)]
    outs = pl.pallas_call(
        body, name=name,
        out_shape=(*sem_shapes, *[pltpu.HBM(a.shape, a.dtype) for a in (*srcs, *lands)], jax.ShapeDtypeStruct((8, 128), f32)),
        in_specs=[_HBM] * (2 * n) + [pl.BlockSpec(memory_space=pl.ANY)] * len(extra),
        out_specs=(*[_SEM] * (2 * ng), *[_HBM] * (2 * n), pl.BlockSpec(memory_space=pltpu.VMEM)),
        input_output_aliases={k: 2 * ng + k for k in range(2 * n)},
        compiler_params=pltpu.CompilerParams(has_side_effects=_DATAFLOW),
    )(*[pltpu.with_memory_space_constraint(a, pltpu.HBM) for a in (*srcs, *lands)], *extra)
    srcs_thru, lands_thru = outs[2 * ng:2 * ng + n], outs[2 * ng + n:2 * ng + 2 * n]
    handles = [(outs[2 * g], outs[2 * g + 1], list(srcs_thru[first[g]:first[g] + sizes[g]]),
                list(lands_thru[first[g]:first[g] + sizes[g]]), gather, pieces) for g in range(ng)]
    return handles, outs[-1]


def _send_wait(name, handle, after):
    send_sems, recv_sems, srcs, lands, gather, pieces = handle
    n = len(srcs)
    after = list(after) if isinstance(after, (list, tuple)) else [after]

    def body(*refs):
        src_refs, land_refs = refs[:n], refs[n:2 * n]
        for cp in _peer_copies(src_refs, land_refs, refs[2 * n], refs[2 * n + 1], gather, pieces):
            cp.wait_send()
            cp.wait_recv()
        refs[-1][...] = jnp.zeros_like(refs[-1])

    outs = pl.pallas_call(
        body, name=name,
        out_shape=(*[pltpu.HBM(a.shape, a.dtype) for a in (*srcs, *lands)], jax.ShapeDtypeStruct((8, 128), f32)),
        in_specs=[_HBM] * (2 * n) + [_SEM, _SEM] + [pl.BlockSpec(memory_space=pl.ANY)] * len(after),
        out_specs=(*[_HBM] * (2 * n), pl.BlockSpec(memory_space=pltpu.VMEM)),
        input_output_aliases={k: k for k in range(2 * n)},
        compiler_params=pltpu.CompilerParams(has_side_effects=_DATAFLOW),
    )(*srcs, *lands, send_sems, recv_sems, *after)
    return list(outs[n:2 * n]), outs[-1]


def _mod_part(c_all, w_mod):
    depth, d, cols = w_mod.shape

    def body(c_ref, w_ref, o_ref):
        cv = c_ref[...]
        cond = cv * _sigmoid(cv)
        o_ref[...] = jnp.dot(cond, w_ref[...], preferred_element_type=f32, precision=lax.Precision.HIGHEST)

    return pl.pallas_call(
        body, name="mod_part", grid=(depth,),
        out_shape=jax.ShapeDtypeStruct((depth, N_DEV, cols), f32),
        in_specs=[pl.BlockSpec((N_DEV, d), lambda i: (0, 0)), pl.BlockSpec((None, d, cols), lambda i: (i, 0, 0))],
        out_specs=pl.BlockSpec((None, N_DEV, cols), lambda i: (i, 0, 0)),
        compiler_params=_params(("arbitrary",), 32),
    )(c_all, w_mod)


def _mod_table(mod_row, b_mod, g_mix, g_ffn):
    def body(m_ref, b_ref, gm_ref, gf_ref, o_ref, token_ref):
        for i in range(DEPTH):
            for k in range(N_MOD):
                o_ref[i, k:k + 1, :] = m_ref[i:i + 1, k * D_MODEL:(k + 1) * D_MODEL] + b_ref[i:i + 1, k * D_MODEL:(k + 1) * D_MODEL]
            o_ref[i, R_GS_M:R_GS_M + 1, :] = gm_ref[i:i + 1, :] * (1.0 + o_ref[i, R_SC_M:R_SC_M + 1, :])
            o_ref[i, R_GS_F:R_GS_F + 1, :] = gf_ref[i:i + 1, :] * (1.0 + o_ref[i, R_SC_F:R_SC_F + 1, :])
        token_ref[...] = jnp.zeros_like(token_ref)

    return pl.pallas_call(
        body, name="mod_table",
        out_shape=(jax.ShapeDtypeStruct((DEPTH, 8, D_MODEL), f32), jax.ShapeDtypeStruct((8, 128), f32)))(
        mod_row, b_mod, g_mix, g_ffn)


def _ffn_tile(s):
    return min(512, s)


def _layer_weights(shape):
    return pl.BlockSpec((N_DEV,) + shape, lambda i: (0, 0, 0))


def _ffn_fwd(x, vec, w1g, w2g, layer):
    s = x.shape[0]
    ts = _ffn_tile(s)

    def body(x_ref, vec_ref, w1_ref, w2_ref, xo_ref, u_ref, y_ref, hb_ref):
        xv = x_ref[...]
        n, _ = _rms(xv)
        hb = (n * vec_ref[R_GS_F:R_GS_F + 1, :] + vec_ref[R_SH_F:R_SH_F + 1, :]).astype(bf16)
        hb_ref[...] = hb
        yv = jnp.zeros((ts, D_MODEL), f32)
        for f in range(N_DEV):
            u = jnp.maximum(_mm(hb, w1_ref[f]), 0.0)
            u_ref[:, f * FF_CHUNK:(f + 1) * FF_CHUNK] = u.astype(bf16)
            yv = yv + _mm((u * u).astype(bf16), w2_ref[f])
        y_ref[...] = yv.astype(bf16)
        xo_ref[...] = xv + vec_ref[R_GT_F:R_GT_F + 1, :] * yv

    row = pl.BlockSpec((ts, D_MODEL), lambda i: (i, 0))
    return pl.pallas_call(
        body, name=f"ffn_fwd_{layer}", grid=(s // ts,),
        out_shape=(jax.ShapeDtypeStruct((s, D_MODEL), f32), jax.ShapeDtypeStruct((s, D_FF), bf16),
                   jax.ShapeDtypeStruct((s, D_MODEL), bf16), jax.ShapeDtypeStruct((s, D_MODEL), bf16)),
        in_specs=[row, pl.BlockSpec((8, D_MODEL), lambda i: (0, 0)),
                  _layer_weights((D_MODEL, FF_CHUNK)), _layer_weights((FF_CHUNK, D_MODEL))],
        out_specs=(row, pl.BlockSpec((ts, D_FF), lambda i: (i, 0)), row, row),
        compiler_params=_params(("arbitrary",), 56),
    )(x, vec, w1g, w2g)


def _ffn_bwd_act(x, dx, u, y, vec, w1g, w2g, layer):
    s = x.shape[0]
    ts = _ffn_tile(s)

    def body(x_ref, dx_ref, u_ref, y_ref, vec_ref, w1_ref, w2_ref, dxo_ref, da_ref, dyb_ref, sm_ref):
        @pl.when(pl.program_id(0) == 0)
        def _():
            sm_ref[...] = jnp.zeros_like(sm_ref)

        dxv = dx_ref[...]
        dyb = (dxv * vec_ref[R_GT_F:R_GT_F + 1, :]).astype(bf16)
        dyb_ref[...] = dyb
        sm_ref[G_GT:G_GT + 1, :] += _colsum(dxv * y_ref[...].astype(f32))
        dh = jnp.zeros((ts, D_MODEL), f32)
        for f in range(N_DEV):
            cols = slice(f * FF_CHUNK, (f + 1) * FF_CHUNK)
            dz = _mm_nt(dyb, w2_ref[f])
            dab = (dz * (2.0 * u_ref[:, cols].astype(f32))).astype(bf16)
            da_ref[:, cols] = dab
            dh = dh + _mm_nt(dab, w1_ref[f])
        n, r = _rms(x_ref[...])
        sm_ref[G_SH:G_SH + 1, :] += _colsum(dh)
        sm_ref[G_GS:G_GS + 1, :] += _colsum(dh * n)
        dxo_ref[...] = dxv + _norm_bwd(dh, n, r, vec_ref[R_GS_F:R_GS_F + 1, :])

    row = pl.BlockSpec((ts, D_MODEL), lambda i: (i, 0))
    wide = pl.BlockSpec((ts, D_FF), lambda i: (i, 0))
    return pl.pallas_call(
        body, name=f"ffn_bwd_act_{layer}", grid=(s // ts,),
        out_shape=(jax.ShapeDtypeStruct((s, D_MODEL), f32), jax.ShapeDtypeStruct((s, D_FF), bf16),
                   jax.ShapeDtypeStruct((s, D_MODEL), bf16), jax.ShapeDtypeStruct((8, D_MODEL), f32)),
        in_specs=[row, row, wide, row, pl.BlockSpec((8, D_MODEL), lambda i: (0, 0)),
                  _layer_weights((D_MODEL, FF_CHUNK)), _layer_weights((FF_CHUNK, D_MODEL))],
        out_specs=(row, wide, row, pl.BlockSpec((8, D_MODEL), lambda i: (0, 0))),
        compiler_params=_params(("arbitrary",), 58),
    )(x, dx, u, y, vec, w1g, w2g)


def _ffn_bwd_w1(hb, da, layer):
    s = hb.shape[0]
    ts = _ffn_tile(s)
    nt = s // ts

    def body(hb_ref, da_ref, dw_ref, acc_ref):
        i = pl.program_id(0)

        @pl.when(i == 0)
        def _():
            acc_ref[...] = jnp.zeros_like(acc_ref)

        hb = hb_ref[...]
        for f in range(N_DEV):
            acc_ref[f] += _mm_tn(hb, da_ref[:, f * FF_CHUNK:(f + 1) * FF_CHUNK])

        @pl.when(i == nt - 1)
        def _():
            dw_ref[...] = acc_ref[...].astype(bf16)

    return pl.pallas_call(
        body, name=f"ffn_bwd_w1_{layer}", grid=(nt,),
        out_shape=jax.ShapeDtypeStruct((N_DEV, D_MODEL, FF_CHUNK), bf16),
        in_specs=[pl.BlockSpec((ts, D_MODEL), lambda i: (i, 0)), pl.BlockSpec((ts, D_FF), lambda i: (i, 0))],
        out_specs=pl.BlockSpec((N_DEV, D_MODEL, FF_CHUNK), lambda i: (0, 0, 0)),
        scratch_shapes=[pltpu.VMEM((N_DEV, D_MODEL, FF_CHUNK), f32)],
        compiler_params=_params(("arbitrary",), 56),
    )(hb, da)


def _ffn_bwd_w2(u, dyb, layer):
    s = u.shape[0]
    ts = _ffn_tile(s)
    nt = s // ts

    def body(u_ref, dyb_ref, dw_ref, acc_ref):
        i = pl.program_id(0)

        @pl.when(i == 0)
        def _():
            acc_ref[...] = jnp.zeros_like(acc_ref)

        dyb = dyb_ref[...]
        for f in range(N_DEV):
            uv = u_ref[:, f * FF_CHUNK:(f + 1) * FF_CHUNK].astype(f32)
            acc_ref[f] += _mm_tn((uv * uv).astype(bf16), dyb)

        @pl.when(i == nt - 1)
        def _():
            dw_ref[...] = acc_ref[...].astype(bf16)

    return pl.pallas_call(
        body, name=f"ffn_bwd_w2_{layer}", grid=(nt,),
        out_shape=jax.ShapeDtypeStruct((N_DEV, FF_CHUNK, D_MODEL), bf16),
        in_specs=[pl.BlockSpec((ts, D_FF), lambda i: (i, 0)), pl.BlockSpec((ts, D_MODEL), lambda i: (i, 0))],
        out_specs=pl.BlockSpec((N_DEV, FF_CHUNK, D_MODEL), lambda i: (0, 0, 0)),
        scratch_shapes=[pltpu.VMEM((N_DEV, FF_CHUNK, D_MODEL), f32)],
        compiler_params=_params(("arbitrary",), 56),
    )(u, dyb)


def _lru_gates(xc, wsm_ref, pv_ref):
    xcb = xc.astype(bf16)
    gr = _sigmoid(_block_diag(xcb, wsm_ref, 0) + pv_ref[P_BA:P_BA + 1, :])
    gi = _sigmoid(_block_diag(xcb, wsm_ref, 1) + pv_ref[P_BX:P_BX + 1, :])
    log_a = (LRU_C * _log_sigmoid(pv_ref[P_LAM:P_LAM + 1, :])) * gr
    t = jnp.tanh(log_a)
    return gr, gi, jnp.exp(log_a), jnp.sqrt((-2.0 * t) / (1.0 - t))


def _conv(xr, taps_before, pv_ref):
    xc = xr * pv_ref[P_CW0 + 3:P_CW0 + 4, :] + pv_ref[P_CONVB:P_CONVB + 1, :]
    for k, v in zip((2, 1, 0), taps_before):
        xc = xc + v * pv_ref[P_CW0 + k:P_CW0 + k + 1, :]
    return xc


LRU_FWD_SUB, LRU_FWD_SUBS = 128, 4
LRU_BWD_SUB, LRU_BWD_SUBS = 256, 1


def _scan_rows(a, u, carry, reverse):
    groups = a.shape[0] // SUBLANES
    row = lax.broadcasted_iota(jnp.int32, (SUBLANES, a.shape[1]), 0)
    outs = [None] * groups
    for j in range(groups):
        g = groups - 1 - j if reverse else j
        av, uv = a[g * SUBLANES:(g + 1) * SUBLANES], u[g * SUBLANES:(g + 1) * SUBLANES]
        for k in (1, 2, 4):
            if reverse:
                valid, shift = row < SUBLANES - k, SUBLANES - k
            else:
                valid, shift = row >= k, k
            a_s = jnp.where(valid, pltpu.roll(av, shift, 0), 1.0)
            u_s = jnp.where(valid, pltpu.roll(uv, shift, 0), 0.0)
            uv = uv + av * u_s
            av = av * a_s
        h = uv + av * carry
        outs[g] = h
        carry = h[0:1, :] if reverse else h[SUBLANES - 1:SUBLANES, :]
    return jnp.concatenate(outs, axis=0), carry


def _lru_fwd(x, vec, wbig, wsm, pvec, layer):
    s = x.shape[0]
    sub = min(LRU_FWD_SUB, s)
    ts = min(sub * LRU_FWD_SUBS, s)
    nsub = ts // sub
    w = LRU_WIDTH

    def body(x_ref, vec_ref, wb_ref, wsm_ref, pv_ref, xo_ref, xr_ref, hs_ref, a_ref, mult_ref, gr_ref, gi_ref,
             gel_ref, geld_ref, y_ref, tail_ref, carry_ref):
        @pl.when(pl.program_id(0) == 0)
        def _():
            tail_ref[...] = jnp.zeros_like(tail_ref)
            carry_ref[...] = jnp.zeros_like(carry_ref)

        sel = _shift_matrix(sub, BF16_ROWS, (1, 2, 3))
        for k in range(nsub):
            rows = slice(k * sub, (k + 1) * sub)
            xv = x_ref[rows, :]
            n, _ = _rms(xv)
            hb = (n * vec_ref[R_GS_M:R_GS_M + 1, :] + vec_ref[R_SH_M:R_SH_M + 1, :]).astype(bf16)
            gelu_v, gelu_d = _gelu_and_grad(_mm(hb, wb_ref[0]) + pv_ref[P_BY:P_BY + 1, :])
            gel_ref[rows, :] = gelu_v.astype(bf16)
            geld_ref[rows, :] = gelu_d.astype(bf16)
            xrb = (_mm(hb, wb_ref[1]) + pv_ref[P_BIN:P_BIN + 1, :]).astype(bf16)
            xr_ref[rows, :] = xrb
            xc = _conv(xrb.astype(f32), _shifted_rows(sel, tail_ref[...], xrb), pv_ref)
            tail_ref[...] = xrb[sub - BF16_ROWS:, :]
            gr, gi, a, mult = _lru_gates(xc, wsm_ref, pv_ref)
            gr_ref[rows, :] = gr.astype(bf16)
            gi_ref[rows, :] = gi.astype(bf16)
            a_ref[rows, :] = a
            mult_ref[rows, :] = mult
            hs, carry = _scan_rows(a, mult * (gi * xc), carry_ref[0:1, :], reverse=False)
            carry_ref[0:1, :] = carry
            hs_ref[rows, :] = hs
            yv = _mm((hs * gelu_v).astype(bf16), wb_ref[2]) + pv_ref[P_BOUT:P_BOUT + 1, :]
            y_ref[rows, :] = yv.astype(bf16)
            xo_ref[rows, :] = xv + vec_ref[R_GT_M:R_GT_M + 1, :] * yv

    row = pl.BlockSpec((ts, D_MODEL), lambda i: (i, 0))
    roww = pl.BlockSpec((ts, w), lambda i: (i, 0))
    wide = lambda dt: jax.ShapeDtypeStruct((s, w), dt)
    return pl.pallas_call(
        body, name=f"lru_fwd_{layer}", grid=(s // ts,),
        out_shape=(jax.ShapeDtypeStruct((s, D_MODEL), f32), wide(bf16), wide(f32), wide(f32), wide(f32),
                   wide(bf16), wide(bf16), wide(bf16), wide(bf16), jax.ShapeDtypeStruct((s, D_MODEL), bf16)),
        in_specs=[row, pl.BlockSpec((8, D_MODEL), lambda i: (0, 0)),
                  pl.BlockSpec((3, w, w), lambda i: (0, 0, 0)),
                  pl.BlockSpec((2, HEADS, HEAD_DIM, HEAD_DIM), lambda i: (0, 0, 0, 0)),
                  pl.BlockSpec((16, w), lambda i: (0, 0))],
        out_specs=(row, roww, roww, roww, roww, roww, roww, roww, roww, row),
        scratch_shapes=[pltpu.VMEM((BF16_ROWS, w), bf16), pltpu.VMEM((SUBLANES, w), f32)],
        compiler_params=_params(("arbitrary",)),
    )(x, vec, wbig, wsm, pvec)


def _lru_bwd(x, dx, saved, vec, wbig, wsm, pvec, layer):
    xr, hs, a_all, mult_all, gr_all, gi_all, gel_all, geld_all, y = saved
    s = x.shape[0]
    sub = min(LRU_BWD_SUB, s)
    ts = min(sub * LRU_BWD_SUBS, s)
    nsub = ts // sub
    nt = s // ts
    w = LRU_WIDTH
    shard = w // N_DEV
    hshard = HEAD_DIM // N_DEV

    def body(x_ref, dx_ref, xr_ref, xrh_ref, hs_ref, hsh_ref, a_ref, mult_ref, gr_ref, gi_ref, gel_ref, geld_ref,
             y_ref, vec_ref, wb_ref, wsm_ref, pv_ref,
             dxo_ref, dwb_ref, dwsm_ref, sm_ref, accb_ref, accs_ref, eps_ref, dxc8_ref,
             hb_scr, dgb_scr, dxrb_scr, mb_scr, dyb_scr, xcb_scr, drab_scr, drxb_scr):
        i = pl.program_id(0)
        first_tile = i == nt - 1

        @pl.when(i == 0)
        def _():
            accb_ref[...] = jnp.zeros_like(accb_ref)
            accs_ref[...] = jnp.zeros_like(accs_ref)
            sm_ref[...] = jnp.zeros_like(sm_ref)
            eps_ref[...] = jnp.zeros_like(eps_ref)
            dxc8_ref[...] = jnp.zeros_like(dxc8_ref)

        gs = vec_ref[R_GS_M:R_GS_M + 1, :]
        c_ls = LRU_C * _log_sigmoid(pv_ref[P_LAM:P_LAM + 1, :])
        for k in reversed(range(nsub)):
            rows = slice(k * sub, (k + 1) * sub)
            xv = x_ref[rows, :]
            dxv = dx_ref[rows, :]
            n, r = _rms(xv)
            hb_scr[rows, :] = (n * gs + vec_ref[R_SH_M:R_SH_M + 1, :]).astype(bf16)
            xrv = xr_ref[rows, :].astype(f32)
            hsv = hs_ref[rows, :]
            if k == 0:
                xr_halo = jnp.where(first_tile, 0.0, xrh_ref[...].astype(f32))
                hs_halo = jnp.where(first_tile, 0.0, hsh_ref[...])
            else:
                xr_halo = xr_ref[k * sub - BF16_ROWS:k * sub, :].astype(f32)
                hs_halo = hs_ref[k * sub - SUBLANES:k * sub, :]
            xs1, xs2, xs3 = _rows_before(xr_halo, xrv, (1, 2, 3))
            xc = _conv(xrv, (xs1, xs2, xs3), pv_ref)
            xcb_scr[rows, :] = xc.astype(bf16)
            a, mult = a_ref[rows, :], mult_ref[rows, :]
            gr, gi = gr_ref[rows, :].astype(f32), gi_ref[rows, :].astype(f32)
            gelu_v = gel_ref[rows, :].astype(f32)

            dy = dxv * vec_ref[R_GT_M:R_GT_M + 1, :]
            dyb = dy.astype(bf16)
            dyb_scr[rows, :] = dyb
            sm_ref[G_GT:G_GT + 1, :] += _colsum(dxv * y_ref[rows, :].astype(f32))
            sm_ref[G_BOUT:G_BOUT + 1, :] += _colsum(dy)
            mb_scr[rows, :] = (hsv * gelu_v).astype(bf16)
            dm = _mm_nt(dyb, wb_ref[2])
            dhs = dm * gelu_v
            dgpre = dm * hsv * geld_ref[rows, :].astype(f32)
            dgb = dgpre.astype(bf16)
            dgb_scr[rows, :] = dgb
            sm_ref[G_BY:G_BY + 1, :] += _colsum(dgpre)

            eps_in = eps_ref[0:1, :]
            eps, eps_out = _scan_rows(a, a * dhs, eps_in, reverse=True)
            eps_ref[0:1, :] = eps_out
            (eps_next,) = _rows_after(eps, jnp.broadcast_to(eps_in, (SUBLANES, w)), (1,))
            delta = dhs + eps_next
            (h_prev,) = _rows_before(hs_halo, hsv, (1,))
            dxi = delta * xc
            dgi = dxi * mult
            dla = (delta * h_prev) * a - (dxi * gi) * (a * a) / mult
            sm_ref[G_LS:G_LS + 1, :] += _colsum(dla * gr)
            dra = (dla * c_ls) * (gr - gr * gr)
            drx = dgi * (gi - gi * gi)
            drab, drxb = dra.astype(bf16), drx.astype(bf16)
            drab_scr[rows, :] = drab
            drxb_scr[rows, :] = drxb
            sm_ref[G_BA:G_BA + 1, :] += _colsum(dra)
            sm_ref[G_BX:G_BX + 1, :] += _colsum(drx)
            dxc = (delta * mult) * gi + _block_diag_t(drab, wsm_ref, 0) + _block_diag_t(drxb, wsm_ref, 1)

            sm_ref[G_CONVB:G_CONVB + 1, :] += _colsum(dxc)
            for kk, v in zip((3, 2, 1, 0), (xrv, xs1, xs2, xs3)):
                sm_ref[G_CW0 + kk:G_CW0 + kk + 1, :] += _colsum(dxc * v)
            ups = _rows_after(dxc, dxc8_ref[...], (1, 2, 3))
            dxc8_ref[...] = dxc[0:SUBLANES, :]
            dxr = dxc * pv_ref[P_CW0 + 3:P_CW0 + 4, :]
            for kk, v in zip((2, 1, 0), ups):
                dxr = dxr + v * pv_ref[P_CW0 + kk:P_CW0 + kk + 1, :]
            dxrb = dxr.astype(bf16)
            dxrb_scr[rows, :] = dxrb
            sm_ref[G_BIN:G_BIN + 1, :] += _colsum(dxr)
            dh = _mm_nt(dgb, wb_ref[0]) + _mm_nt(dxrb, wb_ref[1])
            sm_ref[G_SH:G_SH + 1, :] += _colsum(dh)
            sm_ref[G_GS:G_GS + 1, :] += _colsum(dh * n)
            dxo_ref[rows, :] = dxv + _norm_bwd(dh, n, r, gs)

        hb = hb_scr[...]
        accb_ref[0] += _mm_tn(hb, dgb_scr[...])
        accb_ref[1] += _mm_tn(hb, dxrb_scr[...])
        accb_ref[2] += _mm_tn(mb_scr[...], dyb_scr[...])
        for h in range(HEADS):
            cols = slice(h * HEAD_DIM, (h + 1) * HEAD_DIM)
            accs_ref[0, h] += _mm_tn(xcb_scr[:, cols], drab_scr[:, cols])
            accs_ref[1, h] += _mm_tn(xcb_scr[:, cols], drxb_scr[:, cols])

        @pl.when(i == nt - 1)
        def _():
            sm_ref[G_LS:G_LS + 1, :] = sm_ref[G_LS:G_LS + 1, :] * LRU_C
            for k in range(3):
                dwb_ref[:, k] = accb_ref[k].astype(bf16).reshape(N_DEV, shard, w)
            for k in range(2):
                for h in range(HEADS):
                    dwsm_ref[:, k, h] = accs_ref[k, h].astype(bf16).reshape(N_DEV, hshard, HEAD_DIM)

    rev = lambda i: (nt - 1 - i, 0)
    row = pl.BlockSpec((ts, D_MODEL), rev)
    roww = pl.BlockSpec((ts, w), rev)
    halo16 = pl.BlockSpec((BF16_ROWS, w), lambda i: (jnp.maximum((nt - 1 - i) * (ts // BF16_ROWS) - 1, 0), 0))
    halo8 = pl.BlockSpec((SUBLANES, w), lambda i: (jnp.maximum((nt - 1 - i) * (ts // SUBLANES) - 1, 0), 0))
    const = lambda *shape: pl.BlockSpec(shape, lambda i: (0,) * len(shape))
    operand = pltpu.VMEM((ts, w), bf16)
    return pl.pallas_call(
        body, name=f"lru_bwd_{layer}", grid=(nt,),
        out_shape=(jax.ShapeDtypeStruct((s, D_MODEL), f32),
                   jax.ShapeDtypeStruct((N_DEV, 3, shard, w), bf16),
                   jax.ShapeDtypeStruct((N_DEV, 2, HEADS, hshard, HEAD_DIM), bf16),
                   jax.ShapeDtypeStruct((16, w), f32)),
        in_specs=[row, row, roww, halo16, roww, halo8, roww, roww, roww, roww, roww, roww, row, const(8, D_MODEL),
                  const(3, w, w), const(2, HEADS, HEAD_DIM, HEAD_DIM), const(16, w)],
        out_specs=(row, const(N_DEV, 3, shard, w), const(N_DEV, 2, HEADS, hshard, HEAD_DIM), const(16, w)),
        scratch_shapes=[pltpu.VMEM((3, w, w), f32), pltpu.VMEM((2, HEADS, HEAD_DIM, HEAD_DIM), f32),
                        pltpu.VMEM((SUBLANES, w), f32), pltpu.VMEM((SUBLANES, w), f32)] + [operand] * 8,
        compiler_params=_params(("arbitrary",), 58),
    )(x, dx, xr, xr, hs, hs, a_all, mult_all, gr_all, gi_all, gel_all, geld_all, y, vec, wbig, wsm, pvec)


def _pool_tile(s):
    return min(1024, s)


def _pool_counts(tile_index, ts):
    t = (tile_index * ts + lax.broadcasted_iota(jnp.int32, (ts, 1), 0) + 1).astype(f32)
    return [1.0 / jnp.minimum(t, float(win)) for win in POOL_WINDOWS]


def _pooled(h, halo, inv):
    ext = jnp.concatenate([halo, h], axis=0)
    out = []
    for g in range(len(POOL_WINDOWS)):
        acc = ext[:, g * HEAD_DIM:(g + 1) * HEAD_DIM]
        for step in range(g + 1):
            acc = acc + pltpu.roll(acc, 1 << step, 0)
        out.append(acc[POOL_HALO:] * inv[g] - h[:, g * HEAD_DIM:(g + 1) * HEAD_DIM])
    return out


def _pool_fwd(x, vec, pw, ps, layer):
    s = x.shape[0]
    ts = _pool_tile(s)

    def body(x_ref, vec_ref, pw_ref, ps_ref, xo_ref, y_ref, halo_ref):
        i = pl.program_id(0)

        @pl.when(i == 0)
        def _():
            halo_ref[...] = jnp.zeros_like(halo_ref)

        xv = x_ref[...]
        n, _ = _rms(xv)
        h = n * vec_ref[R_GS_M:R_GS_M + 1, :] + vec_ref[R_SH_M:R_SH_M + 1, :]
        pooled = _pooled(h, halo_ref[...], _pool_counts(i, ts))
        halo_ref[...] = h[ts - POOL_HALO:, :]
        mixed = jnp.concatenate([_mm(pooled[g].astype(bf16), pw_ref[g]) for g in range(HEADS)], axis=1)
        yv = mixed * ps_ref[0:1, :]
        y_ref[...] = yv.astype(bf16)
        xo_ref[...] = xv + vec_ref[R_GT_M:R_GT_M + 1, :] * yv

    row = pl.BlockSpec((ts, D_MODEL), lambda i: (i, 0))
    return pl.pallas_call(
        body, name=f"pool_fwd_{layer}", grid=(s // ts,),
        out_shape=(jax.ShapeDtypeStruct((s, D_MODEL), f32), jax.ShapeDtypeStruct((s, D_MODEL), bf16)),
        in_specs=[row, pl.BlockSpec((8, D_MODEL), lambda i: (0, 0)),
                  pl.BlockSpec((HEADS, HEAD_DIM, HEAD_DIM), lambda i: (0, 0, 0)),
                  pl.BlockSpec((8, D_MODEL), lambda i: (0, 0))],
        out_specs=(row, row),
        scratch_shapes=[pltpu.VMEM((POOL_HALO, D_MODEL), f32)],
        compiler_params=_params(("arbitrary",)),
    )(x, vec, pw, ps)


def _pool_bwd(x, dx, y, vec, pw, ps, layer):
    s = x.shape[0]
    ts = _pool_tile(s)
    nt = s // ts
    hshard = HEAD_DIM // N_DEV

    def body(x_ref, xh_ref, dx_ref, y_ref, vec_ref, pw_ref, ps_ref, dxo_ref, dpw_ref, sm_ref, acc_ref, q16_ref):
        i = pl.program_id(0)
        tile = nt - 1 - i

        @pl.when(i == 0)
        def _():
            acc_ref[...] = jnp.zeros_like(acc_ref)
            sm_ref[...] = jnp.zeros_like(sm_ref)
            q16_ref[...] = jnp.zeros_like(q16_ref)

        gs, sh = vec_ref[R_GS_M:R_GS_M + 1, :], vec_ref[R_SH_M:R_SH_M + 1, :]
        xv = x_ref[...]
        dxv = dx_ref[...]
        n, r = _rms(xv)
        h = n * gs + sh
        nh, _ = _rms(xh_ref[...])
        halo = jnp.where(tile == 0, 0.0, nh * gs + sh)
        inv = _pool_counts(tile, ts)
        pooled = _pooled(h, halo, inv)
        mixed = jnp.concatenate([_mm(pooled[g].astype(bf16), pw_ref[g]) for g in range(HEADS)], axis=1)

        dy = dxv * vec_ref[R_GT_M:R_GT_M + 1, :]
        sm_ref[G_GT:G_GT + 1, :] += _colsum(dxv * y_ref[...].astype(f32))
        sm_ref[3:4, :] += _colsum(dy * mixed)
        dmixed = (dy * ps_ref[0:1, :]).astype(bf16)
        dh_parts = []
        for g in range(HEADS):
            cols = slice(g * HEAD_DIM, (g + 1) * HEAD_DIM)
            acc_ref[g] += _mm_tn(pooled[g].astype(bf16), dmixed[:, cols])
            dpooled = _mm_nt(dmixed[:, cols], pw_ref[g])
            q = dpooled * inv[g]
            ext = jnp.concatenate([q, q16_ref[:, cols]], axis=0)
            q16_ref[:, cols] = q[0:POOL_HALO, :]
            for step in range(g + 1):
                ext = ext + pltpu.roll(ext, ext.shape[0] - (1 << step), 0)
            dh_parts.append(ext[:ts] - dpooled)
        dh = jnp.concatenate(dh_parts, axis=1)
        sm_ref[G_SH:G_SH + 1, :] += _colsum(dh)
        sm_ref[G_GS:G_GS + 1, :] += _colsum(dh * n)
        dxo_ref[...] = dxv + _norm_bwd(dh, n, r, gs)

        @pl.when(i == nt - 1)
        def _():
            for g in range(HEADS):
                dpw_ref[:, g] = acc_ref[g].astype(bf16).reshape(N_DEV, hshard, HEAD_DIM)

    rev = lambda i: (nt - 1 - i, 0)
    row = pl.BlockSpec((ts, D_MODEL), rev)
    halo16 = pl.BlockSpec((POOL_HALO, D_MODEL), lambda i: (jnp.maximum((nt - 1 - i) * (ts // POOL_HALO) - 1, 0), 0))
    const = lambda *shape: pl.BlockSpec(shape, lambda i: (0,) * len(shape))
    return pl.pallas_call(
        body, name=f"pool_bwd_{layer}", grid=(nt,),
        out_shape=(jax.ShapeDtypeStruct((s, D_MODEL), f32),
                   jax.ShapeDtypeStruct((N_DEV, HEADS, hshard, HEAD_DIM), bf16),
                   jax.ShapeDtypeStruct((8, D_MODEL), f32)),
        in_specs=[row, halo16, row, row, const(8, D_MODEL), const(HEADS, HEAD_DIM, HEAD_DIM), const(8, D_MODEL)],
        out_specs=(row, const(N_DEV, HEADS, hshard, HEAD_DIM), const(8, D_MODEL)),
        scratch_shapes=[pltpu.VMEM((HEADS, HEAD_DIM, HEAD_DIM), f32), pltpu.VMEM((POOL_HALO, D_MODEL), f32)],
        compiler_params=_params(("arbitrary",)),
    )(x, x, dx, y, vec, pw, ps)


def _final(x, target, g_fin):
    s = x.shape[0]
    ts = min(1024, s)

    def body(x_ref, t_ref, g_ref, dx_ref, sm_ref):
        @pl.when(pl.program_id(0) == 0)
        def _():
            sm_ref[...] = jnp.zeros_like(sm_ref)

        g = g_ref[0:1, :]
        n, r = _rms(x_ref[...])
        err = n * g - t_ref[...]
        sm_ref[1:2, :] += 0.5 * jnp.sum(jnp.mean(err * err, axis=-1, keepdims=True), axis=0, keepdims=True)
        dyv = err * (1.0 / D_MODEL)
        sm_ref[0:1, :] += _colsum(dyv * n)
        dx_ref[...] = _norm_bwd(dyv, n, r, g)

    row = pl.BlockSpec((ts, D_MODEL), lambda i: (i, 0))
    return pl.pallas_call(
        body, name="final_loss", grid=(s // ts,),
        out_shape=(jax.ShapeDtypeStruct((s, D_MODEL), f32), jax.ShapeDtypeStruct((8, D_MODEL), f32)),
        in_specs=[row, row, pl.BlockSpec((8, D_MODEL), lambda i: (0, 0))],
        out_specs=(row, pl.BlockSpec((8, D_MODEL), lambda i: (0, 0))),
        compiler_params=_params(("arbitrary",)),
    )(x, target, g_fin)


def _small_pack(sm_ffn, sm_mix, sm_fin, table, g_mix, g_ffn, lam):
    def body(*refs):
        ffn, mix = refs[0:DEPTH], refs[DEPTH:2 * DEPTH]
        fin_ref, tab_ref, gm_ref, gf_ref, lam_ref, o_ref = refs[2 * DEPTH:]
        o_ref[...] = jnp.zeros_like(o_ref)
        for i in range(DEPTH):
            base = K_MOD + i * N_MOD
            o_ref[base + 0:base + 1, :] = mix[i][G_SH:G_SH + 1, :]
            o_ref[base + 1:base + 2, :] = mix[i][G_GS:G_GS + 1, :] * gm_ref[i:i + 1, :]
            o_ref[base + 2:base + 3, :] = mix[i][G_GT:G_GT + 1, :]
            o_ref[base + 3:base + 4, :] = ffn[i][G_SH:G_SH + 1, :]
            o_ref[base + 4:base + 5, :] = ffn[i][G_GS:G_GS + 1, :] * gf_ref[i:i + 1, :]
            o_ref[base + 5:base + 6, :] = ffn[i][G_GT:G_GT + 1, :]
            o_ref[K_NMIX + i:K_NMIX + i + 1, :] = mix[i][G_GS:G_GS + 1, :] * (1.0 + tab_ref[i, R_SC_M:R_SC_M + 1, :])
            o_ref[K_NFFN + i:K_NFFN + i + 1, :] = ffn[i][G_GS:G_GS + 1, :] * (1.0 + tab_ref[i, R_SC_F:R_SC_F + 1, :])
            j = i // 2
            if i % 2 == 0:
                for k, src in enumerate((G_BY, G_BIN, G_CONVB, None, G_BOUT)):
                    dst = K_LRUB + j * 5 + k
                    if src is None:
                        o_ref[dst:dst + 1, :] = mix[i][G_LS:G_LS + 1, :] * _sigmoid(-lam_ref[j:j + 1, :])
                    else:
                        o_ref[dst:dst + 1, :] = mix[i][src:src + 1, :]
                o_ref[K_CONVW + j * 4:K_CONVW + j * 4 + 4, :] = mix[i][G_CW0:G_CW0 + 4, :]
                o_ref[K_BA + j:K_BA + j + 1, :] = mix[i][G_BA:G_BA + 1, :]
                o_ref[K_BX + j:K_BX + j + 1, :] = mix[i][G_BX:G_BX + 1, :]
            else:
                o_ref[K_PS + j:K_PS + j + 1, :] = mix[i][3:4, :]
        o_ref[K_FIN:K_FIN + 2, :] = fin_ref[0:2, :]

    return pl.pallas_call(body, name="small_pack", out_shape=jax.ShapeDtypeStruct((K_ROWS, D_MODEL), f32))(
        *sm_ffn, *sm_mix, sm_fin, table, g_mix, g_ffn, lam)


def _small_sum(gathered):
    def body(g_ref, o_ref, token_ref):
        tot = g_ref[0]
        for src in range(1, N_DEV):
            tot = tot + g_ref[src]
        o_ref[...] = tot
        token_ref[...] = jnp.zeros_like(token_ref)

    return pl.pallas_call(
        body, name="small_sum",
        out_shape=(jax.ShapeDtypeStruct(gathered.shape[1:], f32), jax.ShapeDtypeStruct((8, 128), f32)))(gathered)


def _adamw_math(g, w, m, v):
    m = ADAM_B1 * m + (1.0 - ADAM_B1) * g
    v = ADAM_B2 * v + (1.0 - ADAM_B2) * (g * g)
    m_hat = m / (1.0 - ADAM_B1 ** ADAM_STEP)
    v_hat = v / (1.0 - ADAM_B2 ** ADAM_STEP)
    delta = -ADAM_LR * (m_hat / (jnp.sqrt(v_hat) + ADAM_EPS) + ADAM_WD * w)
    return delta, m, v


def _adamw_small(name, g, w, m, v):
    shape = w.shape
    two_d = (1, shape[0]) if len(shape) == 1 else (math.prod(shape[:-1]), shape[-1])

    def body(g_ref, w_ref, m_ref, v_ref, d_ref, mo_ref, vo_ref):
        d_ref[...], mo_ref[...], vo_ref[...] = _adamw_math(g_ref[...], w_ref[...], m_ref[...], v_ref[...])

    outs = pl.pallas_call(body, name=f"adamw_{name}", out_shape=tuple(jax.ShapeDtypeStruct(two_d, f32) for _ in range(3)))(
        *(t.reshape(two_d) for t in (g, w, m, v)))
    return tuple(t.reshape(shape) for t in outs)


def _block_rows(rows, cols):
    tr = max(SUBLANES, min(rows, (512 * 1024) // (4 * cols)))
    while rows % tr:
        tr //= 2
    return tr


def _adamw_reduce(name, landings, kind, w, m, v):
    nl = len(landings)
    rows, cols = landings[0].shape[2:]
    tr = _block_rows(rows, cols)
    per_layer = rows // tr

    def body(*refs):
        l_refs = refs[:nl]
        w_ref, m_ref, v_ref, g_ref, d_ref, mo_ref, vo_ref = refs[nl:]
        layer = pl.program_id(0)
        for k in range(nl):
            @pl.when(layer == k)
            def _(k=k):
                g = l_refs[k][0].astype(f32)
                for src in range(1, N_DEV):
                    g = g + l_refs[k][src].astype(f32)
                g_ref[...] = g
        d_ref[...], mo_ref[...], vo_ref[...] = _adamw_math(g_ref[...], w_ref[...], m_ref[...], v_ref[...])

    blk = pl.BlockSpec((tr, cols), lambda l, r: (l * per_layer + r, 0))
    land = [pl.BlockSpec((N_DEV, None, tr, cols), lambda l, r, k=k: (0, kind, jnp.where(l == k, r, 0), 0)) for k in range(nl)]
    return pl.pallas_call(
        body, name=f"adamw_{name}", grid=(nl, per_layer),
        out_shape=tuple(jax.ShapeDtypeStruct((nl * rows, cols), f32) for _ in range(4)),
        in_specs=land + [blk, blk, blk],
        out_specs=(blk, blk, blk, blk),
        compiler_params=_params(("arbitrary", "arbitrary"), 32),
    )(*landings, w, m, v)


def _adamw_w_mod(c_all, dmod_all, w, m, v):
    depth, d, cols = w.shape
    tr = 256

    def body(c_ref, dm_ref, w_ref, m_ref, v_ref, g_ref, d_ref, mo_ref, vo_ref):
        cv = c_ref[...]
        cond = cv * _sigmoid(cv)
        g = lax.dot_general(cond, dm_ref[...], (((0,), (0,)), ((), ())), preferred_element_type=f32,
                            precision=lax.Precision.HIGHEST)
        g_ref[...] = g
        d_ref[...], mo_ref[...], vo_ref[...] = _adamw_math(g, w_ref[...], m_ref[...], v_ref[...])

    blk = pl.BlockSpec((None, tr, cols), lambda i, r: (i, r, 0))
    return pl.pallas_call(
        body, name="adamw_w_mod", grid=(depth, d // tr),
        out_shape=tuple(jax.ShapeDtypeStruct(w.shape, f32) for _ in range(4)),
        in_specs=[pl.BlockSpec((N_DEV, tr), lambda i, r: (0, r)),
                  pl.BlockSpec((None, N_DEV, cols), lambda i, r: (i, 0, 0)), blk, blk, blk],
        out_specs=(blk, blk, blk, blk),
        compiler_params=_params(("arbitrary", "arbitrary"), 32),
    )(c_all, dmod_all, w, m, v)


def kernel(x, c, w_mod, b_mod, norm_mix_g, norm_ffn_g, lru_w_y, lru_b_y, lru_w_in, lru_b_in, lru_conv_w, lru_conv_b, lru_w_a, lru_b_a, lru_w_x, lru_b_x, lru_lambda, lru_w_out, lru_b_out, pool_w, pool_scale, ffn_w1, ffn_w2, final_norm_g, loss_target, m_w_mod, m_b_mod, m_norm_mix_g, m_norm_ffn_g, m_lru_w_y, m_lru_b_y, m_lru_w_in, m_lru_b_in, m_lru_conv_w, m_lru_conv_b, m_lru_w_a, m_lru_b_a, m_lru_w_x, m_lru_b_x, m_lru_lambda, m_lru_w_out, m_lru_b_out, m_pool_w, m_pool_scale, m_ffn_w1, m_ffn_w2, m_final_norm_g, v_w_mod, v_b_mod, v_norm_mix_g, v_norm_ffn_g, v_lru_w_y, v_lru_b_y, v_lru_w_in, v_lru_b_in, v_lru_conv_w, v_lru_conv_b, v_lru_w_a, v_lru_b_a, v_lru_w_x, v_lru_b_x, v_lru_lambda, v_lru_w_out, v_lru_b_out, v_pool_w, v_pool_scale, v_ffn_w1, v_ffn_w2, v_final_norm_g):
    me = 4 * lax.axis_index("x") + 2 * lax.axis_index("y") + lax.axis_index("c")
    n_lru = lru_w_y.shape[0]
    shard = LRU_WIDTH // N_DEV
    hshard = HEAD_DIM // N_DEV
    xs = x[0]
    target = loss_target[0]

    small_vecs = jnp.concatenate([
        lru_conv_w.reshape(n_lru * 4, shard), lru_b_a.reshape(n_lru, HEADS * hshard),
        lru_b_x.reshape(n_lru, HEADS * hshard), pool_scale, jnp.zeros((2, shard), f32)], axis=0)
    sv_g, c_g = _exchange([small_vecs, c], True, "gather_cond")
    conv_w_full = sv_g[:, 0:8].reshape(N_DEV, n_lru, 4, shard).transpose(1, 2, 0, 3).reshape(n_lru, 4, LRU_WIDTH)
    b_a_full = sv_g[:, 8:10].reshape(N_DEV, n_lru, HEADS, hshard).transpose(1, 2, 0, 3).reshape(n_lru, LRU_WIDTH)
    b_x_full = sv_g[:, 10:12].reshape(N_DEV, n_lru, HEADS, hshard).transpose(1, 2, 0, 3).reshape(n_lru, LRU_WIDTH)
    ps_full = sv_g[:, 12:14].transpose(1, 0, 2).reshape(n_lru, D_MODEL)
    c_all = c_g.reshape(N_DEV, D_MODEL)

    (mod_g,) = _exchange([_mod_part(c_all, w_mod)], True, "gather_mod", pieces=DEPTH)
    mod_row = lax.dynamic_index_in_dim(mod_g, me, axis=2, keepdims=False)
    mod_row = mod_row.transpose(1, 0, 2).reshape(DEPTH, N_MOD * D_MODEL)
    table, token = _mod_table(mod_row, b_mod, norm_mix_g, norm_ffn_g)

    first_pieces = 4
    (first_mix,), token = _send_start("gather_first_start", [[
        jnp.stack([lru_w_y[0], lru_w_in[0], lru_w_out[0]]).astype(bf16).reshape(3 * first_pieces, -1, LRU_WIDTH),
        jnp.stack([lru_w_a[0], lru_w_x[0]]).astype(bf16).reshape(first_pieces, -1, HEAD_DIM)]], True, me,
        pieces=first_pieces, after=token)

    parts = []
    for i in range(DEPTH):
        j = i // 2
        if i > 0 and i % 2 == 0:
            parts.append([(jnp.stack([lru_w_y[j], lru_w_in[j], lru_w_out[j]]) + token[0, 0]).astype(bf16),
                          (jnp.stack([lru_w_a[j], lru_w_x[j]]) + token[0, 0]).astype(bf16)])
        elif i % 2 == 1:
            parts.append([(pool_w[j] + token[0, 0]).astype(bf16)])
        parts.append([(ffn_w1[i] + token[0, 0]).astype(bf16), (ffn_w2[i] + token[0, 0]).astype(bf16)])
    first_got, token = _send_wait("gather_mix_wait_0", first_mix, [a for part in parts for a in part])
    handles, token = _send_start("gather_rest_start", parts, True, me, after=token)
    h_ffn = [handles[0], handles[2], handles[4], handles[6]]
    h_mix = [None, handles[1], handles[3], handles[5]]

    zero_row = jnp.zeros((1, LRU_WIDTH), f32)
    pvecs = [jnp.concatenate([lru_b_y[j:j + 1], lru_b_in[j:j + 1], lru_conv_b[j:j + 1], b_a_full[j:j + 1],
                              b_x_full[j:j + 1], lru_lambda[j:j + 1], lru_b_out[j:j + 1], zero_row,
                              conv_w_full[j], zero_row, zero_row, zero_row, zero_row], axis=0) for j in range(n_lru)]
    ps_rows = [jnp.concatenate([ps_full[j:j + 1], jnp.zeros((7, D_MODEL), f32)], axis=0) for j in range(n_lru)]

    saved = []
    ffn_w, mix_w = [], []
    h = xs
    for i in range(DEPTH):
        j = i // 2
        got = first_got if i == 0 else _send_wait(f"gather_mix_wait_{i}", h_mix[i], h)[0]
        if i % 2 == 0:
            got = [got[0].reshape(N_DEV, 3, shard, LRU_WIDTH), got[1].reshape(N_DEV, 2, HEADS, hshard, HEAD_DIM)]
            mix_w.append((got[0].transpose(1, 0, 2, 3).reshape(3, LRU_WIDTH, LRU_WIDTH),
                          got[1].transpose(1, 2, 0, 3, 4).reshape(2, HEADS, HEAD_DIM, HEAD_DIM)))
            h_mid, *lru_saved = _lru_fwd(h, table[i] + token[0, 0], mix_w[i][0], mix_w[i][1], pvecs[j], i)
            mix_saved = (h, tuple(lru_saved))
        else:
            mix_w.append((got[0].transpose(1, 0, 2, 3).reshape(HEADS, HEAD_DIM, HEAD_DIM),))
            h_mid, y_mix = _pool_fwd(h, table[i], mix_w[i][0], ps_rows[j], i)
            mix_saved = (h, y_mix)
        ffn_w.append(_send_wait(f"gather_ffn_wait_{i}", h_ffn[i], h_mid)[0])
        h_out, u, y_ffn, hb = _ffn_fwd(h_mid, table[i], ffn_w[i][0], ffn_w[i][1], i)
        saved.append((mix_saved, (h_mid, u, y_ffn, hb)))
        h = h_out
    fin_rows = jnp.concatenate([final_norm_g[None, :], jnp.zeros((7, D_MODEL), f32)], axis=0)
    dx, sm_fin = _final(h, target, fin_rows)

    sm_ffn, sm_mix = [None] * DEPTH, [None] * DEPTH
    x_ffn, x_mix = [None] * DEPTH, [None] * DEPTH
    token = jnp.zeros((8, 128), f32)
    last_mix = None
    for i in reversed(range(DEPTH)):
        j = i // 2
        mix_saved, (h_mid, u, y_ffn, hb) = saved[i]
        dx, da, dyb, sm_ffn[i] = _ffn_bwd_act(h_mid, dx, u, y_ffn, table[i] + token[0, 0], ffn_w[i][0], ffn_w[i][1], i)
        ffn_grads = [_ffn_bwd_w1(hb, da, i), _ffn_bwd_w2(u, dyb, i)]
        if last_mix is None:
            (x_ffn[i],), token = _send_start(f"grads_start_{i}", [ffn_grads], False, me)
        else:
            (x_mix[i + 1], x_ffn[i]), token = _send_start(f"grads_start_{i}", [last_mix, ffn_grads], False, me)
        if i % 2 == 0:
            h_in, lru_saved = mix_saved
            dx, dbig, dsmall, sm_mix[i] = _lru_bwd(
                h_in, dx, lru_saved, table[i] + token[0, 0], mix_w[i][0], mix_w[i][1], pvecs[j], i)
            last_mix = [dbig, dsmall]
        else:
            h_in, y_mix = mix_saved
            dx, dpool, sm = _pool_bwd(h_in, dx, y_mix, table[i] + token[0, 0], mix_w[i][0], ps_rows[j], i)
            sm_mix[i] = jnp.concatenate([sm, jnp.zeros((8, D_MODEL), f32)], axis=0)
            last_mix = [dpool]
    grad_x = dx[None]

    pack = _small_pack(sm_ffn, sm_mix, sm_fin, table + token[0, 0], norm_mix_g, norm_ffn_g, lru_lambda)
    (pack_g,) = _exchange([pack], True, "gather_small_grads", pieces=4)
    tot, token = _small_sum(pack_g)
    loss = tot[K_LOSS, 0]
    (x_mix[0],), _ = _send_start("grads_last_start", [[t + token[0, 0].astype(bf16) for t in last_mix]], False, me)
    cols = w_mod.shape[2]
    dmod_all = lax.dynamic_slice_in_dim(pack_g[:, K_MOD:K_MOD + DEPTH * N_MOD].reshape(N_DEV, DEPTH, N_MOD * D_MODEL),
                                        me * cols, cols, axis=2).transpose(1, 0, 2)
    results = {"w_mod": _adamw_w_mod(c_all, dmod_all, w_mod, m_w_mod, v_w_mod)}

    after = results["w_mod"][1]
    l_ffn = [_send_wait(f"grads_ffn_wait_{i}", x_ffn[i], after)[0] for i in reversed(range(DEPTH))][::-1]

    def reduce_update(name, landings, kind, w, m, v):
        rows = w.size // w.shape[-1]
        two_d = (rows, w.shape[-1])
        lands = [t.reshape(N_DEV, -1, rows // len(landings), w.shape[-1]) for t in landings]
        outs = _adamw_reduce(name, lands, kind, w.reshape(two_d), m.reshape(two_d), v.reshape(two_d))
        return tuple(t.reshape(w.shape) for t in outs)

    results["ffn_w1"] = reduce_update("ffn_w1", [t[0] for t in l_ffn], 0, ffn_w1, m_ffn_w1, v_ffn_w1)
    results["ffn_w2"] = reduce_update("ffn_w2", [t[1] for t in l_ffn], 0, ffn_w2, m_ffn_w2, v_ffn_w2)
    after = results["ffn_w2"][1]
    l_mix = [_send_wait(f"grads_mix_wait_{i}", x_mix[i], after)[0] for i in reversed(range(DEPTH))][::-1]
    l_lru_big = [l_mix[i][0] for i in range(0, DEPTH, 2)]
    l_lru_small = [l_mix[i][1] for i in range(0, DEPTH, 2)]
    l_pool = [l_mix[i][0] for i in range(1, DEPTH, 2)]
    results["lru_w_y"] = reduce_update("lru_w_y", l_lru_big, 0, lru_w_y, m_lru_w_y, v_lru_w_y)
    results["lru_w_in"] = reduce_update("lru_w_in", l_lru_big, 1, lru_w_in, m_lru_w_in, v_lru_w_in)
    results["lru_w_out"] = reduce_update("lru_w_out", l_lru_big, 2, lru_w_out, m_lru_w_out, v_lru_w_out)
    results["lru_w_a"] = reduce_update("lru_w_a", l_lru_small, 0, lru_w_a, m_lru_w_a, v_lru_w_a)
    results["lru_w_x"] = reduce_update("lru_w_x", l_lru_small, 1, lru_w_x, m_lru_w_x, v_lru_w_x)
    results["pool_w"] = reduce_update("pool_w", l_pool, 0, pool_w, m_pool_w, v_pool_w)

    def my_cols(full, width):
        return lax.dynamic_slice_in_dim(full, me * width, width, axis=full.ndim - 1)

    lru_rows = tot[K_LRUB:K_LRUB + 5 * n_lru].reshape(n_lru, 5, LRU_WIDTH)
    small_grads = {
        "b_mod": tot[K_MOD:K_MOD + DEPTH * N_MOD].reshape(DEPTH, N_MOD * D_MODEL),
        "norm_mix_g": tot[K_NMIX:K_NMIX + DEPTH],
        "norm_ffn_g": tot[K_NFFN:K_NFFN + DEPTH],
        "lru_b_y": lru_rows[:, 0], "lru_b_in": lru_rows[:, 1], "lru_conv_b": lru_rows[:, 2],
        "lru_lambda": lru_rows[:, 3], "lru_b_out": lru_rows[:, 4],
        "lru_conv_w": my_cols(tot[K_CONVW:K_CONVW + 4 * n_lru].reshape(n_lru, 4, LRU_WIDTH), shard),
        "lru_b_a": my_cols(tot[K_BA:K_BA + n_lru].reshape(n_lru, HEADS, HEAD_DIM), hshard),
        "lru_b_x": my_cols(tot[K_BX:K_BX + n_lru].reshape(n_lru, HEADS, HEAD_DIM), hshard),
        "pool_scale": my_cols(tot[K_PS:K_PS + n_lru], shard),
        "final_norm_g": tot[K_FIN],
    }
    given = dict(b_mod=(b_mod, m_b_mod, v_b_mod), norm_mix_g=(norm_mix_g, m_norm_mix_g, v_norm_mix_g),
                 norm_ffn_g=(norm_ffn_g, m_norm_ffn_g, v_norm_ffn_g), lru_b_y=(lru_b_y, m_lru_b_y, v_lru_b_y),
                 lru_b_in=(lru_b_in, m_lru_b_in, v_lru_b_in), lru_conv_w=(lru_conv_w, m_lru_conv_w, v_lru_conv_w),
                 lru_conv_b=(lru_conv_b, m_lru_conv_b, v_lru_conv_b), lru_b_a=(lru_b_a, m_lru_b_a, v_lru_b_a),
                 lru_b_x=(lru_b_x, m_lru_b_x, v_lru_b_x), lru_lambda=(lru_lambda, m_lru_lambda, v_lru_lambda),
                 lru_b_out=(lru_b_out, m_lru_b_out, v_lru_b_out), pool_scale=(pool_scale, m_pool_scale, v_pool_scale),
                 final_norm_g=(final_norm_g, m_final_norm_g, v_final_norm_g))
    for name, g in small_grads.items():
        results[name] = (g,) + _adamw_small(name, g, *given[name])

    order = ["w_mod", "b_mod", "norm_mix_g", "norm_ffn_g", "lru_w_y", "lru_b_y", "lru_w_in", "lru_b_in", "lru_conv_w",
             "lru_conv_b", "lru_w_a", "lru_b_a", "lru_w_x", "lru_b_x", "lru_lambda", "lru_w_out", "lru_b_out", "pool_w",
             "pool_scale", "ffn_w1", "ffn_w2", "final_norm_g"]
    return (loss, grad_x, *[results[n][0] for n in order], *[results[n][1] for n in order],
            *[results[n][2] for n in order], *[results[n][3] for n in order])
```

```python
import math

import jax
import jax.numpy as jnp
from jax import lax
from jax.experimental import pallas as pl
from jax.experimental.pallas import tpu as pltpu

f32, bf16 = jnp.float32, jnp.bfloat16

D_MODEL = 1024
LRU_WIDTH = 1024
HEADS = 4
HEAD_DIM = 256
D_FF = 4096
DEPTH = 4
N_MOD = 6
N_DEV = 8
FF_CHUNK = D_FF // N_DEV
POOL_WINDOWS = (2, 4, 8, 16)
POOL_HALO = 16
EPS = 1e-6
LRU_C = 8.0

ADAM_LR = 0.001
ADAM_B1 = 0.9
ADAM_B2 = 0.999
ADAM_EPS = 1e-08
ADAM_WD = 0.01
ADAM_STEP = 10

SUBLANES = 8
BF16_ROWS = 16

R_SH_M, R_SC_M, R_GT_M, R_SH_F, R_SC_F, R_GT_F, R_GS_M, R_GS_F = range(8)
P_BY, P_BIN, P_CONVB, P_BA, P_BX, P_LAM, P_BOUT, P_CW0 = 0, 1, 2, 3, 4, 5, 6, 8
G_SH, G_GS, G_GT, G_BY, G_BIN, G_CONVB, G_BA, G_BX, G_LS, G_BOUT, G_CW0 = 0, 1, 2, 3, 4, 5, 6, 7, 8, 9, 10
K_MOD, K_NMIX, K_NFFN, K_LRUB, K_CONVW, K_BA, K_BX, K_PS, K_FIN, K_LOSS, K_ROWS = 0, 24, 28, 32, 42, 50, 52, 54, 56, 57, 64


def _params(semantics=None, vmem_mb=56):
    return pltpu.CompilerParams(dimension_semantics=semantics, vmem_limit_bytes=vmem_mb * 1024 * 1024)


def _mm(a, b):
    return jnp.dot(a, b, preferred_element_type=f32)


def _mm_nt(a, b):
    return lax.dot_general(a, b, (((1,), (1,)), ((), ())), preferred_element_type=f32)


def _mm_tn(a, b):
    return lax.dot_general(a, b, (((0,), (0,)), ((), ())), preferred_element_type=f32)


def _rms(x):
    r = lax.rsqrt(jnp.mean(x * x, axis=-1, keepdims=True) + EPS)
    return x * r, r


def _norm_bwd(dh, n, r, gs):
    dn = dh * gs
    return r * (dn - n * jnp.mean(dn * n, axis=-1, keepdims=True))


def _colsum(v):
    return jnp.sum(v, axis=0, keepdims=True)


def _sigmoid(v):
    return 0.5 * jnp.tanh(0.5 * v) + 0.5


def _log_sigmoid(v):
    return jnp.minimum(v, 0.0) - jnp.log1p(jnp.exp(-jnp.abs(v)))


_GELU_C = 0.7978845608028654
_GELU_A = 0.044715


def _gelu_and_grad(v):
    v2 = v * v
    t = jnp.tanh(_GELU_C * v * (1.0 + _GELU_A * v2))
    p = 0.5 + 0.5 * t
    return v * p, p + (0.5 * v) * (1.0 - t * t) * (_GELU_C + (3.0 * _GELU_A * _GELU_C) * v2)


def _rows_before(halo, v, shifts):
    hr = halo.shape[0]
    ext = jnp.concatenate([halo, v], axis=0)
    return [pltpu.roll(ext, k, 0)[hr:] for k in shifts]


def _rows_after(v, halo, shifts):
    n = v.shape[0]
    ext = jnp.concatenate([v, halo], axis=0)
    return [pltpu.roll(ext, ext.shape[0] - k, 0)[:n] for k in shifts]


def _shift_matrix(n, halo_rows, shifts):
    rows = lax.broadcasted_iota(jnp.int32, (n, n + halo_rows), 0)
    cols = lax.broadcasted_iota(jnp.int32, (n, n + halo_rows), 1)
    return jnp.concatenate([(cols == rows + halo_rows - k).astype(bf16) for k in shifts], axis=0)


def _shifted_rows(sel, halo, v):
    n = v.shape[0]
    out = _mm(sel, jnp.concatenate([halo, v], axis=0))
    return [out[j * n:(j + 1) * n] for j in range(sel.shape[0] // n)]


def _block_diag(v, w_ref, kind):
    return jnp.concatenate(
        [_mm(v[:, h * HEAD_DIM:(h + 1) * HEAD_DIM], w_ref[kind, h]) for h in range(HEADS)], axis=1)


def _block_diag_t(v, w_ref, kind):
    return jnp.concatenate(
        [_mm_nt(v[:, h * HEAD_DIM:(h + 1) * HEAD_DIM], w_ref[kind, h]) for h in range(HEADS)], axis=1)


def _exchange(arrays, gather, name, pieces=1):
    n = len(arrays)
    peers = N_DEV - 1

    def body(*refs):
        ins, outs = refs[:n], refs[n:2 * n]
        send_sems, recv_sems, local_sems = refs[2 * n:]
        x, y, c = lax.axis_index("x"), lax.axis_index("y"), lax.axis_index("c")
        me = 4 * x + 2 * y + c
        local = []
        for k in range(n):
            cp = pltpu.make_async_copy(ins[k] if gather else ins[k].at[me], outs[k].at[me], local_sems.at[k])
            cp.start()
            local.append(cp)
        remote = _peer_copies(ins, outs, send_sems, recv_sems, gather, pieces)
        for cp in remote:
            cp.start()
        for cp in remote:
            cp.wait()
        for cp in local:
            cp.wait()

    out_shape = tuple(
        jax.ShapeDtypeStruct(((N_DEV,) + a.shape) if gather else a.shape, a.dtype) for a in arrays)
    outs = pl.pallas_call(
        body, name=name, out_shape=out_shape,
        in_specs=[pl.BlockSpec(memory_space=pl.ANY)] * n,
        out_specs=tuple(pl.BlockSpec(memory_space=pl.ANY) for _ in range(n)),
        scratch_shapes=[pltpu.SemaphoreType.DMA((n * pieces * peers,)), pltpu.SemaphoreType.DMA((n * pieces * peers,)),
                        pltpu.SemaphoreType.DMA((n,))],
        compiler_params=pltpu.CompilerParams(has_side_effects=True),
    )(*arrays)
    return list(outs)


_HBM = pl.BlockSpec(memory_space=pltpu.HBM)
_SEM = pl.BlockSpec(memory_space=pltpu.SEMAPHORE)
_DATAFLOW = pltpu.SideEffectType.DATAFLOW_SIDE_EFFECTING


def _peer_copies(src_refs, land_refs, send_sems, recv_sems, gather, pieces=1):
    x, y, c = lax.axis_index("x"), lax.axis_index("y"), lax.axis_index("c")
    me = 4 * x + 2 * y + c
    peers = N_DEV - 1
    copies = []
    for p in range(1, N_DEV):
        px = 1 - x if p & 4 else x
        py = 1 - y if p & 2 else y
        pc = 1 - c if p & 1 else c
        for k in range(len(src_refs)):
            block = src_refs[k] if gather else src_refs[k].at[4 * px + 2 * py + pc]
            dst = land_refs[k].at[me]
            rows = block.shape[0] // pieces
            for r in range(pieces):
                part = pl.ds(r * rows, rows)
                sem = (k * pieces + r) * peers + p - 1
                copies.append(pltpu.make_async_remote_copy(
                    src_ref=block.at[part] if pieces > 1 else block, dst_ref=dst.at[part] if pieces > 1 else dst,
                    send_sem=send_sems.at[sem], recv_sem=recv_sems.at[sem],
                    device_id=(px, py, pc), device_id_type=pl.DeviceIdType.MESH))
    return copies


def _landing(srcs, gather, me):
    out = []
    for a in srcs:
        own = a if gather else lax.dynamic_index_in_dim(a, me, 0, keepdims=False)
        out.append(lax.dynamic_update_index_in_dim(lax.empty((N_DEV,) + own.shape, own.dtype), own, me, 0))
    return out


def _send_start(name, groups, gather, me, pieces=1, after=None):
    sizes = [len(g) for g in groups]
    srcs = [a for g in groups for a in g]
    n = len(srcs)
    lands = _landing(srcs, gather, me)
    ng = len(groups)
    first = [sum(sizes[:g]) for g in range(ng)]
    extra = [] if after is None else [after]

    def body(*refs):
        src_refs, land_refs = refs[:n], refs[n:2 * n]
        sems, token = refs[2 * n + len(extra):2 * n + len(extra) + 2 * ng], refs[-1]
        for g in range(ng):
            part = slice(first[g], first[g] + sizes[g])
            for cp in _peer_copies(src_refs[part], land_refs[part], sems[2 * g], sems[2 * g + 1], gather, pieces):
                cp.start()
        token[...] = jnp.zeros_like(token)

    sem_shapes = [pltpu.SemaphoreType.DMA((sizes[g // 2] * pieces * (N_DEV - 1),)) for g in range(2 * ng)]
    outs = pl.pallas_call(
        body, name=name,
        out_shape=(*sem_shapes, *[pltpu.HBM(a.shape, a.dtype) for a in (*srcs, *lands)], jax.ShapeDtypeStruct((8, 128), f32)),
        in_specs=[_HBM] * (2 * n) + [pl.BlockSpec(memory_space=pl.ANY)] * len(extra),
        out_specs=(*[_SEM] * (2 * ng), *[_HBM] * (2 * n), pl.BlockSpec(memory_space=pltpu.VMEM)),
        input_output_aliases={k: 2 * ng + k for k in range(2 * n)},
        compiler_params=pltpu.CompilerParams(has_side_effects=_DATAFLOW),
    )(*[pltpu.with_memory_space_constraint(a, pltpu.HBM) for a in (*srcs, *lands)], *extra)
    srcs_thru, lands_thru = outs[2 * ng:2 * ng + n], outs[2 * ng + n:2 * ng + 2 * n]
    handles = [(outs[2 * g], outs[2 * g + 1], list(srcs_thru[first[g]:first[g] + sizes[g]]),
                list(lands_thru[first[g]:first[g] + sizes[g]]), gather, pieces) for g in range(ng)]
    return handles, outs[-1]


def _send_wait(name, handle, after):
    send_sems, recv_sems, srcs, lands, gather, pieces = handle
    n = len(srcs)
    after = list(after) if isinstance(after, (list, tuple)) else [after]

    def body(*refs):
        src_refs, land_refs = refs[:n], refs[n:2 * n]
        for cp in _peer_copies(src_refs, land_refs, refs[2 * n], refs[2 * n + 1], gather, pieces):
            cp.wait_send()
            cp.wait_recv()
        refs[-1][...] = jnp.zeros_like(refs[-1])

    outs = pl.pallas_call(
        body, name=name,
        out_shape=(*[pltpu.HBM(a.shape, a.dtype) for a in (*srcs, *lands)], jax.ShapeDtypeStruct((8, 128), f32)),
        in_specs=[_HBM] * (2 * n) + [_SEM, _SEM] + [pl.BlockSpec(memory_space=pl.ANY)] * len(after),
        out_specs=(*[_HBM] * (2 * n), pl.BlockSpec(memory_space=pltpu.VMEM)),
        input_output_aliases={k: k for k in range(2 * n)},
        compiler_params=pltpu.CompilerParams(has_side_effects=_DATAFLOW),
    )(*srcs, *lands, send_sems, recv_sems, *after)
    return list(outs[n:2 * n]), outs[-1]


def _mod_part(c_all, w_mod):
    depth, d, cols = w_mod.shape

    def body(c_ref, w_ref, o_ref):
        cv = c_ref[...]
        cond = cv * _sigmoid(cv)
        o_ref[...] = jnp.dot(cond, w_ref[...], preferred_element_type=f32, precision=lax.Precision.HIGHEST)

    return pl.pallas_call(
        body, name="mod_part", grid=(depth,),
        out_shape=jax.ShapeDtypeStruct((depth, N_DEV, cols), f32),
        in_specs=[pl.BlockSpec((N_DEV, d), lambda i: (0, 0)), pl.BlockSpec((None, d, cols), lambda i: (i, 0, 0))],
        out_specs=pl.BlockSpec((None, N_DEV, cols), lambda i: (i, 0, 0)),
        compiler_params=_params(("arbitrary",), 32),
    )(c_all, w_mod)


def _mod_table(mod_row, b_mod, g_mix, g_ffn):
    def body(m_ref, b_ref, gm_ref, gf_ref, o_ref, token_ref):
        for i in range(DEPTH):
            for k in range(N_MOD):
                o_ref[i, k:k + 1, :] = m_ref[i:i + 1, k * D_MODEL:(k + 1) * D_MODEL] + b_ref[i:i + 1, k * D_MODEL:(k + 1) * D_MODEL]
            o_ref[i, R_GS_M:R_GS_M + 1, :] = gm_ref[i:i + 1, :] * (1.0 + o_ref[i, R_SC_M:R_SC_M + 1, :])
            o_ref[i, R_GS_F:R_GS_F + 1, :] = gf_ref[i:i + 1, :] * (1.0 + o_ref[i, R_SC_F:R_SC_F + 1, :])
        token_ref[...] = jnp.zeros_like(token_ref)

    return pl.pallas_call(
        body, name="mod_table",
        out_shape=(jax.ShapeDtypeStruct((DEPTH, 8, D_MODEL), f32), jax.ShapeDtypeStruct((8, 128), f32)))(
        mod_row, b_mod, g_mix, g_ffn)


def _ffn_tile(s):
    return min(512, s)


def _layer_weights(shape):
    return pl.BlockSpec((N_DEV,) + shape, lambda i: (0, 0, 0))


def _ffn_fwd(x, vec, w1g, w2g, layer):
    s = x.shape[0]
    ts = _ffn_tile(s)

    def body(x_ref, vec_ref, w1_ref, w2_ref, xo_ref, u_ref, y_ref, hb_ref):
        xv = x_ref[...]
        n, _ = _rms(xv)
        hb = (n * vec_ref[R_GS_F:R_GS_F + 1, :] + vec_ref[R_SH_F:R_SH_F + 1, :]).astype(bf16)
        hb_ref[...] = hb
        yv = jnp.zeros((ts, D_MODEL), f32)
        for f in range(N_DEV):
            u = jnp.maximum(_mm(hb, w1_ref[f]), 0.0)
            u_ref[:, f * FF_CHUNK:(f + 1) * FF_CHUNK] = u.astype(bf16)
            yv = yv + _mm((u * u).astype(bf16), w2_ref[f])
        y_ref[...] = yv.astype(bf16)
        xo_ref[...] = xv + vec_ref[R_GT_F:R_GT_F + 1, :] * yv

    row = pl.BlockSpec((ts, D_MODEL), lambda i: (i, 0))
    return pl.pallas_call(
        body, name=f"ffn_fwd_{layer}", grid=(s // ts,),
        out_shape=(jax.ShapeDtypeStruct((s, D_MODEL), f32), jax.ShapeDtypeStruct((s, D_FF), bf16),
                   jax.ShapeDtypeStruct((s, D_MODEL), bf16), jax.ShapeDtypeStruct((s, D_MODEL), bf16)),
        in_specs=[row, pl.BlockSpec((8, D_MODEL), lambda i: (0, 0)),
                  _layer_weights((D_MODEL, FF_CHUNK)), _layer_weights((FF_CHUNK, D_MODEL))],
        out_specs=(row, pl.BlockSpec((ts, D_FF), lambda i: (i, 0)), row, row),
        compiler_params=_params(("arbitrary",), 56),
    )(x, vec, w1g, w2g)


def _ffn_bwd_act(x, dx, u, y, vec, w1g, w2g, layer):
    s = x.shape[0]
    ts = _ffn_tile(s)

    def body(x_ref, dx_ref, u_ref, y_ref, vec_ref, w1_ref, w2_ref, dxo_ref, da_ref, dyb_ref, sm_ref):
        @pl.when(pl.program_id(0) == 0)
        def _():
            sm_ref[...] = jnp.zeros_like(sm_ref)

        dxv = dx_ref[...]
        dyb = (dxv * vec_ref[R_GT_F:R_GT_F + 1, :]).astype(bf16)
        dyb_ref[...] = dyb
        sm_ref[G_GT:G_GT + 1, :] += _colsum(dxv * y_ref[...].astype(f32))
        dh = jnp.zeros((ts, D_MODEL), f32)
        for f in range(N_DEV):
            cols = slice(f * FF_CHUNK, (f + 1) * FF_CHUNK)
            dz = _mm_nt(dyb, w2_ref[f])
            dab = (dz * (2.0 * u_ref[:, cols].astype(f32))).astype(bf16)
            da_ref[:, cols] = dab
            dh = dh + _mm_nt(dab, w1_ref[f])
        n, r = _rms(x_ref[...])
        sm_ref[G_SH:G_SH + 1, :] += _colsum(dh)
        sm_ref[G_GS:G_GS + 1, :] += _colsum(dh * n)
        dxo_ref[...] = dxv + _norm_bwd(dh, n, r, vec_ref[R_GS_F:R_GS_F + 1, :])

    row = pl.BlockSpec((ts, D_MODEL), lambda i: (i, 0))
    wide = pl.BlockSpec((ts, D_FF), lambda i: (i, 0))
    return pl.pallas_call(
        body, name=f"ffn_bwd_act_{layer}", grid=(s // ts,),
        out_shape=(jax.ShapeDtypeStruct((s, D_MODEL), f32), jax.ShapeDtypeStruct((s, D_FF), bf16),
                   jax.ShapeDtypeStruct((s, D_MODEL), bf16), jax.ShapeDtypeStruct((8, D_MODEL), f32)),
        in_specs=[row, row, wide, row, pl.BlockSpec((8, D_MODEL), lambda i: (0, 0)),
                  _layer_weights((D_MODEL, FF_CHUNK)), _layer_weights((FF_CHUNK, D_MODEL))],
        out_specs=(row, wide, row, pl.BlockSpec((8, D_MODEL), lambda i: (0, 0))),
        compiler_params=_params(("arbitrary",), 58),
    )(x, dx, u, y, vec, w1g, w2g)


def _ffn_bwd_w1(hb, da, layer):
    s = hb.shape[0]
    ts = _ffn_tile(s)
    nt = s // ts

    def body(hb_ref, da_ref, dw_ref, acc_ref):
        i = pl.program_id(0)

        @pl.when(i == 0)
        def _():
            acc_ref[...] = jnp.zeros_like(acc_ref)

        hb = hb_ref[...]
        for f in range(N_DEV):
            acc_ref[f] += _mm_tn(hb, da_ref[:, f * FF_CHUNK:(f + 1) * FF_CHUNK])

        @pl.when(i == nt - 1)
        def _():
            dw_ref[...] = acc_ref[...].astype(bf16)

    return pl.pallas_call(
        body, name=f"ffn_bwd_w1_{layer}", grid=(nt,),
        out_shape=jax.ShapeDtypeStruct((N_DEV, D_MODEL, FF_CHUNK), bf16),
        in_specs=[pl.BlockSpec((ts, D_MODEL), lambda i: (i, 0)), pl.BlockSpec((ts, D_FF), lambda i: (i, 0))],
        out_specs=pl.BlockSpec((N_DEV, D_MODEL, FF_CHUNK), lambda i: (0, 0, 0)),
        scratch_shapes=[pltpu.VMEM((N_DEV, D_MODEL, FF_CHUNK), f32)],
        compiler_params=_params(("arbitrary",), 56),
    )(hb, da)


def _ffn_bwd_w2(u, dyb, layer):
    s = u.shape[0]
    ts = _ffn_tile(s)
    nt = s // ts

    def body(u_ref, dyb_ref, dw_ref, acc_ref):
        i = pl.program_id(0)

        @pl.when(i == 0)
        def _():
            acc_ref[...] = jnp.zeros_like(acc_ref)

        dyb = dyb_ref[...]
        for f in range(N_DEV):
            uv = u_ref[:, f * FF_CHUNK:(f + 1) * FF_CHUNK].astype(f32)
            acc_ref[f] += _mm_tn((uv * uv).astype(bf16), dyb)

        @pl.when(i == nt - 1)
        def _():
            dw_ref[...] = acc_ref[...].astype(bf16)

    return pl.pallas_call(
        body, name=f"ffn_bwd_w2_{layer}", grid=(nt,),
        out_shape=jax.ShapeDtypeStruct((N_DEV, FF_CHUNK, D_MODEL), bf16),
        in_specs=[pl.BlockSpec((ts, D_FF), lambda i: (i, 0)), pl.BlockSpec((ts, D_MODEL), lambda i: (i, 0))],
        out_specs=pl.BlockSpec((N_DEV, FF_CHUNK, D_MODEL), lambda i: (0, 0, 0)),
        scratch_shapes=[pltpu.VMEM((N_DEV, FF_CHUNK, D_MODEL), f32)],
        compiler_params=_params(("arbitrary",), 56),
    )(u, dyb)


def _lru_gates(xc, wsm_ref, pv_ref):
    xcb = xc.astype(bf16)
    gr = _sigmoid(_block_diag(xcb, wsm_ref, 0) + pv_ref[P_BA:P_BA + 1, :])
    gi = _sigmoid(_block_diag(xcb, wsm_ref, 1) + pv_ref[P_BX:P_BX + 1, :])
    log_a = (LRU_C * _log_sigmoid(pv_ref[P_LAM:P_LAM + 1, :])) * gr
    t = jnp.tanh(log_a)
    return gr, gi, jnp.exp(log_a), jnp.sqrt((-2.0 * t) / (1.0 - t))


def _conv(xr, taps_before, pv_ref):
    xc = xr * pv_ref[P_CW0 + 3:P_CW0 + 4, :] + pv_ref[P_CONVB:P_CONVB + 1, :]
    for k, v in zip((2, 1, 0), taps_before):
        xc = xc + v * pv_ref[P_CW0 + k:P_CW0 + k + 1, :]
    return xc


LRU_FWD_SUB, LRU_FWD_SUBS = 128, 4
LRU_BWD_SUB, LRU_BWD_SUBS = 256, 1


def _scan_rows(a, u, carry, reverse):
    groups = a.shape[0] // SUBLANES
    row = lax.broadcasted_iota(jnp.int32, (SUBLANES, a.shape[1]), 0)
    outs = [None] * groups
    for j in range(groups):
        g = groups - 1 - j if reverse else j
        av, uv = a[g * SUBLANES:(g + 1) * SUBLANES], u[g * SUBLANES:(g + 1) * SUBLANES]
        for k in (1, 2, 4):
            if reverse:
                valid, shift = row < SUBLANES - k, SUBLANES - k
            else:
                valid, shift = row >= k, k
            a_s = jnp.where(valid, pltpu.roll(av, shift, 0), 1.0)
            u_s = jnp.where(valid, pltpu.roll(uv, shift, 0), 0.0)
            uv = uv + av * u_s
            av = av * a_s
        h = uv + av * carry
        outs[g] = h
        carry = h[0:1, :] if reverse else h[SUBLANES - 1:SUBLANES, :]
    return jnp.concatenate(outs, axis=0), carry


def _lru_fwd(x, vec, wbig, wsm, pvec, layer):
    s = x.shape[0]
    sub = min(LRU_FWD_SUB, s)
    ts = min(sub * LRU_FWD_SUBS, s)
    nsub = ts // sub
    w = LRU_WIDTH

    def body(x_ref, vec_ref, wb_ref, wsm_ref, pv_ref, xo_ref, xr_ref, hs_ref, a_ref, mult_ref, gr_ref, gi_ref,
             gel_ref, geld_ref, y_ref, tail_ref, carry_ref):
        @pl.when(pl.program_id(0) == 0)
        def _():
            tail_ref[...] = jnp.zeros_like(tail_ref)
            carry_ref[...] = jnp.zeros_like(carry_ref)

        sel = _shift_matrix(sub, BF16_ROWS, (1, 2, 3))
        for k in range(nsub):
            rows = slice(k * sub, (k + 1) * sub)
            xv = x_ref[rows, :]
            n, _ = _rms(xv)
            hb = (n * vec_ref[R_GS_M:R_GS_M + 1, :] + vec_ref[R_SH_M:R_SH_M + 1, :]).astype(bf16)
            gelu_v, gelu_d = _gelu_and_grad(_mm(hb, wb_ref[0]) + pv_ref[P_BY:P_BY + 1, :])
            gel_ref[rows, :] = gelu_v.astype(bf16)
            geld_ref[rows, :] = gelu_d.astype(bf16)
            xrb = (_mm(hb, wb_ref[1]) + pv_ref[P_BIN:P_BIN + 1, :]).astype(bf16)
            xr_ref[rows, :] = xrb
            xc = _conv(xrb.astype(f32), _shifted_rows(sel, tail_ref[...], xrb), pv_ref)
            tail_ref[...] = xrb[sub - BF16_ROWS:, :]
            gr, gi, a, mult = _lru_gates(xc, wsm_ref, pv_ref)
            gr_ref[rows, :] = gr.astype(bf16)
            gi_ref[rows, :] = gi.astype(bf16)
            a_ref[rows, :] = a
            mult_ref[rows, :] = mult
            hs, carry = _scan_rows(a, mult * (gi * xc), carry_ref[0:1, :], reverse=False)
            carry_ref[0:1, :] = carry
            hs_ref[rows, :] = hs
            yv = _mm((hs * gelu_v).astype(bf16), wb_ref[2]) + pv_ref[P_BOUT:P_BOUT + 1, :]
            y_ref[rows, :] = yv.astype(bf16)
            xo_ref[rows, :] = xv + vec_ref[R_GT_M:R_GT_M + 1, :] * yv

    row = pl.BlockSpec((ts, D_MODEL), lambda i: (i, 0))
    roww = pl.BlockSpec((ts, w), lambda i: (i, 0))
    wide = lambda dt: jax.ShapeDtypeStruct((s, w), dt)
    return pl.pallas_call(
        body, name=f"lru_fwd_{layer}", grid=(s // ts,),
        out_shape=(jax.ShapeDtypeStruct((s, D_MODEL), f32), wide(bf16), wide(f32), wide(f32), wide(f32),
                   wide(bf16), wide(bf16), wide(bf16), wide(bf16), jax.ShapeDtypeStruct((s, D_MODEL), bf16)),
        in_specs=[row, pl.BlockSpec((8, D_MODEL), lambda i: (0, 0)),
                  pl.BlockSpec((3, w, w), lambda i: (0, 0, 0)),
                  pl.BlockSpec((2, HEADS, HEAD_DIM, HEAD_DIM), lambda i: (0, 0, 0, 0)),
                  pl.BlockSpec((16, w), lambda i: (0, 0))],
        out_specs=(row, roww, roww, roww, roww, roww, roww, roww, roww, row),
        scratch_shapes=[pltpu.VMEM((BF16_ROWS, w), bf16), pltpu.VMEM((SUBLANES, w), f32)],
        compiler_params=_params(("arbitrary",)),
    )(x, vec, wbig, wsm, pvec)


def _lru_bwd(x, dx, saved, vec, wbig, wsm, pvec, layer):
    xr, hs, a_all, mult_all, gr_all, gi_all, gel_all, geld_all, y = saved
    s = x.shape[0]
    sub = min(LRU_BWD_SUB, s)
    ts = min(sub * LRU_BWD_SUBS, s)
    nsub = ts // sub
    nt = s // ts
    w = LRU_WIDTH
    shard = w // N_DEV
    hshard = HEAD_DIM // N_DEV

    def body(x_ref, dx_ref, xr_ref, xrh_ref, hs_ref, hsh_ref, a_ref, mult_ref, gr_ref, gi_ref, gel_ref, geld_ref,
             y_ref, vec_ref, wb_ref, wsm_ref, pv_ref,
             dxo_ref, dwb_ref, dwsm_ref, sm_ref, accb_ref, accs_ref, eps_ref, dxc8_ref,
             hb_scr, dgb_scr, dxrb_scr, mb_scr, dyb_scr, xcb_scr, drab_scr, drxb_scr):
        i = pl.program_id(0)
        first_tile = i == nt - 1

        @pl.when(i == 0)
        def _():
            accb_ref[...] = jnp.zeros_like(accb_ref)
            accs_ref[...] = jnp.zeros_like(accs_ref)
            sm_ref[...] = jnp.zeros_like(sm_ref)
            eps_ref[...] = jnp.zeros_like(eps_ref)
            dxc8_ref[...] = jnp.zeros_like(dxc8_ref)

        gs = vec_ref[R_GS_M:R_GS_M + 1, :]
        c_ls = LRU_C * _log_sigmoid(pv_ref[P_LAM:P_LAM + 1, :])
        for k in reversed(range(nsub)):
            rows = slice(k * sub, (k + 1) * sub)
            xv = x_ref[rows, :]
            dxv = dx_ref[rows, :]
            n, r = _rms(xv)
            hb_scr[rows, :] = (n * gs + vec_ref[R_SH_M:R_SH_M + 1, :]).astype(bf16)
            xrv = xr_ref[rows, :].astype(f32)
            hsv = hs_ref[rows, :]
            if k == 0:
                xr_halo = jnp.where(first_tile, 0.0, xrh_ref[...].astype(f32))
                hs_halo = jnp.where(first_tile, 0.0, hsh_ref[...])
            else:
                xr_halo = xr_ref[k * sub - BF16_ROWS:k * sub, :].astype(f32)
                hs_halo = hs_ref[k * sub - SUBLANES:k * sub, :]
            xs1, xs2, xs3 = _rows_before(xr_halo, xrv, (1, 2, 3))
            xc = _conv(xrv, (xs1, xs2, xs3), pv_ref)
            xcb_scr[rows, :] = xc.astype(bf16)
            a, mult = a_ref[rows, :], mult_ref[rows, :]
            gr, gi = gr_ref[rows, :].astype(f32), gi_ref[rows, :].astype(f32)
            gelu_v = gel_ref[rows, :].astype(f32)

            dy = dxv * vec_ref[R_GT_M:R_GT_M + 1, :]
            dyb = dy.astype(bf16)
            dyb_scr[rows, :] = dyb
            sm_ref[G_GT:G_GT + 1, :] += _colsum(dxv * y_ref[rows, :].astype(f32))
            sm_ref[G_BOUT:G_BOUT + 1, :] += _colsum(dy)
            mb_scr[rows, :] = (hsv * gelu_v).astype(bf16)
            dm = _mm_nt(dyb, wb_ref[2])
            dhs = dm * gelu_v
            dgpre = dm * hsv * geld_ref[rows, :].astype(f32)
            dgb = dgpre.astype(bf16)
            dgb_scr[rows, :] = dgb
            sm_ref[G_BY:G_BY + 1, :] += _colsum(dgpre)

            eps_in = eps_ref[0:1, :]
            eps, eps_out = _scan_rows(a, a * dhs, eps_in, reverse=True)
            eps_ref[0:1, :] = eps_out
            (eps_next,) = _rows_after(eps, jnp.broadcast_to(eps_in, (SUBLANES, w)), (1,))
            delta = dhs + eps_next
            (h_prev,) = _rows_before(hs_halo, hsv, (1,))
            dxi = delta * xc
            dgi = dxi * mult
            dla = (delta * h_prev) * a - (dxi * gi) * (a * a) / mult
            sm_ref[G_LS:G_LS + 1, :] += _colsum(dla * gr)
            dra = (dla * c_ls) * (gr - gr * gr)
            drx = dgi * (gi - gi * gi)
            drab, drxb = dra.astype(bf16), drx.astype(bf16)
            drab_scr[rows, :] = drab
            drxb_scr[rows, :] = drxb
            sm_ref[G_BA:G_BA + 1, :] += _colsum(dra)
            sm_ref[G_BX:G_BX + 1, :] += _colsum(drx)
            dxc = (delta * mult) * gi + _block_diag_t(drab, wsm_ref, 0) + _block_diag_t(drxb, wsm_ref, 1)

            sm_ref[G_CONVB:G_CONVB + 1, :] += _colsum(dxc)
            for kk, v in zip((3, 2, 1, 0), (xrv, xs1, xs2, xs3)):
                sm_ref[G_CW0 + kk:G_CW0 + kk + 1, :] += _colsum(dxc * v)
            ups = _rows_after(dxc, dxc8_ref[...], (1, 2, 3))
            dxc8_ref[...] = dxc[0:SUBLANES, :]
            dxr = dxc * pv_ref[P_CW0 + 3:P_CW0 + 4, :]
            for kk, v in zip((2, 1, 0), ups):
                dxr = dxr + v * pv_ref[P_CW0 + kk:P_CW0 + kk + 1, :]
            dxrb = dxr.astype(bf16)
            dxrb_scr[rows, :] = dxrb
            sm_ref[G_BIN:G_BIN + 1, :] += _colsum(dxr)
            dh = _mm_nt(dgb, wb_ref[0]) + _mm_nt(dxrb, wb_ref[1])
            sm_ref[G_SH:G_SH + 1, :] += _colsum(dh)
            sm_ref[G_GS:G_GS + 1, :] += _colsum(dh * n)
            dxo_ref[rows, :] = dxv + _norm_bwd(dh, n, r, gs)

        hb = hb_scr[...]
        accb_ref[0] += _mm_tn(hb, dgb_scr[...])
        accb_ref[1] += _mm_tn(hb, dxrb_scr[...])
        accb_ref[2] += _mm_tn(mb_scr[...], dyb_scr[...])
        for h in range(HEADS):
            cols = slice(h * HEAD_DIM, (h + 1) * HEAD_DIM)
            accs_ref[0, h] += _mm_tn(xcb_scr[:, cols], drab_scr[:, cols])
            accs_ref[1, h] += _mm_tn(xcb_scr[:, cols], drxb_scr[:, cols])

        @pl.when(i == nt - 1)
        def _():
            sm_ref[G_LS:G_LS + 1, :] = sm_ref[G_LS:G_LS + 1, :] * LRU_C
            for k in range(3):
                dwb_ref[:, k] = accb_ref[k].astype(bf16).reshape(N_DEV, shard, w)
            for k in range(2):
                for h in range(HEADS):
                    dwsm_ref[:, k, h] = accs_ref[k, h].astype(bf16).reshape(N_DEV, hshard, HEAD_DIM)

    rev = lambda i: (nt - 1 - i, 0)
    row = pl.BlockSpec((ts, D_MODEL), rev)
    roww = pl.BlockSpec((ts, w), rev)
    halo16 = pl.BlockSpec((BF16_ROWS, w), lambda i: (jnp.maximum((nt - 1 - i) * (ts // BF16_ROWS) - 1, 0), 0))
    halo8 = pl.BlockSpec((SUBLANES, w), lambda i: (jnp.maximum((nt - 1 - i) * (ts // SUBLANES) - 1, 0), 0))
    const = lambda *shape: pl.BlockSpec(shape, lambda i: (0,) * len(shape))
    operand = pltpu.VMEM((ts, w), bf16)
    return pl.pallas_call(
        body, name=f"lru_bwd_{layer}", grid=(nt,),
        out_shape=(jax.ShapeDtypeStruct((s, D_MODEL), f32),
                   jax.ShapeDtypeStruct((N_DEV, 3, shard, w), bf16),
                   jax.ShapeDtypeStruct((N_DEV, 2, HEADS, hshard, HEAD_DIM), bf16),
                   jax.ShapeDtypeStruct((16, w), f32)),
        in_specs=[row, row, roww, halo16, roww, halo8, roww, roww, roww, roww, roww, roww, row, const(8, D_MODEL),
                  const(3, w, w), const(2, HEADS, HEAD_DIM, HEAD_DIM), const(16, w)],
        out_specs=(row, const(N_DEV, 3, shard, w), const(N_DEV, 2, HEADS, hshard, HEAD_DIM), const(16, w)),
        scratch_shapes=[pltpu.VMEM((3, w, w), f32), pltpu.VMEM((2, HEADS, HEAD_DIM, HEAD_DIM), f32),
                        pltpu.VMEM((SUBLANES, w), f32), pltpu.VMEM((SUBLANES, w), f32)] + [operand] * 8,
        compiler_params=_params(("arbitrary",), 58),
    )(x, dx, xr, xr, hs, hs, a_all, mult_all, gr_all, gi_all, gel_all, geld_all, y, vec, wbig, wsm, pvec)


def _pool_tile(s):
    return min(1024, s)


def _pool_counts(tile_index, ts):
    t = (tile_index * ts + lax.broadcasted_iota(jnp.int32, (ts, 1), 0) + 1).astype(f32)
    return [1.0 / jnp.minimum(t, float(win)) for win in POOL_WINDOWS]


def _pooled(h, halo, inv):
    ext = jnp.concatenate([halo, h], axis=0)
    out = []
    for g in range(len(POOL_WINDOWS)):
        acc = ext[:, g * HEAD_DIM:(g + 1) * HEAD_DIM]
        for step in range(g + 1):
            acc = acc + pltpu.roll(acc, 1 << step, 0)
        out.append(acc[POOL_HALO:] * inv[g] - h[:, g * HEAD_DIM:(g + 1) * HEAD_DIM])
    return out


def _pool_fwd(x, vec, pw, ps, layer):
    s = x.shape[0]
    ts = _pool_tile(s)

    def body(x_ref, vec_ref, pw_ref, ps_ref, xo_ref, y_ref, halo_ref):
        i = pl.program_id(0)

        @pl.when(i == 0)
        def _():
            halo_ref[...] = jnp.zeros_like(halo_ref)

        xv = x_ref[...]
        n, _ = _rms(xv)
        h = n * vec_ref[R_GS_M:R_GS_M + 1, :] + vec_ref[R_SH_M:R_SH_M + 1, :]
        pooled = _pooled(h, halo_ref[...], _pool_counts(i, ts))
        halo_ref[...] = h[ts - POOL_HALO:, :]
        mixed = jnp.concatenate([_mm(pooled[g].astype(bf16), pw_ref[g]) for g in range(HEADS)], axis=1)
        yv = mixed * ps_ref[0:1, :]
        y_ref[...] = yv.astype(bf16)
        xo_ref[...] = xv + vec_ref[R_GT_M:R_GT_M + 1, :] * yv

    row = pl.BlockSpec((ts, D_MODEL), lambda i: (i, 0))
    return pl.pallas_call(
        body, name=f"pool_fwd_{layer}", grid=(s // ts,),
        out_shape=(jax.ShapeDtypeStruct((s, D_MODEL), f32), jax.ShapeDtypeStruct((s, D_MODEL), bf16)),
        in_specs=[row, pl.BlockSpec((8, D_MODEL), lambda i: (0, 0)),
                  pl.BlockSpec((HEADS, HEAD_DIM, HEAD_DIM), lambda i: (0, 0, 0)),
                  pl.BlockSpec((8, D_MODEL), lambda i: (0, 0))],
        out_specs=(row, row),
        scratch_shapes=[pltpu.VMEM((POOL_HALO, D_MODEL), f32)],
        compiler_params=_params(("arbitrary",)),
    )(x, vec, pw, ps)


def _pool_bwd(x, dx, y, vec, pw, ps, layer):
    s = x.shape[0]
    ts = _pool_tile(s)
    nt = s // ts
    hshard = HEAD_DIM // N_DEV

    def body(x_ref, xh_ref, dx_ref, y_ref, vec_ref, pw_ref, ps_ref, dxo_ref, dpw_ref, sm_ref, acc_ref, q16_ref):
        i = pl.program_id(0)
        tile = nt - 1 - i

        @pl.when(i == 0)
        def _():
            acc_ref[...] = jnp.zeros_like(acc_ref)
            sm_ref[...] = jnp.zeros_like(sm_ref)
            q16_ref[...] = jnp.zeros_like(q16_ref)

        gs, sh = vec_ref[R_GS_M:R_GS_M + 1, :], vec_ref[R_SH_M:R_SH_M + 1, :]
        xv = x_ref[...]
        dxv = dx_ref[...]
        n, r = _rms(xv)
        h = n * gs + sh
        nh, _ = _rms(xh_ref[...])
        halo = jnp.where(tile == 0, 0.0, nh * gs + sh)
        inv = _pool_counts(tile, ts)
        pooled = _pooled(h, halo, inv)
        mixed = jnp.concatenate([_mm(pooled[g].astype(bf16), pw_ref[g]) for g in range(HEADS)], axis=1)

        dy = dxv * vec_ref[R_GT_M:R_GT_M + 1, :]
        sm_ref[G_GT:G_GT + 1, :] += _colsum(dxv * y_ref[...].astype(f32))
        sm_ref[3:4, :] += _colsum(dy * mixed)
        dmixed = (dy * ps_ref[0:1, :]).astype(bf16)
        dh_parts = []
        for g in range(HEADS):
            cols = slice(g * HEAD_DIM, (g + 1) * HEAD_DIM)
            acc_ref[g] += _mm_tn(pooled[g].astype(bf16), dmixed[:, cols])
            dpooled = _mm_nt(dmixed[:, cols], pw_ref[g])
            q = dpooled * inv[g]
            ext = jnp.concatenate([q, q16_ref[:, cols]], axis=0)
            q16_ref[:, cols] = q[0:POOL_HALO, :]
            for step in range(g + 1):
                ext = ext + pltpu.roll(ext, ext.shape[0] - (1 << step), 0)
            dh_parts.append(ext[:ts] - dpooled)
        dh = jnp.concatenate(dh_parts, axis=1)
        sm_ref[G_SH:G_SH + 1, :] += _colsum(dh)
        sm_ref[G_GS:G_GS + 1, :] += _colsum(dh * n)
        dxo_ref[...] = dxv + _norm_bwd(dh, n, r, gs)

        @pl.when(i == nt - 1)
        def _():
            for g in range(HEADS):
                dpw_ref[:, g] = acc_ref[g].astype(bf16).reshape(N_DEV, hshard, HEAD_DIM)

    rev = lambda i: (nt - 1 - i, 0)
    row = pl.BlockSpec((ts, D_MODEL), rev)
    halo16 = pl.BlockSpec((POOL_HALO, D_MODEL), lambda i: (jnp.maximum((nt - 1 - i) * (ts // POOL_HALO) - 1, 0), 0))
    const = lambda *shape: pl.BlockSpec(shape, lambda i: (0,) * len(shape))
    return pl.pallas_call(
        body, name=f"pool_bwd_{layer}", grid=(nt,),
        out_shape=(jax.ShapeDtypeStruct((s, D_MODEL), f32),
                   jax.ShapeDtypeStruct((N_DEV, HEADS, hshard, HEAD_DIM), bf16),
                   jax.ShapeDtypeStruct((8, D_MODEL), f32)),
        in_specs=[row, halo16, row, row, const(8, D_MODEL), const(HEADS, HEAD_DIM, HEAD_DIM), const(8, D_MODEL)],
        out_specs=(row, const(N_DEV, HEADS, hshard, HEAD_DIM), const(8, D_MODEL)),
        scratch_shapes=[pltpu.VMEM((HEADS, HEAD_DIM, HEAD_DIM), f32), pltpu.VMEM((POOL_HALO, D_MODEL), f32)],
        compiler_params=_params(("arbitrary",)),
    )(x, x, dx, y, vec, pw, ps)


def _final(x, target, g_fin):
    s = x.shape[0]
    ts = min(1024, s)

    def body(x_ref, t_ref, g_ref, dx_ref, sm_ref):
        @pl.when(pl.program_id(0) == 0)
        def _():
            sm_ref[...] = jnp.zeros_like(sm_ref)

        g = g_ref[0:1, :]
        n, r = _rms(x_ref[...])
        err = n * g - t_ref[...]
        sm_ref[1:2, :] += 0.5 * jnp.sum(jnp.mean(err * err, axis=-1, keepdims=True), axis=0, keepdims=True)
        dyv = err * (1.0 / D_MODEL)
        sm_ref[0:1, :] += _colsum(dyv * n)
        dx_ref[...] = _norm_bwd(dyv, n, r, g)

    row = pl.BlockSpec((ts, D_MODEL), lambda i: (i, 0))
    return pl.pallas_call(
        body, name="final_loss", grid=(s // ts,),
        out_shape=(jax.ShapeDtypeStruct((s, D_MODEL), f32), jax.ShapeDtypeStruct((8, D_MODEL), f32)),
        in_specs=[row, row, pl.BlockSpec((8, D_MODEL), lambda i: (0, 0))],
        out_specs=(row, pl.BlockSpec((8, D_MODEL), lambda i: (0, 0))),
        compiler_params=_params(("arbitrary",)),
    )(x, target, g_fin)


def _small_pack(sm_ffn, sm_mix, sm_fin, table, g_mix, g_ffn, lam):
    def body(*refs):
        ffn, mix = refs[0:DEPTH], refs[DEPTH:2 * DEPTH]
        fin_ref, tab_ref, gm_ref, gf_ref, lam_ref, o_ref = refs[2 * DEPTH:]
        o_ref[...] = jnp.zeros_like(o_ref)
        for i in range(DEPTH):
            base = K_MOD + i * N_MOD
            o_ref[base + 0:base + 1, :] = mix[i][G_SH:G_SH + 1, :]
            o_ref[base + 1:base + 2, :] = mix[i][G_GS:G_GS + 1, :] * gm_ref[i:i + 1, :]
            o_ref[base + 2:base + 3, :] = mix[i][G_GT:G_GT + 1, :]
            o_ref[base + 3:base + 4, :] = ffn[i][G_SH:G_SH + 1, :]
            o_ref[base + 4:base + 5, :] = ffn[i][G_GS:G_GS + 1, :] * gf_ref[i:i + 1, :]
            o_ref[base + 5:base + 6, :] = ffn[i][G_GT:G_GT + 1, :]
            o_ref[K_NMIX + i:K_NMIX + i + 1, :] = mix[i][G_GS:G_GS + 1, :] * (1.0 + tab_ref[i, R_SC_M:R_SC_M + 1, :])
            o_ref[K_NFFN + i:K_NFFN + i + 1, :] = ffn[i][G_GS:G_GS + 1, :] * (1.0 + tab_ref[i, R_SC_F:R_SC_F + 1, :])
            j = i // 2
            if i % 2 == 0:
                for k, src in enumerate((G_BY, G_BIN, G_CONVB, None, G_BOUT)):
                    dst = K_LRUB + j * 5 + k
                    if src is None:
                        o_ref[dst:dst + 1, :] = mix[i][G_LS:G_LS + 1, :] * _sigmoid(-lam_ref[j:j + 1, :])
                    else:
                        o_ref[dst:dst + 1, :] = mix[i][src:src + 1, :]
                o_ref[K_CONVW + j * 4:K_CONVW + j * 4 + 4, :] = mix[i][G_CW0:G_CW0 + 4, :]
                o_ref[K_BA + j:K_BA + j + 1, :] = mix[i][G_BA:G_BA + 1, :]
                o_ref[K_BX + j:K_BX + j + 1, :] = mix[i][G_BX:G_BX + 1, :]
            else:
                o_ref[K_PS + j:K_PS + j + 1, :] = mix[i][3:4, :]
        o_ref[K_FIN:K_FIN + 2, :] = fin_ref[0:2, :]

    return pl.pallas_call(body, name="small_pack", out_shape=jax.ShapeDtypeStruct((K_ROWS, D_MODEL), f32))(
        *sm_ffn, *sm_mix, sm_fin, table, g_mix, g_ffn, lam)


def _small_sum(gathered):
    def body(g_ref, o_ref, token_ref):
        tot = g_ref[0]
        for src in range(1, N_DEV):
            tot = tot + g_ref[src]
        o_ref[...] = tot
        token_ref[...] = jnp.zeros_like(token_ref)

    return pl.pallas_call(
        body, name="small_sum",
        out_shape=(jax.ShapeDtypeStruct(gathered.shape[1:], f32), jax.ShapeDtypeStruct((8, 128), f32)))(gathered)


def _adamw_math(g, w, m, v):
    m = ADAM_B1 * m + (1.0 - ADAM_B1) * g
    v = ADAM_B2 * v + (1.0 - ADAM_B2) * (g * g)
    m_hat = m / (1.0 - ADAM_B1 ** ADAM_STEP)
    v_hat = v / (1.0 - ADAM_B2 ** ADAM_STEP)
    delta = -ADAM_LR * (m_hat / (jnp.sqrt(v_hat) + ADAM_EPS) + ADAM_WD * w)
    return delta, m, v


def _adamw_small(params):
    n = len(params)
    shapes = [w.shape for _, w, _, _ in params]
    two_d = [(1, s[0]) if len(s) == 1 else (math.prod(s[:-1]), s[-1]) for s in shapes]

    def body(*refs):
        ins, outs = refs[:4 * n], refs[4 * n:]
        for k in range(n):
            g_ref, w_ref, m_ref, v_ref = ins[4 * k:4 * k + 4]
            outs[3 * k][...], outs[3 * k + 1][...], outs[3 * k + 2][...] = _adamw_math(
                g_ref[...], w_ref[...], m_ref[...], v_ref[...])

    outs = pl.pallas_call(
        body, name="adamw_small",
        out_shape=tuple(jax.ShapeDtypeStruct(two_d[k], f32) for k in range(n) for _ in range(3)))(
        *(t.reshape(two_d[k]) for k, p in enumerate(params) for t in p))
    return [tuple(outs[3 * k + i].reshape(shapes[k]) for i in range(3)) for k in range(n)]


def _block_rows(rows, cols):
    tr = max(SUBLANES, min(rows, (512 * 1024) // (4 * cols)))
    while rows % tr:
        tr //= 2
    return tr


def _adamw_reduce(name, landings, kind, w, m, v):
    nl = len(landings)
    rows, cols = landings[0].shape[2:]
    tr = _block_rows(rows, cols)
    per_layer = rows // tr

    def body(*refs):
        l_refs = refs[:nl]
        w_ref, m_ref, v_ref, g_ref, d_ref, mo_ref, vo_ref = refs[nl:]
        layer = pl.program_id(0)
        for k in range(nl):
            @pl.when(layer == k)
            def _(k=k):
                g = l_refs[k][0].astype(f32)
                for src in range(1, N_DEV):
                    g = g + l_refs[k][src].astype(f32)
                g_ref[...] = g
        d_ref[...], mo_ref[...], vo_ref[...] = _adamw_math(g_ref[...], w_ref[...], m_ref[...], v_ref[...])

    blk = pl.BlockSpec((tr, cols), lambda l, r: (l * per_layer + r, 0))
    land = [pl.BlockSpec((N_DEV, None, tr, cols), lambda l, r, k=k: (0, kind, jnp.where(l == k, r, 0), 0)) for k in range(nl)]
    return pl.pallas_call(
        body, name=f"adamw_{name}", grid=(nl, per_layer),
        out_shape=tuple(jax.ShapeDtypeStruct((nl * rows, cols), f32) for _ in range(4)),
        in_specs=land + [blk, blk, blk],
        out_specs=(blk, blk, blk, blk),
        compiler_params=_params(("arbitrary", "arbitrary"), 32),
    )(*landings, w, m, v)


def _adamw_w_mod(c_all, dmod_all, w, m, v):
    depth, d, cols = w.shape
    tr = 256

    def body(c_ref, dm_ref, w_ref, m_ref, v_ref, g_ref, d_ref, mo_ref, vo_ref):
        cv = c_ref[...]
        cond = cv * _sigmoid(cv)
        g = lax.dot_general(cond, dm_ref[...], (((0,), (0,)), ((), ())), preferred_element_type=f32,
                            precision=lax.Precision.HIGHEST)
        g_ref[...] = g
        d_ref[...], mo_ref[...], vo_ref[...] = _adamw_math(g, w_ref[...], m_ref[...], v_ref[...])

    blk = pl.BlockSpec((None, tr, cols), lambda i, r: (i, r, 0))
    return pl.pallas_call(
        body, name="adamw_w_mod", grid=(depth, d // tr),
        out_shape=tuple(jax.ShapeDtypeStruct(w.shape, f32) for _ in range(4)),
        in_specs=[pl.BlockSpec((N_DEV, tr), lambda i, r: (0, r)),
                  pl.BlockSpec((None, N_DEV, cols), lambda i, r: (i, 0, 0)), blk, blk, blk],
        out_specs=(blk, blk, blk, blk),
        compiler_params=_params(("arbitrary", "arbitrary"), 32),
    )(c_all, dmod_all, w, m, v)


def kernel(x, c, w_mod, b_mod, norm_mix_g, norm_ffn_g, lru_w_y, lru_b_y, lru_w_in, lru_b_in, lru_conv_w, lru_conv_b, lru_w_a, lru_b_a, lru_w_x, lru_b_x, lru_lambda, lru_w_out, lru_b_out, pool_w, pool_scale, ffn_w1, ffn_w2, final_norm_g, loss_target, m_w_mod, m_b_mod, m_norm_mix_g, m_norm_ffn_g, m_lru_w_y, m_lru_b_y, m_lru_w_in, m_lru_b_in, m_lru_conv_w, m_lru_conv_b, m_lru_w_a, m_lru_b_a, m_lru_w_x, m_lru_b_x, m_lru_lambda, m_lru_w_out, m_lru_b_out, m_pool_w, m_pool_scale, m_ffn_w1, m_ffn_w2, m_final_norm_g, v_w_mod, v_b_mod, v_norm_mix_g, v_norm_ffn_g, v_lru_w_y, v_lru_b_y, v_lru_w_in, v_lru_b_in, v_lru_conv_w, v_lru_conv_b, v_lru_w_a, v_lru_b_a, v_lru_w_x, v_lru_b_x, v_lru_lambda, v_lru_w_out, v_lru_b_out, v_pool_w, v_pool_scale, v_ffn_w1, v_ffn_w2, v_final_norm_g):
    me = 4 * lax.axis_index("x") + 2 * lax.axis_index("y") + lax.axis_index("c")
    n_lru = lru_w_y.shape[0]
    shard = LRU_WIDTH // N_DEV
    hshard = HEAD_DIM // N_DEV
    xs = x[0]
    target = loss_target[0]

    small_vecs = jnp.concatenate([
        lru_conv_w.reshape(n_lru * 4, shard), lru_b_a.reshape(n_lru, HEADS * hshard),
        lru_b_x.reshape(n_lru, HEADS * hshard), pool_scale, jnp.zeros((2, shard), f32)], axis=0)
    sv_g, c_g = _exchange([small_vecs, c], True, "gather_cond")
    conv_w_full = sv_g[:, 0:8].reshape(N_DEV, n_lru, 4, shard).transpose(1, 2, 0, 3).reshape(n_lru, 4, LRU_WIDTH)
    b_a_full = sv_g[:, 8:10].reshape(N_DEV, n_lru, HEADS, hshard).transpose(1, 2, 0, 3).reshape(n_lru, LRU_WIDTH)
    b_x_full = sv_g[:, 10:12].reshape(N_DEV, n_lru, HEADS, hshard).transpose(1, 2, 0, 3).reshape(n_lru, LRU_WIDTH)
    ps_full = sv_g[:, 12:14].transpose(1, 0, 2).reshape(n_lru, D_MODEL)
    c_all = c_g.reshape(N_DEV, D_MODEL)

    (mod_g,) = _exchange([_mod_part(c_all, w_mod)], True, "gather_mod", pieces=DEPTH)
    mod_row = lax.dynamic_index_in_dim(mod_g, me, axis=2, keepdims=False)
    mod_row = mod_row.transpose(1, 0, 2).reshape(DEPTH, N_MOD * D_MODEL)
    table, token = _mod_table(mod_row, b_mod, norm_mix_g, norm_ffn_g)

    first_pieces = 4
    (first_mix,), token = _send_start("gather_first_start", [[
        jnp.stack([lru_w_y[0], lru_w_in[0], lru_w_out[0]]).astype(bf16).reshape(3 * first_pieces, -1, LRU_WIDTH),
        jnp.stack([lru_w_a[0], lru_w_x[0]]).astype(bf16).reshape(first_pieces, -1, HEAD_DIM)]], True, me,
        pieces=first_pieces, after=token)

    parts = []
    for i in range(DEPTH):
        j = i // 2
        if i > 0 and i % 2 == 0:
            parts.append([(jnp.stack([lru_w_y[j], lru_w_in[j], lru_w_out[j]]) + token[0, 0]).astype(bf16),
                          (jnp.stack([lru_w_a[j], lru_w_x[j]]) + token[0, 0]).astype(bf16)])
        elif i % 2 == 1:
            parts.append([(pool_w[j] + token[0, 0]).astype(bf16)])
        parts.append([(ffn_w1[i] + token[0, 0]).astype(bf16), (ffn_w2[i] + token[0, 0]).astype(bf16)])
    first_got, token = _send_wait("gather_mix_wait_0", first_mix, [a for part in parts for a in part])
    handles, token = _send_start("gather_rest_start", parts, True, me, after=token)
    h_ffn = [handles[0], handles[2], handles[4], handles[6]]
    h_mix = [None, handles[1], handles[3], handles[5]]

    zero_row = jnp.zeros((1, LRU_WIDTH), f32)
    pvecs = [jnp.concatenate([lru_b_y[j:j + 1], lru_b_in[j:j + 1], lru_conv_b[j:j + 1], b_a_full[j:j + 1],
                              b_x_full[j:j + 1], lru_lambda[j:j + 1], lru_b_out[j:j + 1], zero_row,
                              conv_w_full[j], zero_row, zero_row, zero_row, zero_row], axis=0) for j in range(n_lru)]
    ps_rows = [jnp.concatenate([ps_full[j:j + 1], jnp.zeros((7, D_MODEL), f32)], axis=0) for j in range(n_lru)]

    saved = []
    ffn_w, mix_w = [], []
    h = xs
    for i in range(DEPTH):
        j = i // 2
        got = first_got if i == 0 else _send_wait(f"gather_mix_wait_{i}", h_mix[i], h)[0]
        if i % 2 == 0:
            got = [got[0].reshape(N_DEV, 3, shard, LRU_WIDTH), got[1].reshape(N_DEV, 2, HEADS, hshard, HEAD_DIM)]
            mix_w.append((got[0].transpose(1, 0, 2, 3).reshape(3, LRU_WIDTH, LRU_WIDTH),
                          got[1].transpose(1, 2, 0, 3, 4).reshape(2, HEADS, HEAD_DIM, HEAD_DIM)))
            h_mid, *lru_saved = _lru_fwd(h, table[i] + token[0, 0], mix_w[i][0], mix_w[i][1], pvecs[j], i)
            mix_saved = (h, tuple(lru_saved))
        else:
            mix_w.append((got[0].transpose(1, 0, 2, 3).reshape(HEADS, HEAD_DIM, HEAD_DIM),))
            h_mid, y_mix = _pool_fwd(h, table[i], mix_w[i][0], ps_rows[j], i)
            mix_saved = (h, y_mix)
        ffn_w.append(_send_wait(f"gather_ffn_wait_{i}", h_ffn[i], h_mid)[0])
        h_out, u, y_ffn, hb = _ffn_fwd(h_mid, table[i], ffn_w[i][0], ffn_w[i][1], i)
        saved.append((mix_saved, (h_mid, u, y_ffn, hb)))
        h = h_out
    fin_rows = jnp.concatenate([final_norm_g[None, :], jnp.zeros((7, D_MODEL), f32)], axis=0)
    dx, sm_fin = _final(h, target, fin_rows)

    sm_ffn, sm_mix = [None] * DEPTH, [None] * DEPTH
    x_ffn, x_mix = [None] * DEPTH, [None] * DEPTH
    token = jnp.zeros((8, 128), f32)
    last_mix = None
    for i in reversed(range(DEPTH)):
        j = i // 2
        mix_saved, (h_mid, u, y_ffn, hb) = saved[i]
        dx, da, dyb, sm_ffn[i] = _ffn_bwd_act(h_mid, dx, u, y_ffn, table[i] + token[0, 0], ffn_w[i][0], ffn_w[i][1], i)
        ffn_grads = [_ffn_bwd_w1(hb, da, i), _ffn_bwd_w2(u, dyb, i)]
        if last_mix is None:
            (x_ffn[i],), token = _send_start(f"grads_start_{i}", [ffn_grads], False, me)
        else:
            (x_mix[i + 1], x_ffn[i]), token = _send_start(f"grads_start_{i}", [last_mix, ffn_grads], False, me)
        if i % 2 == 0:
            h_in, lru_saved = mix_saved
            dx, dbig, dsmall, sm_mix[i] = _lru_bwd(
                h_in, dx, lru_saved, table[i] + token[0, 0], mix_w[i][0], mix_w[i][1], pvecs[j], i)
            last_mix = [dbig, dsmall]
        else:
            h_in, y_mix = mix_saved
            dx, dpool, sm = _pool_bwd(h_in, dx, y_mix, table[i] + token[0, 0], mix_w[i][0], ps_rows[j], i)
            sm_mix[i] = jnp.concatenate([sm, jnp.zeros((8, D_MODEL), f32)], axis=0)
            last_mix = [dpool]
    grad_x = dx[None]

    pack = _small_pack(sm_ffn, sm_mix, sm_fin, table + token[0, 0], norm_mix_g, norm_ffn_g, lru_lambda)
    (pack_g,) = _exchange([pack], True, "gather_small_grads", pieces=4)
    tot, token = _small_sum(pack_g)
    loss = tot[K_LOSS, 0]
    (x_mix[0],), _ = _send_start("grads_last_start", [[t + token[0, 0].astype(bf16) for t in last_mix]], False, me)
    cols = w_mod.shape[2]
    dmod_all = lax.dynamic_slice_in_dim(pack_g[:, K_MOD:K_MOD + DEPTH * N_MOD].reshape(N_DEV, DEPTH, N_MOD * D_MODEL),
                                        me * cols, cols, axis=2).transpose(1, 0, 2)
    results = {"w_mod": _adamw_w_mod(c_all, dmod_all, w_mod, m_w_mod, v_w_mod)}

    after = results["w_mod"][1]
    l_ffn = [_send_wait(f"grads_ffn_wait_{i}", x_ffn[i], after)[0] for i in reversed(range(DEPTH))][::-1]

    def reduce_update(name, landings, kind, w, m, v):
        rows = w.size // w.shape[-1]
        two_d = (rows, w.shape[-1])
        lands = [t.reshape(N_DEV, -1, rows // len(landings), w.shape[-1]) for t in landings]
        outs = _adamw_reduce(name, lands, kind, w.reshape(two_d), m.reshape(two_d), v.reshape(two_d))
        return tuple(t.reshape(w.shape) for t in outs)

    results["ffn_w1"] = reduce_update("ffn_w1", [t[0] for t in l_ffn], 0, ffn_w1, m_ffn_w1, v_ffn_w1)
    results["ffn_w2"] = reduce_update("ffn_w2", [t[1] for t in l_ffn], 0, ffn_w2, m_ffn_w2, v_ffn_w2)
    after = results["ffn_w2"][1]
    l_mix = [_send_wait(f"grads_mix_wait_{i}", x_mix[i], after)[0] for i in reversed(range(DEPTH))][::-1]
    l_lru_big = [l_mix[i][0] for i in range(0, DEPTH, 2)]
    l_lru_small = [l_mix[i][1] for i in range(0, DEPTH, 2)]
    l_pool = [l_mix[i][0] for i in range(1, DEPTH, 2)]
    results["lru_w_y"] = reduce_update("lru_w_y", l_lru_big, 0, lru_w_y, m_lru_w_y, v_lru_w_y)
    results["lru_w_in"] = reduce_update("lru_w_in", l_lru_big, 1, lru_w_in, m_lru_w_in, v_lru_w_in)
    results["lru_w_out"] = reduce_update("lru_w_out", l_lru_big, 2, lru_w_out, m_lru_w_out, v_lru_w_out)
    results["lru_w_a"] = reduce_update("lru_w_a", l_lru_small, 0, lru_w_a, m_lru_w_a, v_lru_w_a)
    results["lru_w_x"] = reduce_update("lru_w_x", l_lru_small, 1, lru_w_x, m_lru_w_x, v_lru_w_x)
    results["pool_w"] = reduce_update("pool_w", l_pool, 0, pool_w, m_pool_w, v_pool_w)

    def my_cols(full, width):
        return lax.dynamic_slice_in_dim(full, me * width, width, axis=full.ndim - 1)

    lru_rows = tot[K_LRUB:K_LRUB + 5 * n_lru].reshape(n_lru, 5, LRU_WIDTH)
    small_grads = {
        "b_mod": tot[K_MOD:K_MOD + DEPTH * N_MOD].reshape(DEPTH, N_MOD * D_MODEL),
        "norm_mix_g": tot[K_NMIX:K_NMIX + DEPTH],
        "norm_ffn_g": tot[K_NFFN:K_NFFN + DEPTH],
        "lru_b_y": lru_rows[:, 0], "lru_b_in": lru_rows[:, 1], "lru_conv_b": lru_rows[:, 2],
        "lru_lambda": lru_rows[:, 3], "lru_b_out": lru_rows[:, 4],
        "lru_conv_w": my_cols(tot[K_CONVW:K_CONVW + 4 * n_lru].reshape(n_lru, 4, LRU_WIDTH), shard),
        "lru_b_a": my_cols(tot[K_BA:K_BA + n_lru].reshape(n_lru, HEADS, HEAD_DIM), hshard),
        "lru_b_x": my_cols(tot[K_BX:K_BX + n_lru].reshape(n_lru, HEADS, HEAD_DIM), hshard),
        "pool_scale": my_cols(tot[K_PS:K_PS + n_lru], shard),
        "final_norm_g": tot[K_FIN],
    }
    given = dict(b_mod=(b_mod, m_b_mod, v_b_mod), norm_mix_g=(norm_mix_g, m_norm_mix_g, v_norm_mix_g),
                 norm_ffn_g=(norm_ffn_g, m_norm_ffn_g, v_norm_ffn_g), lru_b_y=(lru_b_y, m_lru_b_y, v_lru_b_y),
                 lru_b_in=(lru_b_in, m_lru_b_in, v_lru_b_in), lru_conv_w=(lru_conv_w, m_lru_conv_w, v_lru_conv_w),
                 lru_conv_b=(lru_conv_b, m_lru_conv_b, v_lru_conv_b), lru_b_a=(lru_b_a, m_lru_b_a, v_lru_b_a),
                 lru_b_x=(lru_b_x, m_lru_b_x, v_lru_b_x), lru_lambda=(lru_lambda, m_lru_lambda, v_lru_lambda),
                 lru_b_out=(lru_b_out, m_lru_b_out, v_lru_b_out), pool_scale=(pool_scale, m_pool_scale, v_pool_scale),
                 final_norm_g=(final_norm_g, m_final_norm_g, v_final_norm_g))
    updates = _adamw_small([(g,) + given[name] for name, g in small_grads.items()])
    for (name, g), update in zip(small_grads.items(), updates):
        results[name] = (g,) + update

    order = ["w_mod", "b_mod", "norm_mix_g", "norm_ffn_g", "lru_w_y", "lru_b_y", "lru_w_in", "lru_b_in", "lru_conv_w",
             "lru_conv_b", "lru_w_a", "lru_b_a", "lru_w_x", "lru_b_x", "lru_lambda", "lru_w_out", "lru_b_out", "pool_w",
             "pool_scale", "ffn_w1", "ffn_w2", "final_norm_g"]
    return (loss, grad_x, *[results[n][0] for n in order], *[results[n][1] for n in order],
            *[results[n][2] for n in order], *[results[n][3] for n in order])
```

```python
import math

import jax
import jax.numpy as jnp
from jax import lax
from jax.experimental import pallas as pl
from jax.experimental.pallas import tpu as pltpu

f32, bf16 = jnp.float32, jnp.bfloat16

D_MODEL = 1024
LRU_WIDTH = 1024
HEADS = 4
HEAD_DIM = 256
D_FF = 4096
DEPTH = 4
N_MOD = 6
N_DEV = 8
FF_CHUNK = D_FF // N_DEV
POOL_WINDOWS = (2, 4, 8, 16)
POOL_HALO = 16
EPS = 1e-6
LRU_C = 8.0

ADAM_LR = 0.001
ADAM_B1 = 0.9
ADAM_B2 = 0.999
ADAM_EPS = 1e-08
ADAM_WD = 0.01
ADAM_STEP = 10

SUBLANES = 8
BF16_ROWS = 16

R_SH_M, R_SC_M, R_GT_M, R_SH_F, R_SC_F, R_GT_F, R_GS_M, R_GS_F = range(8)
P_BY, P_BIN, P_CONVB, P_BA, P_BX, P_LAM, P_BOUT, P_CW0 = 0, 1, 2, 3, 4, 5, 6, 8
G_SH, G_GS, G_GT, G_BY, G_BIN, G_CONVB, G_BA, G_BX, G_LS, G_BOUT, G_CW0 = 0, 1, 2, 3, 4, 5, 6, 7, 8, 9, 10
K_MOD, K_NMIX, K_NFFN, K_LRUB, K_CONVW, K_BA, K_BX, K_PS, K_FIN, K_LOSS, K_ROWS = 0, 24, 28, 32, 42, 50, 52, 54, 56, 57, 64


def _params(semantics=None, vmem_mb=56):
    return pltpu.CompilerParams(dimension_semantics=semantics, vmem_limit_bytes=vmem_mb * 1024 * 1024)


def _mod_rows(layer):
    return pl.BlockSpec((None, 8, D_MODEL), lambda *_: (layer, 0, 0))


def _behind(body, n_in, dep):
    if dep is None:
        return body, [], []

    def run(*refs):
        body(*refs[:n_in], *refs[n_in + 1:])

    return run, [pl.BlockSpec(memory_space=pl.ANY)], [dep]


def _mm(a, b):
    return jnp.dot(a, b, preferred_element_type=f32)


def _mm_nt(a, b):
    return lax.dot_general(a, b, (((1,), (1,)), ((), ())), preferred_element_type=f32)


def _mm_tn(a, b):
    return lax.dot_general(a, b, (((0,), (0,)), ((), ())), preferred_element_type=f32)


def _rms(x):
    r = lax.rsqrt(jnp.mean(x * x, axis=-1, keepdims=True) + EPS)
    return x * r, r


def _norm_bwd(dh, n, r, gs):
    dn = dh * gs
    return r * (dn - n * jnp.mean(dn * n, axis=-1, keepdims=True))


def _colsum(v):
    return jnp.sum(v, axis=0, keepdims=True)


def _sigmoid(v):
    return 0.5 * jnp.tanh(0.5 * v) + 0.5


def _log_sigmoid(v):
    return jnp.minimum(v, 0.0) - jnp.log1p(jnp.exp(-jnp.abs(v)))


_GELU_C = 0.7978845608028654
_GELU_A = 0.044715


def _gelu_and_grad(v):
    v2 = v * v
    t = jnp.tanh(_GELU_C * v * (1.0 + _GELU_A * v2))
    p = 0.5 + 0.5 * t
    return v * p, p + (0.5 * v) * (1.0 - t * t) * (_GELU_C + (3.0 * _GELU_A * _GELU_C) * v2)


def _rows_before(halo, v, shifts):
    hr = halo.shape[0]
    ext = jnp.concatenate([halo, v], axis=0)
    return [pltpu.roll(ext, k, 0)[hr:] for k in shifts]


def _rows_after(v, halo, shifts):
    n = v.shape[0]
    ext = jnp.concatenate([v, halo], axis=0)
    return [pltpu.roll(ext, ext.shape[0] - k, 0)[:n] for k in shifts]


def _shift_matrix(n, halo_rows, shifts):
    rows = lax.broadcasted_iota(jnp.int32, (n, n + halo_rows), 0)
    cols = lax.broadcasted_iota(jnp.int32, (n, n + halo_rows), 1)
    return jnp.concatenate([(cols == rows + halo_rows - k).astype(bf16) for k in shifts], axis=0)


def _shifted_rows(sel, halo, v):
    n = v.shape[0]
    out = _mm(sel, jnp.concatenate([halo, v], axis=0))
    return [out[j * n:(j + 1) * n] for j in range(sel.shape[0] // n)]


def _block_diag(v, w_ref, kind):
    return jnp.concatenate(
        [_mm(v[:, h * HEAD_DIM:(h + 1) * HEAD_DIM], w_ref[kind, h]) for h in range(HEADS)], axis=1)


def _block_diag_t(v, w_ref, kind):
    return jnp.concatenate(
        [_mm_nt(v[:, h * HEAD_DIM:(h + 1) * HEAD_DIM], w_ref[kind, h]) for h in range(HEADS)], axis=1)


def _exchange(arrays, gather, name, pieces=1):
    n = len(arrays)
    peers = N_DEV - 1

    def body(*refs):
        ins, outs = refs[:n], refs[n:2 * n]
        send_sems, recv_sems, local_sems = refs[2 * n:]
        x, y, c = lax.axis_index("x"), lax.axis_index("y"), lax.axis_index("c")
        me = 4 * x + 2 * y + c
        local = []
        for k in range(n):
            cp = pltpu.make_async_copy(ins[k] if gather else ins[k].at[me], outs[k].at[me], local_sems.at[k])
            cp.start()
            local.append(cp)
        remote = _peer_copies(ins, outs, send_sems, recv_sems, gather, pieces)
        for cp in remote:
            cp.start()
        for cp in remote:
            cp.wait()
        for cp in local:
            cp.wait()

    out_shape = tuple(
        jax.ShapeDtypeStruct(((N_DEV,) + a.shape) if gather else a.shape, a.dtype) for a in arrays)
    outs = pl.pallas_call(
        body, name=name, out_shape=out_shape,
        in_specs=[pl.BlockSpec(memory_space=pl.ANY)] * n,
        out_specs=tuple(pl.BlockSpec(memory_space=pl.ANY) for _ in range(n)),
        scratch_shapes=[pltpu.SemaphoreType.DMA((n * pieces * peers,)), pltpu.SemaphoreType.DMA((n * pieces * peers,)),
                        pltpu.SemaphoreType.DMA((n,))],
        compiler_params=pltpu.CompilerParams(has_side_effects=True),
    )(*arrays)
    return list(outs)


_HBM = pl.BlockSpec(memory_space=pltpu.HBM)
_SEM = pl.BlockSpec(memory_space=pltpu.SEMAPHORE)
_DATAFLOW = pltpu.SideEffectType.DATAFLOW_SIDE_EFFECTING


def _peer_copies(src_refs, land_refs, send_sems, recv_sems, gather, pieces=1):
    x, y, c = lax.axis_index("x"), lax.axis_index("y"), lax.axis_index("c")
    me = 4 * x + 2 * y + c
    peers = N_DEV - 1
    copies = []
    for p in range(1, N_DEV):
        px = 1 - x if p & 4 else x
        py = 1 - y if p & 2 else y
        pc = 1 - c if p & 1 else c
        for k in range(len(src_refs)):
            block = src_refs[k] if gather else src_refs[k].at[4 * px + 2 * py + pc]
            dst = land_refs[k].at[me]
            rows = block.shape[0] // pieces
            for r in range(pieces):
                part = pl.ds(r * rows, rows)
                sem = (k * pieces + r) * peers + p - 1
                copies.append(pltpu.make_async_remote_copy(
                    src_ref=block.at[part] if pieces > 1 else block, dst_ref=dst.at[part] if pieces > 1 else dst,
                    send_sem=send_sems.at[sem], recv_sem=recv_sems.at[sem],
                    device_id=(px, py, pc), device_id_type=pl.DeviceIdType.MESH))
    return copies


def _landing(srcs, gather, me):
    out = []
    for a in srcs:
        own = a if gather else lax.dynamic_index_in_dim(a, me, 0, keepdims=False)
        out.append(lax.dynamic_update_index_in_dim(lax.empty((N_DEV,) + own.shape, own.dtype), own, me, 0))
    return out


def _send_start(name, groups, gather, me, pieces=1, after=None):
    sizes = [len(g) for g in groups]
    srcs = [a for g in groups for a in g]
    n = len(srcs)
    lands = _landing(srcs, gather, me)
    ng = len(groups)
    first = [sum(sizes[:g]) for g in range(ng)]
    extra = [] if after is None else [after]

    def body(*refs):
        src_refs, land_refs = refs[:n], refs[n:2 * n]
        sems, token = refs[2 * n + len(extra):2 * n + len(extra) + 2 * ng], refs[-1]
        for g in range(ng):
            part = slice(first[g], first[g] + sizes[g])
            for cp in _peer_copies(src_refs[part], land_refs[part], sems[2 * g], sems[2 * g + 1], gather, pieces):
                cp.start()
        token[...] = jnp.zeros_like(token)

    sem_shapes = [pltpu.SemaphoreType.DMA((sizes[g // 2] * pieces * (N_DEV - 1),)) for g in range(2 * ng)]
    outs = pl.pallas_call(
        body, name=name,
        out_shape=(*sem_shapes, *[pltpu.HBM(a.shape, a.dtype) for a in (*srcs, *lands)], jax.ShapeDtypeStruct((8, 128), f32)),
        in_specs=[_HBM] * (2 * n) + [pl.BlockSpec(memory_space=pl.ANY)] * len(extra),
        out_specs=(*[_SEM] * (2 * ng), *[_HBM] * (2 * n), pl.BlockSpec(memory_space=pltpu.VMEM)),
        input_output_aliases={k: 2 * ng + k for k in range(2 * n)},
        compiler_params=pltpu.CompilerParams(has_side_effects=_DATAFLOW),
    )(*[pltpu.with_memory_space_constraint(a, pltpu.HBM) for a in (*srcs, *lands)], *extra)
    srcs_thru, lands_thru = outs[2 * ng:2 * ng + n], outs[2 * ng + n:2 * ng + 2 * n]
    handles = [(outs[2 * g], outs[2 * g + 1], list(srcs_thru[first[g]:first[g] + sizes[g]]),
                list(lands_thru[first[g]:first[g] + sizes[g]]), gather, pieces) for g in range(ng)]
    return handles, outs[-1]


def _send_wait(name, handle, after):
    send_sems, recv_sems, srcs, lands, gather, pieces = handle
    n = len(srcs)
    after = list(after) if isinstance(after, (list, tuple)) else [after]

    def body(*refs):
        src_refs, land_refs = refs[:n], refs[n:2 * n]
        for cp in _peer_copies(src_refs, land_refs, refs[2 * n], refs[2 * n + 1], gather, pieces):
            cp.wait_send()
            cp.wait_recv()
        refs[-1][...] = jnp.zeros_like(refs[-1])

    outs = pl.pallas_call(
        body, name=name,
        out_shape=(*[pltpu.HBM(a.shape, a.dtype) for a in (*srcs, *lands)], jax.ShapeDtypeStruct((8, 128), f32)),
        in_specs=[_HBM] * (2 * n) + [_SEM, _SEM] + [pl.BlockSpec(memory_space=pl.ANY)] * len(after),
        out_specs=(*[_HBM] * (2 * n), pl.BlockSpec(memory_space=pltpu.VMEM)),
        input_output_aliases={k: k for k in range(2 * n)},
        compiler_params=pltpu.CompilerParams(has_side_effects=_DATAFLOW),
    )(*srcs, *lands, send_sems, recv_sems, *after)
    return list(outs[n:2 * n]), outs[-1]


def _mod_part(c_all, w_mod):
    depth, d, cols = w_mod.shape

    def body(c_ref, w_ref, o_ref):
        cv = c_ref[...]
        cond = cv * _sigmoid(cv)
        o_ref[...] = jnp.dot(cond, w_ref[...], preferred_element_type=f32, precision=lax.Precision.HIGHEST)

    return pl.pallas_call(
        body, name="mod_part", grid=(depth,),
        out_shape=jax.ShapeDtypeStruct((depth, N_DEV, cols), f32),
        in_specs=[pl.BlockSpec((N_DEV, d), lambda i: (0, 0)), pl.BlockSpec((None, d, cols), lambda i: (i, 0, 0))],
        out_specs=pl.BlockSpec((None, N_DEV, cols), lambda i: (i, 0, 0)),
        compiler_params=_params(("arbitrary",), 32),
    )(c_all, w_mod)


def _mod_table(mod_row, b_mod, g_mix, g_ffn):
    def body(m_ref, b_ref, gm_ref, gf_ref, o_ref, token_ref):
        for i in range(DEPTH):
            for k in range(N_MOD):
                o_ref[i, k:k + 1, :] = m_ref[i:i + 1, k * D_MODEL:(k + 1) * D_MODEL] + b_ref[i:i + 1, k * D_MODEL:(k + 1) * D_MODEL]
            o_ref[i, R_GS_M:R_GS_M + 1, :] = gm_ref[i:i + 1, :] * (1.0 + o_ref[i, R_SC_M:R_SC_M + 1, :])
            o_ref[i, R_GS_F:R_GS_F + 1, :] = gf_ref[i:i + 1, :] * (1.0 + o_ref[i, R_SC_F:R_SC_F + 1, :])
        token_ref[...] = jnp.zeros_like(token_ref)

    return pl.pallas_call(
        body, name="mod_table",
        out_shape=(jax.ShapeDtypeStruct((DEPTH, 8, D_MODEL), f32), jax.ShapeDtypeStruct((8, 128), f32)))(
        mod_row, b_mod, g_mix, g_ffn)


def _ffn_tile(s):
    return min(512, s)


def _layer_weights(shape):
    return pl.BlockSpec((N_DEV,) + shape, lambda i: (0, 0, 0))


def _ffn_fwd(x, vec, w1g, w2g, layer):
    s = x.shape[0]
    ts = _ffn_tile(s)

    def body(x_ref, vec_ref, w1_ref, w2_ref, xo_ref, u_ref, y_ref, hb_ref):
        xv = x_ref[...]
        n, _ = _rms(xv)
        hb = (n * vec_ref[R_GS_F:R_GS_F + 1, :] + vec_ref[R_SH_F:R_SH_F + 1, :]).astype(bf16)
        hb_ref[...] = hb
        yv = jnp.zeros((ts, D_MODEL), f32)
        for f in range(N_DEV):
            u = jnp.maximum(_mm(hb, w1_ref[f]), 0.0)
            u_ref[:, f * FF_CHUNK:(f + 1) * FF_CHUNK] = u.astype(bf16)
            yv = yv + _mm((u * u).astype(bf16), w2_ref[f])
        y_ref[...] = yv.astype(bf16)
        xo_ref[...] = xv + vec_ref[R_GT_F:R_GT_F + 1, :] * yv

    row = pl.BlockSpec((ts, D_MODEL), lambda i: (i, 0))
    return pl.pallas_call(
        body, name=f"ffn_fwd_{layer}", grid=(s // ts,),
        out_shape=(jax.ShapeDtypeStruct((s, D_MODEL), f32), jax.ShapeDtypeStruct((s, D_FF), bf16),
                   jax.ShapeDtypeStruct((s, D_MODEL), bf16), jax.ShapeDtypeStruct((s, D_MODEL), bf16)),
        in_specs=[row, _mod_rows(layer),
                  _layer_weights((D_MODEL, FF_CHUNK)), _layer_weights((FF_CHUNK, D_MODEL))],
        out_specs=(row, pl.BlockSpec((ts, D_FF), lambda i: (i, 0)), row, row),
        compiler_params=_params(("arbitrary",), 56),
    )(x, vec, w1g, w2g)


def _ffn_bwd_act(x, dx, u, y, vec, w1g, w2g, layer, dep=None):
    s = x.shape[0]
    ts = _ffn_tile(s)

    def body(x_ref, dx_ref, u_ref, y_ref, vec_ref, w1_ref, w2_ref, dxo_ref, da_ref, dyb_ref, sm_ref):
        @pl.when(pl.program_id(0) == 0)
        def _():
            sm_ref[...] = jnp.zeros_like(sm_ref)

        dxv = dx_ref[...]
        dyb = (dxv * vec_ref[R_GT_F:R_GT_F + 1, :]).astype(bf16)
        dyb_ref[...] = dyb
        sm_ref[G_GT:G_GT + 1, :] += _colsum(dxv * y_ref[...].astype(f32))
        dh = jnp.zeros((ts, D_MODEL), f32)
        for f in range(N_DEV):
            cols = slice(f * FF_CHUNK, (f + 1) * FF_CHUNK)
            dz = _mm_nt(dyb, w2_ref[f])
            dab = (dz * (2.0 * u_ref[:, cols].astype(f32))).astype(bf16)
            da_ref[:, cols] = dab
            dh = dh + _mm_nt(dab, w1_ref[f])
        n, r = _rms(x_ref[...])
        sm_ref[G_SH:G_SH + 1, :] += _colsum(dh)
        sm_ref[G_GS:G_GS + 1, :] += _colsum(dh * n)
        dxo_ref[...] = dxv + _norm_bwd(dh, n, r, vec_ref[R_GS_F:R_GS_F + 1, :])

    row = pl.BlockSpec((ts, D_MODEL), lambda i: (i, 0))
    wide = pl.BlockSpec((ts, D_FF), lambda i: (i, 0))
    body, dep_specs, dep_args = _behind(body, 7, dep)
    return pl.pallas_call(
        body, name=f"ffn_bwd_act_{layer}", grid=(s // ts,),
        out_shape=(jax.ShapeDtypeStruct((s, D_MODEL), f32), jax.ShapeDtypeStruct((s, D_FF), bf16),
                   jax.ShapeDtypeStruct((s, D_MODEL), bf16), jax.ShapeDtypeStruct((8, D_MODEL), f32)),
        in_specs=[row, row, wide, row, _mod_rows(layer),
                  _layer_weights((D_MODEL, FF_CHUNK)), _layer_weights((FF_CHUNK, D_MODEL))] + dep_specs,
        out_specs=(row, wide, row, pl.BlockSpec((8, D_MODEL), lambda i: (0, 0))),
        compiler_params=_params(("arbitrary",), 58),
    )(x, dx, u, y, vec, w1g, w2g, *dep_args)


def _ffn_bwd_w1(hb, da, layer):
    s = hb.shape[0]
    ts = _ffn_tile(s)
    nt = s // ts

    def body(hb_ref, da_ref, dw_ref, acc_ref):
        i = pl.program_id(0)

        @pl.when(i == 0)
        def _():
            acc_ref[...] = jnp.zeros_like(acc_ref)

        hb = hb_ref[...]
        for f in range(N_DEV):
            acc_ref[f] += _mm_tn(hb, da_ref[:, f * FF_CHUNK:(f + 1) * FF_CHUNK])

        @pl.when(i == nt - 1)
        def _():
            dw_ref[...] = acc_ref[...].astype(bf16)

    return pl.pallas_call(
        body, name=f"ffn_bwd_w1_{layer}", grid=(nt,),
        out_shape=jax.ShapeDtypeStruct((N_DEV, D_MODEL, FF_CHUNK), bf16),
        in_specs=[pl.BlockSpec((ts, D_MODEL), lambda i: (i, 0)), pl.BlockSpec((ts, D_FF), lambda i: (i, 0))],
        out_specs=pl.BlockSpec((N_DEV, D_MODEL, FF_CHUNK), lambda i: (0, 0, 0)),
        scratch_shapes=[pltpu.VMEM((N_DEV, D_MODEL, FF_CHUNK), f32)],
        compiler_params=_params(("arbitrary",), 56),
    )(hb, da)


def _ffn_bwd_w2(u, dyb, layer):
    s = u.shape[0]
    ts = _ffn_tile(s)
    nt = s // ts

    def body(u_ref, dyb_ref, dw_ref, acc_ref):
        i = pl.program_id(0)

        @pl.when(i == 0)
        def _():
            acc_ref[...] = jnp.zeros_like(acc_ref)

        dyb = dyb_ref[...]
        for f in range(N_DEV):
            uv = u_ref[:, f * FF_CHUNK:(f + 1) * FF_CHUNK].astype(f32)
            acc_ref[f] += _mm_tn((uv * uv).astype(bf16), dyb)

        @pl.when(i == nt - 1)
        def _():
            dw_ref[...] = acc_ref[...].astype(bf16)

    return pl.pallas_call(
        body, name=f"ffn_bwd_w2_{layer}", grid=(nt,),
        out_shape=jax.ShapeDtypeStruct((N_DEV, FF_CHUNK, D_MODEL), bf16),
        in_specs=[pl.BlockSpec((ts, D_FF), lambda i: (i, 0)), pl.BlockSpec((ts, D_MODEL), lambda i: (i, 0))],
        out_specs=pl.BlockSpec((N_DEV, FF_CHUNK, D_MODEL), lambda i: (0, 0, 0)),
        scratch_shapes=[pltpu.VMEM((N_DEV, FF_CHUNK, D_MODEL), f32)],
        compiler_params=_params(("arbitrary",), 56),
    )(u, dyb)


def _lru_gates(xc, wsm_ref, pv_ref):
    xcb = xc.astype(bf16)
    gr = _sigmoid(_block_diag(xcb, wsm_ref, 0) + pv_ref[P_BA:P_BA + 1, :])
    gi = _sigmoid(_block_diag(xcb, wsm_ref, 1) + pv_ref[P_BX:P_BX + 1, :])
    log_a = (LRU_C * _log_sigmoid(pv_ref[P_LAM:P_LAM + 1, :])) * gr
    t = jnp.tanh(log_a)
    return gr, gi, jnp.exp(log_a), jnp.sqrt((-2.0 * t) / (1.0 - t))


def _conv(xr, taps_before, pv_ref):
    xc = xr * pv_ref[P_CW0 + 3:P_CW0 + 4, :] + pv_ref[P_CONVB:P_CONVB + 1, :]
    for k, v in zip((2, 1, 0), taps_before):
        xc = xc + v * pv_ref[P_CW0 + k:P_CW0 + k + 1, :]
    return xc


LRU_FWD_SUB, LRU_FWD_SUBS = 128, 4
LRU_BWD_SUB, LRU_BWD_SUBS = 256, 1


def _scan_rows(a, u, carry, reverse):
    groups = a.shape[0] // SUBLANES
    row = lax.broadcasted_iota(jnp.int32, (SUBLANES, a.shape[1]), 0)
    outs = [None] * groups
    for j in range(groups):
        g = groups - 1 - j if reverse else j
        av, uv = a[g * SUBLANES:(g + 1) * SUBLANES], u[g * SUBLANES:(g + 1) * SUBLANES]
        for k in (1, 2, 4):
            if reverse:
                valid, shift = row < SUBLANES - k, SUBLANES - k
            else:
                valid, shift = row >= k, k
            a_s = jnp.where(valid, pltpu.roll(av, shift, 0), 1.0)
            u_s = jnp.where(valid, pltpu.roll(uv, shift, 0), 0.0)
            uv = uv + av * u_s
            av = av * a_s
        h = uv + av * carry
        outs[g] = h
        carry = h[0:1, :] if reverse else h[SUBLANES - 1:SUBLANES, :]
    return jnp.concatenate(outs, axis=0), carry


def _lru_fwd(x, vec, wbig, wsm, pvec, layer, dep=None):
    s = x.shape[0]
    sub = min(LRU_FWD_SUB, s)
    ts = min(sub * LRU_FWD_SUBS, s)
    nsub = ts // sub
    w = LRU_WIDTH

    def body(x_ref, vec_ref, wb_ref, wsm_ref, pv_ref, xo_ref, xr_ref, hs_ref, a_ref, mult_ref, gr_ref, gi_ref,
             gel_ref, geld_ref, y_ref, tail_ref, carry_ref):
        @pl.when(pl.program_id(0) == 0)
        def _():
            tail_ref[...] = jnp.zeros_like(tail_ref)
            carry_ref[...] = jnp.zeros_like(carry_ref)

        sel = _shift_matrix(sub, BF16_ROWS, (1, 2, 3))
        for k in range(nsub):
            rows = slice(k * sub, (k + 1) * sub)
            xv = x_ref[rows, :]
            n, _ = _rms(xv)
            hb = (n * vec_ref[R_GS_M:R_GS_M + 1, :] + vec_ref[R_SH_M:R_SH_M + 1, :]).astype(bf16)
            gelu_v, gelu_d = _gelu_and_grad(_mm(hb, wb_ref[0]) + pv_ref[P_BY:P_BY + 1, :])
            gel_ref[rows, :] = gelu_v.astype(bf16)
            geld_ref[rows, :] = gelu_d.astype(bf16)
            xrb = (_mm(hb, wb_ref[1]) + pv_ref[P_BIN:P_BIN + 1, :]).astype(bf16)
            xr_ref[rows, :] = xrb
            xc = _conv(xrb.astype(f32), _shifted_rows(sel, tail_ref[...], xrb), pv_ref)
            tail_ref[...] = xrb[sub - BF16_ROWS:, :]
            gr, gi, a, mult = _lru_gates(xc, wsm_ref, pv_ref)
            gr_ref[rows, :] = gr.astype(bf16)
            gi_ref[rows, :] = gi.astype(bf16)
            a_ref[rows, :] = a
            mult_ref[rows, :] = mult
            hs, carry = _scan_rows(a, mult * (gi * xc), carry_ref[0:1, :], reverse=False)
            carry_ref[0:1, :] = carry
            hs_ref[rows, :] = hs
            yv = _mm((hs * gelu_v).astype(bf16), wb_ref[2]) + pv_ref[P_BOUT:P_BOUT + 1, :]
            y_ref[rows, :] = yv.astype(bf16)
            xo_ref[rows, :] = xv + vec_ref[R_GT_M:R_GT_M + 1, :] * yv

    row = pl.BlockSpec((ts, D_MODEL), lambda i: (i, 0))
    roww = pl.BlockSpec((ts, w), lambda i: (i, 0))
    wide = lambda dt: jax.ShapeDtypeStruct((s, w), dt)
    body, dep_specs, dep_args = _behind(body, 5, dep)
    return pl.pallas_call(
        body, name=f"lru_fwd_{layer}", grid=(s // ts,),
        out_shape=(jax.ShapeDtypeStruct((s, D_MODEL), f32), wide(bf16), wide(f32), wide(f32), wide(f32),
                   wide(bf16), wide(bf16), wide(bf16), wide(bf16), jax.ShapeDtypeStruct((s, D_MODEL), bf16)),
        in_specs=[row, _mod_rows(layer),
                  pl.BlockSpec((3, w, w), lambda i: (0, 0, 0)),
                  pl.BlockSpec((2, HEADS, HEAD_DIM, HEAD_DIM), lambda i: (0, 0, 0, 0)),
                  pl.BlockSpec((16, w), lambda i: (0, 0))] + dep_specs,
        out_specs=(row, roww, roww, roww, roww, roww, roww, roww, roww, row),
        scratch_shapes=[pltpu.VMEM((BF16_ROWS, w), bf16), pltpu.VMEM((SUBLANES, w), f32)],
        compiler_params=_params(("arbitrary",)),
    )(x, vec, wbig, wsm, pvec, *dep_args)


def _lru_bwd(x, dx, saved, vec, wbig, wsm, pvec, layer, dep=None):
    xr, hs, a_all, mult_all, gr_all, gi_all, gel_all, geld_all, y = saved
    s = x.shape[0]
    sub = min(LRU_BWD_SUB, s)
    ts = min(sub * LRU_BWD_SUBS, s)
    nsub = ts // sub
    nt = s // ts
    w = LRU_WIDTH
    shard = w // N_DEV
    hshard = HEAD_DIM // N_DEV

    def body(x_ref, dx_ref, xr_ref, xrh_ref, hs_ref, hsh_ref, a_ref, mult_ref, gr_ref, gi_ref, gel_ref, geld_ref,
             y_ref, vec_ref, wb_ref, wsm_ref, pv_ref,
             dxo_ref, dwb_ref, dwsm_ref, sm_ref, accb_ref, accs_ref, eps_ref, dxc8_ref,
             hb_scr, dgb_scr, dxrb_scr, mb_scr, dyb_scr, xcb_scr, drab_scr, drxb_scr):
        i = pl.program_id(0)
        first_tile = i == nt - 1

        @pl.when(i == 0)
        def _():
            accb_ref[...] = jnp.zeros_like(accb_ref)
            accs_ref[...] = jnp.zeros_like(accs_ref)
            sm_ref[...] = jnp.zeros_like(sm_ref)
            eps_ref[...] = jnp.zeros_like(eps_ref)
            dxc8_ref[...] = jnp.zeros_like(dxc8_ref)

        gs = vec_ref[R_GS_M:R_GS_M + 1, :]
        c_ls = LRU_C * _log_sigmoid(pv_ref[P_LAM:P_LAM + 1, :])
        for k in reversed(range(nsub)):
            rows = slice(k * sub, (k + 1) * sub)
            xv = x_ref[rows, :]
            dxv = dx_ref[rows, :]
            n, r = _rms(xv)
            hb_scr[rows, :] = (n * gs + vec_ref[R_SH_M:R_SH_M + 1, :]).astype(bf16)
            xrv = xr_ref[rows, :].astype(f32)
            hsv = hs_ref[rows, :]
            if k == 0:
                xr_halo = jnp.where(first_tile, 0.0, xrh_ref[...].astype(f32))
                hs_halo = jnp.where(first_tile, 0.0, hsh_ref[...])
            else:
                xr_halo = xr_ref[k * sub - BF16_ROWS:k * sub, :].astype(f32)
                hs_halo = hs_ref[k * sub - SUBLANES:k * sub, :]
            xs1, xs2, xs3 = _rows_before(xr_halo, xrv, (1, 2, 3))
            xc = _conv(xrv, (xs1, xs2, xs3), pv_ref)
            xcb_scr[rows, :] = xc.astype(bf16)
            a, mult = a_ref[rows, :], mult_ref[rows, :]
            gr, gi = gr_ref[rows, :].astype(f32), gi_ref[rows, :].astype(f32)
            gelu_v = gel_ref[rows, :].astype(f32)

            dy = dxv * vec_ref[R_GT_M:R_GT_M + 1, :]
            dyb = dy.astype(bf16)
            dyb_scr[rows, :] = dyb
            sm_ref[G_GT:G_GT + 1, :] += _colsum(dxv * y_ref[rows, :].astype(f32))
            sm_ref[G_BOUT:G_BOUT + 1, :] += _colsum(dy)
            mb_scr[rows, :] = (hsv * gelu_v).astype(bf16)
            dm = _mm_nt(dyb, wb_ref[2])
            dhs = dm * gelu_v
            dgpre = dm * hsv * geld_ref[rows, :].astype(f32)
            dgb = dgpre.astype(bf16)
            dgb_scr[rows, :] = dgb
            sm_ref[G_BY:G_BY + 1, :] += _colsum(dgpre)

            eps_in = eps_ref[0:1, :]
            eps, eps_out = _scan_rows(a, a * dhs, eps_in, reverse=True)
            eps_ref[0:1, :] = eps_out
            (eps_next,) = _rows_after(eps, jnp.broadcast_to(eps_in, (SUBLANES, w)), (1,))
            delta = dhs + eps_next
            (h_prev,) = _rows_before(hs_halo, hsv, (1,))
            dxi = delta * xc
            dgi = dxi * mult
            dla = (delta * h_prev) * a - (dxi * gi) * (a * a) / mult
            sm_ref[G_LS:G_LS + 1, :] += _colsum(dla * gr)
            dra = (dla * c_ls) * (gr - gr * gr)
            drx = dgi * (gi - gi * gi)
            drab, drxb = dra.astype(bf16), drx.astype(bf16)
            drab_scr[rows, :] = drab
            drxb_scr[rows, :] = drxb
            sm_ref[G_BA:G_BA + 1, :] += _colsum(dra)
            sm_ref[G_BX:G_BX + 1, :] += _colsum(drx)
            dxc = (delta * mult) * gi + _block_diag_t(drab, wsm_ref, 0) + _block_diag_t(drxb, wsm_ref, 1)

            sm_ref[G_CONVB:G_CONVB + 1, :] += _colsum(dxc)
            for kk, v in zip((3, 2, 1, 0), (xrv, xs1, xs2, xs3)):
                sm_ref[G_CW0 + kk:G_CW0 + kk + 1, :] += _colsum(dxc * v)
            ups = _rows_after(dxc, dxc8_ref[...], (1, 2, 3))
            dxc8_ref[...] = dxc[0:SUBLANES, :]
            dxr = dxc * pv_ref[P_CW0 + 3:P_CW0 + 4, :]
            for kk, v in zip((2, 1, 0), ups):
                dxr = dxr + v * pv_ref[P_CW0 + kk:P_CW0 + kk + 1, :]
            dxrb = dxr.astype(bf16)
            dxrb_scr[rows, :] = dxrb
            sm_ref[G_BIN:G_BIN + 1, :] += _colsum(dxr)
            dh = _mm_nt(dgb, wb_ref[0]) + _mm_nt(dxrb, wb_ref[1])
            sm_ref[G_SH:G_SH + 1, :] += _colsum(dh)
            sm_ref[G_GS:G_GS + 1, :] += _colsum(dh * n)
            dxo_ref[rows, :] = dxv + _norm_bwd(dh, n, r, gs)

        hb = hb_scr[...]
        accb_ref[0] += _mm_tn(hb, dgb_scr[...])
        accb_ref[1] += _mm_tn(hb, dxrb_scr[...])
        accb_ref[2] += _mm_tn(mb_scr[...], dyb_scr[...])
        for h in range(HEADS):
            cols = slice(h * HEAD_DIM, (h + 1) * HEAD_DIM)
            accs_ref[0, h] += _mm_tn(xcb_scr[:, cols], drab_scr[:, cols])
            accs_ref[1, h] += _mm_tn(xcb_scr[:, cols], drxb_scr[:, cols])

        @pl.when(i == nt - 1)
        def _():
            sm_ref[G_LS:G_LS + 1, :] = sm_ref[G_LS:G_LS + 1, :] * LRU_C
            for k in range(3):
                dwb_ref[:, k] = accb_ref[k].astype(bf16).reshape(N_DEV, shard, w)
            for k in range(2):
                for h in range(HEADS):
                    dwsm_ref[:, k, h] = accs_ref[k, h].astype(bf16).reshape(N_DEV, hshard, HEAD_DIM)

    rev = lambda i: (nt - 1 - i, 0)
    row = pl.BlockSpec((ts, D_MODEL), rev)
    roww = pl.BlockSpec((ts, w), rev)
    halo16 = pl.BlockSpec((BF16_ROWS, w), lambda i: (jnp.maximum((nt - 1 - i) * (ts // BF16_ROWS) - 1, 0), 0))
    halo8 = pl.BlockSpec((SUBLANES, w), lambda i: (jnp.maximum((nt - 1 - i) * (ts // SUBLANES) - 1, 0), 0))
    const = lambda *shape: pl.BlockSpec(shape, lambda i: (0,) * len(shape))
    operand = pltpu.VMEM((ts, w), bf16)
    body, dep_specs, dep_args = _behind(body, 17, dep)
    return pl.pallas_call(
        body, name=f"lru_bwd_{layer}", grid=(nt,),
        out_shape=(jax.ShapeDtypeStruct((s, D_MODEL), f32),
                   jax.ShapeDtypeStruct((N_DEV, 3, shard, w), bf16),
                   jax.ShapeDtypeStruct((N_DEV, 2, HEADS, hshard, HEAD_DIM), bf16),
                   jax.ShapeDtypeStruct((16, w), f32)),
        in_specs=[row, row, roww, halo16, roww, halo8, roww, roww, roww, roww, roww, roww, row, _mod_rows(layer),
                  const(3, w, w), const(2, HEADS, HEAD_DIM, HEAD_DIM), const(16, w)] + dep_specs,
        out_specs=(row, const(N_DEV, 3, shard, w), const(N_DEV, 2, HEADS, hshard, HEAD_DIM), const(16, w)),
        scratch_shapes=[pltpu.VMEM((3, w, w), f32), pltpu.VMEM((2, HEADS, HEAD_DIM, HEAD_DIM), f32),
                        pltpu.VMEM((SUBLANES, w), f32), pltpu.VMEM((SUBLANES, w), f32)] + [operand] * 8,
        compiler_params=_params(("arbitrary",), 58),
    )(x, dx, xr, xr, hs, hs, a_all, mult_all, gr_all, gi_all, gel_all, geld_all, y, vec, wbig, wsm, pvec, *dep_args)


def _pool_tile(s):
    return min(1024, s)


def _pool_counts(tile_index, ts):
    t = (tile_index * ts + lax.broadcasted_iota(jnp.int32, (ts, 1), 0) + 1).astype(f32)
    return [1.0 / jnp.minimum(t, float(win)) for win in POOL_WINDOWS]


def _pooled(h, halo, inv):
    ext = jnp.concatenate([halo, h], axis=0)
    out = []
    for g in range(len(POOL_WINDOWS)):
        acc = ext[:, g * HEAD_DIM:(g + 1) * HEAD_DIM]
        for step in range(g + 1):
            acc = acc + pltpu.roll(acc, 1 << step, 0)
        out.append(acc[POOL_HALO:] * inv[g] - h[:, g * HEAD_DIM:(g + 1) * HEAD_DIM])
    return out


def _pool_fwd(x, vec, pw, ps, layer):
    s = x.shape[0]
    ts = _pool_tile(s)

    def body(x_ref, vec_ref, pw_ref, ps_ref, xo_ref, y_ref, halo_ref):
        i = pl.program_id(0)

        @pl.when(i == 0)
        def _():
            halo_ref[...] = jnp.zeros_like(halo_ref)

        xv = x_ref[...]
        n, _ = _rms(xv)
        h = n * vec_ref[R_GS_M:R_GS_M + 1, :] + vec_ref[R_SH_M:R_SH_M + 1, :]
        pooled = _pooled(h, halo_ref[...], _pool_counts(i, ts))
        halo_ref[...] = h[ts - POOL_HALO:, :]
        mixed = jnp.concatenate([_mm(pooled[g].astype(bf16), pw_ref[g]) for g in range(HEADS)], axis=1)
        yv = mixed * ps_ref[0:1, :]
        y_ref[...] = yv.astype(bf16)
        xo_ref[...] = xv + vec_ref[R_GT_M:R_GT_M + 1, :] * yv

    row = pl.BlockSpec((ts, D_MODEL), lambda i: (i, 0))
    return pl.pallas_call(
        body, name=f"pool_fwd_{layer}", grid=(s // ts,),
        out_shape=(jax.ShapeDtypeStruct((s, D_MODEL), f32), jax.ShapeDtypeStruct((s, D_MODEL), bf16)),
        in_specs=[row, _mod_rows(layer),
                  pl.BlockSpec((HEADS, HEAD_DIM, HEAD_DIM), lambda i: (0, 0, 0)),
                  pl.BlockSpec((8, D_MODEL), lambda i: (0, 0))],
        out_specs=(row, row),
        scratch_shapes=[pltpu.VMEM((POOL_HALO, D_MODEL), f32)],
        compiler_params=_params(("arbitrary",)),
    )(x, vec, pw, ps)


def _pool_bwd(x, dx, y, vec, pw, ps, layer, dep=None):
    s = x.shape[0]
    ts = _pool_tile(s)
    nt = s // ts
    hshard = HEAD_DIM // N_DEV

    def body(x_ref, xh_ref, dx_ref, y_ref, vec_ref, pw_ref, ps_ref, dxo_ref, dpw_ref, sm_ref, acc_ref, q16_ref):
        i = pl.program_id(0)
        tile = nt - 1 - i

        @pl.when(i == 0)
        def _():
            acc_ref[...] = jnp.zeros_like(acc_ref)
            sm_ref[...] = jnp.zeros_like(sm_ref)
            q16_ref[...] = jnp.zeros_like(q16_ref)

        gs, sh = vec_ref[R_GS_M:R_GS_M + 1, :], vec_ref[R_SH_M:R_SH_M + 1, :]
        xv = x_ref[...]
        dxv = dx_ref[...]
        n, r = _rms(xv)
        h = n * gs + sh
        nh, _ = _rms(xh_ref[...])
        halo = jnp.where(tile == 0, 0.0, nh * gs + sh)
        inv = _pool_counts(tile, ts)
        pooled = _pooled(h, halo, inv)
        mixed = jnp.concatenate([_mm(pooled[g].astype(bf16), pw_ref[g]) for g in range(HEADS)], axis=1)

        dy = dxv * vec_ref[R_GT_M:R_GT_M + 1, :]
        sm_ref[G_GT:G_GT + 1, :] += _colsum(dxv * y_ref[...].astype(f32))
        sm_ref[3:4, :] += _colsum(dy * mixed)
        dmixed = (dy * ps_ref[0:1, :]).astype(bf16)
        dh_parts = []
        for g in range(HEADS):
            cols = slice(g * HEAD_DIM, (g + 1) * HEAD_DIM)
            acc_ref[g] += _mm_tn(pooled[g].astype(bf16), dmixed[:, cols])
            dpooled = _mm_nt(dmixed[:, cols], pw_ref[g])
            q = dpooled * inv[g]
            ext = jnp.concatenate([q, q16_ref[:, cols]], axis=0)
            q16_ref[:, cols] = q[0:POOL_HALO, :]
            for step in range(g + 1):
                ext = ext + pltpu.roll(ext, ext.shape[0] - (1 << step), 0)
            dh_parts.append(ext[:ts] - dpooled)
        dh = jnp.concatenate(dh_parts, axis=1)
        sm_ref[G_SH:G_SH + 1, :] += _colsum(dh)
        sm_ref[G_GS:G_GS + 1, :] += _colsum(dh * n)
        dxo_ref[...] = dxv + _norm_bwd(dh, n, r, gs)

        @pl.when(i == nt - 1)
        def _():
            for g in range(HEADS):
                dpw_ref[:, g] = acc_ref[g].astype(bf16).reshape(N_DEV, hshard, HEAD_DIM)

    rev = lambda i: (nt - 1 - i, 0)
    row = pl.BlockSpec((ts, D_MODEL), rev)
    halo16 = pl.BlockSpec((POOL_HALO, D_MODEL), lambda i: (jnp.maximum((nt - 1 - i) * (ts // POOL_HALO) - 1, 0), 0))
    const = lambda *shape: pl.BlockSpec(shape, lambda i: (0,) * len(shape))
    body, dep_specs, dep_args = _behind(body, 7, dep)
    return pl.pallas_call(
        body, name=f"pool_bwd_{layer}", grid=(nt,),
        out_shape=(jax.ShapeDtypeStruct((s, D_MODEL), f32),
                   jax.ShapeDtypeStruct((N_DEV, HEADS, hshard, HEAD_DIM), bf16),
                   jax.ShapeDtypeStruct((8, D_MODEL), f32)),
        in_specs=[row, halo16, row, row, _mod_rows(layer), const(HEADS, HEAD_DIM, HEAD_DIM),
                  const(8, D_MODEL)] + dep_specs,
        out_specs=(row, const(N_DEV, HEADS, hshard, HEAD_DIM), const(8, D_MODEL)),
        scratch_shapes=[pltpu.VMEM((HEADS, HEAD_DIM, HEAD_DIM), f32), pltpu.VMEM((POOL_HALO, D_MODEL), f32)],
        compiler_params=_params(("arbitrary",)),
    )(x, x, dx, y, vec, pw, ps, *dep_args)


def _final(x, target, g_fin):
    s = x.shape[0]
    ts = min(1024, s)

    def body(x_ref, t_ref, g_ref, dx_ref, sm_ref):
        @pl.when(pl.program_id(0) == 0)
        def _():
            sm_ref[...] = jnp.zeros_like(sm_ref)

        g = g_ref[0:1, :]
        n, r = _rms(x_ref[...])
        err = n * g - t_ref[...]
        sm_ref[1:2, :] += 0.5 * jnp.sum(jnp.mean(err * err, axis=-1, keepdims=True), axis=0, keepdims=True)
        dyv = err * (1.0 / D_MODEL)
        sm_ref[0:1, :] += _colsum(dyv * n)
        dx_ref[...] = _norm_bwd(dyv, n, r, g)

    row = pl.BlockSpec((ts, D_MODEL), lambda i: (i, 0))
    return pl.pallas_call(
        body, name="final_loss", grid=(s // ts,),
        out_shape=(jax.ShapeDtypeStruct((s, D_MODEL), f32), jax.ShapeDtypeStruct((8, D_MODEL), f32)),
        in_specs=[row, row, pl.BlockSpec((8, D_MODEL), lambda i: (0, 0))],
        out_specs=(row, pl.BlockSpec((8, D_MODEL), lambda i: (0, 0))),
        compiler_params=_params(("arbitrary",)),
    )(x, target, g_fin)


def _small_pack(sm_ffn, sm_mix, sm_fin, table, g_mix, g_ffn, lam):
    def body(*refs):
        ffn, mix = refs[0:DEPTH], refs[DEPTH:2 * DEPTH]
        fin_ref, tab_ref, gm_ref, gf_ref, lam_ref, o_ref = refs[2 * DEPTH:]
        o_ref[...] = jnp.zeros_like(o_ref)
        for i in range(DEPTH):
            base = K_MOD + i * N_MOD
            o_ref[base + 0:base + 1, :] = mix[i][G_SH:G_SH + 1, :]
            o_ref[base + 1:base + 2, :] = mix[i][G_GS:G_GS + 1, :] * gm_ref[i:i + 1, :]
            o_ref[base + 2:base + 3, :] = mix[i][G_GT:G_GT + 1, :]
            o_ref[base + 3:base + 4, :] = ffn[i][G_SH:G_SH + 1, :]
            o_ref[base + 4:base + 5, :] = ffn[i][G_GS:G_GS + 1, :] * gf_ref[i:i + 1, :]
            o_ref[base + 5:base + 6, :] = ffn[i][G_GT:G_GT + 1, :]
            o_ref[K_NMIX + i:K_NMIX + i + 1, :] = mix[i][G_GS:G_GS + 1, :] * (1.0 + tab_ref[i, R_SC_M:R_SC_M + 1, :])
            o_ref[K_NFFN + i:K_NFFN + i + 1, :] = ffn[i][G_GS:G_GS + 1, :] * (1.0 + tab_ref[i, R_SC_F:R_SC_F + 1, :])
            j = i // 2
            if i % 2 == 0:
                for k, src in enumerate((G_BY, G_BIN, G_CONVB, None, G_BOUT)):
                    dst = K_LRUB + j * 5 + k
                    if src is None:
                        o_ref[dst:dst + 1, :] = mix[i][G_LS:G_LS + 1, :] * _sigmoid(-lam_ref[j:j + 1, :])
                    else:
                        o_ref[dst:dst + 1, :] = mix[i][src:src + 1, :]
                o_ref[K_CONVW + j * 4:K_CONVW + j * 4 + 4, :] = mix[i][G_CW0:G_CW0 + 4, :]
                o_ref[K_BA + j:K_BA + j + 1, :] = mix[i][G_BA:G_BA + 1, :]
                o_ref[K_BX + j:K_BX + j + 1, :] = mix[i][G_BX:G_BX + 1, :]
            else:
                o_ref[K_PS + j:K_PS + j + 1, :] = mix[i][3:4, :]
        o_ref[K_FIN:K_FIN + 2, :] = fin_ref[0:2, :]

    return pl.pallas_call(body, name="small_pack", out_shape=jax.ShapeDtypeStruct((K_ROWS, D_MODEL), f32))(
        *sm_ffn, *sm_mix, sm_fin, table, g_mix, g_ffn, lam)


def _small_sum(gathered):
    def body(g_ref, o_ref, token_ref):
        tot = g_ref[0]
        for src in range(1, N_DEV):
            tot = tot + g_ref[src]
        o_ref[...] = tot
        token_ref[...] = jnp.zeros_like(token_ref)

    return pl.pallas_call(
        body, name="small_sum",
        out_shape=(jax.ShapeDtypeStruct(gathered.shape[1:], f32), jax.ShapeDtypeStruct((8, 128), f32)))(gathered)


def _adamw_math(g, w, m, v):
    m = ADAM_B1 * m + (1.0 - ADAM_B1) * g
    v = ADAM_B2 * v + (1.0 - ADAM_B2) * (g * g)
    m_hat = m / (1.0 - ADAM_B1 ** ADAM_STEP)
    v_hat = v / (1.0 - ADAM_B2 ** ADAM_STEP)
    delta = -ADAM_LR * (m_hat / (jnp.sqrt(v_hat) + ADAM_EPS) + ADAM_WD * w)
    return delta, m, v


def _adamw_small(params):
    n = len(params)
    shapes = [w.shape for _, w, _, _ in params]
    two_d = [(1, s[0]) if len(s) == 1 else (math.prod(s[:-1]), s[-1]) for s in shapes]

    def body(*refs):
        ins, outs = refs[:4 * n], refs[4 * n:]
        for k in range(n):
            g_ref, w_ref, m_ref, v_ref = ins[4 * k:4 * k + 4]
            outs[3 * k][...], outs[3 * k + 1][...], outs[3 * k + 2][...] = _adamw_math(
                g_ref[...], w_ref[...], m_ref[...], v_ref[...])

    outs = pl.pallas_call(
        body, name="adamw_small",
        out_shape=tuple(jax.ShapeDtypeStruct(two_d[k], f32) for k in range(n) for _ in range(3)))(
        *(t.reshape(two_d[k]) for k, p in enumerate(params) for t in p))
    return [tuple(outs[3 * k + i].reshape(shapes[k]) for i in range(3)) for k in range(n)]


def _block_rows(rows, cols):
    tr = max(SUBLANES, min(rows, (512 * 1024) // (4 * cols)))
    while rows % tr:
        tr //= 2
    return tr


def _adamw_reduce(name, landings, kind, w, m, v):
    nl = len(landings)
    rows, cols = landings[0].shape[2:]
    tr = _block_rows(rows, cols)
    per_layer = rows // tr

    def body(*refs):
        l_refs = refs[:nl]
        w_ref, m_ref, v_ref, g_ref, d_ref, mo_ref, vo_ref = refs[nl:]
        layer = pl.program_id(0)
        for k in range(nl):
            @pl.when(layer == k)
            def _(k=k):
                g = l_refs[k][0].astype(f32)
                for src in range(1, N_DEV):
                    g = g + l_refs[k][src].astype(f32)
                g_ref[...] = g
        d_ref[...], mo_ref[...], vo_ref[...] = _adamw_math(g_ref[...], w_ref[...], m_ref[...], v_ref[...])

    blk = pl.BlockSpec((tr, cols), lambda l, r: (l * per_layer + r, 0))
    land = [pl.BlockSpec((N_DEV, None, tr, cols), lambda l, r, k=k: (0, kind, jnp.where(l == k, r, 0), 0)) for k in range(nl)]
    return pl.pallas_call(
        body, name=f"adamw_{name}", grid=(nl, per_layer),
        out_shape=tuple(jax.ShapeDtypeStruct((nl * rows, cols), f32) for _ in range(4)),
        in_specs=land + [blk, blk, blk],
        out_specs=(blk, blk, blk, blk),
        compiler_params=_params(("arbitrary", "arbitrary"), 32),
    )(*landings, w, m, v)


def _adamw_w_mod(c_all, dmod_all, w, m, v):
    depth, d, cols = w.shape
    tr = 256

    def body(c_ref, dm_ref, w_ref, m_ref, v_ref, g_ref, d_ref, mo_ref, vo_ref):
        cv = c_ref[...]
        cond = cv * _sigmoid(cv)
        g = lax.dot_general(cond, dm_ref[...], (((0,), (0,)), ((), ())), preferred_element_type=f32,
                            precision=lax.Precision.HIGHEST)
        g_ref[...] = g
        d_ref[...], mo_ref[...], vo_ref[...] = _adamw_math(g, w_ref[...], m_ref[...], v_ref[...])

    blk = pl.BlockSpec((None, tr, cols), lambda i, r: (i, r, 0))
    return pl.pallas_call(
        body, name="adamw_w_mod", grid=(depth, d // tr),
        out_shape=tuple(jax.ShapeDtypeStruct(w.shape, f32) for _ in range(4)),
        in_specs=[pl.BlockSpec((N_DEV, tr), lambda i, r: (0, r)),
                  pl.BlockSpec((None, N_DEV, cols), lambda i, r: (i, 0, 0)), blk, blk, blk],
        out_specs=(blk, blk, blk, blk),
        compiler_params=_params(("arbitrary", "arbitrary"), 32),
    )(c_all, dmod_all, w, m, v)


def kernel(x, c, w_mod, b_mod, norm_mix_g, norm_ffn_g, lru_w_y, lru_b_y, lru_w_in, lru_b_in, lru_conv_w, lru_conv_b, lru_w_a, lru_b_a, lru_w_x, lru_b_x, lru_lambda, lru_w_out, lru_b_out, pool_w, pool_scale, ffn_w1, ffn_w2, final_norm_g, loss_target, m_w_mod, m_b_mod, m_norm_mix_g, m_norm_ffn_g, m_lru_w_y, m_lru_b_y, m_lru_w_in, m_lru_b_in, m_lru_conv_w, m_lru_conv_b, m_lru_w_a, m_lru_b_a, m_lru_w_x, m_lru_b_x, m_lru_lambda, m_lru_w_out, m_lru_b_out, m_pool_w, m_pool_scale, m_ffn_w1, m_ffn_w2, m_final_norm_g, v_w_mod, v_b_mod, v_norm_mix_g, v_norm_ffn_g, v_lru_w_y, v_lru_b_y, v_lru_w_in, v_lru_b_in, v_lru_conv_w, v_lru_conv_b, v_lru_w_a, v_lru_b_a, v_lru_w_x, v_lru_b_x, v_lru_lambda, v_lru_w_out, v_lru_b_out, v_pool_w, v_pool_scale, v_ffn_w1, v_ffn_w2, v_final_norm_g):
    me = 4 * lax.axis_index("x") + 2 * lax.axis_index("y") + lax.axis_index("c")
    n_lru = lru_w_y.shape[0]
    shard = LRU_WIDTH // N_DEV
    hshard = HEAD_DIM // N_DEV
    xs = x[0]
    target = loss_target[0]

    small_vecs = jnp.concatenate([
        lru_conv_w.reshape(n_lru * 4, shard), lru_b_a.reshape(n_lru, HEADS * hshard),
        lru_b_x.reshape(n_lru, HEADS * hshard), pool_scale, jnp.zeros((2, shard), f32)], axis=0)
    sv_g, c_g = _exchange([small_vecs, c], True, "gather_cond")
    conv_w_full = sv_g[:, 0:8].reshape(N_DEV, n_lru, 4, shard).transpose(1, 2, 0, 3).reshape(n_lru, 4, LRU_WIDTH)
    b_a_full = sv_g[:, 8:10].reshape(N_DEV, n_lru, HEADS, hshard).transpose(1, 2, 0, 3).reshape(n_lru, LRU_WIDTH)
    b_x_full = sv_g[:, 10:12].reshape(N_DEV, n_lru, HEADS, hshard).transpose(1, 2, 0, 3).reshape(n_lru, LRU_WIDTH)
    ps_full = sv_g[:, 12:14].transpose(1, 0, 2).reshape(n_lru, D_MODEL)
    c_all = c_g.reshape(N_DEV, D_MODEL)

    (mod_g,) = _exchange([_mod_part(c_all, w_mod)], True, "gather_mod", pieces=DEPTH)
    mod_row = lax.dynamic_index_in_dim(mod_g, me, axis=2, keepdims=False)
    mod_row = mod_row.transpose(1, 0, 2).reshape(DEPTH, N_MOD * D_MODEL)
    table, token = _mod_table(mod_row, b_mod, norm_mix_g, norm_ffn_g)

    first_pieces = 4
    (first_mix,), token = _send_start("gather_first_start", [[
        jnp.stack([lru_w_y[0], lru_w_in[0], lru_w_out[0]]).astype(bf16).reshape(3 * first_pieces, -1, LRU_WIDTH),
        jnp.stack([lru_w_a[0], lru_w_x[0]]).astype(bf16).reshape(first_pieces, -1, HEAD_DIM)]], True, me,
        pieces=first_pieces, after=token)

    parts = []
    for i in range(DEPTH):
        j = i // 2
        if i > 0 and i % 2 == 0:
            parts.append([(jnp.stack([lru_w_y[j], lru_w_in[j], lru_w_out[j]]) + token[0, 0]).astype(bf16),
                          (jnp.stack([lru_w_a[j], lru_w_x[j]]) + token[0, 0]).astype(bf16)])
        elif i % 2 == 1:
            parts.append([(pool_w[j] + token[0, 0]).astype(bf16)])
        parts.append([(ffn_w1[i] + token[0, 0]).astype(bf16), (ffn_w2[i] + token[0, 0]).astype(bf16)])
    first_got, token = _send_wait("gather_mix_wait_0", first_mix, [a for part in parts for a in part])
    handles, token = _send_start("gather_rest_start", parts, True, me, after=token)
    h_ffn = [handles[0], handles[2], handles[4], handles[6]]
    h_mix = [None, handles[1], handles[3], handles[5]]

    zero_row = jnp.zeros((1, LRU_WIDTH), f32)
    pvecs = [jnp.concatenate([lru_b_y[j:j + 1], lru_b_in[j:j + 1], lru_conv_b[j:j + 1], b_a_full[j:j + 1],
                              b_x_full[j:j + 1], lru_lambda[j:j + 1], lru_b_out[j:j + 1], zero_row,
                              conv_w_full[j], zero_row, zero_row, zero_row, zero_row], axis=0) for j in range(n_lru)]
    ps_rows = [jnp.concatenate([ps_full[j:j + 1], jnp.zeros((7, D_MODEL), f32)], axis=0) for j in range(n_lru)]

    saved = []
    ffn_w, mix_w = [], []
    h = xs
    for i in range(DEPTH):
        j = i // 2
        got = first_got if i == 0 else _send_wait(f"gather_mix_wait_{i}", h_mix[i], h)[0]
        if i % 2 == 0:
            got = [got[0].reshape(N_DEV, 3, shard, LRU_WIDTH), got[1].reshape(N_DEV, 2, HEADS, hshard, HEAD_DIM)]
            mix_w.append((got[0].transpose(1, 0, 2, 3).reshape(3, LRU_WIDTH, LRU_WIDTH),
                          got[1].transpose(1, 2, 0, 3, 4).reshape(2, HEADS, HEAD_DIM, HEAD_DIM)))
            h_mid, *lru_saved = _lru_fwd(h, table, mix_w[i][0], mix_w[i][1], pvecs[j], i, dep=token)
            mix_saved = (h, tuple(lru_saved))
        else:
            mix_w.append((got[0].transpose(1, 0, 2, 3).reshape(HEADS, HEAD_DIM, HEAD_DIM),))
            h_mid, y_mix = _pool_fwd(h, table, mix_w[i][0], ps_rows[j], i)
            mix_saved = (h, y_mix)
        ffn_w.append(_send_wait(f"gather_ffn_wait_{i}", h_ffn[i], h_mid)[0])
        h_out, u, y_ffn, hb = _ffn_fwd(h_mid, table, ffn_w[i][0], ffn_w[i][1], i)
        saved.append((mix_saved, (h_mid, u, y_ffn, hb)))
        h = h_out
    fin_rows = jnp.concatenate([final_norm_g[None, :], jnp.zeros((7, D_MODEL), f32)], axis=0)
    dx, sm_fin = _final(h, target, fin_rows)

    sm_ffn, sm_mix = [None] * DEPTH, [None] * DEPTH
    x_ffn, x_mix = [None] * DEPTH, [None] * DEPTH
    token = jnp.zeros((8, 128), f32)
    last_mix = None
    for i in reversed(range(DEPTH)):
        j = i // 2
        mix_saved, (h_mid, u, y_ffn, hb) = saved[i]
        dx, da, dyb, sm_ffn[i] = _ffn_bwd_act(h_mid, dx, u, y_ffn, table, ffn_w[i][0], ffn_w[i][1], i, dep=token)
        ffn_grads = [_ffn_bwd_w1(hb, da, i), _ffn_bwd_w2(u, dyb, i)]
        if last_mix is None:
            (x_ffn[i],), token = _send_start(f"grads_start_{i}", [ffn_grads], False, me)
        else:
            (x_mix[i + 1], x_ffn[i]), token = _send_start(f"grads_start_{i}", [last_mix, ffn_grads], False, me)
        if i % 2 == 0:
            h_in, lru_saved = mix_saved
            dx, dbig, dsmall, sm_mix[i] = _lru_bwd(
                h_in, dx, lru_saved, table, mix_w[i][0], mix_w[i][1], pvecs[j], i, dep=token)
            last_mix = [dbig, dsmall]
        else:
            h_in, y_mix = mix_saved
            dx, dpool, sm = _pool_bwd(h_in, dx, y_mix, table, mix_w[i][0], ps_rows[j], i, dep=token)
            sm_mix[i] = jnp.concatenate([sm, jnp.zeros((8, D_MODEL), f32)], axis=0)
            last_mix = [dpool]
    grad_x = dx[None]

    pack = _small_pack(sm_ffn, sm_mix, sm_fin, table + token[0, 0], norm_mix_g, norm_ffn_g, lru_lambda)
    (pack_g,) = _exchange([pack], True, "gather_small_grads", pieces=4)
    tot, token = _small_sum(pack_g)
    loss = tot[K_LOSS, 0]
    (x_mix[0],), _ = _send_start("grads_last_start", [[t + token[0, 0].astype(bf16) for t in last_mix]], False, me)
    cols = w_mod.shape[2]
    dmod_all = lax.dynamic_slice_in_dim(pack_g[:, K_MOD:K_MOD + DEPTH * N_MOD].reshape(N_DEV, DEPTH, N_MOD * D_MODEL),
                                        me * cols, cols, axis=2).transpose(1, 0, 2)
    results = {"w_mod": _adamw_w_mod(c_all, dmod_all, w_mod, m_w_mod, v_w_mod)}

    after = results["w_mod"][1]
    l_ffn = [_send_wait(f"grads_ffn_wait_{i}", x_ffn[i], after)[0] for i in reversed(range(DEPTH))][::-1]

    def reduce_update(name, landings, kind, w, m, v):
        rows = w.size // w.shape[-1]
        two_d = (rows, w.shape[-1])
        lands = [t.reshape(N_DEV, -1, rows // len(landings), w.shape[-1]) for t in landings]
        outs = _adamw_reduce(name, lands, kind, w.reshape(two_d), m.reshape(two_d), v.reshape(two_d))
        return tuple(t.reshape(w.shape) for t in outs)

    results["ffn_w1"] = reduce_update("ffn_w1", [t[0] for t in l_ffn], 0, ffn_w1, m_ffn_w1, v_ffn_w1)
    results["ffn_w2"] = reduce_update("ffn_w2", [t[1] for t in l_ffn], 0, ffn_w2, m_ffn_w2, v_ffn_w2)
    after = results["ffn_w2"][1]
    l_mix = [_send_wait(f"grads_mix_wait_{i}", x_mix[i], after)[0] for i in reversed(range(DEPTH))][::-1]
    l_lru_big = [l_mix[i][0] for i in range(0, DEPTH, 2)]
    l_lru_small = [l_mix[i][1] for i in range(0, DEPTH, 2)]
    l_pool = [l_mix[i][0] for i in range(1, DEPTH, 2)]
    results["lru_w_y"] = reduce_update("lru_w_y", l_lru_big, 0, lru_w_y, m_lru_w_y, v_lru_w_y)
    results["lru_w_in"] = reduce_update("lru_w_in", l_lru_big, 1, lru_w_in, m_lru_w_in, v_lru_w_in)
    results["lru_w_out"] = reduce_update("lru_w_out", l_lru_big, 2, lru_w_out, m_lru_w_out, v_lru_w_out)
    results["lru_w_a"] = reduce_update("lru_w_a", l_lru_small, 0, lru_w_a, m_lru_w_a, v_lru_w_a)
    results["lru_w_x"] = reduce_update("lru_w_x", l_lru_small, 1, lru_w_x, m_lru_w_x, v_lru_w_x)
    results["pool_w"] = reduce_update("pool_w", l_pool, 0, pool_w, m_pool_w, v_pool_w)

    def my_cols(full, width):
        return lax.dynamic_slice_in_dim(full, me * width, width, axis=full.ndim - 1)

    lru_rows = tot[K_LRUB:K_LRUB + 5 * n_lru].reshape(n_lru, 5, LRU_WIDTH)
    small_grads = {
        "b_mod": tot[K_MOD:K_MOD + DEPTH * N_MOD].reshape(DEPTH, N_MOD * D_MODEL),
        "norm_mix_g": tot[K_NMIX:K_NMIX + DEPTH],
        "norm_ffn_g": tot[K_NFFN:K_NFFN + DEPTH],
        "lru_b_y": lru_rows[:, 0], "lru_b_in": lru_rows[:, 1], "lru_conv_b": lru_rows[:, 2],
        "lru_lambda": lru_rows[:, 3], "lru_b_out": lru_rows[:, 4],
        "lru_conv_w": my_cols(tot[K_CONVW:K_CONVW + 4 * n_lru].reshape(n_lru, 4, LRU_WIDTH), shard),
        "lru_b_a": my_cols(tot[K_BA:K_BA + n_lru].reshape(n_lru, HEADS, HEAD_DIM), hshard),
        "lru_b_x": my_cols(tot[K_BX:K_BX + n_lru].reshape(n_lru, HEADS, HEAD_DIM), hshard),
        "pool_scale": my_cols(tot[K_PS:K_PS + n_lru], shard),
        "final_norm_g": tot[K_FIN],
    }
    given = dict(b_mod=(b_mod, m_b_mod, v_b_mod), norm_mix_g=(norm_mix_g, m_norm_mix_g, v_norm_mix_g),
                 norm_ffn_g=(norm_ffn_g, m_norm_ffn_g, v_norm_ffn_g), lru_b_y=(lru_b_y, m_lru_b_y, v_lru_b_y),
                 lru_b_in=(lru_b_in, m_lru_b_in, v_lru_b_in), lru_conv_w=(lru_conv_w, m_lru_conv_w, v_lru_conv_w),
                 lru_conv_b=(lru_conv_b, m_lru_conv_b, v_lru_conv_b), lru_b_a=(lru_b_a, m_lru_b_a, v_lru_b_a),
                 lru_b_x=(lru_b_x, m_lru_b_x, v_lru_b_x), lru_lambda=(lru_lambda, m_lru_lambda, v_lru_lambda),
                 lru_b_out=(lru_b_out, m_lru_b_out, v_lru_b_out), pool_scale=(pool_scale, m_pool_scale, v_pool_scale),
                 final_norm_g=(final_norm_g, m_final_norm_g, v_final_norm_g))
    updates = _adamw_small([(g,) + given[name] for name, g in small_grads.items()])
    for (name, g), update in zip(small_grads.items(), updates):
        results[name] = (g,) + update

    order = ["w_mod", "b_mod", "norm_mix_g", "norm_ffn_g", "lru_w_y", "lru_b_y", "lru_w_in", "lru_b_in", "lru_conv_w",
             "lru_conv_b", "lru_w_a", "lru_b_a", "lru_w_x", "lru_b_x", "lru_lambda", "lru_w_out", "lru_b_out", "pool_w",
             "pool_scale", "ffn_w1", "ffn_w2", "final_norm_g"]
    return (loss, grad_x, *[results[n][0] for n in order], *[results[n][1] for n in order],
            *[results[n][2] for n in order], *[results[n][3] for n in order])
```

```python
import math

import jax
import jax.numpy as jnp
from jax import lax
from jax.experimental import pallas as pl
from jax.experimental.pallas import tpu as pltpu

f32, bf16 = jnp.float32, jnp.bfloat16

D_MODEL = 1024
LRU_WIDTH = 1024
HEADS = 4
HEAD_DIM = 256
D_FF = 4096
DEPTH = 4
N_MOD = 6
N_DEV = 8
FF_CHUNK = D_FF // N_DEV
POOL_WINDOWS = (2, 4, 8, 16)
POOL_HALO = 16
EPS = 1e-6
LRU_C = 8.0

ADAM_LR = 0.001
ADAM_B1 = 0.9
ADAM_B2 = 0.999
ADAM_EPS = 1e-08
ADAM_WD = 0.01
ADAM_STEP = 10

SUBLANES = 8
BF16_ROWS = 16

R_SH_M, R_SC_M, R_GT_M, R_SH_F, R_SC_F, R_GT_F, R_GS_M, R_GS_F = range(8)
P_BY, P_BIN, P_CONVB, P_BA, P_BX, P_LAM, P_BOUT, P_CW0 = 0, 1, 2, 3, 4, 5, 6, 8
G_SH, G_GS, G_GT, G_BY, G_BIN, G_CONVB, G_BA, G_BX, G_LS, G_BOUT, G_CW0 = 0, 1, 2, 3, 4, 5, 6, 7, 8, 9, 10
K_MOD, K_NMIX, K_NFFN, K_LRUB, K_CONVW, K_BA, K_BX, K_PS, K_FIN, K_LOSS, K_ROWS = 0, 24, 28, 32, 42, 50, 52, 54, 56, 57, 64


def _params(semantics=None, vmem_mb=56):
    return pltpu.CompilerParams(dimension_semantics=semantics, vmem_limit_bytes=vmem_mb * 1024 * 1024)


def _mod_rows(layer):
    return pl.BlockSpec((None, 8, D_MODEL), lambda *_: (layer, 0, 0))


def _behind(body, n_in, dep):
    if dep is None:
        return body, [], []

    def run(*refs):
        body(*refs[:n_in], *refs[n_in + 1:])

    return run, [pl.BlockSpec(memory_space=pl.ANY)], [dep]


def _mm(a, b):
    return jnp.dot(a, b, preferred_element_type=f32)


def _mm_nt(a, b):
    return lax.dot_general(a, b, (((1,), (1,)), ((), ())), preferred_element_type=f32)


def _mm_tn(a, b):
    return lax.dot_general(a, b, (((0,), (0,)), ((), ())), preferred_element_type=f32)


def _rms(x):
    r = lax.rsqrt(jnp.mean(x * x, axis=-1, keepdims=True) + EPS)
    return x * r, r


def _norm_bwd(dh, n, r, gs):
    dn = dh * gs
    return r * (dn - n * jnp.mean(dn * n, axis=-1, keepdims=True))


def _colsum(v):
    return jnp.sum(v, axis=0, keepdims=True)


def _sigmoid(v):
    return 0.5 * jnp.tanh(0.5 * v) + 0.5


def _log_sigmoid(v):
    return jnp.minimum(v, 0.0) - jnp.log1p(jnp.exp(-jnp.abs(v)))


_GELU_C = 0.7978845608028654
_GELU_A = 0.044715


def _gelu_and_grad(v):
    v2 = v * v
    t = jnp.tanh(_GELU_C * v * (1.0 + _GELU_A * v2))
    p = 0.5 + 0.5 * t
    return v * p, p + (0.5 * v) * (1.0 - t * t) * (_GELU_C + (3.0 * _GELU_A * _GELU_C) * v2)


def _rows_before(halo, v, shifts):
    hr = halo.shape[0]
    ext = jnp.concatenate([halo, v], axis=0)
    return [pltpu.roll(ext, k, 0)[hr:] for k in shifts]


def _rows_after(v, halo, shifts):
    n = v.shape[0]
    ext = jnp.concatenate([v, halo], axis=0)
    return [pltpu.roll(ext, ext.shape[0] - k, 0)[:n] for k in shifts]


def _shift_matrix(n, halo_rows, shifts):
    rows = lax.broadcasted_iota(jnp.int32, (n, n + halo_rows), 0)
    cols = lax.broadcasted_iota(jnp.int32, (n, n + halo_rows), 1)
    return jnp.concatenate([(cols == rows + halo_rows - k).astype(bf16) for k in shifts], axis=0)


def _shifted_rows(sel, halo, v):
    n = v.shape[0]
    out = _mm(sel, jnp.concatenate([halo, v], axis=0))
    return [out[j * n:(j + 1) * n] for j in range(sel.shape[0] // n)]


def _block_diag(v, w_ref, kind):
    return jnp.concatenate(
        [_mm(v[:, h * HEAD_DIM:(h + 1) * HEAD_DIM], w_ref[kind, h]) for h in range(HEADS)], axis=1)


def _block_diag_t(v, w_ref, kind):
    return jnp.concatenate(
        [_mm_nt(v[:, h * HEAD_DIM:(h + 1) * HEAD_DIM], w_ref[kind, h]) for h in range(HEADS)], axis=1)


def _exchange(arrays, gather, name, pieces=1, after=()):
    n = len(arrays)
    peers = N_DEV - 1
    after = list(after)

    def body(*refs):
        ins, outs = refs[:n], refs[n + len(after):2 * n + len(after)]
        send_sems, recv_sems, local_sems = refs[2 * n + len(after):]
        x, y, c = lax.axis_index("x"), lax.axis_index("y"), lax.axis_index("c")
        me = 4 * x + 2 * y + c
        local = []
        for k in range(n):
            cp = pltpu.make_async_copy(ins[k] if gather else ins[k].at[me], outs[k].at[me], local_sems.at[k])
            cp.start()
            local.append(cp)
        remote = _peer_copies(ins, outs, send_sems, recv_sems, gather, pieces)
        for cp in remote:
            cp.start()
        for cp in remote:
            cp.wait()
        for cp in local:
            cp.wait()

    out_shape = tuple(
        jax.ShapeDtypeStruct(((N_DEV,) + a.shape) if gather else a.shape, a.dtype) for a in arrays)
    outs = pl.pallas_call(
        body, name=name, out_shape=out_shape,
        in_specs=[pl.BlockSpec(memory_space=pl.ANY)] * (n + len(after)),
        out_specs=tuple(pl.BlockSpec(memory_space=pl.ANY) for _ in range(n)),
        scratch_shapes=[pltpu.SemaphoreType.DMA((n * pieces * peers,)), pltpu.SemaphoreType.DMA((n * pieces * peers,)),
                        pltpu.SemaphoreType.DMA((n,))],
        compiler_params=pltpu.CompilerParams(has_side_effects=True),
    )(*arrays, *after)
    return list(outs)


_HBM = pl.BlockSpec(memory_space=pltpu.HBM)
_SEM = pl.BlockSpec(memory_space=pltpu.SEMAPHORE)
_DATAFLOW = pltpu.SideEffectType.DATAFLOW_SIDE_EFFECTING


def _peer_copies(src_refs, land_refs, send_sems, recv_sems, gather, pieces=1):
    x, y, c = lax.axis_index("x"), lax.axis_index("y"), lax.axis_index("c")
    me = 4 * x + 2 * y + c
    peers = N_DEV - 1
    copies = []
    for p in range(1, N_DEV):
        px = 1 - x if p & 4 else x
        py = 1 - y if p & 2 else y
        pc = 1 - c if p & 1 else c
        for k in range(len(src_refs)):
            block = src_refs[k] if gather else src_refs[k].at[4 * px + 2 * py + pc]
            dst = land_refs[k].at[me]
            rows = block.shape[0] // pieces
            for r in range(pieces):
                part = pl.ds(r * rows, rows)
                sem = (k * pieces + r) * peers + p - 1
                copies.append(pltpu.make_async_remote_copy(
                    src_ref=block.at[part] if pieces > 1 else block, dst_ref=dst.at[part] if pieces > 1 else dst,
                    send_sem=send_sems.at[sem], recv_sem=recv_sems.at[sem],
                    device_id=(px, py, pc), device_id_type=pl.DeviceIdType.MESH))
    return copies


def _landing(srcs, gather, me):
    out = []
    for a in srcs:
        own = a if gather else lax.dynamic_index_in_dim(a, me, 0, keepdims=False)
        out.append(lax.dynamic_update_index_in_dim(lax.empty((N_DEV,) + own.shape, own.dtype), own, me, 0))
    return out


def _send_start(name, groups, gather, me, pieces=1, after=None):
    sizes = [len(g) for g in groups]
    srcs = [a for g in groups for a in g]
    n = len(srcs)
    lands = _landing(srcs, gather, me)
    ng = len(groups)
    first = [sum(sizes[:g]) for g in range(ng)]
    extra = [] if after is None else [after]

    def body(*refs):
        src_refs, land_refs = refs[:n], refs[n:2 * n]
        sems, token = refs[2 * n + len(extra):2 * n + len(extra) + 2 * ng], refs[-1]
        for g in range(ng):
            part = slice(first[g], first[g] + sizes[g])
            for cp in _peer_copies(src_refs[part], land_refs[part], sems[2 * g], sems[2 * g + 1], gather, pieces):
                cp.start()
        token[...] = jnp.zeros_like(token)

    sem_shapes = [pltpu.SemaphoreType.DMA((sizes[g // 2] * pieces * (N_DEV - 1),)) for g in range(2 * ng)]
    outs = pl.pallas_call(
        body, name=name,
        out_shape=(*sem_shapes, *[pltpu.HBM(a.shape, a.dtype) for a in (*srcs, *lands)], jax.ShapeDtypeStruct((8, 128), f32)),
        in_specs=[_HBM] * (2 * n) + [pl.BlockSpec(memory_space=pl.ANY)] * len(extra),
        out_specs=(*[_SEM] * (2 * ng), *[_HBM] * (2 * n), pl.BlockSpec(memory_space=pltpu.VMEM)),
        input_output_aliases={k: 2 * ng + k for k in range(2 * n)},
        compiler_params=pltpu.CompilerParams(has_side_effects=_DATAFLOW),
    )(*[pltpu.with_memory_space_constraint(a, pltpu.HBM) for a in (*srcs, *lands)], *extra)
    srcs_thru, lands_thru = outs[2 * ng:2 * ng + n], outs[2 * ng + n:2 * ng + 2 * n]
    handles = [(outs[2 * g], outs[2 * g + 1], list(srcs_thru[first[g]:first[g] + sizes[g]]),
                list(lands_thru[first[g]:first[g] + sizes[g]]), gather, pieces) for g in range(ng)]
    return handles, outs[-1]


def _send_wait(name, handle, after):
    send_sems, recv_sems, srcs, lands, gather, pieces = handle
    n = len(srcs)
    after = list(after) if isinstance(after, (list, tuple)) else [after]

    def body(*refs):
        src_refs, land_refs = refs[:n], refs[n:2 * n]
        for cp in _peer_copies(src_refs, land_refs, refs[2 * n], refs[2 * n + 1], gather, pieces):
            cp.wait_send()
            cp.wait_recv()
        refs[-1][...] = jnp.zeros_like(refs[-1])

    outs = pl.pallas_call(
        body, name=name,
        out_shape=(*[pltpu.HBM(a.shape, a.dtype) for a in (*srcs, *lands)], jax.ShapeDtypeStruct((8, 128), f32)),
        in_specs=[_HBM] * (2 * n) + [_SEM, _SEM] + [pl.BlockSpec(memory_space=pl.ANY)] * len(after),
        out_specs=(*[_HBM] * (2 * n), pl.BlockSpec(memory_space=pltpu.VMEM)),
        input_output_aliases={k: k for k in range(2 * n)},
        compiler_params=pltpu.CompilerParams(has_side_effects=_DATAFLOW),
    )(*srcs, *lands, send_sems, recv_sems, *after)
    return list(outs[n:2 * n]), outs[-1]


def _mod_part(c_all, w_mod):
    depth, d, cols = w_mod.shape

    def body(c_ref, w_ref, o_ref):
        cv = c_ref[...]
        cond = cv * _sigmoid(cv)
        o_ref[...] = jnp.dot(cond, w_ref[...], preferred_element_type=f32, precision=lax.Precision.HIGHEST)

    return pl.pallas_call(
        body, name="mod_part", grid=(depth,),
        out_shape=jax.ShapeDtypeStruct((depth, N_DEV, cols), f32),
        in_specs=[pl.BlockSpec((N_DEV, d), lambda i: (0, 0)), pl.BlockSpec((None, d, cols), lambda i: (i, 0, 0))],
        out_specs=pl.BlockSpec((None, N_DEV, cols), lambda i: (i, 0, 0)),
        compiler_params=_params(("arbitrary",), 32),
    )(c_all, w_mod)


def _mod_table(mod_row, b_mod, g_mix, g_ffn):
    def body(m_ref, b_ref, gm_ref, gf_ref, o_ref, token_ref):
        for i in range(DEPTH):
            for k in range(N_MOD):
                o_ref[i, k:k + 1, :] = m_ref[i:i + 1, k * D_MODEL:(k + 1) * D_MODEL] + b_ref[i:i + 1, k * D_MODEL:(k + 1) * D_MODEL]
            o_ref[i, R_GS_M:R_GS_M + 1, :] = gm_ref[i:i + 1, :] * (1.0 + o_ref[i, R_SC_M:R_SC_M + 1, :])
            o_ref[i, R_GS_F:R_GS_F + 1, :] = gf_ref[i:i + 1, :] * (1.0 + o_ref[i, R_SC_F:R_SC_F + 1, :])
        token_ref[...] = jnp.zeros_like(token_ref)

    return pl.pallas_call(
        body, name="mod_table",
        out_shape=(jax.ShapeDtypeStruct((DEPTH, 8, D_MODEL), f32), jax.ShapeDtypeStruct((8, 128), f32)))(
        mod_row, b_mod, g_mix, g_ffn)


def _ffn_tile(s):
    return min(512, s)


def _layer_weights(shape):
    return pl.BlockSpec((N_DEV,) + shape, lambda i: (0, 0, 0))


def _ffn_fwd(x, table, w1g, w2g, layer):
    s = x.shape[0]
    ts = _ffn_tile(s)

    def body(x_ref, vec_ref, w1_ref, w2_ref, xo_ref, u_ref, y_ref, hb_ref):
        xv = x_ref[...]
        n, _ = _rms(xv)
        hb = (n * vec_ref[R_GS_F:R_GS_F + 1, :] + vec_ref[R_SH_F:R_SH_F + 1, :]).astype(bf16)
        hb_ref[...] = hb
        yv = jnp.zeros((ts, D_MODEL), f32)
        for f in range(N_DEV):
            u = jnp.maximum(_mm(hb, w1_ref[f]), 0.0)
            u_ref[:, f * FF_CHUNK:(f + 1) * FF_CHUNK] = u.astype(bf16)
            yv = yv + _mm((u * u).astype(bf16), w2_ref[f])
        y_ref[...] = yv.astype(bf16)
        xo_ref[...] = xv + vec_ref[R_GT_F:R_GT_F + 1, :] * yv

    row = pl.BlockSpec((ts, D_MODEL), lambda i: (i, 0))
    return pl.pallas_call(
        body, name=f"ffn_fwd_{layer}", grid=(s // ts,),
        out_shape=(jax.ShapeDtypeStruct((s, D_MODEL), f32), jax.ShapeDtypeStruct((s, D_FF), bf16),
                   jax.ShapeDtypeStruct((s, D_MODEL), bf16), jax.ShapeDtypeStruct((s, D_MODEL), bf16)),
        in_specs=[row, _mod_rows(layer),
                  _layer_weights((D_MODEL, FF_CHUNK)), _layer_weights((FF_CHUNK, D_MODEL))],
        out_specs=(row, pl.BlockSpec((ts, D_FF), lambda i: (i, 0)), row, row),
        compiler_params=_params(("arbitrary",), 56),
    )(x, table, w1g, w2g)


def _ffn_bwd_act(x, dx, u, y, table, w1g, w2g, layer, dep=None):
    s = x.shape[0]
    ts = _ffn_tile(s)

    def body(x_ref, dx_ref, u_ref, y_ref, vec_ref, w1_ref, w2_ref, dxo_ref, da_ref, dyb_ref, sm_ref):
        @pl.when(pl.program_id(0) == 0)
        def _():
            sm_ref[...] = jnp.zeros_like(sm_ref)

        dxv = dx_ref[...]
        dyb = (dxv * vec_ref[R_GT_F:R_GT_F + 1, :]).astype(bf16)
        dyb_ref[...] = dyb
        sm_ref[G_GT:G_GT + 1, :] += _colsum(dxv * y_ref[...].astype(f32))
        dh = jnp.zeros((ts, D_MODEL), f32)
        for f in range(N_DEV):
            cols = slice(f * FF_CHUNK, (f + 1) * FF_CHUNK)
            dz = _mm_nt(dyb, w2_ref[f])
            dab = (dz * (2.0 * u_ref[:, cols].astype(f32))).astype(bf16)
            da_ref[:, cols] = dab
            dh = dh + _mm_nt(dab, w1_ref[f])
        n, r = _rms(x_ref[...])
        sm_ref[G_SH:G_SH + 1, :] += _colsum(dh)
        sm_ref[G_GS:G_GS + 1, :] += _colsum(dh * n)
        dxo_ref[...] = dxv + _norm_bwd(dh, n, r, vec_ref[R_GS_F:R_GS_F + 1, :])

    row = pl.BlockSpec((ts, D_MODEL), lambda i: (i, 0))
    wide = pl.BlockSpec((ts, D_FF), lambda i: (i, 0))
    body, dep_specs, dep_args = _behind(body, 7, dep)
    return pl.pallas_call(
        body, name=f"ffn_bwd_act_{layer}", grid=(s // ts,),
        out_shape=(jax.ShapeDtypeStruct((s, D_MODEL), f32), jax.ShapeDtypeStruct((s, D_FF), bf16),
                   jax.ShapeDtypeStruct((s, D_MODEL), bf16), jax.ShapeDtypeStruct((8, D_MODEL), f32)),
        in_specs=[row, row, wide, row, _mod_rows(layer),
                  _layer_weights((D_MODEL, FF_CHUNK)), _layer_weights((FF_CHUNK, D_MODEL))] + dep_specs,
        out_specs=(row, wide, row, pl.BlockSpec((8, D_MODEL), lambda i: (0, 0))),
        compiler_params=_params(("arbitrary",), 58),
    )(x, dx, u, y, table, w1g, w2g, *dep_args)


def _ffn_bwd_w1(hb, da, layer):
    s = hb.shape[0]
    ts = _ffn_tile(s)
    nt = s // ts

    def body(hb_ref, da_ref, dw_ref, acc_ref):
        i = pl.program_id(0)

        @pl.when(i == 0)
        def _():
            acc_ref[...] = jnp.zeros_like(acc_ref)

        hb = hb_ref[...]
        for f in range(N_DEV):
            acc_ref[f] += _mm_tn(hb, da_ref[:, f * FF_CHUNK:(f + 1) * FF_CHUNK])

        @pl.when(i == nt - 1)
        def _():
            dw_ref[...] = acc_ref[...].astype(bf16)

    return pl.pallas_call(
        body, name=f"ffn_bwd_w1_{layer}", grid=(nt,),
        out_shape=jax.ShapeDtypeStruct((N_DEV, D_MODEL, FF_CHUNK), bf16),
        in_specs=[pl.BlockSpec((ts, D_MODEL), lambda i: (i, 0)), pl.BlockSpec((ts, D_FF), lambda i: (i, 0))],
        out_specs=pl.BlockSpec((N_DEV, D_MODEL, FF_CHUNK), lambda i: (0, 0, 0)),
        scratch_shapes=[pltpu.VMEM((N_DEV, D_MODEL, FF_CHUNK), f32)],
        compiler_params=_params(("arbitrary",), 56),
    )(hb, da)


def _ffn_bwd_w2(u, dyb, layer):
    s = u.shape[0]
    ts = _ffn_tile(s)
    nt = s // ts

    def body(u_ref, dyb_ref, dw_ref, acc_ref):
        i = pl.program_id(0)

        @pl.when(i == 0)
        def _():
            acc_ref[...] = jnp.zeros_like(acc_ref)

        dyb = dyb_ref[...]
        for f in range(N_DEV):
            uv = u_ref[:, f * FF_CHUNK:(f + 1) * FF_CHUNK].astype(f32)
            acc_ref[f] += _mm_tn((uv * uv).astype(bf16), dyb)

        @pl.when(i == nt - 1)
        def _():
            dw_ref[...] = acc_ref[...].astype(bf16)

    return pl.pallas_call(
        body, name=f"ffn_bwd_w2_{layer}", grid=(nt,),
        out_shape=jax.ShapeDtypeStruct((N_DEV, FF_CHUNK, D_MODEL), bf16),
        in_specs=[pl.BlockSpec((ts, D_FF), lambda i: (i, 0)), pl.BlockSpec((ts, D_MODEL), lambda i: (i, 0))],
        out_specs=pl.BlockSpec((N_DEV, FF_CHUNK, D_MODEL), lambda i: (0, 0, 0)),
        scratch_shapes=[pltpu.VMEM((N_DEV, FF_CHUNK, D_MODEL), f32)],
        compiler_params=_params(("arbitrary",), 56),
    )(u, dyb)


def _lru_gates(xc, wsm_ref, pv_ref):
    xcb = xc.astype(bf16)
    gr = _sigmoid(_block_diag(xcb, wsm_ref, 0) + pv_ref[P_BA:P_BA + 1, :])
    gi = _sigmoid(_block_diag(xcb, wsm_ref, 1) + pv_ref[P_BX:P_BX + 1, :])
    log_a = (LRU_C * _log_sigmoid(pv_ref[P_LAM:P_LAM + 1, :])) * gr
    t = jnp.tanh(log_a)
    return gr, gi, jnp.exp(log_a), jnp.sqrt((-2.0 * t) / (1.0 - t))


def _conv(xr, taps_before, pv_ref):
    xc = xr * pv_ref[P_CW0 + 3:P_CW0 + 4, :] + pv_ref[P_CONVB:P_CONVB + 1, :]
    for k, v in zip((2, 1, 0), taps_before):
        xc = xc + v * pv_ref[P_CW0 + k:P_CW0 + k + 1, :]
    return xc


LRU_FWD_SUB, LRU_FWD_SUBS = 128, 4
LRU_BWD_SUB, LRU_BWD_SUBS = 256, 1


def _scan_rows(a, u, carry, reverse):
    groups = a.shape[0] // SUBLANES
    row = lax.broadcasted_iota(jnp.int32, (SUBLANES, a.shape[1]), 0)
    outs = [None] * groups
    for j in range(groups):
        g = groups - 1 - j if reverse else j
        av, uv = a[g * SUBLANES:(g + 1) * SUBLANES], u[g * SUBLANES:(g + 1) * SUBLANES]
        for k in (1, 2, 4):
            if reverse:
                valid, shift = row < SUBLANES - k, SUBLANES - k
            else:
                valid, shift = row >= k, k
            a_s = jnp.where(valid, pltpu.roll(av, shift, 0), 1.0)
            u_s = jnp.where(valid, pltpu.roll(uv, shift, 0), 0.0)
            uv = uv + av * u_s
            av = av * a_s
        h = uv + av * carry
        outs[g] = h
        carry = h[0:1, :] if reverse else h[SUBLANES - 1:SUBLANES, :]
    return jnp.concatenate(outs, axis=0), carry


def _lru_fwd(x, table, wbig, wsm, pvec, layer, dep=None):
    s = x.shape[0]
    sub = min(LRU_FWD_SUB, s)
    ts = min(sub * LRU_FWD_SUBS, s)
    nsub = ts // sub
    w = LRU_WIDTH

    def body(x_ref, vec_ref, wb_ref, wsm_ref, pv_ref, xo_ref, xr_ref, hs_ref, a_ref, mult_ref, gr_ref, gi_ref,
             gel_ref, geld_ref, y_ref, tail_ref, carry_ref):
        @pl.when(pl.program_id(0) == 0)
        def _():
            tail_ref[...] = jnp.zeros_like(tail_ref)
            carry_ref[...] = jnp.zeros_like(carry_ref)

        sel = _shift_matrix(sub, BF16_ROWS, (1, 2, 3))
        for k in range(nsub):
            rows = slice(k * sub, (k + 1) * sub)
            xv = x_ref[rows, :]
            n, _ = _rms(xv)
            hb = (n * vec_ref[R_GS_M:R_GS_M + 1, :] + vec_ref[R_SH_M:R_SH_M + 1, :]).astype(bf16)
            gelu_v, gelu_d = _gelu_and_grad(_mm(hb, wb_ref[0]) + pv_ref[P_BY:P_BY + 1, :])
            gel_ref[rows, :] = gelu_v.astype(bf16)
            geld_ref[rows, :] = gelu_d.astype(bf16)
            xrb = (_mm(hb, wb_ref[1]) + pv_ref[P_BIN:P_BIN + 1, :]).astype(bf16)
            xr_ref[rows, :] = xrb
            xc = _conv(xrb.astype(f32), _shifted_rows(sel, tail_ref[...], xrb), pv_ref)
            tail_ref[...] = xrb[sub - BF16_ROWS:, :]
            gr, gi, a, mult = _lru_gates(xc, wsm_ref, pv_ref)
            gr_ref[rows, :] = gr.astype(bf16)
            gi_ref[rows, :] = gi.astype(bf16)
            a_ref[rows, :] = a
            mult_ref[rows, :] = mult
            hs, carry = _scan_rows(a, mult * (gi * xc), carry_ref[0:1, :], reverse=False)
            carry_ref[0:1, :] = carry
            hs_ref[rows, :] = hs
            yv = _mm((hs * gelu_v).astype(bf16), wb_ref[2]) + pv_ref[P_BOUT:P_BOUT + 1, :]
            y_ref[rows, :] = yv.astype(bf16)
            xo_ref[rows, :] = xv + vec_ref[R_GT_M:R_GT_M + 1, :] * yv

    row = pl.BlockSpec((ts, D_MODEL), lambda i: (i, 0))
    roww = pl.BlockSpec((ts, w), lambda i: (i, 0))
    wide = lambda dt: jax.ShapeDtypeStruct((s, w), dt)
    body, dep_specs, dep_args = _behind(body, 5, dep)
    return pl.pallas_call(
        body, name=f"lru_fwd_{layer}", grid=(s // ts,),
        out_shape=(jax.ShapeDtypeStruct((s, D_MODEL), f32), wide(bf16), wide(f32), wide(f32), wide(f32),
                   wide(bf16), wide(bf16), wide(bf16), wide(bf16), jax.ShapeDtypeStruct((s, D_MODEL), bf16)),
        in_specs=[row, _mod_rows(layer),
                  pl.BlockSpec((3, w, w), lambda i: (0, 0, 0)),
                  pl.BlockSpec((2, HEADS, HEAD_DIM, HEAD_DIM), lambda i: (0, 0, 0, 0)),
                  pl.BlockSpec((16, w), lambda i: (0, 0))] + dep_specs,
        out_specs=(row, roww, roww, roww, roww, roww, roww, roww, roww, row),
        scratch_shapes=[pltpu.VMEM((BF16_ROWS, w), bf16), pltpu.VMEM((SUBLANES, w), f32)],
        compiler_params=_params(("arbitrary",)),
    )(x, table, wbig, wsm, pvec, *dep_args)


def _lru_bwd(x, dx, saved, table, wbig, wsm, pvec, layer, dep=None):
    xr, hs, a_all, mult_all, gr_all, gi_all, gel_all, geld_all, y = saved
    s = x.shape[0]
    sub = min(LRU_BWD_SUB, s)
    ts = min(sub * LRU_BWD_SUBS, s)
    nsub = ts // sub
    nt = s // ts
    w = LRU_WIDTH
    shard = w // N_DEV
    hshard = HEAD_DIM // N_DEV

    def body(x_ref, dx_ref, xr_ref, xrh_ref, hs_ref, hsh_ref, a_ref, mult_ref, gr_ref, gi_ref, gel_ref, geld_ref,
             y_ref, vec_ref, wb_ref, wsm_ref, pv_ref,
             dxo_ref, dwb_ref, dwsm_ref, sm_ref, accb_ref, accs_ref, eps_ref, dxc8_ref,
             hb_scr, dgb_scr, dxrb_scr, mb_scr, dyb_scr, xcb_scr, drab_scr, drxb_scr):
        i = pl.program_id(0)
        first_tile = i == nt - 1

        @pl.when(i == 0)
        def _():
            accb_ref[...] = jnp.zeros_like(accb_ref)
            accs_ref[...] = jnp.zeros_like(accs_ref)
            sm_ref[...] = jnp.zeros_like(sm_ref)
            eps_ref[...] = jnp.zeros_like(eps_ref)
            dxc8_ref[...] = jnp.zeros_like(dxc8_ref)

        gs = vec_ref[R_GS_M:R_GS_M + 1, :]
        c_ls = LRU_C * _log_sigmoid(pv_ref[P_LAM:P_LAM + 1, :])
        for k in reversed(range(nsub)):
            rows = slice(k * sub, (k + 1) * sub)
            xv = x_ref[rows, :]
            dxv = dx_ref[rows, :]
            n, r = _rms(xv)
            hb_scr[rows, :] = (n * gs + vec_ref[R_SH_M:R_SH_M + 1, :]).astype(bf16)
            xrv = xr_ref[rows, :].astype(f32)
            hsv = hs_ref[rows, :]
            if k == 0:
                xr_halo = jnp.where(first_tile, 0.0, xrh_ref[...].astype(f32))
                hs_halo = jnp.where(first_tile, 0.0, hsh_ref[...])
            else:
                xr_halo = xr_ref[k * sub - BF16_ROWS:k * sub, :].astype(f32)
                hs_halo = hs_ref[k * sub - SUBLANES:k * sub, :]
            xs1, xs2, xs3 = _rows_before(xr_halo, xrv, (1, 2, 3))
            xc = _conv(xrv, (xs1, xs2, xs3), pv_ref)
            xcb_scr[rows, :] = xc.astype(bf16)
            a, mult = a_ref[rows, :], mult_ref[rows, :]
            gr, gi = gr_ref[rows, :].astype(f32), gi_ref[rows, :].astype(f32)
            gelu_v = gel_ref[rows, :].astype(f32)

            dy = dxv * vec_ref[R_GT_M:R_GT_M + 1, :]
            dyb = dy.astype(bf16)
            dyb_scr[rows, :] = dyb
            sm_ref[G_GT:G_GT + 1, :] += _colsum(dxv * y_ref[rows, :].astype(f32))
            sm_ref[G_BOUT:G_BOUT + 1, :] += _colsum(dy)
            mb_scr[rows, :] = (hsv * gelu_v).astype(bf16)
            dm = _mm_nt(dyb, wb_ref[2])
            dhs = dm * gelu_v
            dgpre = dm * hsv * geld_ref[rows, :].astype(f32)
            dgb = dgpre.astype(bf16)
            dgb_scr[rows, :] = dgb
            sm_ref[G_BY:G_BY + 1, :] += _colsum(dgpre)

            eps_in = eps_ref[0:1, :]
            eps, eps_out = _scan_rows(a, a * dhs, eps_in, reverse=True)
            eps_ref[0:1, :] = eps_out
            (eps_next,) = _rows_after(eps, jnp.broadcast_to(eps_in, (SUBLANES, w)), (1,))
            delta = dhs + eps_next
            (h_prev,) = _rows_before(hs_halo, hsv, (1,))
            dxi = delta * xc
            dgi = dxi * mult
            dla = (delta * h_prev) * a - (dxi * gi) * (a * a) / mult
            sm_ref[G_LS:G_LS + 1, :] += _colsum(dla * gr)
            dra = (dla * c_ls) * (gr - gr * gr)
            drx = dgi * (gi - gi * gi)
            drab, drxb = dra.astype(bf16), drx.astype(bf16)
            drab_scr[rows, :] = drab
            drxb_scr[rows, :] = drxb
            sm_ref[G_BA:G_BA + 1, :] += _colsum(dra)
            sm_ref[G_BX:G_BX + 1, :] += _colsum(drx)
            dxc = (delta * mult) * gi + _block_diag_t(drab, wsm_ref, 0) + _block_diag_t(drxb, wsm_ref, 1)

            sm_ref[G_CONVB:G_CONVB + 1, :] += _colsum(dxc)
            for kk, v in zip((3, 2, 1, 0), (xrv, xs1, xs2, xs3)):
                sm_ref[G_CW0 + kk:G_CW0 + kk + 1, :] += _colsum(dxc * v)
            ups = _rows_after(dxc, dxc8_ref[...], (1, 2, 3))
            dxc8_ref[...] = dxc[0:SUBLANES, :]
            dxr = dxc * pv_ref[P_CW0 + 3:P_CW0 + 4, :]
            for kk, v in zip((2, 1, 0), ups):
                dxr = dxr + v * pv_ref[P_CW0 + kk:P_CW0 + kk + 1, :]
            dxrb = dxr.astype(bf16)
            dxrb_scr[rows, :] = dxrb
            sm_ref[G_BIN:G_BIN + 1, :] += _colsum(dxr)
            dh = _mm_nt(dgb, wb_ref[0]) + _mm_nt(dxrb, wb_ref[1])
            sm_ref[G_SH:G_SH + 1, :] += _colsum(dh)
            sm_ref[G_GS:G_GS + 1, :] += _colsum(dh * n)
            dxo_ref[rows, :] = dxv + _norm_bwd(dh, n, r, gs)

        hb = hb_scr[...]
        accb_ref[0] += _mm_tn(hb, dgb_scr[...])
        accb_ref[1] += _mm_tn(hb, dxrb_scr[...])
        accb_ref[2] += _mm_tn(mb_scr[...], dyb_scr[...])
        for h in range(HEADS):
            cols = slice(h * HEAD_DIM, (h + 1) * HEAD_DIM)
            accs_ref[0, h] += _mm_tn(xcb_scr[:, cols], drab_scr[:, cols])
            accs_ref[1, h] += _mm_tn(xcb_scr[:, cols], drxb_scr[:, cols])

        @pl.when(i == nt - 1)
        def _():
            sm_ref[G_LS:G_LS + 1, :] = sm_ref[G_LS:G_LS + 1, :] * LRU_C
            for k in range(3):
                dwb_ref[:, k] = accb_ref[k].astype(bf16).reshape(N_DEV, shard, w)
            for k in range(2):
                for h in range(HEADS):
                    dwsm_ref[:, k, h] = accs_ref[k, h].astype(bf16).reshape(N_DEV, hshard, HEAD_DIM)

    rev = lambda i: (nt - 1 - i, 0)
    row = pl.BlockSpec((ts, D_MODEL), rev)
    roww = pl.BlockSpec((ts, w), rev)
    halo16 = pl.BlockSpec((BF16_ROWS, w), lambda i: (jnp.maximum((nt - 1 - i) * (ts // BF16_ROWS) - 1, 0), 0))
    halo8 = pl.BlockSpec((SUBLANES, w), lambda i: (jnp.maximum((nt - 1 - i) * (ts // SUBLANES) - 1, 0), 0))
    const = lambda *shape: pl.BlockSpec(shape, lambda i: (0,) * len(shape))
    operand = pltpu.VMEM((ts, w), bf16)
    body, dep_specs, dep_args = _behind(body, 17, dep)
    return pl.pallas_call(
        body, name=f"lru_bwd_{layer}", grid=(nt,),
        out_shape=(jax.ShapeDtypeStruct((s, D_MODEL), f32),
                   jax.ShapeDtypeStruct((N_DEV, 3, shard, w), bf16),
                   jax.ShapeDtypeStruct((N_DEV, 2, HEADS, hshard, HEAD_DIM), bf16),
                   jax.ShapeDtypeStruct((16, w), f32)),
        in_specs=[row, row, roww, halo16, roww, halo8, roww, roww, roww, roww, roww, roww, row, _mod_rows(layer),
                  const(3, w, w), const(2, HEADS, HEAD_DIM, HEAD_DIM), const(16, w)] + dep_specs,
        out_specs=(row, const(N_DEV, 3, shard, w), const(N_DEV, 2, HEADS, hshard, HEAD_DIM), const(16, w)),
        scratch_shapes=[pltpu.VMEM((3, w, w), f32), pltpu.VMEM((2, HEADS, HEAD_DIM, HEAD_DIM), f32),
                        pltpu.VMEM((SUBLANES, w), f32), pltpu.VMEM((SUBLANES, w), f32)] + [operand] * 8,
        compiler_params=_params(("arbitrary",), 58),
    )(x, dx, xr, xr, hs, hs, a_all, mult_all, gr_all, gi_all, gel_all, geld_all, y, table, wbig, wsm, pvec, *dep_args)


def _pool_tile(s):
    return min(1024, s)


def _pool_counts(tile_index, ts):
    t = (tile_index * ts + lax.broadcasted_iota(jnp.int32, (ts, 1), 0) + 1).astype(f32)
    return [1.0 / jnp.minimum(t, float(win)) for win in POOL_WINDOWS]


def _pooled(h, halo, inv):
    ext = jnp.concatenate([halo, h], axis=0)
    out = []
    for g in range(len(POOL_WINDOWS)):
        acc = ext[:, g * HEAD_DIM:(g + 1) * HEAD_DIM]
        for step in range(g + 1):
            acc = acc + pltpu.roll(acc, 1 << step, 0)
        out.append(acc[POOL_HALO:] * inv[g] - h[:, g * HEAD_DIM:(g + 1) * HEAD_DIM])
    return out


def _pool_fwd(x, table, pw, ps, layer):
    s = x.shape[0]
    ts = _pool_tile(s)

    def body(x_ref, vec_ref, pw_ref, ps_ref, xo_ref, y_ref, halo_ref):
        i = pl.program_id(0)

        @pl.when(i == 0)
        def _():
            halo_ref[...] = jnp.zeros_like(halo_ref)

        xv = x_ref[...]
        n, _ = _rms(xv)
        h = n * vec_ref[R_GS_M:R_GS_M + 1, :] + vec_ref[R_SH_M:R_SH_M + 1, :]
        pooled = _pooled(h, halo_ref[...], _pool_counts(i, ts))
        halo_ref[...] = h[ts - POOL_HALO:, :]
        mixed = jnp.concatenate([_mm(pooled[g].astype(bf16), pw_ref[g]) for g in range(HEADS)], axis=1)
        yv = mixed * ps_ref[0:1, :]
        y_ref[...] = yv.astype(bf16)
        xo_ref[...] = xv + vec_ref[R_GT_M:R_GT_M + 1, :] * yv

    row = pl.BlockSpec((ts, D_MODEL), lambda i: (i, 0))
    return pl.pallas_call(
        body, name=f"pool_fwd_{layer}", grid=(s // ts,),
        out_shape=(jax.ShapeDtypeStruct((s, D_MODEL), f32), jax.ShapeDtypeStruct((s, D_MODEL), bf16)),
        in_specs=[row, _mod_rows(layer),
                  pl.BlockSpec((HEADS, HEAD_DIM, HEAD_DIM), lambda i: (0, 0, 0)),
                  pl.BlockSpec((8, D_MODEL), lambda i: (0, 0))],
        out_specs=(row, row),
        scratch_shapes=[pltpu.VMEM((POOL_HALO, D_MODEL), f32)],
        compiler_params=_params(("arbitrary",)),
    )(x, table, pw, ps)


def _pool_bwd(x, dx, y, table, pw, ps, layer, dep=None):
    s = x.shape[0]
    ts = _pool_tile(s)
    nt = s // ts
    hshard = HEAD_DIM // N_DEV

    def body(x_ref, xh_ref, dx_ref, y_ref, vec_ref, pw_ref, ps_ref, dxo_ref, dpw_ref, sm_ref, acc_ref, q16_ref):
        i = pl.program_id(0)
        tile = nt - 1 - i

        @pl.when(i == 0)
        def _():
            acc_ref[...] = jnp.zeros_like(acc_ref)
            sm_ref[...] = jnp.zeros_like(sm_ref)
            q16_ref[...] = jnp.zeros_like(q16_ref)

        gs, sh = vec_ref[R_GS_M:R_GS_M + 1, :], vec_ref[R_SH_M:R_SH_M + 1, :]
        xv = x_ref[...]
        dxv = dx_ref[...]
        n, r = _rms(xv)
        h = n * gs + sh
        nh, _ = _rms(xh_ref[...])
        halo = jnp.where(tile == 0, 0.0, nh * gs + sh)
        inv = _pool_counts(tile, ts)
        pooled = _pooled(h, halo, inv)
        mixed = jnp.concatenate([_mm(pooled[g].astype(bf16), pw_ref[g]) for g in range(HEADS)], axis=1)

        dy = dxv * vec_ref[R_GT_M:R_GT_M + 1, :]
        sm_ref[G_GT:G_GT + 1, :] += _colsum(dxv * y_ref[...].astype(f32))
        sm_ref[3:4, :] += _colsum(dy * mixed)
        dmixed = (dy * ps_ref[0:1, :]).astype(bf16)
        dh_parts = []
        for g in range(HEADS):
            cols = slice(g * HEAD_DIM, (g + 1) * HEAD_DIM)
            acc_ref[g] += _mm_tn(pooled[g].astype(bf16), dmixed[:, cols])
            dpooled = _mm_nt(dmixed[:, cols], pw_ref[g])
            q = dpooled * inv[g]
            ext = jnp.concatenate([q, q16_ref[:, cols]], axis=0)
            q16_ref[:, cols] = q[0:POOL_HALO, :]
            for step in range(g + 1):
                ext = ext + pltpu.roll(ext, ext.shape[0] - (1 << step), 0)
            dh_parts.append(ext[:ts] - dpooled)
        dh = jnp.concatenate(dh_parts, axis=1)
        sm_ref[G_SH:G_SH + 1, :] += _colsum(dh)
        sm_ref[G_GS:G_GS + 1, :] += _colsum(dh * n)
        dxo_ref[...] = dxv + _norm_bwd(dh, n, r, gs)

        @pl.when(i == nt - 1)
        def _():
            for g in range(HEADS):
                dpw_ref[:, g] = acc_ref[g].astype(bf16).reshape(N_DEV, hshard, HEAD_DIM)

    rev = lambda i: (nt - 1 - i, 0)
    row = pl.BlockSpec((ts, D_MODEL), rev)
    halo16 = pl.BlockSpec((POOL_HALO, D_MODEL), lambda i: (jnp.maximum((nt - 1 - i) * (ts // POOL_HALO) - 1, 0), 0))
    const = lambda *shape: pl.BlockSpec(shape, lambda i: (0,) * len(shape))
    body, dep_specs, dep_args = _behind(body, 7, dep)
    return pl.pallas_call(
        body, name=f"pool_bwd_{layer}", grid=(nt,),
        out_shape=(jax.ShapeDtypeStruct((s, D_MODEL), f32),
                   jax.ShapeDtypeStruct((N_DEV, HEADS, hshard, HEAD_DIM), bf16),
                   jax.ShapeDtypeStruct((8, D_MODEL), f32)),
        in_specs=[row, halo16, row, row, _mod_rows(layer), const(HEADS, HEAD_DIM, HEAD_DIM),
                  const(8, D_MODEL)] + dep_specs,
        out_specs=(row, const(N_DEV, HEADS, hshard, HEAD_DIM), const(8, D_MODEL)),
        scratch_shapes=[pltpu.VMEM((HEADS, HEAD_DIM, HEAD_DIM), f32), pltpu.VMEM((POOL_HALO, D_MODEL), f32)],
        compiler_params=_params(("arbitrary",)),
    )(x, x, dx, y, table, pw, ps, *dep_args)


def _final(x, target, g_fin):
    s = x.shape[0]
    ts = min(1024, s)

    def body(x_ref, t_ref, g_ref, dx_ref, sm_ref):
        @pl.when(pl.program_id(0) == 0)
        def _():
            sm_ref[...] = jnp.zeros_like(sm_ref)

        g = g_ref[0:1, :]
        n, r = _rms(x_ref[...])
        err = n * g - t_ref[...]
        sm_ref[1:2, :] += 0.5 * jnp.sum(jnp.mean(err * err, axis=-1, keepdims=True), axis=0, keepdims=True)
        dyv = err * (1.0 / D_MODEL)
        sm_ref[0:1, :] += _colsum(dyv * n)
        dx_ref[...] = _norm_bwd(dyv, n, r, g)

    row = pl.BlockSpec((ts, D_MODEL), lambda i: (i, 0))
    return pl.pallas_call(
        body, name="final_loss", grid=(s // ts,),
        out_shape=(jax.ShapeDtypeStruct((s, D_MODEL), f32), jax.ShapeDtypeStruct((8, D_MODEL), f32)),
        in_specs=[row, row, pl.BlockSpec((8, D_MODEL), lambda i: (0, 0))],
        out_specs=(row, pl.BlockSpec((8, D_MODEL), lambda i: (0, 0))),
        compiler_params=_params(("arbitrary",)),
    )(x, target, g_fin)


def _small_pack(sm_ffn, sm_mix, sm_fin, table, g_mix, g_ffn, lam):
    def body(*refs):
        ffn, mix = refs[0:DEPTH], refs[DEPTH:2 * DEPTH]
        fin_ref, tab_ref, gm_ref, gf_ref, lam_ref, o_ref = refs[2 * DEPTH:]
        o_ref[...] = jnp.zeros_like(o_ref)
        for i in range(DEPTH):
            base = K_MOD + i * N_MOD
            o_ref[base + 0:base + 1, :] = mix[i][G_SH:G_SH + 1, :]
            o_ref[base + 1:base + 2, :] = mix[i][G_GS:G_GS + 1, :] * gm_ref[i:i + 1, :]
            o_ref[base + 2:base + 3, :] = mix[i][G_GT:G_GT + 1, :]
            o_ref[base + 3:base + 4, :] = ffn[i][G_SH:G_SH + 1, :]
            o_ref[base + 4:base + 5, :] = ffn[i][G_GS:G_GS + 1, :] * gf_ref[i:i + 1, :]
            o_ref[base + 5:base + 6, :] = ffn[i][G_GT:G_GT + 1, :]
            o_ref[K_NMIX + i:K_NMIX + i + 1, :] = mix[i][G_GS:G_GS + 1, :] * (1.0 + tab_ref[i, R_SC_M:R_SC_M + 1, :])
            o_ref[K_NFFN + i:K_NFFN + i + 1, :] = ffn[i][G_GS:G_GS + 1, :] * (1.0 + tab_ref[i, R_SC_F:R_SC_F + 1, :])
            j = i // 2
            if i % 2 == 0:
                for k, src in enumerate((G_BY, G_BIN, G_CONVB, None, G_BOUT)):
                    dst = K_LRUB + j * 5 + k
                    if src is None:
                        o_ref[dst:dst + 1, :] = mix[i][G_LS:G_LS + 1, :] * _sigmoid(-lam_ref[j:j + 1, :])
                    else:
                        o_ref[dst:dst + 1, :] = mix[i][src:src + 1, :]
                o_ref[K_CONVW + j * 4:K_CONVW + j * 4 + 4, :] = mix[i][G_CW0:G_CW0 + 4, :]
                o_ref[K_BA + j:K_BA + j + 1, :] = mix[i][G_BA:G_BA + 1, :]
                o_ref[K_BX + j:K_BX + j + 1, :] = mix[i][G_BX:G_BX + 1, :]
            else:
                o_ref[K_PS + j:K_PS + j + 1, :] = mix[i][3:4, :]
        o_ref[K_FIN:K_FIN + 2, :] = fin_ref[0:2, :]

    return pl.pallas_call(body, name="small_pack", out_shape=jax.ShapeDtypeStruct((K_ROWS, D_MODEL), f32))(
        *sm_ffn, *sm_mix, sm_fin, table, g_mix, g_ffn, lam)


def _small_sum(gathered):
    def body(g_ref, o_ref, token_ref):
        tot = g_ref[0]
        for src in range(1, N_DEV):
            tot = tot + g_ref[src]
        o_ref[...] = tot
        token_ref[...] = jnp.zeros_like(token_ref)

    return pl.pallas_call(
        body, name="small_sum",
        out_shape=(jax.ShapeDtypeStruct(gathered.shape[1:], f32), jax.ShapeDtypeStruct((8, 128), f32)))(gathered)


def _adamw_math(g, w, m, v):
    m = ADAM_B1 * m + (1.0 - ADAM_B1) * g
    v = ADAM_B2 * v + (1.0 - ADAM_B2) * (g * g)
    m_hat = m / (1.0 - ADAM_B1 ** ADAM_STEP)
    v_hat = v / (1.0 - ADAM_B2 ** ADAM_STEP)
    delta = -ADAM_LR * (m_hat / (jnp.sqrt(v_hat) + ADAM_EPS) + ADAM_WD * w)
    return delta, m, v


def _adamw_small(params):
    n = len(params)
    shapes = [w.shape for _, w, _, _ in params]
    two_d = [(1, s[0]) if len(s) == 1 else (math.prod(s[:-1]), s[-1]) for s in shapes]

    def body(*refs):
        ins, outs = refs[:4 * n], refs[4 * n:]
        for k in range(n):
            g_ref, w_ref, m_ref, v_ref = ins[4 * k:4 * k + 4]
            outs[3 * k][...], outs[3 * k + 1][...], outs[3 * k + 2][...] = _adamw_math(
                g_ref[...], w_ref[...], m_ref[...], v_ref[...])

    outs = pl.pallas_call(
        body, name="adamw_small",
        out_shape=tuple(jax.ShapeDtypeStruct(two_d[k], f32) for k in range(n) for _ in range(3)))(
        *(t.reshape(two_d[k]) for k, p in enumerate(params) for t in p))
    return [tuple(outs[3 * k + i].reshape(shapes[k]) for i in range(3)) for k in range(n)]


def _block_rows(rows, cols):
    tr = max(SUBLANES, min(rows, (512 * 1024) // (4 * cols)))
    while rows % tr:
        tr //= 2
    return tr


def _adamw_reduce(name, landings, kind, w, m, v):
    nl = len(landings)
    rows, cols = landings[0].shape[2:]
    tr = _block_rows(rows, cols)
    per_layer = rows // tr

    def body(*refs):
        l_refs = refs[:nl]
        w_ref, m_ref, v_ref, g_ref, d_ref, mo_ref, vo_ref = refs[nl:]
        layer = pl.program_id(0)
        for k in range(nl):
            @pl.when(layer == k)
            def _(k=k):
                g = l_refs[k][0].astype(f32)
                for src in range(1, N_DEV):
                    g = g + l_refs[k][src].astype(f32)
                g_ref[...] = g
        d_ref[...], mo_ref[...], vo_ref[...] = _adamw_math(g_ref[...], w_ref[...], m_ref[...], v_ref[...])

    blk = pl.BlockSpec((tr, cols), lambda l, r: (l * per_layer + r, 0))
    land = [pl.BlockSpec((N_DEV, None, tr, cols), lambda l, r, k=k: (0, kind, jnp.where(l == k, r, 0), 0)) for k in range(nl)]
    return pl.pallas_call(
        body, name=f"adamw_{name}", grid=(nl, per_layer),
        out_shape=tuple(jax.ShapeDtypeStruct((nl * rows, cols), f32) for _ in range(4)),
        in_specs=land + [blk, blk, blk],
        out_specs=(blk, blk, blk, blk),
        compiler_params=_params(("arbitrary", "arbitrary"), 32),
    )(*landings, w, m, v)


def _adamw_w_mod(c_all, dmod_all, w, m, v):
    depth, d, cols = w.shape
    tr = 256

    def body(c_ref, dm_ref, w_ref, m_ref, v_ref, g_ref, d_ref, mo_ref, vo_ref):
        cv = c_ref[...]
        cond = cv * _sigmoid(cv)
        g = lax.dot_general(cond, dm_ref[...], (((0,), (0,)), ((), ())), preferred_element_type=f32,
                            precision=lax.Precision.HIGHEST)
        g_ref[...] = g
        d_ref[...], mo_ref[...], vo_ref[...] = _adamw_math(g, w_ref[...], m_ref[...], v_ref[...])

    blk = pl.BlockSpec((None, tr, cols), lambda i, r: (i, r, 0))
    return pl.pallas_call(
        body, name="adamw_w_mod", grid=(depth, d // tr),
        out_shape=tuple(jax.ShapeDtypeStruct(w.shape, f32) for _ in range(4)),
        in_specs=[pl.BlockSpec((N_DEV, tr), lambda i, r: (0, r)),
                  pl.BlockSpec((None, N_DEV, cols), lambda i, r: (i, 0, 0)), blk, blk, blk],
        out_specs=(blk, blk, blk, blk),
        compiler_params=_params(("arbitrary", "arbitrary"), 32),
    )(c_all, dmod_all, w, m, v)


def kernel(x, c, w_mod, b_mod, norm_mix_g, norm_ffn_g, lru_w_y, lru_b_y, lru_w_in, lru_b_in, lru_conv_w, lru_conv_b, lru_w_a, lru_b_a, lru_w_x, lru_b_x, lru_lambda, lru_w_out, lru_b_out, pool_w, pool_scale, ffn_w1, ffn_w2, final_norm_g, loss_target, m_w_mod, m_b_mod, m_norm_mix_g, m_norm_ffn_g, m_lru_w_y, m_lru_b_y, m_lru_w_in, m_lru_b_in, m_lru_conv_w, m_lru_conv_b, m_lru_w_a, m_lru_b_a, m_lru_w_x, m_lru_b_x, m_lru_lambda, m_lru_w_out, m_lru_b_out, m_pool_w, m_pool_scale, m_ffn_w1, m_ffn_w2, m_final_norm_g, v_w_mod, v_b_mod, v_norm_mix_g, v_norm_ffn_g, v_lru_w_y, v_lru_b_y, v_lru_w_in, v_lru_b_in, v_lru_conv_w, v_lru_conv_b, v_lru_w_a, v_lru_b_a, v_lru_w_x, v_lru_b_x, v_lru_lambda, v_lru_w_out, v_lru_b_out, v_pool_w, v_pool_scale, v_ffn_w1, v_ffn_w2, v_final_norm_g):
    me = 4 * lax.axis_index("x") + 2 * lax.axis_index("y") + lax.axis_index("c")
    n_lru = lru_w_y.shape[0]
    shard = LRU_WIDTH // N_DEV
    hshard = HEAD_DIM // N_DEV
    xs = x[0]
    target = loss_target[0]

    small_vecs = jnp.concatenate([
        lru_conv_w.reshape(n_lru * 4, shard), lru_b_a.reshape(n_lru, HEADS * hshard),
        lru_b_x.reshape(n_lru, HEADS * hshard), pool_scale, jnp.zeros((2, shard), f32)], axis=0)
    (first_cond, first_mix), _ = _send_start("gather_first_start", [[small_vecs, c], [
        jnp.stack([lru_w_y[0], lru_w_in[0], lru_w_out[0]]).astype(bf16),
        jnp.stack([lru_w_a[0], lru_w_x[0]]).astype(bf16)]], True, me)
    (sv_g, c_g), _ = _send_wait("gather_cond_wait", first_cond, small_vecs)
    conv_w_full = sv_g[:, 0:8].reshape(N_DEV, n_lru, 4, shard).transpose(1, 2, 0, 3).reshape(n_lru, 4, LRU_WIDTH)
    b_a_full = sv_g[:, 8:10].reshape(N_DEV, n_lru, HEADS, hshard).transpose(1, 2, 0, 3).reshape(n_lru, LRU_WIDTH)
    b_x_full = sv_g[:, 10:12].reshape(N_DEV, n_lru, HEADS, hshard).transpose(1, 2, 0, 3).reshape(n_lru, LRU_WIDTH)
    ps_full = sv_g[:, 12:14].transpose(1, 0, 2).reshape(n_lru, D_MODEL)
    c_all = c_g.reshape(N_DEV, D_MODEL)

    parts = []
    for i in range(DEPTH):
        j = i // 2
        if i > 0 and i % 2 == 0:
            parts.append([jnp.stack([lru_w_y[j], lru_w_in[j], lru_w_out[j]]).astype(bf16),
                          jnp.stack([lru_w_a[j], lru_w_x[j]]).astype(bf16)])
        elif i % 2 == 1:
            parts.append([pool_w[j].astype(bf16)])
        parts.append([ffn_w1[i].astype(bf16), ffn_w2[i].astype(bf16)])

    (mod_g,) = _exchange([_mod_part(c_all, w_mod)], True, "gather_mod", pieces=DEPTH,
                         after=[a for part in parts for a in part])
    mod_row = lax.dynamic_index_in_dim(mod_g, me, axis=2, keepdims=False)
    mod_row = mod_row.transpose(1, 0, 2).reshape(DEPTH, N_MOD * D_MODEL)
    table, token = _mod_table(mod_row, b_mod, norm_mix_g, norm_ffn_g)

    first_got, token = _send_wait("gather_mix_wait_0", first_mix, token)
    handles, token = _send_start("gather_rest_start", parts, True, me, after=token)
    h_ffn = [handles[0], handles[2], handles[4], handles[6]]
    h_mix = [None, handles[1], handles[3], handles[5]]

    zero_row = jnp.zeros((1, LRU_WIDTH), f32)
    pvecs = [jnp.concatenate([lru_b_y[j:j + 1], lru_b_in[j:j + 1], lru_conv_b[j:j + 1], b_a_full[j:j + 1],
                              b_x_full[j:j + 1], lru_lambda[j:j + 1], lru_b_out[j:j + 1], zero_row,
                              conv_w_full[j], zero_row, zero_row, zero_row, zero_row], axis=0) for j in range(n_lru)]
    ps_rows = [jnp.concatenate([ps_full[j:j + 1], jnp.zeros((7, D_MODEL), f32)], axis=0) for j in range(n_lru)]

    saved = []
    ffn_w, mix_w = [], []
    h = xs
    for i in range(DEPTH):
        j = i // 2
        got = first_got if i == 0 else _send_wait(f"gather_mix_wait_{i}", h_mix[i], h)[0]
        if i % 2 == 0:
            got = [got[0].reshape(N_DEV, 3, shard, LRU_WIDTH), got[1].reshape(N_DEV, 2, HEADS, hshard, HEAD_DIM)]
            mix_w.append((got[0].transpose(1, 0, 2, 3).reshape(3, LRU_WIDTH, LRU_WIDTH),
                          got[1].transpose(1, 2, 0, 3, 4).reshape(2, HEADS, HEAD_DIM, HEAD_DIM)))
            h_mid, *lru_saved = _lru_fwd(h, table, mix_w[i][0], mix_w[i][1], pvecs[j], i, dep=token)
            mix_saved = (h, tuple(lru_saved))
        else:
            mix_w.append((got[0].transpose(1, 0, 2, 3).reshape(HEADS, HEAD_DIM, HEAD_DIM),))
            h_mid, y_mix = _pool_fwd(h, table, mix_w[i][0], ps_rows[j], i)
            mix_saved = (h, y_mix)
        ffn_w.append(_send_wait(f"gather_ffn_wait_{i}", h_ffn[i], h_mid)[0])
        h_out, u, y_ffn, hb = _ffn_fwd(h_mid, table, ffn_w[i][0], ffn_w[i][1], i)
        saved.append((mix_saved, (h_mid, u, y_ffn, hb)))
        h = h_out
    fin_rows = jnp.concatenate([final_norm_g[None, :], jnp.zeros((7, D_MODEL), f32)], axis=0)
    dx, sm_fin = _final(h, target, fin_rows)

    sm_ffn, sm_mix = [None] * DEPTH, [None] * DEPTH
    x_ffn, x_mix = [None] * DEPTH, [None] * DEPTH
    token = jnp.zeros((8, 128), f32)
    last_mix = None
    for i in reversed(range(DEPTH)):
        j = i // 2
        mix_saved, (h_mid, u, y_ffn, hb) = saved[i]
        dx, da, dyb, sm_ffn[i] = _ffn_bwd_act(h_mid, dx, u, y_ffn, table, ffn_w[i][0], ffn_w[i][1], i, dep=token)
        ffn_grads = [_ffn_bwd_w1(hb, da, i), _ffn_bwd_w2(u, dyb, i)]
        if last_mix is None:
            (x_ffn[i],), token = _send_start(f"grads_start_{i}", [ffn_grads], False, me)
        else:
            (x_mix[i + 1], x_ffn[i]), token = _send_start(f"grads_start_{i}", [last_mix, ffn_grads], False, me)
        if i % 2 == 0:
            h_in, lru_saved = mix_saved
            dx, dbig, dsmall, sm_mix[i] = _lru_bwd(
                h_in, dx, lru_saved, table, mix_w[i][0], mix_w[i][1], pvecs[j], i, dep=token)
            last_mix = [dbig, dsmall]
        else:
            h_in, y_mix = mix_saved
            dx, dpool, sm = _pool_bwd(h_in, dx, y_mix, table, mix_w[i][0], ps_rows[j], i, dep=token)
            sm_mix[i] = jnp.concatenate([sm, jnp.zeros((8, D_MODEL), f32)], axis=0)
            last_mix = [dpool]
    grad_x = dx[None]

    pack = _small_pack(sm_ffn, sm_mix, sm_fin, table + token[0, 0], norm_mix_g, norm_ffn_g, lru_lambda)
    (pack_g,) = _exchange([pack], True, "gather_small_grads", pieces=4)
    tot, token = _small_sum(pack_g)
    loss = tot[K_LOSS, 0]
    (x_mix[0],), _ = _send_start("grads_last_start", [[t + token[0, 0].astype(bf16) for t in last_mix]], False, me)
    cols = w_mod.shape[2]
    dmod_all = lax.dynamic_slice_in_dim(pack_g[:, K_MOD:K_MOD + DEPTH * N_MOD].reshape(N_DEV, DEPTH, N_MOD * D_MODEL),
                                        me * cols, cols, axis=2).transpose(1, 0, 2)
    results = {"w_mod": _adamw_w_mod(c_all, dmod_all, w_mod, m_w_mod, v_w_mod)}

    after = results["w_mod"][1]
    l_ffn = [_send_wait(f"grads_ffn_wait_{i}", x_ffn[i], after)[0] for i in reversed(range(DEPTH))][::-1]

    def reduce_update(name, landings, kind, w, m, v):
        rows = w.size // w.shape[-1]
        two_d = (rows, w.shape[-1])
        lands = [t.reshape(N_DEV, -1, rows // len(landings), w.shape[-1]) for t in landings]
        outs = _adamw_reduce(name, lands, kind, w.reshape(two_d), m.reshape(two_d), v.reshape(two_d))
        return tuple(t.reshape(w.shape) for t in outs)

    results["ffn_w1"] = reduce_update("ffn_w1", [t[0] for t in l_ffn], 0, ffn_w1, m_ffn_w1, v_ffn_w1)
    results["ffn_w2"] = reduce_update("ffn_w2", [t[1] for t in l_ffn], 0, ffn_w2, m_ffn_w2, v_ffn_w2)
    after = results["ffn_w2"][1]
    l_mix = [_send_wait(f"grads_mix_wait_{i}", x_mix[i], after)[0] for i in reversed(range(DEPTH))][::-1]
    l_lru_big = [l_mix[i][0] for i in range(0, DEPTH, 2)]
    l_lru_small = [l_mix[i][1] for i in range(0, DEPTH, 2)]
    l_pool = [l_mix[i][0] for i in range(1, DEPTH, 2)]
    results["lru_w_y"] = reduce_update("lru_w_y", l_lru_big, 0, lru_w_y, m_lru_w_y, v_lru_w_y)
    results["lru_w_in"] = reduce_update("lru_w_in", l_lru_big, 1, lru_w_in, m_lru_w_in, v_lru_w_in)
    results["lru_w_out"] = reduce_update("lru_w_out", l_lru_big, 2, lru_w_out, m_lru_w_out, v_lru_w_out)
    results["lru_w_a"] = reduce_update("lru_w_a", l_lru_small, 0, lru_w_a, m_lru_w_a, v_lru_w_a)
    results["lru_w_x"] = reduce_update("lru_w_x", l_lru_small, 1, lru_w_x, m_lru_w_x, v_lru_w_x)
    results["pool_w"] = reduce_update("pool_w", l_pool, 0, pool_w, m_pool_w, v_pool_w)

    def my_cols(full, width):
        return lax.dynamic_slice_in_dim(full, me * width, width, axis=full.ndim - 1)

    lru_rows = tot[K_LRUB:K_LRUB + 5 * n_lru].reshape(n_lru, 5, LRU_WIDTH)
    small_grads = {
        "b_mod": tot[K_MOD:K_MOD + DEPTH * N_MOD].reshape(DEPTH, N_MOD * D_MODEL),
        "norm_mix_g": tot[K_NMIX:K_NMIX + DEPTH],
        "norm_ffn_g": tot[K_NFFN:K_NFFN + DEPTH],
        "lru_b_y": lru_rows[:, 0], "lru_b_in": lru_rows[:, 1], "lru_conv_b": lru_rows[:, 2],
        "lru_lambda": lru_rows[:, 3], "lru_b_out": lru_rows[:, 4],
        "lru_conv_w": my_cols(tot[K_CONVW:K_CONVW + 4 * n_lru].reshape(n_lru, 4, LRU_WIDTH), shard),
        "lru_b_a": my_cols(tot[K_BA:K_BA + n_lru].reshape(n_lru, HEADS, HEAD_DIM), hshard),
        "lru_b_x": my_cols(tot[K_BX:K_BX + n_lru].reshape(n_lru, HEADS, HEAD_DIM), hshard),
        "pool_scale": my_cols(tot[K_PS:K_PS + n_lru], shard),
        "final_norm_g": tot[K_FIN],
    }
    given = dict(b_mod=(b_mod, m_b_mod, v_b_mod), norm_mix_g=(norm_mix_g, m_norm_mix_g, v_norm_mix_g),
                 norm_ffn_g=(norm_ffn_g, m_norm_ffn_g, v_norm_ffn_g), lru_b_y=(lru_b_y, m_lru_b_y, v_lru_b_y),
                 lru_b_in=(lru_b_in, m_lru_b_in, v_lru_b_in), lru_conv_w=(lru_conv_w, m_lru_conv_w, v_lru_conv_w),
                 lru_conv_b=(lru_conv_b, m_lru_conv_b, v_lru_conv_b), lru_b_a=(lru_b_a, m_lru_b_a, v_lru_b_a),
                 lru_b_x=(lru_b_x, m_lru_b_x, v_lru_b_x), lru_lambda=(lru_lambda, m_lru_lambda, v_lru_lambda),
                 lru_b_out=(lru_b_out, m_lru_b_out, v_lru_b_out), pool_scale=(pool_scale, m_pool_scale, v_pool_scale),
                 final_norm_g=(final_norm_g, m_final_norm_g, v_final_norm_g))
    updates = _adamw_small([(g,) + given[name] for name, g in small_grads.items()])
    for (name, g), update in zip(small_grads.items(), updates):
        results[name] = (g,) + update

    order = ["w_mod", "b_mod", "norm_mix_g", "norm_ffn_g", "lru_w_y", "lru_b_y", "lru_w_in", "lru_b_in", "lru_conv_w",
             "lru_conv_b", "lru_w_a", "lru_b_a", "lru_w_x", "lru_b_x", "lru_lambda", "lru_w_out", "lru_b_out", "pool_w",
             "pool_scale", "ffn_w1", "ffn_w2", "final_norm_g"]
    return (loss, grad_x, *[results[n][0] for n in order], *[results[n][1] for n in order],
            *[results[n][2] for n in order], *[results[n][3] for n in order])
```

```python
import math

import jax
import jax.numpy as jnp
from jax import lax
from jax.experimental import pallas as pl
from jax.experimental.pallas import tpu as pltpu

f32, bf16 = jnp.float32, jnp.bfloat16

D_MODEL = 1024
LRU_WIDTH = 1024
HEADS = 4
HEAD_DIM = 256
D_FF = 4096
DEPTH = 4
N_MOD = 6
N_DEV = 8
FF_CHUNK = D_FF // N_DEV
POOL_WINDOWS = (2, 4, 8, 16)
POOL_HALO = 16
EPS = 1e-6
LRU_C = 8.0

ADAM_LR = 0.001
ADAM_B1 = 0.9
ADAM_B2 = 0.999
ADAM_EPS = 1e-08
ADAM_WD = 0.01
ADAM_STEP = 10

SUBLANES = 8
BF16_ROWS = 16

R_SH_M, R_SC_M, R_GT_M, R_SH_F, R_SC_F, R_GT_F, R_GS_M, R_GS_F = range(8)
P_BY, P_BIN, P_CONVB, P_BA, P_BX, P_LAM, P_BOUT, P_CW0 = 0, 1, 2, 3, 4, 5, 6, 8
G_SH, G_GS, G_GT, G_BY, G_BIN, G_CONVB, G_BA, G_BX, G_LS, G_BOUT, G_CW0 = 0, 1, 2, 3, 4, 5, 6, 7, 8, 9, 10
K_MOD, K_NMIX, K_NFFN, K_LRUB, K_CONVW, K_BA, K_BX, K_PS, K_FIN, K_LOSS, K_ROWS = 0, 24, 28, 32, 42, 50, 52, 54, 56, 57, 64


def _params(semantics=None, vmem_mb=56):
    return pltpu.CompilerParams(dimension_semantics=semantics, vmem_limit_bytes=vmem_mb * 1024 * 1024)


def _mod_rows(layer):
    return pl.BlockSpec((None, 8, D_MODEL), lambda *_: (layer, 0, 0))


def _behind(body, n_in, dep):
    if dep is None:
        return body, [], []

    def run(*refs):
        body(*refs[:n_in], *refs[n_in + 1:])

    return run, [pl.BlockSpec(memory_space=pl.ANY)], [dep]


def _mm(a, b):
    return jnp.dot(a, b, preferred_element_type=f32)


def _mm_nt(a, b):
    return lax.dot_general(a, b, (((1,), (1,)), ((), ())), preferred_element_type=f32)


def _mm_tn(a, b):
    return lax.dot_general(a, b, (((0,), (0,)), ((), ())), preferred_element_type=f32)


def _rms(x):
    r = lax.rsqrt(jnp.mean(x * x, axis=-1, keepdims=True) + EPS)
    return x * r, r


def _norm_bwd(dh, n, r, gs):
    dn = dh * gs
    return r * (dn - n * jnp.mean(dn * n, axis=-1, keepdims=True))


def _colsum(v):
    return jnp.sum(v, axis=0, keepdims=True)


def _sigmoid(v):
    return 0.5 * jnp.tanh(0.5 * v) + 0.5


def _log_sigmoid(v):
    return jnp.minimum(v, 0.0) - jnp.log1p(jnp.exp(-jnp.abs(v)))


_GELU_C = 0.7978845608028654
_GELU_A = 0.044715


def _gelu_and_grad(v):
    v2 = v * v
    t = jnp.tanh(_GELU_C * v * (1.0 + _GELU_A * v2))
    p = 0.5 + 0.5 * t
    return v * p, p + (0.5 * v) * (1.0 - t * t) * (_GELU_C + (3.0 * _GELU_A * _GELU_C) * v2)


def _rows_before(halo, v, shifts):
    hr = halo.shape[0]
    ext = jnp.concatenate([halo, v], axis=0)
    return [pltpu.roll(ext, k, 0)[hr:] for k in shifts]


def _rows_after(v, halo, shifts):
    n = v.shape[0]
    ext = jnp.concatenate([v, halo], axis=0)
    return [pltpu.roll(ext, ext.shape[0] - k, 0)[:n] for k in shifts]


def _shift_matrix(n, halo_rows, shifts):
    rows = lax.broadcasted_iota(jnp.int32, (n, n + halo_rows), 0)
    cols = lax.broadcasted_iota(jnp.int32, (n, n + halo_rows), 1)
    return jnp.concatenate([(cols == rows + halo_rows - k).astype(bf16) for k in shifts], axis=0)


def _shifted_rows(sel, halo, v):
    n = v.shape[0]
    out = _mm(sel, jnp.concatenate([halo, v], axis=0))
    return [out[j * n:(j + 1) * n] for j in range(sel.shape[0] // n)]


def _block_diag(v, w_ref, kind):
    return jnp.concatenate(
        [_mm(v[:, h * HEAD_DIM:(h + 1) * HEAD_DIM], w_ref[kind, h]) for h in range(HEADS)], axis=1)


def _block_diag_t(v, w_ref, kind):
    return jnp.concatenate(
        [_mm_nt(v[:, h * HEAD_DIM:(h + 1) * HEAD_DIM], w_ref[kind, h]) for h in range(HEADS)], axis=1)


def _exchange(arrays, gather, name, pieces=1, after=()):
    n = len(arrays)
    peers = N_DEV - 1
    after = list(after)

    def body(*refs):
        ins, outs = refs[:n], refs[n + len(after):2 * n + len(after)]
        send_sems, recv_sems, local_sems = refs[2 * n + len(after):]
        x, y, c = lax.axis_index("x"), lax.axis_index("y"), lax.axis_index("c")
        me = 4 * x + 2 * y + c
        local = []
        for k in range(n):
            cp = pltpu.make_async_copy(ins[k] if gather else ins[k].at[me], outs[k].at[me], local_sems.at[k])
            cp.start()
            local.append(cp)
        remote = _peer_copies(ins, outs, send_sems, recv_sems, gather, pieces)
        for cp in remote:
            cp.start()
        for cp in remote:
            cp.wait()
        for cp in local:
            cp.wait()

    out_shape = tuple(
        jax.ShapeDtypeStruct(((N_DEV,) + a.shape) if gather else a.shape, a.dtype) for a in arrays)
    outs = pl.pallas_call(
        body, name=name, out_shape=out_shape,
        in_specs=[pl.BlockSpec(memory_space=pl.ANY)] * (n + len(after)),
        out_specs=tuple(pl.BlockSpec(memory_space=pl.ANY) for _ in range(n)),
        scratch_shapes=[pltpu.SemaphoreType.DMA((n * pieces * peers,)), pltpu.SemaphoreType.DMA((n * pieces * peers,)),
                        pltpu.SemaphoreType.DMA((n,))],
        compiler_params=pltpu.CompilerParams(has_side_effects=True),
    )(*arrays, *after)
    return list(outs)


_HBM = pl.BlockSpec(memory_space=pltpu.HBM)
_SEM = pl.BlockSpec(memory_space=pltpu.SEMAPHORE)
_DATAFLOW = pltpu.SideEffectType.DATAFLOW_SIDE_EFFECTING


def _peer_copies(src_refs, land_refs, send_sems, recv_sems, gather, pieces=1):
    x, y, c = lax.axis_index("x"), lax.axis_index("y"), lax.axis_index("c")
    me = 4 * x + 2 * y + c
    peers = N_DEV - 1
    copies = []
    for p in range(1, N_DEV):
        px = 1 - x if p & 4 else x
        py = 1 - y if p & 2 else y
        pc = 1 - c if p & 1 else c
        for k in range(len(src_refs)):
            block = src_refs[k] if gather else src_refs[k].at[4 * px + 2 * py + pc]
            dst = land_refs[k].at[me]
            rows = block.shape[0] // pieces
            for r in range(pieces):
                part = pl.ds(r * rows, rows)
                sem = (k * pieces + r) * peers + p - 1
                copies.append(pltpu.make_async_remote_copy(
                    src_ref=block.at[part] if pieces > 1 else block, dst_ref=dst.at[part] if pieces > 1 else dst,
                    send_sem=send_sems.at[sem], recv_sem=recv_sems.at[sem],
                    device_id=(px, py, pc), device_id_type=pl.DeviceIdType.MESH))
    return copies


def _landing(srcs, gather, me):
    out = []
    for a in srcs:
        own = a if gather else lax.dynamic_index_in_dim(a, me, 0, keepdims=False)
        out.append(lax.dynamic_update_index_in_dim(lax.empty((N_DEV,) + own.shape, own.dtype), own, me, 0))
    return out


def _send_start(name, groups, gather, me, pieces=1, after=None):
    sizes = [len(g) for g in groups]
    srcs = [a for g in groups for a in g]
    n = len(srcs)
    lands = _landing(srcs, gather, me)
    ng = len(groups)
    first = [sum(sizes[:g]) for g in range(ng)]
    extra = [] if after is None else [after]

    def body(*refs):
        src_refs, land_refs = refs[:n], refs[n:2 * n]
        sems, token = refs[2 * n + len(extra):2 * n + len(extra) + 2 * ng], refs[-1]
        for g in range(ng):
            part = slice(first[g], first[g] + sizes[g])
            for cp in _peer_copies(src_refs[part], land_refs[part], sems[2 * g], sems[2 * g + 1], gather, pieces):
                cp.start()
        token[...] = jnp.zeros_like(token)

    sem_shapes = [pltpu.SemaphoreType.DMA((sizes[g // 2] * pieces * (N_DEV - 1),)) for g in range(2 * ng)]
    outs = pl.pallas_call(
        body, name=name,
        out_shape=(*sem_shapes, *[pltpu.HBM(a.shape, a.dtype) for a in (*srcs, *lands)], jax.ShapeDtypeStruct((8, 128), f32)),
        in_specs=[_HBM] * (2 * n) + [pl.BlockSpec(memory_space=pl.ANY)] * len(extra),
        out_specs=(*[_SEM] * (2 * ng), *[_HBM] * (2 * n), pl.BlockSpec(memory_space=pltpu.VMEM)),
        input_output_aliases={k: 2 * ng + k for k in range(2 * n)},
        compiler_params=pltpu.CompilerParams(has_side_effects=_DATAFLOW),
    )(*[pltpu.with_memory_space_constraint(a, pltpu.HBM) for a in (*srcs, *lands)], *extra)
    srcs_thru, lands_thru = outs[2 * ng:2 * ng + n], outs[2 * ng + n:2 * ng + 2 * n]
    handles = [(outs[2 * g], outs[2 * g + 1], list(srcs_thru[first[g]:first[g] + sizes[g]]),
                list(lands_thru[first[g]:first[g] + sizes[g]]), gather, pieces) for g in range(ng)]
    return handles, outs[-1]


def _send_wait(name, handle, after):
    send_sems, recv_sems, srcs, lands, gather, pieces = handle
    n = len(srcs)
    after = list(after) if isinstance(after, (list, tuple)) else [after]

    def body(*refs):
        src_refs, land_refs = refs[:n], refs[n:2 * n]
        for cp in _peer_copies(src_refs, land_refs, refs[2 * n], refs[2 * n + 1], gather, pieces):
            cp.wait_send()
            cp.wait_recv()
        refs[-1][...] = jnp.zeros_like(refs[-1])

    outs = pl.pallas_call(
        body, name=name,
        out_shape=(*[pltpu.HBM(a.shape, a.dtype) for a in (*srcs, *lands)], jax.ShapeDtypeStruct((8, 128), f32)),
        in_specs=[_HBM] * (2 * n) + [_SEM, _SEM] + [pl.BlockSpec(memory_space=pl.ANY)] * len(after),
        out_specs=(*[_HBM] * (2 * n), pl.BlockSpec(memory_space=pltpu.VMEM)),
        input_output_aliases={k: k for k in range(2 * n)},
        compiler_params=pltpu.CompilerParams(has_side_effects=_DATAFLOW),
    )(*srcs, *lands, send_sems, recv_sems, *after)
    return list(outs[n:2 * n]), outs[-1]


def _mod_part(c_all, w_mod):
    depth, d, cols = w_mod.shape

    def body(c_ref, w_ref, o_ref):
        cv = c_ref[...]
        cond = cv * _sigmoid(cv)
        o_ref[...] = jnp.dot(cond, w_ref[...], preferred_element_type=f32, precision=lax.Precision.HIGHEST)

    return pl.pallas_call(
        body, name="mod_part", grid=(depth,),
        out_shape=jax.ShapeDtypeStruct((depth, N_DEV, cols), f32),
        in_specs=[pl.BlockSpec((N_DEV, d), lambda i: (0, 0)), pl.BlockSpec((None, d, cols), lambda i: (i, 0, 0))],
        out_specs=pl.BlockSpec((None, N_DEV, cols), lambda i: (i, 0, 0)),
        compiler_params=_params(("arbitrary",), 32),
    )(c_all, w_mod)


def _mod_table(mod_row, b_mod, g_mix, g_ffn):
    def body(m_ref, b_ref, gm_ref, gf_ref, o_ref, token_ref):
        for i in range(DEPTH):
            for k in range(N_MOD):
                o_ref[i, k:k + 1, :] = m_ref[i:i + 1, k * D_MODEL:(k + 1) * D_MODEL] + b_ref[i:i + 1, k * D_MODEL:(k + 1) * D_MODEL]
            o_ref[i, R_GS_M:R_GS_M + 1, :] = gm_ref[i:i + 1, :] * (1.0 + o_ref[i, R_SC_M:R_SC_M + 1, :])
            o_ref[i, R_GS_F:R_GS_F + 1, :] = gf_ref[i:i + 1, :] * (1.0 + o_ref[i, R_SC_F:R_SC_F + 1, :])
        token_ref[...] = jnp.zeros_like(token_ref)

    return pl.pallas_call(
        body, name="mod_table",
        out_shape=(jax.ShapeDtypeStruct((DEPTH, 8, D_MODEL), f32), jax.ShapeDtypeStruct((8, 128), f32)))(
        mod_row, b_mod, g_mix, g_ffn)


def _ffn_tile(s):
    return min(512, s)


def _layer_weights(shape):
    return pl.BlockSpec((N_DEV,) + shape, lambda i: (0, 0, 0))


def _ffn_fwd(x, table, w1g, w2g, layer):
    s = x.shape[0]
    ts = _ffn_tile(s)

    def body(x_ref, vec_ref, w1_ref, w2_ref, xo_ref, u_ref, y_ref, hb_ref):
        xv = x_ref[...]
        n, _ = _rms(xv)
        hb = (n * vec_ref[R_GS_F:R_GS_F + 1, :] + vec_ref[R_SH_F:R_SH_F + 1, :]).astype(bf16)
        hb_ref[...] = hb
        yv = jnp.zeros((ts, D_MODEL), f32)
        for f in range(N_DEV):
            u = jnp.maximum(_mm(hb, w1_ref[f]), 0.0)
            u_ref[:, f * FF_CHUNK:(f + 1) * FF_CHUNK] = u.astype(bf16)
            yv = yv + _mm((u * u).astype(bf16), w2_ref[f])
        y_ref[...] = yv.astype(bf16)
        xo_ref[...] = xv + vec_ref[R_GT_F:R_GT_F + 1, :] * yv

    row = pl.BlockSpec((ts, D_MODEL), lambda i: (i, 0))
    return pl.pallas_call(
        body, name=f"ffn_fwd_{layer}", grid=(s // ts,),
        out_shape=(jax.ShapeDtypeStruct((s, D_MODEL), f32), jax.ShapeDtypeStruct((s, D_FF), bf16),
                   jax.ShapeDtypeStruct((s, D_MODEL), bf16), jax.ShapeDtypeStruct((s, D_MODEL), bf16)),
        in_specs=[row, _mod_rows(layer),
                  _layer_weights((D_MODEL, FF_CHUNK)), _layer_weights((FF_CHUNK, D_MODEL))],
        out_specs=(row, pl.BlockSpec((ts, D_FF), lambda i: (i, 0)), row, row),
        compiler_params=_params(("arbitrary",), 56),
    )(x, table, w1g, w2g)


def _ffn_bwd_act(x, dx, u, y, table, w1g, w2g, layer, dep=None):
    s = x.shape[0]
    ts = _ffn_tile(s)

    def body(x_ref, dx_ref, u_ref, y_ref, vec_ref, w1_ref, w2_ref, dxo_ref, da_ref, dyb_ref, sm_ref):
        @pl.when(pl.program_id(0) == 0)
        def _():
            sm_ref[...] = jnp.zeros_like(sm_ref)

        dxv = dx_ref[...]
        dyb = (dxv * vec_ref[R_GT_F:R_GT_F + 1, :]).astype(bf16)
        dyb_ref[...] = dyb
        sm_ref[G_GT:G_GT + 1, :] += _colsum(dxv * y_ref[...].astype(f32))
        dh = jnp.zeros((ts, D_MODEL), f32)
        for f in range(N_DEV):
            cols = slice(f * FF_CHUNK, (f + 1) * FF_CHUNK)
            dz = _mm_nt(dyb, w2_ref[f])
            dab = (dz * (2.0 * u_ref[:, cols].astype(f32))).astype(bf16)
            da_ref[:, cols] = dab
            dh = dh + _mm_nt(dab, w1_ref[f])
        n, r = _rms(x_ref[...])
        sm_ref[G_SH:G_SH + 1, :] += _colsum(dh)
        sm_ref[G_GS:G_GS + 1, :] += _colsum(dh * n)
        dxo_ref[...] = dxv + _norm_bwd(dh, n, r, vec_ref[R_GS_F:R_GS_F + 1, :])

    row = pl.BlockSpec((ts, D_MODEL), lambda i: (i, 0))
    wide = pl.BlockSpec((ts, D_FF), lambda i: (i, 0))
    body, dep_specs, dep_args = _behind(body, 7, dep)
    return pl.pallas_call(
        body, name=f"ffn_bwd_act_{layer}", grid=(s // ts,),
        out_shape=(jax.ShapeDtypeStruct((s, D_MODEL), f32), jax.ShapeDtypeStruct((s, D_FF), bf16),
                   jax.ShapeDtypeStruct((s, D_MODEL), bf16), jax.ShapeDtypeStruct((8, D_MODEL), f32)),
        in_specs=[row, row, wide, row, _mod_rows(layer),
                  _layer_weights((D_MODEL, FF_CHUNK)), _layer_weights((FF_CHUNK, D_MODEL))] + dep_specs,
        out_specs=(row, wide, row, pl.BlockSpec((8, D_MODEL), lambda i: (0, 0))),
        compiler_params=_params(("arbitrary",), 58),
    )(x, dx, u, y, table, w1g, w2g, *dep_args)


def _ffn_bwd_w1(hb, da, layer):
    s = hb.shape[0]
    ts = _ffn_tile(s)
    nt = s // ts

    def body(hb_ref, da_ref, dw_ref, acc_ref):
        i = pl.program_id(0)

        @pl.when(i == 0)
        def _():
            acc_ref[...] = jnp.zeros_like(acc_ref)

        hb = hb_ref[...]
        for f in range(N_DEV):
            acc_ref[f] += _mm_tn(hb, da_ref[:, f * FF_CHUNK:(f + 1) * FF_CHUNK])

        @pl.when(i == nt - 1)
        def _():
            dw_ref[...] = acc_ref[...].astype(bf16)

    return pl.pallas_call(
        body, name=f"ffn_bwd_w1_{layer}", grid=(nt,),
        out_shape=jax.ShapeDtypeStruct((N_DEV, D_MODEL, FF_CHUNK), bf16),
        in_specs=[pl.BlockSpec((ts, D_MODEL), lambda i: (i, 0)), pl.BlockSpec((ts, D_FF), lambda i: (i, 0))],
        out_specs=pl.BlockSpec((N_DEV, D_MODEL, FF_CHUNK), lambda i: (0, 0, 0)),
        scratch_shapes=[pltpu.VMEM((N_DEV, D_MODEL, FF_CHUNK), f32)],
        compiler_params=_params(("arbitrary",), 56),
    )(hb, da)


def _ffn_bwd_w2(u, dyb, layer):
    s = u.shape[0]
    ts = _ffn_tile(s)
    nt = s // ts

    def body(u_ref, dyb_ref, dw_ref, acc_ref):
        i = pl.program_id(0)

        @pl.when(i == 0)
        def _():
            acc_ref[...] = jnp.zeros_like(acc_ref)

        dyb = dyb_ref[...]
        for f in range(N_DEV):
            uv = u_ref[:, f * FF_CHUNK:(f + 1) * FF_CHUNK].astype(f32)
            acc_ref[f] += _mm_tn((uv * uv).astype(bf16), dyb)

        @pl.when(i == nt - 1)
        def _():
            dw_ref[...] = acc_ref[...].astype(bf16)

    return pl.pallas_call(
        body, name=f"ffn_bwd_w2_{layer}", grid=(nt,),
        out_shape=jax.ShapeDtypeStruct((N_DEV, FF_CHUNK, D_MODEL), bf16),
        in_specs=[pl.BlockSpec((ts, D_FF), lambda i: (i, 0)), pl.BlockSpec((ts, D_MODEL), lambda i: (i, 0))],
        out_specs=pl.BlockSpec((N_DEV, FF_CHUNK, D_MODEL), lambda i: (0, 0, 0)),
        scratch_shapes=[pltpu.VMEM((N_DEV, FF_CHUNK, D_MODEL), f32)],
        compiler_params=_params(("arbitrary",), 56),
    )(u, dyb)


def _lru_gates(xc, wsm_ref, pv_ref):
    xcb = xc.astype(bf16)
    gr = _sigmoid(_block_diag(xcb, wsm_ref, 0) + pv_ref[P_BA:P_BA + 1, :])
    gi = _sigmoid(_block_diag(xcb, wsm_ref, 1) + pv_ref[P_BX:P_BX + 1, :])
    log_a = (LRU_C * _log_sigmoid(pv_ref[P_LAM:P_LAM + 1, :])) * gr
    t = jnp.tanh(log_a)
    return gr, gi, jnp.exp(log_a), jnp.sqrt((-2.0 * t) / (1.0 - t))


def _conv(xr, taps_before, pv_ref):
    xc = xr * pv_ref[P_CW0 + 3:P_CW0 + 4, :] + pv_ref[P_CONVB:P_CONVB + 1, :]
    for k, v in zip((2, 1, 0), taps_before):
        xc = xc + v * pv_ref[P_CW0 + k:P_CW0 + k + 1, :]
    return xc


LRU_FWD_SUB, LRU_FWD_SUBS = 128, 4
LRU_BWD_SUB, LRU_BWD_SUBS = 256, 1


def _scan_rows(a, u, carry, reverse):
    groups = a.shape[0] // SUBLANES
    row = lax.broadcasted_iota(jnp.int32, (SUBLANES, a.shape[1]), 0)
    outs = [None] * groups
    for j in range(groups):
        g = groups - 1 - j if reverse else j
        av, uv = a[g * SUBLANES:(g + 1) * SUBLANES], u[g * SUBLANES:(g + 1) * SUBLANES]
        for k in (1, 2, 4):
            if reverse:
                valid, shift = row < SUBLANES - k, SUBLANES - k
            else:
                valid, shift = row >= k, k
            a_s = jnp.where(valid, pltpu.roll(av, shift, 0), 1.0)
            u_s = jnp.where(valid, pltpu.roll(uv, shift, 0), 0.0)
            uv = uv + av * u_s
            av = av * a_s
        h = uv + av * carry
        outs[g] = h
        carry = h[0:1, :] if reverse else h[SUBLANES - 1:SUBLANES, :]
    return jnp.concatenate(outs, axis=0), carry


def _lru_fwd(x, table, wbig, wsm, pvec, layer, dep=None):
    s = x.shape[0]
    sub = min(LRU_FWD_SUB, s)
    ts = min(sub * LRU_FWD_SUBS, s)
    nsub = ts // sub
    w = LRU_WIDTH

    def body(x_ref, vec_ref, wb_ref, wsm_ref, pv_ref, xo_ref, xr_ref, hs_ref, a_ref, mult_ref, gr_ref, gi_ref,
             gel_ref, geld_ref, y_ref, tail_ref, carry_ref):
        @pl.when(pl.program_id(0) == 0)
        def _():
            tail_ref[...] = jnp.zeros_like(tail_ref)
            carry_ref[...] = jnp.zeros_like(carry_ref)

        sel = _shift_matrix(sub, BF16_ROWS, (1, 2, 3))
        for k in range(nsub):
            rows = slice(k * sub, (k + 1) * sub)
            xv = x_ref[rows, :]
            n, _ = _rms(xv)
            hb = (n * vec_ref[R_GS_M:R_GS_M + 1, :] + vec_ref[R_SH_M:R_SH_M + 1, :]).astype(bf16)
            gelu_v, gelu_d = _gelu_and_grad(_mm(hb, wb_ref[0]) + pv_ref[P_BY:P_BY + 1, :])
            gel_ref[rows, :] = gelu_v.astype(bf16)
            geld_ref[rows, :] = gelu_d.astype(bf16)
            xrb = (_mm(hb, wb_ref[1]) + pv_ref[P_BIN:P_BIN + 1, :]).astype(bf16)
            xr_ref[rows, :] = xrb
            xc = _conv(xrb.astype(f32), _shifted_rows(sel, tail_ref[...], xrb), pv_ref)
            tail_ref[...] = xrb[sub - BF16_ROWS:, :]
            gr, gi, a, mult = _lru_gates(xc, wsm_ref, pv_ref)
            gr_ref[rows, :] = gr.astype(bf16)
            gi_ref[rows, :] = gi.astype(bf16)
            a_ref[rows, :] = a
            mult_ref[rows, :] = mult
            hs, carry = _scan_rows(a, mult * (gi * xc), carry_ref[0:1, :], reverse=False)
            carry_ref[0:1, :] = carry
            hs_ref[rows, :] = hs
            yv = _mm((hs * gelu_v).astype(bf16), wb_ref[2]) + pv_ref[P_BOUT:P_BOUT + 1, :]
            y_ref[rows, :] = yv.astype(bf16)
            xo_ref[rows, :] = xv + vec_ref[R_GT_M:R_GT_M + 1, :] * yv

    row = pl.BlockSpec((ts, D_MODEL), lambda i: (i, 0))
    roww = pl.BlockSpec((ts, w), lambda i: (i, 0))
    wide = lambda dt: jax.ShapeDtypeStruct((s, w), dt)
    body, dep_specs, dep_args = _behind(body, 5, dep)
    return pl.pallas_call(
        body, name=f"lru_fwd_{layer}", grid=(s // ts,),
        out_shape=(jax.ShapeDtypeStruct((s, D_MODEL), f32), wide(bf16), wide(f32), wide(f32), wide(f32),
                   wide(bf16), wide(bf16), wide(bf16), wide(bf16), jax.ShapeDtypeStruct((s, D_MODEL), bf16)),
        in_specs=[row, _mod_rows(layer),
                  pl.BlockSpec((3, w, w), lambda i: (0, 0, 0)),
                  pl.BlockSpec((2, HEADS, HEAD_DIM, HEAD_DIM), lambda i: (0, 0, 0, 0)),
                  pl.BlockSpec((16, w), lambda i: (0, 0))] + dep_specs,
        out_specs=(row, roww, roww, roww, roww, roww, roww, roww, roww, row),
        scratch_shapes=[pltpu.VMEM((BF16_ROWS, w), bf16), pltpu.VMEM((SUBLANES, w), f32)],
        compiler_params=_params(("arbitrary",)),
    )(x, table, wbig, wsm, pvec, *dep_args)


def _lru_bwd(x, dx, saved, table, wbig, wsm, pvec, layer, dep=None):
    xr, hs, a_all, mult_all, gr_all, gi_all, gel_all, geld_all, y = saved
    s = x.shape[0]
    sub = min(LRU_BWD_SUB, s)
    ts = min(sub * LRU_BWD_SUBS, s)
    nsub = ts // sub
    nt = s // ts
    w = LRU_WIDTH
    shard = w // N_DEV
    hshard = HEAD_DIM // N_DEV

    def body(x_ref, dx_ref, xr_ref, xrh_ref, hs_ref, hsh_ref, a_ref, mult_ref, gr_ref, gi_ref, gel_ref, geld_ref,
             y_ref, vec_ref, wb_ref, wsm_ref, pv_ref,
             dxo_ref, dwb_ref, dwsm_ref, sm_ref, accb_ref, accs_ref, eps_ref, dxc8_ref,
             hb_scr, dgb_scr, dxrb_scr, mb_scr, dyb_scr, xcb_scr, drab_scr, drxb_scr):
        i = pl.program_id(0)
        first_tile = i == nt - 1

        @pl.when(i == 0)
        def _():
            accb_ref[...] = jnp.zeros_like(accb_ref)
            accs_ref[...] = jnp.zeros_like(accs_ref)
            sm_ref[...] = jnp.zeros_like(sm_ref)
            eps_ref[...] = jnp.zeros_like(eps_ref)
            dxc8_ref[...] = jnp.zeros_like(dxc8_ref)

        gs = vec_ref[R_GS_M:R_GS_M + 1, :]
        c_ls = LRU_C * _log_sigmoid(pv_ref[P_LAM:P_LAM + 1, :])
        for k in reversed(range(nsub)):
            rows = slice(k * sub, (k + 1) * sub)
            xv = x_ref[rows, :]
            dxv = dx_ref[rows, :]
            n, r = _rms(xv)
            hb_scr[rows, :] = (n * gs + vec_ref[R_SH_M:R_SH_M + 1, :]).astype(bf16)
            xrv = xr_ref[rows, :].astype(f32)
            hsv = hs_ref[rows, :]
            if k == 0:
                xr_halo = jnp.where(first_tile, 0.0, xrh_ref[...].astype(f32))
                hs_halo = jnp.where(first_tile, 0.0, hsh_ref[...])
            else:
                xr_halo = xr_ref[k * sub - BF16_ROWS:k * sub, :].astype(f32)
                hs_halo = hs_ref[k * sub - SUBLANES:k * sub, :]
            xs1, xs2, xs3 = _rows_before(xr_halo, xrv, (1, 2, 3))
            xc = _conv(xrv, (xs1, xs2, xs3), pv_ref)
            xcb_scr[rows, :] = xc.astype(bf16)
            a, mult = a_ref[rows, :], mult_ref[rows, :]
            gr, gi = gr_ref[rows, :].astype(f32), gi_ref[rows, :].astype(f32)
            gelu_v = gel_ref[rows, :].astype(f32)

            dy = dxv * vec_ref[R_GT_M:R_GT_M + 1, :]
            dyb = dy.astype(bf16)
            dyb_scr[rows, :] = dyb
            sm_ref[G_GT:G_GT + 1, :] += _colsum(dxv * y_ref[rows, :].astype(f32))
            sm_ref[G_BOUT:G_BOUT + 1, :] += _colsum(dy)
            mb_scr[rows, :] = (hsv * gelu_v).astype(bf16)
            dm = _mm_nt(dyb, wb_ref[2])
            dhs = dm * gelu_v
            dgpre = dm * hsv * geld_ref[rows, :].astype(f32)
            dgb = dgpre.astype(bf16)
            dgb_scr[rows, :] = dgb
            sm_ref[G_BY:G_BY + 1, :] += _colsum(dgpre)

            eps_in = eps_ref[0:1, :]
            eps, eps_out = _scan_rows(a, a * dhs, eps_in, reverse=True)
            eps_ref[0:1, :] = eps_out
            (eps_next,) = _rows_after(eps, jnp.broadcast_to(eps_in, (SUBLANES, w)), (1,))
            delta = dhs + eps_next
            (h_prev,) = _rows_before(hs_halo, hsv, (1,))
            dxi = delta * xc
            dgi = dxi * mult
            dla = (delta * h_prev) * a - (dxi * gi) * (a * a) / mult
            sm_ref[G_LS:G_LS + 1, :] += _colsum(dla * gr)
            dra = (dla * c_ls) * (gr - gr * gr)
            drx = dgi * (gi - gi * gi)
            drab, drxb = dra.astype(bf16), drx.astype(bf16)
            drab_scr[rows, :] = drab
            drxb_scr[rows, :] = drxb
            sm_ref[G_BA:G_BA + 1, :] += _colsum(dra)
            sm_ref[G_BX:G_BX + 1, :] += _colsum(drx)
            dxc = (delta * mult) * gi + _block_diag_t(drab, wsm_ref, 0) + _block_diag_t(drxb, wsm_ref, 1)

            sm_ref[G_CONVB:G_CONVB + 1, :] += _colsum(dxc)
            for kk, v in zip((3, 2, 1, 0), (xrv, xs1, xs2, xs3)):
                sm_ref[G_CW0 + kk:G_CW0 + kk + 1, :] += _colsum(dxc * v)
            ups = _rows_after(dxc, dxc8_ref[...], (1, 2, 3))
            dxc8_ref[...] = dxc[0:SUBLANES, :]
            dxr = dxc * pv_ref[P_CW0 + 3:P_CW0 + 4, :]
            for kk, v in zip((2, 1, 0), ups):
                dxr = dxr + v * pv_ref[P_CW0 + kk:P_CW0 + kk + 1, :]
            dxrb = dxr.astype(bf16)
            dxrb_scr[rows, :] = dxrb
            sm_ref[G_BIN:G_BIN + 1, :] += _colsum(dxr)
            dh = _mm_nt(dgb, wb_ref[0]) + _mm_nt(dxrb, wb_ref[1])
            sm_ref[G_SH:G_SH + 1, :] += _colsum(dh)
            sm_ref[G_GS:G_GS + 1, :] += _colsum(dh * n)
            dxo_ref[rows, :] = dxv + _norm_bwd(dh, n, r, gs)

        hb = hb_scr[...]
        accb_ref[0] += _mm_tn(hb, dgb_scr[...])
        accb_ref[1] += _mm_tn(hb, dxrb_scr[...])
        accb_ref[2] += _mm_tn(mb_scr[...], dyb_scr[...])
        for h in range(HEADS):
            cols = slice(h * HEAD_DIM, (h + 1) * HEAD_DIM)
            accs_ref[0, h] += _mm_tn(xcb_scr[:, cols], drab_scr[:, cols])
            accs_ref[1, h] += _mm_tn(xcb_scr[:, cols], drxb_scr[:, cols])

        @pl.when(i == nt - 1)
        def _():
            sm_ref[G_LS:G_LS + 1, :] = sm_ref[G_LS:G_LS + 1, :] * LRU_C
            for k in range(3):
                dwb_ref[:, k] = accb_ref[k].astype(bf16).reshape(N_DEV, shard, w)
            for k in range(2):
                for h in range(HEADS):
                    dwsm_ref[:, k, h] = accs_ref[k, h].astype(bf16).reshape(N_DEV, hshard, HEAD_DIM)

    rev = lambda i: (nt - 1 - i, 0)
    row = pl.BlockSpec((ts, D_MODEL), rev)
    roww = pl.BlockSpec((ts, w), rev)
    halo16 = pl.BlockSpec((BF16_ROWS, w), lambda i: (jnp.maximum((nt - 1 - i) * (ts // BF16_ROWS) - 1, 0), 0))
    halo8 = pl.BlockSpec((SUBLANES, w), lambda i: (jnp.maximum((nt - 1 - i) * (ts // SUBLANES) - 1, 0), 0))
    const = lambda *shape: pl.BlockSpec(shape, lambda i: (0,) * len(shape))
    operand = pltpu.VMEM((ts, w), bf16)
    body, dep_specs, dep_args = _behind(body, 17, dep)
    return pl.pallas_call(
        body, name=f"lru_bwd_{layer}", grid=(nt,),
        out_shape=(jax.ShapeDtypeStruct((s, D_MODEL), f32),
                   jax.ShapeDtypeStruct((N_DEV, 3, shard, w), bf16),
                   jax.ShapeDtypeStruct((N_DEV, 2, HEADS, hshard, HEAD_DIM), bf16),
                   jax.ShapeDtypeStruct((16, w), f32)),
        in_specs=[row, row, roww, halo16, roww, halo8, roww, roww, roww, roww, roww, roww, row, _mod_rows(layer),
                  const(3, w, w), const(2, HEADS, HEAD_DIM, HEAD_DIM), const(16, w)] + dep_specs,
        out_specs=(row, const(N_DEV, 3, shard, w), const(N_DEV, 2, HEADS, hshard, HEAD_DIM), const(16, w)),
        scratch_shapes=[pltpu.VMEM((3, w, w), f32), pltpu.VMEM((2, HEADS, HEAD_DIM, HEAD_DIM), f32),
                        pltpu.VMEM((SUBLANES, w), f32), pltpu.VMEM((SUBLANES, w), f32)] + [operand] * 8,
        compiler_params=_params(("arbitrary",), 58),
    )(x, dx, xr, xr, hs, hs, a_all, mult_all, gr_all, gi_all, gel_all, geld_all, y, table, wbig, wsm, pvec, *dep_args)


def _pool_tile(s):
    return min(1024, s)


def _pool_counts(tile_index, ts):
    t = (tile_index * ts + lax.broadcasted_iota(jnp.int32, (ts, 1), 0) + 1).astype(f32)
    return [1.0 / jnp.minimum(t, float(win)) for win in POOL_WINDOWS]


def _pooled(h, halo, inv):
    ext = jnp.concatenate([halo, h], axis=0)
    out = []
    for g in range(len(POOL_WINDOWS)):
        acc = ext[:, g * HEAD_DIM:(g + 1) * HEAD_DIM]
        for step in range(g + 1):
            acc = acc + pltpu.roll(acc, 1 << step, 0)
        out.append(acc[POOL_HALO:] * inv[g] - h[:, g * HEAD_DIM:(g + 1) * HEAD_DIM])
    return out


def _pool_fwd(x, table, pw, ps, layer):
    s = x.shape[0]
    ts = _pool_tile(s)

    def body(x_ref, vec_ref, pw_ref, ps_ref, xo_ref, y_ref, halo_ref):
        i = pl.program_id(0)

        @pl.when(i == 0)
        def _():
            halo_ref[...] = jnp.zeros_like(halo_ref)

        xv = x_ref[...]
        n, _ = _rms(xv)
        h = n * vec_ref[R_GS_M:R_GS_M + 1, :] + vec_ref[R_SH_M:R_SH_M + 1, :]
        pooled = _pooled(h, halo_ref[...], _pool_counts(i, ts))
        halo_ref[...] = h[ts - POOL_HALO:, :]
        mixed = jnp.concatenate([_mm(pooled[g].astype(bf16), pw_ref[g]) for g in range(HEADS)], axis=1)
        yv = mixed * ps_ref[0:1, :]
        y_ref[...] = yv.astype(bf16)
        xo_ref[...] = xv + vec_ref[R_GT_M:R_GT_M + 1, :] * yv

    row = pl.BlockSpec((ts, D_MODEL), lambda i: (i, 0))
    return pl.pallas_call(
        body, name=f"pool_fwd_{layer}", grid=(s // ts,),
        out_shape=(jax.ShapeDtypeStruct((s, D_MODEL), f32), jax.ShapeDtypeStruct((s, D_MODEL), bf16)),
        in_specs=[row, _mod_rows(layer),
                  pl.BlockSpec((HEADS, HEAD_DIM, HEAD_DIM), lambda i: (0, 0, 0)),
                  pl.BlockSpec((8, D_MODEL), lambda i: (0, 0))],
        out_specs=(row, row),
        scratch_shapes=[pltpu.VMEM((POOL_HALO, D_MODEL), f32)],
        compiler_params=_params(("arbitrary",)),
    )(x, table, pw, ps)


def _pool_bwd(x, dx, y, table, pw, ps, layer, dep=None):
    s = x.shape[0]
    ts = _pool_tile(s)
    nt = s // ts
    hshard = HEAD_DIM // N_DEV

    def body(x_ref, xh_ref, dx_ref, y_ref, vec_ref, pw_ref, ps_ref, dxo_ref, dpw_ref, sm_ref, acc_ref, q16_ref):
        i = pl.program_id(0)
        tile = nt - 1 - i

        @pl.when(i == 0)
        def _():
            acc_ref[...] = jnp.zeros_like(acc_ref)
            sm_ref[...] = jnp.zeros_like(sm_ref)
            q16_ref[...] = jnp.zeros_like(q16_ref)

        gs, sh = vec_ref[R_GS_M:R_GS_M + 1, :], vec_ref[R_SH_M:R_SH_M + 1, :]
        xv = x_ref[...]
        dxv = dx_ref[...]
        n, r = _rms(xv)
        h = n * gs + sh
        nh, _ = _rms(xh_ref[...])
        halo = jnp.where(tile == 0, 0.0, nh * gs + sh)
        inv = _pool_counts(tile, ts)
        pooled = _pooled(h, halo, inv)
        mixed = jnp.concatenate([_mm(pooled[g].astype(bf16), pw_ref[g]) for g in range(HEADS)], axis=1)

        dy = dxv * vec_ref[R_GT_M:R_GT_M + 1, :]
        sm_ref[G_GT:G_GT + 1, :] += _colsum(dxv * y_ref[...].astype(f32))
        sm_ref[3:4, :] += _colsum(dy * mixed)
        dmixed = (dy * ps_ref[0:1, :]).astype(bf16)
        dh_parts = []
        for g in range(HEADS):
            cols = slice(g * HEAD_DIM, (g + 1) * HEAD_DIM)
            acc_ref[g] += _mm_tn(pooled[g].astype(bf16), dmixed[:, cols])
            dpooled = _mm_nt(dmixed[:, cols], pw_ref[g])
            q = dpooled * inv[g]
            ext = jnp.concatenate([q, q16_ref[:, cols]], axis=0)
            q16_ref[:, cols] = q[0:POOL_HALO, :]
            for step in range(g + 1):
                ext = ext + pltpu.roll(ext, ext.shape[0] - (1 << step), 0)
            dh_parts.append(ext[:ts] - dpooled)
        dh = jnp.concatenate(dh_parts, axis=1)
        sm_ref[G_SH:G_SH + 1, :] += _colsum(dh)
        sm_ref[G_GS:G_GS + 1, :] += _colsum(dh * n)
        dxo_ref[...] = dxv + _norm_bwd(dh, n, r, gs)

        @pl.when(i == nt - 1)
        def _():
            for g in range(HEADS):
                dpw_ref[:, g] = acc_ref[g].astype(bf16).reshape(N_DEV, hshard, HEAD_DIM)

    rev = lambda i: (nt - 1 - i, 0)
    row = pl.BlockSpec((ts, D_MODEL), rev)
    halo16 = pl.BlockSpec((POOL_HALO, D_MODEL), lambda i: (jnp.maximum((nt - 1 - i) * (ts // POOL_HALO) - 1, 0), 0))
    const = lambda *shape: pl.BlockSpec(shape, lambda i: (0,) * len(shape))
    body, dep_specs, dep_args = _behind(body, 7, dep)
    return pl.pallas_call(
        body, name=f"pool_bwd_{layer}", grid=(nt,),
        out_shape=(jax.ShapeDtypeStruct((s, D_MODEL), f32),
                   jax.ShapeDtypeStruct((N_DEV, HEADS, hshard, HEAD_DIM), bf16),
                   jax.ShapeDtypeStruct((8, D_MODEL), f32)),
        in_specs=[row, halo16, row, row, _mod_rows(layer), const(HEADS, HEAD_DIM, HEAD_DIM),
                  const(8, D_MODEL)] + dep_specs,
        out_specs=(row, const(N_DEV, HEADS, hshard, HEAD_DIM), const(8, D_MODEL)),
        scratch_shapes=[pltpu.VMEM((HEADS, HEAD_DIM, HEAD_DIM), f32), pltpu.VMEM((POOL_HALO, D_MODEL), f32)],
        compiler_params=_params(("arbitrary",)),
    )(x, x, dx, y, table, pw, ps, *dep_args)


def _final(x, target, g_fin):
    s = x.shape[0]
    ts = min(1024, s)

    def body(x_ref, t_ref, g_ref, dx_ref, sm_ref):
        @pl.when(pl.program_id(0) == 0)
        def _():
            sm_ref[...] = jnp.zeros_like(sm_ref)

        g = g_ref[0:1, :]
        n, r = _rms(x_ref[...])
        err = n * g - t_ref[...]
        sm_ref[1:2, :] += 0.5 * jnp.sum(jnp.mean(err * err, axis=-1, keepdims=True), axis=0, keepdims=True)
        dyv = err * (1.0 / D_MODEL)
        sm_ref[0:1, :] += _colsum(dyv * n)
        dx_ref[...] = _norm_bwd(dyv, n, r, g)

    row = pl.BlockSpec((ts, D_MODEL), lambda i: (i, 0))
    return pl.pallas_call(
        body, name="final_loss", grid=(s // ts,),
        out_shape=(jax.ShapeDtypeStruct((s, D_MODEL), f32), jax.ShapeDtypeStruct((8, D_MODEL), f32)),
        in_specs=[row, row, pl.BlockSpec((8, D_MODEL), lambda i: (0, 0))],
        out_specs=(row, pl.BlockSpec((8, D_MODEL), lambda i: (0, 0))),
        compiler_params=_params(("arbitrary",)),
    )(x, target, g_fin)


def _small_pack(sm_ffn, sm_mix, sm_fin, table, g_mix, g_ffn, lam):
    def body(*refs):
        ffn, mix = refs[0:DEPTH], refs[DEPTH:2 * DEPTH]
        fin_ref, tab_ref, gm_ref, gf_ref, lam_ref, o_ref = refs[2 * DEPTH:]
        o_ref[...] = jnp.zeros_like(o_ref)
        for i in range(DEPTH):
            base = K_MOD + i * N_MOD
            o_ref[base + 0:base + 1, :] = mix[i][G_SH:G_SH + 1, :]
            o_ref[base + 1:base + 2, :] = mix[i][G_GS:G_GS + 1, :] * gm_ref[i:i + 1, :]
            o_ref[base + 2:base + 3, :] = mix[i][G_GT:G_GT + 1, :]
            o_ref[base + 3:base + 4, :] = ffn[i][G_SH:G_SH + 1, :]
            o_ref[base + 4:base + 5, :] = ffn[i][G_GS:G_GS + 1, :] * gf_ref[i:i + 1, :]
            o_ref[base + 5:base + 6, :] = ffn[i][G_GT:G_GT + 1, :]
            o_ref[K_NMIX + i:K_NMIX + i + 1, :] = mix[i][G_GS:G_GS + 1, :] * (1.0 + tab_ref[i, R_SC_M:R_SC_M + 1, :])
            o_ref[K_NFFN + i:K_NFFN + i + 1, :] = ffn[i][G_GS:G_GS + 1, :] * (1.0 + tab_ref[i, R_SC_F:R_SC_F + 1, :])
            j = i // 2
            if i % 2 == 0:
                for k, src in enumerate((G_BY, G_BIN, G_CONVB, None, G_BOUT)):
                    dst = K_LRUB + j * 5 + k
                    if src is None:
                        o_ref[dst:dst + 1, :] = mix[i][G_LS:G_LS + 1, :] * _sigmoid(-lam_ref[j:j + 1, :])
                    else:
                        o_ref[dst:dst + 1, :] = mix[i][src:src + 1, :]
                o_ref[K_CONVW + j * 4:K_CONVW + j * 4 + 4, :] = mix[i][G_CW0:G_CW0 + 4, :]
                o_ref[K_BA + j:K_BA + j + 1, :] = mix[i][G_BA:G_BA + 1, :]
                o_ref[K_BX + j:K_BX + j + 1, :] = mix[i][G_BX:G_BX + 1, :]
            else:
                o_ref[K_PS + j:K_PS + j + 1, :] = mix[i][3:4, :]
        o_ref[K_FIN:K_FIN + 2, :] = fin_ref[0:2, :]

    return pl.pallas_call(body, name="small_pack", out_shape=jax.ShapeDtypeStruct((K_ROWS, D_MODEL), f32))(
        *sm_ffn, *sm_mix, sm_fin, table, g_mix, g_ffn, lam)


def _small_sum(gathered):
    def body(g_ref, o_ref, token_ref):
        tot = g_ref[0]
        for src in range(1, N_DEV):
            tot = tot + g_ref[src]
        o_ref[...] = tot
        token_ref[...] = jnp.zeros_like(token_ref)

    return pl.pallas_call(
        body, name="small_sum",
        out_shape=(jax.ShapeDtypeStruct(gathered.shape[1:], f32), jax.ShapeDtypeStruct((8, 128), f32)))(gathered)


def _adamw_math(g, w, m, v):
    m = ADAM_B1 * m + (1.0 - ADAM_B1) * g
    v = ADAM_B2 * v + (1.0 - ADAM_B2) * (g * g)
    m_hat = m / (1.0 - ADAM_B1 ** ADAM_STEP)
    v_hat = v / (1.0 - ADAM_B2 ** ADAM_STEP)
    delta = -ADAM_LR * (m_hat / (jnp.sqrt(v_hat) + ADAM_EPS) + ADAM_WD * w)
    return delta, m, v


def _adamw_small(params):
    n = len(params)
    shapes = [w.shape for _, w, _, _ in params]
    two_d = [(1, s[0]) if len(s) == 1 else (math.prod(s[:-1]), s[-1]) for s in shapes]

    def body(*refs):
        ins, outs = refs[:4 * n], refs[4 * n:]
        for k in range(n):
            g_ref, w_ref, m_ref, v_ref = ins[4 * k:4 * k + 4]
            outs[3 * k][...], outs[3 * k + 1][...], outs[3 * k + 2][...] = _adamw_math(
                g_ref[...], w_ref[...], m_ref[...], v_ref[...])

    outs = pl.pallas_call(
        body, name="adamw_small",
        out_shape=tuple(jax.ShapeDtypeStruct(two_d[k], f32) for k in range(n) for _ in range(3)))(
        *(t.reshape(two_d[k]) for k, p in enumerate(params) for t in p))
    return [tuple(outs[3 * k + i].reshape(shapes[k]) for i in range(3)) for k in range(n)]


def _block_rows(rows, cols):
    tr = max(SUBLANES, min(rows, (512 * 1024) // (4 * cols)))
    while rows % tr:
        tr //= 2
    return tr


def _adamw_reduce(name, landings, kind, w, m, v):
    nl = len(landings)
    rows, cols = landings[0].shape[2:]
    tr = _block_rows(rows, cols)
    per_layer = rows // tr

    def body(*refs):
        l_refs = refs[:nl]
        w_ref, m_ref, v_ref, g_ref, d_ref, mo_ref, vo_ref = refs[nl:]
        layer = pl.program_id(0)
        for k in range(nl):
            @pl.when(layer == k)
            def _(k=k):
                g = l_refs[k][0].astype(f32)
                for src in range(1, N_DEV):
                    g = g + l_refs[k][src].astype(f32)
                g_ref[...] = g
        d_ref[...], mo_ref[...], vo_ref[...] = _adamw_math(g_ref[...], w_ref[...], m_ref[...], v_ref[...])

    blk = pl.BlockSpec((tr, cols), lambda l, r: (l * per_layer + r, 0))
    land = [pl.BlockSpec((N_DEV, None, tr, cols), lambda l, r, k=k: (0, kind, jnp.where(l == k, r, 0), 0)) for k in range(nl)]
    return pl.pallas_call(
        body, name=f"adamw_{name}", grid=(nl, per_layer),
        out_shape=tuple(jax.ShapeDtypeStruct((nl * rows, cols), f32) for _ in range(4)),
        in_specs=land + [blk, blk, blk],
        out_specs=(blk, blk, blk, blk),
        compiler_params=_params(("arbitrary", "arbitrary"), 32),
    )(*landings, w, m, v)


def _adamw_w_mod(c_all, dmod_all, w, m, v):
    depth, d, cols = w.shape
    tr = 256

    def body(c_ref, dm_ref, w_ref, m_ref, v_ref, g_ref, d_ref, mo_ref, vo_ref):
        cv = c_ref[...]
        cond = cv * _sigmoid(cv)
        g = lax.dot_general(cond, dm_ref[...], (((0,), (0,)), ((), ())), preferred_element_type=f32,
                            precision=lax.Precision.HIGHEST)
        g_ref[...] = g
        d_ref[...], mo_ref[...], vo_ref[...] = _adamw_math(g, w_ref[...], m_ref[...], v_ref[...])

    blk = pl.BlockSpec((None, tr, cols), lambda i, r: (i, r, 0))
    return pl.pallas_call(
        body, name="adamw_w_mod", grid=(depth, d // tr),
        out_shape=tuple(jax.ShapeDtypeStruct(w.shape, f32) for _ in range(4)),
        in_specs=[pl.BlockSpec((N_DEV, tr), lambda i, r: (0, r)),
                  pl.BlockSpec((None, N_DEV, cols), lambda i, r: (i, 0, 0)), blk, blk, blk],
        out_specs=(blk, blk, blk, blk),
        compiler_params=_params(("arbitrary", "arbitrary"), 32),
    )(c_all, dmod_all, w, m, v)


def kernel(x, c, w_mod, b_mod, norm_mix_g, norm_ffn_g, lru_w_y, lru_b_y, lru_w_in, lru_b_in, lru_conv_w, lru_conv_b, lru_w_a, lru_b_a, lru_w_x, lru_b_x, lru_lambda, lru_w_out, lru_b_out, pool_w, pool_scale, ffn_w1, ffn_w2, final_norm_g, loss_target, m_w_mod, m_b_mod, m_norm_mix_g, m_norm_ffn_g, m_lru_w_y, m_lru_b_y, m_lru_w_in, m_lru_b_in, m_lru_conv_w, m_lru_conv_b, m_lru_w_a, m_lru_b_a, m_lru_w_x, m_lru_b_x, m_lru_lambda, m_lru_w_out, m_lru_b_out, m_pool_w, m_pool_scale, m_ffn_w1, m_ffn_w2, m_final_norm_g, v_w_mod, v_b_mod, v_norm_mix_g, v_norm_ffn_g, v_lru_w_y, v_lru_b_y, v_lru_w_in, v_lru_b_in, v_lru_conv_w, v_lru_conv_b, v_lru_w_a, v_lru_b_a, v_lru_w_x, v_lru_b_x, v_lru_lambda, v_lru_w_out, v_lru_b_out, v_pool_w, v_pool_scale, v_ffn_w1, v_ffn_w2, v_final_norm_g):
    me = 4 * lax.axis_index("x") + 2 * lax.axis_index("y") + lax.axis_index("c")
    n_lru = lru_w_y.shape[0]
    shard = LRU_WIDTH // N_DEV
    hshard = HEAD_DIM // N_DEV
    xs = x[0]
    target = loss_target[0]

    small_vecs = jnp.concatenate([
        lru_conv_w.reshape(n_lru * 4, shard), lru_b_a.reshape(n_lru, HEADS * hshard),
        lru_b_x.reshape(n_lru, HEADS * hshard), pool_scale, jnp.zeros((2, shard), f32)], axis=0)
    (first_cond, first_mix), _ = _send_start("gather_first_start", [[small_vecs, c], [
        jnp.stack([lru_w_y[0], lru_w_in[0], lru_w_out[0]]).astype(bf16),
        jnp.stack([lru_w_a[0], lru_w_x[0]]).astype(bf16)]], True, me)
    (sv_g, c_g), _ = _send_wait("gather_cond_wait", first_cond, small_vecs)
    conv_w_full = sv_g[:, 0:8].reshape(N_DEV, n_lru, 4, shard).transpose(1, 2, 0, 3).reshape(n_lru, 4, LRU_WIDTH)
    b_a_full = sv_g[:, 8:10].reshape(N_DEV, n_lru, HEADS, hshard).transpose(1, 2, 0, 3).reshape(n_lru, LRU_WIDTH)
    b_x_full = sv_g[:, 10:12].reshape(N_DEV, n_lru, HEADS, hshard).transpose(1, 2, 0, 3).reshape(n_lru, LRU_WIDTH)
    ps_full = sv_g[:, 12:14].transpose(1, 0, 2).reshape(n_lru, D_MODEL)
    c_all = c_g.reshape(N_DEV, D_MODEL)

    parts = []
    for i in range(DEPTH):
        j = i // 2
        if i > 0 and i % 2 == 0:
            parts.append([jnp.stack([lru_w_y[j], lru_w_in[j], lru_w_out[j]]).astype(bf16),
                          jnp.stack([lru_w_a[j], lru_w_x[j]]).astype(bf16)])
        elif i % 2 == 1:
            parts.append([pool_w[j].astype(bf16)])
        parts.append([ffn_w1[i].astype(bf16), ffn_w2[i].astype(bf16)])

    (mod_g,) = _exchange([_mod_part(c_all, w_mod)], True, "gather_mod", pieces=DEPTH,
                         after=[a for part in parts for a in part])
    mod_row = lax.dynamic_index_in_dim(mod_g, me, axis=2, keepdims=False)
    mod_row = mod_row.transpose(1, 0, 2).reshape(DEPTH, N_MOD * D_MODEL)
    table, token = _mod_table(mod_row, b_mod, norm_mix_g, norm_ffn_g)

    first_got, token = _send_wait("gather_mix_wait_0", first_mix, token)
    handles, token = _send_start("gather_rest_start", parts, True, me, after=token)
    h_ffn = [handles[0], handles[2], handles[4], handles[6]]
    h_mix = [None, handles[1], handles[3], handles[5]]

    zero_row = jnp.zeros((1, LRU_WIDTH), f32)
    pvecs = [jnp.concatenate([lru_b_y[j:j + 1], lru_b_in[j:j + 1], lru_conv_b[j:j + 1], b_a_full[j:j + 1],
                              b_x_full[j:j + 1], lru_lambda[j:j + 1], lru_b_out[j:j + 1], zero_row,
                              conv_w_full[j], zero_row, zero_row, zero_row, zero_row], axis=0) for j in range(n_lru)]
    ps_rows = [jnp.concatenate([ps_full[j:j + 1], jnp.zeros((7, D_MODEL), f32)], axis=0) for j in range(n_lru)]

    saved = []
    ffn_w, mix_w = [], []
    h = xs
    for i in range(DEPTH):
        j = i // 2
        got = first_got if i == 0 else _send_wait(f"gather_mix_wait_{i}", h_mix[i], h)[0]
        if i % 2 == 0:
            got = [got[0].reshape(N_DEV, 3, shard, LRU_WIDTH), got[1].reshape(N_DEV, 2, HEADS, hshard, HEAD_DIM)]
            mix_w.append((got[0].transpose(1, 0, 2, 3).reshape(3, LRU_WIDTH, LRU_WIDTH),
                          got[1].transpose(1, 2, 0, 3, 4).reshape(2, HEADS, HEAD_DIM, HEAD_DIM)))
            h_mid, *lru_saved = _lru_fwd(h, table, mix_w[i][0], mix_w[i][1], pvecs[j], i, dep=token)
            mix_saved = (h, tuple(lru_saved))
        else:
            mix_w.append((got[0].transpose(1, 0, 2, 3).reshape(HEADS, HEAD_DIM, HEAD_DIM),))
            h_mid, y_mix = _pool_fwd(h, table, mix_w[i][0], ps_rows[j], i)
            mix_saved = (h, y_mix)
        ffn_w.append(_send_wait(f"gather_ffn_wait_{i}", h_ffn[i], h_mid)[0])
        h_out, u, y_ffn, hb = _ffn_fwd(h_mid, table, ffn_w[i][0], ffn_w[i][1], i)
        saved.append((mix_saved, (h_mid, u, y_ffn, hb)))
        h = h_out
    fin_rows = jnp.concatenate([final_norm_g[None, :], jnp.zeros((7, D_MODEL), f32)], axis=0)
    dx, sm_fin = _final(h, target, fin_rows)

    sm_ffn, sm_mix = [None] * DEPTH, [None] * DEPTH
    x_ffn, x_mix = [None] * DEPTH, [None] * DEPTH
    token = jnp.zeros((8, 128), f32)
    last_mix = None
    for i in reversed(range(DEPTH)):
        j = i // 2
        mix_saved, (h_mid, u, y_ffn, hb) = saved[i]
        dx, da, dyb, sm_ffn[i] = _ffn_bwd_act(h_mid, dx, u, y_ffn, table, ffn_w[i][0], ffn_w[i][1], i, dep=token)
        ffn_grads = [_ffn_bwd_w1(hb, da, i), _ffn_bwd_w2(u, dyb, i)]
        if last_mix is None:
            (x_ffn[i],), token = _send_start(f"grads_start_{i}", [ffn_grads], False, me)
        else:
            (x_mix[i + 1], x_ffn[i]), token = _send_start(f"grads_start_{i}", [last_mix, ffn_grads], False, me)
        if i % 2 == 0:
            h_in, lru_saved = mix_saved
            dx, dbig, dsmall, sm_mix[i] = _lru_bwd(
                h_in, dx, lru_saved, table, mix_w[i][0], mix_w[i][1], pvecs[j], i, dep=token)
            last_mix = [dbig, dsmall]
        else:
            h_in, y_mix = mix_saved
            dx, dpool, sm = _pool_bwd(h_in, dx, y_mix, table, mix_w[i][0], ps_rows[j], i, dep=token)
            sm_mix[i] = jnp.concatenate([sm, jnp.zeros((8, D_MODEL), f32)], axis=0)
            last_mix = [dpool]
    grad_x = dx[None]

    pack = _small_pack(sm_ffn, sm_mix, sm_fin, table + token[0, 0], norm_mix_g, norm_ffn_g, lru_lambda)
    (h_small,), token = _send_start("gather_small_start", [[pack]], True, me)
    (x_mix[0],), token = _send_start("grads_last_start", [last_mix], False, me, after=token)
    l_ffn = [_send_wait(f"grads_ffn_wait_{i}", x_ffn[i], token)[0] for i in reversed(range(DEPTH))][::-1]

    def reduce_update(name, landings, kind, w, m, v):
        rows = w.size // w.shape[-1]
        two_d = (rows, w.shape[-1])
        lands = [t.reshape(N_DEV, -1, rows // len(landings), w.shape[-1]) for t in landings]
        outs = _adamw_reduce(name, lands, kind, w.reshape(two_d), m.reshape(two_d), v.reshape(two_d))
        return tuple(t.reshape(w.shape) for t in outs)

    results = {"ffn_w1": reduce_update("ffn_w1", [t[0] for t in l_ffn], 0, ffn_w1, m_ffn_w1, v_ffn_w1),
               "ffn_w2": reduce_update("ffn_w2", [t[1] for t in l_ffn], 0, ffn_w2, m_ffn_w2, v_ffn_w2)}

    (pack_g,), _ = _send_wait("gather_small_wait", h_small, results["ffn_w2"][1])
    tot, _ = _small_sum(pack_g)
    loss = tot[K_LOSS, 0]
    cols = w_mod.shape[2]
    dmod_all = lax.dynamic_slice_in_dim(pack_g[:, K_MOD:K_MOD + DEPTH * N_MOD].reshape(N_DEV, DEPTH, N_MOD * D_MODEL),
                                        me * cols, cols, axis=2).transpose(1, 0, 2)
    results["w_mod"] = _adamw_w_mod(c_all, dmod_all, w_mod, m_w_mod, v_w_mod)
    after = results["w_mod"][1]
    l_mix = [_send_wait(f"grads_mix_wait_{i}", x_mix[i], after)[0] for i in reversed(range(DEPTH))][::-1]
    l_lru_big = [l_mix[i][0] for i in range(0, DEPTH, 2)]
    l_lru_small = [l_mix[i][1] for i in range(0, DEPTH, 2)]
    l_pool = [l_mix[i][0] for i in range(1, DEPTH, 2)]
    results["lru_w_y"] = reduce_update("lru_w_y", l_lru_big, 0, lru_w_y, m_lru_w_y, v_lru_w_y)
    results["lru_w_in"] = reduce_update("lru_w_in", l_lru_big, 1, lru_w_in, m_lru_w_in, v_lru_w_in)
    results["lru_w_out"] = reduce_update("lru_w_out", l_lru_big, 2, lru_w_out, m_lru_w_out, v_lru_w_out)
    results["lru_w_a"] = reduce_update("lru_w_a", l_lru_small, 0, lru_w_a, m_lru_w_a, v_lru_w_a)
    results["lru_w_x"] = reduce_update("lru_w_x", l_lru_small, 1, lru_w_x, m_lru_w_x, v_lru_w_x)
    results["pool_w"] = reduce_update("pool_w", l_pool, 0, pool_w, m_pool_w, v_pool_w)

    def my_cols(full, width):
        return lax.dynamic_slice_in_dim(full, me * width, width, axis=full.ndim - 1)

    lru_rows = tot[K_LRUB:K_LRUB + 5 * n_lru].reshape(n_lru, 5, LRU_WIDTH)
    small_grads = {
        "b_mod": tot[K_MOD:K_MOD + DEPTH * N_MOD].reshape(DEPTH, N_MOD * D_MODEL),
        "norm_mix_g": tot[K_NMIX:K_NMIX + DEPTH],
        "norm_ffn_g": tot[K_NFFN:K_NFFN + DEPTH],
        "lru_b_y": lru_rows[:, 0], "lru_b_in": lru_rows[:, 1], "lru_conv_b": lru_rows[:, 2],
        "lru_lambda": lru_rows[:, 3], "lru_b_out": lru_rows[:, 4],
        "lru_conv_w": my_cols(tot[K_CONVW:K_CONVW + 4 * n_lru].reshape(n_lru, 4, LRU_WIDTH), shard),
        "lru_b_a": my_cols(tot[K_BA:K_BA + n_lru].reshape(n_lru, HEADS, HEAD_DIM), hshard),
        "lru_b_x": my_cols(tot[K_BX:K_BX + n_lru].reshape(n_lru, HEADS, HEAD_DIM), hshard),
        "pool_scale": my_cols(tot[K_PS:K_PS + n_lru], shard),
        "final_norm_g": tot[K_FIN],
    }
    given = dict(b_mod=(b_mod, m_b_mod, v_b_mod), norm_mix_g=(norm_mix_g, m_norm_mix_g, v_norm_mix_g),
                 norm_ffn_g=(norm_ffn_g, m_norm_ffn_g, v_norm_ffn_g), lru_b_y=(lru_b_y, m_lru_b_y, v_lru_b_y),
                 lru_b_in=(lru_b_in, m_lru_b_in, v_lru_b_in), lru_conv_w=(lru_conv_w, m_lru_conv_w, v_lru_conv_w),
                 lru_conv_b=(lru_conv_b, m_lru_conv_b, v_lru_conv_b), lru_b_a=(lru_b_a, m_lru_b_a, v_lru_b_a),
                 lru_b_x=(lru_b_x, m_lru_b_x, v_lru_b_x), lru_lambda=(lru_lambda, m_lru_lambda, v_lru_lambda),
                 lru_b_out=(lru_b_out, m_lru_b_out, v_lru_b_out), pool_scale=(pool_scale, m_pool_scale, v_pool_scale),
                 final_norm_g=(final_norm_g, m_final_norm_g, v_final_norm_g))
    updates = _adamw_small([(g,) + given[name] for name, g in small_grads.items()])
    for (name, g), update in zip(small_grads.items(), updates):
        results[name] = (g,) + update

    order = ["w_mod", "b_mod", "norm_mix_g", "norm_ffn_g", "lru_w_y", "lru_b_y", "lru_w_in", "lru_b_in", "lru_conv_w",
             "lru_conv_b", "lru_w_a", "lru_b_a", "lru_w_x", "lru_b_x", "lru_lambda", "lru_w_out", "lru_b_out", "pool_w",
             "pool_scale", "ffn_w1", "ffn_w2", "final_norm_g"]
    return (loss, grad_x, *[results[n][0] for n in order], *[results[n][1] for n in order],
            *[results[n][2] for n in order], *[results[n][3] for n in order])
```
